```python
import jax, jax.numpy as jnp
from jax import lax
import numpy as np

D_MODEL = 1024
BATCH = 8
SEQ = 8192
DEPTH = 1

N_META = 16
D_MIX = D_MODEL
GLA_HEADS = 4
GLA_DK = D_MIX // 4 // GLA_HEADS
GLA_DV = D_MIX // 2 // GLA_HEADS
GLA_RANK = 16
GLA_TAU = 16.0
GLA_CHUNK = 64
GLA_PAD = GLA_CHUNK - N_META
SWA_HEADS = 8
SWA_KV_HEADS = 2
SWA_GROUP = SWA_HEADS // SWA_KV_HEADS
SWA_HD = D_MIX // 2 // SWA_HEADS
SWA_WINDOW = 128
SWA_BLOCK = 128
ROPE_DIM = SWA_HD // 4
ROPE_THETA = 500000.0
D_FF = 4 * D_MODEL
EPS = 1e-5

IN_SIZES = (GLA_HEADS * GLA_DK,
            GLA_HEADS * GLA_DK,
            GLA_HEADS * GLA_DV,
            GLA_HEADS * GLA_DV,
            GLA_RANK,
            SWA_HEADS * SWA_HD,
            SWA_KV_HEADS * SWA_HD,
            SWA_KV_HEADS * SWA_HD)
D_IN = sum(IN_SIZES)

kernel_name = "hybrid_gla_swa_sink_meta_layer"


def rmsnorm(x, w):
    xf = x.astype(jnp.float32)
    y = xf * lax.rsqrt(jnp.mean(jnp.square(xf), axis=-1, keepdims=True) + EPS)
    return (y * w.astype(jnp.float32)).astype(x.dtype)


def partial_rope(x, pos):
    inv_freq = 1.0 / (ROPE_THETA ** (jnp.arange(0, ROPE_DIM, 2, dtype=jnp.float32) / ROPE_DIM))
    ang = pos.astype(jnp.float32)[:, None] * inv_freq[None, :]
    ang = jnp.concatenate([ang, ang], axis=-1)[:, None, :]
    cos, sin = jnp.cos(ang), jnp.sin(ang)
    xr = x[..., :ROPE_DIM].astype(jnp.float32)
    half = ROPE_DIM // 2
    rot = jnp.concatenate([-xr[..., half:], xr[..., :half]], axis=-1)
    xr = (xr * cos + rot * sin).astype(x.dtype)
    return jnp.concatenate([xr, x[..., ROPE_DIM:]], axis=-1)


def gla_chunk_step(S, inp):
    q, k, v, g = inp
    b = jnp.cumsum(g, axis=2)
    causal = jnp.tril(jnp.ones((GLA_CHUNK, GLA_CHUNK), dtype=bool))
    diff = b[:, :, :, None, :] - b[:, :, None, :, :]
    decay = jnp.exp(jnp.where(causal[None, None, :, :, None], diff, -jnp.inf))
    A = jnp.einsum('bhid,bhjd,bhijd->bhij', q, k, decay)
    o = (jnp.einsum('bhij,bhjv->bhiv', A, v)
         + jnp.einsum('bhid,bhdv->bhiv', q * jnp.exp(b), S))
    b_last = b[:, :, -1:, :]
    S = (jnp.exp(b_last[:, :, 0, :])[..., None] * S
         + jnp.einsum('bhjd,bhjv->bhdv', k * jnp.exp(b_last - b), v))
    return S, o


def gla_mixer(q, k, v, r, lr, w_gate_up, b_gate, gla_norm_w):
    B, L, _ = q.shape
    dtype = q.dtype
    g = jax.nn.log_sigmoid((lr @ w_gate_up + b_gate).astype(jnp.float32)) / GLA_TAU
    q = q.astype(jnp.float32).reshape(B, L, GLA_HEADS, GLA_DK) * (GLA_DK ** -0.5)
    k = k.astype(jnp.float32).reshape(B, L, GLA_HEADS, GLA_DK)
    v = v.astype(jnp.float32).reshape(B, L, GLA_HEADS, GLA_DV)
    g = g.reshape(B, L, GLA_HEADS, GLA_DK)
    pad = ((0, 0), (GLA_PAD, 0), (0, 0), (0, 0))
    Lp = L + GLA_PAD
    n_chunks = Lp // GLA_CHUNK

    def to_chunks(t):
        t = jnp.pad(t, pad).reshape(B, n_chunks, GLA_CHUNK, GLA_HEADS, t.shape[-1])
        return t.transpose(1, 0, 3, 2, 4)

    S0 = jnp.zeros((B, GLA_HEADS, GLA_DK, GLA_DV), jnp.float32)
    _, o = lax.scan(gla_chunk_step, S0, (to_chunks(q), to_chunks(k), to_chunks(v), to_chunks(g)))
    o = o.transpose(1, 0, 3, 2, 4).reshape(B, Lp, GLA_HEADS, GLA_DV)[:, GLA_PAD:]
    o = rmsnorm(o.astype(dtype), gla_norm_w)
    o = o * jax.nn.silu(r.reshape(B, L, GLA_HEADS, GLA_DV))
    return o.reshape(B, L, GLA_HEADS * GLA_DV)


def sink_softmax(scores, sink):
    sink_b = jnp.broadcast_to(sink, scores.shape[:-1] + (1,))
    p = jax.nn.softmax(jnp.concatenate([scores, sink_b], axis=-1), axis=-1)
    return p[..., :-1]


def swa_mixer(q, k, v, sinks, pos):
    B, L, _ = q.shape
    dtype = q.dtype
    q = partial_rope(q.reshape(B, L, SWA_HEADS, SWA_HD), pos) * (SWA_HD ** -0.5)
    k = partial_rope(k.reshape(B, L, SWA_KV_HEADS, SWA_HD), pos)
    v = v.reshape(B, L, SWA_KV_HEADS, SWA_HD)
    q = q.reshape(B, L, SWA_KV_HEADS, SWA_GROUP, SWA_HD)
    qm, qr = q[:, :N_META], q[:, N_META:]
    km, kr = k[:, :N_META], k[:, N_META:]
    vm, vr = v[:, :N_META], v[:, N_META:]
    sink = sinks.astype(jnp.float32).reshape(SWA_KV_HEADS, SWA_GROUP)

    sm = jnp.einsum('bqkgd,bjkd->bkgqj', qm, km).astype(jnp.float32)
    mmask = jnp.tril(jnp.ones((N_META, N_META), dtype=bool))
    pm = sink_softmax(jnp.where(mmask, sm, -jnp.inf), sink[None, :, :, None, None])
    om = jnp.einsum('bkgqj,bjkd->bqkgd', pm.astype(dtype), vm).reshape(B, N_META, SWA_HEADS * SWA_HD)

    S = L - N_META
    nb = S // SWA_BLOCK
    qb = qr.reshape(B, nb, SWA_BLOCK, SWA_KV_HEADS, SWA_GROUP, SWA_HD)

    def band(t):
        cur = t.reshape(B, nb, SWA_BLOCK, SWA_KV_HEADS, SWA_HD)
        prev = jnp.pad(t, ((0, 0), (SWA_BLOCK, 0), (0, 0), (0, 0)))[:, :S]
        prev = prev.reshape(B, nb, SWA_BLOCK, SWA_KV_HEADS, SWA_HD)
        return jnp.concatenate([prev, cur], axis=2)

    kw, vw = band(kr), band(vr)
    s_meta = jnp.einsum('bnqkgd,bjkd->bnkgqj', qb, km).astype(jnp.float32)
    s_win = jnp.einsum('bnqkgd,bnjkd->bnkgqj', qb, kw).astype(jnp.float32)
    rr = jnp.arange(SWA_BLOCK)[:, None]
    jj = jnp.arange(2 * SWA_BLOCK)[None, :]
    dist = SWA_BLOCK + rr - jj
    in_band = (dist >= 0) & (dist < SWA_WINDOW)
    blk = jnp.arange(nb)[:, None, None]
    wmask = in_band[None] & ((blk > 0) | (jj[None] >= SWA_BLOCK))
    s_win = jnp.where(wmask[None, :, None, None], s_win, -jnp.inf)
    p = sink_softmax(jnp.concatenate([s_meta, s_win], axis=-1),
                     sink[None, None, :, :, None, None]).astype(dtype)
    orr = (jnp.einsum('bnkgqj,bjkd->bnqkgd', p[..., :N_META], vm)
           + jnp.einsum('bnkgqj,bnjkd->bnqkgd', p[..., N_META:], vw))
    orr = orr.reshape(B, S, SWA_HEADS * SWA_HD)
    return jnp.concatenate([om, orr], axis=1)


def hybrid_layer(h, pos, norm_mix_w, w_in, w_gate_up, b_gate, gla_norm_w, sinks,
                 w_out, norm_ff_w, w_ff1, w_ff2):
    u = rmsnorm(h, norm_mix_w)
    proj = u @ w_in
    split_points = np.cumsum(IN_SIZES)[:-1].tolist()
    gq, gk, gv, gr, glr, sq, sk, sv = jnp.split(proj, split_points, axis=-1)
    o_gla = gla_mixer(gq, gk, gv, gr, glr, w_gate_up, b_gate, gla_norm_w)
    o_swa = swa_mixer(sq, sk, sv, sinks, pos)
    h = h + jnp.concatenate([o_gla, o_swa], axis=-1) @ w_out
    f = rmsnorm(h, norm_ff_w)
    return h + jnp.square(jax.nn.relu(f @ w_ff1)) @ w_ff2


def _fwd_setup_inputs(seed: int = 0) -> dict:
    key = jax.random.key(seed)
    ks = jax.random.split(key, 14)
    f32 = jnp.float32
    nrm = lambda k, shape, s: jax.random.normal(k, shape, f32) * s
    return {
        "x": nrm(ks[0], (BATCH, SEQ, D_MODEL), 1.0),
        "meta_tokens": nrm(ks[1], (N_META, D_MODEL), 1.0),
        "norm_mix_w": 1.0 + nrm(ks[2], (DEPTH, D_MODEL), 0.02),
        "w_in": nrm(ks[3], (DEPTH, D_MODEL, D_IN), D_MODEL ** -0.5),
        "w_gate_up": nrm(ks[4], (DEPTH, GLA_RANK, GLA_HEADS * GLA_DK), GLA_RANK ** -0.5),
        "b_gate": nrm(ks[5], (DEPTH, GLA_HEADS * GLA_DK), 0.1),
        "gla_norm_w": 1.0 + nrm(ks[6], (DEPTH, GLA_DV), 0.02),
        "sinks": nrm(ks[7], (DEPTH, SWA_HEADS), 1.0),
        "w_out": nrm(ks[8], (DEPTH, D_MIX, D_MODEL), D_MIX ** -0.5),
        "norm_ff_w": 1.0 + nrm(ks[9], (DEPTH, D_MODEL), 0.02),
        "w_ff1": nrm(ks[10], (DEPTH, D_MODEL, D_FF), D_MODEL ** -0.5),
        "w_ff2": nrm(ks[11], (DEPTH, D_FF, D_MODEL), D_FF ** -0.5),
        "final_norm_w": 1.0 + nrm(ks[12], (D_MODEL,), 0.02),
    }


def _fwd_reference(x, meta_tokens, norm_mix_w, w_in, w_gate_up, b_gate, gla_norm_w, sinks,
              w_out, norm_ff_w, w_ff1, w_ff2, final_norm_w):
    B = x.shape[0]
    meta = jnp.broadcast_to(meta_tokens[None].astype(x.dtype), (B, N_META, D_MODEL))
    h = jnp.concatenate([meta, x], axis=1)
    pos = jnp.arange(h.shape[1], dtype=jnp.int32)
    for layer in range(DEPTH):
        h = hybrid_layer(h, pos, norm_mix_w[layer], w_in[layer], w_gate_up[layer], b_gate[layer],
                         gla_norm_w[layer], sinks[layer], w_out[layer], norm_ff_w[layer],
                         w_ff1[layer], w_ff2[layer])
    return rmsnorm(h, final_norm_w)[:, N_META:]


import jax as _jax
import jax.numpy as _jnp

TWIN_FORMAT = 'train_step'
FWD_PARAMS = ['x', 'meta_tokens', 'norm_mix_w', 'w_in', 'w_gate_up', 'b_gate', 'gla_norm_w', 'sinks', 'w_out', 'norm_ff_w', 'w_ff1', 'w_ff2', 'final_norm_w']
TWIN_WEIGHTS = ['meta_tokens', 'norm_mix_w', 'w_in', 'w_gate_up', 'b_gate', 'gla_norm_w', 'sinks', 'w_out', 'norm_ff_w', 'w_ff1', 'w_ff2', 'final_norm_w']
TWIN_DIFF_INPUT = 'x'
TWIN_INPUTS = ['x', 'meta_tokens', 'norm_mix_w', 'w_in', 'w_gate_up', 'b_gate', 'gla_norm_w', 'sinks', 'w_out', 'norm_ff_w', 'w_ff1', 'w_ff2', 'final_norm_w', 'loss_target', 'm_meta_tokens', 'm_norm_mix_w', 'm_w_in', 'm_w_gate_up', 'm_b_gate', 'm_gla_norm_w', 'm_sinks', 'm_w_out', 'm_norm_ff_w', 'm_w_ff1', 'm_w_ff2', 'm_final_norm_w', 'v_meta_tokens', 'v_norm_mix_w', 'v_w_in', 'v_w_gate_up', 'v_b_gate', 'v_gla_norm_w', 'v_sinks', 'v_w_out', 'v_norm_ff_w', 'v_w_ff1', 'v_w_ff2', 'v_final_norm_w']
TWIN_OUTPUTS = ['loss', 'grad_x', 'grad_meta_tokens', 'grad_norm_mix_w', 'grad_w_in', 'grad_w_gate_up', 'grad_b_gate', 'grad_gla_norm_w', 'grad_sinks', 'grad_w_out', 'grad_norm_ff_w', 'grad_w_ff1', 'grad_w_ff2', 'grad_final_norm_w', 'delta_meta_tokens', 'delta_norm_mix_w', 'delta_w_in', 'delta_w_gate_up', 'delta_b_gate', 'delta_gla_norm_w', 'delta_sinks', 'delta_w_out', 'delta_norm_ff_w', 'delta_w_ff1', 'delta_w_ff2', 'delta_final_norm_w', 'new_m_meta_tokens', 'new_m_norm_mix_w', 'new_m_w_in', 'new_m_w_gate_up', 'new_m_b_gate', 'new_m_gla_norm_w', 'new_m_sinks', 'new_m_w_out', 'new_m_norm_ff_w', 'new_m_w_ff1', 'new_m_w_ff2', 'new_m_final_norm_w', 'new_v_meta_tokens', 'new_v_norm_mix_w', 'new_v_w_in', 'new_v_w_gate_up', 'new_v_b_gate', 'new_v_gla_norm_w', 'new_v_sinks', 'new_v_w_out', 'new_v_norm_ff_w', 'new_v_w_ff1', 'new_v_w_ff2', 'new_v_final_norm_w']
TWIN_LEAF_KINDS = {'loss': 'loss', 'grad_x': 'grad_x', 'grad_meta_tokens': 'grad_w', 'grad_norm_mix_w': 'grad_w', 'grad_w_in': 'grad_w', 'grad_w_gate_up': 'grad_w', 'grad_b_gate': 'grad_w', 'grad_gla_norm_w': 'grad_w', 'grad_sinks': 'grad_w', 'grad_w_out': 'grad_w', 'grad_norm_ff_w': 'grad_w', 'grad_w_ff1': 'grad_w', 'grad_w_ff2': 'grad_w', 'grad_final_norm_w': 'grad_w', 'delta_meta_tokens': 'delta_w', 'delta_norm_mix_w': 'delta_w', 'delta_w_in': 'delta_w', 'delta_w_gate_up': 'delta_w', 'delta_b_gate': 'delta_w', 'delta_gla_norm_w': 'delta_w', 'delta_sinks': 'delta_w', 'delta_w_out': 'delta_w', 'delta_norm_ff_w': 'delta_w', 'delta_w_ff1': 'delta_w', 'delta_w_ff2': 'delta_w', 'delta_final_norm_w': 'delta_w', 'new_m_meta_tokens': 'new_m', 'new_m_norm_mix_w': 'new_m', 'new_m_w_in': 'new_m', 'new_m_w_gate_up': 'new_m', 'new_m_b_gate': 'new_m', 'new_m_gla_norm_w': 'new_m', 'new_m_sinks': 'new_m', 'new_m_w_out': 'new_m', 'new_m_norm_ff_w': 'new_m', 'new_m_w_ff1': 'new_m', 'new_m_w_ff2': 'new_m', 'new_m_final_norm_w': 'new_m', 'new_v_meta_tokens': 'new_v', 'new_v_norm_mix_w': 'new_v', 'new_v_w_in': 'new_v', 'new_v_w_gate_up': 'new_v', 'new_v_b_gate': 'new_v', 'new_v_gla_norm_w': 'new_v', 'new_v_sinks': 'new_v', 'new_v_w_out': 'new_v', 'new_v_norm_ff_w': 'new_v', 'new_v_w_ff1': 'new_v', 'new_v_w_ff2': 'new_v', 'new_v_final_norm_w': 'new_v'}


def _forward(args):
    return _fwd_reference(*[args[k] for k in FWD_PARAMS])


def _output_shape():
    def fwd():
        inp = _fwd_setup_inputs(0)
        return _fwd_reference(*[inp[k] for k in FWD_PARAMS])
    out = _jax.eval_shape(fwd)
    return out.shape, out.dtype

N_MICROBATCH = 1
ADAM_LR = 0.001
ADAM_B1 = 0.9
ADAM_B2 = 0.999
ADAM_EPS = 1e-08
ADAM_WD = 0.01
ADAM_STEP = 10
PER_EXAMPLE_BATCH_AXIS = {'x': 0, 'loss_target': 0}
SHARED_INPUTS = []
_WEIGHT_DTYPES = {'meta_tokens': _jnp.float32, 'norm_mix_w': _jnp.float32, 'w_in': _jnp.float32, 'w_gate_up': _jnp.float32, 'b_gate': _jnp.float32, 'gla_norm_w': _jnp.float32, 'sinks': _jnp.float32, 'w_out': _jnp.float32, 'norm_ff_w': _jnp.float32, 'w_ff1': _jnp.float32, 'w_ff2': _jnp.float32, 'final_norm_w': _jnp.float32}
MOMENT_SCALE = {'meta_tokens': 8.977068e-03, 'norm_mix_w': 2.327164e-01, 'w_in': 1.480693e-01, 'w_gate_up': 2.466490e-02, 'b_gate': 1.008869e-01, 'gla_norm_w': 2.923689e-01, 'sinks': 5.625274e-03, 'w_out': 1.099082e-01, 'norm_ff_w': 2.350713e-01, 'w_ff1': 1.088209e-01, 'w_ff2': 2.286894e-01, 'final_norm_w': 6.465999e+01}


def _to_microbatches(a, axis):
    t = _jnp.moveaxis(a, axis, 0)
    t = t.reshape((N_MICROBATCH, t.shape[0] // N_MICROBATCH) + t.shape[1:])
    return _jnp.moveaxis(t, 1, axis + 1)


def setup_inputs(seed: int = 0) -> dict:
    inp = _fwd_setup_inputs(seed)
    key = _jax.random.fold_in(_jax.random.key(seed), 7919)
    shape, _ = _output_shape()
    out = dict(inp)
    out["loss_target"] = _jax.random.normal(_jax.random.fold_in(key, 0), shape, _jnp.float32)
    for i, name in enumerate(TWIN_WEIGHTS):
        w = inp[name].astype(_jnp.float32)
        if MOMENT_SCALE is None:
            s = _jnp.sqrt(_jnp.mean(_jnp.square(w)) + 1e-30)
        else:
            s = MOMENT_SCALE[name]
        km, kv = _jax.random.split(_jax.random.fold_in(key, i + 1))
        out[name] = w
        out["m_" + name] = s * _jax.random.normal(km, w.shape, _jnp.float32)
        out["v_" + name] = (s * s) * _jax.random.uniform(kv, w.shape, _jnp.float32, 0.5, 1.5)
    if N_MICROBATCH > 1:
        for name, axis in PER_EXAMPLE_BATCH_AXIS.items():
            out[name] = _to_microbatches(out[name], axis)
    return {'x': out['x'], 'meta_tokens': out['meta_tokens'], 'norm_mix_w': out['norm_mix_w'], 'w_in': out['w_in'], 'w_gate_up': out['w_gate_up'], 'b_gate': out['b_gate'], 'gla_norm_w': out['gla_norm_w'], 'sinks': out['sinks'], 'w_out': out['w_out'], 'norm_ff_w': out['norm_ff_w'], 'w_ff1': out['w_ff1'], 'w_ff2': out['w_ff2'], 'final_norm_w': out['final_norm_w'], 'loss_target': out['loss_target'], 'm_meta_tokens': out['m_meta_tokens'], 'm_norm_mix_w': out['m_norm_mix_w'], 'm_w_in': out['m_w_in'], 'm_w_gate_up': out['m_w_gate_up'], 'm_b_gate': out['m_b_gate'], 'm_gla_norm_w': out['m_gla_norm_w'], 'm_sinks': out['m_sinks'], 'm_w_out': out['m_w_out'], 'm_norm_ff_w': out['m_norm_ff_w'], 'm_w_ff1': out['m_w_ff1'], 'm_w_ff2': out['m_w_ff2'], 'm_final_norm_w': out['m_final_norm_w'], 'v_meta_tokens': out['v_meta_tokens'], 'v_norm_mix_w': out['v_norm_mix_w'], 'v_w_in': out['v_w_in'], 'v_w_gate_up': out['v_w_gate_up'], 'v_b_gate': out['v_b_gate'], 'v_gla_norm_w': out['v_gla_norm_w'], 'v_sinks': out['v_sinks'], 'v_w_out': out['v_w_out'], 'v_norm_ff_w': out['v_norm_ff_w'], 'v_w_ff1': out['v_w_ff1'], 'v_w_ff2': out['v_w_ff2'], 'v_final_norm_w': out['v_final_norm_w']}


def _loss(weights, diff, rest, loss_target):
    with _jax.named_scope("forward"):
        args = {**rest, TWIN_DIFF_INPUT: diff, **{k: w.astype(_WEIGHT_DTYPES[k]) for k, w in weights.items()}}
        y = _forward(args)
    with _jax.named_scope("loss_head"):
        err = _jnp.square(y.astype(_jnp.float32) - loss_target)
        return 0.5 * _jnp.sum(_jnp.mean(err, axis=-1)) if err.ndim else 0.5 * err


def _adamw(w, g, m, v):
    m = ADAM_B1 * m + (1.0 - ADAM_B1) * g
    v = ADAM_B2 * v + (1.0 - ADAM_B2) * _jnp.square(g)
    m_hat = m / (1.0 - ADAM_B1 ** ADAM_STEP)
    v_hat = v / (1.0 - ADAM_B2 ** ADAM_STEP)
    delta = -ADAM_LR * (m_hat / (_jnp.sqrt(v_hat) + ADAM_EPS) + ADAM_WD * w)
    return delta, m, v


def reference(x, meta_tokens, norm_mix_w, w_in, w_gate_up, b_gate, gla_norm_w, sinks, w_out, norm_ff_w, w_ff1, w_ff2, final_norm_w, loss_target, m_meta_tokens, m_norm_mix_w, m_w_in, m_w_gate_up, m_b_gate, m_gla_norm_w, m_sinks, m_w_out, m_norm_ff_w, m_w_ff1, m_w_ff2, m_final_norm_w, v_meta_tokens, v_norm_mix_w, v_w_in, v_w_gate_up, v_b_gate, v_gla_norm_w, v_sinks, v_w_out, v_norm_ff_w, v_w_ff1, v_w_ff2, v_final_norm_w):
    given = dict(x=x, meta_tokens=meta_tokens, norm_mix_w=norm_mix_w, w_in=w_in, w_gate_up=w_gate_up, b_gate=b_gate, gla_norm_w=gla_norm_w, sinks=sinks, w_out=w_out, norm_ff_w=norm_ff_w, w_ff1=w_ff1, w_ff2=w_ff2, final_norm_w=final_norm_w, loss_target=loss_target, m_meta_tokens=m_meta_tokens, m_norm_mix_w=m_norm_mix_w, m_w_in=m_w_in, m_w_gate_up=m_w_gate_up, m_b_gate=m_b_gate, m_gla_norm_w=m_gla_norm_w, m_sinks=m_sinks, m_w_out=m_w_out, m_norm_ff_w=m_norm_ff_w, m_w_ff1=m_w_ff1, m_w_ff2=m_w_ff2, m_final_norm_w=m_final_norm_w, v_meta_tokens=v_meta_tokens, v_norm_mix_w=v_norm_mix_w, v_w_in=v_w_in, v_w_gate_up=v_w_gate_up, v_b_gate=v_b_gate, v_gla_norm_w=v_gla_norm_w, v_sinks=v_sinks, v_w_out=v_w_out, v_norm_ff_w=v_norm_ff_w, v_w_ff1=v_w_ff1, v_w_ff2=v_w_ff2, v_final_norm_w=v_final_norm_w)
    weights = {n: given[n] for n in TWIN_WEIGHTS}
    shared = {n: given[n] for n in SHARED_INPUTS}
    per_example = {n: given[n] for n in ['x']}
    grad_fn = _jax.value_and_grad(_loss, argnums=(0, 1))

    def one_microbatch(ex, loss_target):
        ex = dict(ex)
        diff = ex.pop(TWIN_DIFF_INPUT)
        return grad_fn(weights, diff, {**shared, **ex}, loss_target)

    if N_MICROBATCH == 1:
        loss, (grad_w, grad_x) = one_microbatch(per_example, given["loss_target"])
    else:
        def body(carry, xs):
            loss_sum, grad_sum = carry
            l_k, (gw_k, gx_k) = one_microbatch(xs[0], xs[1])
            with _jax.named_scope("update"):
                return (loss_sum + l_k, _jax.tree.map(_jnp.add, grad_sum, gw_k)), gx_k

        init = (_jnp.zeros((), _jnp.float32), _jax.tree.map(_jnp.zeros_like, weights))
        (loss, grad_w), grad_x = _jax.lax.scan(body, init, (per_example, given["loss_target"]))
    with _jax.named_scope("update"):
        delta_w, new_m, new_v = {}, {}, {}
        for n in TWIN_WEIGHTS:
            delta_w[n], new_m[n], new_v[n] = _adamw(weights[n], grad_w[n], given["m_" + n], given["v_" + n])
    return (loss, grad_x, *[grad_w[n] for n in TWIN_WEIGHTS], *[delta_w[n] for n in TWIN_WEIGHTS],
            *[new_m[n] for n in TWIN_WEIGHTS], *[new_v[n] for n in TWIN_WEIGHTS])
```

```python
import functools

import jax
import jax.numpy as jnp
from jax import lax
from jax.experimental import pallas as pl
from jax.experimental.pallas import tpu as pltpu

F32 = jnp.float32
BF16 = jnp.bfloat16

D = 1024
DFF = 4096
NM = 16
TM = 256
CH = 64
SB = 128
EPS = 1e-5
C_GQ, C_GK, C_GV, C_GR, C_SQ, C_SK, C_SV, C_LR, DINP = 0, 256, 512, 1024, 1536, 2048, 2176, 2304, 2432
DIN = 2320
ROPE_THETA = 500000.0
ADAM_LR, ADAM_B1, ADAM_B2, ADAM_EPS, ADAM_WD, ADAM_STEP = 0.001, 0.9, 0.999, 1e-08, 0.01, 10
NEG = -1e30
MESH = pl.DeviceIdType.MESH
VMEM_SPEC = pl.BlockSpec(memory_space=pltpu.VMEM)
ANY_SPEC = pl.BlockSpec(memory_space=pl.ANY)
SMEM_SPEC = pl.BlockSpec(memory_space=pltpu.SMEM)


def _cp(vmem_mb, sem=("arbitrary",)):
    return pltpu.CompilerParams(dimension_semantics=sem, vmem_limit_bytes=vmem_mb << 20)


def _dot(a, b):
    return jnp.dot(a, b, preferred_element_type=F32)


def _dot_nt(a, b):
    return lax.dot_general(a, b, (((1,), (1,)), ((), ())), preferred_element_type=F32)


def _dot_tn(a, b):
    return lax.dot_general(a, b, (((0,), (0,)), ((), ())), preferred_element_type=F32)


def _bf(x):
    return x.astype(BF16)


def _dot3(m01, x):
    x1 = _bf(x)
    r1 = x - x1.astype(F32)
    x2 = _bf(r1)
    x3 = _bf(r1 - x2.astype(F32))
    return _dot(m01, x1) + _dot(m01, x2) + _dot(m01, x3)


def _rms(h):
    rs = lax.rsqrt(jnp.mean(h * h, axis=-1, keepdims=True) + EPS)
    return h * rs, rs


def _rms_bwd(dy, yhat, rs, w):
    dyh = dy * w
    return rs * (dyh - yhat * jnp.mean(dyh * yhat, axis=-1, keepdims=True))


def _proj_fwd(x, metapad, wm, winp):
    t = x.shape[0]
    nblk = t // TM

    def body(x_ref, mp_ref, wm_ref, w_ref, proj_ref):
        i = pl.program_id(0)
        h = jnp.where(i == nblk, mp_ref[...], x_ref[...])
        u, _ = _rms(h)
        proj_ref[...] = _dot(_bf(u * wm_ref[...]), w_ref[...])

    return pl.pallas_call(
        body, name="proj_fwd", grid=(nblk + 1,),
        in_specs=[pl.BlockSpec((TM, D), lambda i: (jnp.minimum(i, nblk - 1), 0)), VMEM_SPEC, VMEM_SPEC, VMEM_SPEC],
        out_specs=pl.BlockSpec((TM, DINP), lambda i: (i, 0)),
        out_shape=jax.ShapeDtypeStruct((t + TM, DINP), F32),
        compiler_params=_cp(40),
    )(x, metapad, wm, winp)


def _chunk_masks():
    r = lax.broadcasted_iota(jnp.int32, (TM, TM), 0)
    c = lax.broadcasted_iota(jnp.int32, (TM, TM), 1)
    same = (r // CH) == (c // CH)
    lower = _bf(jnp.where(same & (c <= r), 1.0, 0.0))
    upper = _bf(jnp.where(same & (c >= r), 1.0, 0.0))
    return lower, upper


def _gla_gate(lr, wgu, bg, valid, lower):
    z = _dot(_bf(lr), wgu) + bg
    g = (jnp.minimum(z, 0.0) - jnp.log(1.0 + jnp.exp(-jnp.abs(z)))) * (1.0 / 16.0)
    g = jnp.where(valid, g, 0.0)
    return z, _dot3(lower, g)


def _gla_decays(q, k, b):
    nc = TM // CH
    b3 = b.reshape(nc, CH, 256)
    blast = b3[:, CH - 1:CH, :]
    eb = jnp.exp(b)
    enb = jnp.exp(-b)
    ebl = jnp.exp(blast - b3).reshape(TM, 256)
    return eb, enb, ebl, jnp.exp(blast)


def _tri(lower_incl):
    r = lax.broadcasted_iota(jnp.int32, (CH, CH), 0)
    c = lax.broadcasted_iota(jnp.int32, (CH, CH), 1)
    return ((c <= r) if lower_incl else (c >= r))[None]


def _gla_fwd(proj, wgu, bg, gnw, t):
    nblk = t // TM
    nt = nblk + 1
    nc = TM // CH

    def blk(i):
        return (i + nblk) % nt

    def body(q_ref, k_ref, v_ref, r_ref, lr_ref, wgu_ref, bg_ref, gnw_ref, o_ref, oraw_ref, sst_ref, st_scr):
        i = pl.program_id(0)

        @pl.when(i == 0)
        def _():
            st_scr[...] = jnp.zeros_like(st_scr)

        rows = blk(i) * TM + lax.broadcasted_iota(jnp.int32, (TM, 1), 0)
        lower, _ = _chunk_masks()
        _, b = _gla_gate(lr_ref[...], wgu_ref[...], bg_ref[...], rows < t + NM, lower)
        q = q_ref[...]
        k = k_ref[...]
        eb, enb, ebl, eblast = _gla_decays(q, k, b)
        qt = q * 0.125 * eb
        kt = k * enb
        kh = k * ebl
        tril = _tri(True)
        outs = []
        for h in range(4):
            hs = slice(h * CH, (h + 1) * CH)
            qh = _bf(qt[:, hs]).reshape(nc, CH, CH)
            kth = _bf(kt[:, hs]).reshape(nc, CH, CH)
            khh = _bf(kh[:, hs]).reshape(nc, CH, CH)
            vh = _bf(v_ref[:, h * 128:(h + 1) * 128]).reshape(nc, CH, 128)
            a = jnp.einsum('cid,cjd->cij', qh, kth, preferred_element_type=F32)
            a = jnp.where(tril, a, 0.0)
            o = jnp.einsum('cij,cjv->civ', _bf(a), vh, preferred_element_type=F32)
            kv = jnp.einsum('cjv,cjd->cvd', vh, khh, preferred_element_type=F32)
            st = st_scr[h]
            o_inter = []
            for c in range(nc):
                sst_ref[c, h] = st
                o_inter.append(_dot_nt(qh[c], _bf(st)))
                st = st * eblast[c, :, hs] + kv[c]
            st_scr[h] = st
            outs.append((o + jnp.stack(o_inter)).reshape(TM, 128))
        oraw = jnp.concatenate(outs, axis=1)
        oraw_ref[...] = oraw
        gn = gnw_ref[...]
        res = []
        for h in range(4):
            on, _ = _rms(oraw[:, h * 128:(h + 1) * 128])
            r = r_ref[:, h * 128:(h + 1) * 128]
            res.append(on * gn * (r * jax.nn.sigmoid(r)))
        o_ref[...] = _bf(jnp.concatenate(res, axis=1))

    def spec(w, cb):
        return pl.BlockSpec((TM, w), lambda i: (blk(i), cb))

    return pl.pallas_call(
        body, name="gla_fwd", grid=(nt,),
        in_specs=[spec(256, 0), spec(256, 1), spec(512, 1), spec(512, 2), spec(128, C_LR // 128), VMEM_SPEC, VMEM_SPEC, VMEM_SPEC],
        out_specs=[spec(512, 0), spec(512, 0), pl.BlockSpec((nc, 4, 128, CH), lambda i: (blk(i), 0, 0, 0))],
        out_shape=[jax.ShapeDtypeStruct((t + TM, 512), BF16), jax.ShapeDtypeStruct((t + TM, 512), F32),
                   jax.ShapeDtypeStruct((nt * nc, 4, 128, CH), F32)],
        scratch_shapes=[pltpu.VMEM((4, 128, CH), F32)],
        compiler_params=_cp(40),
    )(proj, proj, proj, proj, proj, wgu, bg, gnw)


def _gla_bwd(proj, oraw, sst, do, wgu, bg, gnw, t):
    nblk = t // TM
    nt = nblk + 1
    nc = TM // CH

    def blk(i):
        return (2 * nblk - i) % nt

    def body(q_ref, k_ref, v_ref, r_ref, lr_ref, oraw_ref, sst_ref, do_ref, wgu_ref, bg_ref, gnw_ref,
             dgla_ref, dlr_ref, dwgu_ref, dbg_ref, dgnw_ref, dst_scr):
        i = pl.program_id(0)

        @pl.when(i == 0)
        def _():
            dst_scr[...] = jnp.zeros_like(dst_scr)
            dwgu_ref[...] = jnp.zeros_like(dwgu_ref)
            dbg_ref[...] = jnp.zeros_like(dbg_ref)
            dgnw_ref[...] = jnp.zeros_like(dgnw_ref)

        rows = blk(i) * TM + lax.broadcasted_iota(jnp.int32, (TM, 1), 0)
        valid = rows < t + NM
        lower, upper = _chunk_masks()
        lr = lr_ref[...]
        z, b = _gla_gate(lr, wgu_ref[...], bg_ref[...], valid, lower)
        q = q_ref[...]
        k = k_ref[...]
        eb, enb, ebl, eblast = _gla_decays(q, k, b)
        qt = q * 0.125 * eb
        kt = k * enb
        kh = k * ebl
        gn = gnw_ref[...]
        tril = _tri(True)
        triu = _tri(False)
        dq_l, dk_l, dv_l, dr_l, db_l, ex_l = [], [], [], [], [], []
        dgn = jnp.zeros((1, 128), F32)
        for h in range(4):
            hs = slice(h * CH, (h + 1) * CH)
            vs = slice(h * 128, (h + 1) * 128)
            on, rs = _rms(oraw_ref[:, vs])
            r = r_ref[:, vs]
            sig = jax.nn.sigmoid(r)
            sil = r * sig
            dy = do_ref[:, vs]
            dr_l.append(dy * on * gn * (sig * (1.0 + r * (1.0 - sig))))
            dgn = dgn + jnp.sum(dy * sil * on, axis=0, keepdims=True)
            doraw = _rms_bwd(dy * sil, on, rs, gn)
            qtf = qt[:, hs].reshape(nc, CH, CH)
            ktf = kt[:, hs].reshape(nc, CH, CH)
            khf = kh[:, hs].reshape(nc, CH, CH)
            qh, kth, khh = _bf(qtf), _bf(ktf), _bf(khf)
            vh = _bf(v_ref[:, vs]).reshape(nc, CH, 128)
            doh = _bf(doraw).reshape(nc, CH, 128)
            at = jnp.where(triu, jnp.einsum('cjd,cid->cji', kth, qh, preferred_element_type=F32), 0.0)
            da = jnp.where(tril, jnp.einsum('civ,cjv->cij', doh, vh, preferred_element_type=F32), 0.0)
            dat = jnp.where(triu, jnp.einsum('cjv,civ->cji', vh, doh, preferred_element_type=F32), 0.0)
            dv = jnp.einsum('cji,civ->cjv', _bf(at), doh, preferred_element_type=F32)
            dqt = jnp.einsum('cij,cjd->cid', _bf(da), kth, preferred_element_type=F32)
            dkt = jnp.einsum('cji,cid->cjd', _bf(dat), qh, preferred_element_type=F32)
            gq = jnp.einsum('civ,cid->cvd', doh, qh, preferred_element_type=F32)
            dst = dst_scr[h]
            dsend = [None] * nc
            for c in reversed(range(nc)):
                dsend[c] = dst
                dst = dst * eblast[c, :, hs] + gq[c]
            dst_scr[h] = dst
            dse = jnp.stack(dsend)
            dseb = _bf(dse)
            stf = sst_ref[:, h]
            dqt = dqt + jnp.einsum('civ,cvd->cid', doh, _bf(stf), preferred_element_type=F32)
            dv = dv + jnp.einsum('cjd,cvd->cjv', khh, dseb, preferred_element_type=F32)
            dkh = jnp.einsum('cjv,cvd->cjd', vh, dseb, preferred_element_type=F32)
            extra = (jnp.sum(dkh * khf, axis=1, keepdims=True)
                     + eblast[:, :, hs] * jnp.sum(dse * stf, axis=1, keepdims=True))
            db_l.append((dqt * qtf - dkt * ktf - dkh * khf).reshape(TM, CH))
            ex_l.append(jnp.broadcast_to(extra, (nc, CH, CH)).reshape(TM, CH))
            dq_l.append((dqt.reshape(TM, CH)) * eb[:, hs] * 0.125)
            dk_l.append(dkt.reshape(TM, CH) * enb[:, hs] + dkh.reshape(TM, CH) * ebl[:, hs])
            dv_l.append(dv.reshape(TM, 128))
        dgnw_ref[...] += dgn
        db = jnp.concatenate(db_l, axis=1)
        dg = _dot3(upper, db) + jnp.concatenate(ex_l, axis=1)
        dz = jnp.where(valid, dg * (1.0 / 16.0) / (1.0 + jnp.exp(z)), 0.0)
        dzb = _bf(dz)
        dlr_ref[...] = _bf(_dot_nt(dzb, wgu_ref[...]))
        dwgu_ref[...] += _dot_tn(_bf(lr), dzb)
        dbg_ref[...] += jnp.sum(dz, axis=0, keepdims=True)
        dgla_ref[...] = _bf(jnp.concatenate(dq_l + dk_l + dv_l + dr_l, axis=1))

    def spec(w, cb):
        return pl.BlockSpec((TM, w), lambda i: (blk(i), cb))

    def acc(shape):
        return pl.BlockSpec(shape, lambda i: (0, 0))

    return pl.pallas_call(
        body, name="gla_bwd", grid=(nt,),
        in_specs=[spec(256, 0), spec(256, 1), spec(512, 1), spec(512, 2), spec(128, C_LR // 128), spec(512, 0),
                  pl.BlockSpec((nc, 4, 128, CH), lambda i: (blk(i), 0, 0, 0)), spec(512, 0), VMEM_SPEC, VMEM_SPEC, VMEM_SPEC],
        out_specs=[spec(1536, 0), spec(128, 0), acc((128, 256)), acc((1, 256)), acc((1, 128))],
        out_shape=[jax.ShapeDtypeStruct((t + TM, 1536), BF16), jax.ShapeDtypeStruct((t + TM, 128), BF16),
                   jax.ShapeDtypeStruct((128, 256), F32), jax.ShapeDtypeStruct((1, 256), F32),
                   jax.ShapeDtypeStruct((1, 128), F32)],
        scratch_shapes=[pltpu.VMEM((4, 128, CH), F32)],
        compiler_params=_cp(48),
    )(proj, proj, proj, proj, proj, oraw, sst, do, wgu, bg, gnw)


def _rope_tables(t):
    r = t + TM
    row = jnp.arange(r, dtype=jnp.int32)
    pos = jnp.where(row < t, row + NM, jnp.where(row < t + NM, row - t, 0))
    inv_freq = 1.0 / (ROPE_THETA ** (jnp.arange(0, 16, 2, dtype=F32) / 16))
    ang = pos.astype(F32)[:, None] * inv_freq[None, :]
    cos, sin = jnp.cos(ang), jnp.sin(ang)
    one, zero = jnp.ones((r, 48), F32), jnp.zeros((r, 48), F32)
    z8 = jnp.zeros((r, 8), F32)
    c = jnp.concatenate([cos, cos, one], axis=1)
    s1 = jnp.concatenate([-sin, z8, zero], axis=1)
    s2 = jnp.concatenate([z8, sin, zero], axis=1)
    return jnp.concatenate([c, c, s1, s1, s2, s2], axis=1)


def _rope(x, tab, sign):
    w = x.shape[1]
    rep = w // 128
    c, s1, s2 = (jnp.tile(tab[:, j * 128:(j + 1) * 128], (1, rep)) if rep > 1 else tab[:, j * 128:(j + 1) * 128]
                 for j in range(3))
    return x * c + sign * (pltpu.roll(x, w - 8, 1) * s1 + pltpu.roll(x, 8, 1) * s2)


def _stack(x):
    return jnp.concatenate([x[:, g * 64:(g + 1) * 64] for g in range(4)], axis=0)


def _unstack(x):
    return jnp.concatenate([x[g * SB:(g + 1) * SB] for g in range(4)], axis=1)


def _swa_masks(i, nsb):
    r = lax.rem(lax.broadcasted_iota(jnp.int32, (4 * SB, SB), 0), SB)
    c = lax.broadcasted_iota(jnp.int32, (4 * SB, SB), 1)
    real = i < nsb
    return c <= r, (c > r) & (i > 0) & real, real


def _swa_specs(nsb):
    def rows(w, cb, f):
        return pl.BlockSpec((SB, w), lambda i: (f(i), cb))
    cur = lambda i: i
    prev = lambda i: jnp.maximum(i - 1, 0)
    meta = lambda i: nsb
    return rows, cur, prev, meta


def _swa_scores(i, nsb, sink_ref, q_ref, kc_ref, kp_ref, km_ref, tc_ref, tp_ref, tm_ref):
    mc, mp, real = _swa_masks(i, nsb)
    qr = _rope(q_ref[...], tc_ref[...], 1.0) * 0.125
    kc = _rope(kc_ref[...], tc_ref[...], 1.0)
    kp = _rope(kp_ref[...], tp_ref[...], 1.0)
    km = _rope(km_ref[...], tm_ref[...], 1.0)[:NM]
    per_kv = []
    for kv in range(2):
        ks = slice(kv * 64, (kv + 1) * 64)
        qg = _bf(_stack(qr[:, kv * 256:(kv + 1) * 256]))
        kcb, kpb, kmb = _bf(kc[:, ks]), _bf(kp[:, ks]), _bf(km[:, ks])
        s_c = jnp.where(mc, _dot_nt(qg, kcb), NEG)
        s_p = jnp.where(mp, _dot_nt(qg, kpb), NEG)
        s_m = jnp.where(real, _dot_nt(qg, kmb), NEG)
        sink = jnp.concatenate([jnp.full((SB, 1), sink_ref[0, kv * 4 + g], F32) for g in range(4)], axis=0)
        per_kv.append((qg, kcb, kpb, kmb, s_c, s_p, s_m, sink))
    return per_kv


def _swa_fwd(proj, tabs, sinks, t):
    nsb = t // SB
    ns = nsb + 2
    rows, cur, prev, meta = _swa_specs(nsb)

    def body(sink_ref, q_ref, kc_ref, kp_ref, km_ref, vc_ref, vp_ref, vm_ref, tc_ref, tp_ref, tm_ref, o_ref, lse_ref):
        i = pl.program_id(0)
        per_kv = _swa_scores(i, nsb, sink_ref, q_ref, kc_ref, kp_ref, km_ref, tc_ref, tp_ref, tm_ref)
        valid = i * SB + lax.broadcasted_iota(jnp.int32, (SB, 1), 0) < t + NM
        o_l, lse_l = [], []
        for kv, (qg, kcb, kpb, kmb, s_c, s_p, s_m, sink) in enumerate(per_kv):
            ks = slice(kv * 64, (kv + 1) * 64)
            m = jnp.maximum(jnp.maximum(jnp.max(s_c, -1, keepdims=True), jnp.max(s_p, -1, keepdims=True)),
                            jnp.maximum(jnp.max(s_m, -1, keepdims=True), sink))
            p_c, p_p, p_m = jnp.exp(s_c - m), jnp.exp(s_p - m), jnp.exp(s_m - m)
            l = (jnp.sum(p_c, -1, keepdims=True) + jnp.sum(p_p, -1, keepdims=True)
                 + jnp.sum(p_m, -1, keepdims=True) + jnp.exp(sink - m))
            inv = 1.0 / l
            o = (_dot(_bf(p_c * inv), _bf(vc_ref[:, ks])) + _dot(_bf(p_p * inv), _bf(vp_ref[:, ks]))
                 + _dot(_bf(p_m * inv), _bf(vm_ref[:NM, ks])))
            o_l.append(_unstack(o))
            lse_l.append(_unstack(m + jnp.log(l)))
        o_ref[...] = _bf(jnp.where(valid, jnp.concatenate(o_l, axis=1), 0.0))
        lse_ref[...] = jnp.concatenate(lse_l, axis=1)

    ck, cv = C_SK // 128, C_SV // 128
    return pl.pallas_call(
        body, name="swa_fwd", grid=(ns,),
        in_specs=[SMEM_SPEC, rows(512, C_SQ // 512, cur),
                  rows(128, ck, cur), rows(128, ck, prev), rows(128, ck, meta),
                  rows(128, cv, cur), rows(128, cv, prev), rows(128, cv, meta),
                  rows(384, 0, cur), rows(384, 0, prev), rows(384, 0, meta)],
        out_specs=[rows(512, 0, cur), rows(8, 0, cur)],
        out_shape=[jax.ShapeDtypeStruct((t + TM, 512), BF16), jax.ShapeDtypeStruct((t + TM, 8), F32)],
        compiler_params=_cp(32),
    )(sinks, proj, proj, proj, proj, proj, proj, proj, tabs, tabs, tabs)


def _swa_bwd(proj, tabs, sinks, lse, do, t):
    nsb = t // SB
    ns = nsb + 2
    r_tot = t + TM
    rows, cur, prev, meta = _swa_specs(nsb)

    def body(sink_ref, q_ref, kc_ref, kp_ref, km_ref, vc_ref, vp_ref, vm_ref, tc_ref, tp_ref, tm_ref, lse_ref, do_ref,
             dq_ref, dk_ref, dv_ref, dsink_ref):
        i = pl.program_id(0)

        @pl.when(i == 0)
        def _():
            dk_ref[...] = jnp.zeros_like(dk_ref)
            dv_ref[...] = jnp.zeros_like(dv_ref)
            dsink_ref[...] = jnp.zeros_like(dsink_ref)

        per_kv = _swa_scores(i, nsb, sink_ref, q_ref, kc_ref, kp_ref, km_ref, tc_ref, tp_ref, tm_ref)
        lse_all = lse_ref[...]
        dq_l, dkc_l, dkp_l, dkm_l, dvc_l, dvp_l, dvm_l, ds_l = [], [], [], [], [], [], [], []
        for kv, (qg, kcb, kpb, kmb, s_c, s_p, s_m, sink) in enumerate(per_kv):
            ks = slice(kv * 64, (kv + 1) * 64)
            lse_g = _stack_cols(lse_all[:, kv * 4:(kv + 1) * 4])
            dog = _bf(_stack(do_ref[:, kv * 256:(kv + 1) * 256]))
            vcb, vpb, vmb = _bf(vc_ref[:, ks]), _bf(vp_ref[:, ks]), _bf(vm_ref[:NM, ks])
            p_c, p_p, p_m = jnp.exp(s_c - lse_g), jnp.exp(s_p - lse_g), jnp.exp(s_m - lse_g)
            dp_c, dp_p, dp_m = _dot_nt(dog, vcb), _dot_nt(dog, vpb), _dot_nt(dog, vmb)
            delta = (jnp.sum(p_c * dp_c, -1, keepdims=True) + jnp.sum(p_p * dp_p, -1, keepdims=True)
                     + jnp.sum(p_m * dp_m, -1, keepdims=True))
            ds_c, ds_p, ds_m = _bf(p_c * (dp_c - delta)), _bf(p_p * (dp_p - delta)), _bf(p_m * (dp_m - delta))
            dq_l.append(_unstack(_dot(ds_c, kcb) + _dot(ds_p, kpb) + _dot(ds_m, kmb)))
            dkc_l.append(_dot_tn(ds_c, qg))
            dkp_l.append(_dot_tn(ds_p, qg))
            dkm_l.append(_dot_tn(ds_m, qg))
            dvc_l.append(_dot_tn(_bf(p_c), dog))
            dvp_l.append(_dot_tn(_bf(p_p), dog))
            dvm_l.append(_dot_tn(_bf(p_m), dog))
            ds_l.append(_unstack(-jnp.exp(sink - lse_g) * delta))
        dq = jnp.concatenate(dq_l, axis=1) * 0.125
        dq_ref[...] = _bf(_rope(dq, tc_ref[...], -1.0))
        c0 = pl.multiple_of(i * SB, SB)
        p0 = pl.multiple_of(jnp.maximum(i - 1, 0) * SB, SB)
        dk_ref[pl.ds(c0, SB), :] += _rope(jnp.concatenate(dkc_l, axis=1), tc_ref[...], -1.0)
        dk_ref[pl.ds(p0, SB), :] += _rope(jnp.concatenate(dkp_l, axis=1), tp_ref[...], -1.0)
        dk_ref[pl.ds(t, NM), :] += _rope(jnp.concatenate(dkm_l, axis=1), tm_ref[:NM, :], -1.0)
        dv_ref[pl.ds(c0, SB), :] += jnp.concatenate(dvc_l, axis=1)
        dv_ref[pl.ds(p0, SB), :] += jnp.concatenate(dvp_l, axis=1)
        dv_ref[pl.ds(t, NM), :] += jnp.concatenate(dvm_l, axis=1)
        dsink_ref[...] += jnp.sum(jnp.concatenate(ds_l, axis=1), axis=0, keepdims=True)

    ck, cv = C_SK // 128, C_SV // 128
    whole = lambda w: pl.BlockSpec((r_tot, w), lambda i: (0, 0))
    return pl.pallas_call(
        body, name="swa_bwd", grid=(ns,),
        in_specs=[SMEM_SPEC, rows(512, C_SQ // 512, cur),
                  rows(128, ck, cur), rows(128, ck, prev), rows(128, ck, meta),
                  rows(128, cv, cur), rows(128, cv, prev), rows(128, cv, meta),
                  rows(384, 0, cur), rows(384, 0, prev), rows(384, 0, meta),
                  rows(8, 0, cur), rows(512, 1, cur)],
        out_specs=[rows(512, 0, cur), whole(128), whole(128), pl.BlockSpec((1, 8), lambda i: (0, 0))],
        out_shape=[jax.ShapeDtypeStruct((r_tot, 512), BF16), jax.ShapeDtypeStruct((r_tot, 128), F32),
                   jax.ShapeDtypeStruct((r_tot, 128), F32), jax.ShapeDtypeStruct((1, 8), F32)],
        compiler_params=_cp(48),
    )(sinks, proj, proj, proj, proj, proj, proj, proj, tabs, tabs, tabs, lse, do)


def _stack_cols(x):
    return jnp.concatenate([x[:, g:g + 1] for g in range(4)], axis=0)


def _mlp_fwd(x, metapad, tgt, ogla, oswa, wo, wff, w1, w2, wfin):
    t = x.shape[0]
    nblk = t // TM

    def body(x_ref, mp_ref, tgt_ref, og_ref, os_ref, wo_ref, wff_ref, w1_ref, w2_ref, wfin_ref,
             h1_ref, f_ref, a_ref, dh2_ref, loss_ref, gfin_ref):
        i = pl.program_id(0)

        @pl.when(i == 0)
        def _():
            loss_ref[...] = jnp.zeros_like(loss_ref)
            gfin_ref[...] = jnp.zeros_like(gfin_ref)

        h0 = jnp.where(i == nblk, mp_ref[...], x_ref[...])
        h1 = h0 + _dot(og_ref[...], wo_ref[0:512, :]) + _dot(os_ref[...], wo_ref[512:1024, :])
        h1_ref[...] = h1
        fh, _ = _rms(h1)
        f = _bf(fh * wff_ref[...])
        f_ref[...] = f
        acc = jnp.zeros((TM, D), F32)
        for n in range(4):
            a = _dot(f, w1_ref[n])
            a_ref[:, n * D:(n + 1) * D] = _bf(a)
            zr = jnp.maximum(a, 0.0)
            acc = acc + _dot(_bf(zr * zr), w2_ref[n])
        h2 = h1 + acc
        yh, rs2 = _rms(h2)
        wf = wfin_ref[...]
        real = i < nblk
        e = jnp.where(real, yh * wf - tgt_ref[...], 0.0)
        loss_ref[...] += jnp.sum(jnp.sum(e * e, axis=0, keepdims=True), axis=1, keepdims=True) * (0.5 / D)
        dy = e * (1.0 / D)
        gfin_ref[...] += jnp.sum(dy * yh, axis=0, keepdims=True)
        dh2_ref[...] = _rms_bwd(dy, yh, rs2, wf)

    xs = pl.BlockSpec((TM, D), lambda i: (jnp.minimum(i, nblk - 1), 0))
    rs = lambda w: pl.BlockSpec((TM, w), lambda i: (i, 0))
    r_tot = t + TM
    return pl.pallas_call(
        body, name="mlp_fwd", grid=(nblk + 1,),
        in_specs=[xs, VMEM_SPEC, xs, rs(512), rs(512), VMEM_SPEC, VMEM_SPEC, VMEM_SPEC, VMEM_SPEC, VMEM_SPEC],
        out_specs=[rs(D), rs(D), rs(DFF), rs(D), pl.BlockSpec((1, 1), lambda i: (0, 0)), pl.BlockSpec((1, D), lambda i: (0, 0))],
        out_shape=[jax.ShapeDtypeStruct((r_tot, D), F32), jax.ShapeDtypeStruct((r_tot, D), BF16),
                   jax.ShapeDtypeStruct((r_tot, DFF), BF16), jax.ShapeDtypeStruct((r_tot, D), F32),
                   jax.ShapeDtypeStruct((1, 1), F32), jax.ShapeDtypeStruct((1, D), F32)],
        compiler_params=_cp(56),
    )(x, metapad, tgt, ogla, oswa, wo, wff, w1, w2, wfin)


def _mlp_bwd(h1, a, dh2, ogla, oswa, wo, wff, w1, w2):
    r_tot = h1.shape[0]
    nt = r_tot // TM

    def body(h1_ref, a_ref, dh2_ref, og_ref, os_ref, wo_ref, wff_ref, w1_ref, w2_ref,
             da_ref, dh2b_ref, dh1_ref, do_ref, dwo_ref, gff_ref):
        i = pl.program_id(0)

        @pl.when(i == 0)
        def _():
            dwo_ref[...] = jnp.zeros_like(dwo_ref)
            gff_ref[...] = jnp.zeros_like(gff_ref)

        dh2 = dh2_ref[...]
        dh2b = _bf(dh2)
        dh2b_ref[...] = dh2b
        df = jnp.zeros((TM, D), F32)
        for n in range(4):
            dz = _dot_nt(dh2b, w2_ref[n])
            da = _bf(dz * (2.0 * jnp.maximum(a_ref[:, n * D:(n + 1) * D].astype(F32), 0.0)))
            da_ref[:, n * D:(n + 1) * D] = da
            df = df + _dot_nt(da, w1_ref[n])
        fh, rs1 = _rms(h1_ref[...])
        gff_ref[...] += jnp.sum(df * fh, axis=0, keepdims=True)
        dh1 = dh2 + _rms_bwd(df, fh, rs1, wff_ref[...])
        dh1_ref[...] = dh1
        dh1b = _bf(dh1)
        do_ref[...] = _dot_nt(dh1b, wo_ref[...])
        dwo_ref[0:512, :] += _dot_tn(og_ref[...], dh1b)
        dwo_ref[512:1024, :] += _dot_tn(os_ref[...], dh1b)

    rs = lambda w: pl.BlockSpec((TM, w), lambda i: (i, 0))
    return pl.pallas_call(
        body, name="mlp_bwd", grid=(nt,),
        in_specs=[rs(D), rs(DFF), rs(D), rs(512), rs(512), VMEM_SPEC, VMEM_SPEC, VMEM_SPEC, VMEM_SPEC],
        out_specs=[rs(DFF), rs(D), rs(D), rs(D), pl.BlockSpec((D, D), lambda i: (0, 0)),
                   pl.BlockSpec((1, D), lambda i: (0, 0))],
        out_shape=[jax.ShapeDtypeStruct((r_tot, DFF), BF16), jax.ShapeDtypeStruct((r_tot, D), BF16),
                   jax.ShapeDtypeStruct((r_tot, D), F32), jax.ShapeDtypeStruct((r_tot, D), F32),
                   jax.ShapeDtypeStruct((D, D), F32), jax.ShapeDtypeStruct((1, D), F32)],
        compiler_params=_cp(56),
    )(h1, a, dh2, ogla, oswa, wo, wff, w1, w2)


def _ffn_wgrad(f, a, da, dh2b):
    r_tot = f.shape[0]
    kt = 768 if r_tot % 768 == 0 else TM
    nk = r_tot // kt

    def body(f_ref, a_ref, da_ref, dh2_ref, dw1_ref, dw2_ref, acc1, acc2):
        k = pl.program_id(1)

        @pl.when(k == 0)
        def _():
            acc1[...] = jnp.zeros_like(acc1)
            acc2[...] = jnp.zeros_like(acc2)

        zr = jnp.maximum(a_ref[...], 0.0)
        acc1[...] += _dot_tn(f_ref[...], da_ref[...])
        acc2[...] += _dot_tn(zr * zr, dh2_ref[...])

        @pl.when(k == nk - 1)
        def _():
            for hh in range(2):
                dw1_ref[hh, 0] = acc1[hh * 512:(hh + 1) * 512, :]
                dw2_ref[hh, 0] = acc2[hh * 512:(hh + 1) * 512, :]

    full = pl.BlockSpec((kt, D), lambda n, k: (k, 0))
    col = pl.BlockSpec((kt, D), lambda n, k: (k, n))
    out = pl.BlockSpec((2, 1, 512, D), lambda n, k: (0, n, 0, 0))
    return pl.pallas_call(
        body, name="ffn_wgrad", grid=(4, nk),
        in_specs=[full, col, col, full], out_specs=[out, out],
        out_shape=[jax.ShapeDtypeStruct((2, 4, 512, D), F32)] * 2,
        scratch_shapes=[pltpu.VMEM((D, D), F32), pltpu.VMEM((D, D), F32)],
        compiler_params=_cp(48, ("arbitrary", "arbitrary")),
    )(f, a, da, dh2b)


def _proj_bwd(x, metapad, wm, winp, dgla, dswa_q, dsk, dsv, dlr, dh1):
    t = x.shape[0]
    nblk = t // TM

    def body(x_ref, mp_ref, wm_ref, w_ref, dg_ref, dq_ref, dk_ref, dv_ref, dlr_ref, dh1_ref,
             gx_ref, gmeta_ref, dw_ref, gmix_ref):
        i = pl.program_id(0)

        @pl.when(i == 0)
        def _():
            dw_ref[...] = jnp.zeros_like(dw_ref)
            gmix_ref[...] = jnp.zeros_like(gmix_ref)

        h = jnp.where(i == nblk, mp_ref[...], x_ref[...])
        uh, rs = _rms(h)
        wm_v = wm_ref[...]
        u = _bf(uh * wm_v)
        parts = ((dg_ref[...], C_GQ, 1536), (dq_ref[...], C_SQ, 512), (_bf(dk_ref[...]), C_SK, 128),
                 (_bf(dv_ref[...]), C_SV, 128), (dlr_ref[...], C_LR, 128))
        du = jnp.zeros((TM, D), F32)
        for val, c0, w in parts:
            du = du + _dot_nt(val, w_ref[:, c0:c0 + w])
            dw_ref[:, c0:c0 + w] += _dot_tn(u, val)
        gmix_ref[...] += jnp.sum(du * uh, axis=0, keepdims=True)
        dh0 = dh1_ref[...] + _rms_bwd(du, uh, rs, wm_v)

        @pl.when(i < nblk)
        def _():
            gx_ref[...] = dh0

        @pl.when(i == nblk)
        def _():
            gmeta_ref[...] = dh0[:NM]

    xs = pl.BlockSpec((TM, D), lambda i: (jnp.minimum(i, nblk - 1), 0))
    rs_ = lambda w: pl.BlockSpec((TM, w), lambda i: (i, 0))
    return pl.pallas_call(
        body, name="proj_bwd", grid=(nblk + 1,),
        in_specs=[xs, VMEM_SPEC, VMEM_SPEC, VMEM_SPEC, rs_(1536), rs_(512), rs_(128), rs_(128), rs_(128), rs_(D)],
        out_specs=[xs, pl.BlockSpec((NM, D), lambda i: (0, 0)), pl.BlockSpec((D, DINP), lambda i: (0, 0)),
                   pl.BlockSpec((1, D), lambda i: (0, 0))],
        out_shape=[jax.ShapeDtypeStruct((t, D), F32), jax.ShapeDtypeStruct((NM, D), F32),
                   jax.ShapeDtypeStruct((D, DINP), F32), jax.ShapeDtypeStruct((1, D), F32)],
        compiler_params=_cp(56),
    )(x, metapad, wm, winp, dgla, dswa_q, dsk, dsv, dlr, dh1)


def _permute_w_in(w):
    return jnp.concatenate([w[:, 0:1536], w[:, 1552:2320], w[:, 1536:1552], jnp.zeros((D, DINP - DIN), w.dtype)], axis=1)


def _unpermute_dw_in(dw):
    return jnp.concatenate([dw[:, 0:1536], dw[:, C_LR:C_LR + 16], dw[:, 1536:C_LR]], axis=1)


def _local_step(x, tgt, meta, wm, winp, wgu, bg, gnw, sinks, wo, wff, w1, w2, wfin):
    t = x.shape[0]
    metapad = jnp.concatenate([meta, jnp.zeros((TM - NM, D), F32)], axis=0)
    wgu_p = _bf(jnp.concatenate([wgu, jnp.zeros((128 - 16, 256), F32)], axis=0))
    tabs = _rope_tables(t)

    proj = _proj_fwd(x, metapad, wm, winp)
    ogla, oraw, sst = _gla_fwd(proj, wgu_p, bg, gnw, t)
    oswa, lse = _swa_fwd(proj, tabs, sinks, t)
    h1, f, a, dh2, loss, gfin = _mlp_fwd(x, metapad, tgt, ogla, oswa, wo, wff, w1, w2, wfin)
    da, dh2b, dh1, do, dwo, gff = _mlp_bwd(h1, a, dh2, ogla, oswa, wo, wff, w1, w2)
    dw1, dw2 = _ffn_wgrad(f, a, da, dh2b)
    dsq, dsk, dsv, dsink = _swa_bwd(proj, tabs, sinks, lse, do, t)
    dgla, dlr, dwgu, dbg, dgnw = _gla_bwd(proj, oraw, sst, do, wgu_p, bg, gnw, t)
    gx, gmeta, dwinp, gmix = _proj_bwd(x, metapad, wm, winp, dgla, dsq, dsk, dsv, dlr, dh1)
    grads = dict(meta=gmeta, mix=gmix, winp=dwinp, wgu=dwgu[:16], bg=dbg, gnw=dgnw, sinks=dsink,
                 wo=dwo, ff=gff, w1=dw1, w2=dw2, fin=gfin)
    return loss, gx, grads


def _place():
    return lax.axis_index("x"), lax.axis_index("y"), lax.axis_index("c")


def _other_chips(x, y):
    return [(1 - x, y), (x, 1 - y), (1 - x, 1 - y)]


def _gather_shards(shards):
    n = len(shards)

    def body(*refs):
        ins, outs = refs[:n], refs[n:2 * n]
        send, recv, loc = refs[2 * n:]
        x, y, c = _place()
        chips = _other_chips(x, y)

        def remote(a, k, shard_of):
            tx, ty = chips[k]
            sx, sy = shard_of
            return pltpu.make_async_remote_copy(
                src_ref=ins[a], dst_ref=outs[a].at[2 * sx + sy], send_sem=send.at[3 * a + k], recv_sem=recv.at[3 * a + k],
                device_id=(tx, ty, c), device_id_type=MESH)

        local = [pltpu.make_async_copy(ins[a], outs[a].at[2 * x + y], loc.at[a]) for a in range(n)]
        sends = [remote(a, k, (x, y)) for a in range(n) for k in range(3)]
        for cp in local + sends:
            cp.start()
        for a in range(n):
            for k in range(3):
                remote(a, k, chips[k]).wait_recv()
        for cp in sends:
            cp.wait_send()
        for cp in local:
            cp.wait()

    return pl.pallas_call(
        body, name="gather_shards",
        in_specs=[ANY_SPEC] * n, out_specs=[ANY_SPEC] * n,
        out_shape=[jax.ShapeDtypeStruct((4,) + s.shape, s.dtype) for s in shards],
        scratch_shapes=[pltpu.SemaphoreType.DMA((3 * n,)), pltpu.SemaphoreType.DMA((3 * n,)), pltpu.SemaphoreType.DMA((n,))],
    )(*shards)


def _swap_halves(grads):
    n = len(grads)

    def body(*refs):
        ins, outs = refs[:n], refs[n:2 * n]
        send, recv = refs[2 * n:]
        x, y, c = _place()
        cps = [pltpu.make_async_remote_copy(
            src_ref=ins[a].at[1 - c], dst_ref=outs[a], send_sem=send.at[a], recv_sem=recv.at[a],
            device_id=(x, y, 1 - c), device_id_type=MESH) for a in range(n)]
        for cp in cps:
            cp.start()
        for cp in cps:
            cp.wait()

    return pl.pallas_call(
        body, name="swap_halves",
        in_specs=[ANY_SPEC] * n, out_specs=[ANY_SPEC] * n,
        out_shape=[jax.ShapeDtypeStruct(g.shape[1:], g.dtype) for g in grads],
        scratch_shapes=[pltpu.SemaphoreType.DMA((n,)), pltpu.SemaphoreType.DMA((n,))],
    )(*grads)


def _scatter_shards(parts):
    n = len(parts)

    def body(*refs):
        ins, outs = refs[:n], refs[n:2 * n]
        send, recv = refs[2 * n:]
        x, y, c = _place()
        chips = _other_chips(x, y)
        cps = []
        for a in range(n):
            for k, (tx, ty) in enumerate(chips):
                cps.append(pltpu.make_async_remote_copy(
                    src_ref=ins[a].at[2 * tx + ty], dst_ref=outs[a].at[k], send_sem=send.at[3 * a + k],
                    recv_sem=recv.at[3 * a + k], device_id=(tx, ty, c), device_id_type=MESH))
        for cp in cps:
            cp.start()
        for cp in cps:
            cp.wait()

    return pl.pallas_call(
        body, name="scatter_shards",
        in_specs=[ANY_SPEC] * n, out_specs=[ANY_SPEC] * n,
        out_shape=[jax.ShapeDtypeStruct((3,) + p.shape[1:], p.dtype) for p in parts],
        scratch_shapes=[pltpu.SemaphoreType.DMA((3 * n,)), pltpu.SemaphoreType.DMA((3 * n,))],
    )(*parts)


def _join_halves(halves):
    n = len(halves)

    def body(*refs):
        ins, outs = refs[:n], refs[n:2 * n]
        send, recv, loc = refs[2 * n:]
        x, y, c = _place()
        local = [pltpu.make_async_copy(ins[a], outs[a].at[c], loc.at[a]) for a in range(n)]
        sends = [pltpu.make_async_remote_copy(
            src_ref=ins[a], dst_ref=outs[a].at[c], send_sem=send.at[a], recv_sem=recv.at[a],
            device_id=(x, y, 1 - c), device_id_type=MESH) for a in range(n)]
        for cp in local + sends:
            cp.start()
        for a in range(n):
            pltpu.make_async_remote_copy(
                src_ref=ins[a], dst_ref=outs[a].at[1 - c], send_sem=send.at[a], recv_sem=recv.at[a],
                device_id=(x, y, 1 - c), device_id_type=MESH).wait_recv()
        for cp in sends:
            cp.wait_send()
        for cp in local:
            cp.wait()

    return pl.pallas_call(
        body, name="join_halves",
        in_specs=[ANY_SPEC] * n, out_specs=[ANY_SPEC] * n,
        out_shape=[jax.ShapeDtypeStruct((2,) + h.shape, h.dtype) for h in halves],
        scratch_shapes=[pltpu.SemaphoreType.DMA((n,)), pltpu.SemaphoreType.DMA((n,)), pltpu.SemaphoreType.DMA((n,))],
    )(*halves)


def _allreduce_small(pack):
    p = pack.shape[0]

    def body(in_ref, out_ref, buf, send, recv):
        x, y, c = _place()
        me = 4 * x + 2 * y + c
        buf[me] = in_ref[...]
        sends = []
        for k in range(1, 8):
            tx, ty, tc = x ^ (k >> 2), y ^ ((k >> 1) & 1), c ^ (k & 1)
            sends.append(pltpu.make_async_remote_copy(
                src_ref=in_ref, dst_ref=buf.at[me], send_sem=send.at[k - 1], recv_sem=recv.at[k - 1],
                device_id=(tx, ty, tc), device_id_type=MESH))
        for cp in sends:
            cp.start()
        for k in range(1, 8):
            peer = 4 * (x ^ (k >> 2)) + 2 * (y ^ ((k >> 1) & 1)) + (c ^ (k & 1))
            pltpu.make_async_remote_copy(
                src_ref=in_ref, dst_ref=buf.at[peer], send_sem=send.at[k - 1], recv_sem=recv.at[k - 1],
                device_id=(x, y, c), device_id_type=MESH).wait_recv()
        for cp in sends:
            cp.wait_send()
        acc = buf[0]
        for d in range(1, 8):
            acc = acc + buf[d]
        out_ref[...] = acc

    return pl.pallas_call(
        body, name="allreduce_small",
        in_specs=[VMEM_SPEC], out_specs=VMEM_SPEC, out_shape=jax.ShapeDtypeStruct(pack.shape, F32),
        scratch_shapes=[pltpu.VMEM((8, p, D), F32), pltpu.SemaphoreType.DMA((7,)), pltpu.SemaphoreType.DMA((7,))],
    )(pack)


GRID4 = 4


def _sum_parts(sel, firsts, others):
    n = len(firsts)
    nk = others[0].shape[0]

    def body(sel_ref, *refs):
        fs, os_, outs = refs[:n], refs[n:2 * n], refs[2 * n:]
        for a in range(n):
            acc = fs[a][0]
            for k in range(nk):
                acc = acc + os_[a][k]
            outs[a][...] = acc

    def rows(a):
        return firsts[a].shape[1] // GRID4

    in_specs = ([pl.BlockSpec((1, rows(a), firsts[a].shape[2]), lambda i, s: (s[0], i, 0)) for a in range(n)]
                + [pl.BlockSpec((nk, rows(a), firsts[a].shape[2]), lambda i, s: (0, i, 0)) for a in range(n)])
    out_specs = [pl.BlockSpec((rows(a), firsts[a].shape[2]), lambda i, s: (i, 0)) for a in range(n)]
    return pl.pallas_call(
        body, name="sum_parts",
        grid_spec=pltpu.PrefetchScalarGridSpec(num_scalar_prefetch=1, grid=(GRID4,), in_specs=in_specs, out_specs=out_specs),
        out_shape=[jax.ShapeDtypeStruct(f.shape[1:], F32) for f in firsts],
        compiler_params=_cp(48),
    )(sel, *firsts, *others)


def _adamw_math(w, g, m, v):
    m2 = ADAM_B1 * m + (1.0 - ADAM_B1) * g
    v2 = ADAM_B2 * v + (1.0 - ADAM_B2) * (g * g)
    m_hat = m2 / (1.0 - ADAM_B1 ** ADAM_STEP)
    v_hat = v2 / (1.0 - ADAM_B2 ** ADAM_STEP)
    return -ADAM_LR * (m_hat / (jnp.sqrt(v_hat) + ADAM_EPS) + ADAM_WD * w), m2, v2


def _adamw_big(ws, gs, ms, vs):
    n = len(ws)

    def body(*refs):
        for a in range(n):
            d, m2, v2 = _adamw_math(refs[a][...], refs[n + a][...], refs[2 * n + a][...], refs[3 * n + a][...])
            refs[4 * n + a][...] = d
            refs[5 * n + a][...] = m2
            refs[6 * n + a][...] = v2

    specs = [pl.BlockSpec((w.shape[0] // GRID4, w.shape[1]), lambda i: (i, 0)) for w in ws]
    return pl.pallas_call(
        body, name="adamw_big", grid=(GRID4,),
        in_specs=specs * 4, out_specs=specs * 3,
        out_shape=[jax.ShapeDtypeStruct(w.shape, F32) for w in ws] * 3,
        compiler_params=_cp(48),
    )(*ws, *gs, *ms, *vs)


def _adamw_small(ws, gs, ms, vs):
    n = len(ws)

    def body(*refs):
        for a in range(n):
            d, m2, v2 = _adamw_math(refs[a][...], refs[n + a][...], refs[2 * n + a][...], refs[3 * n + a][...])
            refs[4 * n + a][...] = d
            refs[5 * n + a][...] = m2
            refs[6 * n + a][...] = v2

    return pl.pallas_call(
        body, name="adamw_small",
        in_specs=[VMEM_SPEC] * (4 * n), out_specs=[VMEM_SPEC] * (3 * n),
        out_shape=[jax.ShapeDtypeStruct(w.shape, F32) for w in ws] * 3,
    )(*ws, *gs, *ms, *vs)


def kernel(x, meta_tokens, norm_mix_w, w_in, w_gate_up, b_gate, gla_norm_w, sinks, w_out, norm_ff_w, w_ff1, w_ff2, final_norm_w, loss_target, m_meta_tokens, m_norm_mix_w, m_w_in, m_w_gate_up, m_b_gate, m_gla_norm_w, m_sinks, m_w_out, m_norm_ff_w, m_w_ff1, m_w_ff2, m_final_norm_w, v_meta_tokens, v_norm_mix_w, v_w_in, v_w_gate_up, v_b_gate, v_gla_norm_w, v_sinks, v_w_out, v_norm_ff_w, v_w_ff1, v_w_ff2, v_final_norm_w):
    xi, yi, ci = _place()
    shard = (2 * xi + yi).astype(jnp.int32).reshape(1)
    core = ci.astype(jnp.int32).reshape(1)

    small = jnp.concatenate([meta_tokens, w_gate_up[0], jnp.zeros((NM, 64), F32)], axis=1)
    g_in, g_out, g_1, g_2, g_small = _gather_shards(
        [_bf(w_in[0]), _bf(w_out[0]), _bf(w_ff1[0]), _bf(w_ff2[0]), small])
    winp = _permute_w_in(g_in.transpose(1, 0, 2).reshape(D, DIN))
    wo = g_out.reshape(D, D)
    meta = g_small[:, :, 0:256].transpose(1, 0, 2).reshape(NM, D)
    wgu = g_small[:, :, 256:320].transpose(1, 0, 2).reshape(NM, 256)

    loss, gx, g = _local_step(x[0], loss_target[0], meta, norm_mix_w, winp, wgu, b_gate, gla_norm_w, sinks,
                              wo, norm_ff_w, g_1, g_2, final_norm_w.reshape(1, D))

    dwin = _unpermute_dw_in(g["winp"]).reshape(2, 512, 4, 580).transpose(0, 2, 1, 3)
    dwo = g["wo"].reshape(4, 2, 128, D).transpose(1, 0, 2, 3)
    big = [dwin, dwo, g["w1"], g["w2"]]
    theirs = _swap_halves(big)
    chip_sum = _sum_parts(core, [b.reshape((2, -1) + b.shape[3:]) for b in big],
                          [t.reshape((1, -1) + t.shape[2:]) for t in theirs])
    chip_sum = [s.reshape(b.shape[1:]) for s, b in zip(chip_sum, big)]
    arrived = _scatter_shards(chip_sum)
    halves = _sum_parts(shard, chip_sum, arrived)
    gw_in, gw_out, gw_1, gw_2 = [f.reshape((-1, f.shape[2])) for f in _join_halves(halves)]

    tail = jnp.concatenate([g["bg"], g["gnw"], g["sinks"], loss, jnp.zeros((1, D - 256 - 128 - 8 - 1), F32)], axis=1)
    pack = jnp.concatenate([g["meta"], g["mix"], g["ff"], g["fin"], tail, g["wgu"].reshape(4, D)], axis=0)
    tot = _allreduce_small(pack)
    g_meta = lax.dynamic_slice_in_dim(tot[0:NM], shard[0] * 256, 256, axis=1)
    g_mix, g_ff, g_fin = tot[16:17], tot[17:18], tot[18]
    g_bg, g_gnw, g_sinks, loss_tot = tot[19:20, 0:256], tot[19:20, 256:384], tot[19:20, 384:392], tot[19, 392]
    g_wgu = lax.dynamic_slice_in_dim(tot[20:24].reshape(NM, 256), shard[0] * 64, 64, axis=1)

    bw = [w_in[0], w_out[0], w_ff1[0], w_ff2[0]]
    bg_ = [gw_in, gw_out, gw_1, gw_2]
    bm = [m_w_in[0], m_w_out[0], m_w_ff1[0], m_w_ff2[0]]
    bv = [v_w_in[0], v_w_out[0], v_w_ff1[0], v_w_ff2[0]]
    bo = _adamw_big(bw, bg_, bm, bv)
    d_in, d_out, d_1, d_2 = [o[None] for o in bo[0:4]]
    m_in, m_out, m_1, m_2 = [o[None] for o in bo[4:8]]
    v_in, v_out, v_1, v_2 = [o[None] for o in bo[8:12]]

    fin2 = lambda a: a.reshape(1, D)
    sw = [meta_tokens, norm_mix_w, w_gate_up[0], b_gate, gla_norm_w, sinks, norm_ff_w, fin2(final_norm_w)]
    sg = [g_meta, g_mix, g_wgu, g_bg, g_gnw, g_sinks, g_ff, fin2(g_fin)]
    sm = [m_meta_tokens, m_norm_mix_w, m_w_gate_up[0], m_b_gate, m_gla_norm_w, m_sinks, m_norm_ff_w, fin2(m_final_norm_w)]
    sv = [v_meta_tokens, v_norm_mix_w, v_w_gate_up[0], v_b_gate, v_gla_norm_w, v_sinks, v_norm_ff_w, fin2(v_final_norm_w)]
    so = _adamw_small(sw, sg, sm, sv)

    def small_outs(o):
        meta_, mix_, wgu_, bg__, gnw_, sinks_, ff_, fin_ = o
        return dict(meta_tokens=meta_, norm_mix_w=mix_, w_gate_up=wgu_[None], b_gate=bg__, gla_norm_w=gnw_, sinks=sinks_,
                    norm_ff_w=ff_, final_norm_w=fin_.reshape(D))

    def ordered(small_d, w_in_, w_out_, w_1_, w_2_):
        return (small_d["meta_tokens"], small_d["norm_mix_w"], w_in_, small_d["w_gate_up"], small_d["b_gate"],
                small_d["gla_norm_w"], small_d["sinks"], w_out_, small_d["norm_ff_w"], w_1_, w_2_, small_d["final_norm_w"])

    grads = ordered(small_outs(sg), gw_in[None], gw_out[None], gw_1[None], gw_2[None])
    deltas = ordered(small_outs(so[0:8]), d_in, d_out, d_1, d_2)
    new_m = ordered(small_outs(so[8:16]), m_in, m_out, m_1, m_2)
    new_v = ordered(small_outs(so[16:24]), v_in, v_out, v_1, v_2)
    return (loss_tot, gx[None], *grads, *deltas, *new_m, *new_v)
```

```python
from typing import Callable, NamedTuple

import jax
import jax.numpy as jnp
from jax import lax
from jax.experimental import pallas as pl
from jax.experimental.pallas import tpu as pltpu

F32 = jnp.float32
BF16 = jnp.bfloat16

D = 1024
DFF = 4096
NM = 16
TM = 256
CH = 64
SB = 128
EPS = 1e-5
C_GQ, C_GK, C_GV, C_GR, C_SQ, C_SK, C_SV, C_LR, DINP = 0, 256, 512, 1024, 1536, 2048, 2176, 2304, 2432
DIN = 2320
ROPE_THETA = 500000.0
ADAM_LR, ADAM_B1, ADAM_B2, ADAM_EPS, ADAM_WD, ADAM_STEP = 0.001, 0.9, 0.999, 1e-08, 0.01, 10
NEG = -1e30
MESH = pl.DeviceIdType.MESH
VMEM_SPEC = pl.BlockSpec(memory_space=pltpu.VMEM)
ANY_SPEC = pl.BlockSpec(memory_space=pl.ANY)
SMEM_SPEC = pl.BlockSpec(memory_space=pltpu.SMEM)


def _cp(vmem_mb, sem=("arbitrary",)):
    return pltpu.CompilerParams(dimension_semantics=sem, vmem_limit_bytes=vmem_mb << 20)


def _dot(a, b):
    return jnp.dot(a, b, preferred_element_type=F32)


def _dot_nt(a, b):
    return lax.dot_general(a, b, (((1,), (1,)), ((), ())), preferred_element_type=F32)


def _dot_tn(a, b):
    return lax.dot_general(a, b, (((0,), (0,)), ((), ())), preferred_element_type=F32)


def _bf(x):
    return x.astype(BF16)


def _dot3(m01, x):
    x1 = _bf(x)
    r1 = x - x1.astype(F32)
    x2 = _bf(r1)
    x3 = _bf(r1 - x2.astype(F32))
    return _dot(m01, x1) + _dot(m01, x2) + _dot(m01, x3)


def _rms(h):
    rs = lax.rsqrt(jnp.mean(h * h, axis=-1, keepdims=True) + EPS)
    return h * rs, rs


def _rms_bwd(dy, yhat, rs, w):
    dyh = dy * w
    return rs * (dyh - yhat * jnp.mean(dyh * yhat, axis=-1, keepdims=True))


class _Comm(NamedTuple):
    ins: tuple
    outs: tuple
    sems: tuple
    plan: Callable


def _call(body, name, grid, in_specs, out_specs, out_shape, scratch, params, args, comm=None):
    if comm is None:
        outs = pl.pallas_call(body, name=name, grid=grid, in_specs=in_specs, out_specs=out_specs, out_shape=out_shape,
                              scratch_shapes=scratch, compiler_params=params)(*args)
        return outs, None
    n_in, n_out, n_scr = len(in_specs), len(out_specs), len(scratch)
    ci, co = len(comm.ins), len(comm.outs)
    last = grid[0] - 1

    def wrapped(*refs):
        own_in, c_in = refs[:n_in], refs[n_in:n_in + ci]
        refs = refs[n_in + ci:]
        own_out, c_out = refs[:n_out], refs[n_out:n_out + co]
        refs = refs[n_out + co:]
        own_scr, c_sem = refs[:n_scr], refs[n_scr:]
        i = pl.program_id(0)

        @pl.when(i == 0)
        def _():
            for cp in comm.plan(c_in, c_out, c_sem)[0]:
                cp.start()

        body(*own_in, *own_out, *own_scr)

        @pl.when(i == last)
        def _():
            for wait in comm.plan(c_in, c_out, c_sem)[1]:
                wait()

    outs = pl.pallas_call(
        wrapped, name=name, grid=grid, in_specs=list(in_specs) + [ANY_SPEC] * ci, out_specs=list(out_specs) + [ANY_SPEC] * co,
        out_shape=list(out_shape) + list(comm.outs), scratch_shapes=list(scratch) + list(comm.sems), compiler_params=params,
    )(*args, *comm.ins)
    return outs[:n_out], outs[n_out:]


def _run_comm(comm, name):
    ci, co = len(comm.ins), len(comm.outs)

    def body(*refs):
        starts, waits = comm.plan(refs[:ci], refs[ci:ci + co], refs[ci + co:])
        for cp in starts:
            cp.start()
        for wait in waits:
            wait()

    return pl.pallas_call(body, name=name, in_specs=[ANY_SPEC] * ci, out_specs=[ANY_SPEC] * co, out_shape=list(comm.outs),
                          scratch_shapes=list(comm.sems))(*comm.ins)


def _proj_fwd(x, metapad, wm, winp):
    t = x.shape[0]
    nblk = t // TM

    def body(x_ref, mp_ref, wm_ref, w_ref, proj_ref):
        i = pl.program_id(0)
        h = jnp.where(i == nblk, mp_ref[...], x_ref[...])
        u, _ = _rms(h)
        proj_ref[...] = _dot(_bf(u * wm_ref[...]), w_ref[...])

    return pl.pallas_call(
        body, name="proj_fwd", grid=(nblk + 1,),
        in_specs=[pl.BlockSpec((TM, D), lambda i: (jnp.minimum(i, nblk - 1), 0)), VMEM_SPEC, VMEM_SPEC, VMEM_SPEC],
        out_specs=pl.BlockSpec((TM, DINP), lambda i: (i, 0)),
        out_shape=jax.ShapeDtypeStruct((t + TM, DINP), F32),
        compiler_params=_cp(40),
    )(x, metapad, wm, winp)


def _chunk_masks():
    r = lax.broadcasted_iota(jnp.int32, (TM, TM), 0)
    c = lax.broadcasted_iota(jnp.int32, (TM, TM), 1)
    same = (r // CH) == (c // CH)
    lower = _bf(jnp.where(same & (c <= r), 1.0, 0.0))
    upper = _bf(jnp.where(same & (c >= r), 1.0, 0.0))
    return lower, upper


def _gla_gate(lr, wgu, bg, valid, lower):
    z = _dot(_bf(lr), wgu) + bg
    g = (jnp.minimum(z, 0.0) - jnp.log(1.0 + jnp.exp(-jnp.abs(z)))) * (1.0 / 16.0)
    g = jnp.where(valid, g, 0.0)
    return z, _dot3(lower, g)


def _gla_decays(q, k, b):
    nc = TM // CH
    b3 = b.reshape(nc, CH, 256)
    blast = b3[:, CH - 1:CH, :]
    eb = jnp.exp(b)
    enb = jnp.exp(-b)
    ebl = jnp.exp(blast - b3).reshape(TM, 256)
    return eb, enb, ebl, jnp.exp(blast)


def _tri(lower_incl):
    r = lax.broadcasted_iota(jnp.int32, (CH, CH), 0)
    c = lax.broadcasted_iota(jnp.int32, (CH, CH), 1)
    return ((c <= r) if lower_incl else (c >= r))[None]


def _gla_fwd(proj, wgu, bg, gnw, t, comm=None):
    nblk = t // TM
    nt = nblk + 1
    nc = TM // CH

    def blk(i):
        return (i + nblk) % nt

    def body(q_ref, k_ref, v_ref, r_ref, lr_ref, wgu_ref, bg_ref, gnw_ref, o_ref, oraw_ref, sst_ref, st_scr):
        i = pl.program_id(0)

        @pl.when(i == 0)
        def _():
            st_scr[...] = jnp.zeros_like(st_scr)

        rows = blk(i) * TM + lax.broadcasted_iota(jnp.int32, (TM, 1), 0)
        lower, _ = _chunk_masks()
        _, b = _gla_gate(lr_ref[...], wgu_ref[...], bg_ref[...], rows < t + NM, lower)
        q = q_ref[...]
        k = k_ref[...]
        eb, enb, ebl, eblast = _gla_decays(q, k, b)
        qt = q * 0.125 * eb
        kt = k * enb
        kh = k * ebl
        tril = _tri(True)
        outs = []
        for h in range(4):
            hs = slice(h * CH, (h + 1) * CH)
            qh = _bf(qt[:, hs]).reshape(nc, CH, CH)
            kth = _bf(kt[:, hs]).reshape(nc, CH, CH)
            khh = _bf(kh[:, hs]).reshape(nc, CH, CH)
            vh = _bf(v_ref[:, h * 128:(h + 1) * 128]).reshape(nc, CH, 128)
            a = jnp.einsum('cid,cjd->cij', qh, kth, preferred_element_type=F32)
            a = jnp.where(tril, a, 0.0)
            o = jnp.einsum('cij,cjv->civ', _bf(a), vh, preferred_element_type=F32)
            kv = jnp.einsum('cjv,cjd->cvd', vh, khh, preferred_element_type=F32)
            st = st_scr[h]
            o_inter = []
            for c in range(nc):
                sst_ref[c, h] = st
                o_inter.append(_dot_nt(qh[c], _bf(st)))
                st = st * eblast[c, :, hs] + kv[c]
            st_scr[h] = st
            outs.append((o + jnp.stack(o_inter)).reshape(TM, 128))
        oraw = jnp.concatenate(outs, axis=1)
        oraw_ref[...] = oraw
        gn = gnw_ref[...]
        res = []
        for h in range(4):
            on, _ = _rms(oraw[:, h * 128:(h + 1) * 128])
            r = r_ref[:, h * 128:(h + 1) * 128]
            res.append(on * gn * (r * jax.nn.sigmoid(r)))
        o_ref[...] = _bf(jnp.concatenate(res, axis=1))

    def spec(w, cb):
        return pl.BlockSpec((TM, w), lambda i: (blk(i), cb))

    return _call(
        body, "gla_fwd", (nt,),
        [spec(256, 0), spec(256, 1), spec(512, 1), spec(512, 2), spec(128, C_LR // 128), VMEM_SPEC, VMEM_SPEC, VMEM_SPEC],
        [spec(512, 0), spec(512, 0), pl.BlockSpec((nc, 4, 128, CH), lambda i: (blk(i), 0, 0, 0))],
        [jax.ShapeDtypeStruct((t + TM, 512), BF16), jax.ShapeDtypeStruct((t + TM, 512), F32),
         jax.ShapeDtypeStruct((nt * nc, 4, 128, CH), F32)],
        [pltpu.VMEM((4, 128, CH), F32)], _cp(40), (proj, proj, proj, proj, proj, wgu, bg, gnw), comm)


def _gla_bwd(proj, oraw, sst, do, wgu, bg, gnw, t, comm=None):
    nblk = t // TM
    nt = nblk + 1
    nc = TM // CH

    def blk(i):
        return (2 * nblk - i) % nt

    def body(q_ref, k_ref, v_ref, r_ref, lr_ref, oraw_ref, sst_ref, do_ref, wgu_ref, bg_ref, gnw_ref,
             dgla_ref, dlr_ref, dwgu_ref, dbg_ref, dgnw_ref, dst_scr):
        i = pl.program_id(0)

        @pl.when(i == 0)
        def _():
            dst_scr[...] = jnp.zeros_like(dst_scr)
            dwgu_ref[...] = jnp.zeros_like(dwgu_ref)
            dbg_ref[...] = jnp.zeros_like(dbg_ref)
            dgnw_ref[...] = jnp.zeros_like(dgnw_ref)

        rows = blk(i) * TM + lax.broadcasted_iota(jnp.int32, (TM, 1), 0)
        valid = rows < t + NM
        lower, upper = _chunk_masks()
        lr = lr_ref[...]
        z, b = _gla_gate(lr, wgu_ref[...], bg_ref[...], valid, lower)
        q = q_ref[...]
        k = k_ref[...]
        eb, enb, ebl, eblast = _gla_decays(q, k, b)
        qt = q * 0.125 * eb
        kt = k * enb
        kh = k * ebl
        gn = gnw_ref[...]
        tril = _tri(True)
        triu = _tri(False)
        dq_l, dk_l, dv_l, dr_l, db_l, ex_l = [], [], [], [], [], []
        dgn = jnp.zeros((1, 128), F32)
        for h in range(4):
            hs = slice(h * CH, (h + 1) * CH)
            vs = slice(h * 128, (h + 1) * 128)
            on, rs = _rms(oraw_ref[:, vs])
            r = r_ref[:, vs]
            sig = jax.nn.sigmoid(r)
            sil = r * sig
            dy = do_ref[:, vs]
            dr_l.append(dy * on * gn * (sig * (1.0 + r * (1.0 - sig))))
            dgn = dgn + jnp.sum(dy * sil * on, axis=0, keepdims=True)
            doraw = _rms_bwd(dy * sil, on, rs, gn)
            qtf = qt[:, hs].reshape(nc, CH, CH)
            ktf = kt[:, hs].reshape(nc, CH, CH)
            khf = kh[:, hs].reshape(nc, CH, CH)
            qh, kth, khh = _bf(qtf), _bf(ktf), _bf(khf)
            vh = _bf(v_ref[:, vs]).reshape(nc, CH, 128)
            doh = _bf(doraw).reshape(nc, CH, 128)
            at = jnp.where(triu, jnp.einsum('cjd,cid->cji', kth, qh, preferred_element_type=F32), 0.0)
            da = jnp.where(tril, jnp.einsum('civ,cjv->cij', doh, vh, preferred_element_type=F32), 0.0)
            dat = jnp.where(triu, jnp.einsum('cjv,civ->cji', vh, doh, preferred_element_type=F32), 0.0)
            dv = jnp.einsum('cji,civ->cjv', _bf(at), doh, preferred_element_type=F32)
            dqt = jnp.einsum('cij,cjd->cid', _bf(da), kth, preferred_element_type=F32)
            dkt = jnp.einsum('cji,cid->cjd', _bf(dat), qh, preferred_element_type=F32)
            gq = jnp.einsum('civ,cid->cvd', doh, qh, preferred_element_type=F32)
            dst = dst_scr[h]
            dsend = [None] * nc
            for c in reversed(range(nc)):
                dsend[c] = dst
                dst = dst * eblast[c, :, hs] + gq[c]
            dst_scr[h] = dst
            dse = jnp.stack(dsend)
            dseb = _bf(dse)
            stf = sst_ref[:, h]
            dqt = dqt + jnp.einsum('civ,cvd->cid', doh, _bf(stf), preferred_element_type=F32)
            dv = dv + jnp.einsum('cjd,cvd->cjv', khh, dseb, preferred_element_type=F32)
            dkh = jnp.einsum('cjv,cvd->cjd', vh, dseb, preferred_element_type=F32)
            extra = (jnp.sum(dkh * khf, axis=1, keepdims=True)
                     + eblast[:, :, hs] * jnp.sum(dse * stf, axis=1, keepdims=True))
            db_l.append((dqt * qtf - dkt * ktf - dkh * khf).reshape(TM, CH))
            ex_l.append(jnp.broadcast_to(extra, (nc, CH, CH)).reshape(TM, CH))
            dq_l.append((dqt.reshape(TM, CH)) * eb[:, hs] * 0.125)
            dk_l.append(dkt.reshape(TM, CH) * enb[:, hs] + dkh.reshape(TM, CH) * ebl[:, hs])
            dv_l.append(dv.reshape(TM, 128))
        dgnw_ref[...] += dgn
        db = jnp.concatenate(db_l, axis=1)
        dg = _dot3(upper, db) + jnp.concatenate(ex_l, axis=1)
        dz = jnp.where(valid, dg * (1.0 / 16.0) / (1.0 + jnp.exp(z)), 0.0)
        dzb = _bf(dz)
        dlr_ref[...] = _bf(_dot_nt(dzb, wgu_ref[...]))
        dwgu_ref[...] += _dot_tn(_bf(lr), dzb)
        dbg_ref[...] += jnp.sum(dz, axis=0, keepdims=True)
        dgla_ref[...] = _bf(jnp.concatenate(dq_l + dk_l + dv_l + dr_l, axis=1))

    def spec(w, cb):
        return pl.BlockSpec((TM, w), lambda i: (blk(i), cb))

    def acc(shape):
        return pl.BlockSpec(shape, lambda i: (0, 0))

    return _call(
        body, "gla_bwd", (nt,),
        [spec(256, 0), spec(256, 1), spec(512, 1), spec(512, 2), spec(128, C_LR // 128), spec(512, 0),
         pl.BlockSpec((nc, 4, 128, CH), lambda i: (blk(i), 0, 0, 0)), spec(512, 0), VMEM_SPEC, VMEM_SPEC, VMEM_SPEC],
        [spec(1536, 0), spec(128, 0), acc((128, 256)), acc((1, 256)), acc((1, 128))],
        [jax.ShapeDtypeStruct((t + TM, 1536), BF16), jax.ShapeDtypeStruct((t + TM, 128), BF16),
         jax.ShapeDtypeStruct((128, 256), F32), jax.ShapeDtypeStruct((1, 256), F32), jax.ShapeDtypeStruct((1, 128), F32)],
        [pltpu.VMEM((4, 128, CH), F32)], _cp(48), (proj, proj, proj, proj, proj, oraw, sst, do, wgu, bg, gnw), comm)


def _rope_tables(t):
    r = t + TM
    row = jnp.arange(r, dtype=jnp.int32)
    pos = jnp.where(row < t, row + NM, jnp.where(row < t + NM, row - t, 0))
    inv_freq = 1.0 / (ROPE_THETA ** (jnp.arange(0, 16, 2, dtype=F32) / 16))
    ang = pos.astype(F32)[:, None] * inv_freq[None, :]
    cos, sin = jnp.cos(ang), jnp.sin(ang)
    one, zero = jnp.ones((r, 48), F32), jnp.zeros((r, 48), F32)
    z8 = jnp.zeros((r, 8), F32)
    c = jnp.concatenate([cos, cos, one], axis=1)
    s1 = jnp.concatenate([-sin, z8, zero], axis=1)
    s2 = jnp.concatenate([z8, sin, zero], axis=1)
    return jnp.concatenate([c, c, s1, s1, s2, s2], axis=1)


def _rope(x, tab, sign):
    w = x.shape[1]
    rep = w // 128
    c, s1, s2 = (jnp.tile(tab[:, j * 128:(j + 1) * 128], (1, rep)) if rep > 1 else tab[:, j * 128:(j + 1) * 128]
                 for j in range(3))
    return x * c + sign * (pltpu.roll(x, w - 8, 1) * s1 + pltpu.roll(x, 8, 1) * s2)


def _stack(x):
    return jnp.concatenate([x[:, g * 64:(g + 1) * 64] for g in range(4)], axis=0)


def _unstack(x):
    return jnp.concatenate([x[g * SB:(g + 1) * SB] for g in range(4)], axis=1)


def _swa_masks(i, nsb):
    r = lax.rem(lax.broadcasted_iota(jnp.int32, (4 * SB, SB), 0), SB)
    c = lax.broadcasted_iota(jnp.int32, (4 * SB, SB), 1)
    real = i < nsb
    return c <= r, (c > r) & (i > 0) & real, real


def _swa_specs(nsb):
    def rows(w, cb, f):
        return pl.BlockSpec((SB, w), lambda i: (f(i), cb))
    cur = lambda i: i
    prev = lambda i: jnp.maximum(i - 1, 0)
    meta = lambda i: nsb
    return rows, cur, prev, meta


def _swa_scores(i, nsb, sink_ref, q_ref, kc_ref, kp_ref, km_ref, tc_ref, tp_ref, tm_ref):
    mc, mp, real = _swa_masks(i, nsb)
    qr = _rope(q_ref[...], tc_ref[...], 1.0) * 0.125
    kc = _rope(kc_ref[...], tc_ref[...], 1.0)
    kp = _rope(kp_ref[...], tp_ref[...], 1.0)
    km = _rope(km_ref[...], tm_ref[...], 1.0)[:NM]
    per_kv = []
    for kv in range(2):
        ks = slice(kv * 64, (kv + 1) * 64)
        qg = _bf(_stack(qr[:, kv * 256:(kv + 1) * 256]))
        kcb, kpb, kmb = _bf(kc[:, ks]), _bf(kp[:, ks]), _bf(km[:, ks])
        s_c = jnp.where(mc, _dot_nt(qg, kcb), NEG)
        s_p = jnp.where(mp, _dot_nt(qg, kpb), NEG)
        s_m = jnp.where(real, _dot_nt(qg, kmb), NEG)
        sink = jnp.concatenate([jnp.full((SB, 1), sink_ref[0, kv * 4 + g], F32) for g in range(4)], axis=0)
        per_kv.append((qg, kcb, kpb, kmb, s_c, s_p, s_m, sink))
    return per_kv


def _swa_fwd(proj, tabs, sinks, t, comm=None):
    nsb = t // SB
    ns = nsb + 2
    rows, cur, prev, meta = _swa_specs(nsb)

    def body(sink_ref, q_ref, kc_ref, kp_ref, km_ref, vc_ref, vp_ref, vm_ref, tc_ref, tp_ref, tm_ref, o_ref, lse_ref):
        i = pl.program_id(0)
        per_kv = _swa_scores(i, nsb, sink_ref, q_ref, kc_ref, kp_ref, km_ref, tc_ref, tp_ref, tm_ref)
        valid = i * SB + lax.broadcasted_iota(jnp.int32, (SB, 1), 0) < t + NM
        o_l, lse_l = [], []
        for kv, (qg, kcb, kpb, kmb, s_c, s_p, s_m, sink) in enumerate(per_kv):
            ks = slice(kv * 64, (kv + 1) * 64)
            m = jnp.maximum(jnp.maximum(jnp.max(s_c, -1, keepdims=True), jnp.max(s_p, -1, keepdims=True)),
                            jnp.maximum(jnp.max(s_m, -1, keepdims=True), sink))
            p_c, p_p, p_m = jnp.exp(s_c - m), jnp.exp(s_p - m), jnp.exp(s_m - m)
            l = (jnp.sum(p_c, -1, keepdims=True) + jnp.sum(p_p, -1, keepdims=True)
                 + jnp.sum(p_m, -1, keepdims=True) + jnp.exp(sink - m))
            inv = 1.0 / l
            o = (_dot(_bf(p_c * inv), _bf(vc_ref[:, ks])) + _dot(_bf(p_p * inv), _bf(vp_ref[:, ks]))
                 + _dot(_bf(p_m * inv), _bf(vm_ref[:NM, ks])))
            o_l.append(_unstack(o))
            lse_l.append(_unstack(m + jnp.log(l)))
        o_ref[...] = _bf(jnp.where(valid, jnp.concatenate(o_l, axis=1), 0.0))
        lse_ref[...] = jnp.concatenate(lse_l, axis=1)

    ck, cv = C_SK // 128, C_SV // 128
    return _call(
        body, "swa_fwd", (ns,),
        [SMEM_SPEC, rows(512, C_SQ // 512, cur),
         rows(128, ck, cur), rows(128, ck, prev), rows(128, ck, meta),
         rows(128, cv, cur), rows(128, cv, prev), rows(128, cv, meta),
         rows(384, 0, cur), rows(384, 0, prev), rows(384, 0, meta)],
        [rows(512, 0, cur), rows(8, 0, cur)],
        [jax.ShapeDtypeStruct((t + TM, 512), BF16), jax.ShapeDtypeStruct((t + TM, 8), F32)],
        [], _cp(32), (sinks, proj, proj, proj, proj, proj, proj, proj, tabs, tabs, tabs), comm)


def _swa_bwd(proj, tabs, sinks, lse, do, t, comm=None):
    nsb = t // SB
    ns = nsb + 2
    r_tot = t + TM
    rows, cur, prev, meta = _swa_specs(nsb)

    def body(sink_ref, q_ref, kc_ref, kp_ref, km_ref, vc_ref, vp_ref, vm_ref, tc_ref, tp_ref, tm_ref, lse_ref, do_ref,
             dq_ref, dk_ref, dv_ref, dsink_ref):
        i = pl.program_id(0)

        @pl.when(i == 0)
        def _():
            dk_ref[...] = jnp.zeros_like(dk_ref)
            dv_ref[...] = jnp.zeros_like(dv_ref)
            dsink_ref[...] = jnp.zeros_like(dsink_ref)

        per_kv = _swa_scores(i, nsb, sink_ref, q_ref, kc_ref, kp_ref, km_ref, tc_ref, tp_ref, tm_ref)
        lse_all = lse_ref[...]
        dq_l, dkc_l, dkp_l, dkm_l, dvc_l, dvp_l, dvm_l, ds_l = [], [], [], [], [], [], [], []
        for kv, (qg, kcb, kpb, kmb, s_c, s_p, s_m, sink) in enumerate(per_kv):
            ks = slice(kv * 64, (kv + 1) * 64)
            lse_g = _stack_cols(lse_all[:, kv * 4:(kv + 1) * 4])
            dog = _bf(_stack(do_ref[:, kv * 256:(kv + 1) * 256]))
            vcb, vpb, vmb = _bf(vc_ref[:, ks]), _bf(vp_ref[:, ks]), _bf(vm_ref[:NM, ks])
            p_c, p_p, p_m = jnp.exp(s_c - lse_g), jnp.exp(s_p - lse_g), jnp.exp(s_m - lse_g)
            dp_c, dp_p, dp_m = _dot_nt(dog, vcb), _dot_nt(dog, vpb), _dot_nt(dog, vmb)
            delta = (jnp.sum(p_c * dp_c, -1, keepdims=True) + jnp.sum(p_p * dp_p, -1, keepdims=True)
                     + jnp.sum(p_m * dp_m, -1, keepdims=True))
            ds_c, ds_p, ds_m = _bf(p_c * (dp_c - delta)), _bf(p_p * (dp_p - delta)), _bf(p_m * (dp_m - delta))
            dq_l.append(_unstack(_dot(ds_c, kcb) + _dot(ds_p, kpb) + _dot(ds_m, kmb)))
            dkc_l.append(_dot_tn(ds_c, qg))
            dkp_l.append(_dot_tn(ds_p, qg))
            dkm_l.append(_dot_tn(ds_m, qg))
            dvc_l.append(_dot_tn(_bf(p_c), dog))
            dvp_l.append(_dot_tn(_bf(p_p), dog))
            dvm_l.append(_dot_tn(_bf(p_m), dog))
            ds_l.append(_unstack(-jnp.exp(sink - lse_g) * delta))
        dq = jnp.concatenate(dq_l, axis=1) * 0.125
        dq_ref[...] = _bf(_rope(dq, tc_ref[...], -1.0))
        c0 = pl.multiple_of(i * SB, SB)
        p0 = pl.multiple_of(jnp.maximum(i - 1, 0) * SB, SB)
        dk_ref[pl.ds(c0, SB), :] += _rope(jnp.concatenate(dkc_l, axis=1), tc_ref[...], -1.0)
        dk_ref[pl.ds(p0, SB), :] += _rope(jnp.concatenate(dkp_l, axis=1), tp_ref[...], -1.0)
        dk_ref[pl.ds(t, NM), :] += _rope(jnp.concatenate(dkm_l, axis=1), tm_ref[:NM, :], -1.0)
        dv_ref[pl.ds(c0, SB), :] += jnp.concatenate(dvc_l, axis=1)
        dv_ref[pl.ds(p0, SB), :] += jnp.concatenate(dvp_l, axis=1)
        dv_ref[pl.ds(t, NM), :] += jnp.concatenate(dvm_l, axis=1)
        dsink_ref[...] += jnp.sum(jnp.concatenate(ds_l, axis=1), axis=0, keepdims=True)

    ck, cv = C_SK // 128, C_SV // 128
    whole = lambda w: pl.BlockSpec((r_tot, w), lambda i: (0, 0))
    return _call(
        body, "swa_bwd", (ns,),
        [SMEM_SPEC, rows(512, C_SQ // 512, cur),
         rows(128, ck, cur), rows(128, ck, prev), rows(128, ck, meta),
         rows(128, cv, cur), rows(128, cv, prev), rows(128, cv, meta),
         rows(384, 0, cur), rows(384, 0, prev), rows(384, 0, meta),
         rows(8, 0, cur), rows(512, 1, cur)],
        [rows(512, 0, cur), whole(128), whole(128), pl.BlockSpec((1, 8), lambda i: (0, 0))],
        [jax.ShapeDtypeStruct((r_tot, 512), BF16), jax.ShapeDtypeStruct((r_tot, 128), F32),
         jax.ShapeDtypeStruct((r_tot, 128), F32), jax.ShapeDtypeStruct((1, 8), F32)],
        [], _cp(48), (sinks, proj, proj, proj, proj, proj, proj, proj, tabs, tabs, tabs, lse, do), comm)


def _stack_cols(x):
    return jnp.concatenate([x[:, g:g + 1] for g in range(4)], axis=0)


def _mlp_fwd(x, metapad, tgt, ogla, oswa, wo, wff, w1, w2, wfin):
    t = x.shape[0]
    nblk = t // TM

    def body(x_ref, mp_ref, tgt_ref, og_ref, os_ref, wo_ref, wff_ref, w1_ref, w2_ref, wfin_ref,
             h1_ref, f_ref, a_ref, dh2_ref, loss_ref, gfin_ref):
        i = pl.program_id(0)

        @pl.when(i == 0)
        def _():
            loss_ref[...] = jnp.zeros_like(loss_ref)
            gfin_ref[...] = jnp.zeros_like(gfin_ref)

        h0 = jnp.where(i == nblk, mp_ref[...], x_ref[...])
        h1 = h0 + _dot(og_ref[...], wo_ref[0:512, :]) + _dot(os_ref[...], wo_ref[512:1024, :])
        h1_ref[...] = h1
        fh, _ = _rms(h1)
        f = _bf(fh * wff_ref[...])
        f_ref[...] = f
        acc = jnp.zeros((TM, D), F32)
        for n in range(4):
            a = _dot(f, w1_ref[n])
            a_ref[:, n * D:(n + 1) * D] = _bf(a)
            zr = jnp.maximum(a, 0.0)
            acc = acc + _dot(_bf(zr * zr), w2_ref[n])
        h2 = h1 + acc
        yh, rs2 = _rms(h2)
        wf = wfin_ref[...]
        real = i < nblk
        e = jnp.where(real, yh * wf - tgt_ref[...], 0.0)
        loss_ref[...] += jnp.sum(jnp.sum(e * e, axis=0, keepdims=True), axis=1, keepdims=True) * (0.5 / D)
        dy = e * (1.0 / D)
        gfin_ref[...] += jnp.sum(dy * yh, axis=0, keepdims=True)
        dh2_ref[...] = _rms_bwd(dy, yh, rs2, wf)

    xs = pl.BlockSpec((TM, D), lambda i: (jnp.minimum(i, nblk - 1), 0))
    rs = lambda w: pl.BlockSpec((TM, w), lambda i: (i, 0))
    r_tot = t + TM
    return pl.pallas_call(
        body, name="mlp_fwd", grid=(nblk + 1,),
        in_specs=[xs, VMEM_SPEC, xs, rs(512), rs(512), VMEM_SPEC, VMEM_SPEC, VMEM_SPEC, VMEM_SPEC, VMEM_SPEC],
        out_specs=[rs(D), rs(D), rs(DFF), rs(D), pl.BlockSpec((1, 1), lambda i: (0, 0)), pl.BlockSpec((1, D), lambda i: (0, 0))],
        out_shape=[jax.ShapeDtypeStruct((r_tot, D), F32), jax.ShapeDtypeStruct((r_tot, D), BF16),
                   jax.ShapeDtypeStruct((r_tot, DFF), BF16), jax.ShapeDtypeStruct((r_tot, D), F32),
                   jax.ShapeDtypeStruct((1, 1), F32), jax.ShapeDtypeStruct((1, D), F32)],
        compiler_params=_cp(56),
    )(x, metapad, tgt, ogla, oswa, wo, wff, w1, w2, wfin)


def _mlp_bwd(h1, a, dh2, ogla, oswa, wo, wff, w1, w2):
    r_tot = h1.shape[0]
    nt = r_tot // TM

    def body(h1_ref, a_ref, dh2_ref, og_ref, os_ref, wo_ref, wff_ref, w1_ref, w2_ref,
             da_ref, dh2b_ref, dh1_ref, do_ref, dwo_ref, gff_ref):
        i = pl.program_id(0)

        @pl.when(i == 0)
        def _():
            dwo_ref[...] = jnp.zeros_like(dwo_ref)
            gff_ref[...] = jnp.zeros_like(gff_ref)

        dh2 = dh2_ref[...]
        dh2b = _bf(dh2)
        dh2b_ref[...] = dh2b
        df = jnp.zeros((TM, D), F32)
        for n in range(4):
            dz = _dot_nt(dh2b, w2_ref[n])
            da = _bf(dz * (2.0 * jnp.maximum(a_ref[:, n * D:(n + 1) * D].astype(F32), 0.0)))
            da_ref[:, n * D:(n + 1) * D] = da
            df = df + _dot_nt(da, w1_ref[n])
        fh, rs1 = _rms(h1_ref[...])
        gff_ref[...] += jnp.sum(df * fh, axis=0, keepdims=True)
        dh1 = dh2 + _rms_bwd(df, fh, rs1, wff_ref[...])
        dh1_ref[...] = dh1
        dh1b = _bf(dh1)
        do_ref[...] = _dot_nt(dh1b, wo_ref[...])
        dwo_ref[0:512, :] += _dot_tn(og_ref[...], dh1b)
        dwo_ref[512:1024, :] += _dot_tn(os_ref[...], dh1b)

    rs = lambda w: pl.BlockSpec((TM, w), lambda i: (i, 0))
    return pl.pallas_call(
        body, name="mlp_bwd", grid=(nt,),
        in_specs=[rs(D), rs(DFF), rs(D), rs(512), rs(512), VMEM_SPEC, VMEM_SPEC, VMEM_SPEC, VMEM_SPEC],
        out_specs=[rs(DFF), rs(D), rs(D), rs(D), pl.BlockSpec((D, D), lambda i: (0, 0)),
                   pl.BlockSpec((1, D), lambda i: (0, 0))],
        out_shape=[jax.ShapeDtypeStruct((r_tot, DFF), BF16), jax.ShapeDtypeStruct((r_tot, D), BF16),
                   jax.ShapeDtypeStruct((r_tot, D), F32), jax.ShapeDtypeStruct((r_tot, D), F32),
                   jax.ShapeDtypeStruct((D, D), F32), jax.ShapeDtypeStruct((1, D), F32)],
        compiler_params=_cp(56),
    )(h1, a, dh2, ogla, oswa, wo, wff, w1, w2)


def _ffn_wgrad(f, a, da, dh2b):
    r_tot = f.shape[0]
    kt = 768 if r_tot % 768 == 0 else TM
    nk = r_tot // kt

    def body(f_ref, a_ref, da_ref, dh2_ref, dw1_ref, dw2_ref, acc1, acc2):
        k = pl.program_id(1)

        @pl.when(k == 0)
        def _():
            acc1[...] = jnp.zeros_like(acc1)
            acc2[...] = jnp.zeros_like(acc2)

        zr = jnp.maximum(a_ref[...], 0.0)
        acc1[...] += _dot_tn(f_ref[...], da_ref[...])
        acc2[...] += _dot_tn(zr * zr, dh2_ref[...])

        @pl.when(k == nk - 1)
        def _():
            for hh in range(2):
                dw1_ref[hh, 0] = acc1[hh * 512:(hh + 1) * 512, :]
                dw2_ref[hh, 0] = acc2[hh * 512:(hh + 1) * 512, :]

    full = pl.BlockSpec((kt, D), lambda n, k: (k, 0))
    col = pl.BlockSpec((kt, D), lambda n, k: (k, n))
    out = pl.BlockSpec((2, 1, 512, D), lambda n, k: (0, n, 0, 0))
    return pl.pallas_call(
        body, name="ffn_wgrad", grid=(4, nk),
        in_specs=[full, col, col, full], out_specs=[out, out],
        out_shape=[jax.ShapeDtypeStruct((2, 4, 512, D), F32)] * 2,
        scratch_shapes=[pltpu.VMEM((D, D), F32), pltpu.VMEM((D, D), F32)],
        compiler_params=_cp(48, ("arbitrary", "arbitrary")),
    )(f, a, da, dh2b)


def _proj_bwd(x, metapad, wm, winp, dgla, dswa_q, dsk, dsv, dlr, dh1, comm=None):
    t = x.shape[0]
    nblk = t // TM

    def body(x_ref, mp_ref, wm_ref, w_ref, dg_ref, dq_ref, dk_ref, dv_ref, dlr_ref, dh1_ref,
             gx_ref, gmeta_ref, dw_ref, gmix_ref):
        i = pl.program_id(0)

        @pl.when(i == 0)
        def _():
            dw_ref[...] = jnp.zeros_like(dw_ref)
            gmix_ref[...] = jnp.zeros_like(gmix_ref)

        h = jnp.where(i == nblk, mp_ref[...], x_ref[...])
        uh, rs = _rms(h)
        wm_v = wm_ref[...]
        u = _bf(uh * wm_v)
        parts = ((dg_ref[...], C_GQ, 1536), (dq_ref[...], C_SQ, 512), (_bf(dk_ref[...]), C_SK, 128),
                 (_bf(dv_ref[...]), C_SV, 128), (dlr_ref[...], C_LR, 128))
        du = jnp.zeros((TM, D), F32)
        for val, c0, w in parts:
            du = du + _dot_nt(val, w_ref[:, c0:c0 + w])
            dw_ref[:, c0:c0 + w] += _dot_tn(u, val)
        gmix_ref[...] += jnp.sum(du * uh, axis=0, keepdims=True)
        dh0 = dh1_ref[...] + _rms_bwd(du, uh, rs, wm_v)

        @pl.when(i < nblk)
        def _():
            gx_ref[...] = dh0

        @pl.when(i == nblk)
        def _():
            gmeta_ref[...] = dh0[:NM]

    xs = pl.BlockSpec((TM, D), lambda i: (jnp.minimum(i, nblk - 1), 0))
    rs_ = lambda w: pl.BlockSpec((TM, w), lambda i: (i, 0))
    return _call(
        body, "proj_bwd", (nblk + 1,),
        [xs, VMEM_SPEC, VMEM_SPEC, VMEM_SPEC, rs_(1536), rs_(512), rs_(128), rs_(128), rs_(128), rs_(D)],
        [xs, pl.BlockSpec((NM, D), lambda i: (0, 0)), pl.BlockSpec((D, DINP), lambda i: (0, 0)),
         pl.BlockSpec((1, D), lambda i: (0, 0))],
        [jax.ShapeDtypeStruct((t, D), F32), jax.ShapeDtypeStruct((NM, D), F32),
         jax.ShapeDtypeStruct((D, DINP), F32), jax.ShapeDtypeStruct((1, D), F32)],
        [], _cp(56), (x, metapad, wm, winp, dgla, dswa_q, dsk, dsv, dlr, dh1), comm)


def _permute_w_in(w):
    return jnp.concatenate([w[:, 0:1536], w[:, 1552:2320], w[:, 1536:1552], jnp.zeros((D, DINP - DIN), w.dtype)], axis=1)


def _unpermute_dw_in(dw):
    return jnp.concatenate([dw[:, 0:1536], dw[:, C_LR:C_LR + 16], dw[:, 1536:C_LR]], axis=1)


def _place():
    return lax.axis_index("x"), lax.axis_index("y"), lax.axis_index("c")


def _other_chips(x, y):
    return [(1 - x, y), (x, 1 - y), (1 - x, 1 - y)]


def _dma_sems(*counts):
    return tuple(pltpu.SemaphoreType.DMA((k,)) for k in counts)


def _gather_shards(shards):
    n = len(shards)

    def plan(ins, outs, sems):
        send, recv, loc = sems
        x, y, c = _place()
        chips = _other_chips(x, y)

        def remote(a, k, shard_of):
            tx, ty = chips[k]
            sx, sy = shard_of
            return pltpu.make_async_remote_copy(
                src_ref=ins[a], dst_ref=outs[a].at[2 * sx + sy], send_sem=send.at[3 * a + k], recv_sem=recv.at[3 * a + k],
                device_id=(tx, ty, c), device_id_type=MESH)

        local = [pltpu.make_async_copy(ins[a], outs[a].at[2 * x + y], loc.at[a]) for a in range(n)]
        sends = [remote(a, k, (x, y)) for a in range(n) for k in range(3)]
        waits = ([lambda a=a, k=k: remote(a, k, chips[k]).wait_recv() for a in range(n) for k in range(3)]
                 + [cp.wait_send for cp in sends] + [cp.wait for cp in local])
        return local + sends, waits

    return _Comm(tuple(shards), tuple(jax.ShapeDtypeStruct((4,) + s.shape, s.dtype) for s in shards),
                 _dma_sems(3 * n, 3 * n, n), plan)


def _swap_halves(grads):
    n = len(grads)

    def plan(ins, outs, sems):
        send, recv = sems
        x, y, c = _place()
        cps = [pltpu.make_async_remote_copy(
            src_ref=ins[a].at[1 - c], dst_ref=outs[a], send_sem=send.at[a], recv_sem=recv.at[a],
            device_id=(x, y, 1 - c), device_id_type=MESH) for a in range(n)]
        return cps, [cp.wait for cp in cps]

    return _Comm(tuple(grads), tuple(jax.ShapeDtypeStruct(g.shape[1:], g.dtype) for g in grads), _dma_sems(n, n), plan)


def _scatter_shards(parts):
    n = len(parts)

    def plan(ins, outs, sems):
        send, recv = sems
        x, y, c = _place()
        cps = [pltpu.make_async_remote_copy(
            src_ref=ins[a].at[2 * tx + ty], dst_ref=outs[a].at[k], send_sem=send.at[3 * a + k],
            recv_sem=recv.at[3 * a + k], device_id=(tx, ty, c), device_id_type=MESH)
            for a in range(n) for k, (tx, ty) in enumerate(_other_chips(x, y))]
        return cps, [cp.wait for cp in cps]

    return _Comm(tuple(parts), tuple(jax.ShapeDtypeStruct((3,) + p.shape[1:], p.dtype) for p in parts),
                 _dma_sems(3 * n, 3 * n), plan)


def _join_halves(halves):
    n = len(halves)

    def plan(ins, outs, sems):
        send, recv, loc = sems
        x, y, c = _place()

        def remote(a, half):
            return pltpu.make_async_remote_copy(
                src_ref=ins[a], dst_ref=outs[a].at[half], send_sem=send.at[a], recv_sem=recv.at[a],
                device_id=(x, y, 1 - c), device_id_type=MESH)

        local = [pltpu.make_async_copy(ins[a], outs[a].at[c], loc.at[a]) for a in range(n)]
        sends = [remote(a, c) for a in range(n)]
        waits = ([lambda a=a: remote(a, 1 - c).wait_recv() for a in range(n)] + [cp.wait_send for cp in sends]
                 + [cp.wait for cp in local])
        return local + sends, waits

    return _Comm(tuple(halves), tuple(jax.ShapeDtypeStruct((2,) + h.shape, h.dtype) for h in halves),
                 _dma_sems(n, n, n), plan)


def _reduce_w_in(parts, parts_bf):
    rows, cols = parts.shape[1:]

    def body(parts_ref, bf_ref, out_ref, own, rbuf, qbuf, sib, send, recv, loc):
        x, y, c = _place()
        mine = pltpu.make_async_copy(parts_ref.at[2 * x + y], own, loc.at[0])
        mine.start()
        cps = [pltpu.make_async_remote_copy(
            src_ref=bf_ref.at[2 * tx + ty], dst_ref=rbuf.at[k], send_sem=send.at[k], recv_sem=recv.at[k],
            device_id=(tx, ty, c), device_id_type=MESH) for k, (tx, ty) in enumerate(_other_chips(x, y))]
        for cp in cps:
            cp.start()
        mine.wait()
        for cp in cps:
            cp.wait()
        qbuf[...] = own[...] + rbuf[0].astype(F32) + rbuf[1].astype(F32) + rbuf[2].astype(F32)
        swap = pltpu.make_async_remote_copy(src_ref=qbuf, dst_ref=sib, send_sem=send.at[3], recv_sem=recv.at[3],
                                            device_id=(x, y, 1 - c), device_id_type=MESH)
        swap.start()
        swap.wait()
        out_ref[...] = qbuf[...] + sib[...]

    return pl.pallas_call(
        body, name="reduce_w_in",
        in_specs=[ANY_SPEC, ANY_SPEC], out_specs=VMEM_SPEC, out_shape=jax.ShapeDtypeStruct((rows, cols), F32),
        scratch_shapes=[pltpu.VMEM((rows, cols), F32), pltpu.VMEM((3, rows, cols), BF16), pltpu.VMEM((rows, cols), F32),
                        pltpu.VMEM((rows, cols), F32), *_dma_sems(4, 4, 1)],
        compiler_params=pltpu.CompilerParams(vmem_limit_bytes=40 << 20),
    )(parts, parts_bf)


def _allreduce_small(pack):
    p = pack.shape[0]

    def body(in_ref, out_ref, buf, send, recv):
        x, y, c = _place()
        me = 4 * x + 2 * y + c
        buf[me] = in_ref[...]
        sends = []
        for k in range(1, 8):
            tx, ty, tc = x ^ (k >> 2), y ^ ((k >> 1) & 1), c ^ (k & 1)
            sends.append(pltpu.make_async_remote_copy(
                src_ref=in_ref, dst_ref=buf.at[me], send_sem=send.at[k - 1], recv_sem=recv.at[k - 1],
                device_id=(tx, ty, tc), device_id_type=MESH))
        for cp in sends:
            cp.start()
        for k in range(1, 8):
            peer = 4 * (x ^ (k >> 2)) + 2 * (y ^ ((k >> 1) & 1)) + (c ^ (k & 1))
            pltpu.make_async_remote_copy(
                src_ref=in_ref, dst_ref=buf.at[peer], send_sem=send.at[k - 1], recv_sem=recv.at[k - 1],
                device_id=(x, y, c), device_id_type=MESH).wait_recv()
        for cp in sends:
            cp.wait_send()
        acc = buf[0]
        for d in range(1, 8):
            acc = acc + buf[d]
        out_ref[...] = acc

    return pl.pallas_call(
        body, name="allreduce_small",
        in_specs=[VMEM_SPEC], out_specs=VMEM_SPEC, out_shape=jax.ShapeDtypeStruct(pack.shape, F32),
        scratch_shapes=[pltpu.VMEM((8, p, D), F32), pltpu.SemaphoreType.DMA((7,)), pltpu.SemaphoreType.DMA((7,))],
    )(pack)


GRID4 = 4


def _sum_parts(sel, firsts, others, name, also_bf16):
    n = len(firsts)
    nk = others[0].shape[0]

    def body(sel_ref, *refs):
        fs, os_, outs = refs[:n], refs[n:2 * n], refs[2 * n:]
        for a in range(n):
            acc = fs[a][0]
            for k in range(nk):
                acc = acc + os_[a][k].astype(F32)
            outs[a][...] = acc
            if also_bf16:
                outs[n + a][...] = _bf(acc)

    def rows(a):
        return firsts[a].shape[1] // GRID4

    in_specs = ([pl.BlockSpec((1, rows(a), firsts[a].shape[2]), lambda i, s: (s[0], i, 0)) for a in range(n)]
                + [pl.BlockSpec((nk, rows(a), firsts[a].shape[2]), lambda i, s: (0, i, 0)) for a in range(n)])
    out_specs = [pl.BlockSpec((rows(a), firsts[a].shape[2]), lambda i, s: (i, 0)) for a in range(n)]
    out_shape = [jax.ShapeDtypeStruct(f.shape[1:], F32) for f in firsts]
    if also_bf16:
        out_specs = out_specs * 2
        out_shape = out_shape + [jax.ShapeDtypeStruct(f.shape[1:], BF16) for f in firsts]
    outs = pl.pallas_call(
        body, name=name,
        grid_spec=pltpu.PrefetchScalarGridSpec(num_scalar_prefetch=1, grid=(GRID4,), in_specs=in_specs, out_specs=out_specs),
        out_shape=out_shape, compiler_params=_cp(48),
    )(sel, *firsts, *others)
    return outs[:n], outs[n:]


def _adamw_math(w, g, m, v):
    m2 = ADAM_B1 * m + (1.0 - ADAM_B1) * g
    v2 = ADAM_B2 * v + (1.0 - ADAM_B2) * (g * g)
    m_hat = m2 / (1.0 - ADAM_B1 ** ADAM_STEP)
    v_hat = v2 / (1.0 - ADAM_B2 ** ADAM_STEP)
    return -ADAM_LR * (m_hat / (jnp.sqrt(v_hat) + ADAM_EPS) + ADAM_WD * w), m2, v2


def _adamw_big(ws, gs, ms, vs):
    n = len(ws)

    def body(*refs):
        for a in range(n):
            d, m2, v2 = _adamw_math(refs[a][...], refs[n + a][...], refs[2 * n + a][...], refs[3 * n + a][...])
            refs[4 * n + a][...] = d
            refs[5 * n + a][...] = m2
            refs[6 * n + a][...] = v2

    specs = [pl.BlockSpec((w.shape[0] // GRID4, w.shape[1]), lambda i: (i, 0)) for w in ws]
    return pl.pallas_call(
        body, name="adamw_big", grid=(GRID4,),
        in_specs=specs * 4, out_specs=specs * 3,
        out_shape=[jax.ShapeDtypeStruct(w.shape, F32) for w in ws] * 3,
        compiler_params=_cp(48),
    )(*ws, *gs, *ms, *vs)


def _adamw_small(ws, gs, ms, vs):
    n = len(ws)

    def body(*refs):
        for a in range(n):
            d, m2, v2 = _adamw_math(refs[a][...], refs[n + a][...], refs[2 * n + a][...], refs[3 * n + a][...])
            refs[4 * n + a][...] = d
            refs[5 * n + a][...] = m2
            refs[6 * n + a][...] = v2

    return pl.pallas_call(
        body, name="adamw_small",
        in_specs=[VMEM_SPEC] * (4 * n), out_specs=[VMEM_SPEC] * (3 * n),
        out_shape=[jax.ShapeDtypeStruct(w.shape, F32) for w in ws] * 3,
    )(*ws, *gs, *ms, *vs)


def kernel(x, meta_tokens, norm_mix_w, w_in, w_gate_up, b_gate, gla_norm_w, sinks, w_out, norm_ff_w, w_ff1, w_ff2, final_norm_w, loss_target, m_meta_tokens, m_norm_mix_w, m_w_in, m_w_gate_up, m_b_gate, m_gla_norm_w, m_sinks, m_w_out, m_norm_ff_w, m_w_ff1, m_w_ff2, m_final_norm_w, v_meta_tokens, v_norm_mix_w, v_w_in, v_w_gate_up, v_b_gate, v_gla_norm_w, v_sinks, v_w_out, v_norm_ff_w, v_w_ff1, v_w_ff2, v_final_norm_w):
    xi, yi, ci = _place()
    shard = (2 * xi + yi).astype(jnp.int32).reshape(1)
    core = ci.astype(jnp.int32).reshape(1)

    small = jnp.concatenate([meta_tokens, w_gate_up[0], jnp.zeros((NM, 64), F32)], axis=1)
    g_in, g_small = _run_comm(_gather_shards([_bf(w_in[0]), small]), "gather_w_in")
    winp = _permute_w_in(g_in.transpose(1, 0, 2).reshape(D, DIN))
    meta = g_small[:, :, 0:256].transpose(1, 0, 2).reshape(NM, D)
    wgu = g_small[:, :, 256:320].transpose(1, 0, 2).reshape(NM, 256)

    xs, tgt = x[0], loss_target[0]
    t = xs.shape[0]
    wfin = final_norm_w.reshape(1, D)
    metapad = jnp.concatenate([meta, jnp.zeros((TM - NM, D), F32)], axis=0)
    wgu_p = _bf(jnp.concatenate([wgu, jnp.zeros((128 - 16, 256), F32)], axis=0))
    tabs = _rope_tables(t)

    proj = _proj_fwd(xs, metapad, norm_mix_w, winp)
    (oswa, lse), (w1, w2) = _swa_fwd(proj, tabs, sinks, t, _gather_shards([_bf(w_ff1[0]), _bf(w_ff2[0])]))
    (ogla, oraw, sst), (g_out,) = _gla_fwd(proj, wgu_p, b_gate, gla_norm_w, t, _gather_shards([_bf(w_out[0])]))
    wo = g_out.reshape(D, D)
    h1, f, a, dh2, loss, gfin = _mlp_fwd(xs, metapad, tgt, ogla, oswa, wo, norm_ff_w, w1, w2, wfin)

    da, dh2b, dh1, do, dwo, gff = _mlp_bwd(h1, a, dh2, ogla, oswa, wo, norm_ff_w, w1, w2)
    dw1, dw2 = _ffn_wgrad(f, a, da, dh2b)
    big = [dwo.reshape(4, 2, 128, D).transpose(1, 0, 2, 3), dw1, dw2]
    (dsq, dsk, dsv, dsink), theirs = _swa_bwd(proj, tabs, sinks, lse, do, t, _swap_halves(big))
    sums, sums_bf = _sum_parts(core, [b.reshape((2, -1) + b.shape[3:]) for b in big],
                               [s.reshape((1, -1) + s.shape[2:]) for s in theirs], "sum_cores", True)
    sums = [s.reshape(b.shape[1:]) for s, b in zip(sums, big)]
    sums_bf = [s.reshape(b.shape[1:]) for s, b in zip(sums_bf, big)]
    (dgla, dlr, dwgu, dbg, dgnw), arrived = _gla_bwd(proj, oraw, sst, do, wgu_p, b_gate, gla_norm_w, t,
                                                     _scatter_shards(sums_bf))
    halves, _ = _sum_parts(shard, sums, arrived, "sum_chips", False)
    (gx, gmeta, dwinp, gmix), joined = _proj_bwd(xs, metapad, norm_mix_w, winp, dgla, dsq, dsk, dsv, dlr, dh1,
                                                 _join_halves(halves))
    gw_out, gw_1, gw_2 = [j.reshape((-1, j.shape[2])) for j in joined]
    dwin = _unpermute_dw_in(dwinp).reshape(D, 4, 580).transpose(1, 0, 2)
    gw_in = _reduce_w_in(dwin, _bf(dwin))
    g = dict(meta=gmeta, mix=gmix, wgu=dwgu[:16], bg=dbg, gnw=dgnw, sinks=dsink, ff=gff, fin=gfin)

    tail = jnp.concatenate([g["bg"], g["gnw"], g["sinks"], loss, jnp.zeros((1, D - 256 - 128 - 8 - 1), F32)], axis=1)
    pack = jnp.concatenate([g["meta"], g["mix"], g["ff"], g["fin"], tail, g["wgu"].reshape(4, D)], axis=0)
    tot = _allreduce_small(pack)
    g_meta = lax.dynamic_slice_in_dim(tot[0:NM], shard[0] * 256, 256, axis=1)
    g_mix, g_ff, g_fin = tot[16:17], tot[17:18], tot[18]
    g_bg, g_gnw, g_sinks, loss_tot = tot[19:20, 0:256], tot[19:20, 256:384], tot[19:20, 384:392], tot[19, 392]
    g_wgu = lax.dynamic_slice_in_dim(tot[20:24].reshape(NM, 256), shard[0] * 64, 64, axis=1)

    bw = [w_in[0], w_out[0], w_ff1[0], w_ff2[0]]
    bg_ = [gw_in, gw_out, gw_1, gw_2]
    bm = [m_w_in[0], m_w_out[0], m_w_ff1[0], m_w_ff2[0]]
    bv = [v_w_in[0], v_w_out[0], v_w_ff1[0], v_w_ff2[0]]
    bo = _adamw_big(bw, bg_, bm, bv)
    d_in, d_out, d_1, d_2 = [o[None] for o in bo[0:4]]
    m_in, m_out, m_1, m_2 = [o[None] for o in bo[4:8]]
    v_in, v_out, v_1, v_2 = [o[None] for o in bo[8:12]]

    fin2 = lambda a: a.reshape(1, D)
    sw = [meta_tokens, norm_mix_w, w_gate_up[0], b_gate, gla_norm_w, sinks, norm_ff_w, fin2(final_norm_w)]
    sg = [g_meta, g_mix, g_wgu, g_bg, g_gnw, g_sinks, g_ff, fin2(g_fin)]
    sm = [m_meta_tokens, m_norm_mix_w, m_w_gate_up[0], m_b_gate, m_gla_norm_w, m_sinks, m_norm_ff_w, fin2(m_final_norm_w)]
    sv = [v_meta_tokens, v_norm_mix_w, v_w_gate_up[0], v_b_gate, v_gla_norm_w, v_sinks, v_norm_ff_w, fin2(v_final_norm_w)]
    so = _adamw_small(sw, sg, sm, sv)

    def small_outs(o):
        meta_, mix_, wgu_, bg__, gnw_, sinks_, ff_, fin_ = o
        return dict(meta_tokens=meta_, norm_mix_w=mix_, w_gate_up=wgu_[None], b_gate=bg__, gla_norm_w=gnw_, sinks=sinks_,
                    norm_ff_w=ff_, final_norm_w=fin_.reshape(D))

    def ordered(small_d, w_in_, w_out_, w_1_, w_2_):
        return (small_d["meta_tokens"], small_d["norm_mix_w"], w_in_, small_d["w_gate_up"], small_d["b_gate"],
                small_d["gla_norm_w"], small_d["sinks"], w_out_, small_d["norm_ff_w"], w_1_, w_2_, small_d["final_norm_w"])

    grads = ordered(small_outs(sg), gw_in[None], gw_out[None], gw_1[None], gw_2[None])
    deltas = ordered(small_outs(so[0:8]), d_in, d_out, d_1, d_2)
    new_m = ordered(small_outs(so[8:16]), m_in, m_out, m_1, m_2)
    new_v = ordered(small_outs(so[16:24]), v_in, v_out, v_1, v_2)
    return (loss_tot, gx[None], *grads, *deltas, *new_m, *new_v)
```

```python
from typing import Callable, NamedTuple

import jax
import jax.numpy as jnp
import numpy as np
from jax import lax
from jax.experimental import pallas as pl
from jax.experimental.pallas import tpu as pltpu

F32 = jnp.float32
BF16 = jnp.bfloat16

D = 1024
DFF = 4096
NM = 16
TM = 256
CH = 64
SB = 128
EPS = 1e-5
C_GQ, C_GK, C_GV, C_GR, C_SQ, C_SK, C_SV, C_LR, DINP = 0, 256, 512, 1024, 1536, 2048, 2176, 2304, 2432
DIN = 2320
R_LR = 1536
ROPE_THETA = 500000.0
ADAM_LR, ADAM_B1, ADAM_B2, ADAM_EPS, ADAM_WD, ADAM_STEP = 0.001, 0.9, 0.999, 1e-08, 0.01, 10
NEG = -1e30
MESH = pl.DeviceIdType.MESH
VMEM_SPEC = pl.BlockSpec(memory_space=pltpu.VMEM)
ANY_SPEC = pl.BlockSpec(memory_space=pl.ANY)
SMEM_SPEC = pl.BlockSpec(memory_space=pltpu.SMEM)


def _cp(vmem_mb, sem=("arbitrary",)):
    return pltpu.CompilerParams(dimension_semantics=sem, vmem_limit_bytes=vmem_mb << 20)


def _dot(a, b):
    return jnp.dot(a, b, preferred_element_type=F32)


def _dot_nt(a, b):
    return lax.dot_general(a, b, (((1,), (1,)), ((), ())), preferred_element_type=F32)


def _dot_tn(a, b):
    return lax.dot_general(a, b, (((0,), (0,)), ((), ())), preferred_element_type=F32)


def _bf(x):
    return x.astype(BF16)


def _dot3(m01, x):
    x1 = _bf(x)
    r1 = x - x1.astype(F32)
    x2 = _bf(r1)
    x3 = _bf(r1 - x2.astype(F32))
    return _dot(m01, x1) + _dot(m01, x2) + _dot(m01, x3)


def _rms(h):
    rs = lax.rsqrt(jnp.mean(h * h, axis=-1, keepdims=True) + EPS)
    return h * rs, rs


def _rms_bwd(dy, yhat, rs, w):
    dyh = dy * w
    return rs * (dyh - yhat * jnp.mean(dyh * yhat, axis=-1, keepdims=True))


class _Comm(NamedTuple):
    ins: tuple
    outs: tuple
    sems: tuple
    plan: Callable


def _call(body, name, grid, in_specs, out_specs, out_shape, scratch, params, args, comm=None):
    if comm is None:
        outs = pl.pallas_call(body, name=name, grid=grid, in_specs=in_specs, out_specs=out_specs, out_shape=out_shape,
                              scratch_shapes=scratch, compiler_params=params)(*args)
        return outs, None
    n_in, n_out, n_scr = len(in_specs), len(out_specs), len(scratch)
    ci, co = len(comm.ins), len(comm.outs)
    last = grid[0] - 1

    def wrapped(*refs):
        own_in, c_in = refs[:n_in], refs[n_in:n_in + ci]
        refs = refs[n_in + ci:]
        own_out, c_out = refs[:n_out], refs[n_out:n_out + co]
        refs = refs[n_out + co:]
        own_scr, c_sem = refs[:n_scr], refs[n_scr:]
        i = pl.program_id(0)

        @pl.when(i == 0)
        def _():
            for cp in comm.plan(c_in, c_out, c_sem)[0]:
                cp.start()

        body(*own_in, *own_out, *own_scr)

        @pl.when(i == last)
        def _():
            for wait in comm.plan(c_in, c_out, c_sem)[1]:
                wait()

    outs = pl.pallas_call(
        wrapped, name=name, grid=grid, in_specs=list(in_specs) + [ANY_SPEC] * ci, out_specs=list(out_specs) + [ANY_SPEC] * co,
        out_shape=list(out_shape) + list(comm.outs), scratch_shapes=list(scratch) + list(comm.sems), compiler_params=params,
    )(*args, *comm.ins)
    return outs[:n_out], outs[n_out:]


def _run_comm(comm, name):
    ci, co = len(comm.ins), len(comm.outs)

    def body(*refs):
        starts, waits = comm.plan(refs[:ci], refs[ci:ci + co], refs[ci + co:])
        for cp in starts:
            cp.start()
        for wait in waits:
            wait()

    return pl.pallas_call(body, name=name, in_specs=[ANY_SPEC] * ci, out_specs=[ANY_SPEC] * co, out_shape=list(comm.outs),
                          scratch_shapes=list(comm.sems))(*comm.ins)


def _proj_fwd(x, metapad, wm, wt):
    t = x.shape[0]
    nblk = t // TM

    def body(x_ref, mp_ref, wm_ref, w_ref, proj_ref):
        i = pl.program_id(0)
        h = jnp.where(i == nblk, mp_ref[...], x_ref[...])
        u, _ = _rms(h)
        ub = _bf(u * wm_ref[...])
        proj_ref[:, 0:C_SQ] = _dot_nt(ub, w_ref[0:R_LR, :])
        proj_ref[:, C_SQ:C_LR] = _dot_nt(ub, w_ref[R_LR + 16:DIN, :])
        proj_ref[:, C_LR:DINP] = jnp.zeros((TM, DINP - C_LR), F32)
        proj_ref[:, C_LR:C_LR + 16] = _dot_nt(ub, w_ref[R_LR:R_LR + 16, :])

    return pl.pallas_call(
        body, name="proj_fwd", grid=(nblk + 1,),
        in_specs=[pl.BlockSpec((TM, D), lambda i: (jnp.minimum(i, nblk - 1), 0)), VMEM_SPEC, VMEM_SPEC, VMEM_SPEC],
        out_specs=pl.BlockSpec((TM, DINP), lambda i: (i, 0)),
        out_shape=jax.ShapeDtypeStruct((t + TM, DINP), F32),
        compiler_params=_cp(40),
    )(x, metapad, wm, wt)


def _chunk_masks():
    r = lax.broadcasted_iota(jnp.int32, (TM, TM), 0)
    c = lax.broadcasted_iota(jnp.int32, (TM, TM), 1)
    same = (r // CH) == (c // CH)
    lower = _bf(jnp.where(same & (c <= r), 1.0, 0.0))
    upper = _bf(jnp.where(same & (c >= r), 1.0, 0.0))
    return lower, upper


def _gla_gate(lr, wgu, bg, valid, lower):
    z = _dot(_bf(lr), wgu) + bg
    g = (jnp.minimum(z, 0.0) - jnp.log(1.0 + jnp.exp(-jnp.abs(z)))) * (1.0 / 16.0)
    g = jnp.where(valid, g, 0.0)
    return z, _dot3(lower, g)


def _gla_decays(q, k, b):
    nc = TM // CH
    b3 = b.reshape(nc, CH, 256)
    blast = b3[:, CH - 1:CH, :]
    eb = jnp.exp(b)
    enb = jnp.exp(-b)
    ebl = jnp.exp(blast - b3).reshape(TM, 256)
    return eb, enb, ebl, jnp.exp(blast)


def _tri(lower_incl):
    r = lax.broadcasted_iota(jnp.int32, (CH, CH), 0)
    c = lax.broadcasted_iota(jnp.int32, (CH, CH), 1)
    return ((c <= r) if lower_incl else (c >= r))[None]


def _gla_fwd(proj, wgu, bg, gnw, t, comm=None):
    nblk = t // TM
    nt = nblk + 1
    nc = TM // CH

    def blk(i):
        return (i + nblk) % nt

    def body(q_ref, k_ref, v_ref, r_ref, lr_ref, wgu_ref, bg_ref, gnw_ref, o_ref, oraw_ref, sst_ref, st_scr):
        i = pl.program_id(0)

        @pl.when(i == 0)
        def _():
            st_scr[...] = jnp.zeros_like(st_scr)

        rows = blk(i) * TM + lax.broadcasted_iota(jnp.int32, (TM, 1), 0)
        lower, _ = _chunk_masks()
        _, b = _gla_gate(lr_ref[...], wgu_ref[...], bg_ref[...], rows < t + NM, lower)
        q = q_ref[...]
        k = k_ref[...]
        eb, enb, ebl, eblast = _gla_decays(q, k, b)
        qt = q * 0.125 * eb
        kt = k * enb
        kh = k * ebl
        tril = _tri(True)
        outs = []
        for h in range(4):
            hs = slice(h * CH, (h + 1) * CH)
            qh = _bf(qt[:, hs]).reshape(nc, CH, CH)
            kth = _bf(kt[:, hs]).reshape(nc, CH, CH)
            khh = _bf(kh[:, hs]).reshape(nc, CH, CH)
            vh = _bf(v_ref[:, h * 128:(h + 1) * 128]).reshape(nc, CH, 128)
            a = jnp.einsum('cid,cjd->cij', qh, kth, preferred_element_type=F32)
            a = jnp.where(tril, a, 0.0)
            o = jnp.einsum('cij,cjv->civ', _bf(a), vh, preferred_element_type=F32)
            kv = jnp.einsum('cjv,cjd->cvd', vh, khh, preferred_element_type=F32)
            st = st_scr[h]
            o_inter = []
            for c in range(nc):
                sst_ref[c, h] = st
                o_inter.append(_dot_nt(qh[c], _bf(st)))
                st = st * eblast[c, :, hs] + kv[c]
            st_scr[h] = st
            outs.append((o + jnp.stack(o_inter)).reshape(TM, 128))
        oraw = jnp.concatenate(outs, axis=1)
        oraw_ref[...] = oraw
        gn = gnw_ref[...]
        res = []
        for h in range(4):
            on, _ = _rms(oraw[:, h * 128:(h + 1) * 128])
            r = r_ref[:, h * 128:(h + 1) * 128]
            res.append(on * gn * (r * jax.nn.sigmoid(r)))
        o_ref[...] = _bf(jnp.concatenate(res, axis=1))

    def spec(w, cb):
        return pl.BlockSpec((TM, w), lambda i: (blk(i), cb))

    return _call(
        body, "gla_fwd", (nt,),
        [spec(256, 0), spec(256, 1), spec(512, 1), spec(512, 2), spec(128, C_LR // 128), VMEM_SPEC, VMEM_SPEC, VMEM_SPEC],
        [spec(512, 0), spec(512, 0), pl.BlockSpec((nc, 4, 128, CH), lambda i: (blk(i), 0, 0, 0))],
        [jax.ShapeDtypeStruct((t + TM, 512), BF16), jax.ShapeDtypeStruct((t + TM, 512), F32),
         jax.ShapeDtypeStruct((nt * nc, 4, 128, CH), F32)],
        [pltpu.VMEM((4, 128, CH), F32)], _cp(40), (proj, proj, proj, proj, proj, wgu, bg, gnw), comm)


def _gla_bwd(proj, oraw, sst, do, wgu, bg, gnw, t, comm=None):
    nblk = t // TM
    nt = nblk + 1
    nc = TM // CH

    def blk(i):
        return (2 * nblk - i) % nt

    def body(q_ref, k_ref, v_ref, r_ref, lr_ref, oraw_ref, sst_ref, do_ref, wgu_ref, bg_ref, gnw_ref,
             dgla_ref, dlr_ref, dwgu_ref, dbg_ref, dgnw_ref, dst_scr):
        i = pl.program_id(0)

        @pl.when(i == 0)
        def _():
            dst_scr[...] = jnp.zeros_like(dst_scr)
            dwgu_ref[...] = jnp.zeros_like(dwgu_ref)
            dbg_ref[...] = jnp.zeros_like(dbg_ref)
            dgnw_ref[...] = jnp.zeros_like(dgnw_ref)

        rows = blk(i) * TM + lax.broadcasted_iota(jnp.int32, (TM, 1), 0)
        valid = rows < t + NM
        lower, upper = _chunk_masks()
        lr = lr_ref[...]
        z, b = _gla_gate(lr, wgu_ref[...], bg_ref[...], valid, lower)
        q = q_ref[...]
        k = k_ref[...]
        eb, enb, ebl, eblast = _gla_decays(q, k, b)
        qt = q * 0.125 * eb
        kt = k * enb
        kh = k * ebl
        gn = gnw_ref[...]
        tril = _tri(True)
        triu = _tri(False)
        dq_l, dk_l, dv_l, dr_l, db_l, ex_l = [], [], [], [], [], []
        dgn = jnp.zeros((1, 128), F32)
        for h in range(4):
            hs = slice(h * CH, (h + 1) * CH)
            vs = slice(h * 128, (h + 1) * 128)
            on, rs = _rms(oraw_ref[:, vs])
            r = r_ref[:, vs]
            sig = jax.nn.sigmoid(r)
            sil = r * sig
            dy = do_ref[:, vs]
            dr_l.append(dy * on * gn * (sig * (1.0 + r * (1.0 - sig))))
            dgn = dgn + jnp.sum(dy * sil * on, axis=0, keepdims=True)
            doraw = _rms_bwd(dy * sil, on, rs, gn)
            qtf = qt[:, hs].reshape(nc, CH, CH)
            ktf = kt[:, hs].reshape(nc, CH, CH)
            khf = kh[:, hs].reshape(nc, CH, CH)
            qh, kth, khh = _bf(qtf), _bf(ktf), _bf(khf)
            vh = _bf(v_ref[:, vs]).reshape(nc, CH, 128)
            doh = _bf(doraw).reshape(nc, CH, 128)
            at = jnp.where(triu, jnp.einsum('cjd,cid->cji', kth, qh, preferred_element_type=F32), 0.0)
            da = jnp.where(tril, jnp.einsum('civ,cjv->cij', doh, vh, preferred_element_type=F32), 0.0)
            dat = jnp.where(triu, jnp.einsum('cjv,civ->cji', vh, doh, preferred_element_type=F32), 0.0)
            dv = jnp.einsum('cji,civ->cjv', _bf(at), doh, preferred_element_type=F32)
            dqt = jnp.einsum('cij,cjd->cid', _bf(da), kth, preferred_element_type=F32)
            dkt = jnp.einsum('cji,cid->cjd', _bf(dat), qh, preferred_element_type=F32)
            gq = jnp.einsum('civ,cid->cvd', doh, qh, preferred_element_type=F32)
            dst = dst_scr[h]
            dsend = [None] * nc
            for c in reversed(range(nc)):
                dsend[c] = dst
                dst = dst * eblast[c, :, hs] + gq[c]
            dst_scr[h] = dst
            dse = jnp.stack(dsend)
            dseb = _bf(dse)
            stf = sst_ref[:, h]
            dqt = dqt + jnp.einsum('civ,cvd->cid', doh, _bf(stf), preferred_element_type=F32)
            dv = dv + jnp.einsum('cjd,cvd->cjv', khh, dseb, preferred_element_type=F32)
            dkh = jnp.einsum('cjv,cvd->cjd', vh, dseb, preferred_element_type=F32)
            extra = (jnp.sum(dkh * khf, axis=1, keepdims=True)
                     + eblast[:, :, hs] * jnp.sum(dse * stf, axis=1, keepdims=True))
            db_l.append((dqt * qtf - dkt * ktf - dkh * khf).reshape(TM, CH))
            ex_l.append(jnp.broadcast_to(extra, (nc, CH, CH)).reshape(TM, CH))
            dq_l.append((dqt.reshape(TM, CH)) * eb[:, hs] * 0.125)
            dk_l.append(dkt.reshape(TM, CH) * enb[:, hs] + dkh.reshape(TM, CH) * ebl[:, hs])
            dv_l.append(dv.reshape(TM, 128))
        dgnw_ref[...] += dgn
        db = jnp.concatenate(db_l, axis=1)
        dg = _dot3(upper, db) + jnp.concatenate(ex_l, axis=1)
        dz = jnp.where(valid, dg * (1.0 / 16.0) / (1.0 + jnp.exp(z)), 0.0)
        dzb = _bf(dz)
        dlr_ref[...] = _bf(_dot_nt(dzb, wgu_ref[...]))
        dwgu_ref[...] += _dot_tn(_bf(lr), dzb)
        dbg_ref[...] += jnp.sum(dz, axis=0, keepdims=True)
        dgla_ref[...] = _bf(jnp.concatenate(dq_l + dk_l + dv_l + dr_l, axis=1))

    def spec(w, cb):
        return pl.BlockSpec((TM, w), lambda i: (blk(i), cb))

    def acc(shape):
        return pl.BlockSpec(shape, lambda i: (0, 0))

    return _call(
        body, "gla_bwd", (nt,),
        [spec(256, 0), spec(256, 1), spec(512, 1), spec(512, 2), spec(128, C_LR // 128), spec(512, 0),
         pl.BlockSpec((nc, 4, 128, CH), lambda i: (blk(i), 0, 0, 0)), spec(512, 0), VMEM_SPEC, VMEM_SPEC, VMEM_SPEC],
        [spec(1536, 0), spec(128, 0), acc((128, 256)), acc((1, 256)), acc((1, 128))],
        [jax.ShapeDtypeStruct((t + TM, 1536), BF16), jax.ShapeDtypeStruct((t + TM, 128), BF16),
         jax.ShapeDtypeStruct((128, 256), F32), jax.ShapeDtypeStruct((1, 256), F32), jax.ShapeDtypeStruct((1, 128), F32)],
        [pltpu.VMEM((4, 128, CH), F32)], _cp(48), (proj, proj, proj, proj, proj, oraw, sst, do, wgu, bg, gnw), comm)


def _rope_tables(t):
    r = t + TM
    row = np.arange(r)
    pos = np.where(row < t, row + NM, np.where(row < t + NM, row - t, 0)).astype(np.float32)
    inv_freq = (1.0 / (np.float32(ROPE_THETA) ** (np.arange(0, 16, 2, dtype=np.float32) / np.float32(16)))).astype(np.float32)
    ang = (pos[:, None] * inv_freq[None, :]).astype(np.float32)
    cos, sin = np.cos(ang).astype(np.float32), np.sin(ang).astype(np.float32)
    one, zero = np.ones((r, 48), np.float32), np.zeros((r, 48), np.float32)
    return jnp.asarray(np.concatenate([cos, cos, one, -sin, sin, zero], axis=1))


def _rope(x, tab, sign):
    w = x.shape[1]
    rep = w // 64
    c = jnp.concatenate([tab[:, 0:64]] * rep, axis=1)
    s = jnp.concatenate([tab[:, 64:128]] * rep, axis=1)
    lane = lax.rem(lax.broadcasted_iota(jnp.int32, x.shape, 1), 64)
    partner = jnp.where(lane < 8, pltpu.roll(x, w - 8, 1), jnp.where(lane < 16, pltpu.roll(x, 8, 1), 0.0))
    return x * c + sign * (partner * s)


def _stack(x):
    return jnp.concatenate([x[:, g * 64:(g + 1) * 64] for g in range(4)], axis=0)


def _unstack(x):
    return jnp.concatenate([x[g * SB:(g + 1) * SB] for g in range(4)], axis=1)


def _swa_masks(i, nsb):
    r = lax.rem(lax.broadcasted_iota(jnp.int32, (4 * SB, SB), 0), SB)
    c = lax.broadcasted_iota(jnp.int32, (4 * SB, SB), 1)
    real = i < nsb
    return c <= r, (c > r) & (i > 0) & real, real


def _swa_specs(nsb):
    def rows(w, cb, f):
        return pl.BlockSpec((SB, w), lambda i: (f(i), cb))
    cur = lambda i: i
    prev = lambda i: jnp.maximum(i - 1, 0)
    meta = lambda i: nsb
    return rows, cur, prev, meta


def _swa_scores(i, nsb, sink_ref, q_ref, kc_ref, kp_ref, km_ref, tc_ref, tp_ref, tm_ref):
    mc, mp, real = _swa_masks(i, nsb)
    qr = _rope(q_ref[...], tc_ref[...], 1.0) * 0.125
    kc = _rope(kc_ref[...], tc_ref[...], 1.0)
    kp = _rope(kp_ref[...], tp_ref[...], 1.0)
    km = _rope(km_ref[...], tm_ref[...], 1.0)[:NM]
    per_kv = []
    for kv in range(2):
        ks = slice(kv * 64, (kv + 1) * 64)
        qg = _bf(_stack(qr[:, kv * 256:(kv + 1) * 256]))
        kcb, kpb, kmb = _bf(kc[:, ks]), _bf(kp[:, ks]), _bf(km[:, ks])
        s_c = jnp.where(mc, _dot_nt(qg, kcb), NEG)
        s_p = jnp.where(mp, _dot_nt(qg, kpb), NEG)
        s_m = jnp.where(real, _dot_nt(qg, kmb), NEG)
        sink = jnp.concatenate([jnp.full((SB, 1), sink_ref[0, kv * 4 + g], F32) for g in range(4)], axis=0)
        per_kv.append((qg, kcb, kpb, kmb, s_c, s_p, s_m, sink))
    return per_kv


def _swa_fwd(proj, tabs, sinks, t, comm=None):
    nsb = t // SB
    ns = nsb + 2
    rows, cur, prev, meta = _swa_specs(nsb)

    def body(sink_ref, q_ref, kc_ref, kp_ref, km_ref, vc_ref, vp_ref, vm_ref, tc_ref, tp_ref, tm_ref, o_ref, lse_ref):
        i = pl.program_id(0)
        per_kv = _swa_scores(i, nsb, sink_ref, q_ref, kc_ref, kp_ref, km_ref, tc_ref, tp_ref, tm_ref)
        valid = i * SB + lax.broadcasted_iota(jnp.int32, (SB, 1), 0) < t + NM
        o_l, lse_l = [], []
        for kv, (qg, kcb, kpb, kmb, s_c, s_p, s_m, sink) in enumerate(per_kv):
            ks = slice(kv * 64, (kv + 1) * 64)
            m = jnp.maximum(jnp.maximum(jnp.max(s_c, -1, keepdims=True), jnp.max(s_p, -1, keepdims=True)),
                            jnp.maximum(jnp.max(s_m, -1, keepdims=True), sink))
            p_c, p_p, p_m = jnp.exp(s_c - m), jnp.exp(s_p - m), jnp.exp(s_m - m)
            l = (jnp.sum(p_c, -1, keepdims=True) + jnp.sum(p_p, -1, keepdims=True)
                 + jnp.sum(p_m, -1, keepdims=True) + jnp.exp(sink - m))
            inv = 1.0 / l
            o = (_dot(_bf(p_c * inv), _bf(vc_ref[:, ks])) + _dot(_bf(p_p * inv), _bf(vp_ref[:, ks]))
                 + _dot(_bf(p_m * inv), _bf(vm_ref[:NM, ks])))
            o_l.append(_unstack(o))
            lse_l.append(_unstack(m + jnp.log(l)))
        o_ref[...] = _bf(jnp.where(valid, jnp.concatenate(o_l, axis=1), 0.0))
        lse_ref[...] = jnp.concatenate(lse_l, axis=1)

    ck, cv = C_SK // 128, C_SV // 128
    return _call(
        body, "swa_fwd", (ns,),
        [SMEM_SPEC, rows(512, C_SQ // 512, cur),
         rows(128, ck, cur), rows(128, ck, prev), rows(128, ck, meta),
         rows(128, cv, cur), rows(128, cv, prev), rows(128, cv, meta),
         rows(128, 0, cur), rows(128, 0, prev), rows(128, 0, meta)],
        [rows(512, 0, cur), rows(8, 0, cur)],
        [jax.ShapeDtypeStruct((t + TM, 512), BF16), jax.ShapeDtypeStruct((t + TM, 8), F32)],
        [], _cp(32), (sinks, proj, proj, proj, proj, proj, proj, proj, tabs, tabs, tabs), comm)


def _swa_bwd(proj, tabs, sinks, lse, do, t, comm=None):
    nsb = t // SB
    ns = nsb + 2
    r_tot = t + TM
    rows, cur, prev, meta = _swa_specs(nsb)

    def body(sink_ref, q_ref, kc_ref, kp_ref, km_ref, vc_ref, vp_ref, vm_ref, tc_ref, tp_ref, tm_ref, lse_ref, do_ref,
             dq_ref, dk_ref, dv_ref, dsink_ref):
        i = pl.program_id(0)

        @pl.when(i == 0)
        def _():
            dk_ref[...] = jnp.zeros_like(dk_ref)
            dv_ref[...] = jnp.zeros_like(dv_ref)
            dsink_ref[...] = jnp.zeros_like(dsink_ref)

        per_kv = _swa_scores(i, nsb, sink_ref, q_ref, kc_ref, kp_ref, km_ref, tc_ref, tp_ref, tm_ref)
        lse_all = lse_ref[...]
        dq_l, dkc_l, dkp_l, dkm_l, dvc_l, dvp_l, dvm_l, ds_l = [], [], [], [], [], [], [], []
        for kv, (qg, kcb, kpb, kmb, s_c, s_p, s_m, sink) in enumerate(per_kv):
            ks = slice(kv * 64, (kv + 1) * 64)
            lse_g = _stack_cols(lse_all[:, kv * 4:(kv + 1) * 4])
            dog = _bf(_stack(do_ref[:, kv * 256:(kv + 1) * 256]))
            vcb, vpb, vmb = _bf(vc_ref[:, ks]), _bf(vp_ref[:, ks]), _bf(vm_ref[:NM, ks])
            p_c, p_p, p_m = jnp.exp(s_c - lse_g), jnp.exp(s_p - lse_g), jnp.exp(s_m - lse_g)
            dp_c, dp_p, dp_m = _dot_nt(dog, vcb), _dot_nt(dog, vpb), _dot_nt(dog, vmb)
            delta = (jnp.sum(p_c * dp_c, -1, keepdims=True) + jnp.sum(p_p * dp_p, -1, keepdims=True)
                     + jnp.sum(p_m * dp_m, -1, keepdims=True))
            ds_c, ds_p, ds_m = _bf(p_c * (dp_c - delta)), _bf(p_p * (dp_p - delta)), _bf(p_m * (dp_m - delta))
            dq_l.append(_unstack(_dot(ds_c, kcb) + _dot(ds_p, kpb) + _dot(ds_m, kmb)))
            dkc_l.append(_dot_tn(ds_c, qg))
            dkp_l.append(_dot_tn(ds_p, qg))
            dkm_l.append(_dot_tn(ds_m, qg))
            dvc_l.append(_dot_tn(_bf(p_c), dog))
            dvp_l.append(_dot_tn(_bf(p_p), dog))
            dvm_l.append(_dot_tn(_bf(p_m), dog))
            ds_l.append(_unstack(-jnp.exp(sink - lse_g) * delta))
        dq = jnp.concatenate(dq_l, axis=1) * 0.125
        dq_ref[...] = _bf(_rope(dq, tc_ref[...], -1.0))
        c0 = pl.multiple_of(i * SB, SB)
        p0 = pl.multiple_of(jnp.maximum(i - 1, 0) * SB, SB)
        dk_ref[pl.ds(c0, SB), :] += _rope(jnp.concatenate(dkc_l, axis=1), tc_ref[...], -1.0)
        dk_ref[pl.ds(p0, SB), :] += _rope(jnp.concatenate(dkp_l, axis=1), tp_ref[...], -1.0)
        dk_ref[pl.ds(t, NM), :] += _rope(jnp.concatenate(dkm_l, axis=1), tm_ref[:NM, :], -1.0)
        dv_ref[pl.ds(c0, SB), :] += jnp.concatenate(dvc_l, axis=1)
        dv_ref[pl.ds(p0, SB), :] += jnp.concatenate(dvp_l, axis=1)
        dv_ref[pl.ds(t, NM), :] += jnp.concatenate(dvm_l, axis=1)
        dsink_ref[...] += jnp.sum(jnp.concatenate(ds_l, axis=1), axis=0, keepdims=True)

    ck, cv = C_SK // 128, C_SV // 128
    whole = lambda w: pl.BlockSpec((r_tot, w), lambda i: (0, 0))
    return _call(
        body, "swa_bwd", (ns,),
        [SMEM_SPEC, rows(512, C_SQ // 512, cur),
         rows(128, ck, cur), rows(128, ck, prev), rows(128, ck, meta),
         rows(128, cv, cur), rows(128, cv, prev), rows(128, cv, meta),
         rows(128, 0, cur), rows(128, 0, prev), rows(128, 0, meta),
         rows(8, 0, cur), rows(512, 1, cur)],
        [rows(512, 0, cur), whole(128), whole(128), pl.BlockSpec((1, 8), lambda i: (0, 0))],
        [jax.ShapeDtypeStruct((r_tot, 512), BF16), jax.ShapeDtypeStruct((r_tot, 128), F32),
         jax.ShapeDtypeStruct((r_tot, 128), F32), jax.ShapeDtypeStruct((1, 8), F32)],
        [], _cp(48), (sinks, proj, proj, proj, proj, proj, proj, proj, tabs, tabs, tabs, lse, do), comm)


def _stack_cols(x):
    return jnp.concatenate([x[:, g:g + 1] for g in range(4)], axis=0)


def _mlp_fwd(x, metapad, tgt, ogla, oswa, wo, wff, w1, w2, wfin):
    t = x.shape[0]
    nblk = t // TM

    def body(x_ref, mp_ref, tgt_ref, og_ref, os_ref, wo_ref, wff_ref, w1_ref, w2_ref, wfin_ref,
             h1_ref, f_ref, a_ref, dh2_ref, loss_ref, gfin_ref):
        i = pl.program_id(0)

        @pl.when(i == 0)
        def _():
            loss_ref[...] = jnp.zeros_like(loss_ref)
            gfin_ref[...] = jnp.zeros_like(gfin_ref)

        h0 = jnp.where(i == nblk, mp_ref[...], x_ref[...])
        h1 = h0 + _dot(og_ref[...], wo_ref[0:512, :]) + _dot(os_ref[...], wo_ref[512:1024, :])
        h1_ref[...] = h1
        fh, _ = _rms(h1)
        f = _bf(fh * wff_ref[...])
        f_ref[...] = f
        acc = jnp.zeros((TM, D), F32)
        for n in range(4):
            a = _dot(f, w1_ref[n])
            a_ref[:, n * D:(n + 1) * D] = _bf(a)
            zr = jnp.maximum(a, 0.0)
            acc = acc + _dot(_bf(zr * zr), w2_ref[n])
        h2 = h1 + acc
        yh, rs2 = _rms(h2)
        wf = wfin_ref[...]
        real = i < nblk
        e = jnp.where(real, yh * wf - tgt_ref[...], 0.0)
        loss_ref[...] += jnp.sum(jnp.sum(e * e, axis=0, keepdims=True), axis=1, keepdims=True) * (0.5 / D)
        dy = e * (1.0 / D)
        gfin_ref[...] += jnp.sum(dy * yh, axis=0, keepdims=True)
        dh2_ref[...] = _rms_bwd(dy, yh, rs2, wf)

    xs = pl.BlockSpec((TM, D), lambda i: (jnp.minimum(i, nblk - 1), 0))
    rs = lambda w: pl.BlockSpec((TM, w), lambda i: (i, 0))
    r_tot = t + TM
    return pl.pallas_call(
        body, name="mlp_fwd", grid=(nblk + 1,),
        in_specs=[xs, VMEM_SPEC, xs, rs(512), rs(512), VMEM_SPEC, VMEM_SPEC, VMEM_SPEC, VMEM_SPEC, VMEM_SPEC],
        out_specs=[rs(D), rs(D), rs(DFF), rs(D), pl.BlockSpec((1, 1), lambda i: (0, 0)), pl.BlockSpec((1, D), lambda i: (0, 0))],
        out_shape=[jax.ShapeDtypeStruct((r_tot, D), F32), jax.ShapeDtypeStruct((r_tot, D), BF16),
                   jax.ShapeDtypeStruct((r_tot, DFF), BF16), jax.ShapeDtypeStruct((r_tot, D), F32),
                   jax.ShapeDtypeStruct((1, 1), F32), jax.ShapeDtypeStruct((1, D), F32)],
        compiler_params=_cp(56),
    )(x, metapad, tgt, ogla, oswa, wo, wff, w1, w2, wfin)


def _mlp_bwd(h1, a, dh2, ogla, oswa, wo, wff, w1, w2):
    r_tot = h1.shape[0]
    nt = r_tot // TM

    def body(h1_ref, a_ref, dh2_ref, og_ref, os_ref, wo_ref, wff_ref, w1_ref, w2_ref,
             da_ref, dh2b_ref, dh1_ref, do_ref, dwo_ref, gff_ref):
        i = pl.program_id(0)

        @pl.when(i == 0)
        def _():
            dwo_ref[...] = jnp.zeros_like(dwo_ref)
            gff_ref[...] = jnp.zeros_like(gff_ref)

        dh2 = dh2_ref[...]
        dh2b = _bf(dh2)
        dh2b_ref[...] = dh2b
        df = jnp.zeros((TM, D), F32)
        for n in range(4):
            dz = _dot_nt(dh2b, w2_ref[n])
            da = _bf(dz * (2.0 * jnp.maximum(a_ref[:, n * D:(n + 1) * D].astype(F32), 0.0)))
            da_ref[:, n * D:(n + 1) * D] = da
            df = df + _dot_nt(da, w1_ref[n])
        fh, rs1 = _rms(h1_ref[...])
        gff_ref[...] += jnp.sum(df * fh, axis=0, keepdims=True)
        dh1 = dh2 + _rms_bwd(df, fh, rs1, wff_ref[...])
        dh1_ref[...] = dh1
        dh1b = _bf(dh1)
        do_ref[...] = _dot_nt(dh1b, wo_ref[...])
        dwo_ref[0:512, :] += _dot_tn(og_ref[...], dh1b)
        dwo_ref[512:1024, :] += _dot_tn(os_ref[...], dh1b)

    rs = lambda w: pl.BlockSpec((TM, w), lambda i: (i, 0))
    return pl.pallas_call(
        body, name="mlp_bwd", grid=(nt,),
        in_specs=[rs(D), rs(DFF), rs(D), rs(512), rs(512), VMEM_SPEC, VMEM_SPEC, VMEM_SPEC, VMEM_SPEC],
        out_specs=[rs(DFF), rs(D), rs(D), rs(D), pl.BlockSpec((D, D), lambda i: (0, 0)),
                   pl.BlockSpec((1, D), lambda i: (0, 0))],
        out_shape=[jax.ShapeDtypeStruct((r_tot, DFF), BF16), jax.ShapeDtypeStruct((r_tot, D), BF16),
                   jax.ShapeDtypeStruct((r_tot, D), F32), jax.ShapeDtypeStruct((r_tot, D), F32),
                   jax.ShapeDtypeStruct((D, D), F32), jax.ShapeDtypeStruct((1, D), F32)],
        compiler_params=_cp(56),
    )(h1, a, dh2, ogla, oswa, wo, wff, w1, w2)


def _ffn_wgrad(f, a, da, dh2b):
    r_tot = f.shape[0]
    kt = 768 if r_tot % 768 == 0 else TM
    nk = r_tot // kt

    def body(f_ref, a_ref, da_ref, dh2_ref, dw1_ref, dw2_ref, acc1, acc2):
        k = pl.program_id(1)

        @pl.when(k == 0)
        def _():
            acc1[...] = jnp.zeros_like(acc1)
            acc2[...] = jnp.zeros_like(acc2)

        zr = jnp.maximum(a_ref[...], 0.0)
        acc1[...] += _dot_tn(f_ref[...], da_ref[...])
        acc2[...] += _dot_tn(zr * zr, dh2_ref[...])

        @pl.when(k == nk - 1)
        def _():
            for hh in range(2):
                dw1_ref[hh, 0] = acc1[hh * 512:(hh + 1) * 512, :]
                dw2_ref[hh, 0] = acc2[hh * 512:(hh + 1) * 512, :]

    full = pl.BlockSpec((kt, D), lambda n, k: (k, 0))
    col = pl.BlockSpec((kt, D), lambda n, k: (k, n))
    out = pl.BlockSpec((2, 1, 512, D), lambda n, k: (0, n, 0, 0))
    return pl.pallas_call(
        body, name="ffn_wgrad", grid=(4, nk),
        in_specs=[full, col, col, full], out_specs=[out, out],
        out_shape=[jax.ShapeDtypeStruct((2, 4, 512, D), F32)] * 2,
        scratch_shapes=[pltpu.VMEM((D, D), F32), pltpu.VMEM((D, D), F32)],
        compiler_params=_cp(48, ("arbitrary", "arbitrary")),
    )(f, a, da, dh2b)


def _proj_bwd(x, metapad, wm, wt, dgla, dswa_q, dsk, dsv, dlr, dh1, comm=None):
    t = x.shape[0]
    nblk = t // TM

    def body(x_ref, mp_ref, wm_ref, w_ref, dg_ref, dq_ref, dk_ref, dv_ref, dlr_ref, dh1_ref,
             gx_ref, gmeta_ref, dw_ref, gmix_ref):
        i = pl.program_id(0)

        @pl.when(i == 0)
        def _():
            dw_ref[...] = jnp.zeros_like(dw_ref)
            gmix_ref[...] = jnp.zeros_like(gmix_ref)

        h = jnp.where(i == nblk, mp_ref[...], x_ref[...])
        uh, rs = _rms(h)
        wm_v = wm_ref[...]
        u = _bf(uh * wm_v)
        parts = ((dg_ref[...], 0, R_LR), (dlr_ref[:, 0:16], R_LR, 16), (dq_ref[...], R_LR + 16, 512),
                 (_bf(dk_ref[...]), R_LR + 528, 128), (_bf(dv_ref[...]), R_LR + 656, 128))
        du = jnp.zeros((TM, D), F32)
        for val, r0, w in parts:
            du = du + _dot(val, w_ref[r0:r0 + w, :])
            dw_ref[r0:r0 + w, :] += _dot_tn(val, u)
        gmix_ref[...] += jnp.sum(du * uh, axis=0, keepdims=True)
        dh0 = dh1_ref[...] + _rms_bwd(du, uh, rs, wm_v)

        @pl.when(i < nblk)
        def _():
            gx_ref[...] = dh0

        @pl.when(i == nblk)
        def _():
            gmeta_ref[...] = dh0[:NM]

    xs = pl.BlockSpec((TM, D), lambda i: (jnp.minimum(i, nblk - 1), 0))
    rs_ = lambda w: pl.BlockSpec((TM, w), lambda i: (i, 0))
    return _call(
        body, "proj_bwd", (nblk + 1,),
        [xs, VMEM_SPEC, VMEM_SPEC, VMEM_SPEC, rs_(1536), rs_(512), rs_(128), rs_(128), rs_(128), rs_(D)],
        [xs, pl.BlockSpec((NM, D), lambda i: (0, 0)), pl.BlockSpec((DIN, D), lambda i: (0, 0)),
         pl.BlockSpec((1, D), lambda i: (0, 0))],
        [jax.ShapeDtypeStruct((t, D), F32), jax.ShapeDtypeStruct((NM, D), F32),
         jax.ShapeDtypeStruct((DIN, D), F32), jax.ShapeDtypeStruct((1, D), F32)],
        [], _cp(56), (x, metapad, wm, wt, dgla, dswa_q, dsk, dsv, dlr, dh1), comm)


def _place():
    return lax.axis_index("x"), lax.axis_index("y"), lax.axis_index("c")


def _other_chips(x, y):
    return [(1 - x, y), (x, 1 - y), (1 - x, 1 - y)]


def _dma_sems(*counts):
    return tuple(pltpu.SemaphoreType.DMA((k,)) for k in counts)


def _gather_shards(shards):
    n = len(shards)

    def plan(ins, outs, sems):
        send, recv, loc = sems
        x, y, c = _place()
        chips = _other_chips(x, y)

        def remote(a, k, shard_of):
            tx, ty = chips[k]
            sx, sy = shard_of
            return pltpu.make_async_remote_copy(
                src_ref=ins[a], dst_ref=outs[a].at[2 * sx + sy], send_sem=send.at[3 * a + k], recv_sem=recv.at[3 * a + k],
                device_id=(tx, ty, c), device_id_type=MESH)

        local = [pltpu.make_async_copy(ins[a], outs[a].at[2 * x + y], loc.at[a]) for a in range(n)]
        sends = [remote(a, k, (x, y)) for a in range(n) for k in range(3)]
        waits = ([lambda a=a, k=k: remote(a, k, chips[k]).wait_recv() for a in range(n) for k in range(3)]
                 + [cp.wait_send for cp in sends] + [cp.wait for cp in local])
        return local + sends, waits

    return _Comm(tuple(shards), tuple(jax.ShapeDtypeStruct((4,) + s.shape, s.dtype) for s in shards),
                 _dma_sems(3 * n, 3 * n, n), plan)


def _swap_halves(grads):
    n = len(grads)

    def plan(ins, outs, sems):
        send, recv = sems
        x, y, c = _place()
        cps = [pltpu.make_async_remote_copy(
            src_ref=ins[a].at[1 - c], dst_ref=outs[a], send_sem=send.at[a], recv_sem=recv.at[a],
            device_id=(x, y, 1 - c), device_id_type=MESH) for a in range(n)]
        return cps, [cp.wait for cp in cps]

    return _Comm(tuple(grads), tuple(jax.ShapeDtypeStruct(g.shape[1:], g.dtype) for g in grads), _dma_sems(n, n), plan)


def _scatter_shards(parts):
    n = len(parts)

    def plan(ins, outs, sems):
        send, recv = sems
        x, y, c = _place()
        cps = [pltpu.make_async_remote_copy(
            src_ref=ins[a].at[2 * tx + ty], dst_ref=outs[a].at[k], send_sem=send.at[3 * a + k],
            recv_sem=recv.at[3 * a + k], device_id=(tx, ty, c), device_id_type=MESH)
            for a in range(n) for k, (tx, ty) in enumerate(_other_chips(x, y))]
        return cps, [cp.wait for cp in cps]

    return _Comm(tuple(parts), tuple(jax.ShapeDtypeStruct((3,) + p.shape[1:], p.dtype) for p in parts),
                 _dma_sems(3 * n, 3 * n), plan)


def _join_halves(halves):
    n = len(halves)

    def plan(ins, outs, sems):
        send, recv, loc = sems
        x, y, c = _place()

        def remote(a, half):
            return pltpu.make_async_remote_copy(
                src_ref=ins[a], dst_ref=outs[a].at[half], send_sem=send.at[a], recv_sem=recv.at[a],
                device_id=(x, y, 1 - c), device_id_type=MESH)

        local = [pltpu.make_async_copy(ins[a], outs[a].at[c], loc.at[a]) for a in range(n)]
        sends = [remote(a, c) for a in range(n)]
        waits = ([lambda a=a: remote(a, 1 - c).wait_recv() for a in range(n)] + [cp.wait_send for cp in sends]
                 + [cp.wait for cp in local])
        return local + sends, waits

    return _Comm(tuple(halves), tuple(jax.ShapeDtypeStruct((2,) + h.shape, h.dtype) for h in halves),
                 _dma_sems(n, n, n), plan)


def _reduce_w_in(parts, comm):
    rows, cols = parts.shape[1:]
    ci, co = len(comm.ins), len(comm.outs)

    def body(*refs):
        parts_ref, c_in, out_ref, c_out = refs[0], refs[1:1 + ci], refs[1 + ci], refs[2 + ci:2 + ci + co]
        mine, tosend, rbuf, qbuf, sib, send, recv, loc = refs[2 + ci + co:10 + ci + co]
        c_sem = refs[10 + ci + co:]
        x, y, c = _place()
        starts, waits = comm.plan(c_in, c_out, c_sem)
        for cp in starts:
            cp.start()
        load = pltpu.make_async_copy(parts_ref, mine, loc.at[0])
        load.start()
        load.wait()
        cps = []
        for k, (tx, ty) in enumerate(_other_chips(x, y)):
            tosend[k] = _bf(mine[2 * tx + ty])
            cps.append(pltpu.make_async_remote_copy(
                src_ref=tosend.at[k], dst_ref=rbuf.at[k], send_sem=send.at[k], recv_sem=recv.at[k],
                device_id=(tx, ty, c), device_id_type=MESH))
            cps[-1].start()
        for cp in cps:
            cp.wait()
        qbuf[...] = mine[2 * x + y] + rbuf[0].astype(F32) + rbuf[1].astype(F32) + rbuf[2].astype(F32)
        swap = pltpu.make_async_remote_copy(src_ref=qbuf, dst_ref=sib, send_sem=send.at[3], recv_sem=recv.at[3],
                                            device_id=(x, y, 1 - c), device_id_type=MESH)
        swap.start()
        swap.wait()
        out_ref[...] = qbuf[...] + sib[...]
        for wait in waits:
            wait()

    outs = pl.pallas_call(
        body, name="reduce_w_in",
        in_specs=[ANY_SPEC] * (1 + ci), out_specs=[VMEM_SPEC] + [ANY_SPEC] * co,
        out_shape=[jax.ShapeDtypeStruct((rows, cols), F32)] + list(comm.outs),
        scratch_shapes=[pltpu.VMEM((4, rows, cols), F32), pltpu.VMEM((3, rows, cols), BF16), pltpu.VMEM((3, rows, cols), BF16),
                        pltpu.VMEM((rows, cols), F32), pltpu.VMEM((rows, cols), F32), *_dma_sems(4, 4, 1), *comm.sems],
        compiler_params=pltpu.CompilerParams(vmem_limit_bytes=48 << 20),
    )(parts, *comm.ins)
    return outs[0], outs[1:]


def _allreduce_small(pack):
    p = pack.shape[0]

    def body(in_ref, out_ref, buf, send, recv):
        x, y, c = _place()
        me = 4 * x + 2 * y + c
        buf[me] = in_ref[...]
        sends = []
        for k in range(1, 8):
            tx, ty, tc = x ^ (k >> 2), y ^ ((k >> 1) & 1), c ^ (k & 1)
            sends.append(pltpu.make_async_remote_copy(
                src_ref=in_ref, dst_ref=buf.at[me], send_sem=send.at[k - 1], recv_sem=recv.at[k - 1],
                device_id=(tx, ty, tc), device_id_type=MESH))
        for cp in sends:
            cp.start()
        for k in range(1, 8):
            peer = 4 * (x ^ (k >> 2)) + 2 * (y ^ ((k >> 1) & 1)) + (c ^ (k & 1))
            pltpu.make_async_remote_copy(
                src_ref=in_ref, dst_ref=buf.at[peer], send_sem=send.at[k - 1], recv_sem=recv.at[k - 1],
                device_id=(x, y, c), device_id_type=MESH).wait_recv()
        for cp in sends:
            cp.wait_send()
        acc = buf[0]
        for d in range(1, 8):
            acc = acc + buf[d]
        out_ref[...] = acc

    return pl.pallas_call(
        body, name="allreduce_small",
        in_specs=[VMEM_SPEC], out_specs=VMEM_SPEC, out_shape=jax.ShapeDtypeStruct(pack.shape, F32),
        scratch_shapes=[pltpu.VMEM((8, p, D), F32), pltpu.SemaphoreType.DMA((7,)), pltpu.SemaphoreType.DMA((7,))],
    )(pack)


GRID4 = 4


def _sum_parts(sel, firsts, others, name, also_bf16):
    n = len(firsts)
    nk = others[0].shape[0]

    def body(sel_ref, *refs):
        fs, os_, outs = refs[:n], refs[n:2 * n], refs[2 * n:]
        for a in range(n):
            acc = fs[a][0]
            for k in range(nk):
                acc = acc + os_[a][k].astype(F32)
            outs[a][...] = acc
            if also_bf16:
                outs[n + a][...] = _bf(acc)

    def rows(a):
        return firsts[a].shape[1] // GRID4

    in_specs = ([pl.BlockSpec((1, rows(a), firsts[a].shape[2]), lambda i, s: (s[0], i, 0)) for a in range(n)]
                + [pl.BlockSpec((nk, rows(a), firsts[a].shape[2]), lambda i, s: (0, i, 0)) for a in range(n)])
    out_specs = [pl.BlockSpec((rows(a), firsts[a].shape[2]), lambda i, s: (i, 0)) for a in range(n)]
    out_shape = [jax.ShapeDtypeStruct(f.shape[1:], F32) for f in firsts]
    if also_bf16:
        out_specs = out_specs * 2
        out_shape = out_shape + [jax.ShapeDtypeStruct(f.shape[1:], BF16) for f in firsts]
    outs = pl.pallas_call(
        body, name=name,
        grid_spec=pltpu.PrefetchScalarGridSpec(num_scalar_prefetch=1, grid=(GRID4,), in_specs=in_specs, out_specs=out_specs),
        out_shape=out_shape, compiler_params=_cp(48),
    )(sel, *firsts, *others)
    return outs[:n], outs[n:]


def _adamw_math(w, g, m, v):
    m2 = ADAM_B1 * m + (1.0 - ADAM_B1) * g
    v2 = ADAM_B2 * v + (1.0 - ADAM_B2) * (g * g)
    m_hat = m2 / (1.0 - ADAM_B1 ** ADAM_STEP)
    v_hat = v2 / (1.0 - ADAM_B2 ** ADAM_STEP)
    return -ADAM_LR * (m_hat / (jnp.sqrt(v_hat) + ADAM_EPS) + ADAM_WD * w), m2, v2


def _adamw_big(ws, gs, ms, vs):
    n = len(ws)

    def body(*refs):
        for a in range(n):
            d, m2, v2 = _adamw_math(refs[a][...], refs[n + a][...], refs[2 * n + a][...], refs[3 * n + a][...])
            refs[4 * n + a][...] = d
            refs[5 * n + a][...] = m2
            refs[6 * n + a][...] = v2

    specs = [pl.BlockSpec((w.shape[0] // GRID4, w.shape[1]), lambda i: (i, 0)) for w in ws]
    return pl.pallas_call(
        body, name="adamw_big", grid=(GRID4,),
        in_specs=specs * 4, out_specs=specs * 3,
        out_shape=[jax.ShapeDtypeStruct(w.shape, F32) for w in ws] * 3,
        compiler_params=_cp(48),
    )(*ws, *gs, *ms, *vs)


def _adamw_small(ws, gs, ms, vs):
    n = len(ws)

    def body(*refs):
        for a in range(n):
            d, m2, v2 = _adamw_math(refs[a][...], refs[n + a][...], refs[2 * n + a][...], refs[3 * n + a][...])
            refs[4 * n + a][...] = d
            refs[5 * n + a][...] = m2
            refs[6 * n + a][...] = v2

    return pl.pallas_call(
        body, name="adamw_small",
        in_specs=[VMEM_SPEC] * (4 * n), out_specs=[VMEM_SPEC] * (3 * n),
        out_shape=[jax.ShapeDtypeStruct(w.shape, F32) for w in ws] * 3,
        compiler_params=pltpu.CompilerParams(vmem_limit_bytes=40 << 20),
    )(*ws, *gs, *ms, *vs)


def kernel(x, meta_tokens, norm_mix_w, w_in, w_gate_up, b_gate, gla_norm_w, sinks, w_out, norm_ff_w, w_ff1, w_ff2, final_norm_w, loss_target, m_meta_tokens, m_norm_mix_w, m_w_in, m_w_gate_up, m_b_gate, m_gla_norm_w, m_sinks, m_w_out, m_norm_ff_w, m_w_ff1, m_w_ff2, m_final_norm_w, v_meta_tokens, v_norm_mix_w, v_w_in, v_w_gate_up, v_b_gate, v_gla_norm_w, v_sinks, v_w_out, v_norm_ff_w, v_w_ff1, v_w_ff2, v_final_norm_w):
    xi, yi, ci = _place()
    shard = (2 * xi + yi).astype(jnp.int32).reshape(1)
    core = ci.astype(jnp.int32).reshape(1)

    small = jnp.concatenate([meta_tokens, w_gate_up[0], jnp.zeros((NM, 64), F32)], axis=1)
    g_in, g_small = _run_comm(_gather_shards([_bf(w_in[0].T), small]), "gather_w_in")
    wt = g_in.reshape(DIN, D)
    meta = g_small[:, :, 0:256].transpose(1, 0, 2).reshape(NM, D)
    wgu = g_small[:, :, 256:320].transpose(1, 0, 2).reshape(NM, 256)

    xs, tgt = x[0], loss_target[0]
    t = xs.shape[0]
    wfin = final_norm_w.reshape(1, D)
    metapad = jnp.concatenate([meta, jnp.zeros((TM - NM, D), F32)], axis=0)
    wgu_p = _bf(jnp.concatenate([wgu, jnp.zeros((128 - 16, 256), F32)], axis=0))
    tabs = _rope_tables(t)

    proj = _proj_fwd(xs, metapad, norm_mix_w, wt)
    (oswa, lse), (w1, w2) = _swa_fwd(proj, tabs, sinks, t, _gather_shards([_bf(w_ff1[0]), _bf(w_ff2[0])]))
    (ogla, oraw, sst), (g_out,) = _gla_fwd(proj, wgu_p, b_gate, gla_norm_w, t, _gather_shards([_bf(w_out[0])]))
    wo = g_out.reshape(D, D)
    h1, f, a, dh2, loss, gfin = _mlp_fwd(xs, metapad, tgt, ogla, oswa, wo, norm_ff_w, w1, w2, wfin)

    da, dh2b, dh1, do, dwo, gff = _mlp_bwd(h1, a, dh2, ogla, oswa, wo, norm_ff_w, w1, w2)
    dw1, dw2 = _ffn_wgrad(f, a, da, dh2b)
    big = [dwo.reshape(4, 2, 128, D).transpose(1, 0, 2, 3), dw1, dw2]
    (dsq, dsk, dsv, dsink), theirs = _swa_bwd(proj, tabs, sinks, lse, do, t, _swap_halves(big))
    sums, sums_bf = _sum_parts(core, [b.reshape((2, -1) + b.shape[3:]) for b in big],
                               [s.reshape((1, -1) + s.shape[2:]) for s in theirs], "sum_cores", True)
    sums = [s.reshape(b.shape[1:]) for s, b in zip(sums, big)]
    sums_bf = [s.reshape(b.shape[1:]) for s, b in zip(sums_bf, big)]
    (dgla, dlr, dwgu, dbg, dgnw), arrived = _gla_bwd(proj, oraw, sst, do, wgu_p, b_gate, gla_norm_w, t,
                                                     _scatter_shards(sums_bf))
    halves, _ = _sum_parts(shard, sums, arrived, "sum_chips", False)
    (gx, gmeta, dwt, gmix), _ = _proj_bwd(xs, metapad, norm_mix_w, wt, dgla, dsq, dsk, dsv, dlr, dh1)
    gwt_in, joined = _reduce_w_in(dwt.reshape(4, DIN // 4, D), _join_halves(halves))
    gw_out, gw_1, gw_2 = [j.reshape((-1, j.shape[2])) for j in joined]
    g = dict(meta=gmeta, mix=gmix, wgu=dwgu[:16], bg=dbg, gnw=dgnw, sinks=dsink, ff=gff, fin=gfin)

    tail = jnp.concatenate([g["bg"], g["gnw"], g["sinks"], loss, jnp.zeros((1, D - 256 - 128 - 8 - 1), F32)], axis=1)
    pack = jnp.concatenate([g["meta"], g["mix"], g["ff"], g["fin"], tail, g["wgu"].reshape(4, D)], axis=0)
    tot = _allreduce_small(pack)
    g_meta = lax.dynamic_slice_in_dim(tot[0:NM], shard[0] * 256, 256, axis=1)
    g_mix, g_ff, g_fin = tot[16:17], tot[17:18], tot[18]
    g_bg, g_gnw, g_sinks, loss_tot = tot[19:20, 0:256], tot[19:20, 256:384], tot[19:20, 384:392], tot[19, 392]
    g_wgu = lax.dynamic_slice_in_dim(tot[20:24].reshape(NM, 256), shard[0] * 64, 64, axis=1)

    bo = _adamw_big([w_out[0], w_ff1[0], w_ff2[0]], [gw_out, gw_1, gw_2], [m_w_out[0], m_w_ff1[0], m_w_ff2[0]],
                    [v_w_out[0], v_w_ff1[0], v_w_ff2[0]])

    fin2 = lambda a: a.reshape(1, D)
    sw = [meta_tokens, norm_mix_w, w_gate_up[0], b_gate, gla_norm_w, sinks, norm_ff_w, fin2(final_norm_w), w_in[0].T]
    sg = [g_meta, g_mix, g_wgu, g_bg, g_gnw, g_sinks, g_ff, fin2(g_fin), gwt_in]
    sm = [m_meta_tokens, m_norm_mix_w, m_w_gate_up[0], m_b_gate, m_gla_norm_w, m_sinks, m_norm_ff_w, fin2(m_final_norm_w),
          m_w_in[0].T]
    sv = [v_meta_tokens, v_norm_mix_w, v_w_gate_up[0], v_b_gate, v_gla_norm_w, v_sinks, v_norm_ff_w, fin2(v_final_norm_w),
          v_w_in[0].T]
    so = _adamw_small(sw, sg, sm, sv)

    def ordered(small_o, big_o):
        meta_, mix_, wgu_, bg_, gnw_, sinks_, ff_, fin_, wt_ = small_o
        w_out_, w_1_, w_2_ = big_o
        return (meta_, mix_, wt_.T[None], wgu_[None], bg_, gnw_, sinks_, w_out_[None], ff_, w_1_[None], w_2_[None],
                fin_.reshape(D))

    grads = ordered(sg, [gw_out, gw_1, gw_2])
    deltas = ordered(so[0:9], bo[0:3])
    new_m = ordered(so[9:18], bo[3:6])
    new_v = ordered(so[18:27], bo[6:9])
    return (loss_tot, gx[None], *grads, *deltas, *new_m, *new_v)
```

```python
from typing import Callable, NamedTuple

import jax
import jax.numpy as jnp
import numpy as np
from jax import lax
from jax.experimental import pallas as pl
from jax.experimental.pallas import tpu as pltpu

F32 = jnp.float32
BF16 = jnp.bfloat16

D = 1024
DFF = 4096
NM = 16
TM = 256
CH = 64
SB = 128
QB = 2 * SB
EPS = 1e-5
C_GQ, C_GK, C_GV, C_GR, C_SQ, C_SK, C_SV, C_LR, DINP = 0, 256, 512, 1024, 1536, 2048, 2176, 2304, 2432
DIN = 2320
R_LR = 1536
ROPE_THETA = 500000.0
ADAM_LR, ADAM_B1, ADAM_B2, ADAM_EPS, ADAM_WD, ADAM_STEP = 0.001, 0.9, 0.999, 1e-08, 0.01, 10
NEG = -1e30
MESH = pl.DeviceIdType.MESH
VMEM_SPEC = pl.BlockSpec(memory_space=pltpu.VMEM)
ANY_SPEC = pl.BlockSpec(memory_space=pl.ANY)
SMEM_SPEC = pl.BlockSpec(memory_space=pltpu.SMEM)


def _cp(vmem_mb, sem=("arbitrary",)):
    return pltpu.CompilerParams(dimension_semantics=sem, vmem_limit_bytes=vmem_mb << 20)


def _dot(a, b):
    return jnp.dot(a, b, preferred_element_type=F32)


def _dot_nt(a, b):
    return lax.dot_general(a, b, (((1,), (1,)), ((), ())), preferred_element_type=F32)


def _dot_tn(a, b):
    return lax.dot_general(a, b, (((0,), (0,)), ((), ())), preferred_element_type=F32)


def _bf(x):
    return x.astype(BF16)


def _dot3(m01, x):
    x1 = _bf(x)
    r1 = x - x1.astype(F32)
    x2 = _bf(r1)
    x3 = _bf(r1 - x2.astype(F32))
    return _dot(m01, x1) + _dot(m01, x2) + _dot(m01, x3)


def _rms(h):
    rs = lax.rsqrt(jnp.mean(h * h, axis=-1, keepdims=True) + EPS)
    return h * rs, rs


def _rms_bwd(dy, yhat, rs, w):
    dyh = dy * w
    return rs * (dyh - yhat * jnp.mean(dyh * yhat, axis=-1, keepdims=True))


class _Comm(NamedTuple):
    ins: tuple
    outs: tuple
    sems: tuple
    plan: Callable


def _call(body, name, grid, in_specs, out_specs, out_shape, scratch, params, args, comm=None):
    if comm is None:
        outs = pl.pallas_call(body, name=name, grid=grid, in_specs=in_specs, out_specs=out_specs, out_shape=out_shape,
                              scratch_shapes=scratch, compiler_params=params)(*args)
        return outs, None
    n_in, n_out, n_scr = len(in_specs), len(out_specs), len(scratch)
    ci, co = len(comm.ins), len(comm.outs)
    last = grid[0] - 1

    def wrapped(*refs):
        own_in, c_in = refs[:n_in], refs[n_in:n_in + ci]
        refs = refs[n_in + ci:]
        own_out, c_out = refs[:n_out], refs[n_out:n_out + co]
        refs = refs[n_out + co:]
        own_scr, c_sem = refs[:n_scr], refs[n_scr:]
        i = pl.program_id(0)

        @pl.when(i == 0)
        def _():
            for cp in comm.plan(c_in, c_out, c_sem)[0]:
                cp.start()

        body(*own_in, *own_out, *own_scr)

        @pl.when(i == last)
        def _():
            for wait in comm.plan(c_in, c_out, c_sem)[1]:
                wait()

    outs = pl.pallas_call(
        wrapped, name=name, grid=grid, in_specs=list(in_specs) + [ANY_SPEC] * ci, out_specs=list(out_specs) + [ANY_SPEC] * co,
        out_shape=list(out_shape) + list(comm.outs), scratch_shapes=list(scratch) + list(comm.sems), compiler_params=params,
    )(*args, *comm.ins)
    return outs[:n_out], outs[n_out:]


def _run_comm(comm, name):
    ci, co = len(comm.ins), len(comm.outs)

    def body(*refs):
        starts, waits = comm.plan(refs[:ci], refs[ci:ci + co], refs[ci + co:])
        for cp in starts:
            cp.start()
        for wait in waits:
            wait()

    return pl.pallas_call(body, name=name, in_specs=[ANY_SPEC] * ci, out_specs=[ANY_SPEC] * co, out_shape=list(comm.outs),
                          scratch_shapes=list(comm.sems))(*comm.ins)


def _proj_fwd(x, metapad, wm, wt, tabs):
    t = x.shape[0]
    nblk = t // TM

    def body(x_ref, mp_ref, wm_ref, w_ref, tab_ref, proj_ref):
        i = pl.program_id(0)
        h = jnp.where(i == nblk, mp_ref[...], x_ref[...])
        u, _ = _rms(h)
        ub = _bf(u * wm_ref[...])
        proj_ref[:, 0:C_SQ] = _dot_nt(ub, w_ref[0:R_LR, :])
        att = _dot_nt(ub, w_ref[R_LR + 16:DIN, :])
        tab = tab_ref[...]
        proj_ref[:, C_SQ:C_SK] = _rope(att[:, 0:512], tab, 1.0) * 0.125
        proj_ref[:, C_SK:C_SV] = _rope(att[:, 512:640], tab, 1.0)
        proj_ref[:, C_SV:C_LR] = att[:, 640:768]
        proj_ref[:, C_LR:DINP] = jnp.zeros((TM, DINP - C_LR), F32)
        proj_ref[:, C_LR:C_LR + 16] = _dot_nt(ub, w_ref[R_LR:R_LR + 16, :])

    return pl.pallas_call(
        body, name="proj_fwd", grid=(nblk + 1,),
        in_specs=[pl.BlockSpec((TM, D), lambda i: (jnp.minimum(i, nblk - 1), 0)), VMEM_SPEC, VMEM_SPEC, VMEM_SPEC,
                  pl.BlockSpec((TM, 128), lambda i: (i, 0))],
        out_specs=pl.BlockSpec((TM, DINP), lambda i: (i, 0)),
        out_shape=jax.ShapeDtypeStruct((t + TM, DINP), F32),
        compiler_params=_cp(40),
    )(x, metapad, wm, wt, tabs)


def _chunk_masks():
    r = lax.broadcasted_iota(jnp.int32, (TM, TM), 0)
    c = lax.broadcasted_iota(jnp.int32, (TM, TM), 1)
    same = (r // CH) == (c // CH)
    lower = _bf(jnp.where(same & (c <= r), 1.0, 0.0))
    upper = _bf(jnp.where(same & (c >= r), 1.0, 0.0))
    return lower, upper


def _gla_gate(lr, wgu, bg, valid, lower):
    z = _dot(_bf(lr), wgu) + bg
    g = (jnp.minimum(z, 0.0) - jnp.log(1.0 + jnp.exp(-jnp.abs(z)))) * (1.0 / 16.0)
    g = jnp.where(valid, g, 0.0)
    return z, _dot3(lower, g)


def _gla_decays(q, k, b):
    nc = TM // CH
    b3 = b.reshape(nc, CH, 256)
    blast = b3[:, CH - 1:CH, :]
    eb = jnp.exp(b)
    enb = jnp.exp(-b)
    ebl = jnp.exp(blast - b3).reshape(TM, 256)
    return eb, enb, ebl, jnp.exp(blast)


def _tri(lower_incl):
    r = lax.broadcasted_iota(jnp.int32, (CH, CH), 0)
    c = lax.broadcasted_iota(jnp.int32, (CH, CH), 1)
    return ((c <= r) if lower_incl else (c >= r))[None]


def _gla_fwd(proj, wgu, bg, gnw, t, comm=None):
    nblk = t // TM
    nt = nblk + 1
    nc = TM // CH

    def blk(i):
        return (i + nblk) % nt

    def body(q_ref, k_ref, v_ref, r_ref, lr_ref, wgu_ref, bg_ref, gnw_ref, o_ref, oraw_ref, sst_ref, st_scr):
        i = pl.program_id(0)

        @pl.when(i == 0)
        def _():
            st_scr[...] = jnp.zeros_like(st_scr)

        rows = blk(i) * TM + lax.broadcasted_iota(jnp.int32, (TM, 1), 0)
        lower, _ = _chunk_masks()
        _, b = _gla_gate(lr_ref[...], wgu_ref[...], bg_ref[...], rows < t + NM, lower)
        q = q_ref[...]
        k = k_ref[...]
        eb, enb, ebl, eblast = _gla_decays(q, k, b)
        qt = q * 0.125 * eb
        kt = k * enb
        kh = k * ebl
        tril = _tri(True)
        outs = []
        for h in range(4):
            hs = slice(h * CH, (h + 1) * CH)
            qh = _bf(qt[:, hs]).reshape(nc, CH, CH)
            kth = _bf(kt[:, hs]).reshape(nc, CH, CH)
            khh = _bf(kh[:, hs]).reshape(nc, CH, CH)
            vh = _bf(v_ref[:, h * 128:(h + 1) * 128]).reshape(nc, CH, 128)
            a = jnp.einsum('cid,cjd->cij', qh, kth, preferred_element_type=F32)
            a = jnp.where(tril, a, 0.0)
            o = jnp.einsum('cij,cjv->civ', _bf(a), vh, preferred_element_type=F32)
            kv = jnp.einsum('cjv,cjd->cvd', vh, khh, preferred_element_type=F32)
            st = st_scr[h]
            o_inter = []
            for c in range(nc):
                sst_ref[c, h] = st
                o_inter.append(_dot_nt(qh[c], _bf(st)))
                st = st * eblast[c, :, hs] + kv[c]
            st_scr[h] = st
            outs.append((o + jnp.stack(o_inter)).reshape(TM, 128))
        oraw = jnp.concatenate(outs, axis=1)
        oraw_ref[...] = oraw
        gn = gnw_ref[...]
        res = []
        for h in range(4):
            on, _ = _rms(oraw[:, h * 128:(h + 1) * 128])
            r = r_ref[:, h * 128:(h + 1) * 128]
            res.append(on * gn * (r * jax.nn.sigmoid(r)))
        o_ref[...] = _bf(jnp.concatenate(res, axis=1))

    def spec(w, cb):
        return pl.BlockSpec((TM, w), lambda i: (blk(i), cb))

    return _call(
        body, "gla_fwd", (nt,),
        [spec(256, 0), spec(256, 1), spec(512, 1), spec(512, 2), spec(128, C_LR // 128), VMEM_SPEC, VMEM_SPEC, VMEM_SPEC],
        [spec(512, 0), spec(512, 0), pl.BlockSpec((nc, 4, 128, CH), lambda i: (blk(i), 0, 0, 0))],
        [jax.ShapeDtypeStruct((t + TM, 512), BF16), jax.ShapeDtypeStruct((t + TM, 512), F32),
         jax.ShapeDtypeStruct((nt * nc, 4, 128, CH), F32)],
        [pltpu.VMEM((4, 128, CH), F32)], _cp(40), (proj, proj, proj, proj, proj, wgu, bg, gnw), comm)


def _gla_bwd(proj, oraw, sst, do, wgu, bg, gnw, t, comm=None):
    nblk = t // TM
    nt = nblk + 1
    nc = TM // CH

    def blk(i):
        return (2 * nblk - i) % nt

    def body(q_ref, k_ref, v_ref, r_ref, lr_ref, oraw_ref, sst_ref, do_ref, wgu_ref, bg_ref, gnw_ref,
             dgla_ref, dlr_ref, dwgu_ref, dbg_ref, dgnw_ref, dst_scr):
        i = pl.program_id(0)

        @pl.when(i == 0)
        def _():
            dst_scr[...] = jnp.zeros_like(dst_scr)
            dwgu_ref[...] = jnp.zeros_like(dwgu_ref)
            dbg_ref[...] = jnp.zeros_like(dbg_ref)
            dgnw_ref[...] = jnp.zeros_like(dgnw_ref)

        rows = blk(i) * TM + lax.broadcasted_iota(jnp.int32, (TM, 1), 0)
        valid = rows < t + NM
        lower, upper = _chunk_masks()
        lr = lr_ref[...]
        z, b = _gla_gate(lr, wgu_ref[...], bg_ref[...], valid, lower)
        q = q_ref[...]
        k = k_ref[...]
        eb, enb, ebl, eblast = _gla_decays(q, k, b)
        qt = q * 0.125 * eb
        kt = k * enb
        kh = k * ebl
        gn = gnw_ref[...]
        tril = _tri(True)
        triu = _tri(False)
        dq_l, dk_l, dv_l, dr_l, db_l, ex_l = [], [], [], [], [], []
        dgn = jnp.zeros((1, 128), F32)
        for h in range(4):
            hs = slice(h * CH, (h + 1) * CH)
            vs = slice(h * 128, (h + 1) * 128)
            on, rs = _rms(oraw_ref[:, vs])
            r = r_ref[:, vs]
            sig = jax.nn.sigmoid(r)
            sil = r * sig
            dy = do_ref[:, vs]
            dr_l.append(dy * on * gn * (sig * (1.0 + r * (1.0 - sig))))
            dgn = dgn + jnp.sum(dy * sil * on, axis=0, keepdims=True)
            doraw = _rms_bwd(dy * sil, on, rs, gn)
            qtf = qt[:, hs].reshape(nc, CH, CH)
            ktf = kt[:, hs].reshape(nc, CH, CH)
            khf = kh[:, hs].reshape(nc, CH, CH)
            qh, kth, khh = _bf(qtf), _bf(ktf), _bf(khf)
            vh = _bf(v_ref[:, vs]).reshape(nc, CH, 128)
            doh = _bf(doraw).reshape(nc, CH, 128)
            at = jnp.where(triu, jnp.einsum('cjd,cid->cji', kth, qh, preferred_element_type=F32), 0.0)
            da = jnp.where(tril, jnp.einsum('civ,cjv->cij', doh, vh, preferred_element_type=F32), 0.0)
            dat = jnp.where(triu, jnp.einsum('cjv,civ->cji', vh, doh, preferred_element_type=F32), 0.0)
            dv = jnp.einsum('cji,civ->cjv', _bf(at), doh, preferred_element_type=F32)
            dqt = jnp.einsum('cij,cjd->cid', _bf(da), kth, preferred_element_type=F32)
            dkt = jnp.einsum('cji,cid->cjd', _bf(dat), qh, preferred_element_type=F32)
            gq = jnp.einsum('civ,cid->cvd', doh, qh, preferred_element_type=F32)
            dst = dst_scr[h]
            dsend = [None] * nc
            for c in reversed(range(nc)):
                dsend[c] = dst
                dst = dst * eblast[c, :, hs] + gq[c]
            dst_scr[h] = dst
            dse = jnp.stack(dsend)
            dseb = _bf(dse)
            stf = sst_ref[:, h]
            dqt = dqt + jnp.einsum('civ,cvd->cid', doh, _bf(stf), preferred_element_type=F32)
            dv = dv + jnp.einsum('cjd,cvd->cjv', khh, dseb, preferred_element_type=F32)
            dkh = jnp.einsum('cjv,cvd->cjd', vh, dseb, preferred_element_type=F32)
            extra = (jnp.sum(dkh * khf, axis=1, keepdims=True)
                     + eblast[:, :, hs] * jnp.sum(dse * stf, axis=1, keepdims=True))
            db_l.append((dqt * qtf - dkt * ktf - dkh * khf).reshape(TM, CH))
            ex_l.append(jnp.broadcast_to(extra, (nc, CH, CH)).reshape(TM, CH))
            dq_l.append((dqt.reshape(TM, CH)) * eb[:, hs] * 0.125)
            dk_l.append(dkt.reshape(TM, CH) * enb[:, hs] + dkh.reshape(TM, CH) * ebl[:, hs])
            dv_l.append(dv.reshape(TM, 128))
        dgnw_ref[...] += dgn
        db = jnp.concatenate(db_l, axis=1)
        dg = _dot3(upper, db) + jnp.concatenate(ex_l, axis=1)
        dz = jnp.where(valid, dg * (1.0 / 16.0) / (1.0 + jnp.exp(z)), 0.0)
        dzb = _bf(dz)
        dlr_ref[...] = _bf(_dot_nt(dzb, wgu_ref[...]))
        dwgu_ref[...] += _dot_tn(_bf(lr), dzb)
        dbg_ref[...] += jnp.sum(dz, axis=0, keepdims=True)
        dgla_ref[...] = _bf(jnp.concatenate(dq_l + dk_l + dv_l + dr_l, axis=1))

    def spec(w, cb):
        return pl.BlockSpec((TM, w), lambda i: (blk(i), cb))

    def acc(shape):
        return pl.BlockSpec(shape, lambda i: (0, 0))

    return _call(
        body, "gla_bwd", (nt,),
        [spec(256, 0), spec(256, 1), spec(512, 1), spec(512, 2), spec(128, C_LR // 128), spec(512, 0),
         pl.BlockSpec((nc, 4, 128, CH), lambda i: (blk(i), 0, 0, 0)), spec(512, 0), VMEM_SPEC, VMEM_SPEC, VMEM_SPEC],
        [spec(1536, 0), spec(128, 0), acc((128, 256)), acc((1, 256)), acc((1, 128))],
        [jax.ShapeDtypeStruct((t + TM, 1536), BF16), jax.ShapeDtypeStruct((t + TM, 128), BF16),
         jax.ShapeDtypeStruct((128, 256), F32), jax.ShapeDtypeStruct((1, 256), F32), jax.ShapeDtypeStruct((1, 128), F32)],
        [pltpu.VMEM((4, 128, CH), F32)], _cp(48), (proj, proj, proj, proj, proj, oraw, sst, do, wgu, bg, gnw), comm)


def _rope_tables(t):
    r = t + TM
    row = np.arange(r)
    pos = np.where(row < t, row + NM, np.where(row < t + NM, row - t, 0)).astype(np.float32)
    inv_freq = (1.0 / (np.float32(ROPE_THETA) ** (np.arange(0, 16, 2, dtype=np.float32) / np.float32(16)))).astype(np.float32)
    ang = (pos[:, None] * inv_freq[None, :]).astype(np.float32)
    cos, sin = np.cos(ang).astype(np.float32), np.sin(ang).astype(np.float32)
    one, zero = np.ones((r, 48), np.float32), np.zeros((r, 48), np.float32)
    return jnp.asarray(np.concatenate([cos, cos, one, -sin, sin, zero], axis=1))


def _rope(x, tab, sign):
    w = x.shape[1]
    rep = w // 64
    c = jnp.concatenate([tab[:, 0:64]] * rep, axis=1)
    s = jnp.concatenate([tab[:, 64:128]] * rep, axis=1)
    lane = lax.rem(lax.broadcasted_iota(jnp.int32, x.shape, 1), 64)
    partner = jnp.where(lane < 8, pltpu.roll(x, w - 8, 1), jnp.where(lane < 16, pltpu.roll(x, 8, 1), 0.0))
    return x * c + sign * (partner * s)


def _stack(x):
    return jnp.concatenate([x[:, g * 64:(g + 1) * 64] for g in range(4)], axis=0)


def _unstack(x):
    return jnp.concatenate([x[g * SB:(g + 1) * SB] for g in range(4)], axis=1)


def _swa_masks(b, nsb):
    r = lax.rem(lax.broadcasted_iota(jnp.int32, (4 * SB, SB), 0), SB)
    c = lax.broadcasted_iota(jnp.int32, (4 * SB, SB), 1)
    real = b < nsb
    return c <= r, (c > r) & (b > 0) & real, (c < NM) & real


def _swa_specs(nsb):
    def rows(h, w, cb, f):
        return pl.BlockSpec((h, w), lambda i: (f(i), cb))
    pair = lambda i: i
    prev = lambda i: jnp.maximum(2 * i - 1, 0)
    meta = lambda i: nsb
    return rows, pair, prev, meta


def _swa_scores(b, nsb, sink_ref, q, kc, kp, km):
    mc, mp, mm = _swa_masks(b, nsb)
    per_kv = []
    for kv in range(2):
        ks = slice(kv * 64, (kv + 1) * 64)
        qg = _bf(_stack(q[:, kv * 256:(kv + 1) * 256]))
        kcb, kpb, kmb = _bf(kc[:, ks]), _bf(kp[:, ks]), _bf(km[:, ks])
        s_c = jnp.where(mc, _dot_nt(qg, kcb), NEG)
        s_p = jnp.where(mp, _dot_nt(qg, kpb), NEG)
        s_m = jnp.where(mm, _dot_nt(qg, kmb), NEG)
        sink = jnp.concatenate([jnp.full((SB, 1), sink_ref[0, kv * 4 + g], F32) for g in range(4)], axis=0)
        per_kv.append((qg, kcb, kpb, kmb, s_c, s_p, s_m, sink))
    return per_kv


def _swa_fwd(proj, sinks, t, comm=None):
    nsb = t // SB
    r_tot = t + TM
    rows, pair, prev, meta = _swa_specs(nsb)

    def body(sink_ref, q_ref, kc_ref, kp_ref, km_ref, vc_ref, vp_ref, vm_ref, o_ref, lse_ref):
        i = pl.program_id(0)
        km, vm = km_ref[...], vm_ref[...]
        for j in range(2):
            b = 2 * i + j
            rs = slice(j * SB, (j + 1) * SB)
            kp = kp_ref[...] if j == 0 else kc_ref[0:SB, :]
            vp = vp_ref[...] if j == 0 else vc_ref[0:SB, :]
            vc = vc_ref[rs, :]
            o_l, lse_l = [], []
            for kv, (qg, kcb, kpb, kmb, s_c, s_p, s_m, sink) in enumerate(
                    _swa_scores(b, nsb, sink_ref, q_ref[rs, :], kc_ref[rs, :], kp, km)):
                ks = slice(kv * 64, (kv + 1) * 64)
                m = jnp.maximum(jnp.max(jnp.maximum(jnp.maximum(s_c, s_p), s_m), -1, keepdims=True), sink)
                p_c, p_p, p_m = jnp.exp(s_c - m), jnp.exp(s_p - m), jnp.exp(s_m - m)
                l = jnp.sum(p_c + p_p + p_m, -1, keepdims=True) + jnp.exp(sink - m)
                o = _dot(_bf(p_c), _bf(vc[:, ks])) + _dot(_bf(p_p), _bf(vp[:, ks])) + _dot(_bf(p_m), _bf(vm[:, ks]))
                o_l.append(_unstack(o * (1.0 / l)))
                lse_l.append(_unstack(m + jnp.log(l)))
            valid = b * SB + lax.broadcasted_iota(jnp.int32, (SB, 1), 0) < t + NM
            o_ref[rs, :] = _bf(jnp.where(valid, jnp.concatenate(o_l, axis=1), 0.0))
            lse_ref[rs, :] = jnp.concatenate(lse_l, axis=1)

    ck, cv = C_SK // 128, C_SV // 128
    return _call(
        body, "swa_fwd", (r_tot // QB,),
        [SMEM_SPEC, rows(QB, 512, C_SQ // 512, pair),
         rows(QB, 128, ck, pair), rows(SB, 128, ck, prev), rows(SB, 128, ck, meta),
         rows(QB, 128, cv, pair), rows(SB, 128, cv, prev), rows(SB, 128, cv, meta)],
        [rows(QB, 512, 0, pair), rows(QB, 8, 0, pair)],
        [jax.ShapeDtypeStruct((r_tot, 512), BF16), jax.ShapeDtypeStruct((r_tot, 8), F32)],
        [], _cp(32), (sinks, proj, proj, proj, proj, proj, proj, proj), comm)


def _swa_bwd(proj, sinks, lse, do, t, comm=None):
    nsb = t // SB
    r_tot = t + TM
    rows, pair, prev, meta = _swa_specs(nsb)

    def body(sink_ref, q_ref, kc_ref, kp_ref, km_ref, vc_ref, vp_ref, vm_ref, lse_ref, do_ref,
             dq_ref, dk_ref, dv_ref, dsink_ref):
        i = pl.program_id(0)

        @pl.when(i == 0)
        def _():
            dk_ref[...] = jnp.zeros_like(dk_ref)
            dv_ref[...] = jnp.zeros_like(dv_ref)
            dsink_ref[...] = jnp.zeros_like(dsink_ref)

        km, vm = km_ref[...], vm_ref[...]
        dsink = jnp.zeros((1, 8), F32)
        for j in range(2):
            b = 2 * i + j
            rs = slice(j * SB, (j + 1) * SB)
            kp = kp_ref[...] if j == 0 else kc_ref[0:SB, :]
            vp = vp_ref[...] if j == 0 else vc_ref[0:SB, :]
            vc = vc_ref[rs, :]
            lse_all = lse_ref[rs, :]
            dq_l, dkc_l, dkp_l, dkm_l, dvc_l, dvp_l, dvm_l, ds_l = [], [], [], [], [], [], [], []
            for kv, (qg, kcb, kpb, kmb, s_c, s_p, s_m, sink) in enumerate(
                    _swa_scores(b, nsb, sink_ref, q_ref[rs, :], kc_ref[rs, :], kp, km)):
                ks = slice(kv * 64, (kv + 1) * 64)
                lse_g = _stack_cols(lse_all[:, kv * 4:(kv + 1) * 4])
                dog = _bf(_stack(do_ref[rs, kv * 256:(kv + 1) * 256]))
                vcb, vpb, vmb = _bf(vc[:, ks]), _bf(vp[:, ks]), _bf(vm[:, ks])
                p_c, p_p, p_m = jnp.exp(s_c - lse_g), jnp.exp(s_p - lse_g), jnp.exp(s_m - lse_g)
                dp_c, dp_p, dp_m = _dot_nt(dog, vcb), _dot_nt(dog, vpb), _dot_nt(dog, vmb)
                delta = jnp.sum(p_c * dp_c + p_p * dp_p + p_m * dp_m, -1, keepdims=True)
                ds_c, ds_p, ds_m = _bf(p_c * (dp_c - delta)), _bf(p_p * (dp_p - delta)), _bf(p_m * (dp_m - delta))
                dq_l.append(_unstack(_dot(ds_c, kcb) + _dot(ds_p, kpb) + _dot(ds_m, kmb)))
                dkc_l.append(_dot_tn(ds_c, qg))
                dkp_l.append(_dot_tn(ds_p, qg))
                dkm_l.append(_dot_tn(ds_m, qg))
                dvc_l.append(_dot_tn(_bf(p_c), dog))
                dvp_l.append(_dot_tn(_bf(p_p), dog))
                dvm_l.append(_dot_tn(_bf(p_m), dog))
                ds_l.append(_unstack(-jnp.exp(sink - lse_g) * delta))
            dq_ref[rs, :] = jnp.concatenate(dq_l, axis=1)
            c0 = pl.multiple_of(b * SB, SB)
            p0 = pl.multiple_of(jnp.maximum(b - 1, 0) * SB, SB)
            dk_ref[pl.ds(c0, SB), :] += jnp.concatenate(dkc_l, axis=1)
            dk_ref[pl.ds(p0, SB), :] += jnp.concatenate(dkp_l, axis=1)
            dk_ref[pl.ds(t, SB), :] += jnp.concatenate(dkm_l, axis=1)
            dv_ref[pl.ds(c0, SB), :] += jnp.concatenate(dvc_l, axis=1)
            dv_ref[pl.ds(p0, SB), :] += jnp.concatenate(dvp_l, axis=1)
            dv_ref[pl.ds(t, SB), :] += jnp.concatenate(dvm_l, axis=1)
            dsink = dsink + jnp.sum(jnp.concatenate(ds_l, axis=1), axis=0, keepdims=True)
        dsink_ref[...] += dsink

    ck, cv = C_SK // 128, C_SV // 128
    whole = lambda w: pl.BlockSpec((r_tot, w), lambda i: (0, 0))
    return _call(
        body, "swa_bwd", (r_tot // QB,),
        [SMEM_SPEC, rows(QB, 512, C_SQ // 512, pair),
         rows(QB, 128, ck, pair), rows(SB, 128, ck, prev), rows(SB, 128, ck, meta),
         rows(QB, 128, cv, pair), rows(SB, 128, cv, prev), rows(SB, 128, cv, meta),
         rows(QB, 8, 0, pair), rows(QB, 512, 1, pair)],
        [rows(QB, 512, 0, pair), whole(128), whole(128), pl.BlockSpec((1, 8), lambda i: (0, 0))],
        [jax.ShapeDtypeStruct((r_tot, 512), F32), jax.ShapeDtypeStruct((r_tot, 128), F32),
         jax.ShapeDtypeStruct((r_tot, 128), F32), jax.ShapeDtypeStruct((1, 8), F32)],
        [], _cp(48), (sinks, proj, proj, proj, proj, proj, proj, proj, lse, do), comm)


def _stack_cols(x):
    return jnp.concatenate([x[:, g:g + 1] for g in range(4)], axis=0)


def _mlp_fwd(x, metapad, tgt, ogla, oswa, wo, wff, w1, w2, wfin):
    t = x.shape[0]
    nblk = t // TM

    def body(x_ref, mp_ref, tgt_ref, og_ref, os_ref, wo_ref, wff_ref, w1_ref, w2_ref, wfin_ref,
             h1_ref, f_ref, a_ref, dh2_ref, loss_ref, gfin_ref):
        i = pl.program_id(0)

        @pl.when(i == 0)
        def _():
            loss_ref[...] = jnp.zeros_like(loss_ref)
            gfin_ref[...] = jnp.zeros_like(gfin_ref)

        h0 = jnp.where(i == nblk, mp_ref[...], x_ref[...])
        h1 = h0 + _dot(og_ref[...], wo_ref[0:512, :]) + _dot(os_ref[...], wo_ref[512:1024, :])
        h1_ref[...] = h1
        fh, _ = _rms(h1)
        f = _bf(fh * wff_ref[...])
        f_ref[...] = f
        acc = jnp.zeros((TM, D), F32)
        for n in range(4):
            a = _dot(f, w1_ref[n])
            a_ref[:, n * D:(n + 1) * D] = _bf(a)
            zr = jnp.maximum(a, 0.0)
            acc = acc + _dot(_bf(zr * zr), w2_ref[n])
        h2 = h1 + acc
        yh, rs2 = _rms(h2)
        wf = wfin_ref[...]
        real = i < nblk
        e = jnp.where(real, yh * wf - tgt_ref[...], 0.0)
        loss_ref[...] += jnp.sum(jnp.sum(e * e, axis=0, keepdims=True), axis=1, keepdims=True) * (0.5 / D)
        dy = e * (1.0 / D)
        gfin_ref[...] += jnp.sum(dy * yh, axis=0, keepdims=True)
        dh2_ref[...] = _rms_bwd(dy, yh, rs2, wf)

    xs = pl.BlockSpec((TM, D), lambda i: (jnp.minimum(i, nblk - 1), 0))
    rs = lambda w: pl.BlockSpec((TM, w), lambda i: (i, 0))
    r_tot = t + TM
    return pl.pallas_call(
        body, name="mlp_fwd", grid=(nblk + 1,),
        in_specs=[xs, VMEM_SPEC, xs, rs(512), rs(512), VMEM_SPEC, VMEM_SPEC, VMEM_SPEC, VMEM_SPEC, VMEM_SPEC],
        out_specs=[rs(D), rs(D), rs(DFF), rs(D), pl.BlockSpec((1, 1), lambda i: (0, 0)), pl.BlockSpec((1, D), lambda i: (0, 0))],
        out_shape=[jax.ShapeDtypeStruct((r_tot, D), F32), jax.ShapeDtypeStruct((r_tot, D), BF16),
                   jax.ShapeDtypeStruct((r_tot, DFF), BF16), jax.ShapeDtypeStruct((r_tot, D), F32),
                   jax.ShapeDtypeStruct((1, 1), F32), jax.ShapeDtypeStruct((1, D), F32)],
        compiler_params=_cp(56),
    )(x, metapad, tgt, ogla, oswa, wo, wff, w1, w2, wfin)


def _mlp_bwd(h1, a, dh2, ogla, oswa, wo, wff, w1, w2):
    r_tot = h1.shape[0]
    nt = r_tot // TM

    def body(h1_ref, a_ref, dh2_ref, og_ref, os_ref, wo_ref, wff_ref, w1_ref, w2_ref,
             da_ref, dh2b_ref, dh1_ref, do_ref, dwo_ref, gff_ref):
        i = pl.program_id(0)

        @pl.when(i == 0)
        def _():
            dwo_ref[...] = jnp.zeros_like(dwo_ref)
            gff_ref[...] = jnp.zeros_like(gff_ref)

        dh2 = dh2_ref[...]
        dh2b = _bf(dh2)
        dh2b_ref[...] = dh2b
        df = jnp.zeros((TM, D), F32)
        for n in range(4):
            dz = _dot_nt(dh2b, w2_ref[n])
            da = _bf(dz * (2.0 * jnp.maximum(a_ref[:, n * D:(n + 1) * D].astype(F32), 0.0)))
            da_ref[:, n * D:(n + 1) * D] = da
            df = df + _dot_nt(da, w1_ref[n])
        fh, rs1 = _rms(h1_ref[...])
        gff_ref[...] += jnp.sum(df * fh, axis=0, keepdims=True)
        dh1 = dh2 + _rms_bwd(df, fh, rs1, wff_ref[...])
        dh1_ref[...] = dh1
        dh1b = _bf(dh1)
        do_ref[...] = _dot_nt(dh1b, wo_ref[...])
        dwo_ref[0:512, :] += _dot_tn(og_ref[...], dh1b)
        dwo_ref[512:1024, :] += _dot_tn(os_ref[...], dh1b)

    rs = lambda w: pl.BlockSpec((TM, w), lambda i: (i, 0))
    return pl.pallas_call(
        body, name="mlp_bwd", grid=(nt,),
        in_specs=[rs(D), rs(DFF), rs(D), rs(512), rs(512), VMEM_SPEC, VMEM_SPEC, VMEM_SPEC, VMEM_SPEC],
        out_specs=[rs(DFF), rs(D), rs(D), rs(D), pl.BlockSpec((D, D), lambda i: (0, 0)),
                   pl.BlockSpec((1, D), lambda i: (0, 0))],
        out_shape=[jax.ShapeDtypeStruct((r_tot, DFF), BF16), jax.ShapeDtypeStruct((r_tot, D), BF16),
                   jax.ShapeDtypeStruct((r_tot, D), F32), jax.ShapeDtypeStruct((r_tot, D), F32),
                   jax.ShapeDtypeStruct((D, D), F32), jax.ShapeDtypeStruct((1, D), F32)],
        compiler_params=_cp(56),
    )(h1, a, dh2, ogla, oswa, wo, wff, w1, w2)


def _ffn_wgrad(f, a, da, dh2b):
    r_tot = f.shape[0]
    kt = 768 if r_tot % 768 == 0 else TM
    nk = r_tot // kt

    def body(f_ref, a_ref, da_ref, dh2_ref, dw1_ref, dw2_ref, acc1, acc2):
        k = pl.program_id(1)

        @pl.when(k == 0)
        def _():
            acc1[...] = jnp.zeros_like(acc1)
            acc2[...] = jnp.zeros_like(acc2)

        zr = jnp.maximum(a_ref[...], 0.0)
        acc1[...] += _dot_tn(f_ref[...], da_ref[...])
        acc2[...] += _dot_tn(zr * zr, dh2_ref[...])

        @pl.when(k == nk - 1)
        def _():
            for hh in range(2):
                dw1_ref[hh, 0] = acc1[hh * 512:(hh + 1) * 512, :]
                dw2_ref[hh, 0] = acc2[hh * 512:(hh + 1) * 512, :]

    full = pl.BlockSpec((kt, D), lambda n, k: (k, 0))
    col = pl.BlockSpec((kt, D), lambda n, k: (k, n))
    out = pl.BlockSpec((2, 1, 512, D), lambda n, k: (0, n, 0, 0))
    return pl.pallas_call(
        body, name="ffn_wgrad", grid=(4, nk),
        in_specs=[full, col, col, full], out_specs=[out, out],
        out_shape=[jax.ShapeDtypeStruct((2, 4, 512, D), F32)] * 2,
        scratch_shapes=[pltpu.VMEM((D, D), F32), pltpu.VMEM((D, D), F32)],
        compiler_params=_cp(48, ("arbitrary", "arbitrary")),
    )(f, a, da, dh2b)


def _proj_bwd(x, metapad, wm, wt, tabs, dgla, dswa_q, dsk, dsv, dlr, dh1, comm=None):
    t = x.shape[0]
    nblk = t // TM

    def body(x_ref, mp_ref, wm_ref, w_ref, tab_ref, dg_ref, dq_ref, dk_ref, dv_ref, dlr_ref, dh1_ref,
             gx_ref, gmeta_ref, dw_ref, gmix_ref):
        i = pl.program_id(0)

        @pl.when(i == 0)
        def _():
            dw_ref[...] = jnp.zeros_like(dw_ref)
            gmix_ref[...] = jnp.zeros_like(gmix_ref)

        h = jnp.where(i == nblk, mp_ref[...], x_ref[...])
        uh, rs = _rms(h)
        wm_v = wm_ref[...]
        u = _bf(uh * wm_v)
        tab = tab_ref[...]
        dq = _bf(_rope(dq_ref[...] * 0.125, tab, -1.0))
        dk = _bf(_rope(dk_ref[...], tab, -1.0))
        parts = ((dg_ref[...], 0, R_LR), (dlr_ref[:, 0:16], R_LR, 16), (dq, R_LR + 16, 512),
                 (dk, R_LR + 528, 128), (_bf(dv_ref[...]), R_LR + 656, 128))
        du = jnp.zeros((TM, D), F32)
        for val, r0, w in parts:
            du = du + _dot(val, w_ref[r0:r0 + w, :])
            dw_ref[r0:r0 + w, :] += _dot_tn(val, u)
        gmix_ref[...] += jnp.sum(du * uh, axis=0, keepdims=True)
        dh0 = dh1_ref[...] + _rms_bwd(du, uh, rs, wm_v)

        @pl.when(i < nblk)
        def _():
            gx_ref[...] = dh0

        @pl.when(i == nblk)
        def _():
            gmeta_ref[...] = dh0[:NM]

    xs = pl.BlockSpec((TM, D), lambda i: (jnp.minimum(i, nblk - 1), 0))
    rs_ = lambda w: pl.BlockSpec((TM, w), lambda i: (i, 0))
    return _call(
        body, "proj_bwd", (nblk + 1,),
        [xs, VMEM_SPEC, VMEM_SPEC, VMEM_SPEC, rs_(128), rs_(1536), rs_(512), rs_(128), rs_(128), rs_(128), rs_(D)],
        [xs, pl.BlockSpec((NM, D), lambda i: (0, 0)), pl.BlockSpec((DIN, D), lambda i: (0, 0)),
         pl.BlockSpec((1, D), lambda i: (0, 0))],
        [jax.ShapeDtypeStruct((t, D), F32), jax.ShapeDtypeStruct((NM, D), F32),
         jax.ShapeDtypeStruct((DIN, D), F32), jax.ShapeDtypeStruct((1, D), F32)],
        [], _cp(56), (x, metapad, wm, wt, tabs, dgla, dswa_q, dsk, dsv, dlr, dh1), comm)


def _place():
    return lax.axis_index("x"), lax.axis_index("y"), lax.axis_index("c")


def _other_chips(x, y):
    return [(1 - x, y), (x, 1 - y), (1 - x, 1 - y)]


def _dma_sems(*counts):
    return tuple(pltpu.SemaphoreType.DMA((k,)) for k in counts)


def _gather_shards(shards):
    n = len(shards)

    def plan(ins, outs, sems):
        send, recv, loc = sems
        x, y, c = _place()
        chips = _other_chips(x, y)

        def remote(a, k, shard_of):
            tx, ty = chips[k]
            sx, sy = shard_of
            return pltpu.make_async_remote_copy(
                src_ref=ins[a], dst_ref=outs[a].at[2 * sx + sy], send_sem=send.at[3 * a + k], recv_sem=recv.at[3 * a + k],
                device_id=(tx, ty, c), device_id_type=MESH)

        local = [pltpu.make_async_copy(ins[a], outs[a].at[2 * x + y], loc.at[a]) for a in range(n)]
        sends = [remote(a, k, (x, y)) for a in range(n) for k in range(3)]
        waits = ([lambda a=a, k=k: remote(a, k, chips[k]).wait_recv() for a in range(n) for k in range(3)]
                 + [cp.wait_send for cp in sends] + [cp.wait for cp in local])
        return local + sends, waits

    return _Comm(tuple(shards), tuple(jax.ShapeDtypeStruct((4,) + s.shape, s.dtype) for s in shards),
                 _dma_sems(3 * n, 3 * n, n), plan)


def _swap_halves(grads):
    n = len(grads)

    def plan(ins, outs, sems):
        send, recv = sems
        x, y, c = _place()
        cps = [pltpu.make_async_remote_copy(
            src_ref=ins[a].at[1 - c], dst_ref=outs[a], send_sem=send.at[a], recv_sem=recv.at[a],
            device_id=(x, y, 1 - c), device_id_type=MESH) for a in range(n)]
        return cps, [cp.wait for cp in cps]

    return _Comm(tuple(grads), tuple(jax.ShapeDtypeStruct(g.shape[1:], g.dtype) for g in grads), _dma_sems(n, n), plan)


def _scatter_shards(parts):
    n = len(parts)

    def plan(ins, outs, sems):
        send, recv = sems
        x, y, c = _place()
        cps = [pltpu.make_async_remote_copy(
            src_ref=ins[a].at[2 * tx + ty], dst_ref=outs[a].at[k], send_sem=send.at[3 * a + k],
            recv_sem=recv.at[3 * a + k], device_id=(tx, ty, c), device_id_type=MESH)
            for a in range(n) for k, (tx, ty) in enumerate(_other_chips(x, y))]
        return cps, [cp.wait for cp in cps]

    return _Comm(tuple(parts), tuple(jax.ShapeDtypeStruct((3,) + p.shape[1:], p.dtype) for p in parts),
                 _dma_sems(3 * n, 3 * n), plan)


def _join_halves(halves):
    n = len(halves)

    def plan(ins, outs, sems):
        send, recv, loc = sems
        x, y, c = _place()

        def remote(a, half):
            return pltpu.make_async_remote_copy(
                src_ref=ins[a], dst_ref=outs[a].at[half], send_sem=send.at[a], recv_sem=recv.at[a],
                device_id=(x, y, 1 - c), device_id_type=MESH)

        local = [pltpu.make_async_copy(ins[a], outs[a].at[c], loc.at[a]) for a in range(n)]
        sends = [remote(a, c) for a in range(n)]
        waits = ([lambda a=a: remote(a, 1 - c).wait_recv() for a in range(n)] + [cp.wait_send for cp in sends]
                 + [cp.wait for cp in local])
        return local + sends, waits

    return _Comm(tuple(halves), tuple(jax.ShapeDtypeStruct((2,) + h.shape, h.dtype) for h in halves),
                 _dma_sems(n, n, n), plan)


def _reduce_w_in(parts, comm):
    rows, cols = parts.shape[1:]
    ci, co = len(comm.ins), len(comm.outs)

    def body(*refs):
        parts_ref, c_in, out_ref, c_out = refs[0], refs[1:1 + ci], refs[1 + ci], refs[2 + ci:2 + ci + co]
        mine, tosend, rbuf, qbuf, sib, send, recv, loc = refs[2 + ci + co:10 + ci + co]
        c_sem = refs[10 + ci + co:]
        x, y, c = _place()
        starts, waits = comm.plan(c_in, c_out, c_sem)
        for cp in starts:
            cp.start()
        load = pltpu.make_async_copy(parts_ref, mine, loc.at[0])
        load.start()
        load.wait()
        cps = []
        for k, (tx, ty) in enumerate(_other_chips(x, y)):
            tosend[k] = _bf(mine[2 * tx + ty])
            cps.append(pltpu.make_async_remote_copy(
                src_ref=tosend.at[k], dst_ref=rbuf.at[k], send_sem=send.at[k], recv_sem=recv.at[k],
                device_id=(tx, ty, c), device_id_type=MESH))
            cps[-1].start()
        for cp in cps:
            cp.wait()
        qbuf[...] = mine[2 * x + y] + rbuf[0].astype(F32) + rbuf[1].astype(F32) + rbuf[2].astype(F32)
        swap = pltpu.make_async_remote_copy(src_ref=qbuf, dst_ref=sib, send_sem=send.at[3], recv_sem=recv.at[3],
                                            device_id=(x, y, 1 - c), device_id_type=MESH)
        swap.start()
        swap.wait()
        out_ref[...] = qbuf[...] + sib[...]
        for wait in waits:
            wait()

    outs = pl.pallas_call(
        body, name="reduce_w_in",
        in_specs=[ANY_SPEC] * (1 + ci), out_specs=[VMEM_SPEC] + [ANY_SPEC] * co,
        out_shape=[jax.ShapeDtypeStruct((rows, cols), F32)] + list(comm.outs),
        scratch_shapes=[pltpu.VMEM((4, rows, cols), F32), pltpu.VMEM((3, rows, cols), BF16), pltpu.VMEM((3, rows, cols), BF16),
                        pltpu.VMEM((rows, cols), F32), pltpu.VMEM((rows, cols), F32), *_dma_sems(4, 4, 1), *comm.sems],
        compiler_params=pltpu.CompilerParams(vmem_limit_bytes=48 << 20),
    )(parts, *comm.ins)
    return outs[0], outs[1:]


def _allreduce_small(pack):
    p = pack.shape[0]

    def body(in_ref, out_ref, buf, send, recv):
        x, y, c = _place()
        me = 4 * x + 2 * y + c
        buf[me] = in_ref[...]
        sends = []
        for k in range(1, 8):
            tx, ty, tc = x ^ (k >> 2), y ^ ((k >> 1) & 1), c ^ (k & 1)
            sends.append(pltpu.make_async_remote_copy(
                src_ref=in_ref, dst_ref=buf.at[me], send_sem=send.at[k - 1], recv_sem=recv.at[k - 1],
                device_id=(tx, ty, tc), device_id_type=MESH))
        for cp in sends:
            cp.start()
        for k in range(1, 8):
            peer = 4 * (x ^ (k >> 2)) + 2 * (y ^ ((k >> 1) & 1)) + (c ^ (k & 1))
            pltpu.make_async_remote_copy(
                src_ref=in_ref, dst_ref=buf.at[peer], send_sem=send.at[k - 1], recv_sem=recv.at[k - 1],
                device_id=(x, y, c), device_id_type=MESH).wait_recv()
        for cp in sends:
            cp.wait_send()
        acc = buf[0]
        for d in range(1, 8):
            acc = acc + buf[d]
        out_ref[...] = acc

    return pl.pallas_call(
        body, name="allreduce_small",
        in_specs=[VMEM_SPEC], out_specs=VMEM_SPEC, out_shape=jax.ShapeDtypeStruct(pack.shape, F32),
        scratch_shapes=[pltpu.VMEM((8, p, D), F32), pltpu.SemaphoreType.DMA((7,)), pltpu.SemaphoreType.DMA((7,))],
    )(pack)


GRID4 = 4


def _sum_parts(sel, firsts, others, name, also_bf16):
    n = len(firsts)
    nk = others[0].shape[0]

    def body(sel_ref, *refs):
        fs, os_, outs = refs[:n], refs[n:2 * n], refs[2 * n:]
        for a in range(n):
            acc = fs[a][0]
            for k in range(nk):
                acc = acc + os_[a][k].astype(F32)
            outs[a][...] = acc
            if also_bf16:
                outs[n + a][...] = _bf(acc)

    def rows(a):
        return firsts[a].shape[1] // GRID4

    in_specs = ([pl.BlockSpec((1, rows(a), firsts[a].shape[2]), lambda i, s: (s[0], i, 0)) for a in range(n)]
                + [pl.BlockSpec((nk, rows(a), firsts[a].shape[2]), lambda i, s: (0, i, 0)) for a in range(n)])
    out_specs = [pl.BlockSpec((rows(a), firsts[a].shape[2]), lambda i, s: (i, 0)) for a in range(n)]
    out_shape = [jax.ShapeDtypeStruct(f.shape[1:], F32) for f in firsts]
    if also_bf16:
        out_specs = out_specs * 2
        out_shape = out_shape + [jax.ShapeDtypeStruct(f.shape[1:], BF16) for f in firsts]
    outs = pl.pallas_call(
        body, name=name,
        grid_spec=pltpu.PrefetchScalarGridSpec(num_scalar_prefetch=1, grid=(GRID4,), in_specs=in_specs, out_specs=out_specs),
        out_shape=out_shape, compiler_params=_cp(48),
    )(sel, *firsts, *others)
    return outs[:n], outs[n:]


def _adamw_math(w, g, m, v):
    m2 = ADAM_B1 * m + (1.0 - ADAM_B1) * g
    v2 = ADAM_B2 * v + (1.0 - ADAM_B2) * (g * g)
    m_hat = m2 / (1.0 - ADAM_B1 ** ADAM_STEP)
    v_hat = v2 / (1.0 - ADAM_B2 ** ADAM_STEP)
    return -ADAM_LR * (m_hat / (jnp.sqrt(v_hat) + ADAM_EPS) + ADAM_WD * w), m2, v2


def _adamw_big(ws, gs, ms, vs):
    n = len(ws)

    def body(*refs):
        for a in range(n):
            d, m2, v2 = _adamw_math(refs[a][...], refs[n + a][...], refs[2 * n + a][...], refs[3 * n + a][...])
            refs[4 * n + a][...] = d
            refs[5 * n + a][...] = m2
            refs[6 * n + a][...] = v2

    specs = [pl.BlockSpec((w.shape[0] // GRID4, w.shape[1]), lambda i: (i, 0)) for w in ws]
    return pl.pallas_call(
        body, name="adamw_big", grid=(GRID4,),
        in_specs=specs * 4, out_specs=specs * 3,
        out_shape=[jax.ShapeDtypeStruct(w.shape, F32) for w in ws] * 3,
        compiler_params=_cp(48),
    )(*ws, *gs, *ms, *vs)


def _adamw_small(ws, gs, ms, vs):
    n = len(ws)

    def body(*refs):
        for a in range(n):
            d, m2, v2 = _adamw_math(refs[a][...], refs[n + a][...], refs[2 * n + a][...], refs[3 * n + a][...])
            refs[4 * n + a][...] = d
            refs[5 * n + a][...] = m2
            refs[6 * n + a][...] = v2

    return pl.pallas_call(
        body, name="adamw_small",
        in_specs=[VMEM_SPEC] * (4 * n), out_specs=[VMEM_SPEC] * (3 * n),
        out_shape=[jax.ShapeDtypeStruct(w.shape, F32) for w in ws] * 3,
        compiler_params=pltpu.CompilerParams(vmem_limit_bytes=40 << 20),
    )(*ws, *gs, *ms, *vs)


def kernel(x, meta_tokens, norm_mix_w, w_in, w_gate_up, b_gate, gla_norm_w, sinks, w_out, norm_ff_w, w_ff1, w_ff2, final_norm_w, loss_target, m_meta_tokens, m_norm_mix_w, m_w_in, m_w_gate_up, m_b_gate, m_gla_norm_w, m_sinks, m_w_out, m_norm_ff_w, m_w_ff1, m_w_ff2, m_final_norm_w, v_meta_tokens, v_norm_mix_w, v_w_in, v_w_gate_up, v_b_gate, v_gla_norm_w, v_sinks, v_w_out, v_norm_ff_w, v_w_ff1, v_w_ff2, v_final_norm_w):
    xi, yi, ci = _place()
    shard = (2 * xi + yi).astype(jnp.int32).reshape(1)
    core = ci.astype(jnp.int32).reshape(1)

    small = jnp.concatenate([meta_tokens, w_gate_up[0], jnp.zeros((NM, 64), F32)], axis=1)
    g_in, g_small = _run_comm(_gather_shards([_bf(w_in[0].T), small]), "gather_w_in")
    wt = g_in.reshape(DIN, D)
    meta = g_small[:, :, 0:256].transpose(1, 0, 2).reshape(NM, D)
    wgu = g_small[:, :, 256:320].transpose(1, 0, 2).reshape(NM, 256)

    xs, tgt = x[0], loss_target[0]
    t = xs.shape[0]
    wfin = final_norm_w.reshape(1, D)
    metapad = jnp.concatenate([meta, jnp.zeros((TM - NM, D), F32)], axis=0)
    wgu_p = _bf(jnp.concatenate([wgu, jnp.zeros((128 - 16, 256), F32)], axis=0))
    tabs = _rope_tables(t)

    proj = _proj_fwd(xs, metapad, norm_mix_w, wt, tabs)
    (oswa, lse), (w1, w2) = _swa_fwd(proj, sinks, t, _gather_shards([_bf(w_ff1[0]), _bf(w_ff2[0])]))
    (ogla, oraw, sst), (g_out,) = _gla_fwd(proj, wgu_p, b_gate, gla_norm_w, t, _gather_shards([_bf(w_out[0])]))
    wo = g_out.reshape(D, D)
    h1, f, a, dh2, loss, gfin = _mlp_fwd(xs, metapad, tgt, ogla, oswa, wo, norm_ff_w, w1, w2, wfin)

    da, dh2b, dh1, do, dwo, gff = _mlp_bwd(h1, a, dh2, ogla, oswa, wo, norm_ff_w, w1, w2)
    dw1, dw2 = _ffn_wgrad(f, a, da, dh2b)
    big = [dwo.reshape(4, 2, 128, D).transpose(1, 0, 2, 3), dw1, dw2]
    (dsq, dsk, dsv, dsink), theirs = _swa_bwd(proj, sinks, lse, do, t, _swap_halves(big))
    sums, sums_bf = _sum_parts(core, [b.reshape((2, -1) + b.shape[3:]) for b in big],
                               [s.reshape((1, -1) + s.shape[2:]) for s in theirs], "sum_cores", True)
    sums = [s.reshape(b.shape[1:]) for s, b in zip(sums, big)]
    sums_bf = [s.reshape(b.shape[1:]) for s, b in zip(sums_bf, big)]
    (dgla, dlr, dwgu, dbg, dgnw), arrived = _gla_bwd(proj, oraw, sst, do, wgu_p, b_gate, gla_norm_w, t,
                                                     _scatter_shards(sums_bf))
    halves, _ = _sum_parts(shard, sums, arrived, "sum_chips", False)
    (gx, gmeta, dwt, gmix), _ = _proj_bwd(xs, metapad, norm_mix_w, wt, tabs, dgla, dsq, dsk, dsv, dlr, dh1)
    gwt_in, joined = _reduce_w_in(dwt.reshape(4, DIN // 4, D), _join_halves(halves))
    gw_out, gw_1, gw_2 = [j.reshape((-1, j.shape[2])) for j in joined]
    g = dict(meta=gmeta, mix=gmix, wgu=dwgu[:16], bg=dbg, gnw=dgnw, sinks=dsink, ff=gff, fin=gfin)

    tail = jnp.concatenate([g["bg"], g["gnw"], g["sinks"], loss, jnp.zeros((1, D - 256 - 128 - 8 - 1), F32)], axis=1)
    pack = jnp.concatenate([g["meta"], g["mix"], g["ff"], g["fin"], tail, g["wgu"].reshape(4, D)], axis=0)
    tot = _allreduce_small(pack)
    g_meta = lax.dynamic_slice_in_dim(tot[0:NM], shard[0] * 256, 256, axis=1)
    g_mix, g_ff, g_fin = tot[16:17], tot[17:18], tot[18]
    g_bg, g_gnw, g_sinks, loss_tot = tot[19:20, 0:256], tot[19:20, 256:384], tot[19:20, 384:392], tot[19, 392]
    g_wgu = lax.dynamic_slice_in_dim(tot[20:24].reshape(NM, 256), shard[0] * 64, 64, axis=1)

    bo = _adamw_big([w_out[0], w_ff1[0], w_ff2[0]], [gw_out, gw_1, gw_2], [m_w_out[0], m_w_ff1[0], m_w_ff2[0]],
                    [v_w_out[0], v_w_ff1[0], v_w_ff2[0]])

    fin2 = lambda a: a.reshape(1, D)
    sw = [meta_tokens, norm_mix_w, w_gate_up[0], b_gate, gla_norm_w, sinks, norm_ff_w, fin2(final_norm_w), w_in[0].T]
    sg = [g_meta, g_mix, g_wgu, g_bg, g_gnw, g_sinks, g_ff, fin2(g_fin), gwt_in]
    sm = [m_meta_tokens, m_norm_mix_w, m_w_gate_up[0], m_b_gate, m_gla_norm_w, m_sinks, m_norm_ff_w, fin2(m_final_norm_w),
          m_w_in[0].T]
    sv = [v_meta_tokens, v_norm_mix_w, v_w_gate_up[0], v_b_gate, v_gla_norm_w, v_sinks, v_norm_ff_w, fin2(v_final_norm_w),
          v_w_in[0].T]
    so = _adamw_small(sw, sg, sm, sv)

    def ordered(small_o, big_o):
        meta_, mix_, wgu_, bg_, gnw_, sinks_, ff_, fin_, wt_ = small_o
        w_out_, w_1_, w_2_ = big_o
        return (meta_, mix_, wt_.T[None], wgu_[None], bg_, gnw_, sinks_, w_out_[None], ff_, w_1_[None], w_2_[None],
                fin_.reshape(D))

    grads = ordered(sg, [gw_out, gw_1, gw_2])
    deltas = ordered(so[0:9], bo[0:3])
    new_m = ordered(so[9:18], bo[3:6])
    new_v = ordered(so[18:27], bo[6:9])
    return (loss_tot, gx[None], *grads, *deltas, *new_m, *new_v)
```

```python
from typing import Callable, NamedTuple

import jax
import jax.numpy as jnp
import numpy as np
from jax import lax
from jax.experimental import pallas as pl
from jax.experimental.pallas import tpu as pltpu

F32 = jnp.float32
BF16 = jnp.bfloat16

D = 1024
DFF = 4096
NM = 16
TM = 256
CH = 64
SB = 128
QB = 2 * SB
EPS = 1e-5
C_GQ, C_GK, C_GV, C_GR, C_SQ, C_SK, C_SV, C_LR, DINP = 0, 256, 512, 1024, 1536, 2048, 2176, 2304, 2432
DIN = 2320
R_LR = 1536
ROPE_THETA = 500000.0
ADAM_LR, ADAM_B1, ADAM_B2, ADAM_EPS, ADAM_WD, ADAM_STEP = 0.001, 0.9, 0.999, 1e-08, 0.01, 10
NEG = -1e30
MESH = pl.DeviceIdType.MESH
VMEM_SPEC = pl.BlockSpec(memory_space=pltpu.VMEM)
ANY_SPEC = pl.BlockSpec(memory_space=pl.ANY)
SMEM_SPEC = pl.BlockSpec(memory_space=pltpu.SMEM)


def _cp(vmem_mb, sem=("arbitrary",)):
    return pltpu.CompilerParams(dimension_semantics=sem, vmem_limit_bytes=vmem_mb << 20)


def _dot(a, b):
    return jnp.dot(a, b, preferred_element_type=F32)


def _dot_nt(a, b):
    return lax.dot_general(a, b, (((1,), (1,)), ((), ())), preferred_element_type=F32)


def _dot_tn(a, b):
    return lax.dot_general(a, b, (((0,), (0,)), ((), ())), preferred_element_type=F32)


def _bf(x):
    return x.astype(BF16)


def _dot3(m01, x):
    x1 = _bf(x)
    r1 = x - x1.astype(F32)
    x2 = _bf(r1)
    x3 = _bf(r1 - x2.astype(F32))
    return _dot(m01, x1) + _dot(m01, x2) + _dot(m01, x3)


def _rms(h):
    rs = lax.rsqrt(jnp.mean(h * h, axis=-1, keepdims=True) + EPS)
    return h * rs, rs


def _rms_bwd(dy, yhat, rs, w):
    dyh = dy * w
    return rs * (dyh - yhat * jnp.mean(dyh * yhat, axis=-1, keepdims=True))


class _Comm(NamedTuple):
    ins: tuple
    outs: tuple
    sems: tuple
    plan: Callable


def _call(body, name, grid, in_specs, out_specs, out_shape, scratch, params, args, comm=None):
    if comm is None:
        outs = pl.pallas_call(body, name=name, grid=grid, in_specs=in_specs, out_specs=out_specs, out_shape=out_shape,
                              scratch_shapes=scratch, compiler_params=params)(*args)
        return outs, None
    n_in, n_out, n_scr = len(in_specs), len(out_specs), len(scratch)
    ci, co = len(comm.ins), len(comm.outs)
    last = grid[0] - 1

    def wrapped(*refs):
        own_in, c_in = refs[:n_in], refs[n_in:n_in + ci]
        refs = refs[n_in + ci:]
        own_out, c_out = refs[:n_out], refs[n_out:n_out + co]
        refs = refs[n_out + co:]
        own_scr, c_sem = refs[:n_scr], refs[n_scr:]
        i = pl.program_id(0)

        @pl.when(i == 0)
        def _():
            for cp in comm.plan(c_in, c_out, c_sem)[0]:
                cp.start()

        body(*own_in, *own_out, *own_scr)

        @pl.when(i == last)
        def _():
            for wait in comm.plan(c_in, c_out, c_sem)[1]:
                wait()

    outs = pl.pallas_call(
        wrapped, name=name, grid=grid, in_specs=list(in_specs) + [ANY_SPEC] * ci, out_specs=list(out_specs) + [ANY_SPEC] * co,
        out_shape=list(out_shape) + list(comm.outs), scratch_shapes=list(scratch) + list(comm.sems), compiler_params=params,
    )(*args, *comm.ins)
    return outs[:n_out], outs[n_out:]


def _run_comm(comm, name):
    ci, co = len(comm.ins), len(comm.outs)

    def body(*refs):
        starts, waits = comm.plan(refs[:ci], refs[ci:ci + co], refs[ci + co:])
        for cp in starts:
            cp.start()
        for wait in waits:
            wait()

    return pl.pallas_call(body, name=name, in_specs=[ANY_SPEC] * ci, out_specs=[ANY_SPEC] * co, out_shape=list(comm.outs),
                          scratch_shapes=list(comm.sems))(*comm.ins)


def _proj_fwd(x, metapad, wm, wt, tabs, comm=None):
    t = x.shape[0]
    nblk = t // TM

    def body(x_ref, mp_ref, wm_ref, w_ref, tab_ref, proj_ref):
        i = pl.program_id(0)
        h = jnp.where(i == nblk, mp_ref[...], x_ref[...])
        u, _ = _rms(h)
        ub = _bf(u * wm_ref[...])
        proj_ref[:, 0:C_SQ] = _dot_nt(ub, w_ref[0:R_LR, :])
        att = _dot_nt(ub, w_ref[R_LR + 16:DIN, :])
        tab = tab_ref[...]
        proj_ref[:, C_SQ:C_SK] = _rope(att[:, 0:512], tab, 1.0) * 0.125
        proj_ref[:, C_SK:C_SV] = _rope(att[:, 512:640], tab, 1.0)
        proj_ref[:, C_SV:C_LR] = att[:, 640:768]
        proj_ref[:, C_LR:DINP] = jnp.zeros((TM, DINP - C_LR), F32)
        proj_ref[:, C_LR:C_LR + 16] = _dot_nt(ub, w_ref[R_LR:R_LR + 16, :])

    (proj,), got = _call(
        body, "proj_fwd", (nblk + 1,),
        [pl.BlockSpec((TM, D), lambda i: (jnp.minimum(i, nblk - 1), 0)), VMEM_SPEC, VMEM_SPEC, VMEM_SPEC,
         pl.BlockSpec((TM, 128), lambda i: (i, 0))],
        [pl.BlockSpec((TM, DINP), lambda i: (i, 0))], [jax.ShapeDtypeStruct((t + TM, DINP), F32)],
        [], _cp(40), (x, metapad, wm, wt, tabs), comm)
    return proj, got


def _chunk_masks():
    r = lax.broadcasted_iota(jnp.int32, (TM, TM), 0)
    c = lax.broadcasted_iota(jnp.int32, (TM, TM), 1)
    same = (r // CH) == (c // CH)
    lower = _bf(jnp.where(same & (c <= r), 1.0, 0.0))
    upper = _bf(jnp.where(same & (c >= r), 1.0, 0.0))
    return lower, upper


def _gla_gate(lr, wgu, bg, valid, lower):
    z = _dot(_bf(lr), wgu) + bg
    g = (jnp.minimum(z, 0.0) - jnp.log(1.0 + jnp.exp(-jnp.abs(z)))) * (1.0 / 16.0)
    g = jnp.where(valid, g, 0.0)
    return z, _dot3(lower, g)


def _gla_decays(q, k, b):
    nc = TM // CH
    b3 = b.reshape(nc, CH, 256)
    blast = b3[:, CH - 1:CH, :]
    eb = jnp.exp(b)
    enb = jnp.exp(-b)
    ebl = jnp.exp(blast - b3).reshape(TM, 256)
    return eb, enb, ebl, jnp.exp(blast)


def _tri(lower_incl):
    r = lax.broadcasted_iota(jnp.int32, (CH, CH), 0)
    c = lax.broadcasted_iota(jnp.int32, (CH, CH), 1)
    return ((c <= r) if lower_incl else (c >= r))[None]


def _gla_fwd(proj, wgu, bg, gnw, t, comm=None):
    nblk = t // TM
    nt = nblk + 1
    nc = TM // CH

    def blk(i):
        return (i + nblk) % nt

    def body(q_ref, k_ref, v_ref, r_ref, lr_ref, wgu_ref, bg_ref, gnw_ref, o_ref, oraw_ref, sst_ref, st_scr):
        i = pl.program_id(0)

        @pl.when(i == 0)
        def _():
            st_scr[...] = jnp.zeros_like(st_scr)

        rows = blk(i) * TM + lax.broadcasted_iota(jnp.int32, (TM, 1), 0)
        lower, _ = _chunk_masks()
        _, b = _gla_gate(lr_ref[...], wgu_ref[...], bg_ref[...], rows < t + NM, lower)
        q = q_ref[...]
        k = k_ref[...]
        eb, enb, ebl, eblast = _gla_decays(q, k, b)
        qt = q * 0.125 * eb
        kt = k * enb
        kh = k * ebl
        tril = _tri(True)
        outs = []
        for h in range(4):
            hs = slice(h * CH, (h + 1) * CH)
            qh = _bf(qt[:, hs]).reshape(nc, CH, CH)
            kth = _bf(kt[:, hs]).reshape(nc, CH, CH)
            khh = _bf(kh[:, hs]).reshape(nc, CH, CH)
            vh = _bf(v_ref[:, h * 128:(h + 1) * 128]).reshape(nc, CH, 128)
            a = jnp.einsum('cid,cjd->cij', qh, kth, preferred_element_type=F32)
            a = jnp.where(tril, a, 0.0)
            o = jnp.einsum('cij,cjv->civ', _bf(a), vh, preferred_element_type=F32)
            kv = jnp.einsum('cjv,cjd->cvd', vh, khh, preferred_element_type=F32)
            st = st_scr[h]
            o_inter = []
            for c in range(nc):
                sst_ref[c, h] = st
                o_inter.append(_dot_nt(qh[c], _bf(st)))
                st = st * eblast[c, :, hs] + kv[c]
            st_scr[h] = st
            outs.append((o + jnp.stack(o_inter)).reshape(TM, 128))
        oraw = jnp.concatenate(outs, axis=1)
        oraw_ref[...] = oraw
        gn = gnw_ref[...]
        res = []
        for h in range(4):
            on, _ = _rms(oraw[:, h * 128:(h + 1) * 128])
            r = r_ref[:, h * 128:(h + 1) * 128]
            res.append(on * gn * (r * jax.nn.sigmoid(r)))
        o_ref[...] = _bf(jnp.concatenate(res, axis=1))

    def spec(w, cb):
        return pl.BlockSpec((TM, w), lambda i: (blk(i), cb))

    return _call(
        body, "gla_fwd", (nt,),
        [spec(256, 0), spec(256, 1), spec(512, 1), spec(512, 2), spec(128, C_LR // 128), VMEM_SPEC, VMEM_SPEC, VMEM_SPEC],
        [spec(512, 0), spec(512, 0), pl.BlockSpec((nc, 4, 128, CH), lambda i: (blk(i), 0, 0, 0))],
        [jax.ShapeDtypeStruct((t + TM, 512), BF16), jax.ShapeDtypeStruct((t + TM, 512), F32),
         jax.ShapeDtypeStruct((nt * nc, 4, 128, CH), F32)],
        [pltpu.VMEM((4, 128, CH), F32)], _cp(40), (proj, proj, proj, proj, proj, wgu, bg, gnw), comm)


def _gla_bwd(proj, oraw, sst, do, wgu, bg, gnw, t, comm=None):
    nblk = t // TM
    nt = nblk + 1
    nc = TM // CH

    def blk(i):
        return (2 * nblk - i) % nt

    def body(q_ref, k_ref, v_ref, r_ref, lr_ref, oraw_ref, sst_ref, do_ref, wgu_ref, bg_ref, gnw_ref,
             dgla_ref, dlr_ref, dwgu_ref, dbg_ref, dgnw_ref, dst_scr):
        i = pl.program_id(0)

        @pl.when(i == 0)
        def _():
            dst_scr[...] = jnp.zeros_like(dst_scr)
            dwgu_ref[...] = jnp.zeros_like(dwgu_ref)
            dbg_ref[...] = jnp.zeros_like(dbg_ref)
            dgnw_ref[...] = jnp.zeros_like(dgnw_ref)

        rows = blk(i) * TM + lax.broadcasted_iota(jnp.int32, (TM, 1), 0)
        valid = rows < t + NM
        lower, upper = _chunk_masks()
        lr = lr_ref[...]
        z, b = _gla_gate(lr, wgu_ref[...], bg_ref[...], valid, lower)
        q = q_ref[...]
        k = k_ref[...]
        eb, enb, ebl, eblast = _gla_decays(q, k, b)
        qt = q * 0.125 * eb
        kt = k * enb
        kh = k * ebl
        gn = gnw_ref[...]
        tril = _tri(True)
        triu = _tri(False)
        dq_l, dk_l, dv_l, dr_l, db_l, ex_l = [], [], [], [], [], []
        dgn = jnp.zeros((1, 128), F32)
        for h in range(4):
            hs = slice(h * CH, (h + 1) * CH)
            vs = slice(h * 128, (h + 1) * 128)
            on, rs = _rms(oraw_ref[:, vs])
            r = r_ref[:, vs]
            sig = jax.nn.sigmoid(r)
            sil = r * sig
            dy = do_ref[:, vs]
            dr_l.append(dy * on * gn * (sig * (1.0 + r * (1.0 - sig))))
            dgn = dgn + jnp.sum(dy * sil * on, axis=0, keepdims=True)
            doraw = _rms_bwd(dy * sil, on, rs, gn)
            qtf = qt[:, hs].reshape(nc, CH, CH)
            ktf = kt[:, hs].reshape(nc, CH, CH)
            khf = kh[:, hs].reshape(nc, CH, CH)
            qh, kth, khh = _bf(qtf), _bf(ktf), _bf(khf)
            vh = _bf(v_ref[:, vs]).reshape(nc, CH, 128)
            doh = _bf(doraw).reshape(nc, CH, 128)
            at = jnp.where(triu, jnp.einsum('cjd,cid->cji', kth, qh, preferred_element_type=F32), 0.0)
            da = jnp.where(tril, jnp.einsum('civ,cjv->cij', doh, vh, preferred_element_type=F32), 0.0)
            dat = jnp.where(triu, jnp.einsum('cjv,civ->cji', vh, doh, preferred_element_type=F32), 0.0)
            dv = jnp.einsum('cji,civ->cjv', _bf(at), doh, preferred_element_type=F32)
            dqt = jnp.einsum('cij,cjd->cid', _bf(da), kth, preferred_element_type=F32)
            dkt = jnp.einsum('cji,cid->cjd', _bf(dat), qh, preferred_element_type=F32)
            gq = jnp.einsum('civ,cid->cvd', doh, qh, preferred_element_type=F32)
            dst = dst_scr[h]
            dsend = [None] * nc
            for c in reversed(range(nc)):
                dsend[c] = dst
                dst = dst * eblast[c, :, hs] + gq[c]
            dst_scr[h] = dst
            dse = jnp.stack(dsend)
            dseb = _bf(dse)
            stf = sst_ref[:, h]
            dqt = dqt + jnp.einsum('civ,cvd->cid', doh, _bf(stf), preferred_element_type=F32)
            dv = dv + jnp.einsum('cjd,cvd->cjv', khh, dseb, preferred_element_type=F32)
            dkh = jnp.einsum('cjv,cvd->cjd', vh, dseb, preferred_element_type=F32)
            extra = (jnp.sum(dkh * khf, axis=1, keepdims=True)
                     + eblast[:, :, hs] * jnp.sum(dse * stf, axis=1, keepdims=True))
            db_l.append((dqt * qtf - dkt * ktf - dkh * khf).reshape(TM, CH))
            ex_l.append(jnp.broadcast_to(extra, (nc, CH, CH)).reshape(TM, CH))
            dq_l.append((dqt.reshape(TM, CH)) * eb[:, hs] * 0.125)
            dk_l.append(dkt.reshape(TM, CH) * enb[:, hs] + dkh.reshape(TM, CH) * ebl[:, hs])
            dv_l.append(dv.reshape(TM, 128))
        dgnw_ref[...] += dgn
        db = jnp.concatenate(db_l, axis=1)
        dg = _dot3(upper, db) + jnp.concatenate(ex_l, axis=1)
        dz = jnp.where(valid, dg * (1.0 / 16.0) / (1.0 + jnp.exp(z)), 0.0)
        dzb = _bf(dz)
        dlr_ref[...] = _bf(_dot_nt(dzb, wgu_ref[...]))
        dwgu_ref[...] += _dot_tn(_bf(lr), dzb)
        dbg_ref[...] += jnp.sum(dz, axis=0, keepdims=True)
        dgla_ref[...] = _bf(jnp.concatenate(dq_l + dk_l + dv_l + dr_l, axis=1))

    def spec(w, cb):
        return pl.BlockSpec((TM, w), lambda i: (blk(i), cb))

    def acc(shape):
        return pl.BlockSpec(shape, lambda i: (0, 0))

    return _call(
        body, "gla_bwd", (nt,),
        [spec(256, 0), spec(256, 1), spec(512, 1), spec(512, 2), spec(128, C_LR // 128), spec(512, 0),
         pl.BlockSpec((nc, 4, 128, CH), lambda i: (blk(i), 0, 0, 0)), spec(512, 0), VMEM_SPEC, VMEM_SPEC, VMEM_SPEC],
        [spec(1536, 0), spec(128, 0), acc((128, 256)), acc((1, 256)), acc((1, 128))],
        [jax.ShapeDtypeStruct((t + TM, 1536), BF16), jax.ShapeDtypeStruct((t + TM, 128), BF16),
         jax.ShapeDtypeStruct((128, 256), F32), jax.ShapeDtypeStruct((1, 256), F32), jax.ShapeDtypeStruct((1, 128), F32)],
        [pltpu.VMEM((4, 128, CH), F32)], _cp(48), (proj, proj, proj, proj, proj, oraw, sst, do, wgu, bg, gnw), comm)


def _rope_tables(t):
    r = t + TM
    row = np.arange(r)
    pos = np.where(row < t, row + NM, np.where(row < t + NM, row - t, 0)).astype(np.float32)
    inv_freq = (1.0 / (np.float32(ROPE_THETA) ** (np.arange(0, 16, 2, dtype=np.float32) / np.float32(16)))).astype(np.float32)
    ang = (pos[:, None] * inv_freq[None, :]).astype(np.float32)
    cos, sin = np.cos(ang).astype(np.float32), np.sin(ang).astype(np.float32)
    one, zero = np.ones((r, 48), np.float32), np.zeros((r, 48), np.float32)
    return jnp.asarray(np.concatenate([cos, cos, one, -sin, sin, zero], axis=1))


def _rope(x, tab, sign):
    w = x.shape[1]
    rep = w // 64
    c = jnp.concatenate([tab[:, 0:64]] * rep, axis=1)
    s = jnp.concatenate([tab[:, 64:128]] * rep, axis=1)
    lane = lax.rem(lax.broadcasted_iota(jnp.int32, x.shape, 1), 64)
    partner = jnp.where(lane < 8, pltpu.roll(x, w - 8, 1), jnp.where(lane < 16, pltpu.roll(x, 8, 1), 0.0))
    return x * c + sign * (partner * s)


def _stack(x):
    return jnp.concatenate([x[:, g * 64:(g + 1) * 64] for g in range(4)], axis=0)


def _unstack(x):
    return jnp.concatenate([x[g * SB:(g + 1) * SB] for g in range(4)], axis=1)


def _swa_masks(b, nsb):
    r = lax.rem(lax.broadcasted_iota(jnp.int32, (4 * SB, SB), 0), SB)
    c = lax.broadcasted_iota(jnp.int32, (4 * SB, SB), 1)
    real = b < nsb
    return c <= r, (c > r) & (b > 0) & real, (c < NM) & real


def _swa_specs(nsb):
    def rows(h, w, cb, f):
        return pl.BlockSpec((h, w), lambda i: (f(i), cb))
    pair = lambda i: i
    prev = lambda i: jnp.maximum(2 * i - 1, 0)
    meta = lambda i: nsb
    return rows, pair, prev, meta


def _swa_scores(b, nsb, sink_ref, q, kc, kp, km):
    mc, mp, mm = _swa_masks(b, nsb)
    per_kv = []
    for kv in range(2):
        ks = slice(kv * 64, (kv + 1) * 64)
        qg = _bf(_stack(q[:, kv * 256:(kv + 1) * 256]))
        kcb, kpb, kmb = _bf(kc[:, ks]), _bf(kp[:, ks]), _bf(km[:, ks])
        s_c = jnp.where(mc, _dot_nt(qg, kcb), NEG)
        s_p = jnp.where(mp, _dot_nt(qg, kpb), NEG)
        s_m = jnp.where(mm, _dot_nt(qg, kmb), NEG)
        sink = jnp.concatenate([jnp.full((SB, 1), sink_ref[0, kv * 4 + g], F32) for g in range(4)], axis=0)
        per_kv.append((qg, kcb, kpb, kmb, s_c, s_p, s_m, sink))
    return per_kv


def _swa_fwd(proj, sinks, t, comm=None):
    nsb = t // SB
    r_tot = t + TM
    rows, pair, prev, meta = _swa_specs(nsb)

    def body(sink_ref, q_ref, kc_ref, kp_ref, km_ref, vc_ref, vp_ref, vm_ref, o_ref, lse_ref):
        i = pl.program_id(0)
        km, vm = km_ref[...], vm_ref[...]
        for j in range(2):
            b = 2 * i + j
            rs = slice(j * SB, (j + 1) * SB)
            kp = kp_ref[...] if j == 0 else kc_ref[0:SB, :]
            vp = vp_ref[...] if j == 0 else vc_ref[0:SB, :]
            vc = vc_ref[rs, :]
            o_l, lse_l = [], []
            for kv, (qg, kcb, kpb, kmb, s_c, s_p, s_m, sink) in enumerate(
                    _swa_scores(b, nsb, sink_ref, q_ref[rs, :], kc_ref[rs, :], kp, km)):
                ks = slice(kv * 64, (kv + 1) * 64)
                m = jnp.maximum(jnp.max(jnp.maximum(jnp.maximum(s_c, s_p), s_m), -1, keepdims=True), sink)
                p_c, p_p, p_m = jnp.exp(s_c - m), jnp.exp(s_p - m), jnp.exp(s_m - m)
                l = jnp.sum(p_c + p_p + p_m, -1, keepdims=True) + jnp.exp(sink - m)
                o = _dot(_bf(p_c), _bf(vc[:, ks])) + _dot(_bf(p_p), _bf(vp[:, ks])) + _dot(_bf(p_m), _bf(vm[:, ks]))
                o_l.append(_unstack(o * (1.0 / l)))
                lse_l.append(_unstack(m + jnp.log(l)))
            valid = b * SB + lax.broadcasted_iota(jnp.int32, (SB, 1), 0) < t + NM
            o_ref[rs, :] = _bf(jnp.where(valid, jnp.concatenate(o_l, axis=1), 0.0))
            lse_ref[rs, :] = jnp.concatenate(lse_l, axis=1)

    ck, cv = C_SK // 128, C_SV // 128
    return _call(
        body, "swa_fwd", (r_tot // QB,),
        [SMEM_SPEC, rows(QB, 512, C_SQ // 512, pair),
         rows(QB, 128, ck, pair), rows(SB, 128, ck, prev), rows(SB, 128, ck, meta),
         rows(QB, 128, cv, pair), rows(SB, 128, cv, prev), rows(SB, 128, cv, meta)],
        [rows(QB, 512, 0, pair), rows(QB, 8, 0, pair)],
        [jax.ShapeDtypeStruct((r_tot, 512), BF16), jax.ShapeDtypeStruct((r_tot, 8), F32)],
        [], _cp(32), (sinks, proj, proj, proj, proj, proj, proj, proj), comm)


def _swa_bwd(proj, sinks, lse, do, t, comm=None):
    nsb = t // SB
    r_tot = t + TM
    rows, pair, prev, meta = _swa_specs(nsb)

    def body(sink_ref, q_ref, kc_ref, kp_ref, km_ref, vc_ref, vp_ref, vm_ref, lse_ref, do_ref,
             dq_ref, dk_ref, dv_ref, dsink_ref):
        i = pl.program_id(0)

        @pl.when(i == 0)
        def _():
            dk_ref[...] = jnp.zeros_like(dk_ref)
            dv_ref[...] = jnp.zeros_like(dv_ref)
            dsink_ref[...] = jnp.zeros_like(dsink_ref)

        km, vm = km_ref[...], vm_ref[...]
        dsink = jnp.zeros((1, 8), F32)
        for j in range(2):
            b = 2 * i + j
            rs = slice(j * SB, (j + 1) * SB)
            kp = kp_ref[...] if j == 0 else kc_ref[0:SB, :]
            vp = vp_ref[...] if j == 0 else vc_ref[0:SB, :]
            vc = vc_ref[rs, :]
            lse_all = lse_ref[rs, :]
            dq_l, dkc_l, dkp_l, dkm_l, dvc_l, dvp_l, dvm_l, ds_l = [], [], [], [], [], [], [], []
            for kv, (qg, kcb, kpb, kmb, s_c, s_p, s_m, sink) in enumerate(
                    _swa_scores(b, nsb, sink_ref, q_ref[rs, :], kc_ref[rs, :], kp, km)):
                ks = slice(kv * 64, (kv + 1) * 64)
                lse_g = _stack_cols(lse_all[:, kv * 4:(kv + 1) * 4])
                dog = _bf(_stack(do_ref[rs, kv * 256:(kv + 1) * 256]))
                vcb, vpb, vmb = _bf(vc[:, ks]), _bf(vp[:, ks]), _bf(vm[:, ks])
                p_c, p_p, p_m = jnp.exp(s_c - lse_g), jnp.exp(s_p - lse_g), jnp.exp(s_m - lse_g)
                dp_c, dp_p, dp_m = _dot_nt(dog, vcb), _dot_nt(dog, vpb), _dot_nt(dog, vmb)
                delta = jnp.sum(p_c * dp_c + p_p * dp_p + p_m * dp_m, -1, keepdims=True)
                ds_c, ds_p, ds_m = _bf(p_c * (dp_c - delta)), _bf(p_p * (dp_p - delta)), _bf(p_m * (dp_m - delta))
                dq_l.append(_unstack(_dot(ds_c, kcb) + _dot(ds_p, kpb) + _dot(ds_m, kmb)))
                dkc_l.append(_dot_tn(ds_c, qg))
                dkp_l.append(_dot_tn(ds_p, qg))
                dkm_l.append(_dot_tn(ds_m, qg))
                dvc_l.append(_dot_tn(_bf(p_c), dog))
                dvp_l.append(_dot_tn(_bf(p_p), dog))
                dvm_l.append(_dot_tn(_bf(p_m), dog))
                ds_l.append(_unstack(-jnp.exp(sink - lse_g) * delta))
            dq_ref[rs, :] = jnp.concatenate(dq_l, axis=1)
            c0 = pl.multiple_of(b * SB, SB)
            p0 = pl.multiple_of(jnp.maximum(b - 1, 0) * SB, SB)
            dk_ref[pl.ds(c0, SB), :] += jnp.concatenate(dkc_l, axis=1)
            dk_ref[pl.ds(p0, SB), :] += jnp.concatenate(dkp_l, axis=1)
            dk_ref[pl.ds(t, SB), :] += jnp.concatenate(dkm_l, axis=1)
            dv_ref[pl.ds(c0, SB), :] += jnp.concatenate(dvc_l, axis=1)
            dv_ref[pl.ds(p0, SB), :] += jnp.concatenate(dvp_l, axis=1)
            dv_ref[pl.ds(t, SB), :] += jnp.concatenate(dvm_l, axis=1)
            dsink = dsink + jnp.sum(jnp.concatenate(ds_l, axis=1), axis=0, keepdims=True)
        dsink_ref[...] += dsink

    ck, cv = C_SK // 128, C_SV // 128
    whole = lambda w: pl.BlockSpec((r_tot, w), lambda i: (0, 0))
    return _call(
        body, "swa_bwd", (r_tot // QB,),
        [SMEM_SPEC, rows(QB, 512, C_SQ // 512, pair),
         rows(QB, 128, ck, pair), rows(SB, 128, ck, prev), rows(SB, 128, ck, meta),
         rows(QB, 128, cv, pair), rows(SB, 128, cv, prev), rows(SB, 128, cv, meta),
         rows(QB, 8, 0, pair), rows(QB, 512, 1, pair)],
        [rows(QB, 512, 0, pair), whole(128), whole(128), pl.BlockSpec((1, 8), lambda i: (0, 0))],
        [jax.ShapeDtypeStruct((r_tot, 512), F32), jax.ShapeDtypeStruct((r_tot, 128), F32),
         jax.ShapeDtypeStruct((r_tot, 128), F32), jax.ShapeDtypeStruct((1, 8), F32)],
        [], _cp(48), (sinks, proj, proj, proj, proj, proj, proj, proj, lse, do), comm)


def _stack_cols(x):
    return jnp.concatenate([x[:, g:g + 1] for g in range(4)], axis=0)


HK = D // 2


def _mlp_fwd(x, metapad, tgt, ogla, oswa, wo, wff, w1, w2, wfin):
    t = x.shape[0]
    nblk = t // TM

    def body(x_ref, mp_ref, tgt_ref, og_ref, os_ref, wo_ref, wff_ref, w1a_ref, w1b_ref, w2a_ref, w2b_ref, wfin_ref,
             h1_ref, f_ref, a_ref, dh2_ref, loss_ref, gfin_ref):
        i = pl.program_id(0)

        @pl.when(i == 0)
        def _():
            loss_ref[...] = jnp.zeros_like(loss_ref)
            gfin_ref[...] = jnp.zeros_like(gfin_ref)

        h0 = jnp.where(i == nblk, mp_ref[...], x_ref[...])
        h1 = h0 + _dot(og_ref[...], wo_ref[0:512, :]) + _dot(os_ref[...], wo_ref[512:1024, :])
        h1_ref[...] = h1
        fh, _ = _rms(h1)
        f = _bf(fh * wff_ref[...])
        f_ref[...] = f
        acc = jnp.zeros((TM, D), F32)
        for n in range(4):
            a = _dot(f[:, 0:HK], w1a_ref[n]) + _dot(f[:, HK:D], w1b_ref[n])
            a_ref[:, n * D:(n + 1) * D] = _bf(a)
            zr = jnp.maximum(a, 0.0)
            z = _bf(zr * zr)
            acc = acc + _dot(z[:, 0:HK], w2a_ref[n]) + _dot(z[:, HK:D], w2b_ref[n])
        h2 = h1 + acc
        yh, rs2 = _rms(h2)
        wf = wfin_ref[...]
        real = i < nblk
        e = jnp.where(real, yh * wf - tgt_ref[...], 0.0)
        loss_ref[...] += jnp.sum(jnp.sum(e * e, axis=0, keepdims=True), axis=1, keepdims=True) * (0.5 / D)
        dy = e * (1.0 / D)
        gfin_ref[...] += jnp.sum(dy * yh, axis=0, keepdims=True)
        dh2_ref[...] = _rms_bwd(dy, yh, rs2, wf)

    xs = pl.BlockSpec((TM, D), lambda i: (jnp.minimum(i, nblk - 1), 0))
    rs = lambda w: pl.BlockSpec((TM, w), lambda i: (i, 0))
    r_tot = t + TM
    return pl.pallas_call(
        body, name="mlp_fwd", grid=(nblk + 1,),
        in_specs=[xs, VMEM_SPEC, xs, rs(512), rs(512)] + [VMEM_SPEC] * 7,
        out_specs=[rs(D), rs(D), rs(DFF), rs(D), pl.BlockSpec((1, 1), lambda i: (0, 0)), pl.BlockSpec((1, D), lambda i: (0, 0))],
        out_shape=[jax.ShapeDtypeStruct((r_tot, D), F32), jax.ShapeDtypeStruct((r_tot, D), BF16),
                   jax.ShapeDtypeStruct((r_tot, DFF), BF16), jax.ShapeDtypeStruct((r_tot, D), F32),
                   jax.ShapeDtypeStruct((1, 1), F32), jax.ShapeDtypeStruct((1, D), F32)],
        compiler_params=_cp(56),
    )(x, metapad, tgt, ogla, oswa, wo, wff, *w1, *w2, wfin)


def _mlp_bwd(h1, a, dh2, ogla, oswa, wo, wff, w1, w2):
    r_tot = h1.shape[0]
    nt = r_tot // TM

    def body(h1_ref, a_ref, dh2_ref, og_ref, os_ref, wo_ref, wff_ref, w1a_ref, w1b_ref, w2a_ref, w2b_ref,
             da_ref, dh2b_ref, dh1_ref, do_ref, dwo_ref, gff_ref):
        i = pl.program_id(0)

        @pl.when(i == 0)
        def _():
            dwo_ref[...] = jnp.zeros_like(dwo_ref)
            gff_ref[...] = jnp.zeros_like(gff_ref)

        dh2 = dh2_ref[...]
        dh2b = _bf(dh2)
        dh2b_ref[...] = dh2b
        dfa = jnp.zeros((TM, HK), F32)
        dfb = jnp.zeros((TM, HK), F32)
        for n in range(4):
            dz = jnp.concatenate([_dot_nt(dh2b, w2a_ref[n]), _dot_nt(dh2b, w2b_ref[n])], axis=1)
            da = _bf(dz * (2.0 * jnp.maximum(a_ref[:, n * D:(n + 1) * D].astype(F32), 0.0)))
            da_ref[:, n * D:(n + 1) * D] = da
            dfa = dfa + _dot_nt(da, w1a_ref[n])
            dfb = dfb + _dot_nt(da, w1b_ref[n])
        df = jnp.concatenate([dfa, dfb], axis=1)
        fh, rs1 = _rms(h1_ref[...])
        gff_ref[...] += jnp.sum(df * fh, axis=0, keepdims=True)
        dh1 = dh2 + _rms_bwd(df, fh, rs1, wff_ref[...])
        dh1_ref[...] = dh1
        dh1b = _bf(dh1)
        do_ref[...] = _dot_nt(dh1b, wo_ref[...])
        dwo_ref[0:512, :] += _dot_tn(og_ref[...], dh1b)
        dwo_ref[512:1024, :] += _dot_tn(os_ref[...], dh1b)

    rs = lambda w: pl.BlockSpec((TM, w), lambda i: (i, 0))
    return pl.pallas_call(
        body, name="mlp_bwd", grid=(nt,),
        in_specs=[rs(D), rs(DFF), rs(D), rs(512), rs(512)] + [VMEM_SPEC] * 6,
        out_specs=[rs(DFF), rs(D), rs(D), rs(D), pl.BlockSpec((D, D), lambda i: (0, 0)),
                   pl.BlockSpec((1, D), lambda i: (0, 0))],
        out_shape=[jax.ShapeDtypeStruct((r_tot, DFF), BF16), jax.ShapeDtypeStruct((r_tot, D), BF16),
                   jax.ShapeDtypeStruct((r_tot, D), F32), jax.ShapeDtypeStruct((r_tot, D), F32),
                   jax.ShapeDtypeStruct((D, D), F32), jax.ShapeDtypeStruct((1, D), F32)],
        compiler_params=_cp(56),
    )(h1, a, dh2, ogla, oswa, wo, wff, *w1, *w2)


def _ffn_wgrad(f, a, da, dh2b):
    r_tot = f.shape[0]
    kt = 768 if r_tot % 768 == 0 else TM
    nk = r_tot // kt

    def body(f_ref, a_ref, da_ref, dh2_ref, dw1_ref, dw2_ref, acc1, acc2):
        k = pl.program_id(1)

        @pl.when(k == 0)
        def _():
            acc1[...] = jnp.zeros_like(acc1)
            acc2[...] = jnp.zeros_like(acc2)

        zr = jnp.maximum(a_ref[...], 0.0)
        acc1[...] += _dot_tn(f_ref[...], da_ref[...])
        acc2[...] += _dot_tn(zr * zr, dh2_ref[...])

        @pl.when(k == nk - 1)
        def _():
            for hh in range(2):
                dw1_ref[hh, 0] = acc1[hh * 512:(hh + 1) * 512, :]
                dw2_ref[hh, 0] = acc2[hh * 512:(hh + 1) * 512, :]

    full = pl.BlockSpec((kt, D), lambda n, k: (k, 0))
    col = pl.BlockSpec((kt, D), lambda n, k: (k, n))
    out = pl.BlockSpec((2, 1, 512, D), lambda n, k: (0, n, 0, 0))
    return pl.pallas_call(
        body, name="ffn_wgrad", grid=(4, nk),
        in_specs=[full, col, col, full], out_specs=[out, out],
        out_shape=[jax.ShapeDtypeStruct((2, 4, 512, D), F32)] * 2,
        scratch_shapes=[pltpu.VMEM((D, D), F32), pltpu.VMEM((D, D), F32)],
        compiler_params=_cp(48, ("arbitrary", "arbitrary")),
    )(f, a, da, dh2b)


def _proj_bwd(x, metapad, wm, wt, tabs, dgla, dswa_q, dsk, dsv, dlr, dh1, comm=None):
    t = x.shape[0]
    nblk = t // TM

    def body(x_ref, mp_ref, wm_ref, w_ref, tab_ref, dg_ref, dq_ref, dk_ref, dv_ref, dlr_ref, dh1_ref,
             gx_ref, gmeta_ref, dw_ref, gmix_ref):
        i = pl.program_id(0)

        @pl.when(i == 0)
        def _():
            dw_ref[...] = jnp.zeros_like(dw_ref)
            gmix_ref[...] = jnp.zeros_like(gmix_ref)

        h = jnp.where(i == nblk, mp_ref[...], x_ref[...])
        uh, rs = _rms(h)
        wm_v = wm_ref[...]
        u = _bf(uh * wm_v)
        tab = tab_ref[...]
        dq = _bf(_rope(dq_ref[...] * 0.125, tab, -1.0))
        dk = _bf(_rope(dk_ref[...], tab, -1.0))
        parts = ((dg_ref[...], 0, R_LR), (dlr_ref[:, 0:16], R_LR, 16), (dq, R_LR + 16, 512),
                 (dk, R_LR + 528, 128), (_bf(dv_ref[...]), R_LR + 656, 128))
        du = jnp.zeros((TM, D), F32)
        for val, r0, w in parts:
            du = du + _dot(val, w_ref[r0:r0 + w, :])
            dw_ref[r0:r0 + w, :] += _dot_tn(val, u)
        gmix_ref[...] += jnp.sum(du * uh, axis=0, keepdims=True)
        dh0 = dh1_ref[...] + _rms_bwd(du, uh, rs, wm_v)

        @pl.when(i < nblk)
        def _():
            gx_ref[...] = dh0

        @pl.when(i == nblk)
        def _():
            gmeta_ref[...] = dh0[:NM]

    xs = pl.BlockSpec((TM, D), lambda i: (jnp.minimum(i, nblk - 1), 0))
    rs_ = lambda w: pl.BlockSpec((TM, w), lambda i: (i, 0))
    return _call(
        body, "proj_bwd", (nblk + 1,),
        [xs, VMEM_SPEC, VMEM_SPEC, VMEM_SPEC, rs_(128), rs_(1536), rs_(512), rs_(128), rs_(128), rs_(128), rs_(D)],
        [xs, pl.BlockSpec((NM, D), lambda i: (0, 0)), pl.BlockSpec((DIN, D), lambda i: (0, 0)),
         pl.BlockSpec((1, D), lambda i: (0, 0))],
        [jax.ShapeDtypeStruct((t, D), F32), jax.ShapeDtypeStruct((NM, D), F32),
         jax.ShapeDtypeStruct((DIN, D), F32), jax.ShapeDtypeStruct((1, D), F32)],
        [], _cp(56), (x, metapad, wm, wt, tabs, dgla, dswa_q, dsk, dsv, dlr, dh1), comm)


def _place():
    return lax.axis_index("x"), lax.axis_index("y"), lax.axis_index("c")


def _other_chips(x, y):
    return [(1 - x, y), (x, 1 - y), (1 - x, 1 - y)]


def _dma_sems(*counts):
    return tuple(pltpu.SemaphoreType.DMA((k,)) for k in counts)


def _gather_shards(shards):
    n = len(shards)

    def plan(ins, outs, sems):
        send, recv, loc = sems
        x, y, c = _place()
        chips = _other_chips(x, y)

        def remote(a, k, shard_of):
            tx, ty = chips[k]
            sx, sy = shard_of
            return pltpu.make_async_remote_copy(
                src_ref=ins[a], dst_ref=outs[a].at[2 * sx + sy], send_sem=send.at[3 * a + k], recv_sem=recv.at[3 * a + k],
                device_id=(tx, ty, c), device_id_type=MESH)

        local = [pltpu.make_async_copy(ins[a], outs[a].at[2 * x + y], loc.at[a]) for a in range(n)]
        sends = [remote(a, k, (x, y)) for a in range(n) for k in range(3)]
        waits = ([lambda a=a, k=k: remote(a, k, chips[k]).wait_recv() for a in range(n) for k in range(3)]
                 + [cp.wait_send for cp in sends] + [cp.wait for cp in local])
        return local + sends, waits

    return _Comm(tuple(shards), tuple(jax.ShapeDtypeStruct((4,) + s.shape, s.dtype) for s in shards),
                 _dma_sems(3 * n, 3 * n, n), plan)


def _swap_halves(grads):
    n = len(grads)

    def plan(ins, outs, sems):
        send, recv = sems
        x, y, c = _place()
        cps = [pltpu.make_async_remote_copy(
            src_ref=ins[a].at[1 - c], dst_ref=outs[a], send_sem=send.at[a], recv_sem=recv.at[a],
            device_id=(x, y, 1 - c), device_id_type=MESH) for a in range(n)]
        return cps, [cp.wait for cp in cps]

    return _Comm(tuple(grads), tuple(jax.ShapeDtypeStruct(g.shape[1:], g.dtype) for g in grads), _dma_sems(n, n), plan)


def _scatter_shards(parts):
    n = len(parts)

    def plan(ins, outs, sems):
        send, recv = sems
        x, y, c = _place()
        cps = [pltpu.make_async_remote_copy(
            src_ref=ins[a].at[2 * tx + ty], dst_ref=outs[a].at[k], send_sem=send.at[3 * a + k],
            recv_sem=recv.at[3 * a + k], device_id=(tx, ty, c), device_id_type=MESH)
            for a in range(n) for k, (tx, ty) in enumerate(_other_chips(x, y))]
        return cps, [cp.wait for cp in cps]

    return _Comm(tuple(parts), tuple(jax.ShapeDtypeStruct((3,) + p.shape[1:], p.dtype) for p in parts),
                 _dma_sems(3 * n, 3 * n), plan)


def _join_halves(halves):
    n = len(halves)

    def plan(ins, outs, sems):
        send, recv, loc = sems
        x, y, c = _place()

        def remote(a, half):
            return pltpu.make_async_remote_copy(
                src_ref=ins[a], dst_ref=outs[a].at[half], send_sem=send.at[a], recv_sem=recv.at[a],
                device_id=(x, y, 1 - c), device_id_type=MESH)

        local = [pltpu.make_async_copy(ins[a], outs[a].at[c], loc.at[a]) for a in range(n)]
        sends = [remote(a, c) for a in range(n)]
        waits = ([lambda a=a: remote(a, 1 - c).wait_recv() for a in range(n)] + [cp.wait_send for cp in sends]
                 + [cp.wait for cp in local])
        return local + sends, waits

    return _Comm(tuple(halves), tuple(jax.ShapeDtypeStruct((2,) + h.shape, h.dtype) for h in halves),
                 _dma_sems(n, n, n), plan)


def _reduce_w_in(parts, comm):
    rows, cols = parts.shape[1:]
    ci, co = len(comm.ins), len(comm.outs)

    def body(*refs):
        parts_ref, c_in, out_ref, c_out = refs[0], refs[1:1 + ci], refs[1 + ci], refs[2 + ci:2 + ci + co]
        mine, tosend, rbuf, qbuf, sib, send, recv, loc = refs[2 + ci + co:10 + ci + co]
        c_sem = refs[10 + ci + co:]
        x, y, c = _place()
        starts, waits = comm.plan(c_in, c_out, c_sem)
        for cp in starts:
            cp.start()
        load = pltpu.make_async_copy(parts_ref, mine, loc.at[0])
        load.start()
        load.wait()
        cps = []
        for k, (tx, ty) in enumerate(_other_chips(x, y)):
            tosend[k] = _bf(mine[2 * tx + ty])
            cps.append(pltpu.make_async_remote_copy(
                src_ref=tosend.at[k], dst_ref=rbuf.at[k], send_sem=send.at[k], recv_sem=recv.at[k],
                device_id=(tx, ty, c), device_id_type=MESH))
            cps[-1].start()
        for cp in cps:
            cp.wait()
        qbuf[...] = mine[2 * x + y] + rbuf[0].astype(F32) + rbuf[1].astype(F32) + rbuf[2].astype(F32)
        swap = pltpu.make_async_remote_copy(src_ref=qbuf, dst_ref=sib, send_sem=send.at[3], recv_sem=recv.at[3],
                                            device_id=(x, y, 1 - c), device_id_type=MESH)
        swap.start()
        swap.wait()
        out_ref[...] = qbuf[...] + sib[...]
        for wait in waits:
            wait()

    outs = pl.pallas_call(
        body, name="reduce_w_in",
        in_specs=[ANY_SPEC] * (1 + ci), out_specs=[VMEM_SPEC] + [ANY_SPEC] * co,
        out_shape=[jax.ShapeDtypeStruct((rows, cols), F32)] + list(comm.outs),
        scratch_shapes=[pltpu.VMEM((4, rows, cols), F32), pltpu.VMEM((3, rows, cols), BF16), pltpu.VMEM((3, rows, cols), BF16),
                        pltpu.VMEM((rows, cols), F32), pltpu.VMEM((rows, cols), F32), *_dma_sems(4, 4, 1), *comm.sems],
        compiler_params=pltpu.CompilerParams(vmem_limit_bytes=48 << 20),
    )(parts, *comm.ins)
    return outs[0], outs[1:]


def _allreduce_small(pack):
    p = pack.shape[0]

    def body(in_ref, out_ref, buf, send, recv):
        x, y, c = _place()
        me = 4 * x + 2 * y + c
        buf[me] = in_ref[...]
        sends = []
        for k in range(1, 8):
            tx, ty, tc = x ^ (k >> 2), y ^ ((k >> 1) & 1), c ^ (k & 1)
            sends.append(pltpu.make_async_remote_copy(
                src_ref=in_ref, dst_ref=buf.at[me], send_sem=send.at[k - 1], recv_sem=recv.at[k - 1],
                device_id=(tx, ty, tc), device_id_type=MESH))
        for cp in sends:
            cp.start()
        for k in range(1, 8):
            peer = 4 * (x ^ (k >> 2)) + 2 * (y ^ ((k >> 1) & 1)) + (c ^ (k & 1))
            pltpu.make_async_remote_copy(
                src_ref=in_ref, dst_ref=buf.at[peer], send_sem=send.at[k - 1], recv_sem=recv.at[k - 1],
                device_id=(x, y, c), device_id_type=MESH).wait_recv()
        for cp in sends:
            cp.wait_send()
        acc = buf[0]
        for d in range(1, 8):
            acc = acc + buf[d]
        out_ref[...] = acc

    return pl.pallas_call(
        body, name="allreduce_small",
        in_specs=[VMEM_SPEC], out_specs=VMEM_SPEC, out_shape=jax.ShapeDtypeStruct(pack.shape, F32),
        scratch_shapes=[pltpu.VMEM((8, p, D), F32), pltpu.SemaphoreType.DMA((7,)), pltpu.SemaphoreType.DMA((7,))],
    )(pack)


GRID4 = 4


def _sum_parts(sel, firsts, others, name, also_bf16):
    n = len(firsts)
    nk = others[0].shape[0]

    def body(sel_ref, *refs):
        fs, os_, outs = refs[:n], refs[n:2 * n], refs[2 * n:]
        for a in range(n):
            acc = fs[a][0]
            for k in range(nk):
                acc = acc + os_[a][k].astype(F32)
            outs[a][...] = acc
            if also_bf16:
                outs[n + a][...] = _bf(acc)

    def rows(a):
        return firsts[a].shape[1] // GRID4

    in_specs = ([pl.BlockSpec((1, rows(a), firsts[a].shape[2]), lambda i, s: (s[0], i, 0)) for a in range(n)]
                + [pl.BlockSpec((nk, rows(a), firsts[a].shape[2]), lambda i, s: (0, i, 0)) for a in range(n)])
    out_specs = [pl.BlockSpec((rows(a), firsts[a].shape[2]), lambda i, s: (i, 0)) for a in range(n)]
    out_shape = [jax.ShapeDtypeStruct(f.shape[1:], F32) for f in firsts]
    if also_bf16:
        out_specs = out_specs * 2
        out_shape = out_shape + [jax.ShapeDtypeStruct(f.shape[1:], BF16) for f in firsts]
    outs = pl.pallas_call(
        body, name=name,
        grid_spec=pltpu.PrefetchScalarGridSpec(num_scalar_prefetch=1, grid=(GRID4,), in_specs=in_specs, out_specs=out_specs),
        out_shape=out_shape, compiler_params=_cp(48),
    )(sel, *firsts, *others)
    return outs[:n], outs[n:]


def _adamw_math(w, g, m, v):
    m2 = ADAM_B1 * m + (1.0 - ADAM_B1) * g
    v2 = ADAM_B2 * v + (1.0 - ADAM_B2) * (g * g)
    m_hat = m2 / (1.0 - ADAM_B1 ** ADAM_STEP)
    v_hat = v2 / (1.0 - ADAM_B2 ** ADAM_STEP)
    return -ADAM_LR * (m_hat / (jnp.sqrt(v_hat) + ADAM_EPS) + ADAM_WD * w), m2, v2


def _adamw_big(ws, gs, ms, vs):
    n = len(ws)

    def body(*refs):
        for a in range(n):
            d, m2, v2 = _adamw_math(refs[a][...], refs[n + a][...], refs[2 * n + a][...], refs[3 * n + a][...])
            refs[4 * n + a][...] = d
            refs[5 * n + a][...] = m2
            refs[6 * n + a][...] = v2

    specs = [pl.BlockSpec((w.shape[0] // GRID4, w.shape[1]), lambda i: (i, 0)) for w in ws]
    return pl.pallas_call(
        body, name="adamw_big", grid=(GRID4,),
        in_specs=specs * 4, out_specs=specs * 3,
        out_shape=[jax.ShapeDtypeStruct(w.shape, F32) for w in ws] * 3,
        compiler_params=_cp(48),
    )(*ws, *gs, *ms, *vs)


def _adamw_small(ws, gs, ms, vs):
    n = len(ws)

    def body(*refs):
        for a in range(n):
            d, m2, v2 = _adamw_math(refs[a][...], refs[n + a][...], refs[2 * n + a][...], refs[3 * n + a][...])
            refs[4 * n + a][...] = d
            refs[5 * n + a][...] = m2
            refs[6 * n + a][...] = v2

    return pl.pallas_call(
        body, name="adamw_small",
        in_specs=[VMEM_SPEC] * (4 * n), out_specs=[VMEM_SPEC] * (3 * n),
        out_shape=[jax.ShapeDtypeStruct(w.shape, F32) for w in ws] * 3,
        compiler_params=pltpu.CompilerParams(vmem_limit_bytes=40 << 20),
    )(*ws, *gs, *ms, *vs)


def kernel(x, meta_tokens, norm_mix_w, w_in, w_gate_up, b_gate, gla_norm_w, sinks, w_out, norm_ff_w, w_ff1, w_ff2, final_norm_w, loss_target, m_meta_tokens, m_norm_mix_w, m_w_in, m_w_gate_up, m_b_gate, m_gla_norm_w, m_sinks, m_w_out, m_norm_ff_w, m_w_ff1, m_w_ff2, m_final_norm_w, v_meta_tokens, v_norm_mix_w, v_w_in, v_w_gate_up, v_b_gate, v_gla_norm_w, v_sinks, v_w_out, v_norm_ff_w, v_w_ff1, v_w_ff2, v_final_norm_w):
    xi, yi, ci = _place()
    shard = (2 * xi + yi).astype(jnp.int32).reshape(1)
    core = ci.astype(jnp.int32).reshape(1)

    small = jnp.concatenate([meta_tokens, w_gate_up[0], jnp.zeros((NM, 64), F32)], axis=1)
    g_in, g_small = _run_comm(_gather_shards([_bf(w_in[0].T), small]), "gather_w_in")
    wt = g_in.reshape(DIN, D)
    meta = g_small[:, :, 0:256].transpose(1, 0, 2).reshape(NM, D)
    wgu = g_small[:, :, 256:320].transpose(1, 0, 2).reshape(NM, 256)

    xs, tgt = x[0], loss_target[0]
    t = xs.shape[0]
    wfin = final_norm_w.reshape(1, D)
    metapad = jnp.concatenate([meta, jnp.zeros((TM - NM, D), F32)], axis=0)
    wgu_p = _bf(jnp.concatenate([wgu, jnp.zeros((128 - 16, 256), F32)], axis=0))
    tabs = _rope_tables(t)

    w1s, w2s = _bf(w_ff1[0]), _bf(w_ff2[0])
    proj, (w1a,) = _proj_fwd(xs, metapad, norm_mix_w, wt, tabs, _gather_shards([w1s[:HK]]))
    (oswa, lse), (g_out, w1b, w2a) = _swa_fwd(proj, sinks, t, _gather_shards([_bf(w_out[0]), w1s[HK:], w2s[:HK]]))
    (ogla, oraw, sst), (w2b,) = _gla_fwd(proj, wgu_p, b_gate, gla_norm_w, t, _gather_shards([w2s[HK:]]))
    wo, w1, w2 = g_out.reshape(D, D), (w1a, w1b), (w2a, w2b)
    h1, f, a, dh2, loss, gfin = _mlp_fwd(xs, metapad, tgt, ogla, oswa, wo, norm_ff_w, w1, w2, wfin)

    da, dh2b, dh1, do, dwo, gff = _mlp_bwd(h1, a, dh2, ogla, oswa, wo, norm_ff_w, w1, w2)
    dw1, dw2 = _ffn_wgrad(f, a, da, dh2b)
    big = [dwo.reshape(4, 2, 128, D).transpose(1, 0, 2, 3), dw1, dw2]
    (dsq, dsk, dsv, dsink), theirs = _swa_bwd(proj, sinks, lse, do, t, _swap_halves(big))
    sums, sums_bf = _sum_parts(core, [b.reshape((2, -1) + b.shape[3:]) for b in big],
                               [s.reshape((1, -1) + s.shape[2:]) for s in theirs], "sum_cores", True)
    sums = [s.reshape(b.shape[1:]) for s, b in zip(sums, big)]
    sums_bf = [s.reshape(b.shape[1:]) for s, b in zip(sums_bf, big)]
    (dgla, dlr, dwgu, dbg, dgnw), arrived = _gla_bwd(proj, oraw, sst, do, wgu_p, b_gate, gla_norm_w, t,
                                                     _scatter_shards(sums_bf))
    halves, _ = _sum_parts(shard, sums, arrived, "sum_chips", False)
    (gx, gmeta, dwt, gmix), _ = _proj_bwd(xs, metapad, norm_mix_w, wt, tabs, dgla, dsq, dsk, dsv, dlr, dh1)
    gwt_in, joined = _reduce_w_in(dwt.reshape(4, DIN // 4, D), _join_halves(halves))
    gw_out, gw_1, gw_2 = [j.reshape((-1, j.shape[2])) for j in joined]
    g = dict(meta=gmeta, mix=gmix, wgu=dwgu[:16], bg=dbg, gnw=dgnw, sinks=dsink, ff=gff, fin=gfin)

    tail = jnp.concatenate([g["bg"], g["gnw"], g["sinks"], loss, jnp.zeros((1, D - 256 - 128 - 8 - 1), F32)], axis=1)
    pack = jnp.concatenate([g["meta"], g["mix"], g["ff"], g["fin"], tail, g["wgu"].reshape(4, D)], axis=0)
    tot = _allreduce_small(pack)
    g_meta = lax.dynamic_slice_in_dim(tot[0:NM], shard[0] * 256, 256, axis=1)
    g_mix, g_ff, g_fin = tot[16:17], tot[17:18], tot[18]
    g_bg, g_gnw, g_sinks, loss_tot = tot[19:20, 0:256], tot[19:20, 256:384], tot[19:20, 384:392], tot[19, 392]
    g_wgu = lax.dynamic_slice_in_dim(tot[20:24].reshape(NM, 256), shard[0] * 64, 64, axis=1)

    bo = _adamw_big([w_out[0], w_ff1[0], w_ff2[0]], [gw_out, gw_1, gw_2], [m_w_out[0], m_w_ff1[0], m_w_ff2[0]],
                    [v_w_out[0], v_w_ff1[0], v_w_ff2[0]])

    fin2 = lambda a: a.reshape(1, D)
    sw = [meta_tokens, norm_mix_w, w_gate_up[0], b_gate, gla_norm_w, sinks, norm_ff_w, fin2(final_norm_w), w_in[0].T]
    sg = [g_meta, g_mix, g_wgu, g_bg, g_gnw, g_sinks, g_ff, fin2(g_fin), gwt_in]
    sm = [m_meta_tokens, m_norm_mix_w, m_w_gate_up[0], m_b_gate, m_gla_norm_w, m_sinks, m_norm_ff_w, fin2(m_final_norm_w),
          m_w_in[0].T]
    sv = [v_meta_tokens, v_norm_mix_w, v_w_gate_up[0], v_b_gate, v_gla_norm_w, v_sinks, v_norm_ff_w, fin2(v_final_norm_w),
          v_w_in[0].T]
    so = _adamw_small(sw, sg, sm, sv)

    def ordered(small_o, big_o):
        meta_, mix_, wgu_, bg_, gnw_, sinks_, ff_, fin_, wt_ = small_o
        w_out_, w_1_, w_2_ = big_o
        return (meta_, mix_, wt_.T[None], wgu_[None], bg_, gnw_, sinks_, w_out_[None], ff_, w_1_[None], w_2_[None],
                fin_.reshape(D))

    grads = ordered(sg, [gw_out, gw_1, gw_2])
    deltas = ordered(so[0:9], bo[0:3])
    new_m = ordered(so[9:18], bo[3:6])
    new_v = ordered(so[18:27], bo[6:9])
    return (loss_tot, gx[None], *grads, *deltas, *new_m, *new_v)
```

```python
from typing import Callable, NamedTuple

import jax
import jax.numpy as jnp
import numpy as np
from jax import lax
from jax.experimental import pallas as pl
from jax.experimental.pallas import tpu as pltpu

F32 = jnp.float32
BF16 = jnp.bfloat16

D = 1024
DFF = 4096
NM = 16
TM = 256
CH = 64
SB = 128
QB = 2 * SB
EPS = 1e-5
C_GQ, C_GK, C_GV, C_GR, C_SQ, C_SK, C_SV, C_LR, DINP = 0, 256, 512, 1024, 1536, 2048, 2176, 2304, 2432
DIN = 2320
R_LR = 1536
ROPE_THETA = 500000.0
ADAM_LR, ADAM_B1, ADAM_B2, ADAM_EPS, ADAM_WD, ADAM_STEP = 0.001, 0.9, 0.999, 1e-08, 0.01, 10
NEG = -1e30
MESH = pl.DeviceIdType.MESH
VMEM_SPEC = pl.BlockSpec(memory_space=pltpu.VMEM)
ANY_SPEC = pl.BlockSpec(memory_space=pl.ANY)
SMEM_SPEC = pl.BlockSpec(memory_space=pltpu.SMEM)


def _cp(vmem_mb, sem=("arbitrary",)):
    return pltpu.CompilerParams(dimension_semantics=sem, vmem_limit_bytes=vmem_mb << 20)


def _dot(a, b):
    return jnp.dot(a, b, preferred_element_type=F32)


def _dot_nt(a, b):
    return lax.dot_general(a, b, (((1,), (1,)), ((), ())), preferred_element_type=F32)


def _dot_tn(a, b):
    return lax.dot_general(a, b, (((0,), (0,)), ((), ())), preferred_element_type=F32)


def _bf(x):
    return x.astype(BF16)


def _dot3(m01, x):
    x1 = _bf(x)
    r1 = x - x1.astype(F32)
    x2 = _bf(r1)
    x3 = _bf(r1 - x2.astype(F32))
    return _dot(m01, x1) + _dot(m01, x2) + _dot(m01, x3)


def _rms(h):
    rs = lax.rsqrt(jnp.mean(h * h, axis=-1, keepdims=True) + EPS)
    return h * rs, rs


def _rms_bwd(dy, yhat, rs, w):
    dyh = dy * w
    return rs * (dyh - yhat * jnp.mean(dyh * yhat, axis=-1, keepdims=True))


class _Comm(NamedTuple):
    ins: tuple
    outs: tuple
    sems: tuple
    plan: Callable


def _call(body, name, grid, in_specs, out_specs, out_shape, scratch, params, args, comm=None):
    if comm is None:
        outs = pl.pallas_call(body, name=name, grid=grid, in_specs=in_specs, out_specs=out_specs, out_shape=out_shape,
                              scratch_shapes=scratch, compiler_params=params)(*args)
        return outs, None
    n_in, n_out, n_scr = len(in_specs), len(out_specs), len(scratch)
    ci, co = len(comm.ins), len(comm.outs)
    last = grid[0] - 1

    def wrapped(*refs):
        own_in, c_in = refs[:n_in], refs[n_in:n_in + ci]
        refs = refs[n_in + ci:]
        own_out, c_out = refs[:n_out], refs[n_out:n_out + co]
        refs = refs[n_out + co:]
        own_scr, c_sem = refs[:n_scr], refs[n_scr:]
        i = pl.program_id(0)

        @pl.when(i == 0)
        def _():
            for cp in comm.plan(c_in, c_out, c_sem)[0]:
                cp.start()

        body(*own_in, *own_out, *own_scr)

        @pl.when(i == last)
        def _():
            for wait in comm.plan(c_in, c_out, c_sem)[1]:
                wait()

    outs = pl.pallas_call(
        wrapped, name=name, grid=grid, in_specs=list(in_specs) + [ANY_SPEC] * ci, out_specs=list(out_specs) + [ANY_SPEC] * co,
        out_shape=list(out_shape) + list(comm.outs), scratch_shapes=list(scratch) + list(comm.sems), compiler_params=params,
    )(*args, *comm.ins)
    return outs[:n_out], outs[n_out:]


def _run_comm(comm, name):
    ci, co = len(comm.ins), len(comm.outs)

    def body(*refs):
        starts, waits = comm.plan(refs[:ci], refs[ci:ci + co], refs[ci + co:])
        for cp in starts:
            cp.start()
        for wait in waits:
            wait()

    return pl.pallas_call(body, name=name, in_specs=[ANY_SPEC] * ci, out_specs=[ANY_SPEC] * co, out_shape=list(comm.outs),
                          scratch_shapes=list(comm.sems))(*comm.ins)


def _join_shards(w3_ref, w_ref):
    for s in range(4):
        w_ref[(DIN // 4) * s:(DIN // 4) * (s + 1), :] = w3_ref[s]


def _proj_fwd(x, metapad, wm, wt3, tabs, comm=None):
    t = x.shape[0]
    nblk = t // TM

    def body(x_ref, mp_ref, wm_ref, w3_ref, tab_ref, proj_ref, ut_ref, w_ref):
        i = pl.program_id(0)

        @pl.when(i == 0)
        def _():
            _join_shards(w3_ref, w_ref)

        h = jnp.where(i == nblk, mp_ref[...], x_ref[...])
        u, _ = _rms(h)
        ub = _bf(u * wm_ref[...])
        ut_ref[...] = ub.T
        proj_ref[:, 0:C_SQ] = _dot_nt(ub, w_ref[0:R_LR, :])
        att = _dot_nt(ub, w_ref[R_LR + 16:DIN, :])
        tab = tab_ref[...]
        proj_ref[:, C_SQ:C_SK] = _rope(att[:, 0:512], tab, 1.0) * 0.125
        proj_ref[:, C_SK:C_SV] = _rope(att[:, 512:640], tab, 1.0)
        proj_ref[:, C_SV:C_LR] = att[:, 640:768]
        proj_ref[:, C_LR:DINP] = jnp.zeros((TM, DINP - C_LR), F32)
        proj_ref[:, C_LR:C_LR + 16] = _dot_nt(ub, w_ref[R_LR:R_LR + 16, :])

    return _call(
        body, "proj_fwd", (nblk + 1,),
        [pl.BlockSpec((TM, D), lambda i: (jnp.minimum(i, nblk - 1), 0)), VMEM_SPEC, VMEM_SPEC, VMEM_SPEC,
         pl.BlockSpec((TM, 128), lambda i: (i, 0))],
        [pl.BlockSpec((TM, DINP), lambda i: (i, 0)), pl.BlockSpec((D, TM), lambda i: (0, i))],
        [jax.ShapeDtypeStruct((t + TM, DINP), F32), jax.ShapeDtypeStruct((D, t + TM), BF16)],
        [pltpu.VMEM((DIN, D), BF16)], _cp(48), (x, metapad, wm, wt3, tabs), comm)


def _chunk_masks():
    r = lax.broadcasted_iota(jnp.int32, (TM, TM), 0)
    c = lax.broadcasted_iota(jnp.int32, (TM, TM), 1)
    same = (r // CH) == (c // CH)
    lower = _bf(jnp.where(same & (c <= r), 1.0, 0.0))
    upper = _bf(jnp.where(same & (c >= r), 1.0, 0.0))
    return lower, upper


def _gla_gate(lr, wgu, bg, valid, lower):
    z = _dot(_bf(lr), wgu) + bg
    g = (jnp.minimum(z, 0.0) - jnp.log(1.0 + jnp.exp(-jnp.abs(z)))) * (1.0 / 16.0)
    g = jnp.where(valid, g, 0.0)
    return z, _dot3(lower, g)


def _gla_decays(q, k, b):
    nc = TM // CH
    b3 = b.reshape(nc, CH, 256)
    blast = b3[:, CH - 1:CH, :]
    eb = jnp.exp(b)
    enb = jnp.exp(-b)
    ebl = jnp.exp(blast - b3).reshape(TM, 256)
    return eb, enb, ebl, jnp.exp(blast)


def _tri(lower_incl):
    r = lax.broadcasted_iota(jnp.int32, (CH, CH), 0)
    c = lax.broadcasted_iota(jnp.int32, (CH, CH), 1)
    return ((c <= r) if lower_incl else (c >= r))[None]


def _gla_fwd(proj, wgu, bg, gnw, t, comm=None):
    nblk = t // TM
    nt = nblk + 1
    nc = TM // CH

    def blk(i):
        return (i + nblk) % nt

    def body(q_ref, k_ref, v_ref, r_ref, lr_ref, wgu_ref, bg_ref, gnw_ref, o_ref, oraw_ref, sst_ref, st_scr):
        i = pl.program_id(0)

        @pl.when(i == 0)
        def _():
            st_scr[...] = jnp.zeros_like(st_scr)

        rows = blk(i) * TM + lax.broadcasted_iota(jnp.int32, (TM, 1), 0)
        lower, _ = _chunk_masks()
        _, b = _gla_gate(lr_ref[...], wgu_ref[...], bg_ref[...], rows < t + NM, lower)
        q = q_ref[...]
        k = k_ref[...]
        eb, enb, ebl, eblast = _gla_decays(q, k, b)
        qt = q * 0.125 * eb
        kt = k * enb
        kh = k * ebl
        tril = _tri(True)
        outs = []
        for h in range(4):
            hs = slice(h * CH, (h + 1) * CH)
            qh = _bf(qt[:, hs]).reshape(nc, CH, CH)
            kth = _bf(kt[:, hs]).reshape(nc, CH, CH)
            khh = _bf(kh[:, hs]).reshape(nc, CH, CH)
            vh = _bf(v_ref[:, h * 128:(h + 1) * 128]).reshape(nc, CH, 128)
            a = jnp.einsum('cid,cjd->cij', qh, kth, preferred_element_type=F32)
            a = jnp.where(tril, a, 0.0)
            o = jnp.einsum('cij,cjv->civ', _bf(a), vh, preferred_element_type=F32)
            kv = jnp.einsum('cjv,cjd->cvd', vh, khh, preferred_element_type=F32)
            st = st_scr[h]
            o_inter = []
            for c in range(nc):
                sst_ref[c, h] = st
                o_inter.append(_dot_nt(qh[c], _bf(st)))
                st = st * eblast[c, :, hs] + kv[c]
            st_scr[h] = st
            outs.append((o + jnp.stack(o_inter)).reshape(TM, 128))
        oraw = jnp.concatenate(outs, axis=1)
        oraw_ref[...] = oraw
        gn = gnw_ref[...]
        res = []
        for h in range(4):
            on, _ = _rms(oraw[:, h * 128:(h + 1) * 128])
            r = r_ref[:, h * 128:(h + 1) * 128]
            res.append(on * gn * (r * jax.nn.sigmoid(r)))
        o_ref[...] = _bf(jnp.concatenate(res, axis=1))

    def spec(w, cb):
        return pl.BlockSpec((TM, w), lambda i: (blk(i), cb))

    return _call(
        body, "gla_fwd", (nt,),
        [spec(256, 0), spec(256, 1), spec(512, 1), spec(512, 2), spec(128, C_LR // 128), VMEM_SPEC, VMEM_SPEC, VMEM_SPEC],
        [spec(512, 0), spec(512, 0), pl.BlockSpec((nc, 4, 128, CH), lambda i: (blk(i), 0, 0, 0))],
        [jax.ShapeDtypeStruct((t + TM, 512), BF16), jax.ShapeDtypeStruct((t + TM, 512), F32),
         jax.ShapeDtypeStruct((nt * nc, 4, 128, CH), F32)],
        [pltpu.VMEM((4, 128, CH), F32)], _cp(40), (proj, proj, proj, proj, proj, wgu, bg, gnw), comm)


def _gla_bwd(proj, oraw, sst, do, wgu, bg, gnw, t, comm=None):
    nblk = t // TM
    nt = nblk + 1
    nc = TM // CH

    def blk(i):
        return (2 * nblk - i) % nt

    def body(q_ref, k_ref, v_ref, r_ref, lr_ref, oraw_ref, sst_ref, do_ref, wgu_ref, bg_ref, gnw_ref,
             dgla_ref, dlr_ref, dwgu_ref, dbg_ref, dgnw_ref, dst_scr):
        i = pl.program_id(0)

        @pl.when(i == 0)
        def _():
            dst_scr[...] = jnp.zeros_like(dst_scr)
            dwgu_ref[...] = jnp.zeros_like(dwgu_ref)
            dbg_ref[...] = jnp.zeros_like(dbg_ref)
            dgnw_ref[...] = jnp.zeros_like(dgnw_ref)

        rows = blk(i) * TM + lax.broadcasted_iota(jnp.int32, (TM, 1), 0)
        valid = rows < t + NM
        lower, upper = _chunk_masks()
        lr = lr_ref[...]
        z, b = _gla_gate(lr, wgu_ref[...], bg_ref[...], valid, lower)
        q = q_ref[...]
        k = k_ref[...]
        eb, enb, ebl, eblast = _gla_decays(q, k, b)
        qt = q * 0.125 * eb
        kt = k * enb
        kh = k * ebl
        gn = gnw_ref[...]
        tril = _tri(True)
        triu = _tri(False)
        dq_l, dk_l, dv_l, dr_l, db_l, ex_l = [], [], [], [], [], []
        dgn = jnp.zeros((1, 128), F32)
        for h in range(4):
            hs = slice(h * CH, (h + 1) * CH)
            vs = slice(h * 128, (h + 1) * 128)
            on, rs = _rms(oraw_ref[:, vs])
            r = r_ref[:, vs]
            sig = jax.nn.sigmoid(r)
            sil = r * sig
            dy = do_ref[:, vs]
            dr_l.append(dy * on * gn * (sig * (1.0 + r * (1.0 - sig))))
            dgn = dgn + jnp.sum(dy * sil * on, axis=0, keepdims=True)
            doraw = _rms_bwd(dy * sil, on, rs, gn)
            qtf = qt[:, hs].reshape(nc, CH, CH)
            ktf = kt[:, hs].reshape(nc, CH, CH)
            khf = kh[:, hs].reshape(nc, CH, CH)
            qh, kth, khh = _bf(qtf), _bf(ktf), _bf(khf)
            vh = _bf(v_ref[:, vs]).reshape(nc, CH, 128)
            doh = _bf(doraw).reshape(nc, CH, 128)
            at = jnp.where(triu, jnp.einsum('cjd,cid->cji', kth, qh, preferred_element_type=F32), 0.0)
            da = jnp.where(tril, jnp.einsum('civ,cjv->cij', doh, vh, preferred_element_type=F32), 0.0)
            dat = jnp.where(triu, jnp.einsum('cjv,civ->cji', vh, doh, preferred_element_type=F32), 0.0)
            dv = jnp.einsum('cji,civ->cjv', _bf(at), doh, preferred_element_type=F32)
            dqt = jnp.einsum('cij,cjd->cid', _bf(da), kth, preferred_element_type=F32)
            dkt = jnp.einsum('cji,cid->cjd', _bf(dat), qh, preferred_element_type=F32)
            gq = jnp.einsum('civ,cid->cvd', doh, qh, preferred_element_type=F32)
            dst = dst_scr[h]
            dsend = [None] * nc
            for c in reversed(range(nc)):
                dsend[c] = dst
                dst = dst * eblast[c, :, hs] + gq[c]
            dst_scr[h] = dst
            dse = jnp.stack(dsend)
            dseb = _bf(dse)
            stf = sst_ref[:, h]
            dqt = dqt + jnp.einsum('civ,cvd->cid', doh, _bf(stf), preferred_element_type=F32)
            dv = dv + jnp.einsum('cjd,cvd->cjv', khh, dseb, preferred_element_type=F32)
            dkh = jnp.einsum('cjv,cvd->cjd', vh, dseb, preferred_element_type=F32)
            extra = (jnp.sum(dkh * khf, axis=1, keepdims=True)
                     + eblast[:, :, hs] * jnp.sum(dse * stf, axis=1, keepdims=True))
            db_l.append((dqt * qtf - dkt * ktf - dkh * khf).reshape(TM, CH))
            ex_l.append(jnp.broadcast_to(extra, (nc, CH, CH)).reshape(TM, CH))
            dq_l.append((dqt.reshape(TM, CH)) * eb[:, hs] * 0.125)
            dk_l.append(dkt.reshape(TM, CH) * enb[:, hs] + dkh.reshape(TM, CH) * ebl[:, hs])
            dv_l.append(dv.reshape(TM, 128))
        dgnw_ref[...] += dgn
        db = jnp.concatenate(db_l, axis=1)
        dg = _dot3(upper, db) + jnp.concatenate(ex_l, axis=1)
        dz = jnp.where(valid, dg * (1.0 / 16.0) / (1.0 + jnp.exp(z)), 0.0)
        dzb = _bf(dz)
        dlr_ref[...] = _bf(_dot_nt(dzb, wgu_ref[...]))
        dwgu_ref[...] += _dot_tn(_bf(lr), dzb)
        dbg_ref[...] += jnp.sum(dz, axis=0, keepdims=True)
        dgla_ref[...] = _bf(jnp.concatenate(dq_l + dk_l + dv_l + dr_l, axis=1))

    def spec(w, cb):
        return pl.BlockSpec((TM, w), lambda i: (blk(i), cb))

    def acc(shape):
        return pl.BlockSpec(shape, lambda i: (0, 0))

    return _call(
        body, "gla_bwd", (nt,),
        [spec(256, 0), spec(256, 1), spec(512, 1), spec(512, 2), spec(128, C_LR // 128), spec(512, 0),
         pl.BlockSpec((nc, 4, 128, CH), lambda i: (blk(i), 0, 0, 0)), spec(512, 0), VMEM_SPEC, VMEM_SPEC, VMEM_SPEC],
        [spec(1536, 0), spec(128, 0), acc((128, 256)), acc((1, 256)), acc((1, 128))],
        [jax.ShapeDtypeStruct((t + TM, 1536), BF16), jax.ShapeDtypeStruct((t + TM, 128), BF16),
         jax.ShapeDtypeStruct((128, 256), F32), jax.ShapeDtypeStruct((1, 256), F32), jax.ShapeDtypeStruct((1, 128), F32)],
        [pltpu.VMEM((4, 128, CH), F32)], _cp(48), (proj, proj, proj, proj, proj, oraw, sst, do, wgu, bg, gnw), comm)


def _rope_tables(t):
    r = t + TM
    row = np.arange(r)
    pos = np.where(row < t, row + NM, np.where(row < t + NM, row - t, 0)).astype(np.float32)
    inv_freq = (1.0 / (np.float32(ROPE_THETA) ** (np.arange(0, 16, 2, dtype=np.float32) / np.float32(16)))).astype(np.float32)
    ang = (pos[:, None] * inv_freq[None, :]).astype(np.float32)
    cos, sin = np.cos(ang).astype(np.float32), np.sin(ang).astype(np.float32)
    one, zero = np.ones((r, 48), np.float32), np.zeros((r, 48), np.float32)
    return jnp.asarray(np.concatenate([cos, cos, one, -sin, sin, zero], axis=1))


def _rope(x, tab, sign):
    w = x.shape[1]
    rep = w // 64
    c = jnp.concatenate([tab[:, 0:64]] * rep, axis=1)
    s = jnp.concatenate([tab[:, 64:128]] * rep, axis=1)
    lane = lax.rem(lax.broadcasted_iota(jnp.int32, x.shape, 1), 64)
    partner = jnp.where(lane < 8, pltpu.roll(x, w - 8, 1), jnp.where(lane < 16, pltpu.roll(x, 8, 1), 0.0))
    return x * c + sign * (partner * s)


def _stack(x):
    return jnp.concatenate([x[:, g * 64:(g + 1) * 64] for g in range(4)], axis=0)


def _unstack(x):
    return jnp.concatenate([x[g * SB:(g + 1) * SB] for g in range(4)], axis=1)


def _swa_masks(b, nsb):
    r = lax.rem(lax.broadcasted_iota(jnp.int32, (4 * SB, SB), 0), SB)
    c = lax.broadcasted_iota(jnp.int32, (4 * SB, SB), 1)
    real = b < nsb
    return c <= r, (c > r) & (b > 0) & real, (c < NM) & real


def _swa_specs(nsb):
    def rows(h, w, cb, f):
        return pl.BlockSpec((h, w), lambda i: (f(i), cb))
    pair = lambda i: i
    prev = lambda i: jnp.maximum(2 * i - 1, 0)
    meta = lambda i: nsb
    return rows, pair, prev, meta


def _swa_scores(b, nsb, sink_ref, q, kc, kp, km):
    mc, mp, mm = _swa_masks(b, nsb)
    per_kv = []
    for kv in range(2):
        ks = slice(kv * 64, (kv + 1) * 64)
        qg = _bf(_stack(q[:, kv * 256:(kv + 1) * 256]))
        kcb, kpb, kmb = _bf(kc[:, ks]), _bf(kp[:, ks]), _bf(km[:, ks])
        s_c = jnp.where(mc, _dot_nt(qg, kcb), NEG)
        s_p = jnp.where(mp, _dot_nt(qg, kpb), NEG)
        s_m = jnp.where(mm, _dot_nt(qg, kmb), NEG)
        sink = jnp.concatenate([jnp.full((SB, 1), sink_ref[0, kv * 4 + g], F32) for g in range(4)], axis=0)
        per_kv.append((qg, kcb, kpb, kmb, s_c, s_p, s_m, sink))
    return per_kv


def _swa_fwd(proj, sinks, t, comm=None):
    nsb = t // SB
    r_tot = t + TM
    rows, pair, prev, meta = _swa_specs(nsb)

    def body(sink_ref, q_ref, kc_ref, kp_ref, km_ref, vc_ref, vp_ref, vm_ref, o_ref, lse_ref):
        i = pl.program_id(0)
        km, vm = km_ref[...], vm_ref[...]
        for j in range(2):
            b = 2 * i + j
            rs = slice(j * SB, (j + 1) * SB)
            kp = kp_ref[...] if j == 0 else kc_ref[0:SB, :]
            vp = vp_ref[...] if j == 0 else vc_ref[0:SB, :]
            vc = vc_ref[rs, :]
            o_l, lse_l = [], []
            for kv, (qg, kcb, kpb, kmb, s_c, s_p, s_m, sink) in enumerate(
                    _swa_scores(b, nsb, sink_ref, q_ref[rs, :], kc_ref[rs, :], kp, km)):
                ks = slice(kv * 64, (kv + 1) * 64)
                m = jnp.maximum(jnp.max(jnp.maximum(jnp.maximum(s_c, s_p), s_m), -1, keepdims=True), sink)
                p_c, p_p, p_m = jnp.exp(s_c - m), jnp.exp(s_p - m), jnp.exp(s_m - m)
                l = jnp.sum(p_c + p_p + p_m, -1, keepdims=True) + jnp.exp(sink - m)
                o = _dot(_bf(p_c), _bf(vc[:, ks])) + _dot(_bf(p_p), _bf(vp[:, ks])) + _dot(_bf(p_m), _bf(vm[:, ks]))
                o_l.append(_unstack(o * (1.0 / l)))
                lse_l.append(_unstack(m + jnp.log(l)))
            valid = b * SB + lax.broadcasted_iota(jnp.int32, (SB, 1), 0) < t + NM
            o_ref[rs, :] = _bf(jnp.where(valid, jnp.concatenate(o_l, axis=1), 0.0))
            lse_ref[rs, :] = jnp.concatenate(lse_l, axis=1)

    ck, cv = C_SK // 128, C_SV // 128
    return _call(
        body, "swa_fwd", (r_tot // QB,),
        [SMEM_SPEC, rows(QB, 512, C_SQ // 512, pair),
         rows(QB, 128, ck, pair), rows(SB, 128, ck, prev), rows(SB, 128, ck, meta),
         rows(QB, 128, cv, pair), rows(SB, 128, cv, prev), rows(SB, 128, cv, meta)],
        [rows(QB, 512, 0, pair), rows(QB, 8, 0, pair)],
        [jax.ShapeDtypeStruct((r_tot, 512), BF16), jax.ShapeDtypeStruct((r_tot, 8), F32)],
        [], _cp(32), (sinks, proj, proj, proj, proj, proj, proj, proj), comm)


def _swa_bwd(proj, sinks, lse, do, t, comm=None):
    nsb = t // SB
    r_tot = t + TM
    rows, pair, prev, meta = _swa_specs(nsb)

    def body(sink_ref, q_ref, kc_ref, kp_ref, km_ref, vc_ref, vp_ref, vm_ref, lse_ref, do_ref,
             dq_ref, dk_ref, dv_ref, dsink_ref):
        i = pl.program_id(0)

        @pl.when(i == 0)
        def _():
            dk_ref[...] = jnp.zeros_like(dk_ref)
            dv_ref[...] = jnp.zeros_like(dv_ref)
            dsink_ref[...] = jnp.zeros_like(dsink_ref)

        km, vm = km_ref[...], vm_ref[...]
        dsink = jnp.zeros((1, 8), F32)
        for j in range(2):
            b = 2 * i + j
            rs = slice(j * SB, (j + 1) * SB)
            kp = kp_ref[...] if j == 0 else kc_ref[0:SB, :]
            vp = vp_ref[...] if j == 0 else vc_ref[0:SB, :]
            vc = vc_ref[rs, :]
            lse_all = lse_ref[rs, :]
            dq_l, dkc_l, dkp_l, dkm_l, dvc_l, dvp_l, dvm_l, ds_l = [], [], [], [], [], [], [], []
            for kv, (qg, kcb, kpb, kmb, s_c, s_p, s_m, sink) in enumerate(
                    _swa_scores(b, nsb, sink_ref, q_ref[rs, :], kc_ref[rs, :], kp, km)):
                ks = slice(kv * 64, (kv + 1) * 64)
                lse_g = _stack_cols(lse_all[:, kv * 4:(kv + 1) * 4])
                dog = _bf(_stack(do_ref[rs, kv * 256:(kv + 1) * 256]))
                vcb, vpb, vmb = _bf(vc[:, ks]), _bf(vp[:, ks]), _bf(vm[:, ks])
                p_c, p_p, p_m = jnp.exp(s_c - lse_g), jnp.exp(s_p - lse_g), jnp.exp(s_m - lse_g)
                dp_c, dp_p, dp_m = _dot_nt(dog, vcb), _dot_nt(dog, vpb), _dot_nt(dog, vmb)
                delta = jnp.sum(p_c * dp_c + p_p * dp_p + p_m * dp_m, -1, keepdims=True)
                ds_c, ds_p, ds_m = _bf(p_c * (dp_c - delta)), _bf(p_p * (dp_p - delta)), _bf(p_m * (dp_m - delta))
                dq_l.append(_unstack(_dot(ds_c, kcb) + _dot(ds_p, kpb) + _dot(ds_m, kmb)))
                dkc_l.append(_dot_tn(ds_c, qg))
                dkp_l.append(_dot_tn(ds_p, qg))
                dkm_l.append(_dot_tn(ds_m, qg))
                dvc_l.append(_dot_tn(_bf(p_c), dog))
                dvp_l.append(_dot_tn(_bf(p_p), dog))
                dvm_l.append(_dot_tn(_bf(p_m), dog))
                ds_l.append(_unstack(-jnp.exp(sink - lse_g) * delta))
            dq_ref[rs, :] = jnp.concatenate(dq_l, axis=1)
            c0 = pl.multiple_of(b * SB, SB)
            p0 = pl.multiple_of(jnp.maximum(b - 1, 0) * SB, SB)
            dk_ref[pl.ds(c0, SB), :] += jnp.concatenate(dkc_l, axis=1)
            dk_ref[pl.ds(p0, SB), :] += jnp.concatenate(dkp_l, axis=1)
            dk_ref[pl.ds(t, SB), :] += jnp.concatenate(dkm_l, axis=1)
            dv_ref[pl.ds(c0, SB), :] += jnp.concatenate(dvc_l, axis=1)
            dv_ref[pl.ds(p0, SB), :] += jnp.concatenate(dvp_l, axis=1)
            dv_ref[pl.ds(t, SB), :] += jnp.concatenate(dvm_l, axis=1)
            dsink = dsink + jnp.sum(jnp.concatenate(ds_l, axis=1), axis=0, keepdims=True)
        dsink_ref[...] += dsink

    ck, cv = C_SK // 128, C_SV // 128
    whole = lambda w: pl.BlockSpec((r_tot, w), lambda i: (0, 0))
    return _call(
        body, "swa_bwd", (r_tot // QB,),
        [SMEM_SPEC, rows(QB, 512, C_SQ // 512, pair),
         rows(QB, 128, ck, pair), rows(SB, 128, ck, prev), rows(SB, 128, ck, meta),
         rows(QB, 128, cv, pair), rows(SB, 128, cv, prev), rows(SB, 128, cv, meta),
         rows(QB, 8, 0, pair), rows(QB, 512, 1, pair)],
        [rows(QB, 512, 0, pair), whole(128), whole(128), pl.BlockSpec((1, 8), lambda i: (0, 0))],
        [jax.ShapeDtypeStruct((r_tot, 512), F32), jax.ShapeDtypeStruct((r_tot, 128), F32),
         jax.ShapeDtypeStruct((r_tot, 128), F32), jax.ShapeDtypeStruct((1, 8), F32)],
        [], _cp(48), (sinks, proj, proj, proj, proj, proj, proj, proj, lse, do), comm)


def _stack_cols(x):
    return jnp.concatenate([x[:, g:g + 1] for g in range(4)], axis=0)


HK = D // 2


def _mlp_fwd(x, metapad, tgt, ogla, oswa, wo, wff, w1, w2, wfin):
    t = x.shape[0]
    nblk = t // TM

    def body(x_ref, mp_ref, tgt_ref, og_ref, os_ref, wo_ref, wff_ref, w1a_ref, w1b_ref, w2a_ref, w2b_ref, wfin_ref,
             h1_ref, ft_ref, a_ref, at_ref, dh2_ref, loss_ref, gfin_ref):
        i = pl.program_id(0)

        @pl.when(i == 0)
        def _():
            loss_ref[...] = jnp.zeros_like(loss_ref)
            gfin_ref[...] = jnp.zeros_like(gfin_ref)

        h0 = jnp.where(i == nblk, mp_ref[...], x_ref[...])
        h1 = h0 + _dot(og_ref[...], wo_ref[0:512, :]) + _dot(os_ref[...], wo_ref[512:1024, :])
        h1_ref[...] = h1
        fh, _ = _rms(h1)
        f = _bf(fh * wff_ref[...])
        ft_ref[...] = f.T
        acc = jnp.zeros((TM, D), F32)
        for n in range(4):
            a = _dot(f[:, 0:HK], w1a_ref[n]) + _dot(f[:, HK:D], w1b_ref[n])
            ab = _bf(a)
            a_ref[:, n * D:(n + 1) * D] = ab
            at_ref[n * D:(n + 1) * D, :] = ab.T
            zr = jnp.maximum(a, 0.0)
            z = _bf(zr * zr)
            acc = acc + _dot(z[:, 0:HK], w2a_ref[n]) + _dot(z[:, HK:D], w2b_ref[n])
        h2 = h1 + acc
        yh, rs2 = _rms(h2)
        wf = wfin_ref[...]
        real = i < nblk
        e = jnp.where(real, yh * wf - tgt_ref[...], 0.0)
        loss_ref[...] += jnp.sum(jnp.sum(e * e, axis=0, keepdims=True), axis=1, keepdims=True) * (0.5 / D)
        dy = e * (1.0 / D)
        gfin_ref[...] += jnp.sum(dy * yh, axis=0, keepdims=True)
        dh2_ref[...] = _rms_bwd(dy, yh, rs2, wf)

    xs = pl.BlockSpec((TM, D), lambda i: (jnp.minimum(i, nblk - 1), 0))
    rs = lambda w: pl.BlockSpec((TM, w), lambda i: (i, 0))
    cs = lambda h: pl.BlockSpec((h, TM), lambda i: (0, i))
    r_tot = t + TM
    return pl.pallas_call(
        body, name="mlp_fwd", grid=(nblk + 1,),
        in_specs=[xs, VMEM_SPEC, xs, rs(512), rs(512)] + [VMEM_SPEC] * 7,
        out_specs=[rs(D), cs(D), rs(DFF), cs(DFF), rs(D), pl.BlockSpec((1, 1), lambda i: (0, 0)),
                   pl.BlockSpec((1, D), lambda i: (0, 0))],
        out_shape=[jax.ShapeDtypeStruct((r_tot, D), F32), jax.ShapeDtypeStruct((D, r_tot), BF16),
                   jax.ShapeDtypeStruct((r_tot, DFF), BF16), jax.ShapeDtypeStruct((DFF, r_tot), BF16),
                   jax.ShapeDtypeStruct((r_tot, D), F32), jax.ShapeDtypeStruct((1, 1), F32), jax.ShapeDtypeStruct((1, D), F32)],
        compiler_params=_cp(56),
    )(x, metapad, tgt, ogla, oswa, wo, wff, *w1, *w2, wfin)


def _mlp_bwd(h1, a, dh2, ogla, oswa, wo, wff, w1, w2):
    r_tot = h1.shape[0]
    nt = r_tot // TM

    def body(h1_ref, a_ref, dh2_ref, og_ref, os_ref, wo_ref, wff_ref, w1a_ref, w1b_ref, w2a_ref, w2b_ref,
             da_ref, dh2b_ref, dh1_ref, do_ref, dwo_ref, gff_ref):
        i = pl.program_id(0)

        @pl.when(i == 0)
        def _():
            dwo_ref[...] = jnp.zeros_like(dwo_ref)
            gff_ref[...] = jnp.zeros_like(gff_ref)

        dh2 = dh2_ref[...]
        dh2b = _bf(dh2)
        dh2b_ref[...] = dh2b
        dfa = jnp.zeros((TM, HK), F32)
        dfb = jnp.zeros((TM, HK), F32)
        for n in range(4):
            dz = jnp.concatenate([_dot_nt(dh2b, w2a_ref[n]), _dot_nt(dh2b, w2b_ref[n])], axis=1)
            da = _bf(dz * (2.0 * jnp.maximum(a_ref[:, n * D:(n + 1) * D].astype(F32), 0.0)))
            da_ref[:, n * D:(n + 1) * D] = da
            dfa = dfa + _dot_nt(da, w1a_ref[n])
            dfb = dfb + _dot_nt(da, w1b_ref[n])
        df = jnp.concatenate([dfa, dfb], axis=1)
        fh, rs1 = _rms(h1_ref[...])
        gff_ref[...] += jnp.sum(df * fh, axis=0, keepdims=True)
        dh1 = dh2 + _rms_bwd(df, fh, rs1, wff_ref[...])
        dh1_ref[...] = dh1
        dh1b = _bf(dh1)
        do_ref[...] = _dot_nt(dh1b, wo_ref[...])
        dwo_ref[0:512, :] += _dot_tn(og_ref[...], dh1b)
        dwo_ref[512:1024, :] += _dot_tn(os_ref[...], dh1b)

    rs = lambda w: pl.BlockSpec((TM, w), lambda i: (i, 0))
    return pl.pallas_call(
        body, name="mlp_bwd", grid=(nt,),
        in_specs=[rs(D), rs(DFF), rs(D), rs(512), rs(512)] + [VMEM_SPEC] * 6,
        out_specs=[rs(DFF), rs(D), rs(D), rs(D), pl.BlockSpec((D, D), lambda i: (0, 0)),
                   pl.BlockSpec((1, D), lambda i: (0, 0))],
        out_shape=[jax.ShapeDtypeStruct((r_tot, DFF), BF16), jax.ShapeDtypeStruct((r_tot, D), BF16),
                   jax.ShapeDtypeStruct((r_tot, D), F32), jax.ShapeDtypeStruct((r_tot, D), F32),
                   jax.ShapeDtypeStruct((D, D), F32), jax.ShapeDtypeStruct((1, D), F32)],
        compiler_params=_cp(56),
    )(h1, a, dh2, ogla, oswa, wo, wff, *w1, *w2)


WG_ROWS = 256


def _ffn_wgrad(ft, at, da, dh2b):
    r_tot = ft.shape[1]
    kt = 768 if r_tot % 768 == 0 else TM
    nk = r_tot // kt

    def body(ft_ref, at_ref, da_ref, dh2_ref, dw1_ref, dw2_ref, acc1, acc2):
        k = pl.program_id(1)

        @pl.when(k == 0)
        def _():
            acc1[...] = jnp.zeros_like(acc1)
            acc2[...] = jnp.zeros_like(acc2)

        da, dh2 = da_ref[...], dh2_ref[...]
        for m in range(0, D, WG_ROWS):
            ms = slice(m, m + WG_ROWS)
            acc1[ms, :] += _dot(ft_ref[ms, :], da)
            zr = jnp.maximum(at_ref[ms, :], 0.0)
            acc2[ms, :] += _dot(zr * zr, dh2)

        @pl.when(k == nk - 1)
        def _():
            for hh in range(2):
                dw1_ref[hh, 0] = acc1[hh * 512:(hh + 1) * 512, :]
                dw2_ref[hh, 0] = acc2[hh * 512:(hh + 1) * 512, :]

    out = pl.BlockSpec((2, 1, 512, D), lambda n, k: (0, n, 0, 0))
    return pl.pallas_call(
        body, name="ffn_wgrad", grid=(4, nk),
        in_specs=[pl.BlockSpec((D, kt), lambda n, k: (0, k)), pl.BlockSpec((D, kt), lambda n, k: (n, k)),
                  pl.BlockSpec((kt, D), lambda n, k: (k, n)), pl.BlockSpec((kt, D), lambda n, k: (k, 0))],
        out_specs=[out, out],
        out_shape=[jax.ShapeDtypeStruct((2, 4, 512, D), F32)] * 2,
        scratch_shapes=[pltpu.VMEM((D, D), F32), pltpu.VMEM((D, D), F32)],
        compiler_params=_cp(48, ("arbitrary", "arbitrary")),
    )(ft, at, da, dh2b)


def _proj_bwd(x, metapad, wm, wt3, tabs, ut, dgla, dswa_q, dsk, dsv, dlr, dh1, comm=None):
    t = x.shape[0]
    nblk = t // TM

    def body(x_ref, mp_ref, wm_ref, w3_ref, tab_ref, ut_ref, dg_ref, dq_ref, dk_ref, dv_ref, dlr_ref, dh1_ref,
             gx_ref, gmeta_ref, dw_ref, gmix_ref, w_ref, acc):
        i = pl.program_id(0)

        @pl.when(i == 0)
        def _():
            _join_shards(w3_ref, w_ref)
            acc[...] = jnp.zeros_like(acc)
            gmix_ref[...] = jnp.zeros_like(gmix_ref)

        h = jnp.where(i == nblk, mp_ref[...], x_ref[...])
        uh, rs = _rms(h)
        wm_v = wm_ref[...]
        ut = ut_ref[...]
        tab = tab_ref[...]
        dq = _bf(_rope(dq_ref[...] * 0.125, tab, -1.0))
        dk = _bf(_rope(dk_ref[...], tab, -1.0))
        dlr = dlr_ref[...]
        parts = ((dg_ref[...], 0, R_LR, C_GQ, R_LR), (dlr, R_LR, 16, C_LR, 128), (dq, R_LR + 16, 512, C_SQ, 512),
                 (dk, R_LR + 528, 128, C_SK, 128), (_bf(dv_ref[...]), R_LR + 656, 128, C_SV, 128))
        du = jnp.zeros((TM, D), F32)
        for val, r0, nr, c0, nc in parts:
            du = du + _dot(val[:, 0:nr], w_ref[r0:r0 + nr, :])
            acc[:, c0:c0 + nc] += _dot(ut, val)
        gmix_ref[...] += jnp.sum(du * uh, axis=0, keepdims=True)
        dh0 = dh1_ref[...] + _rms_bwd(du, uh, rs, wm_v)

        @pl.when(i < nblk)
        def _():
            gx_ref[...] = dh0

        @pl.when(i == nblk)
        def _():
            gmeta_ref[...] = dh0[:NM]
            for c in range(0, C_LR, 256):
                r = c if c < C_SQ else c + 16
                dw_ref[r:r + 256, :] = acc[:, c:c + 256].T
            dw_ref[R_LR:R_LR + 16, :] = acc[:, C_LR:DINP].T[0:16, :]

    xs = pl.BlockSpec((TM, D), lambda i: (jnp.minimum(i, nblk - 1), 0))
    rs_ = lambda w: pl.BlockSpec((TM, w), lambda i: (i, 0))
    return _call(
        body, "proj_bwd", (nblk + 1,),
        [xs, VMEM_SPEC, VMEM_SPEC, VMEM_SPEC, rs_(128), pl.BlockSpec((D, TM), lambda i: (0, i)),
         rs_(1536), rs_(512), rs_(128), rs_(128), rs_(128), rs_(D)],
        [xs, pl.BlockSpec((NM, D), lambda i: (0, 0)), VMEM_SPEC, pl.BlockSpec((1, D), lambda i: (0, 0))],
        [jax.ShapeDtypeStruct((t, D), F32), jax.ShapeDtypeStruct((NM, D), F32),
         jax.ShapeDtypeStruct((DIN, D), F32), jax.ShapeDtypeStruct((1, D), F32)],
        [pltpu.VMEM((DIN, D), BF16), pltpu.VMEM((D, DINP), F32)], _cp(56),
        (x, metapad, wm, wt3, tabs, ut, dgla, dswa_q, dsk, dsv, dlr, dh1), comm)


def _place():
    return lax.axis_index("x"), lax.axis_index("y"), lax.axis_index("c")


def _other_chips(x, y):
    return [(1 - x, y), (x, 1 - y), (1 - x, 1 - y)]


def _dma_sems(*counts):
    return tuple(pltpu.SemaphoreType.DMA((k,)) for k in counts)


def _gather_shards(shards):
    n = len(shards)

    def plan(ins, outs, sems):
        send, recv, loc = sems
        x, y, c = _place()
        chips = _other_chips(x, y)

        def remote(a, k, shard_of):
            tx, ty = chips[k]
            sx, sy = shard_of
            return pltpu.make_async_remote_copy(
                src_ref=ins[a], dst_ref=outs[a].at[2 * sx + sy], send_sem=send.at[3 * a + k], recv_sem=recv.at[3 * a + k],
                device_id=(tx, ty, c), device_id_type=MESH)

        local = [pltpu.make_async_copy(ins[a], outs[a].at[2 * x + y], loc.at[a]) for a in range(n)]
        sends = [remote(a, k, (x, y)) for a in range(n) for k in range(3)]
        waits = ([lambda a=a, k=k: remote(a, k, chips[k]).wait_recv() for a in range(n) for k in range(3)]
                 + [cp.wait_send for cp in sends] + [cp.wait for cp in local])
        return local + sends, waits

    return _Comm(tuple(shards), tuple(jax.ShapeDtypeStruct((4,) + s.shape, s.dtype) for s in shards),
                 _dma_sems(3 * n, 3 * n, n), plan)


def _swap_halves(grads):
    n = len(grads)

    def plan(ins, outs, sems):
        send, recv = sems
        x, y, c = _place()
        cps = [pltpu.make_async_remote_copy(
            src_ref=ins[a].at[1 - c], dst_ref=outs[a], send_sem=send.at[a], recv_sem=recv.at[a],
            device_id=(x, y, 1 - c), device_id_type=MESH) for a in range(n)]
        return cps, [cp.wait for cp in cps]

    return _Comm(tuple(grads), tuple(jax.ShapeDtypeStruct(g.shape[1:], g.dtype) for g in grads), _dma_sems(n, n), plan)


def _scatter_shards(parts):
    n = len(parts)

    def plan(ins, outs, sems):
        send, recv = sems
        x, y, c = _place()
        cps = [pltpu.make_async_remote_copy(
            src_ref=ins[a].at[2 * tx + ty], dst_ref=outs[a].at[k], send_sem=send.at[3 * a + k],
            recv_sem=recv.at[3 * a + k], device_id=(tx, ty, c), device_id_type=MESH)
            for a in range(n) for k, (tx, ty) in enumerate(_other_chips(x, y))]
        return cps, [cp.wait for cp in cps]

    return _Comm(tuple(parts), tuple(jax.ShapeDtypeStruct((3,) + p.shape[1:], p.dtype) for p in parts),
                 _dma_sems(3 * n, 3 * n), plan)


def _join_halves(halves):
    n = len(halves)

    def plan(ins, outs, sems):
        send, recv, loc = sems
        x, y, c = _place()

        def remote(a, half):
            return pltpu.make_async_remote_copy(
                src_ref=ins[a], dst_ref=outs[a].at[half], send_sem=send.at[a], recv_sem=recv.at[a],
                device_id=(x, y, 1 - c), device_id_type=MESH)

        local = [pltpu.make_async_copy(ins[a], outs[a].at[c], loc.at[a]) for a in range(n)]
        sends = [remote(a, c) for a in range(n)]
        waits = ([lambda a=a: remote(a, 1 - c).wait_recv() for a in range(n)] + [cp.wait_send for cp in sends]
                 + [cp.wait for cp in local])
        return local + sends, waits

    return _Comm(tuple(halves), tuple(jax.ShapeDtypeStruct((2,) + h.shape, h.dtype) for h in halves),
                 _dma_sems(n, n, n), plan)


def _reduce_tail(dwt, pack, comm):
    rows = DIN // 4
    p = pack.shape[0]
    ci, co = len(comm.ins), len(comm.outs)

    def body(*refs):
        dw_ref, pack_ref, c_in = refs[0], refs[1], refs[2:2 + ci]
        out_ref, tot_ref, c_out = refs[2 + ci], refs[3 + ci], refs[4 + ci:4 + ci + co]
        flat, mine, tosend, rbuf, qbuf, sib, pbuf, send, recv, loc, psend, precv = refs[4 + ci + co:16 + ci + co]
        c_sem = refs[16 + ci + co:]
        x, y, c = _place()
        me = 4 * x + 2 * y + c
        starts, waits = comm.plan(c_in, c_out, c_sem)
        for cp in starts:
            cp.start()
        load = pltpu.make_async_copy(dw_ref, flat, loc.at[0])
        load.start()

        def peer_of(k):
            return x ^ (k >> 2), y ^ ((k >> 1) & 1), c ^ (k & 1)

        pbuf[me] = pack_ref[...]
        psends = [pltpu.make_async_remote_copy(
            src_ref=pack_ref, dst_ref=pbuf.at[me], send_sem=psend.at[k - 1], recv_sem=precv.at[k - 1],
            device_id=peer_of(k), device_id_type=MESH) for k in range(1, 8)]
        for cp in psends:
            cp.start()

        load.wait()
        for s in range(4):
            mine[s] = flat[rows * s:rows * (s + 1), :]
        cps = []
        for k, (tx, ty) in enumerate(_other_chips(x, y)):
            tosend[k] = _bf(mine[2 * tx + ty])
            cps.append(pltpu.make_async_remote_copy(
                src_ref=tosend.at[k], dst_ref=rbuf.at[k], send_sem=send.at[k], recv_sem=recv.at[k],
                device_id=(tx, ty, c), device_id_type=MESH))
            cps[-1].start()

        for k in range(1, 8):
            px, py, pc = peer_of(k)
            pltpu.make_async_remote_copy(
                src_ref=pack_ref, dst_ref=pbuf.at[4 * px + 2 * py + pc], send_sem=psend.at[k - 1], recv_sem=precv.at[k - 1],
                device_id=(x, y, c), device_id_type=MESH).wait_recv()
        for cp in psends:
            cp.wait_send()
        tot = pbuf[0]
        for d in range(1, 8):
            tot = tot + pbuf[d]
        tot_ref[...] = tot

        for cp in cps:
            cp.wait()
        qbuf[...] = mine[2 * x + y] + rbuf[0].astype(F32) + rbuf[1].astype(F32) + rbuf[2].astype(F32)
        swap = pltpu.make_async_remote_copy(src_ref=qbuf, dst_ref=sib, send_sem=send.at[3], recv_sem=recv.at[3],
                                            device_id=(x, y, 1 - c), device_id_type=MESH)
        swap.start()
        swap.wait()
        out_ref[...] = qbuf[...] + sib[...]
        for wait in waits:
            wait()

    outs = pl.pallas_call(
        body, name="reduce_tail",
        in_specs=[ANY_SPEC, VMEM_SPEC] + [ANY_SPEC] * ci, out_specs=[VMEM_SPEC, VMEM_SPEC] + [ANY_SPEC] * co,
        out_shape=[jax.ShapeDtypeStruct((rows, D), F32), jax.ShapeDtypeStruct(pack.shape, F32)] + list(comm.outs),
        scratch_shapes=[pltpu.VMEM((DIN, D), F32), pltpu.VMEM((4, rows, D), F32), pltpu.VMEM((3, rows, D), BF16),
                        pltpu.VMEM((3, rows, D), BF16), pltpu.VMEM((rows, D), F32), pltpu.VMEM((rows, D), F32),
                        pltpu.VMEM((8, p, D), F32), *_dma_sems(4, 4, 1, 7, 7), *comm.sems],
        compiler_params=pltpu.CompilerParams(vmem_limit_bytes=56 << 20),
    )(dwt, pack, *comm.ins)
    return outs[0], outs[1], outs[2:]


GRID4 = 4


def _sum_parts(sel, firsts, others, name, also_bf16):
    n = len(firsts)
    nk = others[0].shape[0]

    def body(sel_ref, *refs):
        fs, os_, outs = refs[:n], refs[n:2 * n], refs[2 * n:]
        for a in range(n):
            acc = fs[a][0]
            for k in range(nk):
                acc = acc + os_[a][k].astype(F32)
            outs[a][...] = acc
            if also_bf16:
                outs[n + a][...] = _bf(acc)

    def rows(a):
        return firsts[a].shape[1] // GRID4

    in_specs = ([pl.BlockSpec((1, rows(a), firsts[a].shape[2]), lambda i, s: (s[0], i, 0)) for a in range(n)]
                + [pl.BlockSpec((nk, rows(a), firsts[a].shape[2]), lambda i, s: (0, i, 0)) for a in range(n)])
    out_specs = [pl.BlockSpec((rows(a), firsts[a].shape[2]), lambda i, s: (i, 0)) for a in range(n)]
    out_shape = [jax.ShapeDtypeStruct(f.shape[1:], F32) for f in firsts]
    if also_bf16:
        out_specs = out_specs * 2
        out_shape = out_shape + [jax.ShapeDtypeStruct(f.shape[1:], BF16) for f in firsts]
    outs = pl.pallas_call(
        body, name=name,
        grid_spec=pltpu.PrefetchScalarGridSpec(num_scalar_prefetch=1, grid=(GRID4,), in_specs=in_specs, out_specs=out_specs),
        out_shape=out_shape, compiler_params=_cp(48),
    )(sel, *firsts, *others)
    return outs[:n], outs[n:]


def _adamw_math(w, g, m, v):
    m2 = ADAM_B1 * m + (1.0 - ADAM_B1) * g
    v2 = ADAM_B2 * v + (1.0 - ADAM_B2) * (g * g)
    m_hat = m2 / (1.0 - ADAM_B1 ** ADAM_STEP)
    v_hat = v2 / (1.0 - ADAM_B2 ** ADAM_STEP)
    return -ADAM_LR * (m_hat / (jnp.sqrt(v_hat) + ADAM_EPS) + ADAM_WD * w), m2, v2


def _adamw_big(ws, gs, ms, vs):
    n = len(ws)

    def body(*refs):
        for a in range(n):
            d, m2, v2 = _adamw_math(refs[a][...], refs[n + a][...], refs[2 * n + a][...], refs[3 * n + a][...])
            refs[4 * n + a][...] = d
            refs[5 * n + a][...] = m2
            refs[6 * n + a][...] = v2

    specs = [pl.BlockSpec((w.shape[0] // GRID4, w.shape[1]), lambda i: (i, 0)) for w in ws]
    return pl.pallas_call(
        body, name="adamw_big", grid=(GRID4,),
        in_specs=specs * 4, out_specs=specs * 3,
        out_shape=[jax.ShapeDtypeStruct(w.shape, F32) for w in ws] * 3,
        compiler_params=_cp(48),
    )(*ws, *gs, *ms, *vs)


def _adamw_small(ws, gs, ms, vs):
    n = len(ws)

    def body(*refs):
        for a in range(n):
            d, m2, v2 = _adamw_math(refs[a][...], refs[n + a][...], refs[2 * n + a][...], refs[3 * n + a][...])
            refs[4 * n + a][...] = d
            refs[5 * n + a][...] = m2
            refs[6 * n + a][...] = v2

    return pl.pallas_call(
        body, name="adamw_small",
        in_specs=[VMEM_SPEC] * (4 * n), out_specs=[VMEM_SPEC] * (3 * n),
        out_shape=[jax.ShapeDtypeStruct(w.shape, F32) for w in ws] * 3,
        compiler_params=pltpu.CompilerParams(vmem_limit_bytes=40 << 20),
    )(*ws, *gs, *ms, *vs)


def kernel(x, meta_tokens, norm_mix_w, w_in, w_gate_up, b_gate, gla_norm_w, sinks, w_out, norm_ff_w, w_ff1, w_ff2, final_norm_w, loss_target, m_meta_tokens, m_norm_mix_w, m_w_in, m_w_gate_up, m_b_gate, m_gla_norm_w, m_sinks, m_w_out, m_norm_ff_w, m_w_ff1, m_w_ff2, m_final_norm_w, v_meta_tokens, v_norm_mix_w, v_w_in, v_w_gate_up, v_b_gate, v_gla_norm_w, v_sinks, v_w_out, v_norm_ff_w, v_w_ff1, v_w_ff2, v_final_norm_w):
    xi, yi, ci = _place()
    shard = (2 * xi + yi).astype(jnp.int32).reshape(1)
    core = ci.astype(jnp.int32).reshape(1)

    small = jnp.concatenate([meta_tokens, w_gate_up[0], jnp.zeros((NM, 64), F32)], axis=1)
    wt3, g_small = _run_comm(_gather_shards([_bf(w_in[0].T), small]), "gather_w_in")
    meta = g_small[:, :, 0:256].transpose(1, 0, 2).reshape(NM, D)
    wgu = g_small[:, :, 256:320].transpose(1, 0, 2).reshape(NM, 256)

    xs, tgt = x[0], loss_target[0]
    t = xs.shape[0]
    wfin = final_norm_w.reshape(1, D)
    metapad = jnp.concatenate([meta, jnp.zeros((TM - NM, D), F32)], axis=0)
    wgu_p = _bf(jnp.concatenate([wgu, jnp.zeros((128 - 16, 256), F32)], axis=0))
    tabs = _rope_tables(t)

    w1s, w2s = _bf(w_ff1[0]), _bf(w_ff2[0])
    (proj, ut), (w1a,) = _proj_fwd(xs, metapad, norm_mix_w, wt3, tabs, _gather_shards([w1s[:HK]]))
    (oswa, lse), (g_out, w1b, w2a) = _swa_fwd(proj, sinks, t, _gather_shards([_bf(w_out[0]), w1s[HK:], w2s[:HK]]))
    (ogla, oraw, sst), (w2b,) = _gla_fwd(proj, wgu_p, b_gate, gla_norm_w, t, _gather_shards([w2s[HK:]]))
    wo, w1, w2 = g_out.reshape(D, D), (w1a, w1b), (w2a, w2b)
    h1, ft, a, at, dh2, loss, gfin = _mlp_fwd(xs, metapad, tgt, ogla, oswa, wo, norm_ff_w, w1, w2, wfin)

    da, dh2b, dh1, do, dwo, gff = _mlp_bwd(h1, a, dh2, ogla, oswa, wo, norm_ff_w, w1, w2)
    dw1, dw2 = _ffn_wgrad(ft, at, da, dh2b)
    big = [dwo.reshape(4, 2, 128, D).transpose(1, 0, 2, 3), dw1, dw2]
    (dsq, dsk, dsv, dsink), theirs = _swa_bwd(proj, sinks, lse, do, t, _swap_halves(big))
    sums, sums_bf = _sum_parts(core, [b.reshape((2, -1) + b.shape[3:]) for b in big],
                               [s.reshape((1, -1) + s.shape[2:]) for s in theirs], "sum_cores", True)
    sums = [s.reshape(b.shape[1:]) for s, b in zip(sums, big)]
    sums_bf = [s.reshape(b.shape[1:]) for s, b in zip(sums_bf, big)]
    (dgla, dlr, dwgu, dbg, dgnw), arrived = _gla_bwd(proj, oraw, sst, do, wgu_p, b_gate, gla_norm_w, t,
                                                     _scatter_shards(sums_bf))
    halves, _ = _sum_parts(shard, sums, arrived, "sum_chips", False)
    (gx, gmeta, dwt, gmix), _ = _proj_bwd(xs, metapad, norm_mix_w, wt3, tabs, ut, dgla, dsq, dsk, dsv, dlr, dh1)

    tail = jnp.concatenate([dbg, dgnw, dsink, loss, jnp.zeros((1, D - 256 - 128 - 8 - 1), F32)], axis=1)
    pack = jnp.concatenate([gmeta, gmix, gff, gfin, tail, dwgu[:16].reshape(4, D)], axis=0)
    gwt_in, tot, joined = _reduce_tail(dwt, pack, _join_halves(halves))
    gw_out, gw_1, gw_2 = [j.reshape((-1, j.shape[2])) for j in joined]
    g_meta = lax.dynamic_slice_in_dim(tot[0:NM], shard[0] * 256, 256, axis=1)
    g_mix, g_ff, g_fin = tot[16:17], tot[17:18], tot[18]
    g_bg, g_gnw, g_sinks, loss_tot = tot[19:20, 0:256], tot[19:20, 256:384], tot[19:20, 384:392], tot[19, 392]
    g_wgu = lax.dynamic_slice_in_dim(tot[20:24].reshape(NM, 256), shard[0] * 64, 64, axis=1)

    bo = _adamw_big([w_out[0], w_ff1[0], w_ff2[0]], [gw_out, gw_1, gw_2], [m_w_out[0], m_w_ff1[0], m_w_ff2[0]],
                    [v_w_out[0], v_w_ff1[0], v_w_ff2[0]])

    fin2 = lambda a: a.reshape(1, D)
    sw = [meta_tokens, norm_mix_w, w_gate_up[0], b_gate, gla_norm_w, sinks, norm_ff_w, fin2(final_norm_w), w_in[0].T]
    sg = [g_meta, g_mix, g_wgu, g_bg, g_gnw, g_sinks, g_ff, fin2(g_fin), gwt_in]
    sm = [m_meta_tokens, m_norm_mix_w, m_w_gate_up[0], m_b_gate, m_gla_norm_w, m_sinks, m_norm_ff_w, fin2(m_final_norm_w),
          m_w_in[0].T]
    sv = [v_meta_tokens, v_norm_mix_w, v_w_gate_up[0], v_b_gate, v_gla_norm_w, v_sinks, v_norm_ff_w, fin2(v_final_norm_w),
          v_w_in[0].T]
    so = _adamw_small(sw, sg, sm, sv)

    def ordered(small_o, big_o):
        meta_, mix_, wgu_, bg_, gnw_, sinks_, ff_, fin_, wt_ = small_o
        w_out_, w_1_, w_2_ = big_o
        return (meta_, mix_, wt_.T[None], wgu_[None], bg_, gnw_, sinks_, w_out_[None], ff_, w_1_[None], w_2_[None],
                fin_.reshape(D))

    grads = ordered(sg, [gw_out, gw_1, gw_2])
    deltas = ordered(so[0:9], bo[0:3])
    new_m = ordered(so[9:18], bo[3:6])
    new_v = ordered(so[18:27], bo[6:9])
    return (loss_tot, gx[None], *grads, *deltas, *new_m, *new_v)
```

```python
from typing import Callable, NamedTuple

import jax
import jax.numpy as jnp
import numpy as np
from jax import lax
from jax.experimental import pallas as pl
from jax.experimental.pallas import tpu as pltpu

F32 = jnp.float32
BF16 = jnp.bfloat16

D = 1024
DFF = 4096
NM = 16
TM = 256
CH = 64
SB = 128
QB = 2 * SB
EPS = 1e-5
C_GQ, C_GK, C_GV, C_GR, C_SQ, C_SK, C_SV, C_LR, DINP = 0, 256, 512, 1024, 1536, 2048, 2176, 2304, 2432
DIN = 2320
R_LR = 1536
ROPE_THETA = 500000.0
ADAM_LR, ADAM_B1, ADAM_B2, ADAM_EPS, ADAM_WD, ADAM_STEP = 0.001, 0.9, 0.999, 1e-08, 0.01, 10
NEG = -1e30
MESH = pl.DeviceIdType.MESH
VMEM_SPEC = pl.BlockSpec(memory_space=pltpu.VMEM)
ANY_SPEC = pl.BlockSpec(memory_space=pl.ANY)
SMEM_SPEC = pl.BlockSpec(memory_space=pltpu.SMEM)


def _cp(vmem_mb, sem=("arbitrary",)):
    return pltpu.CompilerParams(dimension_semantics=sem, vmem_limit_bytes=vmem_mb << 20)


def _dot(a, b):
    return jnp.dot(a, b, preferred_element_type=F32)


def _dot_nt(a, b):
    return lax.dot_general(a, b, (((1,), (1,)), ((), ())), preferred_element_type=F32)


def _dot_tn(a, b):
    return lax.dot_general(a, b, (((0,), (0,)), ((), ())), preferred_element_type=F32)


def _bf(x):
    return x.astype(BF16)


def _dot3(m01, x):
    x1 = _bf(x)
    r1 = x - x1.astype(F32)
    x2 = _bf(r1)
    x3 = _bf(r1 - x2.astype(F32))
    return _dot(m01, x1) + _dot(m01, x2) + _dot(m01, x3)


def _rms(h):
    rs = lax.rsqrt(jnp.mean(h * h, axis=-1, keepdims=True) + EPS)
    return h * rs, rs


def _rms_bwd(dy, yhat, rs, w):
    dyh = dy * w
    return rs * (dyh - yhat * jnp.mean(dyh * yhat, axis=-1, keepdims=True))


class _Comm(NamedTuple):
    ins: tuple
    outs: tuple
    sems: tuple
    phases: int
    plan: Callable


def _run_phase(fns):
    for fn in fns:
        fn()


def _call(body, name, grid, in_specs, out_specs, out_shape, scratch, params, args, comm=None):
    if comm is None:
        outs = pl.pallas_call(body, name=name, grid=grid, in_specs=in_specs, out_specs=out_specs, out_shape=out_shape,
                              scratch_shapes=scratch, compiler_params=params)(*args)
        return outs, None
    n_in, n_out, n_scr = len(in_specs), len(out_specs), len(scratch)
    ci, co = len(comm.ins), len(comm.outs)
    last = grid[0] - 1
    marks = [0, max(1, last - max(2, (last + 1) // 6))][:comm.phases]

    def wrapped(*refs):
        own_in, c_in = refs[:n_in], refs[n_in:n_in + ci]
        refs = refs[n_in + ci:]
        own_out, c_out = refs[:n_out], refs[n_out:n_out + co]
        refs = refs[n_out + co:]
        own_scr, c_sem = refs[:n_scr], refs[n_scr:]
        i = pl.program_id(0)

        for p, mark in enumerate(marks):
            @pl.when(i == mark)
            def _():
                plan = comm.plan(c_in, c_out, c_sem)
                if p > 0:
                    _run_phase(plan[p - 1][1])
                _run_phase(plan[p][0])

        body(*own_in, *own_out, *own_scr)

        @pl.when(i == last)
        def _():
            _run_phase(comm.plan(c_in, c_out, c_sem)[-1][1])

    outs = pl.pallas_call(
        wrapped, name=name, grid=grid, in_specs=list(in_specs) + [ANY_SPEC] * ci, out_specs=list(out_specs) + [ANY_SPEC] * co,
        out_shape=list(out_shape) + list(comm.outs), scratch_shapes=list(scratch) + list(comm.sems), compiler_params=params,
    )(*args, *comm.ins)
    return outs[:n_out], outs[n_out:]


def _run_comm(comm, name):
    ci, co = len(comm.ins), len(comm.outs)

    def body(*refs):
        for starts, waits in comm.plan(refs[:ci], refs[ci:ci + co], refs[ci + co:]):
            _run_phase(starts)
            _run_phase(waits)

    return pl.pallas_call(body, name=name, in_specs=[ANY_SPEC] * ci, out_specs=[ANY_SPEC] * co, out_shape=list(comm.outs),
                          scratch_shapes=list(comm.sems))(*comm.ins)


def _join_shards(w3_ref, w_ref):
    for s in range(4):
        w_ref[(DIN // 4) * s:(DIN // 4) * (s + 1), :] = w3_ref[s]


def _proj_fwd(x, metapad, wm, wt3, tabs, comm=None):
    t = x.shape[0]
    nblk = t // TM

    def body(x_ref, mp_ref, wm_ref, w3_ref, tab_ref, proj_ref, w_ref):
        i = pl.program_id(0)

        @pl.when(i == 0)
        def _():
            _join_shards(w3_ref, w_ref)

        h = jnp.where(i == nblk, mp_ref[...], x_ref[...])
        u, _ = _rms(h)
        ub = _bf(u * wm_ref[...])
        proj_ref[:, 0:C_SQ] = _dot_nt(ub, w_ref[0:R_LR, :])
        att = _dot_nt(ub, w_ref[R_LR + 16:DIN, :])
        tab = tab_ref[...]
        proj_ref[:, C_SQ:C_SK] = _rope(att[:, 0:512], tab, 1.0) * 0.125
        proj_ref[:, C_SK:C_SV] = _rope(att[:, 512:640], tab, 1.0)
        proj_ref[:, C_SV:C_LR] = att[:, 640:768]
        proj_ref[:, C_LR:DINP] = jnp.zeros((TM, DINP - C_LR), F32)
        proj_ref[:, C_LR:C_LR + 16] = _dot_nt(ub, w_ref[R_LR:R_LR + 16, :])

    (proj,), got = _call(
        body, "proj_fwd", (nblk + 1,),
        [pl.BlockSpec((TM, D), lambda i: (jnp.minimum(i, nblk - 1), 0)), VMEM_SPEC, VMEM_SPEC, VMEM_SPEC,
         pl.BlockSpec((TM, 128), lambda i: (i, 0))],
        [pl.BlockSpec((TM, DINP), lambda i: (i, 0))], [jax.ShapeDtypeStruct((t + TM, DINP), F32)],
        [pltpu.VMEM((DIN, D), BF16)], _cp(48), (x, metapad, wm, wt3, tabs), comm)
    return proj, got


def _chunk_masks():
    r = lax.broadcasted_iota(jnp.int32, (TM, TM), 0)
    c = lax.broadcasted_iota(jnp.int32, (TM, TM), 1)
    same = (r // CH) == (c // CH)
    lower = _bf(jnp.where(same & (c <= r), 1.0, 0.0))
    upper = _bf(jnp.where(same & (c >= r), 1.0, 0.0))
    return lower, upper


def _gla_gate(lr, wgu, bg, valid, lower):
    z = _dot(_bf(lr), wgu) + bg
    g = (jnp.minimum(z, 0.0) - jnp.log(1.0 + jnp.exp(-jnp.abs(z)))) * (1.0 / 16.0)
    g = jnp.where(valid, g, 0.0)
    return z, _dot3(lower, g)


def _gla_decays(q, k, b):
    nc = TM // CH
    b3 = b.reshape(nc, CH, 256)
    blast = b3[:, CH - 1:CH, :]
    eb = jnp.exp(b)
    enb = jnp.exp(-b)
    ebl = jnp.exp(blast - b3).reshape(TM, 256)
    return eb, enb, ebl, jnp.exp(blast)


def _tri(lower_incl):
    r = lax.broadcasted_iota(jnp.int32, (CH, CH), 0)
    c = lax.broadcasted_iota(jnp.int32, (CH, CH), 1)
    return ((c <= r) if lower_incl else (c >= r))[None]


def _gla_fwd(proj, wgu, bg, gnw, t, comm=None):
    nblk = t // TM
    nt = nblk + 1
    nc = TM // CH

    def blk(i):
        return (i + nblk) % nt

    def body(q_ref, k_ref, v_ref, r_ref, lr_ref, wgu_ref, bg_ref, gnw_ref, o_ref, oraw_ref, sst_ref, st_scr):
        i = pl.program_id(0)

        @pl.when(i == 0)
        def _():
            st_scr[...] = jnp.zeros_like(st_scr)

        rows = blk(i) * TM + lax.broadcasted_iota(jnp.int32, (TM, 1), 0)
        lower, _ = _chunk_masks()
        _, b = _gla_gate(lr_ref[...], wgu_ref[...], bg_ref[...], rows < t + NM, lower)
        q = q_ref[...]
        k = k_ref[...]
        eb, enb, ebl, eblast = _gla_decays(q, k, b)
        qt = q * 0.125 * eb
        kt = k * enb
        kh = k * ebl
        tril = _tri(True)
        outs = []
        for h in range(4):
            hs = slice(h * CH, (h + 1) * CH)
            qh = _bf(qt[:, hs]).reshape(nc, CH, CH)
            kth = _bf(kt[:, hs]).reshape(nc, CH, CH)
            khh = _bf(kh[:, hs]).reshape(nc, CH, CH)
            vh = _bf(v_ref[:, h * 128:(h + 1) * 128]).reshape(nc, CH, 128)
            a = jnp.einsum('cid,cjd->cij', qh, kth, preferred_element_type=F32)
            a = jnp.where(tril, a, 0.0)
            o = jnp.einsum('cij,cjv->civ', _bf(a), vh, preferred_element_type=F32)
            kv = jnp.einsum('cjv,cjd->cvd', vh, khh, preferred_element_type=F32)
            st = st_scr[h]
            o_inter = []
            for c in range(nc):
                sst_ref[c, h] = st
                o_inter.append(_dot_nt(qh[c], _bf(st)))
                st = st * eblast[c, :, hs] + kv[c]
            st_scr[h] = st
            outs.append((o + jnp.stack(o_inter)).reshape(TM, 128))
        oraw = jnp.concatenate(outs, axis=1)
        oraw_ref[...] = oraw
        gn = gnw_ref[...]
        res = []
        for h in range(4):
            on, _ = _rms(oraw[:, h * 128:(h + 1) * 128])
            r = r_ref[:, h * 128:(h + 1) * 128]
            res.append(on * gn * (r * jax.nn.sigmoid(r)))
        o_ref[...] = _bf(jnp.concatenate(res, axis=1))

    def spec(w, cb):
        return pl.BlockSpec((TM, w), lambda i: (blk(i), cb))

    return _call(
        body, "gla_fwd", (nt,),
        [spec(256, 0), spec(256, 1), spec(512, 1), spec(512, 2), spec(128, C_LR // 128), VMEM_SPEC, VMEM_SPEC, VMEM_SPEC],
        [spec(512, 0), spec(512, 0), pl.BlockSpec((nc, 4, 128, CH), lambda i: (blk(i), 0, 0, 0))],
        [jax.ShapeDtypeStruct((t + TM, 512), BF16), jax.ShapeDtypeStruct((t + TM, 512), F32),
         jax.ShapeDtypeStruct((nt * nc, 4, 128, CH), F32)],
        [pltpu.VMEM((4, 128, CH), F32)], _cp(40), (proj, proj, proj, proj, proj, wgu, bg, gnw), comm)


def _gla_bwd(proj, oraw, sst, do, wgu, bg, gnw, t, comm=None):
    nblk = t // TM
    nt = nblk + 1
    nc = TM // CH

    def blk(i):
        return (2 * nblk - i) % nt

    def body(q_ref, k_ref, v_ref, r_ref, lr_ref, oraw_ref, sst_ref, do_ref, wgu_ref, bg_ref, gnw_ref,
             dgla_ref, dlr_ref, dwgu_ref, dbg_ref, dgnw_ref, dst_scr):
        i = pl.program_id(0)

        @pl.when(i == 0)
        def _():
            dst_scr[...] = jnp.zeros_like(dst_scr)
            dwgu_ref[...] = jnp.zeros_like(dwgu_ref)
            dbg_ref[...] = jnp.zeros_like(dbg_ref)
            dgnw_ref[...] = jnp.zeros_like(dgnw_ref)

        rows = blk(i) * TM + lax.broadcasted_iota(jnp.int32, (TM, 1), 0)
        valid = rows < t + NM
        lower, upper = _chunk_masks()
        lr = lr_ref[...]
        z, b = _gla_gate(lr, wgu_ref[...], bg_ref[...], valid, lower)
        q = q_ref[...]
        k = k_ref[...]
        eb, enb, ebl, eblast = _gla_decays(q, k, b)
        qt = q * 0.125 * eb
        kt = k * enb
        kh = k * ebl
        gn = gnw_ref[...]
        tril = _tri(True)
        triu = _tri(False)
        dq_l, dk_l, dv_l, dr_l, db_l, ex_l = [], [], [], [], [], []
        dgn = jnp.zeros((1, 128), F32)
        for h in range(4):
            hs = slice(h * CH, (h + 1) * CH)
            vs = slice(h * 128, (h + 1) * 128)
            on, rs = _rms(oraw_ref[:, vs])
            r = r_ref[:, vs]
            sig = jax.nn.sigmoid(r)
            sil = r * sig
            dy = do_ref[:, vs]
            dr_l.append(dy * on * gn * (sig * (1.0 + r * (1.0 - sig))))
            dgn = dgn + jnp.sum(dy * sil * on, axis=0, keepdims=True)
            doraw = _rms_bwd(dy * sil, on, rs, gn)
            qtf = qt[:, hs].reshape(nc, CH, CH)
            ktf = kt[:, hs].reshape(nc, CH, CH)
            khf = kh[:, hs].reshape(nc, CH, CH)
            qh, kth, khh = _bf(qtf), _bf(ktf), _bf(khf)
            vh = _bf(v_ref[:, vs]).reshape(nc, CH, 128)
            doh = _bf(doraw).reshape(nc, CH, 128)
            at = jnp.where(triu, jnp.einsum('cjd,cid->cji', kth, qh, preferred_element_type=F32), 0.0)
            da = jnp.where(tril, jnp.einsum('civ,cjv->cij', doh, vh, preferred_element_type=F32), 0.0)
            dat = jnp.where(triu, jnp.einsum('cjv,civ->cji', vh, doh, preferred_element_type=F32), 0.0)
            dv = jnp.einsum('cji,civ->cjv', _bf(at), doh, preferred_element_type=F32)
            dqt = jnp.einsum('cij,cjd->cid', _bf(da), kth, preferred_element_type=F32)
            dkt = jnp.einsum('cji,cid->cjd', _bf(dat), qh, preferred_element_type=F32)
            gq = jnp.einsum('civ,cid->cvd', doh, qh, preferred_element_type=F32)
            dst = dst_scr[h]
            dsend = [None] * nc
            for c in reversed(range(nc)):
                dsend[c] = dst
                dst = dst * eblast[c, :, hs] + gq[c]
            dst_scr[h] = dst
            dse = jnp.stack(dsend)
            dseb = _bf(dse)
            stf = sst_ref[:, h]
            dqt = dqt + jnp.einsum('civ,cvd->cid', doh, _bf(stf), preferred_element_type=F32)
            dv = dv + jnp.einsum('cjd,cvd->cjv', khh, dseb, preferred_element_type=F32)
            dkh = jnp.einsum('cjv,cvd->cjd', vh, dseb, preferred_element_type=F32)
            extra = (jnp.sum(dkh * khf, axis=1, keepdims=True)
                     + eblast[:, :, hs] * jnp.sum(dse * stf, axis=1, keepdims=True))
            db_l.append((dqt * qtf - dkt * ktf - dkh * khf).reshape(TM, CH))
            ex_l.append(jnp.broadcast_to(extra, (nc, CH, CH)).reshape(TM, CH))
            dq_l.append((dqt.reshape(TM, CH)) * eb[:, hs] * 0.125)
            dk_l.append(dkt.reshape(TM, CH) * enb[:, hs] + dkh.reshape(TM, CH) * ebl[:, hs])
            dv_l.append(dv.reshape(TM, 128))
        dgnw_ref[...] += dgn
        db = jnp.concatenate(db_l, axis=1)
        dg = _dot3(upper, db) + jnp.concatenate(ex_l, axis=1)
        dz = jnp.where(valid, dg * (1.0 / 16.0) / (1.0 + jnp.exp(z)), 0.0)
        dzb = _bf(dz)
        dlr_ref[...] = _bf(_dot_nt(dzb, wgu_ref[...]))
        dwgu_ref[...] += _dot_tn(_bf(lr), dzb)
        dbg_ref[...] += jnp.sum(dz, axis=0, keepdims=True)
        dgla_ref[...] = _bf(jnp.concatenate(dq_l + dk_l + dv_l + dr_l, axis=1))

    def spec(w, cb):
        return pl.BlockSpec((TM, w), lambda i: (blk(i), cb))

    def acc(shape):
        return pl.BlockSpec(shape, lambda i: (0, 0))

    return _call(
        body, "gla_bwd", (nt,),
        [spec(256, 0), spec(256, 1), spec(512, 1), spec(512, 2), spec(128, C_LR // 128), spec(512, 0),
         pl.BlockSpec((nc, 4, 128, CH), lambda i: (blk(i), 0, 0, 0)), spec(512, 0), VMEM_SPEC, VMEM_SPEC, VMEM_SPEC],
        [spec(1536, 0), spec(128, 0), acc((128, 256)), acc((1, 256)), acc((1, 128))],
        [jax.ShapeDtypeStruct((t + TM, 1536), BF16), jax.ShapeDtypeStruct((t + TM, 128), BF16),
         jax.ShapeDtypeStruct((128, 256), F32), jax.ShapeDtypeStruct((1, 256), F32), jax.ShapeDtypeStruct((1, 128), F32)],
        [pltpu.VMEM((4, 128, CH), F32)], _cp(48), (proj, proj, proj, proj, proj, oraw, sst, do, wgu, bg, gnw), comm)


def _rope_tables(t):
    r = t + TM
    row = np.arange(r)
    pos = np.where(row < t, row + NM, np.where(row < t + NM, row - t, 0)).astype(np.float32)
    inv_freq = (1.0 / (np.float32(ROPE_THETA) ** (np.arange(0, 16, 2, dtype=np.float32) / np.float32(16)))).astype(np.float32)
    ang = (pos[:, None] * inv_freq[None, :]).astype(np.float32)
    cos, sin = np.cos(ang).astype(np.float32), np.sin(ang).astype(np.float32)
    one, zero = np.ones((r, 48), np.float32), np.zeros((r, 48), np.float32)
    return jnp.asarray(np.concatenate([cos, cos, one, -sin, sin, zero], axis=1))


def _rope(x, tab, sign):
    w = x.shape[1]
    rep = w // 64
    c = jnp.concatenate([tab[:, 0:64]] * rep, axis=1)
    s = jnp.concatenate([tab[:, 64:128]] * rep, axis=1)
    lane = lax.rem(lax.broadcasted_iota(jnp.int32, x.shape, 1), 64)
    partner = jnp.where(lane < 8, pltpu.roll(x, w - 8, 1), jnp.where(lane < 16, pltpu.roll(x, 8, 1), 0.0))
    return x * c + sign * (partner * s)


HG_FWD, HG_BWD = 1, 4


def _stack(x, hg):
    w = x.shape[1] // hg
    return x if hg == 1 else jnp.concatenate([x[:, g * w:(g + 1) * w] for g in range(hg)], axis=0)


def _unstack(x, hg):
    return x if hg == 1 else jnp.concatenate([x[g * SB:(g + 1) * SB] for g in range(hg)], axis=1)


def _swa_masks(b, nsb, hg):
    r = lax.rem(lax.broadcasted_iota(jnp.int32, (hg * SB, SB), 0), SB)
    c = lax.broadcasted_iota(jnp.int32, (hg * SB, SB), 1)
    real = b < nsb
    return c <= r, (c > r) & (b > 0) & real, (c < NM) & real


def _swa_specs(nsb):
    def rows(h, w, cb, f):
        return pl.BlockSpec((h, w), lambda i: (f(i), cb))
    pair = lambda i: i
    prev = lambda i: jnp.maximum(2 * i - 1, 0)
    meta = lambda i: nsb
    return rows, pair, prev, meta


def _swa_scores(b, nsb, hg, sink_ref, q, kc, kp, km):
    mc, mp, mm = _swa_masks(b, nsb, hg)
    groups = []
    for kv in range(2):
        ks = slice(kv * 64, (kv + 1) * 64)
        kcb, kpb, kmb = _bf(kc[:, ks]), _bf(kp[:, ks]), _bf(km[:, ks])
        for h0 in range(4 * kv, 4 * kv + 4, hg):
            qg = _bf(_stack(q[:, h0 * 64:(h0 + hg) * 64], hg))
            s_c = jnp.where(mc, _dot_nt(qg, kcb), NEG)
            s_p = jnp.where(mp, _dot_nt(qg, kpb), NEG)
            s_m = jnp.where(mm, _dot_nt(qg, kmb), NEG)
            sink = jnp.concatenate([jnp.full((SB, 1), sink_ref[0, h0 + g], F32) for g in range(hg)], axis=0)
            groups.append((kv, h0, qg, kcb, kpb, kmb, s_c, s_p, s_m, sink))
    return groups


def _swa_fwd(proj, sinks, t, comm=None):
    nsb = t // SB
    r_tot = t + TM
    rows, pair, prev, meta = _swa_specs(nsb)

    def body(sink_ref, q_ref, kc_ref, kp_ref, km_ref, vc_ref, vp_ref, vm_ref, o_ref, lse_ref):
        i = pl.program_id(0)
        km, vm = km_ref[...], vm_ref[...]
        for j in range(2):
            b = 2 * i + j
            rs = slice(j * SB, (j + 1) * SB)
            kp = kp_ref[...] if j == 0 else kc_ref[0:SB, :]
            vp = vp_ref[...] if j == 0 else vc_ref[0:SB, :]
            vc = vc_ref[rs, :]
            o_l, lse_l = [], []
            for kv, h0, qg, kcb, kpb, kmb, s_c, s_p, s_m, sink in _swa_scores(
                    b, nsb, HG_FWD, sink_ref, q_ref[rs, :], kc_ref[rs, :], kp, km):
                ks = slice(kv * 64, (kv + 1) * 64)
                m = jnp.maximum(jnp.max(jnp.maximum(jnp.maximum(s_c, s_p), s_m), -1, keepdims=True), sink)
                p_c, p_p, p_m = jnp.exp(s_c - m), jnp.exp(s_p - m), jnp.exp(s_m - m)
                l = jnp.sum(p_c + p_p + p_m, -1, keepdims=True) + jnp.exp(sink - m)
                o = _dot(_bf(p_c), _bf(vc[:, ks])) + _dot(_bf(p_p), _bf(vp[:, ks])) + _dot(_bf(p_m), _bf(vm[:, ks]))
                o_l.append(_unstack(o * (1.0 / l), HG_FWD))
                lse_l.append(_unstack(m + jnp.log(l), HG_FWD))
            valid = b * SB + lax.broadcasted_iota(jnp.int32, (SB, 1), 0) < t + NM
            o_ref[rs, :] = _bf(jnp.where(valid, jnp.concatenate(o_l, axis=1), 0.0))
            lse_ref[rs, :] = jnp.concatenate(lse_l, axis=1)

    ck, cv = C_SK // 128, C_SV // 128
    return _call(
        body, "swa_fwd", (r_tot // QB,),
        [SMEM_SPEC, rows(QB, 512, C_SQ // 512, pair),
         rows(QB, 128, ck, pair), rows(SB, 128, ck, prev), rows(SB, 128, ck, meta),
         rows(QB, 128, cv, pair), rows(SB, 128, cv, prev), rows(SB, 128, cv, meta)],
        [rows(QB, 512, 0, pair), rows(QB, 8, 0, pair)],
        [jax.ShapeDtypeStruct((r_tot, 512), BF16), jax.ShapeDtypeStruct((r_tot, 8), F32)],
        [], _cp(32), (sinks, proj, proj, proj, proj, proj, proj, proj), comm)


def _swa_bwd(proj, sinks, lse, do, t, comm=None):
    nsb = t // SB
    r_tot = t + TM
    rows, pair, prev, meta = _swa_specs(nsb)

    def body(sink_ref, q_ref, kc_ref, kp_ref, km_ref, vc_ref, vp_ref, vm_ref, lse_ref, do_ref,
             dq_ref, dk_ref, dv_ref, dsink_ref):
        i = pl.program_id(0)

        @pl.when(i == 0)
        def _():
            dk_ref[...] = jnp.zeros_like(dk_ref)
            dv_ref[...] = jnp.zeros_like(dv_ref)
            dsink_ref[...] = jnp.zeros_like(dsink_ref)

        km, vm = km_ref[...], vm_ref[...]
        dsink = jnp.zeros((1, 8), F32)
        for j in range(2):
            b = 2 * i + j
            rs = slice(j * SB, (j + 1) * SB)
            kp = kp_ref[...] if j == 0 else kc_ref[0:SB, :]
            vp = vp_ref[...] if j == 0 else vc_ref[0:SB, :]
            vc = vc_ref[rs, :]
            lse_all = lse_ref[rs, :]
            dq_l, ds_l = [], []
            zero = jnp.zeros((SB, 64), F32)
            acc = {name: [zero, zero] for name in ("kc", "kp", "km", "vc", "vp", "vm")}
            hg = HG_BWD
            for kv, h0, qg, kcb, kpb, kmb, s_c, s_p, s_m, sink in _swa_scores(
                    b, nsb, hg, sink_ref, q_ref[rs, :], kc_ref[rs, :], kp, km):
                ks = slice(kv * 64, (kv + 1) * 64)
                lse_g = _stack(lse_all[:, h0:h0 + hg], hg)
                dog = _bf(_stack(do_ref[rs, h0 * 64:(h0 + hg) * 64], hg))
                vcb, vpb, vmb = _bf(vc[:, ks]), _bf(vp[:, ks]), _bf(vm[:, ks])
                p_c, p_p, p_m = jnp.exp(s_c - lse_g), jnp.exp(s_p - lse_g), jnp.exp(s_m - lse_g)
                dp_c, dp_p, dp_m = _dot_nt(dog, vcb), _dot_nt(dog, vpb), _dot_nt(dog, vmb)
                delta = jnp.sum(p_c * dp_c + p_p * dp_p + p_m * dp_m, -1, keepdims=True)
                ds_c, ds_p, ds_m = _bf(p_c * (dp_c - delta)), _bf(p_p * (dp_p - delta)), _bf(p_m * (dp_m - delta))
                dq_l.append(_unstack(_dot(ds_c, kcb) + _dot(ds_p, kpb) + _dot(ds_m, kmb), hg))
                for name, lhs, rhs in (("kc", ds_c, qg), ("kp", ds_p, qg), ("km", ds_m, qg),
                                       ("vc", _bf(p_c), dog), ("vp", _bf(p_p), dog), ("vm", _bf(p_m), dog)):
                    acc[name][kv] = acc[name][kv] + _dot_tn(lhs, rhs)
                ds_l.append(_unstack(-jnp.exp(sink - lse_g) * delta, hg))
            dq_ref[rs, :] = jnp.concatenate(dq_l, axis=1)
            c0 = pl.multiple_of(b * SB, SB)
            p0 = pl.multiple_of(jnp.maximum(b - 1, 0) * SB, SB)
            both = lambda name: jnp.concatenate(acc[name], axis=1)
            dk_ref[pl.ds(c0, SB), :] += both("kc")
            dk_ref[pl.ds(p0, SB), :] += both("kp")
            dk_ref[pl.ds(t, SB), :] += both("km")
            dv_ref[pl.ds(c0, SB), :] += both("vc")
            dv_ref[pl.ds(p0, SB), :] += both("vp")
            dv_ref[pl.ds(t, SB), :] += both("vm")
            dsink = dsink + jnp.sum(jnp.concatenate(ds_l, axis=1), axis=0, keepdims=True)
        dsink_ref[...] += dsink

    ck, cv = C_SK // 128, C_SV // 128
    whole = lambda w: pl.BlockSpec((r_tot, w), lambda i: (0, 0))
    return _call(
        body, "swa_bwd", (r_tot // QB,),
        [SMEM_SPEC, rows(QB, 512, C_SQ // 512, pair),
         rows(QB, 128, ck, pair), rows(SB, 128, ck, prev), rows(SB, 128, ck, meta),
         rows(QB, 128, cv, pair), rows(SB, 128, cv, prev), rows(SB, 128, cv, meta),
         rows(QB, 8, 0, pair), rows(QB, 512, 1, pair)],
        [rows(QB, 512, 0, pair), whole(128), whole(128), pl.BlockSpec((1, 8), lambda i: (0, 0))],
        [jax.ShapeDtypeStruct((r_tot, 512), F32), jax.ShapeDtypeStruct((r_tot, 128), F32),
         jax.ShapeDtypeStruct((r_tot, 128), F32), jax.ShapeDtypeStruct((1, 8), F32)],
        [], _cp(48), (sinks, proj, proj, proj, proj, proj, proj, proj, lse, do), comm)


HK = D // 2


def _mlp_fwd(x, metapad, tgt, ogla, oswa, wo, wff, w1, w2, wfin):
    t = x.shape[0]
    nblk = t // TM

    def body(x_ref, mp_ref, tgt_ref, og_ref, os_ref, wo_ref, wff_ref, w1a_ref, w1b_ref, w2a_ref, w2b_ref, wfin_ref,
             h1_ref, f_ref, a_ref, dh2_ref, loss_ref, gfin_ref):
        i = pl.program_id(0)

        @pl.when(i == 0)
        def _():
            loss_ref[...] = jnp.zeros_like(loss_ref)
            gfin_ref[...] = jnp.zeros_like(gfin_ref)

        h0 = jnp.where(i == nblk, mp_ref[...], x_ref[...])
        h1 = h0 + _dot(og_ref[...], wo_ref[0:512, :]) + _dot(os_ref[...], wo_ref[512:1024, :])
        h1_ref[...] = h1
        fh, _ = _rms(h1)
        f = _bf(fh * wff_ref[...])
        f_ref[...] = f
        acc = jnp.zeros((TM, D), F32)
        for n in range(4):
            a = _dot(f[:, 0:HK], w1a_ref[n]) + _dot(f[:, HK:D], w1b_ref[n])
            a_ref[:, n * D:(n + 1) * D] = _bf(a)
            zr = jnp.maximum(a, 0.0)
            z = _bf(zr * zr)
            acc = acc + _dot(z[:, 0:HK], w2a_ref[n]) + _dot(z[:, HK:D], w2b_ref[n])
        h2 = h1 + acc
        yh, rs2 = _rms(h2)
        wf = wfin_ref[...]
        real = i < nblk
        e = jnp.where(real, yh * wf - tgt_ref[...], 0.0)
        loss_ref[...] += jnp.sum(jnp.sum(e * e, axis=0, keepdims=True), axis=1, keepdims=True) * (0.5 / D)
        dy = e * (1.0 / D)
        gfin_ref[...] += jnp.sum(dy * yh, axis=0, keepdims=True)
        dh2_ref[...] = _rms_bwd(dy, yh, rs2, wf)

    xs = pl.BlockSpec((TM, D), lambda i: (jnp.minimum(i, nblk - 1), 0))
    rs = lambda w: pl.BlockSpec((TM, w), lambda i: (i, 0))
    r_tot = t + TM
    return pl.pallas_call(
        body, name="mlp_fwd", grid=(nblk + 1,),
        in_specs=[xs, VMEM_SPEC, xs, rs(512), rs(512)] + [VMEM_SPEC] * 7,
        out_specs=[rs(D), rs(D), rs(DFF), rs(D), pl.BlockSpec((1, 1), lambda i: (0, 0)), pl.BlockSpec((1, D), lambda i: (0, 0))],
        out_shape=[jax.ShapeDtypeStruct((r_tot, D), F32), jax.ShapeDtypeStruct((r_tot, D), BF16),
                   jax.ShapeDtypeStruct((r_tot, DFF), BF16), jax.ShapeDtypeStruct((r_tot, D), F32),
                   jax.ShapeDtypeStruct((1, 1), F32), jax.ShapeDtypeStruct((1, D), F32)],
        compiler_params=_cp(56),
    )(x, metapad, tgt, ogla, oswa, wo, wff, *w1, *w2, wfin)


def _mlp_bwd(h1, a, dh2, ogla, oswa, wo, wff, w1, w2):
    r_tot = h1.shape[0]
    nt = r_tot // TM

    def body(h1_ref, a_ref, dh2_ref, og_ref, os_ref, wo_ref, wff_ref, w1a_ref, w1b_ref, w2a_ref, w2b_ref,
             da_ref, dh2b_ref, dh1_ref, do_ref, dwo_ref, gff_ref):
        i = pl.program_id(0)

        @pl.when(i == 0)
        def _():
            dwo_ref[...] = jnp.zeros_like(dwo_ref)
            gff_ref[...] = jnp.zeros_like(gff_ref)

        dh2 = dh2_ref[...]
        dh2b = _bf(dh2)
        dh2b_ref[...] = dh2b
        dfa = jnp.zeros((TM, HK), F32)
        dfb = jnp.zeros((TM, HK), F32)
        for n in range(4):
            dz = jnp.concatenate([_dot_nt(dh2b, w2a_ref[n]), _dot_nt(dh2b, w2b_ref[n])], axis=1)
            da = _bf(dz * (2.0 * jnp.maximum(a_ref[:, n * D:(n + 1) * D].astype(F32), 0.0)))
            da_ref[:, n * D:(n + 1) * D] = da
            dfa = dfa + _dot_nt(da, w1a_ref[n])
            dfb = dfb + _dot_nt(da, w1b_ref[n])
        df = jnp.concatenate([dfa, dfb], axis=1)
        fh, rs1 = _rms(h1_ref[...])
        gff_ref[...] += jnp.sum(df * fh, axis=0, keepdims=True)
        dh1 = dh2 + _rms_bwd(df, fh, rs1, wff_ref[...])
        dh1_ref[...] = dh1
        dh1b = _bf(dh1)
        do_ref[...] = _dot_nt(dh1b, wo_ref[...])
        dwo_ref[0:512, :] += _dot_tn(og_ref[...], dh1b)
        dwo_ref[512:1024, :] += _dot_tn(os_ref[...], dh1b)

    rs = lambda w: pl.BlockSpec((TM, w), lambda i: (i, 0))
    return pl.pallas_call(
        body, name="mlp_bwd", grid=(nt,),
        in_specs=[rs(D), rs(DFF), rs(D), rs(512), rs(512)] + [VMEM_SPEC] * 6,
        out_specs=[rs(DFF), rs(D), rs(D), rs(D), pl.BlockSpec((D, D), lambda i: (0, 0)),
                   pl.BlockSpec((1, D), lambda i: (0, 0))],
        out_shape=[jax.ShapeDtypeStruct((r_tot, DFF), BF16), jax.ShapeDtypeStruct((r_tot, D), BF16),
                   jax.ShapeDtypeStruct((r_tot, D), F32), jax.ShapeDtypeStruct((r_tot, D), F32),
                   jax.ShapeDtypeStruct((D, D), F32), jax.ShapeDtypeStruct((1, D), F32)],
        compiler_params=_cp(56),
    )(h1, a, dh2, ogla, oswa, wo, wff, *w1, *w2)


def _ffn_wgrad(f, a, da, dh2b):
    r_tot = f.shape[0]
    kt = 768 if r_tot % 768 == 0 else TM
    nk = r_tot // kt

    def body(f_ref, a_ref, da_ref, dh2_ref, dw1_ref, dw2_ref, acc1, acc2):
        k = pl.program_id(1)

        @pl.when(k == 0)
        def _():
            acc1[...] = jnp.zeros_like(acc1)
            acc2[...] = jnp.zeros_like(acc2)

        zr = jnp.maximum(a_ref[...], 0.0)
        acc1[...] += _dot_tn(f_ref[...], da_ref[...])
        acc2[...] += _dot_tn(zr * zr, dh2_ref[...])

        @pl.when(k == nk - 1)
        def _():
            for hh in range(2):
                dw1_ref[hh, 0] = acc1[hh * 512:(hh + 1) * 512, :]
                dw2_ref[hh, 0] = acc2[hh * 512:(hh + 1) * 512, :]

    out = pl.BlockSpec((2, 1, 512, D), lambda n, k: (0, n, 0, 0))
    return pl.pallas_call(
        body, name="ffn_wgrad", grid=(4, nk),
        in_specs=[pl.BlockSpec((kt, D), lambda n, k: (k, 0)), pl.BlockSpec((kt, D), lambda n, k: (k, n)),
                  pl.BlockSpec((kt, D), lambda n, k: (k, n)), pl.BlockSpec((kt, D), lambda n, k: (k, 0))],
        out_specs=[out, out],
        out_shape=[jax.ShapeDtypeStruct((2, 4, 512, D), F32)] * 2,
        scratch_shapes=[pltpu.VMEM((D, D), F32), pltpu.VMEM((D, D), F32)],
        compiler_params=_cp(48, ("arbitrary", "arbitrary")),
    )(f, a, da, dh2b)


def _proj_bwd(x, metapad, wm, wt3, tabs, dgla, dswa_q, dsk, dsv, dlr, dh1, comm=None):
    t = x.shape[0]
    nblk = t // TM

    def body(x_ref, mp_ref, wm_ref, w3_ref, tab_ref, dg_ref, dq_ref, dk_ref, dv_ref, dlr_ref, dh1_ref,
             gx_ref, gmeta_ref, dw_ref, gmix_ref, w_ref):
        i = pl.program_id(0)

        @pl.when(i == 0)
        def _():
            _join_shards(w3_ref, w_ref)
            dw_ref[...] = jnp.zeros_like(dw_ref)
            gmix_ref[...] = jnp.zeros_like(gmix_ref)

        h = jnp.where(i == nblk, mp_ref[...], x_ref[...])
        uh, rs = _rms(h)
        wm_v = wm_ref[...]
        u = _bf(uh * wm_v)
        tab = tab_ref[...]
        dq = _bf(_rope(dq_ref[...] * 0.125, tab, -1.0))
        dk = _bf(_rope(dk_ref[...], tab, -1.0))
        parts = ((dg_ref[...], 0, R_LR), (dlr_ref[:, 0:16], R_LR, 16), (dq, R_LR + 16, 512),
                 (dk, R_LR + 528, 128), (_bf(dv_ref[...]), R_LR + 656, 128))
        du = jnp.zeros((TM, D), F32)
        for val, r0, w in parts:
            du = du + _dot(val, w_ref[r0:r0 + w, :])
            dw_ref[r0:r0 + w, :] += _dot_tn(val, u)
        gmix_ref[...] += jnp.sum(du * uh, axis=0, keepdims=True)
        dh0 = dh1_ref[...] + _rms_bwd(du, uh, rs, wm_v)

        @pl.when(i < nblk)
        def _():
            gx_ref[...] = dh0

        @pl.when(i == nblk)
        def _():
            gmeta_ref[...] = dh0[:NM]

    xs = pl.BlockSpec((TM, D), lambda i: (jnp.minimum(i, nblk - 1), 0))
    rs_ = lambda w: pl.BlockSpec((TM, w), lambda i: (i, 0))
    return _call(
        body, "proj_bwd", (nblk + 1,),
        [xs, VMEM_SPEC, VMEM_SPEC, VMEM_SPEC, rs_(128), rs_(1536), rs_(512), rs_(128), rs_(128), rs_(128), rs_(D)],
        [xs, pl.BlockSpec((NM, D), lambda i: (0, 0)), VMEM_SPEC, pl.BlockSpec((1, D), lambda i: (0, 0))],
        [jax.ShapeDtypeStruct((t, D), F32), jax.ShapeDtypeStruct((NM, D), F32),
         jax.ShapeDtypeStruct((DIN, D), F32), jax.ShapeDtypeStruct((1, D), F32)],
        [pltpu.VMEM((DIN, D), BF16)], _cp(56),
        (x, metapad, wm, wt3, tabs, dgla, dswa_q, dsk, dsv, dlr, dh1), comm)


def _place():
    return lax.axis_index("x"), lax.axis_index("y"), lax.axis_index("c")


def _other_chips(x, y):
    return [(1 - x, y), (x, 1 - y), (1 - x, 1 - y)]


def _dma_sems(*counts):
    return tuple(pltpu.SemaphoreType.DMA((k,)) for k in counts)


def _gather_shards(shards, split):
    n = len(shards)
    two = [a for a in range(n) if split[a]]

    def plan(ins, outs, sems):
        isend, irecv, dsend, drecv, loc = sems
        x, y, c = _place()
        chips = _other_chips(x, y)

        def part(ref, a, half):
            if not split[a]:
                return ref
            w = shards[a].shape[1] // 2
            return ref.at[:, pl.ds(pl.multiple_of(half * w, 128), w)]

        def over_ici(a, k, shard_of):
            tx, ty = chips[k]
            sx, sy = shard_of
            return pltpu.make_async_remote_copy(
                src_ref=part(ins[a], a, c), dst_ref=part(outs[a].at[2 * sx + sy], a, c), send_sem=isend.at[3 * a + k],
                recv_sem=irecv.at[3 * a + k], device_id=(tx, ty, c), device_id_type=MESH)

        def over_d2d(a, k, half):
            tx, ty = chips[k]
            ref = part(outs[a].at[2 * tx + ty], a, half)
            return pltpu.make_async_remote_copy(
                src_ref=ref, dst_ref=ref, send_sem=dsend.at[3 * a + k], recv_sem=drecv.at[3 * a + k],
                device_id=(x, y, 1 - c), device_id_type=MESH)

        def local(a):
            return pltpu.make_async_copy(ins[a], outs[a].at[2 * x + y], loc.at[a])

        pairs = [(a, k) for a in range(n) for k in range(3)]
        first = ([lambda a=a: local(a).start() for a in range(n)]
                 + [lambda a=a, k=k: over_ici(a, k, (x, y)).start() for a, k in pairs],
                 [lambda a=a, k=k: over_ici(a, k, chips[k]).wait_recv() for a, k in pairs]
                 + [lambda a=a, k=k: over_ici(a, k, (x, y)).wait_send() for a, k in pairs]
                 + [lambda a=a: local(a).wait() for a in range(n)])
        pairs2 = [(a, k) for a in two for k in range(3)]
        second = ([lambda a=a, k=k: over_d2d(a, k, c).start() for a, k in pairs2],
                  [lambda a=a, k=k: over_d2d(a, k, 1 - c).wait_recv() for a, k in pairs2]
                  + [lambda a=a, k=k: over_d2d(a, k, c).wait_send() for a, k in pairs2])
        return [first, second] if two else [first]

    return _Comm(tuple(shards), tuple(jax.ShapeDtypeStruct((4,) + s.shape, s.dtype) for s in shards),
                 _dma_sems(3 * n, 3 * n, 3 * n, 3 * n, n), 2 if two else 1, plan)


def _swap_halves(grads):
    n = len(grads)

    def plan(ins, outs, sems):
        send, recv = sems
        x, y, c = _place()

        def swap(a):
            return pltpu.make_async_remote_copy(
                src_ref=ins[a].at[1 - c], dst_ref=outs[a], send_sem=send.at[a], recv_sem=recv.at[a],
                device_id=(x, y, 1 - c), device_id_type=MESH)

        return [([lambda a=a: swap(a).start() for a in range(n)], [lambda a=a: swap(a).wait() for a in range(n)])]

    return _Comm(tuple(grads), tuple(jax.ShapeDtypeStruct(g.shape[1:], g.dtype) for g in grads), _dma_sems(n, n), 1, plan)


def _scatter_shards(parts):
    n = len(parts)

    def plan(ins, outs, sems):
        send, recv = sems
        x, y, c = _place()
        chips = _other_chips(x, y)

        def scatter(a, k):
            tx, ty = chips[k]
            return pltpu.make_async_remote_copy(
                src_ref=ins[a].at[2 * tx + ty], dst_ref=outs[a].at[k], send_sem=send.at[3 * a + k],
                recv_sem=recv.at[3 * a + k], device_id=(tx, ty, c), device_id_type=MESH)

        pairs = [(a, k) for a in range(n) for k in range(3)]
        return [([lambda a=a, k=k: scatter(a, k).start() for a, k in pairs],
                 [lambda a=a, k=k: scatter(a, k).wait() for a, k in pairs])]

    return _Comm(tuple(parts), tuple(jax.ShapeDtypeStruct((3,) + p.shape[1:], p.dtype) for p in parts),
                 _dma_sems(3 * n, 3 * n), 1, plan)


def _join_halves(halves):
    n = len(halves)

    def plan(ins, outs, sems):
        send, recv, loc = sems
        x, y, c = _place()

        def remote(a, half):
            return pltpu.make_async_remote_copy(
                src_ref=ins[a], dst_ref=outs[a].at[half], send_sem=send.at[a], recv_sem=recv.at[a],
                device_id=(x, y, 1 - c), device_id_type=MESH)

        def local(a):
            return pltpu.make_async_copy(ins[a], outs[a].at[c], loc.at[a])

        every = range(n)
        return [([lambda a=a: local(a).start() for a in every] + [lambda a=a: remote(a, c).start() for a in every],
                 [lambda a=a: remote(a, 1 - c).wait_recv() for a in every]
                 + [lambda a=a: remote(a, c).wait_send() for a in every] + [lambda a=a: local(a).wait() for a in every])]

    return _Comm(tuple(halves), tuple(jax.ShapeDtypeStruct((2,) + h.shape, h.dtype) for h in halves),
                 _dma_sems(n, n, n), 1, plan)


def _reduce_w_in(dwt, comm):
    rows = DIN // 4
    ci, co = len(comm.ins), len(comm.outs)

    def body(*refs):
        dw_ref, c_in, out_ref, c_out = refs[0], refs[1:1 + ci], refs[1 + ci], refs[2 + ci:2 + ci + co]
        mine, tosend, rbuf, qbuf, sib, send, recv, loc = refs[2 + ci + co:10 + ci + co]
        c_sem = refs[10 + ci + co:]
        x, y, c = _place()
        (starts, waits), = comm.plan(c_in, c_out, c_sem)
        _run_phase(starts)
        load = pltpu.make_async_copy(dw_ref, mine, loc.at[0])
        load.start()
        load.wait()
        cps = []
        for k, (tx, ty) in enumerate(_other_chips(x, y)):
            tosend[k] = _bf(mine[2 * tx + ty])
            cps.append(pltpu.make_async_remote_copy(
                src_ref=tosend.at[k], dst_ref=rbuf.at[k], send_sem=send.at[k], recv_sem=recv.at[k],
                device_id=(tx, ty, c), device_id_type=MESH))
            cps[-1].start()
        for cp in cps:
            cp.wait()
        qbuf[...] = mine[2 * x + y] + rbuf[0].astype(F32) + rbuf[1].astype(F32) + rbuf[2].astype(F32)
        swap = pltpu.make_async_remote_copy(src_ref=qbuf, dst_ref=sib, send_sem=send.at[3], recv_sem=recv.at[3],
                                            device_id=(x, y, 1 - c), device_id_type=MESH)
        swap.start()
        swap.wait()
        out_ref[...] = qbuf[...] + sib[...]
        _run_phase(waits)

    outs = pl.pallas_call(
        body, name="reduce_w_in",
        in_specs=[ANY_SPEC] * (1 + ci), out_specs=[VMEM_SPEC] + [ANY_SPEC] * co,
        out_shape=[jax.ShapeDtypeStruct((rows, D), F32)] + list(comm.outs),
        scratch_shapes=[pltpu.VMEM((4, rows, D), F32), pltpu.VMEM((3, rows, D), BF16), pltpu.VMEM((3, rows, D), BF16),
                        pltpu.VMEM((rows, D), F32), pltpu.VMEM((rows, D), F32), *_dma_sems(4, 4, 1), *comm.sems],
        compiler_params=pltpu.CompilerParams(vmem_limit_bytes=48 << 20),
    )(dwt, *comm.ins)
    return outs[0], outs[1:]


def _allreduce_small(pack):
    p = pack.shape[0]

    def body(in_ref, out_ref, buf, send, recv):
        x, y, c = _place()
        me = 4 * x + 2 * y + c
        buf[me] = in_ref[...]

        def peer_of(k):
            return x ^ (k >> 2), y ^ ((k >> 1) & 1), c ^ (k & 1)

        sends = [pltpu.make_async_remote_copy(
            src_ref=in_ref, dst_ref=buf.at[me], send_sem=send.at[k - 1], recv_sem=recv.at[k - 1],
            device_id=peer_of(k), device_id_type=MESH) for k in range(1, 8)]
        for cp in sends:
            cp.start()
        for k in range(1, 8):
            px, py, pc = peer_of(k)
            pltpu.make_async_remote_copy(
                src_ref=in_ref, dst_ref=buf.at[4 * px + 2 * py + pc], send_sem=send.at[k - 1], recv_sem=recv.at[k - 1],
                device_id=(x, y, c), device_id_type=MESH).wait_recv()
        for cp in sends:
            cp.wait_send()
        acc = buf[0]
        for d in range(1, 8):
            acc = acc + buf[d]
        out_ref[...] = acc

    return pl.pallas_call(
        body, name="allreduce_small",
        in_specs=[VMEM_SPEC], out_specs=VMEM_SPEC, out_shape=jax.ShapeDtypeStruct(pack.shape, F32),
        scratch_shapes=[pltpu.VMEM((8, p, D), F32), *_dma_sems(7, 7)],
    )(pack)


GRID4 = 4


def _sum_parts(sel, firsts, others, name, also_bf16):
    n = len(firsts)
    nk = others[0].shape[0]

    def body(sel_ref, *refs):
        fs, os_, outs = refs[:n], refs[n:2 * n], refs[2 * n:]
        for a in range(n):
            acc = fs[a][0]
            for k in range(nk):
                acc = acc + os_[a][k].astype(F32)
            outs[a][...] = acc
            if also_bf16:
                outs[n + a][...] = _bf(acc)

    def rows(a):
        return firsts[a].shape[1] // GRID4

    in_specs = ([pl.BlockSpec((1, rows(a), firsts[a].shape[2]), lambda i, s: (s[0], i, 0)) for a in range(n)]
                + [pl.BlockSpec((nk, rows(a), firsts[a].shape[2]), lambda i, s: (0, i, 0)) for a in range(n)])
    out_specs = [pl.BlockSpec((rows(a), firsts[a].shape[2]), lambda i, s: (i, 0)) for a in range(n)]
    out_shape = [jax.ShapeDtypeStruct(f.shape[1:], F32) for f in firsts]
    if also_bf16:
        out_specs = out_specs * 2
        out_shape = out_shape + [jax.ShapeDtypeStruct(f.shape[1:], BF16) for f in firsts]
    outs = pl.pallas_call(
        body, name=name,
        grid_spec=pltpu.PrefetchScalarGridSpec(num_scalar_prefetch=1, grid=(GRID4,), in_specs=in_specs, out_specs=out_specs),
        out_shape=out_shape, compiler_params=_cp(48),
    )(sel, *firsts, *others)
    return outs[:n], outs[n:]


def _adamw_math(w, g, m, v):
    m2 = ADAM_B1 * m + (1.0 - ADAM_B1) * g
    v2 = ADAM_B2 * v + (1.0 - ADAM_B2) * (g * g)
    m_hat = m2 / (1.0 - ADAM_B1 ** ADAM_STEP)
    v_hat = v2 / (1.0 - ADAM_B2 ** ADAM_STEP)
    return -ADAM_LR * (m_hat / (jnp.sqrt(v_hat) + ADAM_EPS) + ADAM_WD * w), m2, v2


def _adamw_big(ws, gs, ms, vs):
    n = len(ws)

    def body(*refs):
        for a in range(n):
            d, m2, v2 = _adamw_math(refs[a][...], refs[n + a][...], refs[2 * n + a][...], refs[3 * n + a][...])
            refs[4 * n + a][...] = d
            refs[5 * n + a][...] = m2
            refs[6 * n + a][...] = v2

    specs = [pl.BlockSpec((w.shape[0] // GRID4, w.shape[1]), lambda i: (i, 0)) for w in ws]
    return pl.pallas_call(
        body, name="adamw_big", grid=(GRID4,),
        in_specs=specs * 4, out_specs=specs * 3,
        out_shape=[jax.ShapeDtypeStruct(w.shape, F32) for w in ws] * 3,
        compiler_params=_cp(48),
    )(*ws, *gs, *ms, *vs)


def _adamw_small(ws, gs, ms, vs):
    n = len(ws)

    def body(*refs):
        for a in range(n):
            d, m2, v2 = _adamw_math(refs[a][...], refs[n + a][...], refs[2 * n + a][...], refs[3 * n + a][...])
            refs[4 * n + a][...] = d
            refs[5 * n + a][...] = m2
            refs[6 * n + a][...] = v2

    return pl.pallas_call(
        body, name="adamw_small",
        in_specs=[VMEM_SPEC] * (4 * n), out_specs=[VMEM_SPEC] * (3 * n),
        out_shape=[jax.ShapeDtypeStruct(w.shape, F32) for w in ws] * 3,
        compiler_params=pltpu.CompilerParams(vmem_limit_bytes=40 << 20),
    )(*ws, *gs, *ms, *vs)


def kernel(x, meta_tokens, norm_mix_w, w_in, w_gate_up, b_gate, gla_norm_w, sinks, w_out, norm_ff_w, w_ff1, w_ff2, final_norm_w, loss_target, m_meta_tokens, m_norm_mix_w, m_w_in, m_w_gate_up, m_b_gate, m_gla_norm_w, m_sinks, m_w_out, m_norm_ff_w, m_w_ff1, m_w_ff2, m_final_norm_w, v_meta_tokens, v_norm_mix_w, v_w_in, v_w_gate_up, v_b_gate, v_gla_norm_w, v_sinks, v_w_out, v_norm_ff_w, v_w_ff1, v_w_ff2, v_final_norm_w):
    xi, yi, ci = _place()
    shard = (2 * xi + yi).astype(jnp.int32).reshape(1)
    core = ci.astype(jnp.int32).reshape(1)

    small = jnp.concatenate([meta_tokens, w_gate_up[0], jnp.zeros((NM, 64), F32)], axis=1)
    wt3, g_small = _run_comm(_gather_shards([_bf(w_in[0].T), small], [True, False]), "gather_w_in")
    meta = g_small[:, :, 0:256].transpose(1, 0, 2).reshape(NM, D)
    wgu = g_small[:, :, 256:320].transpose(1, 0, 2).reshape(NM, 256)

    xs, tgt = x[0], loss_target[0]
    t = xs.shape[0]
    wfin = final_norm_w.reshape(1, D)
    metapad = jnp.concatenate([meta, jnp.zeros((TM - NM, D), F32)], axis=0)
    wgu_p = _bf(jnp.concatenate([wgu, jnp.zeros((128 - 16, 256), F32)], axis=0))
    tabs = _rope_tables(t)

    w1s, w2s = _bf(w_ff1[0]), _bf(w_ff2[0])
    proj, (g_out, w1a, w1b) = _proj_fwd(xs, metapad, norm_mix_w, wt3, tabs,
                                        _gather_shards([_bf(w_out[0]), w1s[:HK], w1s[HK:]], [True] * 3))
    (oswa, lse), (w2a, w2b) = _swa_fwd(proj, sinks, t, _gather_shards([w2s[:HK], w2s[HK:]], [True] * 2))
    (ogla, oraw, sst), _ = _gla_fwd(proj, wgu_p, b_gate, gla_norm_w, t)
    wo, w1, w2 = g_out.reshape(D, D), (w1a, w1b), (w2a, w2b)
    h1, f, a, dh2, loss, gfin = _mlp_fwd(xs, metapad, tgt, ogla, oswa, wo, norm_ff_w, w1, w2, wfin)

    da, dh2b, dh1, do, dwo, gff = _mlp_bwd(h1, a, dh2, ogla, oswa, wo, norm_ff_w, w1, w2)
    dw1, dw2 = _ffn_wgrad(f, a, da, dh2b)
    big = [dwo.reshape(4, 2, 128, D).transpose(1, 0, 2, 3), dw1, dw2]
    (dsq, dsk, dsv, dsink), theirs = _swa_bwd(proj, sinks, lse, do, t, _swap_halves(big))
    sums, sums_bf = _sum_parts(core, [b.reshape((2, -1) + b.shape[3:]) for b in big],
                               [s.reshape((1, -1) + s.shape[2:]) for s in theirs], "sum_cores", True)
    sums = [s.reshape(b.shape[1:]) for s, b in zip(sums, big)]
    sums_bf = [s.reshape(b.shape[1:]) for s, b in zip(sums_bf, big)]
    (dgla, dlr, dwgu, dbg, dgnw), arrived = _gla_bwd(proj, oraw, sst, do, wgu_p, b_gate, gla_norm_w, t,
                                                     _scatter_shards(sums_bf))
    halves, _ = _sum_parts(shard, sums, arrived, "sum_chips", False)
    (gx, gmeta, dwt, gmix), _ = _proj_bwd(xs, metapad, norm_mix_w, wt3, tabs, dgla, dsq, dsk, dsv, dlr, dh1)

    gwt_in, joined = _reduce_w_in(dwt.reshape(4, DIN // 4, D), _join_halves(halves))
    gw_out, gw_1, gw_2 = [j.reshape((-1, j.shape[2])) for j in joined]

    tail = jnp.concatenate([dbg, dgnw, dsink, loss, jnp.zeros((1, D - 256 - 128 - 8 - 1), F32)], axis=1)
    pack = jnp.concatenate([gmeta, gmix, gff, gfin, tail, dwgu[:16].reshape(4, D)], axis=0)
    tot = _allreduce_small(pack)
    g_meta = lax.dynamic_slice_in_dim(tot[0:NM], shard[0] * 256, 256, axis=1)
    g_mix, g_ff, g_fin = tot[16:17], tot[17:18], tot[18]
    g_bg, g_gnw, g_sinks, loss_tot = tot[19:20, 0:256], tot[19:20, 256:384], tot[19:20, 384:392], tot[19, 392]
    g_wgu = lax.dynamic_slice_in_dim(tot[20:24].reshape(NM, 256), shard[0] * 64, 64, axis=1)

    bo = _adamw_big([w_out[0], w_ff1[0], w_ff2[0]], [gw_out, gw_1, gw_2], [m_w_out[0], m_w_ff1[0], m_w_ff2[0]],
                    [v_w_out[0], v_w_ff1[0], v_w_ff2[0]])

    fin2 = lambda a: a.reshape(1, D)
    sw = [meta_tokens, norm_mix_w, w_gate_up[0], b_gate, gla_norm_w, sinks, norm_ff_w, fin2(final_norm_w), w_in[0].T]
    sg = [g_meta, g_mix, g_wgu, g_bg, g_gnw, g_sinks, g_ff, fin2(g_fin), gwt_in]
    sm = [m_meta_tokens, m_norm_mix_w, m_w_gate_up[0], m_b_gate, m_gla_norm_w, m_sinks, m_norm_ff_w, fin2(m_final_norm_w),
          m_w_in[0].T]
    sv = [v_meta_tokens, v_norm_mix_w, v_w_gate_up[0], v_b_gate, v_gla_norm_w, v_sinks, v_norm_ff_w, fin2(v_final_norm_w),
          v_w_in[0].T]
    so = _adamw_small(sw, sg, sm, sv)

    def ordered(small_o, big_o):
        meta_, mix_, wgu_, bg_, gnw_, sinks_, ff_, fin_, wt_ = small_o
        w_out_, w_1_, w_2_ = big_o
        return (meta_, mix_, wt_.T[None], wgu_[None], bg_, gnw_, sinks_, w_out_[None], ff_, w_1_[None], w_2_[None],
                fin_.reshape(D))

    grads = ordered(sg, [gw_out, gw_1, gw_2])
    deltas = ordered(so[0:9], bo[0:3])
    new_m = ordered(so[9:18], bo[3:6])
    new_v = ordered(so[18:27], bo[6:9])
    return (loss_tot, gx[None], *grads, *deltas, *new_m, *new_v)
```

```python
from typing import Callable, NamedTuple

import jax
import jax.numpy as jnp
import numpy as np
from jax import lax
from jax.experimental import pallas as pl
from jax.experimental.pallas import tpu as pltpu

F32 = jnp.float32
BF16 = jnp.bfloat16

D = 1024
DFF = 4096
NM = 16
TM = 256
CH = 64
SB = 128
QB = 2 * SB
EPS = 1e-5
C_GQ, C_GK, C_GV, C_GR, C_SQ, C_SK, C_SV, C_LR, DINP = 0, 256, 512, 1024, 1536, 2048, 2176, 2304, 2432
DIN = 2320
R_LR = 1536
ROPE_THETA = 500000.0
ADAM_LR, ADAM_B1, ADAM_B2, ADAM_EPS, ADAM_WD, ADAM_STEP = 0.001, 0.9, 0.999, 1e-08, 0.01, 10
NEG = -1e30
MESH = pl.DeviceIdType.MESH
VMEM_SPEC = pl.BlockSpec(memory_space=pltpu.VMEM)
ANY_SPEC = pl.BlockSpec(memory_space=pl.ANY)
SMEM_SPEC = pl.BlockSpec(memory_space=pltpu.SMEM)


def _cp(vmem_mb, sem=("arbitrary",)):
    return pltpu.CompilerParams(dimension_semantics=sem, vmem_limit_bytes=vmem_mb << 20)


def _dot(a, b):
    return jnp.dot(a, b, preferred_element_type=F32)


def _dot_nt(a, b):
    return lax.dot_general(a, b, (((1,), (1,)), ((), ())), preferred_element_type=F32)


def _dot_tn(a, b):
    return lax.dot_general(a, b, (((0,), (0,)), ((), ())), preferred_element_type=F32)


def _bf(x):
    return x.astype(BF16)


def _dot3(m01, x):
    x1 = _bf(x)
    r1 = x - x1.astype(F32)
    x2 = _bf(r1)
    x3 = _bf(r1 - x2.astype(F32))
    return _dot(m01, x1) + _dot(m01, x2) + _dot(m01, x3)


def _rms(h):
    rs = lax.rsqrt(jnp.mean(h * h, axis=-1, keepdims=True) + EPS)
    return h * rs, rs


def _rms_bwd(dy, yhat, rs, w):
    dyh = dy * w
    return rs * (dyh - yhat * jnp.mean(dyh * yhat, axis=-1, keepdims=True))


class _Comm(NamedTuple):
    ins: tuple
    outs: tuple
    sems: tuple
    phases: int
    plan: Callable


def _run_phase(fns):
    for fn in fns:
        fn()


def _call(body, name, grid, in_specs, out_specs, out_shape, scratch, params, args, comm=None):
    if comm is None:
        outs = pl.pallas_call(body, name=name, grid=grid, in_specs=in_specs, out_specs=out_specs, out_shape=out_shape,
                              scratch_shapes=scratch, compiler_params=params)(*args)
        return outs, None
    n_in, n_out, n_scr = len(in_specs), len(out_specs), len(scratch)
    ci, co = len(comm.ins), len(comm.outs)
    last = grid[0] - 1
    marks = [0, max(1, last - max(2, (last + 1) // 6))][:comm.phases]

    def wrapped(*refs):
        own_in, c_in = refs[:n_in], refs[n_in:n_in + ci]
        refs = refs[n_in + ci:]
        own_out, c_out = refs[:n_out], refs[n_out:n_out + co]
        refs = refs[n_out + co:]
        own_scr, c_sem = refs[:n_scr], refs[n_scr:]
        i = pl.program_id(0)

        for p, mark in enumerate(marks):
            @pl.when(i == mark)
            def _():
                plan = comm.plan(c_in, c_out, c_sem)
                if p > 0:
                    _run_phase(plan[p - 1][1])
                _run_phase(plan[p][0])

        body(*own_in, *own_out, *own_scr)

        @pl.when(i == last)
        def _():
            _run_phase(comm.plan(c_in, c_out, c_sem)[-1][1])

    outs = pl.pallas_call(
        wrapped, name=name, grid=grid, in_specs=list(in_specs) + [ANY_SPEC] * ci, out_specs=list(out_specs) + [ANY_SPEC] * co,
        out_shape=list(out_shape) + list(comm.outs), scratch_shapes=list(scratch) + list(comm.sems), compiler_params=params,
    )(*args, *comm.ins)
    return outs[:n_out], outs[n_out:]


def _run_comm(comm, name):
    ci, co = len(comm.ins), len(comm.outs)

    def body(*refs):
        for starts, waits in comm.plan(refs[:ci], refs[ci:ci + co], refs[ci + co:]):
            _run_phase(starts)
            _run_phase(waits)

    return pl.pallas_call(body, name=name, in_specs=[ANY_SPEC] * ci, out_specs=[ANY_SPEC] * co, out_shape=list(comm.outs),
                          scratch_shapes=list(comm.sems))(*comm.ins)


def _join_shards(w3_ref, w_ref):
    for s in range(4):
        w_ref[(DIN // 4) * s:(DIN // 4) * (s + 1), :] = w3_ref[s]


def _proj_fwd(x, metapad, wm, wt3, tabs, comm=None):
    t = x.shape[0]
    nblk = t // TM

    def body(x_ref, mp_ref, wm_ref, w3_ref, tab_ref, proj_ref, w_ref):
        i = pl.program_id(0)

        @pl.when(i == 0)
        def _():
            _join_shards(w3_ref, w_ref)

        h = jnp.where(i == nblk, mp_ref[...], x_ref[...])
        u, _ = _rms(h)
        ub = _bf(u * wm_ref[...])
        proj_ref[:, 0:C_SQ] = _dot_nt(ub, w_ref[0:R_LR, :])
        att = _dot_nt(ub, w_ref[R_LR + 16:DIN, :])
        tab = tab_ref[...]
        proj_ref[:, C_SQ:C_SK] = _rope(att[:, 0:512], tab, 1.0) * 0.125
        proj_ref[:, C_SK:C_SV] = _rope(att[:, 512:640], tab, 1.0)
        proj_ref[:, C_SV:C_LR] = att[:, 640:768]
        proj_ref[:, C_LR:DINP] = jnp.zeros((TM, DINP - C_LR), F32)
        proj_ref[:, C_LR:C_LR + 16] = _dot_nt(ub, w_ref[R_LR:R_LR + 16, :])

    (proj,), got = _call(
        body, "proj_fwd", (nblk + 1,),
        [pl.BlockSpec((TM, D), lambda i: (jnp.minimum(i, nblk - 1), 0)), VMEM_SPEC, VMEM_SPEC, VMEM_SPEC,
         pl.BlockSpec((TM, 128), lambda i: (i, 0))],
        [pl.BlockSpec((TM, DINP), lambda i: (i, 0))], [jax.ShapeDtypeStruct((t + TM, DINP), F32)],
        [pltpu.VMEM((DIN, D), BF16)], _cp(48), (x, metapad, wm, wt3, tabs), comm)
    return proj, got


def _chunk_masks():
    r = lax.broadcasted_iota(jnp.int32, (TM, TM), 0)
    c = lax.broadcasted_iota(jnp.int32, (TM, TM), 1)
    same = (r // CH) == (c // CH)
    lower = _bf(jnp.where(same & (c <= r), 1.0, 0.0))
    upper = _bf(jnp.where(same & (c >= r), 1.0, 0.0))
    return lower, upper


def _gla_gate(lr, wgu, bg, valid, lower):
    z = _dot(_bf(lr), wgu) + bg
    g = (jnp.minimum(z, 0.0) - jnp.log(1.0 + jnp.exp(-jnp.abs(z)))) * (1.0 / 16.0)
    g = jnp.where(valid, g, 0.0)
    return z, _dot3(lower, g)


def _gla_decays(q, k, b):
    nc = TM // CH
    b3 = b.reshape(nc, CH, 256)
    blast = b3[:, CH - 1:CH, :]
    eb = jnp.exp(b)
    enb = jnp.exp(-b)
    ebl = jnp.exp(blast - b3).reshape(TM, 256)
    return eb, enb, ebl, jnp.exp(blast)


def _tri(lower_incl):
    r = lax.broadcasted_iota(jnp.int32, (CH, CH), 0)
    c = lax.broadcasted_iota(jnp.int32, (CH, CH), 1)
    return ((c <= r) if lower_incl else (c >= r))[None]


def _gla_fwd(proj, wgu, bg, gnw, t, comm=None):
    nblk = t // TM
    nt = nblk + 1
    nc = TM // CH

    def blk(i):
        return (i + nblk) % nt

    def body(q_ref, k_ref, v_ref, r_ref, lr_ref, wgu_ref, bg_ref, gnw_ref, o_ref, oraw_ref, sst_ref, st_scr):
        i = pl.program_id(0)

        @pl.when(i == 0)
        def _():
            st_scr[...] = jnp.zeros_like(st_scr)

        rows = blk(i) * TM + lax.broadcasted_iota(jnp.int32, (TM, 1), 0)
        lower, _ = _chunk_masks()
        _, b = _gla_gate(lr_ref[...], wgu_ref[...], bg_ref[...], rows < t + NM, lower)
        q = q_ref[...]
        k = k_ref[...]
        eb, enb, ebl, eblast = _gla_decays(q, k, b)
        qt = q * 0.125 * eb
        kt = k * enb
        kh = k * ebl
        tril = _tri(True)
        outs = []
        for h in range(4):
            hs = slice(h * CH, (h + 1) * CH)
            qh = _bf(qt[:, hs]).reshape(nc, CH, CH)
            kth = _bf(kt[:, hs]).reshape(nc, CH, CH)
            khh = _bf(kh[:, hs]).reshape(nc, CH, CH)
            vh = _bf(v_ref[:, h * 128:(h + 1) * 128]).reshape(nc, CH, 128)
            a = jnp.einsum('cid,cjd->cij', qh, kth, preferred_element_type=F32)
            a = jnp.where(tril, a, 0.0)
            o = jnp.einsum('cij,cjv->civ', _bf(a), vh, preferred_element_type=F32)
            kv = jnp.einsum('cjv,cjd->cvd', vh, khh, preferred_element_type=F32)
            st = st_scr[h]
            o_inter = []
            for c in range(nc):
                sst_ref[c, h] = st
                o_inter.append(_dot_nt(qh[c], _bf(st)))
                st = st * eblast[c, :, hs] + kv[c]
            st_scr[h] = st
            outs.append((o + jnp.stack(o_inter)).reshape(TM, 128))
        oraw = jnp.concatenate(outs, axis=1)
        oraw_ref[...] = oraw
        gn = gnw_ref[...]
        res = []
        for h in range(4):
            on, _ = _rms(oraw[:, h * 128:(h + 1) * 128])
            r = r_ref[:, h * 128:(h + 1) * 128]
            res.append(on * gn * (r * jax.nn.sigmoid(r)))
        o_ref[...] = _bf(jnp.concatenate(res, axis=1))

    def spec(w, cb):
        return pl.BlockSpec((TM, w), lambda i: (blk(i), cb))

    return _call(
        body, "gla_fwd", (nt,),
        [spec(256, 0), spec(256, 1), spec(512, 1), spec(512, 2), spec(128, C_LR // 128), VMEM_SPEC, VMEM_SPEC, VMEM_SPEC],
        [spec(512, 0), spec(512, 0), pl.BlockSpec((nc, 4, 128, CH), lambda i: (blk(i), 0, 0, 0))],
        [jax.ShapeDtypeStruct((t + TM, 512), BF16), jax.ShapeDtypeStruct((t + TM, 512), F32),
         jax.ShapeDtypeStruct((nt * nc, 4, 128, CH), F32)],
        [pltpu.VMEM((4, 128, CH), F32)], _cp(40), (proj, proj, proj, proj, proj, wgu, bg, gnw), comm)


def _gla_bwd(proj, oraw, sst, do, wgu, bg, gnw, t, comm=None):
    nblk = t // TM
    nt = nblk + 1
    nc = TM // CH

    def blk(i):
        return (2 * nblk - i) % nt

    def body(q_ref, k_ref, v_ref, r_ref, lr_ref, oraw_ref, sst_ref, do_ref, wgu_ref, bg_ref, gnw_ref,
             dgla_ref, dlr_ref, dwgu_ref, dbg_ref, dgnw_ref, dst_scr):
        i = pl.program_id(0)

        @pl.when(i == 0)
        def _():
            dst_scr[...] = jnp.zeros_like(dst_scr)
            dwgu_ref[...] = jnp.zeros_like(dwgu_ref)
            dbg_ref[...] = jnp.zeros_like(dbg_ref)
            dgnw_ref[...] = jnp.zeros_like(dgnw_ref)

        rows = blk(i) * TM + lax.broadcasted_iota(jnp.int32, (TM, 1), 0)
        valid = rows < t + NM
        lower, upper = _chunk_masks()
        lr = lr_ref[...]
        z, b = _gla_gate(lr, wgu_ref[...], bg_ref[...], valid, lower)
        q = q_ref[...]
        k = k_ref[...]
        eb, enb, ebl, eblast = _gla_decays(q, k, b)
        qt = q * 0.125 * eb
        kt = k * enb
        kh = k * ebl
        gn = gnw_ref[...]
        tril = _tri(True)
        triu = _tri(False)
        dq_l, dk_l, dv_l, dr_l, db_l, ex_l = [], [], [], [], [], []
        dgn = jnp.zeros((1, 128), F32)
        for h in range(4):
            hs = slice(h * CH, (h + 1) * CH)
            vs = slice(h * 128, (h + 1) * 128)
            on, rs = _rms(oraw_ref[:, vs])
            r = r_ref[:, vs]
            sig = jax.nn.sigmoid(r)
            sil = r * sig
            dy = do_ref[:, vs]
            dr_l.append(dy * on * gn * (sig * (1.0 + r * (1.0 - sig))))
            dgn = dgn + jnp.sum(dy * sil * on, axis=0, keepdims=True)
            doraw = _rms_bwd(dy * sil, on, rs, gn)
            qtf = qt[:, hs].reshape(nc, CH, CH)
            ktf = kt[:, hs].reshape(nc, CH, CH)
            khf = kh[:, hs].reshape(nc, CH, CH)
            qh, kth, khh = _bf(qtf), _bf(ktf), _bf(khf)
            vh = _bf(v_ref[:, vs]).reshape(nc, CH, 128)
            doh = _bf(doraw).reshape(nc, CH, 128)
            at = jnp.where(triu, jnp.einsum('cjd,cid->cji', kth, qh, preferred_element_type=F32), 0.0)
            da = jnp.where(tril, jnp.einsum('civ,cjv->cij', doh, vh, preferred_element_type=F32), 0.0)
            dat = jnp.where(triu, jnp.einsum('cjv,civ->cji', vh, doh, preferred_element_type=F32), 0.0)
            dv = jnp.einsum('cji,civ->cjv', _bf(at), doh, preferred_element_type=F32)
            dqt = jnp.einsum('cij,cjd->cid', _bf(da), kth, preferred_element_type=F32)
            dkt = jnp.einsum('cji,cid->cjd', _bf(dat), qh, preferred_element_type=F32)
            gq = jnp.einsum('civ,cid->cvd', doh, qh, preferred_element_type=F32)
            dst = dst_scr[h]
            dsend = [None] * nc
            for c in reversed(range(nc)):
                dsend[c] = dst
                dst = dst * eblast[c, :, hs] + gq[c]
            dst_scr[h] = dst
            dse = jnp.stack(dsend)
            dseb = _bf(dse)
            stf = sst_ref[:, h]
            dqt = dqt + jnp.einsum('civ,cvd->cid', doh, _bf(stf), preferred_element_type=F32)
            dv = dv + jnp.einsum('cjd,cvd->cjv', khh, dseb, preferred_element_type=F32)
            dkh = jnp.einsum('cjv,cvd->cjd', vh, dseb, preferred_element_type=F32)
            extra = (jnp.sum(dkh * khf, axis=1, keepdims=True)
                     + eblast[:, :, hs] * jnp.sum(dse * stf, axis=1, keepdims=True))
            db_l.append((dqt * qtf - dkt * ktf - dkh * khf).reshape(TM, CH))
            ex_l.append(jnp.broadcast_to(extra, (nc, CH, CH)).reshape(TM, CH))
            dq_l.append((dqt.reshape(TM, CH)) * eb[:, hs] * 0.125)
            dk_l.append(dkt.reshape(TM, CH) * enb[:, hs] + dkh.reshape(TM, CH) * ebl[:, hs])
            dv_l.append(dv.reshape(TM, 128))
        dgnw_ref[...] += dgn
        db = jnp.concatenate(db_l, axis=1)
        dg = _dot3(upper, db) + jnp.concatenate(ex_l, axis=1)
        dz = jnp.where(valid, dg * (1.0 / 16.0) / (1.0 + jnp.exp(z)), 0.0)
        dzb = _bf(dz)
        dlr_ref[...] = _bf(_dot_nt(dzb, wgu_ref[...]))
        dwgu_ref[...] += _dot_tn(_bf(lr), dzb)
        dbg_ref[...] += jnp.sum(dz, axis=0, keepdims=True)
        dgla_ref[...] = _bf(jnp.concatenate(dq_l + dk_l + dv_l + dr_l, axis=1))

    def spec(w, cb):
        return pl.BlockSpec((TM, w), lambda i: (blk(i), cb))

    def acc(shape):
        return pl.BlockSpec(shape, lambda i: (0, 0))

    return _call(
        body, "gla_bwd", (nt,),
        [spec(256, 0), spec(256, 1), spec(512, 1), spec(512, 2), spec(128, C_LR // 128), spec(512, 0),
         pl.BlockSpec((nc, 4, 128, CH), lambda i: (blk(i), 0, 0, 0)), spec(512, 0), VMEM_SPEC, VMEM_SPEC, VMEM_SPEC],
        [spec(1536, 0), spec(128, 0), acc((128, 256)), acc((1, 256)), acc((1, 128))],
        [jax.ShapeDtypeStruct((t + TM, 1536), BF16), jax.ShapeDtypeStruct((t + TM, 128), BF16),
         jax.ShapeDtypeStruct((128, 256), F32), jax.ShapeDtypeStruct((1, 256), F32), jax.ShapeDtypeStruct((1, 128), F32)],
        [pltpu.VMEM((4, 128, CH), F32)], _cp(48), (proj, proj, proj, proj, proj, oraw, sst, do, wgu, bg, gnw), comm)


def _rope_tables(t):
    r = t + TM
    row = np.arange(r)
    pos = np.where(row < t, row + NM, np.where(row < t + NM, row - t, 0)).astype(np.float32)
    inv_freq = (1.0 / (np.float32(ROPE_THETA) ** (np.arange(0, 16, 2, dtype=np.float32) / np.float32(16)))).astype(np.float32)
    ang = (pos[:, None] * inv_freq[None, :]).astype(np.float32)
    cos, sin = np.cos(ang).astype(np.float32), np.sin(ang).astype(np.float32)
    one, zero = np.ones((r, 48), np.float32), np.zeros((r, 48), np.float32)
    return jnp.asarray(np.concatenate([cos, cos, one, -sin, sin, zero], axis=1))


def _rope(x, tab, sign):
    w = x.shape[1]
    rep = w // 64
    c = jnp.concatenate([tab[:, 0:64]] * rep, axis=1)
    s = jnp.concatenate([tab[:, 64:128]] * rep, axis=1)
    lane = lax.rem(lax.broadcasted_iota(jnp.int32, x.shape, 1), 64)
    partner = jnp.where(lane < 8, pltpu.roll(x, w - 8, 1), jnp.where(lane < 16, pltpu.roll(x, 8, 1), 0.0))
    return x * c + sign * (partner * s)


HG_FWD = 1
HB_BWD = 4


def _stack(x, hg):
    w = x.shape[1] // hg
    return x if hg == 1 else jnp.concatenate([x[:, g * w:(g + 1) * w] for g in range(hg)], axis=0)


def _unstack(x, hg):
    return x if hg == 1 else jnp.concatenate([x[g * SB:(g + 1) * SB] for g in range(hg)], axis=1)


def _swa_masks(b, nsb, hg):
    r = lax.rem(lax.broadcasted_iota(jnp.int32, (hg * SB, SB), 0), SB)
    c = lax.broadcasted_iota(jnp.int32, (hg * SB, SB), 1)
    real = b < nsb
    return c <= r, (c > r) & (b > 0) & real, (c < NM) & real


def _swa_specs(nsb):
    def rows(h, w, cb, f):
        return pl.BlockSpec((h, w), lambda i: (f(i), cb))
    pair = lambda i: i
    prev = lambda i: jnp.maximum(2 * i - 1, 0)
    meta = lambda i: nsb
    return rows, pair, prev, meta


def _swa_scores(b, nsb, hg, sink_ref, q, kc, kp, km):
    mc, mp, mm = _swa_masks(b, nsb, hg)
    groups = []
    for kv in range(2):
        ks = slice(kv * 64, (kv + 1) * 64)
        kcb, kpb, kmb = _bf(kc[:, ks]), _bf(kp[:, ks]), _bf(km[:, ks])
        for h0 in range(4 * kv, 4 * kv + 4, hg):
            qg = _bf(_stack(q[:, h0 * 64:(h0 + hg) * 64], hg))
            s_c = jnp.where(mc, _dot_nt(qg, kcb), NEG)
            s_p = jnp.where(mp, _dot_nt(qg, kpb), NEG)
            s_m = jnp.where(mm, _dot_nt(qg, kmb), NEG)
            sink = jnp.concatenate([jnp.full((SB, 1), sink_ref[0, h0 + g], F32) for g in range(hg)], axis=0)
            groups.append((kv, h0, qg, kcb, kpb, kmb, s_c, s_p, s_m, sink))
    return groups


def _swa_fwd(proj, sinks, t, comm=None):
    nsb = t // SB
    r_tot = t + TM
    rows, pair, prev, meta = _swa_specs(nsb)

    def body(sink_ref, q_ref, kc_ref, kp_ref, km_ref, vc_ref, vp_ref, vm_ref, o_ref, lse_ref):
        i = pl.program_id(0)
        km, vm = km_ref[...], vm_ref[...]
        for j in range(2):
            b = 2 * i + j
            rs = slice(j * SB, (j + 1) * SB)
            kp = kp_ref[...] if j == 0 else kc_ref[0:SB, :]
            vp = vp_ref[...] if j == 0 else vc_ref[0:SB, :]
            vc = vc_ref[rs, :]
            o_l, lse_l = [], []
            for kv, h0, qg, kcb, kpb, kmb, s_c, s_p, s_m, sink in _swa_scores(
                    b, nsb, HG_FWD, sink_ref, q_ref[rs, :], kc_ref[rs, :], kp, km):
                ks = slice(kv * 64, (kv + 1) * 64)
                m = jnp.maximum(jnp.max(jnp.maximum(jnp.maximum(s_c, s_p), s_m), -1, keepdims=True), sink)
                p_c, p_p, p_m = jnp.exp(s_c - m), jnp.exp(s_p - m), jnp.exp(s_m - m)
                l = jnp.sum(p_c + p_p + p_m, -1, keepdims=True) + jnp.exp(sink - m)
                o = _dot(_bf(p_c), _bf(vc[:, ks])) + _dot(_bf(p_p), _bf(vp[:, ks])) + _dot(_bf(p_m), _bf(vm[:, ks]))
                o_l.append(_unstack(o * (1.0 / l), HG_FWD))
                lse_l.append(_unstack(m + jnp.log(l), HG_FWD))
            valid = b * SB + lax.broadcasted_iota(jnp.int32, (SB, 1), 0) < t + NM
            o_ref[rs, :] = _bf(jnp.where(valid, jnp.concatenate(o_l, axis=1), 0.0))
            lse_ref[:, rs] = jnp.concatenate(lse_l, axis=1).T

    ck, cv = C_SK // 128, C_SV // 128
    return _call(
        body, "swa_fwd", (r_tot // QB,),
        [SMEM_SPEC, rows(QB, 512, C_SQ // 512, pair),
         rows(QB, 128, ck, pair), rows(SB, 128, ck, prev), rows(SB, 128, ck, meta),
         rows(QB, 128, cv, pair), rows(SB, 128, cv, prev), rows(SB, 128, cv, meta)],
        [rows(QB, 512, 0, pair), pl.BlockSpec((8, QB), lambda i: (0, i))],
        [jax.ShapeDtypeStruct((r_tot, 512), BF16), jax.ShapeDtypeStruct((8, r_tot), F32)],
        [], _cp(32), (sinks, proj, proj, proj, proj, proj, proj, proj), comm)


def _swa_bwd(proj, sinks, lse_t, do, t, comm=None):
    nsb = t // SB
    r_tot = t + TM
    rows, pair, prev, meta = _swa_specs(nsb)
    hb = HB_BWD
    lanes = hb * SB

    def body(sink_ref, q_ref, kc_ref, kp_ref, km_ref, vc_ref, vp_ref, vm_ref, lse_ref, do_ref,
             dq_ref, dk_ref, dv_ref, dsink_ref):
        i = pl.program_id(0)

        @pl.when(i == 0)
        def _():
            dk_ref[...] = jnp.zeros_like(dk_ref)
            dv_ref[...] = jnp.zeros_like(dv_ref)
            dsink_ref[...] = jnp.zeros_like(dsink_ref)

        key = lax.broadcasted_iota(jnp.int32, (SB, lanes), 0)
        qry = lax.rem(lax.broadcasted_iota(jnp.int32, (SB, lanes), 1), SB)
        km, vm = km_ref[...], vm_ref[...]
        dsink_l = []
        for j in range(2):
            b = 2 * i + j
            rs = slice(j * SB, (j + 1) * SB)
            real = b < nsb
            masks = (key <= qry, (key > qry) & (b > 0) & real, (key < NM) & real)
            k3 = (kc_ref[rs, :], kp_ref[...] if j == 0 else kc_ref[0:SB, :], km)
            v3 = (vc_ref[rs, :], vp_ref[...] if j == 0 else vc_ref[0:SB, :], vm)
            zero = jnp.zeros((SB, 64), F32)
            dq_l, ds_blk = [], []
            dk_l, dv_l = [[zero, zero] for _ in range(3)], [[zero, zero] for _ in range(3)]
            for h0 in range(0, 8, hb):
                kv = h0 // 4
                ks, hs = slice(kv * 64, (kv + 1) * 64), slice(h0 * 64, (h0 + hb) * 64)
                qg = _bf(_stack(q_ref[rs, hs], hb))
                dog = _bf(_stack(do_ref[rs, hs], hb))
                lse_row = jnp.concatenate([lse_ref[h:h + 1, rs] for h in range(h0, h0 + hb)], axis=1)
                sink_row = jnp.concatenate([jnp.full((1, SB), sink_ref[0, h], F32) for h in range(h0, h0 + hb)], axis=1)
                kb = [_bf(k[:, ks]) for k in k3]
                vb = [_bf(v[:, ks]) for v in v3]
                p = [jnp.exp(jnp.where(m, _dot_nt(k, qg), NEG) - lse_row) for m, k in zip(masks, kb)]
                dp = [_dot_nt(v, dog) for v in vb]
                delta = jnp.sum(p[0] * dp[0] + p[1] * dp[1] + p[2] * dp[2], axis=0, keepdims=True)
                ds = [_bf(pp * (dd - delta)) for pp, dd in zip(p, dp)]
                dq_t = _dot_tn(kb[0], ds[0]) + _dot_tn(kb[1], ds[1]) + _dot_tn(kb[2], ds[2])
                dq_l.append(_unstack(dq_t.T, hb))
                for x in range(3):
                    dk_l[x][kv] = dk_l[x][kv] + _dot(ds[x], qg)
                    dv_l[x][kv] = dv_l[x][kv] + _dot(_bf(p[x]), dog)
                ds_row = -jnp.exp(sink_row - lse_row) * delta
                ds_blk += [jnp.sum(ds_row[:, g * SB:(g + 1) * SB], axis=1, keepdims=True) for g in range(hb)]
            dsink_l.append(jnp.concatenate(ds_blk, axis=1))
            dq_ref[rs, :] = jnp.concatenate(dq_l, axis=1)
            starts = (pl.multiple_of(b * SB, SB), pl.multiple_of(jnp.maximum(b - 1, 0) * SB, SB), t)
            for x in range(3):
                dk_ref[pl.ds(starts[x], SB), :] += jnp.concatenate(dk_l[x], axis=1)
                dv_ref[pl.ds(starts[x], SB), :] += jnp.concatenate(dv_l[x], axis=1)
        dsink_ref[...] += dsink_l[0] + dsink_l[1]

    ck, cv = C_SK // 128, C_SV // 128
    whole = lambda w: pl.BlockSpec((r_tot, w), lambda i: (0, 0))
    return _call(
        body, "swa_bwd", (r_tot // QB,),
        [SMEM_SPEC, rows(QB, 512, C_SQ // 512, pair),
         rows(QB, 128, ck, pair), rows(SB, 128, ck, prev), rows(SB, 128, ck, meta),
         rows(QB, 128, cv, pair), rows(SB, 128, cv, prev), rows(SB, 128, cv, meta),
         pl.BlockSpec((8, QB), lambda i: (0, i)), rows(QB, 512, 1, pair)],
        [rows(QB, 512, 0, pair), whole(128), whole(128), pl.BlockSpec((1, 8), lambda i: (0, 0))],
        [jax.ShapeDtypeStruct((r_tot, 512), F32), jax.ShapeDtypeStruct((r_tot, 128), F32),
         jax.ShapeDtypeStruct((r_tot, 128), F32), jax.ShapeDtypeStruct((1, 8), F32)],
        [], _cp(48), (sinks, proj, proj, proj, proj, proj, proj, proj, lse_t, do), comm)


HK = D // 2


def _mlp_fwd(x, metapad, tgt, ogla, oswa, wo, wff, w1, w2, wfin):
    t = x.shape[0]
    nblk = t // TM

    def body(x_ref, mp_ref, tgt_ref, og_ref, os_ref, wo_ref, wff_ref, w1a_ref, w1b_ref, w2a_ref, w2b_ref, wfin_ref,
             h1_ref, f_ref, a_ref, dh2_ref, loss_ref, gfin_ref):
        i = pl.program_id(0)

        @pl.when(i == 0)
        def _():
            loss_ref[...] = jnp.zeros_like(loss_ref)
            gfin_ref[...] = jnp.zeros_like(gfin_ref)

        h0 = jnp.where(i == nblk, mp_ref[...], x_ref[...])
        h1 = h0 + _dot(og_ref[...], wo_ref[0:512, :]) + _dot(os_ref[...], wo_ref[512:1024, :])
        h1_ref[...] = h1
        fh, _ = _rms(h1)
        f = _bf(fh * wff_ref[...])
        f_ref[...] = f
        acc = jnp.zeros((TM, D), F32)
        for n in range(4):
            a = _dot(f[:, 0:HK], w1a_ref[n]) + _dot(f[:, HK:D], w1b_ref[n])
            a_ref[:, n * D:(n + 1) * D] = _bf(a)
            zr = jnp.maximum(a, 0.0)
            z = _bf(zr * zr)
            acc = acc + _dot(z[:, 0:HK], w2a_ref[n]) + _dot(z[:, HK:D], w2b_ref[n])
        h2 = h1 + acc
        yh, rs2 = _rms(h2)
        wf = wfin_ref[...]
        real = i < nblk
        e = jnp.where(real, yh * wf - tgt_ref[...], 0.0)
        loss_ref[...] += jnp.sum(jnp.sum(e * e, axis=0, keepdims=True), axis=1, keepdims=True) * (0.5 / D)
        dy = e * (1.0 / D)
        gfin_ref[...] += jnp.sum(dy * yh, axis=0, keepdims=True)
        dh2_ref[...] = _rms_bwd(dy, yh, rs2, wf)

    xs = pl.BlockSpec((TM, D), lambda i: (jnp.minimum(i, nblk - 1), 0))
    rs = lambda w: pl.BlockSpec((TM, w), lambda i: (i, 0))
    r_tot = t + TM
    return pl.pallas_call(
        body, name="mlp_fwd", grid=(nblk + 1,),
        in_specs=[xs, VMEM_SPEC, xs, rs(512), rs(512)] + [VMEM_SPEC] * 7,
        out_specs=[rs(D), rs(D), rs(DFF), rs(D), pl.BlockSpec((1, 1), lambda i: (0, 0)), pl.BlockSpec((1, D), lambda i: (0, 0))],
        out_shape=[jax.ShapeDtypeStruct((r_tot, D), F32), jax.ShapeDtypeStruct((r_tot, D), BF16),
                   jax.ShapeDtypeStruct((r_tot, DFF), BF16), jax.ShapeDtypeStruct((r_tot, D), F32),
                   jax.ShapeDtypeStruct((1, 1), F32), jax.ShapeDtypeStruct((1, D), F32)],
        compiler_params=_cp(56),
    )(x, metapad, tgt, ogla, oswa, wo, wff, *w1, *w2, wfin)


def _mlp_bwd(h1, a, dh2, ogla, oswa, wo, wff, w1, w2):
    r_tot = h1.shape[0]
    nt = r_tot // TM

    def body(h1_ref, a_ref, dh2_ref, og_ref, os_ref, wo_ref, wff_ref, w1a_ref, w1b_ref, w2a_ref, w2b_ref,
             da_ref, dh2b_ref, dh1_ref, do_ref, dwo_ref, gff_ref):
        i = pl.program_id(0)

        @pl.when(i == 0)
        def _():
            dwo_ref[...] = jnp.zeros_like(dwo_ref)
            gff_ref[...] = jnp.zeros_like(gff_ref)

        dh2 = dh2_ref[...]
        dh2b = _bf(dh2)
        dh2b_ref[...] = dh2b
        dfa = jnp.zeros((TM, HK), F32)
        dfb = jnp.zeros((TM, HK), F32)
        for n in range(4):
            dz = jnp.concatenate([_dot_nt(dh2b, w2a_ref[n]), _dot_nt(dh2b, w2b_ref[n])], axis=1)
            da = _bf(dz * (2.0 * jnp.maximum(a_ref[:, n * D:(n + 1) * D].astype(F32), 0.0)))
            da_ref[:, n * D:(n + 1) * D] = da
            dfa = dfa + _dot_nt(da, w1a_ref[n])
            dfb = dfb + _dot_nt(da, w1b_ref[n])
        df = jnp.concatenate([dfa, dfb], axis=1)
        fh, rs1 = _rms(h1_ref[...])
        gff_ref[...] += jnp.sum(df * fh, axis=0, keepdims=True)
        dh1 = dh2 + _rms_bwd(df, fh, rs1, wff_ref[...])
        dh1_ref[...] = dh1
        dh1b = _bf(dh1)
        do_ref[...] = _dot_nt(dh1b, wo_ref[...])
        dwo_ref[0:512, :] += _dot_tn(og_ref[...], dh1b)
        dwo_ref[512:1024, :] += _dot_tn(os_ref[...], dh1b)

    rs = lambda w: pl.BlockSpec((TM, w), lambda i: (i, 0))
    return pl.pallas_call(
        body, name="mlp_bwd", grid=(nt,),
        in_specs=[rs(D), rs(DFF), rs(D), rs(512), rs(512)] + [VMEM_SPEC] * 6,
        out_specs=[rs(DFF), rs(D), rs(D), rs(D), pl.BlockSpec((D, D), lambda i: (0, 0)),
                   pl.BlockSpec((1, D), lambda i: (0, 0))],
        out_shape=[jax.ShapeDtypeStruct((r_tot, DFF), BF16), jax.ShapeDtypeStruct((r_tot, D), BF16),
                   jax.ShapeDtypeStruct((r_tot, D), F32), jax.ShapeDtypeStruct((r_tot, D), F32),
                   jax.ShapeDtypeStruct((D, D), F32), jax.ShapeDtypeStruct((1, D), F32)],
        compiler_params=_cp(56),
    )(h1, a, dh2, ogla, oswa, wo, wff, *w1, *w2)


def _ffn_wgrad(f, a, da, dh2b):
    r_tot = f.shape[0]
    kt = 768 if r_tot % 768 == 0 else TM
    nk = r_tot // kt

    def body(f_ref, a_ref, da_ref, dh2_ref, dw1_ref, dw2_ref, acc1, acc2):
        k = pl.program_id(1)

        @pl.when(k == 0)
        def _():
            acc1[...] = jnp.zeros_like(acc1)
            acc2[...] = jnp.zeros_like(acc2)

        zr = jnp.maximum(a_ref[...], 0.0)
        acc1[...] += _dot_tn(f_ref[...], da_ref[...])
        acc2[...] += _dot_tn(zr * zr, dh2_ref[...])

        @pl.when(k == nk - 1)
        def _():
            for hh in range(2):
                dw1_ref[hh, 0] = acc1[hh * 512:(hh + 1) * 512, :]
                dw2_ref[hh, 0] = acc2[hh * 512:(hh + 1) * 512, :]

    out = pl.BlockSpec((2, 1, 512, D), lambda n, k: (0, n, 0, 0))
    return pl.pallas_call(
        body, name="ffn_wgrad", grid=(4, nk),
        in_specs=[pl.BlockSpec((kt, D), lambda n, k: (k, 0)), pl.BlockSpec((kt, D), lambda n, k: (k, n)),
                  pl.BlockSpec((kt, D), lambda n, k: (k, n)), pl.BlockSpec((kt, D), lambda n, k: (k, 0))],
        out_specs=[out, out],
        out_shape=[jax.ShapeDtypeStruct((2, 4, 512, D), F32)] * 2,
        scratch_shapes=[pltpu.VMEM((D, D), F32), pltpu.VMEM((D, D), F32)],
        compiler_params=_cp(48, ("arbitrary", "arbitrary")),
    )(f, a, da, dh2b)


def _proj_bwd(x, metapad, wm, wt3, tabs, dgla, dswa_q, dsk, dsv, dlr, dh1, comm=None):
    t = x.shape[0]
    nblk = t // TM

    def body(x_ref, mp_ref, wm_ref, w3_ref, tab_ref, dg_ref, dq_ref, dk_ref, dv_ref, dlr_ref, dh1_ref,
             gx_ref, gmeta_ref, dw_ref, gmix_ref, w_ref):
        i = pl.program_id(0)

        @pl.when(i == 0)
        def _():
            _join_shards(w3_ref, w_ref)
            dw_ref[...] = jnp.zeros_like(dw_ref)
            gmix_ref[...] = jnp.zeros_like(gmix_ref)

        h = jnp.where(i == nblk, mp_ref[...], x_ref[...])
        uh, rs = _rms(h)
        wm_v = wm_ref[...]
        u = _bf(uh * wm_v)
        tab = tab_ref[...]
        dq = _bf(_rope(dq_ref[...] * 0.125, tab, -1.0))
        dk = _bf(_rope(dk_ref[...], tab, -1.0))
        parts = ((dg_ref[...], 0, R_LR), (dlr_ref[:, 0:16], R_LR, 16), (dq, R_LR + 16, 512),
                 (dk, R_LR + 528, 128), (_bf(dv_ref[...]), R_LR + 656, 128))
        du = jnp.zeros((TM, D), F32)
        for val, r0, w in parts:
            du = du + _dot(val, w_ref[r0:r0 + w, :])
            dw_ref[r0:r0 + w, :] += _dot_tn(val, u)
        gmix_ref[...] += jnp.sum(du * uh, axis=0, keepdims=True)
        dh0 = dh1_ref[...] + _rms_bwd(du, uh, rs, wm_v)

        @pl.when(i < nblk)
        def _():
            gx_ref[...] = dh0

        @pl.when(i == nblk)
        def _():
            gmeta_ref[...] = dh0[:NM]

    xs = pl.BlockSpec((TM, D), lambda i: (jnp.minimum(i, nblk - 1), 0))
    rs_ = lambda w: pl.BlockSpec((TM, w), lambda i: (i, 0))
    return _call(
        body, "proj_bwd", (nblk + 1,),
        [xs, VMEM_SPEC, VMEM_SPEC, VMEM_SPEC, rs_(128), rs_(1536), rs_(512), rs_(128), rs_(128), rs_(128), rs_(D)],
        [xs, pl.BlockSpec((NM, D), lambda i: (0, 0)), VMEM_SPEC, pl.BlockSpec((1, D), lambda i: (0, 0))],
        [jax.ShapeDtypeStruct((t, D), F32), jax.ShapeDtypeStruct((NM, D), F32),
         jax.ShapeDtypeStruct((DIN, D), F32), jax.ShapeDtypeStruct((1, D), F32)],
        [pltpu.VMEM((DIN, D), BF16)], _cp(56),
        (x, metapad, wm, wt3, tabs, dgla, dswa_q, dsk, dsv, dlr, dh1), comm)


def _place():
    return lax.axis_index("x"), lax.axis_index("y"), lax.axis_index("c")


def _other_chips(x, y):
    return [(1 - x, y), (x, 1 - y), (1 - x, 1 - y)]


def _dma_sems(*counts):
    return tuple(pltpu.SemaphoreType.DMA((k,)) for k in counts)


def _gather_shards(shards, split):
    n = len(shards)
    two = [a for a in range(n) if split[a]]

    def plan(ins, outs, sems):
        isend, irecv, dsend, drecv, loc = sems
        x, y, c = _place()
        chips = _other_chips(x, y)

        def part(ref, a, half):
            if not split[a]:
                return ref
            w = shards[a].shape[1] // 2
            return ref.at[:, pl.ds(pl.multiple_of(half * w, 128), w)]

        def over_ici(a, k, shard_of):
            tx, ty = chips[k]
            sx, sy = shard_of
            return pltpu.make_async_remote_copy(
                src_ref=part(ins[a], a, c), dst_ref=part(outs[a].at[2 * sx + sy], a, c), send_sem=isend.at[3 * a + k],
                recv_sem=irecv.at[3 * a + k], device_id=(tx, ty, c), device_id_type=MESH)

        def over_d2d(a, k, half):
            tx, ty = chips[k]
            ref = part(outs[a].at[2 * tx + ty], a, half)
            return pltpu.make_async_remote_copy(
                src_ref=ref, dst_ref=ref, send_sem=dsend.at[3 * a + k], recv_sem=drecv.at[3 * a + k],
                device_id=(x, y, 1 - c), device_id_type=MESH)

        def local(a):
            return pltpu.make_async_copy(ins[a], outs[a].at[2 * x + y], loc.at[a])

        pairs = [(a, k) for a in range(n) for k in range(3)]
        first = ([lambda a=a: local(a).start() for a in range(n)]
                 + [lambda a=a, k=k: over_ici(a, k, (x, y)).start() for a, k in pairs],
                 [lambda a=a, k=k: over_ici(a, k, chips[k]).wait_recv() for a, k in pairs]
                 + [lambda a=a, k=k: over_ici(a, k, (x, y)).wait_send() for a, k in pairs]
                 + [lambda a=a: local(a).wait() for a in range(n)])
        pairs2 = [(a, k) for a in two for k in range(3)]
        second = ([lambda a=a, k=k: over_d2d(a, k, c).start() for a, k in pairs2],
                  [lambda a=a, k=k: over_d2d(a, k, 1 - c).wait_recv() for a, k in pairs2]
                  + [lambda a=a, k=k: over_d2d(a, k, c).wait_send() for a, k in pairs2])
        return [first, second] if two else [first]

    return _Comm(tuple(shards), tuple(jax.ShapeDtypeStruct((4,) + s.shape, s.dtype) for s in shards),
                 _dma_sems(3 * n, 3 * n, 3 * n, 3 * n, n), 2 if two else 1, plan)


def _swap_halves(grads):
    n = len(grads)

    def plan(ins, outs, sems):
        send, recv = sems
        x, y, c = _place()

        def swap(a):
            return pltpu.make_async_remote_copy(
                src_ref=ins[a].at[1 - c], dst_ref=outs[a], send_sem=send.at[a], recv_sem=recv.at[a],
                device_id=(x, y, 1 - c), device_id_type=MESH)

        return [([lambda a=a: swap(a).start() for a in range(n)], [lambda a=a: swap(a).wait() for a in range(n)])]

    return _Comm(tuple(grads), tuple(jax.ShapeDtypeStruct(g.shape[1:], g.dtype) for g in grads), _dma_sems(n, n), 1, plan)


def _scatter_shards(parts):
    n = len(parts)

    def plan(ins, outs, sems):
        send, recv = sems
        x, y, c = _place()
        chips = _other_chips(x, y)

        def scatter(a, k):
            tx, ty = chips[k]
            return pltpu.make_async_remote_copy(
                src_ref=ins[a].at[2 * tx + ty], dst_ref=outs[a].at[k], send_sem=send.at[3 * a + k],
                recv_sem=recv.at[3 * a + k], device_id=(tx, ty, c), device_id_type=MESH)

        pairs = [(a, k) for a in range(n) for k in range(3)]
        return [([lambda a=a, k=k: scatter(a, k).start() for a, k in pairs],
                 [lambda a=a, k=k: scatter(a, k).wait() for a, k in pairs])]

    return _Comm(tuple(parts), tuple(jax.ShapeDtypeStruct((3,) + p.shape[1:], p.dtype) for p in parts),
                 _dma_sems(3 * n, 3 * n), 1, plan)


def _join_halves(halves):
    n = len(halves)

    def plan(ins, outs, sems):
        send, recv, loc = sems
        x, y, c = _place()

        def remote(a, half):
            return pltpu.make_async_remote_copy(
                src_ref=ins[a], dst_ref=outs[a].at[half], send_sem=send.at[a], recv_sem=recv.at[a],
                device_id=(x, y, 1 - c), device_id_type=MESH)

        def local(a):
            return pltpu.make_async_copy(ins[a], outs[a].at[c], loc.at[a])

        every = range(n)
        return [([lambda a=a: local(a).start() for a in every] + [lambda a=a: remote(a, c).start() for a in every],
                 [lambda a=a: remote(a, 1 - c).wait_recv() for a in every]
                 + [lambda a=a: remote(a, c).wait_send() for a in every] + [lambda a=a: local(a).wait() for a in every])]

    return _Comm(tuple(halves), tuple(jax.ShapeDtypeStruct((2,) + h.shape, h.dtype) for h in halves),
                 _dma_sems(n, n, n), 1, plan)


def _reduce_w_in(dwt, comm):
    rows, hw = DIN // 4, D // 2
    ci, co = len(comm.ins), len(comm.outs)

    def body(*refs):
        dw_ref, c_in, out_ref, c_out = refs[0], refs[1:1 + ci], refs[1 + ci], refs[2 + ci:2 + ci + co]
        mine, sib, tosend, rbuf, qbuf, full, send, recv, loc = refs[2 + ci + co:11 + ci + co]
        c_sem = refs[11 + ci + co:]
        x, y, c = _place()
        sibling = (x, y, 1 - c)
        (starts, waits), = comm.plan(c_in, c_out, c_sem)
        _run_phase(starts)

        def cols(ref, half):
            window = pl.ds(pl.multiple_of(half * hw, 128), hw)
            return ref.at[:, :, window] if len(ref.shape) == 3 else ref.at[:, window]

        load = pltpu.make_async_copy(cols(dw_ref, c), mine, loc.at[0])
        give = pltpu.make_async_remote_copy(src_ref=cols(dw_ref, 1 - c), dst_ref=sib, send_sem=send.at[3], recv_sem=recv.at[3],
                                            device_id=sibling, device_id_type=MESH)
        load.start()
        give.start()
        load.wait()
        give.wait()
        mine[...] = mine[...] + sib[...]
        cps = []
        for k, (tx, ty) in enumerate(_other_chips(x, y)):
            tosend[k] = _bf(mine[2 * tx + ty])
            cps.append(pltpu.make_async_remote_copy(
                src_ref=tosend.at[k], dst_ref=rbuf.at[k], send_sem=send.at[k], recv_sem=recv.at[k],
                device_id=(tx, ty, c), device_id_type=MESH))
            cps[-1].start()
        for cp in cps:
            cp.wait()
        qbuf[...] = mine[2 * x + y] + rbuf[0].astype(F32) + rbuf[1].astype(F32) + rbuf[2].astype(F32)
        keep = pltpu.make_async_copy(qbuf, cols(full, c), loc.at[1])
        pass_on = pltpu.make_async_remote_copy(src_ref=qbuf, dst_ref=cols(full, c), send_sem=send.at[4], recv_sem=recv.at[4],
                                               device_id=sibling, device_id_type=MESH)
        keep.start()
        pass_on.start()
        keep.wait()
        pass_on.wait_send()
        pltpu.make_async_remote_copy(src_ref=qbuf, dst_ref=cols(full, 1 - c), send_sem=send.at[4], recv_sem=recv.at[4],
                                     device_id=sibling, device_id_type=MESH).wait_recv()
        out_ref[...] = full[...]
        _run_phase(waits)

    outs = pl.pallas_call(
        body, name="reduce_w_in",
        in_specs=[ANY_SPEC] * (1 + ci), out_specs=[VMEM_SPEC] + [ANY_SPEC] * co,
        out_shape=[jax.ShapeDtypeStruct((rows, D), F32)] + list(comm.outs),
        scratch_shapes=[pltpu.VMEM((4, rows, hw), F32), pltpu.VMEM((4, rows, hw), F32), pltpu.VMEM((3, rows, hw), BF16),
                        pltpu.VMEM((3, rows, hw), BF16), pltpu.VMEM((rows, hw), F32), pltpu.VMEM((rows, D), F32),
                        *_dma_sems(5, 5, 2), *comm.sems],
        compiler_params=pltpu.CompilerParams(vmem_limit_bytes=48 << 20),
    )(dwt, *comm.ins)
    return outs[0], outs[1:]


def _allreduce_small(pack):
    p = pack.shape[0]

    def body(in_ref, out_ref, buf, send, recv):
        x, y, c = _place()
        me = 4 * x + 2 * y + c
        buf[me] = in_ref[...]

        def peer_of(k):
            return x ^ (k >> 2), y ^ ((k >> 1) & 1), c ^ (k & 1)

        sends = [pltpu.make_async_remote_copy(
            src_ref=in_ref, dst_ref=buf.at[me], send_sem=send.at[k - 1], recv_sem=recv.at[k - 1],
            device_id=peer_of(k), device_id_type=MESH) for k in range(1, 8)]
        for cp in sends:
            cp.start()
        for k in range(1, 8):
            px, py, pc = peer_of(k)
            pltpu.make_async_remote_copy(
                src_ref=in_ref, dst_ref=buf.at[4 * px + 2 * py + pc], send_sem=send.at[k - 1], recv_sem=recv.at[k - 1],
                device_id=(x, y, c), device_id_type=MESH).wait_recv()
        for cp in sends:
            cp.wait_send()
        acc = buf[0]
        for d in range(1, 8):
            acc = acc + buf[d]
        out_ref[...] = acc

    return pl.pallas_call(
        body, name="allreduce_small",
        in_specs=[VMEM_SPEC], out_specs=VMEM_SPEC, out_shape=jax.ShapeDtypeStruct(pack.shape, F32),
        scratch_shapes=[pltpu.VMEM((8, p, D), F32), *_dma_sems(7, 7)],
    )(pack)


GRID4 = 4


def _sum_parts(sel, firsts, others, name, also_bf16):
    n = len(firsts)
    nk = others[0].shape[0]

    def body(sel_ref, *refs):
        fs, os_, outs = refs[:n], refs[n:2 * n], refs[2 * n:]
        for a in range(n):
            acc = fs[a][0]
            for k in range(nk):
                acc = acc + os_[a][k].astype(F32)
            outs[a][...] = acc
            if also_bf16:
                outs[n + a][...] = _bf(acc)

    def rows(a):
        return firsts[a].shape[1] // GRID4

    in_specs = ([pl.BlockSpec((1, rows(a), firsts[a].shape[2]), lambda i, s: (s[0], i, 0)) for a in range(n)]
                + [pl.BlockSpec((nk, rows(a), firsts[a].shape[2]), lambda i, s: (0, i, 0)) for a in range(n)])
    out_specs = [pl.BlockSpec((rows(a), firsts[a].shape[2]), lambda i, s: (i, 0)) for a in range(n)]
    out_shape = [jax.ShapeDtypeStruct(f.shape[1:], F32) for f in firsts]
    if also_bf16:
        out_specs = out_specs * 2
        out_shape = out_shape + [jax.ShapeDtypeStruct(f.shape[1:], BF16) for f in firsts]
    outs = pl.pallas_call(
        body, name=name,
        grid_spec=pltpu.PrefetchScalarGridSpec(num_scalar_prefetch=1, grid=(GRID4,), in_specs=in_specs, out_specs=out_specs),
        out_shape=out_shape, compiler_params=_cp(48),
    )(sel, *firsts, *others)
    return outs[:n], outs[n:]


def _adamw_math(w, g, m, v):
    m2 = ADAM_B1 * m + (1.0 - ADAM_B1) * g
    v2 = ADAM_B2 * v + (1.0 - ADAM_B2) * (g * g)
    m_hat = m2 / (1.0 - ADAM_B1 ** ADAM_STEP)
    v_hat = v2 / (1.0 - ADAM_B2 ** ADAM_STEP)
    return -ADAM_LR * (m_hat / (jnp.sqrt(v_hat) + ADAM_EPS) + ADAM_WD * w), m2, v2


def _adamw_big(ws, gs, ms, vs):
    n = len(ws)

    def body(*refs):
        for a in range(n):
            d, m2, v2 = _adamw_math(refs[a][...], refs[n + a][...], refs[2 * n + a][...], refs[3 * n + a][...])
            refs[4 * n + a][...] = d
            refs[5 * n + a][...] = m2
            refs[6 * n + a][...] = v2

    specs = [pl.BlockSpec((w.shape[0] // GRID4, w.shape[1]), lambda i: (i, 0)) for w in ws]
    return pl.pallas_call(
        body, name="adamw_big", grid=(GRID4,),
        in_specs=specs * 4, out_specs=specs * 3,
        out_shape=[jax.ShapeDtypeStruct(w.shape, F32) for w in ws] * 3,
        compiler_params=_cp(48),
    )(*ws, *gs, *ms, *vs)


def _adamw_small(ws, gs, ms, vs):
    n = len(ws)

    def body(*refs):
        for a in range(n):
            d, m2, v2 = _adamw_math(refs[a][...], refs[n + a][...], refs[2 * n + a][...], refs[3 * n + a][...])
            refs[4 * n + a][...] = d
            refs[5 * n + a][...] = m2
            refs[6 * n + a][...] = v2

    return pl.pallas_call(
        body, name="adamw_small",
        in_specs=[VMEM_SPEC] * (4 * n), out_specs=[VMEM_SPEC] * (3 * n),
        out_shape=[jax.ShapeDtypeStruct(w.shape, F32) for w in ws] * 3,
        compiler_params=pltpu.CompilerParams(vmem_limit_bytes=40 << 20),
    )(*ws, *gs, *ms, *vs)


def kernel(x, meta_tokens, norm_mix_w, w_in, w_gate_up, b_gate, gla_norm_w, sinks, w_out, norm_ff_w, w_ff1, w_ff2, final_norm_w, loss_target, m_meta_tokens, m_norm_mix_w, m_w_in, m_w_gate_up, m_b_gate, m_gla_norm_w, m_sinks, m_w_out, m_norm_ff_w, m_w_ff1, m_w_ff2, m_final_norm_w, v_meta_tokens, v_norm_mix_w, v_w_in, v_w_gate_up, v_b_gate, v_gla_norm_w, v_sinks, v_w_out, v_norm_ff_w, v_w_ff1, v_w_ff2, v_final_norm_w):
    xi, yi, ci = _place()
    shard = (2 * xi + yi).astype(jnp.int32).reshape(1)
    core = ci.astype(jnp.int32).reshape(1)

    small = jnp.concatenate([meta_tokens, w_gate_up[0], jnp.zeros((NM, 64), F32)], axis=1)
    wt3, g_small = _run_comm(_gather_shards([_bf(w_in[0].T), small], [True, False]), "gather_w_in")
    meta = g_small[:, :, 0:256].transpose(1, 0, 2).reshape(NM, D)
    wgu = g_small[:, :, 256:320].transpose(1, 0, 2).reshape(NM, 256)

    xs, tgt = x[0], loss_target[0]
    t = xs.shape[0]
    wfin = final_norm_w.reshape(1, D)
    metapad = jnp.concatenate([meta, jnp.zeros((TM - NM, D), F32)], axis=0)
    wgu_p = _bf(jnp.concatenate([wgu, jnp.zeros((128 - 16, 256), F32)], axis=0))
    tabs = _rope_tables(t)

    w1s, w2s = _bf(w_ff1[0]), _bf(w_ff2[0])
    proj, (g_out, w1a, w1b) = _proj_fwd(xs, metapad, norm_mix_w, wt3, tabs,
                                        _gather_shards([_bf(w_out[0]), w1s[:HK], w1s[HK:]], [True] * 3))
    (oswa, lse), (w2a, w2b) = _swa_fwd(proj, sinks, t, _gather_shards([w2s[:HK], w2s[HK:]], [True] * 2))
    (ogla, oraw, sst), _ = _gla_fwd(proj, wgu_p, b_gate, gla_norm_w, t)
    wo, w1, w2 = g_out.reshape(D, D), (w1a, w1b), (w2a, w2b)
    h1, f, a, dh2, loss, gfin = _mlp_fwd(xs, metapad, tgt, ogla, oswa, wo, norm_ff_w, w1, w2, wfin)

    da, dh2b, dh1, do, dwo, gff = _mlp_bwd(h1, a, dh2, ogla, oswa, wo, norm_ff_w, w1, w2)
    dw1, dw2 = _ffn_wgrad(f, a, da, dh2b)
    big = [dwo.reshape(4, 2, 128, D).transpose(1, 0, 2, 3), dw1, dw2]
    (dsq, dsk, dsv, dsink), theirs = _swa_bwd(proj, sinks, lse, do, t, _swap_halves(big))
    sums, sums_bf = _sum_parts(core, [b.reshape((2, -1) + b.shape[3:]) for b in big],
                               [s.reshape((1, -1) + s.shape[2:]) for s in theirs], "sum_cores", True)
    sums = [s.reshape(b.shape[1:]) for s, b in zip(sums, big)]
    sums_bf = [s.reshape(b.shape[1:]) for s, b in zip(sums_bf, big)]
    (dgla, dlr, dwgu, dbg, dgnw), arrived = _gla_bwd(proj, oraw, sst, do, wgu_p, b_gate, gla_norm_w, t,
                                                     _scatter_shards(sums_bf))
    halves, _ = _sum_parts(shard, sums, arrived, "sum_chips", False)
    (gx, gmeta, dwt, gmix), _ = _proj_bwd(xs, metapad, norm_mix_w, wt3, tabs, dgla, dsq, dsk, dsv, dlr, dh1)

    gwt_in, joined = _reduce_w_in(dwt.reshape(4, DIN // 4, D), _join_halves(halves))
    gw_out, gw_1, gw_2 = [j.reshape((-1, j.shape[2])) for j in joined]

    tail = jnp.concatenate([dbg, dgnw, dsink, loss, jnp.zeros((1, D - 256 - 128 - 8 - 1), F32)], axis=1)
    pack = jnp.concatenate([gmeta, gmix, gff, gfin, tail, dwgu[:16].reshape(4, D)], axis=0)
    tot = _allreduce_small(pack)
    g_meta = lax.dynamic_slice_in_dim(tot[0:NM], shard[0] * 256, 256, axis=1)
    g_mix, g_ff, g_fin = tot[16:17], tot[17:18], tot[18]
    g_bg, g_gnw, g_sinks, loss_tot = tot[19:20, 0:256], tot[19:20, 256:384], tot[19:20, 384:392], tot[19, 392]
    g_wgu = lax.dynamic_slice_in_dim(tot[20:24].reshape(NM, 256), shard[0] * 64, 64, axis=1)

    bo = _adamw_big([w_out[0], w_ff1[0], w_ff2[0]], [gw_out, gw_1, gw_2], [m_w_out[0], m_w_ff1[0], m_w_ff2[0]],
                    [v_w_out[0], v_w_ff1[0], v_w_ff2[0]])

    fin2 = lambda a: a.reshape(1, D)
    sw = [meta_tokens, norm_mix_w, w_gate_up[0], b_gate, gla_norm_w, sinks, norm_ff_w, fin2(final_norm_w), w_in[0].T]
    sg = [g_meta, g_mix, g_wgu, g_bg, g_gnw, g_sinks, g_ff, fin2(g_fin), gwt_in]
    sm = [m_meta_tokens, m_norm_mix_w, m_w_gate_up[0], m_b_gate, m_gla_norm_w, m_sinks, m_norm_ff_w, fin2(m_final_norm_w),
          m_w_in[0].T]
    sv = [v_meta_tokens, v_norm_mix_w, v_w_gate_up[0], v_b_gate, v_gla_norm_w, v_sinks, v_norm_ff_w, fin2(v_final_norm_w),
          v_w_in[0].T]
    so = _adamw_small(sw, sg, sm, sv)

    def ordered(small_o, big_o):
        meta_, mix_, wgu_, bg_, gnw_, sinks_, ff_, fin_, wt_ = small_o
        w_out_, w_1_, w_2_ = big_o
        return (meta_, mix_, wt_.T[None], wgu_[None], bg_, gnw_, sinks_, w_out_[None], ff_, w_1_[None], w_2_[None],
                fin_.reshape(D))

    grads = ordered(sg, [gw_out, gw_1, gw_2])
    deltas = ordered(so[0:9], bo[0:3])
    new_m = ordered(so[9:18], bo[3:6])
    new_v = ordered(so[18:27], bo[6:9])
    return (loss_tot, gx[None], *grads, *deltas, *new_m, *new_v)
```

```python
from typing import Callable, NamedTuple

import jax
import jax.numpy as jnp
import numpy as np
from jax import lax
from jax.experimental import pallas as pl
from jax.experimental.pallas import tpu as pltpu

F32 = jnp.float32
BF16 = jnp.bfloat16

D = 1024
DFF = 4096
NM = 16
TM = 256
CH = 64
SB = 128
QB = 2 * SB
EPS = 1e-5
C_GQ, C_GK, C_GV, C_GR, C_SQ, C_SK, C_SV, C_LR, DINP = 0, 256, 512, 1024, 1536, 2048, 2176, 2304, 2432
DIN = 2320
R_LR = 1536
ROPE_THETA = 500000.0
ADAM_LR, ADAM_B1, ADAM_B2, ADAM_EPS, ADAM_WD, ADAM_STEP = 0.001, 0.9, 0.999, 1e-08, 0.01, 10
NEG = -1e30
MESH = pl.DeviceIdType.MESH
VMEM_SPEC = pl.BlockSpec(memory_space=pltpu.VMEM)
ANY_SPEC = pl.BlockSpec(memory_space=pl.ANY)
SMEM_SPEC = pl.BlockSpec(memory_space=pltpu.SMEM)


def _cp(vmem_mb, sem=("arbitrary",)):
    return pltpu.CompilerParams(dimension_semantics=sem, vmem_limit_bytes=vmem_mb << 20)


def _dot(a, b):
    return jnp.dot(a, b, preferred_element_type=F32)


def _dot_nt(a, b):
    return lax.dot_general(a, b, (((1,), (1,)), ((), ())), preferred_element_type=F32)


def _dot_tn(a, b):
    return lax.dot_general(a, b, (((0,), (0,)), ((), ())), preferred_element_type=F32)


def _bf(x):
    return x.astype(BF16)


def _dot3(m01, x):
    x1 = _bf(x)
    r1 = x - x1.astype(F32)
    x2 = _bf(r1)
    x3 = _bf(r1 - x2.astype(F32))
    return _dot(m01, x1) + _dot(m01, x2) + _dot(m01, x3)


def _rms(h):
    rs = lax.rsqrt(jnp.mean(h * h, axis=-1, keepdims=True) + EPS)
    return h * rs, rs


def _rms_bwd(dy, yhat, rs, w):
    dyh = dy * w
    return rs * (dyh - yhat * jnp.mean(dyh * yhat, axis=-1, keepdims=True))


class _Comm(NamedTuple):
    ins: tuple
    outs: tuple
    sems: tuple
    phases: int
    plan: Callable


def _run_phase(fns):
    for fn in fns:
        fn()


def _call(body, name, grid, in_specs, out_specs, out_shape, scratch, params, args, comm=None):
    if comm is None:
        outs = pl.pallas_call(body, name=name, grid=grid, in_specs=in_specs, out_specs=out_specs, out_shape=out_shape,
                              scratch_shapes=scratch, compiler_params=params)(*args)
        return outs, None
    n_in, n_out, n_scr = len(in_specs), len(out_specs), len(scratch)
    ci, co = len(comm.ins), len(comm.outs)
    last = grid[0] - 1
    marks = [0, max(1, last - max(2, (last + 1) // 6))][:comm.phases]

    def wrapped(*refs):
        own_in, c_in = refs[:n_in], refs[n_in:n_in + ci]
        refs = refs[n_in + ci:]
        own_out, c_out = refs[:n_out], refs[n_out:n_out + co]
        refs = refs[n_out + co:]
        own_scr, c_sem = refs[:n_scr], refs[n_scr:]
        i = pl.program_id(0)

        for p, mark in enumerate(marks):
            @pl.when(i == mark)
            def _():
                plan = comm.plan(c_in, c_out, c_sem)
                if p > 0:
                    _run_phase(plan[p - 1][1])
                _run_phase(plan[p][0])

        body(*own_in, *own_out, *own_scr)

        @pl.when(i == last)
        def _():
            _run_phase(comm.plan(c_in, c_out, c_sem)[-1][1])

    outs = pl.pallas_call(
        wrapped, name=name, grid=grid, in_specs=list(in_specs) + [ANY_SPEC] * ci, out_specs=list(out_specs) + [ANY_SPEC] * co,
        out_shape=list(out_shape) + list(comm.outs), scratch_shapes=list(scratch) + list(comm.sems), compiler_params=params,
    )(*args, *comm.ins)
    return outs[:n_out], outs[n_out:]


def _run_comm(comm, name):
    ci, co = len(comm.ins), len(comm.outs)

    def body(*refs):
        for starts, waits in comm.plan(refs[:ci], refs[ci:ci + co], refs[ci + co:]):
            _run_phase(starts)
            _run_phase(waits)

    return pl.pallas_call(body, name=name, in_specs=[ANY_SPEC] * ci, out_specs=[ANY_SPEC] * co, out_shape=list(comm.outs),
                          scratch_shapes=list(comm.sems))(*comm.ins)


def _join_shards(w3_ref, w_ref):
    for s in range(4):
        w_ref[(DIN // 4) * s:(DIN // 4) * (s + 1), :] = w3_ref[s]


def _proj_fwd(x, metapad, wm, wt3, tabs, comm=None):
    t = x.shape[0]
    nblk = t // TM

    def body(x_ref, mp_ref, wm_ref, w3_ref, tab_ref, proj_ref, w_ref):
        i = pl.program_id(0)

        @pl.when(i == 0)
        def _():
            _join_shards(w3_ref, w_ref)

        h = jnp.where(i == nblk, mp_ref[...], x_ref[...])
        u, _ = _rms(h)
        ub = _bf(u * wm_ref[...])
        proj_ref[:, 0:C_SQ] = _dot_nt(ub, w_ref[0:R_LR, :])
        att = _dot_nt(ub, w_ref[R_LR + 16:DIN, :])
        tab = tab_ref[...]
        proj_ref[:, C_SQ:C_SK] = _rope(att[:, 0:512], tab, 1.0) * 0.125
        proj_ref[:, C_SK:C_SV] = _rope(att[:, 512:640], tab, 1.0)
        proj_ref[:, C_SV:C_LR] = att[:, 640:768]
        proj_ref[:, C_LR:DINP] = jnp.zeros((TM, DINP - C_LR), F32)
        proj_ref[:, C_LR:C_LR + 16] = _dot_nt(ub, w_ref[R_LR:R_LR + 16, :])

    (proj,), got = _call(
        body, "proj_fwd", (nblk + 1,),
        [pl.BlockSpec((TM, D), lambda i: (jnp.minimum(i, nblk - 1), 0)), VMEM_SPEC, VMEM_SPEC, VMEM_SPEC,
         pl.BlockSpec((TM, 128), lambda i: (i, 0))],
        [pl.BlockSpec((TM, DINP), lambda i: (i, 0))], [jax.ShapeDtypeStruct((t + TM, DINP), F32)],
        [pltpu.VMEM((DIN, D), BF16)], _cp(48), (x, metapad, wm, wt3, tabs), comm)
    return proj, got


def _chunk_masks():
    r = lax.broadcasted_iota(jnp.int32, (TM, TM), 0)
    c = lax.broadcasted_iota(jnp.int32, (TM, TM), 1)
    same = (r // CH) == (c // CH)
    lower = _bf(jnp.where(same & (c <= r), 1.0, 0.0))
    upper = _bf(jnp.where(same & (c >= r), 1.0, 0.0))
    return lower, upper


def _gla_gate(lr, wgu, bg, valid, lower):
    z = _dot(_bf(lr), wgu) + bg
    g = (jnp.minimum(z, 0.0) - jnp.log(1.0 + jnp.exp(-jnp.abs(z)))) * (1.0 / 16.0)
    g = jnp.where(valid, g, 0.0)
    return z, _dot3(lower, g)


def _gla_decays(q, k, b):
    nc = TM // CH
    b3 = b.reshape(nc, CH, 256)
    blast = b3[:, CH - 1:CH, :]
    eb = jnp.exp(b)
    enb = jnp.exp(-b)
    ebl = jnp.exp(blast - b3).reshape(TM, 256)
    return eb, enb, ebl, jnp.exp(blast)


def _tri(lower_incl):
    r = lax.broadcasted_iota(jnp.int32, (CH, CH), 0)
    c = lax.broadcasted_iota(jnp.int32, (CH, CH), 1)
    return ((c <= r) if lower_incl else (c >= r))[None]


def _gla_fwd(proj, wgu, bg, gnw, t, comm=None):
    nblk = t // TM
    nt = nblk + 1
    nc = TM // CH

    def blk(i):
        return (i + nblk) % nt

    def body(q_ref, k_ref, v_ref, r_ref, lr_ref, wgu_ref, bg_ref, gnw_ref, o_ref, oraw_ref, sst_ref, b_ref, dgate_ref,
             st_scr):
        i = pl.program_id(0)

        @pl.when(i == 0)
        def _():
            st_scr[...] = jnp.zeros_like(st_scr)

        rows = blk(i) * TM + lax.broadcasted_iota(jnp.int32, (TM, 1), 0)
        lower, _ = _chunk_masks()
        valid = rows < t + NM
        z, b = _gla_gate(lr_ref[...], wgu_ref[...], bg_ref[...], valid, lower)
        b_ref[...] = b
        dgate_ref[...] = jnp.where(valid, (1.0 / 16.0) / (1.0 + jnp.exp(z)), 0.0)
        q = q_ref[...]
        k = k_ref[...]
        eb, enb, ebl, eblast = _gla_decays(q, k, b)
        qt = q * 0.125 * eb
        kt = k * enb
        kh = k * ebl
        tril = _tri(True)
        outs = []
        for h in range(4):
            hs = slice(h * CH, (h + 1) * CH)
            qh = _bf(qt[:, hs]).reshape(nc, CH, CH)
            kth = _bf(kt[:, hs]).reshape(nc, CH, CH)
            khh = _bf(kh[:, hs]).reshape(nc, CH, CH)
            vh = _bf(v_ref[:, h * 128:(h + 1) * 128]).reshape(nc, CH, 128)
            a = jnp.einsum('cid,cjd->cij', qh, kth, preferred_element_type=F32)
            a = jnp.where(tril, a, 0.0)
            o = jnp.einsum('cij,cjv->civ', _bf(a), vh, preferred_element_type=F32)
            kv = jnp.einsum('cjv,cjd->cvd', vh, khh, preferred_element_type=F32)
            st = st_scr[h]
            o_inter = []
            for c in range(nc):
                sst_ref[c, h] = st
                o_inter.append(_dot_nt(qh[c], _bf(st)))
                st = st * eblast[c, :, hs] + kv[c]
            st_scr[h] = st
            outs.append((o + jnp.stack(o_inter)).reshape(TM, 128))
        oraw = jnp.concatenate(outs, axis=1)
        oraw_ref[...] = oraw
        gn = gnw_ref[...]
        res = []
        for h in range(4):
            on, _ = _rms(oraw[:, h * 128:(h + 1) * 128])
            r = r_ref[:, h * 128:(h + 1) * 128]
            res.append(on * gn * (r * jax.nn.sigmoid(r)))
        o_ref[...] = _bf(jnp.concatenate(res, axis=1))

    def spec(w, cb):
        return pl.BlockSpec((TM, w), lambda i: (blk(i), cb))

    return _call(
        body, "gla_fwd", (nt,),
        [spec(256, 0), spec(256, 1), spec(512, 1), spec(512, 2), spec(128, C_LR // 128), VMEM_SPEC, VMEM_SPEC, VMEM_SPEC],
        [spec(512, 0), spec(512, 0), pl.BlockSpec((nc, 4, 128, CH), lambda i: (blk(i), 0, 0, 0)), spec(256, 0), spec(256, 0)],
        [jax.ShapeDtypeStruct((t + TM, 512), BF16), jax.ShapeDtypeStruct((t + TM, 512), F32),
         jax.ShapeDtypeStruct((nt * nc, 4, 128, CH), F32), jax.ShapeDtypeStruct((t + TM, 256), F32),
         jax.ShapeDtypeStruct((t + TM, 256), F32)],
        [pltpu.VMEM((4, 128, CH), F32)], _cp(40), (proj, proj, proj, proj, proj, wgu, bg, gnw), comm)


def _gla_bwd(proj, oraw, sst, bcum, dgate, do, wgu, gnw, t, comm=None):
    nblk = t // TM
    nt = nblk + 1
    nc = TM // CH

    def blk(i):
        return (2 * nblk - i) % nt

    def body(q_ref, k_ref, v_ref, r_ref, lr_ref, oraw_ref, sst_ref, b_ref, dgate_ref, do_ref, wgu_ref, gnw_ref,
             dgla_ref, dlr_ref, dwgu_ref, dbg_ref, dgnw_ref, dst_scr):
        i = pl.program_id(0)

        @pl.when(i == 0)
        def _():
            dst_scr[...] = jnp.zeros_like(dst_scr)
            dwgu_ref[...] = jnp.zeros_like(dwgu_ref)
            dbg_ref[...] = jnp.zeros_like(dbg_ref)
            dgnw_ref[...] = jnp.zeros_like(dgnw_ref)

        _, upper = _chunk_masks()
        lr = lr_ref[...]
        b = b_ref[...]
        q = q_ref[...]
        k = k_ref[...]
        eb, enb, ebl, eblast = _gla_decays(q, k, b)
        qt = q * 0.125 * eb
        kt = k * enb
        kh = k * ebl
        gn = gnw_ref[...]
        tril = _tri(True)
        triu = _tri(False)
        dq_l, dk_l, dv_l, dr_l, db_l, ex_l = [], [], [], [], [], []
        dgn = jnp.zeros((1, 128), F32)
        for h in range(4):
            hs = slice(h * CH, (h + 1) * CH)
            vs = slice(h * 128, (h + 1) * 128)
            on, rs = _rms(oraw_ref[:, vs])
            r = r_ref[:, vs]
            sig = jax.nn.sigmoid(r)
            sil = r * sig
            dy = do_ref[:, vs]
            dr_l.append(dy * on * gn * (sig * (1.0 + r * (1.0 - sig))))
            dgn = dgn + jnp.sum(dy * sil * on, axis=0, keepdims=True)
            doraw = _rms_bwd(dy * sil, on, rs, gn)
            qtf = qt[:, hs].reshape(nc, CH, CH)
            ktf = kt[:, hs].reshape(nc, CH, CH)
            khf = kh[:, hs].reshape(nc, CH, CH)
            qh, kth, khh = _bf(qtf), _bf(ktf), _bf(khf)
            vh = _bf(v_ref[:, vs]).reshape(nc, CH, 128)
            doh = _bf(doraw).reshape(nc, CH, 128)
            at = jnp.where(triu, jnp.einsum('cjd,cid->cji', kth, qh, preferred_element_type=F32), 0.0)
            da = jnp.where(tril, jnp.einsum('civ,cjv->cij', doh, vh, preferred_element_type=F32), 0.0)
            dat = jnp.where(triu, jnp.einsum('cjv,civ->cji', vh, doh, preferred_element_type=F32), 0.0)
            dv = jnp.einsum('cji,civ->cjv', _bf(at), doh, preferred_element_type=F32)
            dqt = jnp.einsum('cij,cjd->cid', _bf(da), kth, preferred_element_type=F32)
            dkt = jnp.einsum('cji,cid->cjd', _bf(dat), qh, preferred_element_type=F32)
            gq = jnp.einsum('civ,cid->cvd', doh, qh, preferred_element_type=F32)
            dst = dst_scr[h]
            dsend = [None] * nc
            for c in reversed(range(nc)):
                dsend[c] = dst
                dst = dst * eblast[c, :, hs] + gq[c]
            dst_scr[h] = dst
            dse = jnp.stack(dsend)
            dseb = _bf(dse)
            stf = sst_ref[:, h]
            dqt = dqt + jnp.einsum('civ,cvd->cid', doh, _bf(stf), preferred_element_type=F32)
            dv = dv + jnp.einsum('cjd,cvd->cjv', khh, dseb, preferred_element_type=F32)
            dkh = jnp.einsum('cjv,cvd->cjd', vh, dseb, preferred_element_type=F32)
            extra = (jnp.sum(dkh * khf, axis=1, keepdims=True)
                     + eblast[:, :, hs] * jnp.sum(dse * stf, axis=1, keepdims=True))
            db_l.append((dqt * qtf - dkt * ktf - dkh * khf).reshape(TM, CH))
            ex_l.append(jnp.broadcast_to(extra, (nc, CH, CH)).reshape(TM, CH))
            dq_l.append((dqt.reshape(TM, CH)) * eb[:, hs] * 0.125)
            dk_l.append(dkt.reshape(TM, CH) * enb[:, hs] + dkh.reshape(TM, CH) * ebl[:, hs])
            dv_l.append(dv.reshape(TM, 128))
        dgnw_ref[...] += dgn
        db = jnp.concatenate(db_l, axis=1)
        dg = _dot3(upper, db) + jnp.concatenate(ex_l, axis=1)
        dz = dg * dgate_ref[...]
        dzb = _bf(dz)
        dlr_ref[...] = _bf(_dot_nt(dzb, wgu_ref[...]))
        dwgu_ref[...] += _dot_tn(_bf(lr), dzb)
        dbg_ref[...] += jnp.sum(dz, axis=0, keepdims=True)
        dgla_ref[...] = _bf(jnp.concatenate(dq_l + dk_l + dv_l + dr_l, axis=1))

    def spec(w, cb):
        return pl.BlockSpec((TM, w), lambda i: (blk(i), cb))

    def acc(shape):
        return pl.BlockSpec(shape, lambda i: (0, 0))

    return _call(
        body, "gla_bwd", (nt,),
        [spec(256, 0), spec(256, 1), spec(512, 1), spec(512, 2), spec(128, C_LR // 128), spec(512, 0),
         pl.BlockSpec((nc, 4, 128, CH), lambda i: (blk(i), 0, 0, 0)), spec(256, 0), spec(256, 0), spec(512, 0),
         VMEM_SPEC, VMEM_SPEC],
        [spec(1536, 0), spec(128, 0), acc((128, 256)), acc((1, 256)), acc((1, 128))],
        [jax.ShapeDtypeStruct((t + TM, 1536), BF16), jax.ShapeDtypeStruct((t + TM, 128), BF16),
         jax.ShapeDtypeStruct((128, 256), F32), jax.ShapeDtypeStruct((1, 256), F32), jax.ShapeDtypeStruct((1, 128), F32)],
        [pltpu.VMEM((4, 128, CH), F32)], _cp(48), (proj, proj, proj, proj, proj, oraw, sst, bcum, dgate, do, wgu, gnw), comm)


def _rope_tables(t):
    r = t + TM
    row = np.arange(r)
    pos = np.where(row < t, row + NM, np.where(row < t + NM, row - t, 0)).astype(np.float32)
    inv_freq = (1.0 / (np.float32(ROPE_THETA) ** (np.arange(0, 16, 2, dtype=np.float32) / np.float32(16)))).astype(np.float32)
    ang = (pos[:, None] * inv_freq[None, :]).astype(np.float32)
    cos, sin = np.cos(ang).astype(np.float32), np.sin(ang).astype(np.float32)
    one, zero = np.ones((r, 48), np.float32), np.zeros((r, 48), np.float32)
    return jnp.asarray(np.concatenate([cos, cos, one, -sin, sin, zero], axis=1))


def _rope(x, tab, sign):
    w = x.shape[1]
    rep = w // 64
    c = jnp.concatenate([tab[:, 0:64]] * rep, axis=1)
    s = jnp.concatenate([tab[:, 64:128]] * rep, axis=1)
    lane = lax.rem(lax.broadcasted_iota(jnp.int32, x.shape, 1), 64)
    partner = jnp.where(lane < 8, pltpu.roll(x, w - 8, 1), jnp.where(lane < 16, pltpu.roll(x, 8, 1), 0.0))
    return x * c + sign * (partner * s)


HG_FWD = 1
HB_BWD = 4


def _stack(x, hg):
    w = x.shape[1] // hg
    return x if hg == 1 else jnp.concatenate([x[:, g * w:(g + 1) * w] for g in range(hg)], axis=0)


def _unstack(x, hg):
    return x if hg == 1 else jnp.concatenate([x[g * SB:(g + 1) * SB] for g in range(hg)], axis=1)


def _swa_masks(b, nsb, hg):
    r = lax.rem(lax.broadcasted_iota(jnp.int32, (hg * SB, SB), 0), SB)
    c = lax.broadcasted_iota(jnp.int32, (hg * SB, SB), 1)
    real = b < nsb
    return c <= r, (c > r) & (b > 0) & real, (c < NM) & real


def _swa_specs(nsb):
    def rows(h, w, cb, f):
        return pl.BlockSpec((h, w), lambda i: (f(i), cb))
    pair = lambda i: i
    prev = lambda i: jnp.maximum(2 * i - 1, 0)
    meta = lambda i: nsb
    return rows, pair, prev, meta


def _swa_scores(b, nsb, hg, sink_ref, q, kc, kp, km):
    mc, mp, mm = _swa_masks(b, nsb, hg)
    groups = []
    for kv in range(2):
        ks = slice(kv * 64, (kv + 1) * 64)
        kcb, kpb, kmb = _bf(kc[:, ks]), _bf(kp[:, ks]), _bf(km[:, ks])
        for h0 in range(4 * kv, 4 * kv + 4, hg):
            qg = _bf(_stack(q[:, h0 * 64:(h0 + hg) * 64], hg))
            s_c = jnp.where(mc, _dot_nt(qg, kcb), NEG)
            s_p = jnp.where(mp, _dot_nt(qg, kpb), NEG)
            s_m = jnp.where(mm, _dot_nt(qg, kmb), NEG)
            sink = jnp.concatenate([jnp.full((SB, 1), sink_ref[0, h0 + g], F32) for g in range(hg)], axis=0)
            groups.append((kv, h0, qg, kcb, kpb, kmb, s_c, s_p, s_m, sink))
    return groups


def _swa_fwd(proj, sinks, t, comm=None):
    nsb = t // SB
    r_tot = t + TM
    rows, pair, prev, meta = _swa_specs(nsb)

    def body(sink_ref, q_ref, kc_ref, kp_ref, km_ref, vc_ref, vp_ref, vm_ref, o_ref, lse_ref):
        i = pl.program_id(0)
        km, vm = km_ref[...], vm_ref[...]
        for j in range(2):
            b = 2 * i + j
            rs = slice(j * SB, (j + 1) * SB)
            kp = kp_ref[...] if j == 0 else kc_ref[0:SB, :]
            vp = vp_ref[...] if j == 0 else vc_ref[0:SB, :]
            vc = vc_ref[rs, :]
            o_l, lse_l = [], []
            for kv, h0, qg, kcb, kpb, kmb, s_c, s_p, s_m, sink in _swa_scores(
                    b, nsb, HG_FWD, sink_ref, q_ref[rs, :], kc_ref[rs, :], kp, km):
                ks = slice(kv * 64, (kv + 1) * 64)
                m = jnp.maximum(jnp.max(jnp.maximum(jnp.maximum(s_c, s_p), s_m), -1, keepdims=True), sink)
                p_c, p_p, p_m = jnp.exp(s_c - m), jnp.exp(s_p - m), jnp.exp(s_m - m)
                l = jnp.sum(p_c + p_p + p_m, -1, keepdims=True) + jnp.exp(sink - m)
                o = _dot(_bf(p_c), _bf(vc[:, ks])) + _dot(_bf(p_p), _bf(vp[:, ks])) + _dot(_bf(p_m), _bf(vm[:, ks]))
                o_l.append(_unstack(o * (1.0 / l), HG_FWD))
                lse_l.append(_unstack(m + jnp.log(l), HG_FWD))
            valid = b * SB + lax.broadcasted_iota(jnp.int32, (SB, 1), 0) < t + NM
            o_ref[rs, :] = _bf(jnp.where(valid, jnp.concatenate(o_l, axis=1), 0.0))
            lse_ref[:, rs] = jnp.concatenate(lse_l, axis=1).T

    ck, cv = C_SK // 128, C_SV // 128
    return _call(
        body, "swa_fwd", (r_tot // QB,),
        [SMEM_SPEC, rows(QB, 512, C_SQ // 512, pair),
         rows(QB, 128, ck, pair), rows(SB, 128, ck, prev), rows(SB, 128, ck, meta),
         rows(QB, 128, cv, pair), rows(SB, 128, cv, prev), rows(SB, 128, cv, meta)],
        [rows(QB, 512, 0, pair), pl.BlockSpec((8, QB), lambda i: (0, i))],
        [jax.ShapeDtypeStruct((r_tot, 512), BF16), jax.ShapeDtypeStruct((8, r_tot), F32)],
        [], _cp(32), (sinks, proj, proj, proj, proj, proj, proj, proj), comm)


def _swa_bwd(proj, sinks, lse_t, do, t, comm=None):
    nsb = t // SB
    r_tot = t + TM
    rows, pair, prev, meta = _swa_specs(nsb)
    hb = HB_BWD
    lanes = hb * SB

    def body(sink_ref, q_ref, kc_ref, kp_ref, km_ref, vc_ref, vp_ref, vm_ref, lse_ref, do_ref,
             dq_ref, dk_ref, dv_ref, dsink_ref):
        i = pl.program_id(0)

        @pl.when(i == 0)
        def _():
            dk_ref[...] = jnp.zeros_like(dk_ref)
            dv_ref[...] = jnp.zeros_like(dv_ref)
            dsink_ref[...] = jnp.zeros_like(dsink_ref)

        key = lax.broadcasted_iota(jnp.int32, (SB, lanes), 0)
        qry = lax.rem(lax.broadcasted_iota(jnp.int32, (SB, lanes), 1), SB)
        km, vm = km_ref[...], vm_ref[...]
        dsink_l = []
        for j in range(2):
            b = 2 * i + j
            rs = slice(j * SB, (j + 1) * SB)
            real = b < nsb
            masks = (key <= qry, (key > qry) & (b > 0) & real, (key < NM) & real)
            k3 = (kc_ref[rs, :], kp_ref[...] if j == 0 else kc_ref[0:SB, :], km)
            v3 = (vc_ref[rs, :], vp_ref[...] if j == 0 else vc_ref[0:SB, :], vm)
            zero = jnp.zeros((SB, 64), F32)
            dq_l, ds_blk = [], []
            dk_l, dv_l = [[zero, zero] for _ in range(3)], [[zero, zero] for _ in range(3)]
            for h0 in range(0, 8, hb):
                kv = h0 // 4
                ks, hs = slice(kv * 64, (kv + 1) * 64), slice(h0 * 64, (h0 + hb) * 64)
                qg = _bf(_stack(q_ref[rs, hs], hb))
                dog = _bf(_stack(do_ref[rs, hs], hb))
                lse_row = jnp.concatenate([lse_ref[h:h + 1, rs] for h in range(h0, h0 + hb)], axis=1)
                sink_row = jnp.concatenate([jnp.full((1, SB), sink_ref[0, h], F32) for h in range(h0, h0 + hb)], axis=1)
                kb = [_bf(k[:, ks]) for k in k3]
                vb = [_bf(v[:, ks]) for v in v3]
                p = [jnp.exp(jnp.where(m, _dot_nt(k, qg), NEG) - lse_row) for m, k in zip(masks, kb)]
                dp = [_dot_nt(v, dog) for v in vb]
                delta = jnp.sum(p[0] * dp[0] + p[1] * dp[1] + p[2] * dp[2], axis=0, keepdims=True)
                ds = [_bf(pp * (dd - delta)) for pp, dd in zip(p, dp)]
                dq_t = _dot_tn(kb[0], ds[0]) + _dot_tn(kb[1], ds[1]) + _dot_tn(kb[2], ds[2])
                dq_l.append(_unstack(dq_t.T, hb))
                for x in range(3):
                    dk_l[x][kv] = dk_l[x][kv] + _dot(ds[x], qg)
                    dv_l[x][kv] = dv_l[x][kv] + _dot(_bf(p[x]), dog)
                ds_row = -jnp.exp(sink_row - lse_row) * delta
                ds_blk += [jnp.sum(ds_row[:, g * SB:(g + 1) * SB], axis=1, keepdims=True) for g in range(hb)]
            dsink_l.append(jnp.concatenate(ds_blk, axis=1))
            dq_ref[rs, :] = jnp.concatenate(dq_l, axis=1)
            starts = (pl.multiple_of(b * SB, SB), pl.multiple_of(jnp.maximum(b - 1, 0) * SB, SB), t)
            for x in range(3):
                dk_ref[pl.ds(starts[x], SB), :] += jnp.concatenate(dk_l[x], axis=1)
                dv_ref[pl.ds(starts[x], SB), :] += jnp.concatenate(dv_l[x], axis=1)
        dsink_ref[...] += dsink_l[0] + dsink_l[1]

    ck, cv = C_SK // 128, C_SV // 128
    whole = lambda w: pl.BlockSpec((r_tot, w), lambda i: (0, 0))
    return _call(
        body, "swa_bwd", (r_tot // QB,),
        [SMEM_SPEC, rows(QB, 512, C_SQ // 512, pair),
         rows(QB, 128, ck, pair), rows(SB, 128, ck, prev), rows(SB, 128, ck, meta),
         rows(QB, 128, cv, pair), rows(SB, 128, cv, prev), rows(SB, 128, cv, meta),
         pl.BlockSpec((8, QB), lambda i: (0, i)), rows(QB, 512, 1, pair)],
        [rows(QB, 512, 0, pair), whole(128), whole(128), pl.BlockSpec((1, 8), lambda i: (0, 0))],
        [jax.ShapeDtypeStruct((r_tot, 512), F32), jax.ShapeDtypeStruct((r_tot, 128), F32),
         jax.ShapeDtypeStruct((r_tot, 128), F32), jax.ShapeDtypeStruct((1, 8), F32)],
        [], _cp(48), (sinks, proj, proj, proj, proj, proj, proj, proj, lse_t, do), comm)


HK = D // 2


def _mlp_fwd(x, metapad, tgt, ogla, oswa, wo, wff, w1, w2, wfin):
    t = x.shape[0]
    nblk = t // TM

    def body(x_ref, mp_ref, tgt_ref, og_ref, os_ref, wo_ref, wff_ref, w1a_ref, w1b_ref, w2a_ref, w2b_ref, wfin_ref,
             h1_ref, f_ref, a_ref, dh2_ref, loss_ref, gfin_ref):
        i = pl.program_id(0)

        @pl.when(i == 0)
        def _():
            loss_ref[...] = jnp.zeros_like(loss_ref)
            gfin_ref[...] = jnp.zeros_like(gfin_ref)

        h0 = jnp.where(i == nblk, mp_ref[...], x_ref[...])
        h1 = h0 + _dot(og_ref[...], wo_ref[0:512, :]) + _dot(os_ref[...], wo_ref[512:1024, :])
        h1_ref[...] = h1
        fh, _ = _rms(h1)
        f = _bf(fh * wff_ref[...])
        f_ref[...] = f
        acc = jnp.zeros((TM, D), F32)
        for n in range(4):
            a = _dot(f[:, 0:HK], w1a_ref[n]) + _dot(f[:, HK:D], w1b_ref[n])
            a_ref[:, n * D:(n + 1) * D] = _bf(a)
            zr = jnp.maximum(a, 0.0)
            z = _bf(zr * zr)
            acc = acc + _dot(z[:, 0:HK], w2a_ref[n]) + _dot(z[:, HK:D], w2b_ref[n])
        h2 = h1 + acc
        yh, rs2 = _rms(h2)
        wf = wfin_ref[...]
        real = i < nblk
        e = jnp.where(real, yh * wf - tgt_ref[...], 0.0)
        loss_ref[...] += jnp.sum(jnp.sum(e * e, axis=0, keepdims=True), axis=1, keepdims=True) * (0.5 / D)
        dy = e * (1.0 / D)
        gfin_ref[...] += jnp.sum(dy * yh, axis=0, keepdims=True)
        dh2_ref[...] = _rms_bwd(dy, yh, rs2, wf)

    xs = pl.BlockSpec((TM, D), lambda i: (jnp.minimum(i, nblk - 1), 0))
    rs = lambda w: pl.BlockSpec((TM, w), lambda i: (i, 0))
    r_tot = t + TM
    return pl.pallas_call(
        body, name="mlp_fwd", grid=(nblk + 1,),
        in_specs=[xs, VMEM_SPEC, xs, rs(512), rs(512)] + [VMEM_SPEC] * 7,
        out_specs=[rs(D), rs(D), rs(DFF), rs(D), pl.BlockSpec((1, 1), lambda i: (0, 0)), pl.BlockSpec((1, D), lambda i: (0, 0))],
        out_shape=[jax.ShapeDtypeStruct((r_tot, D), F32), jax.ShapeDtypeStruct((r_tot, D), BF16),
                   jax.ShapeDtypeStruct((r_tot, DFF), BF16), jax.ShapeDtypeStruct((r_tot, D), F32),
                   jax.ShapeDtypeStruct((1, 1), F32), jax.ShapeDtypeStruct((1, D), F32)],
        compiler_params=_cp(56),
    )(x, metapad, tgt, ogla, oswa, wo, wff, *w1, *w2, wfin)


def _mlp_bwd(h1, a, dh2, ogla, oswa, wo, wff, w1, w2):
    r_tot = h1.shape[0]
    nt = r_tot // TM

    def body(h1_ref, a_ref, dh2_ref, og_ref, os_ref, wo_ref, wff_ref, w1a_ref, w1b_ref, w2a_ref, w2b_ref,
             da_ref, dh2b_ref, dh1_ref, do_ref, dwo_ref, gff_ref):
        i = pl.program_id(0)

        @pl.when(i == 0)
        def _():
            dwo_ref[...] = jnp.zeros_like(dwo_ref)
            gff_ref[...] = jnp.zeros_like(gff_ref)

        dh2 = dh2_ref[...]
        dh2b = _bf(dh2)
        dh2b_ref[...] = dh2b
        dfa = jnp.zeros((TM, HK), F32)
        dfb = jnp.zeros((TM, HK), F32)
        for n in range(4):
            dz = jnp.concatenate([_dot_nt(dh2b, w2a_ref[n]), _dot_nt(dh2b, w2b_ref[n])], axis=1)
            da = _bf(dz * (2.0 * jnp.maximum(a_ref[:, n * D:(n + 1) * D].astype(F32), 0.0)))
            da_ref[:, n * D:(n + 1) * D] = da
            dfa = dfa + _dot_nt(da, w1a_ref[n])
            dfb = dfb + _dot_nt(da, w1b_ref[n])
        df = jnp.concatenate([dfa, dfb], axis=1)
        fh, rs1 = _rms(h1_ref[...])
        gff_ref[...] += jnp.sum(df * fh, axis=0, keepdims=True)
        dh1 = dh2 + _rms_bwd(df, fh, rs1, wff_ref[...])
        dh1_ref[...] = dh1
        dh1b = _bf(dh1)
        do_ref[...] = _dot_nt(dh1b, wo_ref[...])
        dwo_ref[0:512, :] += _dot_tn(og_ref[...], dh1b)
        dwo_ref[512:1024, :] += _dot_tn(os_ref[...], dh1b)

    rs = lambda w: pl.BlockSpec((TM, w), lambda i: (i, 0))
    return pl.pallas_call(
        body, name="mlp_bwd", grid=(nt,),
        in_specs=[rs(D), rs(DFF), rs(D), rs(512), rs(512)] + [VMEM_SPEC] * 6,
        out_specs=[rs(DFF), rs(D), rs(D), rs(D), pl.BlockSpec((D, D), lambda i: (0, 0)),
                   pl.BlockSpec((1, D), lambda i: (0, 0))],
        out_shape=[jax.ShapeDtypeStruct((r_tot, DFF), BF16), jax.ShapeDtypeStruct((r_tot, D), BF16),
                   jax.ShapeDtypeStruct((r_tot, D), F32), jax.ShapeDtypeStruct((r_tot, D), F32),
                   jax.ShapeDtypeStruct((D, D), F32), jax.ShapeDtypeStruct((1, D), F32)],
        compiler_params=_cp(56),
    )(h1, a, dh2, ogla, oswa, wo, wff, *w1, *w2)


def _ffn_wgrad(f, a, da, dh2b):
    r_tot = f.shape[0]
    kt = 768 if r_tot % 768 == 0 else TM
    nk = r_tot // kt

    def body(f_ref, a_ref, da_ref, dh2_ref, dw1_ref, dw2_ref, acc1, acc2):
        k = pl.program_id(1)

        @pl.when(k == 0)
        def _():
            acc1[...] = jnp.zeros_like(acc1)
            acc2[...] = jnp.zeros_like(acc2)

        zr = jnp.maximum(a_ref[...], 0.0)
        acc1[...] += _dot_tn(f_ref[...], da_ref[...])
        acc2[...] += _dot_tn(zr * zr, dh2_ref[...])

        @pl.when(k == nk - 1)
        def _():
            for hh in range(2):
                dw1_ref[hh, 0] = acc1[hh * 512:(hh + 1) * 512, :]
                dw2_ref[hh, 0] = acc2[hh * 512:(hh + 1) * 512, :]

    out = pl.BlockSpec((2, 1, 512, D), lambda n, k: (0, n, 0, 0))
    return pl.pallas_call(
        body, name="ffn_wgrad", grid=(4, nk),
        in_specs=[pl.BlockSpec((kt, D), lambda n, k: (k, 0)), pl.BlockSpec((kt, D), lambda n, k: (k, n)),
                  pl.BlockSpec((kt, D), lambda n, k: (k, n)), pl.BlockSpec((kt, D), lambda n, k: (k, 0))],
        out_specs=[out, out],
        out_shape=[jax.ShapeDtypeStruct((2, 4, 512, D), F32)] * 2,
        scratch_shapes=[pltpu.VMEM((D, D), F32), pltpu.VMEM((D, D), F32)],
        compiler_params=_cp(48, ("arbitrary", "arbitrary")),
    )(f, a, da, dh2b)


def _proj_bwd(x, metapad, wm, wt3, tabs, dgla, dswa_q, dsk, dsv, dlr, dh1, comm=None):
    t = x.shape[0]
    nblk = t // TM

    def body(x_ref, mp_ref, wm_ref, w3_ref, tab_ref, dg_ref, dq_ref, dk_ref, dv_ref, dlr_ref, dh1_ref,
             gx_ref, gmeta_ref, dw_ref, gmix_ref, w_ref, acc):
        i = pl.program_id(0)

        @pl.when(i == 0)
        def _():
            _join_shards(w3_ref, w_ref)
            acc[...] = jnp.zeros_like(acc)
            gmix_ref[...] = jnp.zeros_like(gmix_ref)

        h = jnp.where(i == nblk, mp_ref[...], x_ref[...])
        uh, rs = _rms(h)
        wm_v = wm_ref[...]
        u = _bf(uh * wm_v)
        tab = tab_ref[...]
        dq = _bf(_rope(dq_ref[...] * 0.125, tab, -1.0))
        dk = _bf(_rope(dk_ref[...], tab, -1.0))
        parts = ((dg_ref[...], 0, R_LR), (dlr_ref[:, 0:16], R_LR, 16), (dq, R_LR + 16, 512),
                 (dk, R_LR + 528, 128), (_bf(dv_ref[...]), R_LR + 656, 128))
        du = jnp.zeros((TM, D), F32)
        for val, r0, w in parts:
            du = du + _dot(val, w_ref[r0:r0 + w, :])
            acc[r0:r0 + w, :] += _dot_tn(val, u)
        gmix_ref[...] += jnp.sum(du * uh, axis=0, keepdims=True)
        dh0 = dh1_ref[...] + _rms_bwd(du, uh, rs, wm_v)

        @pl.when(i < nblk)
        def _():
            gx_ref[...] = dh0

        @pl.when(i == nblk)
        def _():
            gmeta_ref[...] = dh0[:NM]
            for s in range(4):
                dw_ref[s] = acc[(DIN // 4) * s:(DIN // 4) * (s + 1), :]

    xs = pl.BlockSpec((TM, D), lambda i: (jnp.minimum(i, nblk - 1), 0))
    rs_ = lambda w: pl.BlockSpec((TM, w), lambda i: (i, 0))
    return _call(
        body, "proj_bwd", (nblk + 1,),
        [xs, VMEM_SPEC, VMEM_SPEC, VMEM_SPEC, rs_(128), rs_(1536), rs_(512), rs_(128), rs_(128), rs_(128), rs_(D)],
        [xs, pl.BlockSpec((NM, D), lambda i: (0, 0)), VMEM_SPEC, pl.BlockSpec((1, D), lambda i: (0, 0))],
        [jax.ShapeDtypeStruct((t, D), F32), jax.ShapeDtypeStruct((NM, D), F32),
         jax.ShapeDtypeStruct((4, DIN // 4, D), F32), jax.ShapeDtypeStruct((1, D), F32)],
        [pltpu.VMEM((DIN, D), BF16), pltpu.VMEM((DIN, D), F32)], _cp(56),
        (x, metapad, wm, wt3, tabs, dgla, dswa_q, dsk, dsv, dlr, dh1), comm)


def _place():
    return lax.axis_index("x"), lax.axis_index("y"), lax.axis_index("c")


def _other_chips(x, y):
    return [(1 - x, y), (x, 1 - y), (1 - x, 1 - y)]


def _dma_sems(*counts):
    return tuple(pltpu.SemaphoreType.DMA((k,)) for k in counts)


def _gather_shards(shards, split):
    n = len(shards)
    two = [a for a in range(n) if split[a]]

    def plan(ins, outs, sems):
        isend, irecv, dsend, drecv, loc = sems
        x, y, c = _place()
        chips = _other_chips(x, y)

        def part(ref, a, half):
            if not split[a]:
                return ref
            w = shards[a].shape[1] // 2
            return ref.at[:, pl.ds(pl.multiple_of(half * w, 128), w)]

        def over_ici(a, k, shard_of):
            tx, ty = chips[k]
            sx, sy = shard_of
            return pltpu.make_async_remote_copy(
                src_ref=part(ins[a], a, c), dst_ref=part(outs[a].at[2 * sx + sy], a, c), send_sem=isend.at[3 * a + k],
                recv_sem=irecv.at[3 * a + k], device_id=(tx, ty, c), device_id_type=MESH)

        def over_d2d(a, k, half):
            tx, ty = chips[k]
            ref = part(outs[a].at[2 * tx + ty], a, half)
            return pltpu.make_async_remote_copy(
                src_ref=ref, dst_ref=ref, send_sem=dsend.at[3 * a + k], recv_sem=drecv.at[3 * a + k],
                device_id=(x, y, 1 - c), device_id_type=MESH)

        def local(a):
            return pltpu.make_async_copy(ins[a], outs[a].at[2 * x + y], loc.at[a])

        pairs = [(a, k) for a in range(n) for k in range(3)]
        first = ([lambda a=a: local(a).start() for a in range(n)]
                 + [lambda a=a, k=k: over_ici(a, k, (x, y)).start() for a, k in pairs],
                 [lambda a=a, k=k: over_ici(a, k, chips[k]).wait_recv() for a, k in pairs]
                 + [lambda a=a, k=k: over_ici(a, k, (x, y)).wait_send() for a, k in pairs]
                 + [lambda a=a: local(a).wait() for a in range(n)])
        pairs2 = [(a, k) for a in two for k in range(3)]
        second = ([lambda a=a, k=k: over_d2d(a, k, c).start() for a, k in pairs2],
                  [lambda a=a, k=k: over_d2d(a, k, 1 - c).wait_recv() for a, k in pairs2]
                  + [lambda a=a, k=k: over_d2d(a, k, c).wait_send() for a, k in pairs2])
        return [first, second] if two else [first]

    return _Comm(tuple(shards), tuple(jax.ShapeDtypeStruct((4,) + s.shape, s.dtype) for s in shards),
                 _dma_sems(3 * n, 3 * n, 3 * n, 3 * n, n), 2 if two else 1, plan)


def _swap_halves(grads):
    n = len(grads)

    def plan(ins, outs, sems):
        send, recv = sems
        x, y, c = _place()

        def swap(a):
            return pltpu.make_async_remote_copy(
                src_ref=ins[a].at[1 - c], dst_ref=outs[a], send_sem=send.at[a], recv_sem=recv.at[a],
                device_id=(x, y, 1 - c), device_id_type=MESH)

        return [([lambda a=a: swap(a).start() for a in range(n)], [lambda a=a: swap(a).wait() for a in range(n)])]

    return _Comm(tuple(grads), tuple(jax.ShapeDtypeStruct(g.shape[1:], g.dtype) for g in grads), _dma_sems(n, n), 1, plan)


def _scatter_shards(parts):
    n = len(parts)

    def plan(ins, outs, sems):
        send, recv = sems
        x, y, c = _place()
        chips = _other_chips(x, y)

        def scatter(a, k):
            tx, ty = chips[k]
            return pltpu.make_async_remote_copy(
                src_ref=ins[a].at[2 * tx + ty], dst_ref=outs[a].at[k], send_sem=send.at[3 * a + k],
                recv_sem=recv.at[3 * a + k], device_id=(tx, ty, c), device_id_type=MESH)

        pairs = [(a, k) for a in range(n) for k in range(3)]
        return [([lambda a=a, k=k: scatter(a, k).start() for a, k in pairs],
                 [lambda a=a, k=k: scatter(a, k).wait() for a, k in pairs])]

    return _Comm(tuple(parts), tuple(jax.ShapeDtypeStruct((3,) + p.shape[1:], p.dtype) for p in parts),
                 _dma_sems(3 * n, 3 * n), 1, plan)


def _join_halves(halves):
    n = len(halves)

    def plan(ins, outs, sems):
        send, recv, loc = sems
        x, y, c = _place()

        def remote(a, half):
            return pltpu.make_async_remote_copy(
                src_ref=ins[a], dst_ref=outs[a].at[half], send_sem=send.at[a], recv_sem=recv.at[a],
                device_id=(x, y, 1 - c), device_id_type=MESH)

        def local(a):
            return pltpu.make_async_copy(ins[a], outs[a].at[c], loc.at[a])

        every = range(n)
        return [([lambda a=a: local(a).start() for a in every] + [lambda a=a: remote(a, c).start() for a in every],
                 [lambda a=a: remote(a, 1 - c).wait_recv() for a in every]
                 + [lambda a=a: remote(a, c).wait_send() for a in every] + [lambda a=a: local(a).wait() for a in every])]

    return _Comm(tuple(halves), tuple(jax.ShapeDtypeStruct((2,) + h.shape, h.dtype) for h in halves),
                 _dma_sems(n, n, n), 1, plan)


def _reduce_w_in(dwt, comm):
    rows, hw = DIN // 4, D // 2
    ci, co = len(comm.ins), len(comm.outs)

    def body(*refs):
        dw_ref, c_in, out_ref, c_out = refs[0], refs[1:1 + ci], refs[1 + ci], refs[2 + ci:2 + ci + co]
        mine, sib, tosend, rbuf, qbuf, full, send, recv, loc = refs[2 + ci + co:11 + ci + co]
        c_sem = refs[11 + ci + co:]
        x, y, c = _place()
        sibling = (x, y, 1 - c)
        (starts, waits), = comm.plan(c_in, c_out, c_sem)
        _run_phase(starts)

        def cols(ref, half):
            window = pl.ds(pl.multiple_of(half * hw, 128), hw)
            return ref.at[:, :, window] if len(ref.shape) == 3 else ref.at[:, window]

        load = pltpu.make_async_copy(cols(dw_ref, c), mine, loc.at[0])
        give = pltpu.make_async_remote_copy(src_ref=cols(dw_ref, 1 - c), dst_ref=sib, send_sem=send.at[3], recv_sem=recv.at[3],
                                            device_id=sibling, device_id_type=MESH)
        load.start()
        give.start()
        load.wait()
        give.wait()
        mine[...] = mine[...] + sib[...]
        cps = []
        for k, (tx, ty) in enumerate(_other_chips(x, y)):
            tosend[k] = _bf(mine[2 * tx + ty])
            cps.append(pltpu.make_async_remote_copy(
                src_ref=tosend.at[k], dst_ref=rbuf.at[k], send_sem=send.at[k], recv_sem=recv.at[k],
                device_id=(tx, ty, c), device_id_type=MESH))
            cps[-1].start()
        for cp in cps:
            cp.wait()
        qbuf[...] = mine[2 * x + y] + rbuf[0].astype(F32) + rbuf[1].astype(F32) + rbuf[2].astype(F32)
        keep = pltpu.make_async_copy(qbuf, cols(full, c), loc.at[1])
        pass_on = pltpu.make_async_remote_copy(src_ref=qbuf, dst_ref=cols(full, c), send_sem=send.at[4], recv_sem=recv.at[4],
                                               device_id=sibling, device_id_type=MESH)
        keep.start()
        pass_on.start()
        keep.wait()
        pass_on.wait_send()
        pltpu.make_async_remote_copy(src_ref=qbuf, dst_ref=cols(full, 1 - c), send_sem=send.at[4], recv_sem=recv.at[4],
                                     device_id=sibling, device_id_type=MESH).wait_recv()
        out_ref[...] = full[...]
        _run_phase(waits)

    outs = pl.pallas_call(
        body, name="reduce_w_in",
        in_specs=[ANY_SPEC] * (1 + ci), out_specs=[VMEM_SPEC] + [ANY_SPEC] * co,
        out_shape=[jax.ShapeDtypeStruct((rows, D), F32)] + list(comm.outs),
        scratch_shapes=[pltpu.VMEM((4, rows, hw), F32), pltpu.VMEM((4, rows, hw), F32), pltpu.VMEM((3, rows, hw), BF16),
                        pltpu.VMEM((3, rows, hw), BF16), pltpu.VMEM((rows, hw), F32), pltpu.VMEM((rows, D), F32),
                        *_dma_sems(5, 5, 2), *comm.sems],
        compiler_params=pltpu.CompilerParams(vmem_limit_bytes=48 << 20),
    )(dwt, *comm.ins)
    return outs[0], outs[1:]


def _allreduce_small(pack):
    p = pack.shape[0]

    def body(in_ref, out_ref, buf, send, recv):
        x, y, c = _place()
        me = 4 * x + 2 * y + c
        buf[me] = in_ref[...]

        def peer_of(k):
            return x ^ (k >> 2), y ^ ((k >> 1) & 1), c ^ (k & 1)

        sends = [pltpu.make_async_remote_copy(
            src_ref=in_ref, dst_ref=buf.at[me], send_sem=send.at[k - 1], recv_sem=recv.at[k - 1],
            device_id=peer_of(k), device_id_type=MESH) for k in range(1, 8)]
        for cp in sends:
            cp.start()
        for k in range(1, 8):
            px, py, pc = peer_of(k)
            pltpu.make_async_remote_copy(
                src_ref=in_ref, dst_ref=buf.at[4 * px + 2 * py + pc], send_sem=send.at[k - 1], recv_sem=recv.at[k - 1],
                device_id=(x, y, c), device_id_type=MESH).wait_recv()
        for cp in sends:
            cp.wait_send()
        acc = buf[0]
        for d in range(1, 8):
            acc = acc + buf[d]
        out_ref[...] = acc

    return pl.pallas_call(
        body, name="allreduce_small",
        in_specs=[VMEM_SPEC], out_specs=VMEM_SPEC, out_shape=jax.ShapeDtypeStruct(pack.shape, F32),
        scratch_shapes=[pltpu.VMEM((8, p, D), F32), *_dma_sems(7, 7)],
    )(pack)


GRID4 = 4


def _sum_parts(sel, firsts, others, name, also_bf16):
    n = len(firsts)
    nk = others[0].shape[0]

    def body(sel_ref, *refs):
        fs, os_, outs = refs[:n], refs[n:2 * n], refs[2 * n:]
        for a in range(n):
            acc = fs[a][0]
            for k in range(nk):
                acc = acc + os_[a][k].astype(F32)
            outs[a][...] = acc
            if also_bf16:
                outs[n + a][...] = _bf(acc)

    def rows(a):
        return firsts[a].shape[1] // GRID4

    in_specs = ([pl.BlockSpec((1, rows(a), firsts[a].shape[2]), lambda i, s: (s[0], i, 0)) for a in range(n)]
                + [pl.BlockSpec((nk, rows(a), firsts[a].shape[2]), lambda i, s: (0, i, 0)) for a in range(n)])
    out_specs = [pl.BlockSpec((rows(a), firsts[a].shape[2]), lambda i, s: (i, 0)) for a in range(n)]
    out_shape = [jax.ShapeDtypeStruct(f.shape[1:], F32) for f in firsts]
    if also_bf16:
        out_specs = out_specs * 2
        out_shape = out_shape + [jax.ShapeDtypeStruct(f.shape[1:], BF16) for f in firsts]
    outs = pl.pallas_call(
        body, name=name,
        grid_spec=pltpu.PrefetchScalarGridSpec(num_scalar_prefetch=1, grid=(GRID4,), in_specs=in_specs, out_specs=out_specs),
        out_shape=out_shape, compiler_params=_cp(48),
    )(sel, *firsts, *others)
    return outs[:n], outs[n:]


def _adamw_math(w, g, m, v):
    m2 = ADAM_B1 * m + (1.0 - ADAM_B1) * g
    v2 = ADAM_B2 * v + (1.0 - ADAM_B2) * (g * g)
    m_hat = m2 / (1.0 - ADAM_B1 ** ADAM_STEP)
    v_hat = v2 / (1.0 - ADAM_B2 ** ADAM_STEP)
    return -ADAM_LR * (m_hat / (jnp.sqrt(v_hat) + ADAM_EPS) + ADAM_WD * w), m2, v2


def _adamw_big(ws, gs, ms, vs):
    n = len(ws)

    def body(*refs):
        for a in range(n):
            d, m2, v2 = _adamw_math(refs[a][...], refs[n + a][...], refs[2 * n + a][...], refs[3 * n + a][...])
            refs[4 * n + a][...] = d
            refs[5 * n + a][...] = m2
            refs[6 * n + a][...] = v2

    specs = [pl.BlockSpec((w.shape[0] // GRID4, w.shape[1]), lambda i: (i, 0)) for w in ws]
    return pl.pallas_call(
        body, name="adamw_big", grid=(GRID4,),
        in_specs=specs * 4, out_specs=specs * 3,
        out_shape=[jax.ShapeDtypeStruct(w.shape, F32) for w in ws] * 3,
        compiler_params=_cp(48),
    )(*ws, *gs, *ms, *vs)


def _adamw_small(ws, gs, ms, vs):
    n = len(ws)

    def body(*refs):
        for a in range(n):
            d, m2, v2 = _adamw_math(refs[a][...], refs[n + a][...], refs[2 * n + a][...], refs[3 * n + a][...])
            refs[4 * n + a][...] = d
            refs[5 * n + a][...] = m2
            refs[6 * n + a][...] = v2

    return pl.pallas_call(
        body, name="adamw_small",
        in_specs=[VMEM_SPEC] * (4 * n), out_specs=[VMEM_SPEC] * (3 * n),
        out_shape=[jax.ShapeDtypeStruct(w.shape, F32) for w in ws] * 3,
        compiler_params=pltpu.CompilerParams(vmem_limit_bytes=40 << 20),
    )(*ws, *gs, *ms, *vs)


def kernel(x, meta_tokens, norm_mix_w, w_in, w_gate_up, b_gate, gla_norm_w, sinks, w_out, norm_ff_w, w_ff1, w_ff2, final_norm_w, loss_target, m_meta_tokens, m_norm_mix_w, m_w_in, m_w_gate_up, m_b_gate, m_gla_norm_w, m_sinks, m_w_out, m_norm_ff_w, m_w_ff1, m_w_ff2, m_final_norm_w, v_meta_tokens, v_norm_mix_w, v_w_in, v_w_gate_up, v_b_gate, v_gla_norm_w, v_sinks, v_w_out, v_norm_ff_w, v_w_ff1, v_w_ff2, v_final_norm_w):
    xi, yi, ci = _place()
    shard = (2 * xi + yi).astype(jnp.int32).reshape(1)
    core = ci.astype(jnp.int32).reshape(1)

    small = jnp.concatenate([meta_tokens, w_gate_up[0], jnp.zeros((NM, 64), F32)], axis=1)
    wt3, g_small = _run_comm(_gather_shards([_bf(w_in[0].T), small], [True, False]), "gather_w_in")
    meta = g_small[:, :, 0:256].transpose(1, 0, 2).reshape(NM, D)
    wgu = g_small[:, :, 256:320].transpose(1, 0, 2).reshape(NM, 256)

    xs, tgt = x[0], loss_target[0]
    t = xs.shape[0]
    wfin = final_norm_w.reshape(1, D)
    metapad = jnp.concatenate([meta, jnp.zeros((TM - NM, D), F32)], axis=0)
    wgu_p = _bf(jnp.concatenate([wgu, jnp.zeros((128 - 16, 256), F32)], axis=0))
    tabs = _rope_tables(t)

    w1s, w2s = _bf(w_ff1[0]), _bf(w_ff2[0])
    proj, (g_out, w1a, w1b) = _proj_fwd(xs, metapad, norm_mix_w, wt3, tabs,
                                        _gather_shards([_bf(w_out[0]), w1s[:HK], w1s[HK:]], [True] * 3))
    (oswa, lse), (w2a, w2b) = _swa_fwd(proj, sinks, t, _gather_shards([w2s[:HK], w2s[HK:]], [True] * 2))
    (ogla, oraw, sst, bcum, dgate), _ = _gla_fwd(proj, wgu_p, b_gate, gla_norm_w, t)
    wo, w1, w2 = g_out.reshape(D, D), (w1a, w1b), (w2a, w2b)
    h1, f, a, dh2, loss, gfin = _mlp_fwd(xs, metapad, tgt, ogla, oswa, wo, norm_ff_w, w1, w2, wfin)

    da, dh2b, dh1, do, dwo, gff = _mlp_bwd(h1, a, dh2, ogla, oswa, wo, norm_ff_w, w1, w2)
    dw1, dw2 = _ffn_wgrad(f, a, da, dh2b)
    big = [dwo.reshape(4, 2, 128, D).transpose(1, 0, 2, 3), dw1, dw2]
    (dsq, dsk, dsv, dsink), theirs = _swa_bwd(proj, sinks, lse, do, t, _swap_halves(big))
    sums, sums_bf = _sum_parts(core, [b.reshape((2, -1) + b.shape[3:]) for b in big],
                               [s.reshape((1, -1) + s.shape[2:]) for s in theirs], "sum_cores", True)
    sums = [s.reshape(b.shape[1:]) for s, b in zip(sums, big)]
    sums_bf = [s.reshape(b.shape[1:]) for s, b in zip(sums_bf, big)]
    (dgla, dlr, dwgu, dbg, dgnw), arrived = _gla_bwd(proj, oraw, sst, bcum, dgate, do, wgu_p, gla_norm_w, t,
                                                     _scatter_shards(sums_bf))
    halves, _ = _sum_parts(shard, sums, arrived, "sum_chips", False)
    (gx, gmeta, dwt, gmix), _ = _proj_bwd(xs, metapad, norm_mix_w, wt3, tabs, dgla, dsq, dsk, dsv, dlr, dh1)

    gwt_in, joined = _reduce_w_in(dwt, _join_halves(halves))
    gw_out, gw_1, gw_2 = [j.reshape((-1, j.shape[2])) for j in joined]

    tail = jnp.concatenate([dbg, dgnw, dsink, loss, jnp.zeros((1, D - 256 - 128 - 8 - 1), F32)], axis=1)
    pack = jnp.concatenate([gmeta, gmix, gff, gfin, tail, dwgu[:16].reshape(4, D)], axis=0)
    tot = _allreduce_small(pack)
    g_meta = lax.dynamic_slice_in_dim(tot[0:NM], shard[0] * 256, 256, axis=1)
    g_mix, g_ff, g_fin = tot[16:17], tot[17:18], tot[18]
    g_bg, g_gnw, g_sinks, loss_tot = tot[19:20, 0:256], tot[19:20, 256:384], tot[19:20, 384:392], tot[19, 392]
    g_wgu = lax.dynamic_slice_in_dim(tot[20:24].reshape(NM, 256), shard[0] * 64, 64, axis=1)

    bo = _adamw_big([w_out[0], w_ff1[0], w_ff2[0]], [gw_out, gw_1, gw_2], [m_w_out[0], m_w_ff1[0], m_w_ff2[0]],
                    [v_w_out[0], v_w_ff1[0], v_w_ff2[0]])

    fin2 = lambda a: a.reshape(1, D)
    sw = [meta_tokens, norm_mix_w, w_gate_up[0], b_gate, gla_norm_w, sinks, norm_ff_w, fin2(final_norm_w), w_in[0].T]
    sg = [g_meta, g_mix, g_wgu, g_bg, g_gnw, g_sinks, g_ff, fin2(g_fin), gwt_in]
    sm = [m_meta_tokens, m_norm_mix_w, m_w_gate_up[0], m_b_gate, m_gla_norm_w, m_sinks, m_norm_ff_w, fin2(m_final_norm_w),
          m_w_in[0].T]
    sv = [v_meta_tokens, v_norm_mix_w, v_w_gate_up[0], v_b_gate, v_gla_norm_w, v_sinks, v_norm_ff_w, fin2(v_final_norm_w),
          v_w_in[0].T]
    so = _adamw_small(sw, sg, sm, sv)

    def ordered(small_o, big_o):
        meta_, mix_, wgu_, bg_, gnw_, sinks_, ff_, fin_, wt_ = small_o
        w_out_, w_1_, w_2_ = big_o
        return (meta_, mix_, wt_.T[None], wgu_[None], bg_, gnw_, sinks_, w_out_[None], ff_, w_1_[None], w_2_[None],
                fin_.reshape(D))

    grads = ordered(sg, [gw_out, gw_1, gw_2])
    deltas = ordered(so[0:9], bo[0:3])
    new_m = ordered(so[9:18], bo[3:6])
    new_v = ordered(so[18:27], bo[6:9])
    return (loss_tot, gx[None], *grads, *deltas, *new_m, *new_v)
```

```python
import functools
from typing import Callable, NamedTuple

import jax
import jax.numpy as jnp
import numpy as np
from jax import lax
from jax.experimental import pallas as pl
from jax.experimental.pallas import tpu as pltpu

F32 = jnp.float32
BF16 = jnp.bfloat16

D = 1024
DFF = 4096
NM = 16
TM = 256
CH = 64
SB = 128
QB = 2 * SB
EPS = 1e-5
C_GQ, C_GK, C_GV, C_GR, C_SQ, C_SK, C_SV, C_LR, DINP = 0, 256, 512, 1024, 1536, 2048, 2176, 2304, 2432
DIN = 2320
R_LR = 1536
ROPE_THETA = 500000.0
ADAM_LR, ADAM_B1, ADAM_B2, ADAM_EPS, ADAM_WD, ADAM_STEP = 0.001, 0.9, 0.999, 1e-08, 0.01, 10
NEG = -1e30
MESH = pl.DeviceIdType.MESH
VMEM_SPEC = pl.BlockSpec(memory_space=pltpu.VMEM)
ANY_SPEC = pl.BlockSpec(memory_space=pl.ANY)
SMEM_SPEC = pl.BlockSpec(memory_space=pltpu.SMEM)


def _cp(vmem_mb, sem=("arbitrary",)):
    return pltpu.CompilerParams(dimension_semantics=sem, vmem_limit_bytes=vmem_mb << 20)


def _dot(a, b):
    return jnp.dot(a, b, preferred_element_type=F32)


def _dot_nt(a, b):
    return lax.dot_general(a, b, (((1,), (1,)), ((), ())), preferred_element_type=F32)


def _dot_tn(a, b):
    return lax.dot_general(a, b, (((0,), (0,)), ((), ())), preferred_element_type=F32)


def _bf(x):
    return x.astype(BF16)


def _dot3(m01, x):
    x1 = _bf(x)
    r1 = x - x1.astype(F32)
    x2 = _bf(r1)
    x3 = _bf(r1 - x2.astype(F32))
    return _dot(m01, x1) + _dot(m01, x2) + _dot(m01, x3)


def _rms(h):
    rs = lax.rsqrt(jnp.mean(h * h, axis=-1, keepdims=True) + EPS)
    return h * rs, rs


def _rms_bwd(dy, yhat, rs, w):
    dyh = dy * w
    return rs * (dyh - yhat * jnp.mean(dyh * yhat, axis=-1, keepdims=True))


class _Comm(NamedTuple):
    ins: tuple
    outs: tuple
    sems: tuple
    phases: int
    plan: Callable


def _run_phase(fns):
    for fn in fns:
        fn()


def _call(body, name, grid, in_specs, out_specs, out_shape, scratch, params, args, comm=None):
    if comm is None:
        outs = pl.pallas_call(body, name=name, grid=grid, in_specs=in_specs, out_specs=out_specs, out_shape=out_shape,
                              scratch_shapes=scratch, compiler_params=params)(*args)
        return outs, None
    n_in, n_out, n_scr = len(in_specs), len(out_specs), len(scratch)
    ci, co = len(comm.ins), len(comm.outs)
    last = grid[0] - 1
    marks = [0, max(1, last - max(2, (last + 1) // 6))][:comm.phases]

    def wrapped(*refs):
        own_in, c_in = refs[:n_in], refs[n_in:n_in + ci]
        refs = refs[n_in + ci:]
        own_out, c_out = refs[:n_out], refs[n_out:n_out + co]
        refs = refs[n_out + co:]
        own_scr, c_sem = refs[:n_scr], refs[n_scr:]
        i = pl.program_id(0)

        for p, mark in enumerate(marks):
            @pl.when(i == mark)
            def _():
                plan = comm.plan(c_in, c_out, c_sem)
                if p > 0:
                    _run_phase(plan[p - 1][1])
                _run_phase(plan[p][0])

        body(*own_in, *own_out, *own_scr)

        @pl.when(i == last)
        def _():
            _run_phase(comm.plan(c_in, c_out, c_sem)[-1][1])

    outs = pl.pallas_call(
        wrapped, name=name, grid=grid, in_specs=list(in_specs) + [ANY_SPEC] * ci, out_specs=list(out_specs) + [ANY_SPEC] * co,
        out_shape=list(out_shape) + list(comm.outs), scratch_shapes=list(scratch) + list(comm.sems), compiler_params=params,
    )(*args, *comm.ins)
    return outs[:n_out], outs[n_out:]


def _run_comm(comm, name):
    ci, co = len(comm.ins), len(comm.outs)

    def body(*refs):
        for starts, waits in comm.plan(refs[:ci], refs[ci:ci + co], refs[ci + co:]):
            _run_phase(starts)
            _run_phase(waits)

    return pl.pallas_call(body, name=name, in_specs=[ANY_SPEC] * ci, out_specs=[ANY_SPEC] * co, out_shape=list(comm.outs),
                          scratch_shapes=list(comm.sems))(*comm.ins)


def _join_shards(w3_ref, w_ref):
    for s in range(4):
        w_ref[(DIN // 4) * s:(DIN // 4) * (s + 1), :] = w3_ref[s]


def _proj_fwd(x, metapad, wm, wt3, tabs, comm=None):
    t = x.shape[0]
    nblk = t // TM

    def body(x_ref, mp_ref, wm_ref, w3_ref, tab_ref, proj_ref, w_ref):
        i = pl.program_id(0)

        @pl.when(i == 0)
        def _():
            _join_shards(w3_ref, w_ref)

        h = jnp.where(i == nblk, mp_ref[...], x_ref[...])
        u, _ = _rms(h)
        ub = _bf(u * wm_ref[...])
        proj_ref[:, 0:C_SQ] = _dot_nt(ub, w_ref[0:R_LR, :])
        att = _dot_nt(ub, w_ref[R_LR + 16:DIN, :])
        tab = tab_ref[...]
        proj_ref[:, C_SQ:C_SK] = _rope(att[:, 0:512], tab, 1.0) * 0.125
        proj_ref[:, C_SK:C_SV] = _rope(att[:, 512:640], tab, 1.0)
        proj_ref[:, C_SV:C_LR] = att[:, 640:768]
        proj_ref[:, C_LR:DINP] = jnp.zeros((TM, DINP - C_LR), F32)
        proj_ref[:, C_LR:C_LR + 16] = _dot_nt(ub, w_ref[R_LR:R_LR + 16, :])

    (proj,), got = _call(
        body, "proj_fwd", (nblk + 1,),
        [pl.BlockSpec((TM, D), lambda i: (jnp.minimum(i, nblk - 1), 0)), VMEM_SPEC, VMEM_SPEC, VMEM_SPEC,
         pl.BlockSpec((TM, 128), lambda i: (i, 0))],
        [pl.BlockSpec((TM, DINP), lambda i: (i, 0))], [jax.ShapeDtypeStruct((t + TM, DINP), F32)],
        [pltpu.VMEM((DIN, D), BF16)], _cp(48), (x, metapad, wm, wt3, tabs), comm)
    return proj, got


def _chunk_masks():
    r = lax.broadcasted_iota(jnp.int32, (TM, TM), 0)
    c = lax.broadcasted_iota(jnp.int32, (TM, TM), 1)
    same = (r // CH) == (c // CH)
    lower = _bf(jnp.where(same & (c <= r), 1.0, 0.0))
    upper = _bf(jnp.where(same & (c >= r), 1.0, 0.0))
    return lower, upper


def _gla_gate(lr, wgu, bg, valid, lower):
    z = _dot(_bf(lr), wgu) + bg
    g = (jnp.minimum(z, 0.0) - jnp.log(1.0 + jnp.exp(-jnp.abs(z)))) * (1.0 / 16.0)
    g = jnp.where(valid, g, 0.0)
    return z, _dot3(lower, g)


def _gla_decays(q, k, b):
    nc = TM // CH
    b3 = b.reshape(nc, CH, 256)
    blast = b3[:, CH - 1:CH, :]
    eb = jnp.exp(b)
    enb = jnp.exp(-b)
    ebl = jnp.exp(blast - b3).reshape(TM, 256)
    return eb, enb, ebl, jnp.exp(blast)


def _tri(lower_incl):
    r = lax.broadcasted_iota(jnp.int32, (CH, CH), 0)
    c = lax.broadcasted_iota(jnp.int32, (CH, CH), 1)
    return ((c <= r) if lower_incl else (c >= r))[None]


def _gla_fwd(proj, wgu, bg, gnw, t, comm=None):
    nblk = t // TM
    nt = nblk + 1
    nc = TM // CH

    def blk(i):
        return (i + nblk) % nt

    def body(q_ref, k_ref, v_ref, r_ref, lr_ref, wgu_ref, bg_ref, gnw_ref, o_ref, oraw_ref, sst_ref, b_ref, dgate_ref,
             st_scr):
        i = pl.program_id(0)

        @pl.when(i == 0)
        def _():
            st_scr[...] = jnp.zeros_like(st_scr)

        rows = blk(i) * TM + lax.broadcasted_iota(jnp.int32, (TM, 1), 0)
        lower, _ = _chunk_masks()
        valid = rows < t + NM
        z, b = _gla_gate(lr_ref[...], wgu_ref[...], bg_ref[...], valid, lower)
        b_ref[...] = b
        dgate_ref[...] = jnp.where(valid, (1.0 / 16.0) / (1.0 + jnp.exp(z)), 0.0)
        q = q_ref[...]
        k = k_ref[...]
        eb, enb, ebl, eblast = _gla_decays(q, k, b)
        qt = q * 0.125 * eb
        kt = k * enb
        kh = k * ebl
        tril = _tri(True)
        heads = range(4)
        hs = [slice(h * CH, (h + 1) * CH) for h in heads]
        qh = [_bf(qt[:, hs[h]]).reshape(nc, CH, CH) for h in heads]
        kth = [_bf(kt[:, hs[h]]).reshape(nc, CH, CH) for h in heads]
        khh = [_bf(kh[:, hs[h]]).reshape(nc, CH, CH) for h in heads]
        vh = [_bf(v_ref[:, h * 128:(h + 1) * 128]).reshape(nc, CH, 128) for h in heads]
        a = [jnp.einsum('cid,cjd->cij', qh[h], kth[h], preferred_element_type=F32) for h in heads]
        kv = [jnp.einsum('cjv,cjd->cvd', vh[h], khh[h], preferred_element_type=F32) for h in heads]
        o = [jnp.einsum('cij,cjv->civ', _bf(jnp.where(tril, a[h], 0.0)), vh[h], preferred_element_type=F32) for h in heads]
        states = []
        for h in heads:
            st = st_scr[h]
            per_chunk = []
            for c in range(nc):
                sst_ref[c, h] = st
                per_chunk.append(_bf(st))
                st = st * eblast[c, :, hs[h]] + kv[h][c]
            st_scr[h] = st
            states.append(per_chunk)
        o_inter = [[_dot_nt(qh[h][c], states[h][c]) for c in range(nc)] for h in heads]
        oraw = jnp.concatenate([(o[h] + jnp.stack(o_inter[h])).reshape(TM, 128) for h in heads], axis=1)
        oraw_ref[...] = oraw
        gn = gnw_ref[...]
        res = []
        for h in range(4):
            on, _ = _rms(oraw[:, h * 128:(h + 1) * 128])
            r = r_ref[:, h * 128:(h + 1) * 128]
            res.append(on * gn * (r * jax.nn.sigmoid(r)))
        o_ref[...] = _bf(jnp.concatenate(res, axis=1))

    def spec(w, cb):
        return pl.BlockSpec((TM, w), lambda i: (blk(i), cb))

    return _call(
        body, "gla_fwd", (nt,),
        [spec(256, 0), spec(256, 1), spec(512, 1), spec(512, 2), spec(128, C_LR // 128), VMEM_SPEC, VMEM_SPEC, VMEM_SPEC],
        [spec(512, 0), spec(512, 0), pl.BlockSpec((nc, 4, 128, CH), lambda i: (blk(i), 0, 0, 0)), spec(256, 0), spec(256, 0)],
        [jax.ShapeDtypeStruct((t + TM, 512), BF16), jax.ShapeDtypeStruct((t + TM, 512), F32),
         jax.ShapeDtypeStruct((nt * nc, 4, 128, CH), F32), jax.ShapeDtypeStruct((t + TM, 256), F32),
         jax.ShapeDtypeStruct((t + TM, 256), F32)],
        [pltpu.VMEM((4, 128, CH), F32)], _cp(40), (proj, proj, proj, proj, proj, wgu, bg, gnw), comm)


def _gla_bwd(proj, oraw, sst, bcum, dgate, do, wgu, gnw, t, comm=None):
    nblk = t // TM
    nt = nblk + 1
    nc = TM // CH

    def blk(i):
        return (2 * nblk - i) % nt

    def body(q_ref, k_ref, v_ref, r_ref, lr_ref, oraw_ref, sst_ref, b_ref, dgate_ref, do_ref, wgu_ref, gnw_ref,
             dgla_ref, dlr_ref, dwgu_ref, dbg_ref, dgnw_ref, dst_scr):
        i = pl.program_id(0)

        @pl.when(i == 0)
        def _():
            dst_scr[...] = jnp.zeros_like(dst_scr)
            dwgu_ref[...] = jnp.zeros_like(dwgu_ref)
            dbg_ref[...] = jnp.zeros_like(dbg_ref)
            dgnw_ref[...] = jnp.zeros_like(dgnw_ref)

        _, upper = _chunk_masks()
        lr = lr_ref[...]
        b = b_ref[...]
        q = q_ref[...]
        k = k_ref[...]
        eb, enb, ebl, eblast = _gla_decays(q, k, b)
        qt = q * 0.125 * eb
        kt = k * enb
        kh = k * ebl
        gn = gnw_ref[...]
        tril = _tri(True)
        triu = _tri(False)
        heads = range(4)
        hs = [slice(h * CH, (h + 1) * CH) for h in heads]
        vs = [slice(h * 128, (h + 1) * 128) for h in heads]
        ein = functools.partial(jnp.einsum, preferred_element_type=F32)
        dr_l, doh = [], []
        dgn = jnp.zeros((1, 128), F32)
        for h in heads:
            on, rs = _rms(oraw_ref[:, vs[h]])
            r = r_ref[:, vs[h]]
            sig = jax.nn.sigmoid(r)
            sil = r * sig
            dy = do_ref[:, vs[h]]
            dr_l.append(dy * on * gn * (sig * (1.0 + r * (1.0 - sig))))
            dgn = dgn + jnp.sum(dy * sil * on, axis=0, keepdims=True)
            doh.append(_bf(_rms_bwd(dy * sil, on, rs, gn)).reshape(nc, CH, 128))
        dgnw_ref[...] += dgn
        qh = [_bf(qt[:, hs[h]]).reshape(nc, CH, CH) for h in heads]
        kth = [_bf(kt[:, hs[h]]).reshape(nc, CH, CH) for h in heads]
        khh = [_bf(kh[:, hs[h]]).reshape(nc, CH, CH) for h in heads]
        vh = [_bf(v_ref[:, vs[h]]).reshape(nc, CH, 128) for h in heads]
        at = [ein('cjd,cid->cji', kth[h], qh[h]) for h in heads]
        da = [ein('civ,cjv->cij', doh[h], vh[h]) for h in heads]
        dat = [ein('cjv,civ->cji', vh[h], doh[h]) for h in heads]
        gq = [ein('civ,cid->cvd', doh[h], qh[h]) for h in heads]
        stf = [sst_ref[:, h] for h in heads]
        dqs = [ein('civ,cvd->cid', doh[h], _bf(stf[h])) for h in heads]
        dv = [ein('cji,civ->cjv', _bf(jnp.where(triu, at[h], 0.0)), doh[h]) for h in heads]
        dqt = [ein('cij,cjd->cid', _bf(jnp.where(tril, da[h], 0.0)), kth[h]) + dqs[h] for h in heads]
        dkt = [ein('cji,cid->cjd', _bf(jnp.where(triu, dat[h], 0.0)), qh[h]) for h in heads]
        dse = []
        for h in heads:
            dst = dst_scr[h]
            dsend = [None] * nc
            for c in reversed(range(nc)):
                dsend[c] = dst
                dst = dst * eblast[c, :, hs[h]] + gq[h][c]
            dst_scr[h] = dst
            dse.append(jnp.stack(dsend))
        dseb = [_bf(d) for d in dse]
        dv = [dv[h] + ein('cjd,cvd->cjv', khh[h], dseb[h]) for h in heads]
        dkh = [ein('cjv,cvd->cjd', vh[h], dseb[h]) for h in heads]
        carried = jnp.concatenate([jnp.sum(dse[h] * stf[h], axis=1, keepdims=True) for h in heads], axis=2)
        wide = lambda parts: jnp.concatenate([p.reshape(TM, CH) for p in parts], axis=1)
        dqt_w, dkt_w, dkh_w = wide(dqt), wide(dkt), wide(dkh)
        dkh_kh = dkh_w * kh
        extra = jnp.sum(dkh_kh.reshape(nc, CH, 256), axis=1, keepdims=True) + eblast * carried
        db = dqt_w * qt - dkt_w * kt - dkh_kh
        dg = _dot3(upper, db) + jnp.broadcast_to(extra, (nc, CH, 256)).reshape(TM, 256)
        dz = dg * dgate_ref[...]
        dzb = _bf(dz)
        dlr_ref[...] = _bf(_dot_nt(dzb, wgu_ref[...]))
        dwgu_ref[...] += _dot_tn(_bf(lr), dzb)
        dbg_ref[...] += jnp.sum(dz, axis=0, keepdims=True)
        dq = dqt_w * eb * 0.125
        dk = dkt_w * enb + dkh_w * ebl
        dgla_ref[...] = _bf(jnp.concatenate([dq, dk] + [d.reshape(TM, 128) for d in dv] + dr_l, axis=1))

    def spec(w, cb):
        return pl.BlockSpec((TM, w), lambda i: (blk(i), cb))

    def acc(shape):
        return pl.BlockSpec(shape, lambda i: (0, 0))

    return _call(
        body, "gla_bwd", (nt,),
        [spec(256, 0), spec(256, 1), spec(512, 1), spec(512, 2), spec(128, C_LR // 128), spec(512, 0),
         pl.BlockSpec((nc, 4, 128, CH), lambda i: (blk(i), 0, 0, 0)), spec(256, 0), spec(256, 0), spec(512, 0),
         VMEM_SPEC, VMEM_SPEC],
        [spec(1536, 0), spec(128, 0), acc((128, 256)), acc((1, 256)), acc((1, 128))],
        [jax.ShapeDtypeStruct((t + TM, 1536), BF16), jax.ShapeDtypeStruct((t + TM, 128), BF16),
         jax.ShapeDtypeStruct((128, 256), F32), jax.ShapeDtypeStruct((1, 256), F32), jax.ShapeDtypeStruct((1, 128), F32)],
        [pltpu.VMEM((4, 128, CH), F32)], _cp(48), (proj, proj, proj, proj, proj, oraw, sst, bcum, dgate, do, wgu, gnw), comm)


def _rope_tables(t):
    r = t + TM
    row = np.arange(r)
    pos = np.where(row < t, row + NM, np.where(row < t + NM, row - t, 0)).astype(np.float32)
    inv_freq = (1.0 / (np.float32(ROPE_THETA) ** (np.arange(0, 16, 2, dtype=np.float32) / np.float32(16)))).astype(np.float32)
    ang = (pos[:, None] * inv_freq[None, :]).astype(np.float32)
    cos, sin = np.cos(ang).astype(np.float32), np.sin(ang).astype(np.float32)
    one, zero = np.ones((r, 48), np.float32), np.zeros((r, 48), np.float32)
    return jnp.asarray(np.concatenate([cos, cos, one, -sin, sin, zero], axis=1))


def _rope(x, tab, sign):
    w = x.shape[1]
    rep = w // 64
    c = jnp.concatenate([tab[:, 0:64]] * rep, axis=1)
    s = jnp.concatenate([tab[:, 64:128]] * rep, axis=1)
    lane = lax.rem(lax.broadcasted_iota(jnp.int32, x.shape, 1), 64)
    partner = jnp.where(lane < 8, pltpu.roll(x, w - 8, 1), jnp.where(lane < 16, pltpu.roll(x, 8, 1), 0.0))
    return x * c + sign * (partner * s)


HG_FWD = 1
HB_BWD = 4


def _stack(x, hg):
    w = x.shape[1] // hg
    return x if hg == 1 else jnp.concatenate([x[:, g * w:(g + 1) * w] for g in range(hg)], axis=0)


def _unstack(x, hg):
    return x if hg == 1 else jnp.concatenate([x[g * SB:(g + 1) * SB] for g in range(hg)], axis=1)


def _swa_masks(b, nsb, hg):
    r = lax.rem(lax.broadcasted_iota(jnp.int32, (hg * SB, SB), 0), SB)
    c = lax.broadcasted_iota(jnp.int32, (hg * SB, SB), 1)
    real = b < nsb
    return c <= r, (c > r) & (b > 0) & real, (c < NM) & real


def _swa_specs(nsb):
    def rows(h, w, cb, f):
        return pl.BlockSpec((h, w), lambda i: (f(i), cb))
    pair = lambda i: i
    prev = lambda i: jnp.maximum(2 * i - 1, 0)
    meta = lambda i: nsb
    return rows, pair, prev, meta


def _swa_scores(b, nsb, hg, sink_ref, q, kc, kp, km):
    mc, mp, mm = _swa_masks(b, nsb, hg)
    groups = []
    for kv in range(2):
        ks = slice(kv * 64, (kv + 1) * 64)
        kcb, kpb, kmb = _bf(kc[:, ks]), _bf(kp[:, ks]), _bf(km[:, ks])
        for h0 in range(4 * kv, 4 * kv + 4, hg):
            qg = _bf(_stack(q[:, h0 * 64:(h0 + hg) * 64], hg))
            s_c = jnp.where(mc, _dot_nt(qg, kcb), NEG)
            s_p = jnp.where(mp, _dot_nt(qg, kpb), NEG)
            s_m = jnp.where(mm, _dot_nt(qg, kmb), NEG)
            sink = jnp.concatenate([jnp.full((SB, 1), sink_ref[0, h0 + g], F32) for g in range(hg)], axis=0)
            groups.append((kv, h0, qg, kcb, kpb, kmb, s_c, s_p, s_m, sink))
    return groups


def _swa_fwd(proj, sinks, t, comm=None):
    nsb = t // SB
    r_tot = t + TM
    rows, pair, prev, meta = _swa_specs(nsb)

    def body(sink_ref, q_ref, kc_ref, kp_ref, km_ref, vc_ref, vp_ref, vm_ref, o_ref, lse_ref):
        i = pl.program_id(0)
        km, vm = km_ref[...], vm_ref[...]
        for j in range(2):
            b = 2 * i + j
            rs = slice(j * SB, (j + 1) * SB)
            kp = kp_ref[...] if j == 0 else kc_ref[0:SB, :]
            vp = vp_ref[...] if j == 0 else vc_ref[0:SB, :]
            vc = vc_ref[rs, :]
            o_l, lse_l = [], []
            for kv, h0, qg, kcb, kpb, kmb, s_c, s_p, s_m, sink in _swa_scores(
                    b, nsb, HG_FWD, sink_ref, q_ref[rs, :], kc_ref[rs, :], kp, km):
                ks = slice(kv * 64, (kv + 1) * 64)
                m = jnp.maximum(jnp.max(jnp.maximum(jnp.maximum(s_c, s_p), s_m), -1, keepdims=True), sink)
                p_c, p_p, p_m = jnp.exp(s_c - m), jnp.exp(s_p - m), jnp.exp(s_m - m)
                l = jnp.sum(p_c + p_p + p_m, -1, keepdims=True) + jnp.exp(sink - m)
                o = _dot(_bf(p_c), _bf(vc[:, ks])) + _dot(_bf(p_p), _bf(vp[:, ks])) + _dot(_bf(p_m), _bf(vm[:, ks]))
                o_l.append(_unstack(o * (1.0 / l), HG_FWD))
                lse_l.append(_unstack(m + jnp.log(l), HG_FWD))
            valid = b * SB + lax.broadcasted_iota(jnp.int32, (SB, 1), 0) < t + NM
            o_ref[rs, :] = _bf(jnp.where(valid, jnp.concatenate(o_l, axis=1), 0.0))
            lse_ref[:, rs] = jnp.concatenate(lse_l, axis=1).T

    ck, cv = C_SK // 128, C_SV // 128
    return _call(
        body, "swa_fwd", (r_tot // QB,),
        [SMEM_SPEC, rows(QB, 512, C_SQ // 512, pair),
         rows(QB, 128, ck, pair), rows(SB, 128, ck, prev), rows(SB, 128, ck, meta),
         rows(QB, 128, cv, pair), rows(SB, 128, cv, prev), rows(SB, 128, cv, meta)],
        [rows(QB, 512, 0, pair), pl.BlockSpec((8, QB), lambda i: (0, i))],
        [jax.ShapeDtypeStruct((r_tot, 512), BF16), jax.ShapeDtypeStruct((8, r_tot), F32)],
        [], _cp(32), (sinks, proj, proj, proj, proj, proj, proj, proj), comm)


def _swa_bwd(proj, sinks, lse_t, do, t, comm=None):
    nsb = t // SB
    r_tot = t + TM
    rows, pair, prev, meta = _swa_specs(nsb)
    hb = HB_BWD
    lanes = hb * SB

    def body(sink_ref, q_ref, kc_ref, kp_ref, km_ref, vc_ref, vp_ref, vm_ref, lse_ref, do_ref,
             dq_ref, dk_ref, dv_ref, dsink_ref):
        i = pl.program_id(0)

        @pl.when(i == 0)
        def _():
            dk_ref[...] = jnp.zeros_like(dk_ref)
            dv_ref[...] = jnp.zeros_like(dv_ref)
            dsink_ref[...] = jnp.zeros_like(dsink_ref)

        key = lax.broadcasted_iota(jnp.int32, (SB, lanes), 0)
        qry = lax.rem(lax.broadcasted_iota(jnp.int32, (SB, lanes), 1), SB)
        km, vm = km_ref[...], vm_ref[...]
        dsink_l = []
        for j in range(2):
            b = 2 * i + j
            rs = slice(j * SB, (j + 1) * SB)
            real = b < nsb
            masks = (key <= qry, (key > qry) & (b > 0) & real, (key < NM) & real)
            k3 = (kc_ref[rs, :], kp_ref[...] if j == 0 else kc_ref[0:SB, :], km)
            v3 = (vc_ref[rs, :], vp_ref[...] if j == 0 else vc_ref[0:SB, :], vm)
            zero = jnp.zeros((SB, 64), F32)
            dq_l, ds_blk = [], []
            dk_l, dv_l = [[zero, zero] for _ in range(3)], [[zero, zero] for _ in range(3)]
            for h0 in range(0, 8, hb):
                kv = h0 // 4
                ks, hs = slice(kv * 64, (kv + 1) * 64), slice(h0 * 64, (h0 + hb) * 64)
                qg = _bf(_stack(q_ref[rs, hs], hb))
                dog = _bf(_stack(do_ref[rs, hs], hb))
                lse_row = jnp.concatenate([lse_ref[h:h + 1, rs] for h in range(h0, h0 + hb)], axis=1)
                sink_row = jnp.concatenate([jnp.full((1, SB), sink_ref[0, h], F32) for h in range(h0, h0 + hb)], axis=1)
                kb = [_bf(k[:, ks]) for k in k3]
                vb = [_bf(v[:, ks]) for v in v3]
                s = [_dot_nt(k, qg) for k in kb]
                dp = [_dot_nt(v, dog) for v in vb]
                p = [jnp.exp(jnp.where(m, sx, NEG) - lse_row) for m, sx in zip(masks, s)]
                delta = jnp.sum(p[0] * dp[0] + p[1] * dp[1] + p[2] * dp[2], axis=0, keepdims=True)
                ds = [_bf(pp * (dd - delta)) for pp, dd in zip(p, dp)]
                dq_t = _dot_tn(kb[0], ds[0]) + _dot_tn(kb[1], ds[1]) + _dot_tn(kb[2], ds[2])
                dq_l.append(_unstack(dq_t.T, hb))
                for x in range(3):
                    dk_l[x][kv] = dk_l[x][kv] + _dot(ds[x], qg)
                    dv_l[x][kv] = dv_l[x][kv] + _dot(_bf(p[x]), dog)
                ds_row = -jnp.exp(sink_row - lse_row) * delta
                ds_blk += [jnp.sum(ds_row[:, g * SB:(g + 1) * SB], axis=1, keepdims=True) for g in range(hb)]
            dsink_l.append(jnp.concatenate(ds_blk, axis=1))
            dq_ref[rs, :] = jnp.concatenate(dq_l, axis=1)
            starts = (pl.multiple_of(b * SB, SB), pl.multiple_of(jnp.maximum(b - 1, 0) * SB, SB), t)
            for x in range(3):
                dk_ref[pl.ds(starts[x], SB), :] += jnp.concatenate(dk_l[x], axis=1)
                dv_ref[pl.ds(starts[x], SB), :] += jnp.concatenate(dv_l[x], axis=1)
        dsink_ref[...] += dsink_l[0] + dsink_l[1]

    ck, cv = C_SK // 128, C_SV // 128
    whole = lambda w: pl.BlockSpec((r_tot, w), lambda i: (0, 0))
    return _call(
        body, "swa_bwd", (r_tot // QB,),
        [SMEM_SPEC, rows(QB, 512, C_SQ // 512, pair),
         rows(QB, 128, ck, pair), rows(SB, 128, ck, prev), rows(SB, 128, ck, meta),
         rows(QB, 128, cv, pair), rows(SB, 128, cv, prev), rows(SB, 128, cv, meta),
         pl.BlockSpec((8, QB), lambda i: (0, i)), rows(QB, 512, 1, pair)],
        [rows(QB, 512, 0, pair), whole(128), whole(128), pl.BlockSpec((1, 8), lambda i: (0, 0))],
        [jax.ShapeDtypeStruct((r_tot, 512), F32), jax.ShapeDtypeStruct((r_tot, 128), F32),
         jax.ShapeDtypeStruct((r_tot, 128), F32), jax.ShapeDtypeStruct((1, 8), F32)],
        [], _cp(48), (sinks, proj, proj, proj, proj, proj, proj, proj, lse_t, do), comm)


HK = D // 2


def _mlp_fwd(x, metapad, tgt, ogla, oswa, wo, wff, w1, w2, wfin):
    t = x.shape[0]
    nblk = t // TM

    def body(x_ref, mp_ref, tgt_ref, og_ref, os_ref, wo_ref, wff_ref, w1a_ref, w1b_ref, w2a_ref, w2b_ref, wfin_ref,
             h1_ref, f_ref, a_ref, dh2_ref, loss_ref, gfin_ref):
        i = pl.program_id(0)

        @pl.when(i == 0)
        def _():
            loss_ref[...] = jnp.zeros_like(loss_ref)
            gfin_ref[...] = jnp.zeros_like(gfin_ref)

        h0 = jnp.where(i == nblk, mp_ref[...], x_ref[...])
        h1 = h0 + _dot(og_ref[...], wo_ref[0:512, :]) + _dot(os_ref[...], wo_ref[512:1024, :])
        h1_ref[...] = h1
        fh, _ = _rms(h1)
        f = _bf(fh * wff_ref[...])
        f_ref[...] = f
        acc = jnp.zeros((TM, D), F32)
        for n in range(4):
            a = _dot(f[:, 0:HK], w1a_ref[n]) + _dot(f[:, HK:D], w1b_ref[n])
            a_ref[:, n * D:(n + 1) * D] = _bf(a)
            zr = jnp.maximum(a, 0.0)
            z = _bf(zr * zr)
            acc = acc + _dot(z[:, 0:HK], w2a_ref[n]) + _dot(z[:, HK:D], w2b_ref[n])
        h2 = h1 + acc
        yh, rs2 = _rms(h2)
        wf = wfin_ref[...]
        real = i < nblk
        e = jnp.where(real, yh * wf - tgt_ref[...], 0.0)
        loss_ref[...] += jnp.sum(jnp.sum(e * e, axis=0, keepdims=True), axis=1, keepdims=True) * (0.5 / D)
        dy = e * (1.0 / D)
        gfin_ref[...] += jnp.sum(dy * yh, axis=0, keepdims=True)
        dh2_ref[...] = _rms_bwd(dy, yh, rs2, wf)

    xs = pl.BlockSpec((TM, D), lambda i: (jnp.minimum(i, nblk - 1), 0))
    rs = lambda w: pl.BlockSpec((TM, w), lambda i: (i, 0))
    r_tot = t + TM
    return pl.pallas_call(
        body, name="mlp_fwd", grid=(nblk + 1,),
        in_specs=[xs, VMEM_SPEC, xs, rs(512), rs(512)] + [VMEM_SPEC] * 7,
        out_specs=[rs(D), rs(D), rs(DFF), rs(D), pl.BlockSpec((1, 1), lambda i: (0, 0)), pl.BlockSpec((1, D), lambda i: (0, 0))],
        out_shape=[jax.ShapeDtypeStruct((r_tot, D), F32), jax.ShapeDtypeStruct((r_tot, D), BF16),
                   jax.ShapeDtypeStruct((r_tot, DFF), BF16), jax.ShapeDtypeStruct((r_tot, D), F32),
                   jax.ShapeDtypeStruct((1, 1), F32), jax.ShapeDtypeStruct((1, D), F32)],
        compiler_params=_cp(56),
    )(x, metapad, tgt, ogla, oswa, wo, wff, *w1, *w2, wfin)


def _mlp_bwd(h1, a, dh2, ogla, oswa, wo, wff, w1, w2):
    r_tot = h1.shape[0]
    nt = r_tot // TM

    def body(h1_ref, a_ref, dh2_ref, og_ref, os_ref, wo_ref, wff_ref, w1a_ref, w1b_ref, w2a_ref, w2b_ref,
             da_ref, dh2b_ref, dh1_ref, do_ref, dwo_ref, gff_ref):
        i = pl.program_id(0)

        @pl.when(i == 0)
        def _():
            dwo_ref[...] = jnp.zeros_like(dwo_ref)
            gff_ref[...] = jnp.zeros_like(gff_ref)

        dh2 = dh2_ref[...]
        dh2b = _bf(dh2)
        dh2b_ref[...] = dh2b
        dfa = jnp.zeros((TM, HK), F32)
        dfb = jnp.zeros((TM, HK), F32)
        for n in range(4):
            dz = jnp.concatenate([_dot_nt(dh2b, w2a_ref[n]), _dot_nt(dh2b, w2b_ref[n])], axis=1)
            da = _bf(dz * (2.0 * jnp.maximum(a_ref[:, n * D:(n + 1) * D].astype(F32), 0.0)))
            da_ref[:, n * D:(n + 1) * D] = da
            dfa = dfa + _dot_nt(da, w1a_ref[n])
            dfb = dfb + _dot_nt(da, w1b_ref[n])
        df = jnp.concatenate([dfa, dfb], axis=1)
        fh, rs1 = _rms(h1_ref[...])
        gff_ref[...] += jnp.sum(df * fh, axis=0, keepdims=True)
        dh1 = dh2 + _rms_bwd(df, fh, rs1, wff_ref[...])
        dh1_ref[...] = dh1
        dh1b = _bf(dh1)
        do_ref[...] = _dot_nt(dh1b, wo_ref[...])
        dwo_ref[0:512, :] += _dot_tn(og_ref[...], dh1b)
        dwo_ref[512:1024, :] += _dot_tn(os_ref[...], dh1b)

    rs = lambda w: pl.BlockSpec((TM, w), lambda i: (i, 0))
    return pl.pallas_call(
        body, name="mlp_bwd", grid=(nt,),
        in_specs=[rs(D), rs(DFF), rs(D), rs(512), rs(512)] + [VMEM_SPEC] * 6,
        out_specs=[rs(DFF), rs(D), rs(D), rs(D), pl.BlockSpec((D, D), lambda i: (0, 0)),
                   pl.BlockSpec((1, D), lambda i: (0, 0))],
        out_shape=[jax.ShapeDtypeStruct((r_tot, DFF), BF16), jax.ShapeDtypeStruct((r_tot, D), BF16),
                   jax.ShapeDtypeStruct((r_tot, D), F32), jax.ShapeDtypeStruct((r_tot, D), F32),
                   jax.ShapeDtypeStruct((D, D), F32), jax.ShapeDtypeStruct((1, D), F32)],
        compiler_params=_cp(56),
    )(h1, a, dh2, ogla, oswa, wo, wff, *w1, *w2)


def _ffn_wgrad(f, a, da, dh2b):
    r_tot = f.shape[0]
    kt = 768 if r_tot % 768 == 0 else TM
    nk = r_tot // kt

    def body(f_ref, a_ref, da_ref, dh2_ref, dw1_ref, dw2_ref, acc1, acc2):
        k = pl.program_id(1)

        @pl.when(k == 0)
        def _():
            acc1[...] = jnp.zeros_like(acc1)
            acc2[...] = jnp.zeros_like(acc2)

        zr = jnp.maximum(a_ref[...], 0.0)
        acc1[...] += _dot_tn(f_ref[...], da_ref[...])
        acc2[...] += _dot_tn(zr * zr, dh2_ref[...])

        @pl.when(k == nk - 1)
        def _():
            for hh in range(2):
                dw1_ref[hh, 0] = acc1[hh * 512:(hh + 1) * 512, :]
                dw2_ref[hh, 0] = acc2[hh * 512:(hh + 1) * 512, :]

    out = pl.BlockSpec((2, 1, 512, D), lambda n, k: (0, n, 0, 0))
    return pl.pallas_call(
        body, name="ffn_wgrad", grid=(4, nk),
        in_specs=[pl.BlockSpec((kt, D), lambda n, k: (k, 0)), pl.BlockSpec((kt, D), lambda n, k: (k, n)),
                  pl.BlockSpec((kt, D), lambda n, k: (k, n)), pl.BlockSpec((kt, D), lambda n, k: (k, 0))],
        out_specs=[out, out],
        out_shape=[jax.ShapeDtypeStruct((2, 4, 512, D), F32)] * 2,
        scratch_shapes=[pltpu.VMEM((D, D), F32), pltpu.VMEM((D, D), F32)],
        compiler_params=_cp(48, ("arbitrary", "arbitrary")),
    )(f, a, da, dh2b)


def _proj_bwd(x, metapad, wm, wt3, tabs, dgla, dswa_q, dsk, dsv, dlr, dh1, comm=None):
    t = x.shape[0]
    nblk = t // TM

    def body(x_ref, mp_ref, wm_ref, w3_ref, tab_ref, dg_ref, dq_ref, dk_ref, dv_ref, dlr_ref, dh1_ref,
             gx_ref, gmeta_ref, dw_ref, gmix_ref, w_ref, acc):
        i = pl.program_id(0)

        @pl.when(i == 0)
        def _():
            _join_shards(w3_ref, w_ref)
            acc[...] = jnp.zeros_like(acc)
            gmix_ref[...] = jnp.zeros_like(gmix_ref)

        h = jnp.where(i == nblk, mp_ref[...], x_ref[...])
        uh, rs = _rms(h)
        wm_v = wm_ref[...]
        u = _bf(uh * wm_v)
        tab = tab_ref[...]
        dq = _bf(_rope(dq_ref[...] * 0.125, tab, -1.0))
        dk = _bf(_rope(dk_ref[...], tab, -1.0))
        parts = ((dg_ref[...], 0, R_LR), (dlr_ref[:, 0:16], R_LR, 16), (dq, R_LR + 16, 512),
                 (dk, R_LR + 528, 128), (_bf(dv_ref[...]), R_LR + 656, 128))
        du = jnp.zeros((TM, D), F32)
        for val, r0, w in parts:
            du = du + _dot(val, w_ref[r0:r0 + w, :])
            acc[r0:r0 + w, :] += _dot_tn(val, u)
        gmix_ref[...] += jnp.sum(du * uh, axis=0, keepdims=True)
        dh0 = dh1_ref[...] + _rms_bwd(du, uh, rs, wm_v)

        @pl.when(i < nblk)
        def _():
            gx_ref[...] = dh0

        @pl.when(i == nblk)
        def _():
            gmeta_ref[...] = dh0[:NM]
            for s in range(4):
                dw_ref[s] = acc[(DIN // 4) * s:(DIN // 4) * (s + 1), :]

    xs = pl.BlockSpec((TM, D), lambda i: (jnp.minimum(i, nblk - 1), 0))
    rs_ = lambda w: pl.BlockSpec((TM, w), lambda i: (i, 0))
    return _call(
        body, "proj_bwd", (nblk + 1,),
        [xs, VMEM_SPEC, VMEM_SPEC, VMEM_SPEC, rs_(128), rs_(1536), rs_(512), rs_(128), rs_(128), rs_(128), rs_(D)],
        [xs, pl.BlockSpec((NM, D), lambda i: (0, 0)), VMEM_SPEC, pl.BlockSpec((1, D), lambda i: (0, 0))],
        [jax.ShapeDtypeStruct((t, D), F32), jax.ShapeDtypeStruct((NM, D), F32),
         jax.ShapeDtypeStruct((4, DIN // 4, D), F32), jax.ShapeDtypeStruct((1, D), F32)],
        [pltpu.VMEM((DIN, D), BF16), pltpu.VMEM((DIN, D), F32)], _cp(56),
        (x, metapad, wm, wt3, tabs, dgla, dswa_q, dsk, dsv, dlr, dh1), comm)


def _place():
    return lax.axis_index("x"), lax.axis_index("y"), lax.axis_index("c")


def _other_chips(x, y):
    return [(1 - x, y), (x, 1 - y), (1 - x, 1 - y)]


def _dma_sems(*counts):
    return tuple(pltpu.SemaphoreType.DMA((k,)) for k in counts)


def _gather_shards(shards, split):
    n = len(shards)
    two = [a for a in range(n) if split[a]]

    def plan(ins, outs, sems):
        isend, irecv, dsend, drecv, loc = sems
        x, y, c = _place()
        chips = _other_chips(x, y)

        def part(ref, a, half):
            if not split[a]:
                return ref
            w = shards[a].shape[1] // 2
            return ref.at[:, pl.ds(pl.multiple_of(half * w, 128), w)]

        def over_ici(a, k, shard_of):
            tx, ty = chips[k]
            sx, sy = shard_of
            return pltpu.make_async_remote_copy(
                src_ref=part(ins[a], a, c), dst_ref=part(outs[a].at[2 * sx + sy], a, c), send_sem=isend.at[3 * a + k],
                recv_sem=irecv.at[3 * a + k], device_id=(tx, ty, c), device_id_type=MESH)

        def over_d2d(a, k, half):
            tx, ty = chips[k]
            ref = part(outs[a].at[2 * tx + ty], a, half)
            return pltpu.make_async_remote_copy(
                src_ref=ref, dst_ref=ref, send_sem=dsend.at[3 * a + k], recv_sem=drecv.at[3 * a + k],
                device_id=(x, y, 1 - c), device_id_type=MESH)

        def local(a):
            return pltpu.make_async_copy(ins[a], outs[a].at[2 * x + y], loc.at[a])

        pairs = [(a, k) for a in range(n) for k in range(3)]
        first = ([lambda a=a: local(a).start() for a in range(n)]
                 + [lambda a=a, k=k: over_ici(a, k, (x, y)).start() for a, k in pairs],
                 [lambda a=a, k=k: over_ici(a, k, chips[k]).wait_recv() for a, k in pairs]
                 + [lambda a=a, k=k: over_ici(a, k, (x, y)).wait_send() for a, k in pairs]
                 + [lambda a=a: local(a).wait() for a in range(n)])
        pairs2 = [(a, k) for a in two for k in range(3)]
        second = ([lambda a=a, k=k: over_d2d(a, k, c).start() for a, k in pairs2],
                  [lambda a=a, k=k: over_d2d(a, k, 1 - c).wait_recv() for a, k in pairs2]
                  + [lambda a=a, k=k: over_d2d(a, k, c).wait_send() for a, k in pairs2])
        return [first, second] if two else [first]

    return _Comm(tuple(shards), tuple(jax.ShapeDtypeStruct((4,) + s.shape, s.dtype) for s in shards),
                 _dma_sems(3 * n, 3 * n, 3 * n, 3 * n, n), 2 if two else 1, plan)


def _swap_halves(grads):
    n = len(grads)

    def plan(ins, outs, sems):
        send, recv = sems
        x, y, c = _place()

        def swap(a):
            return pltpu.make_async_remote_copy(
                src_ref=ins[a].at[1 - c], dst_ref=outs[a], send_sem=send.at[a], recv_sem=recv.at[a],
                device_id=(x, y, 1 - c), device_id_type=MESH)

        return [([lambda a=a: swap(a).start() for a in range(n)], [lambda a=a: swap(a).wait() for a in range(n)])]

    return _Comm(tuple(grads), tuple(jax.ShapeDtypeStruct(g.shape[1:], g.dtype) for g in grads), _dma_sems(n, n), 1, plan)


def _scatter_shards(parts):
    n = len(parts)

    def plan(ins, outs, sems):
        send, recv = sems
        x, y, c = _place()
        chips = _other_chips(x, y)

        def scatter(a, k):
            tx, ty = chips[k]
            return pltpu.make_async_remote_copy(
                src_ref=ins[a].at[2 * tx + ty], dst_ref=outs[a].at[k], send_sem=send.at[3 * a + k],
                recv_sem=recv.at[3 * a + k], device_id=(tx, ty, c), device_id_type=MESH)

        pairs = [(a, k) for a in range(n) for k in range(3)]
        return [([lambda a=a, k=k: scatter(a, k).start() for a, k in pairs],
                 [lambda a=a, k=k: scatter(a, k).wait() for a, k in pairs])]

    return _Comm(tuple(parts), tuple(jax.ShapeDtypeStruct((3,) + p.shape[1:], p.dtype) for p in parts),
                 _dma_sems(3 * n, 3 * n), 1, plan)


def _join_halves(halves):
    n = len(halves)

    def plan(ins, outs, sems):
        send, recv, loc = sems
        x, y, c = _place()

        def remote(a, half):
            return pltpu.make_async_remote_copy(
                src_ref=ins[a], dst_ref=outs[a].at[half], send_sem=send.at[a], recv_sem=recv.at[a],
                device_id=(x, y, 1 - c), device_id_type=MESH)

        def local(a):
            return pltpu.make_async_copy(ins[a], outs[a].at[c], loc.at[a])

        every = range(n)
        return [([lambda a=a: local(a).start() for a in every] + [lambda a=a: remote(a, c).start() for a in every],
                 [lambda a=a: remote(a, 1 - c).wait_recv() for a in every]
                 + [lambda a=a: remote(a, c).wait_send() for a in every] + [lambda a=a: local(a).wait() for a in every])]

    return _Comm(tuple(halves), tuple(jax.ShapeDtypeStruct((2,) + h.shape, h.dtype) for h in halves),
                 _dma_sems(n, n, n), 1, plan)


def _reduce_w_in(dwt, comm):
    rows, hw = DIN // 4, D // 2
    ci, co = len(comm.ins), len(comm.outs)

    def body(*refs):
        dw_ref, c_in, out_ref, c_out = refs[0], refs[1:1 + ci], refs[1 + ci], refs[2 + ci:2 + ci + co]
        mine, sib, tosend, rbuf, qbuf, full, send, recv, loc = refs[2 + ci + co:11 + ci + co]
        c_sem = refs[11 + ci + co:]
        x, y, c = _place()
        sibling = (x, y, 1 - c)
        (starts, waits), = comm.plan(c_in, c_out, c_sem)
        _run_phase(starts)

        def cols(ref, half):
            window = pl.ds(pl.multiple_of(half * hw, 128), hw)
            return ref.at[:, :, window] if len(ref.shape) == 3 else ref.at[:, window]

        load = pltpu.make_async_copy(cols(dw_ref, c), mine, loc.at[0])
        give = pltpu.make_async_remote_copy(src_ref=cols(dw_ref, 1 - c), dst_ref=sib, send_sem=send.at[3], recv_sem=recv.at[3],
                                            device_id=sibling, device_id_type=MESH)
        load.start()
        give.start()
        load.wait()
        give.wait()
        mine[...] = mine[...] + sib[...]
        cps = []
        for k, (tx, ty) in enumerate(_other_chips(x, y)):
            tosend[k] = _bf(mine[2 * tx + ty])
            cps.append(pltpu.make_async_remote_copy(
                src_ref=tosend.at[k], dst_ref=rbuf.at[k], send_sem=send.at[k], recv_sem=recv.at[k],
                device_id=(tx, ty, c), device_id_type=MESH))
            cps[-1].start()
        for cp in cps:
            cp.wait()
        qbuf[...] = mine[2 * x + y] + rbuf[0].astype(F32) + rbuf[1].astype(F32) + rbuf[2].astype(F32)
        keep = pltpu.make_async_copy(qbuf, cols(full, c), loc.at[1])
        pass_on = pltpu.make_async_remote_copy(src_ref=qbuf, dst_ref=cols(full, c), send_sem=send.at[4], recv_sem=recv.at[4],
                                               device_id=sibling, device_id_type=MESH)
        keep.start()
        pass_on.start()
        keep.wait()
        pass_on.wait_send()
        pltpu.make_async_remote_copy(src_ref=qbuf, dst_ref=cols(full, 1 - c), send_sem=send.at[4], recv_sem=recv.at[4],
                                     device_id=sibling, device_id_type=MESH).wait_recv()
        out_ref[...] = full[...]
        _run_phase(waits)

    outs = pl.pallas_call(
        body, name="reduce_w_in",
        in_specs=[ANY_SPEC] * (1 + ci), out_specs=[VMEM_SPEC] + [ANY_SPEC] * co,
        out_shape=[jax.ShapeDtypeStruct((rows, D), F32)] + list(comm.outs),
        scratch_shapes=[pltpu.VMEM((4, rows, hw), F32), pltpu.VMEM((4, rows, hw), F32), pltpu.VMEM((3, rows, hw), BF16),
                        pltpu.VMEM((3, rows, hw), BF16), pltpu.VMEM((rows, hw), F32), pltpu.VMEM((rows, D), F32),
                        *_dma_sems(5, 5, 2), *comm.sems],
        compiler_params=pltpu.CompilerParams(vmem_limit_bytes=48 << 20),
    )(dwt, *comm.ins)
    return outs[0], outs[1:]


def _allreduce_small(pack):
    p = pack.shape[0]

    def body(in_ref, out_ref, buf, send, recv):
        x, y, c = _place()
        me = 4 * x + 2 * y + c
        buf[me] = in_ref[...]

        def peer_of(k):
            return x ^ (k >> 2), y ^ ((k >> 1) & 1), c ^ (k & 1)

        sends = [pltpu.make_async_remote_copy(
            src_ref=in_ref, dst_ref=buf.at[me], send_sem=send.at[k - 1], recv_sem=recv.at[k - 1],
            device_id=peer_of(k), device_id_type=MESH) for k in range(1, 8)]
        for cp in sends:
            cp.start()
        for k in range(1, 8):
            px, py, pc = peer_of(k)
            pltpu.make_async_remote_copy(
                src_ref=in_ref, dst_ref=buf.at[4 * px + 2 * py + pc], send_sem=send.at[k - 1], recv_sem=recv.at[k - 1],
                device_id=(x, y, c), device_id_type=MESH).wait_recv()
        for cp in sends:
            cp.wait_send()
        acc = buf[0]
        for d in range(1, 8):
            acc = acc + buf[d]
        out_ref[...] = acc

    return pl.pallas_call(
        body, name="allreduce_small",
        in_specs=[VMEM_SPEC], out_specs=VMEM_SPEC, out_shape=jax.ShapeDtypeStruct(pack.shape, F32),
        scratch_shapes=[pltpu.VMEM((8, p, D), F32), *_dma_sems(7, 7)],
    )(pack)


GRID4 = 4


def _sum_parts(sel, firsts, others, name, also_bf16):
    n = len(firsts)
    nk = others[0].shape[0]

    def body(sel_ref, *refs):
        fs, os_, outs = refs[:n], refs[n:2 * n], refs[2 * n:]
        for a in range(n):
            acc = fs[a][0]
            for k in range(nk):
                acc = acc + os_[a][k].astype(F32)
            outs[a][...] = acc
            if also_bf16:
                outs[n + a][...] = _bf(acc)

    def rows(a):
        return firsts[a].shape[1] // GRID4

    in_specs = ([pl.BlockSpec((1, rows(a), firsts[a].shape[2]), lambda i, s: (s[0], i, 0)) for a in range(n)]
                + [pl.BlockSpec((nk, rows(a), firsts[a].shape[2]), lambda i, s: (0, i, 0)) for a in range(n)])
    out_specs = [pl.BlockSpec((rows(a), firsts[a].shape[2]), lambda i, s: (i, 0)) for a in range(n)]
    out_shape = [jax.ShapeDtypeStruct(f.shape[1:], F32) for f in firsts]
    if also_bf16:
        out_specs = out_specs * 2
        out_shape = out_shape + [jax.ShapeDtypeStruct(f.shape[1:], BF16) for f in firsts]
    outs = pl.pallas_call(
        body, name=name,
        grid_spec=pltpu.PrefetchScalarGridSpec(num_scalar_prefetch=1, grid=(GRID4,), in_specs=in_specs, out_specs=out_specs),
        out_shape=out_shape, compiler_params=_cp(48),
    )(sel, *firsts, *others)
    return outs[:n], outs[n:]


def _adamw_math(w, g, m, v):
    m2 = ADAM_B1 * m + (1.0 - ADAM_B1) * g
    v2 = ADAM_B2 * v + (1.0 - ADAM_B2) * (g * g)
    m_hat = m2 / (1.0 - ADAM_B1 ** ADAM_STEP)
    v_hat = v2 / (1.0 - ADAM_B2 ** ADAM_STEP)
    return -ADAM_LR * (m_hat / (jnp.sqrt(v_hat) + ADAM_EPS) + ADAM_WD * w), m2, v2


def _adamw_big(ws, gs, ms, vs):
    n = len(ws)

    def body(*refs):
        for a in range(n):
            d, m2, v2 = _adamw_math(refs[a][...], refs[n + a][...], refs[2 * n + a][...], refs[3 * n + a][...])
            refs[4 * n + a][...] = d
            refs[5 * n + a][...] = m2
            refs[6 * n + a][...] = v2

    specs = [pl.BlockSpec((w.shape[0] // GRID4, w.shape[1]), lambda i: (i, 0)) for w in ws]
    return pl.pallas_call(
        body, name="adamw_big", grid=(GRID4,),
        in_specs=specs * 4, out_specs=specs * 3,
        out_shape=[jax.ShapeDtypeStruct(w.shape, F32) for w in ws] * 3,
        compiler_params=_cp(48),
    )(*ws, *gs, *ms, *vs)


def _adamw_small(ws, gs, ms, vs):
    n = len(ws)

    def body(*refs):
        for a in range(n):
            d, m2, v2 = _adamw_math(refs[a][...], refs[n + a][...], refs[2 * n + a][...], refs[3 * n + a][...])
            refs[4 * n + a][...] = d
            refs[5 * n + a][...] = m2
            refs[6 * n + a][...] = v2

    return pl.pallas_call(
        body, name="adamw_small",
        in_specs=[VMEM_SPEC] * (4 * n), out_specs=[VMEM_SPEC] * (3 * n),
        out_shape=[jax.ShapeDtypeStruct(w.shape, F32) for w in ws] * 3,
        compiler_params=pltpu.CompilerParams(vmem_limit_bytes=40 << 20),
    )(*ws, *gs, *ms, *vs)


def kernel(x, meta_tokens, norm_mix_w, w_in, w_gate_up, b_gate, gla_norm_w, sinks, w_out, norm_ff_w, w_ff1, w_ff2, final_norm_w, loss_target, m_meta_tokens, m_norm_mix_w, m_w_in, m_w_gate_up, m_b_gate, m_gla_norm_w, m_sinks, m_w_out, m_norm_ff_w, m_w_ff1, m_w_ff2, m_final_norm_w, v_meta_tokens, v_norm_mix_w, v_w_in, v_w_gate_up, v_b_gate, v_gla_norm_w, v_sinks, v_w_out, v_norm_ff_w, v_w_ff1, v_w_ff2, v_final_norm_w):
    xi, yi, ci = _place()
    shard = (2 * xi + yi).astype(jnp.int32).reshape(1)
    core = ci.astype(jnp.int32).reshape(1)

    small = jnp.concatenate([meta_tokens, w_gate_up[0], jnp.zeros((NM, 64), F32)], axis=1)
    wt3, g_small = _run_comm(_gather_shards([_bf(w_in[0].T), small], [True, False]), "gather_w_in")
    meta = g_small[:, :, 0:256].transpose(1, 0, 2).reshape(NM, D)
    wgu = g_small[:, :, 256:320].transpose(1, 0, 2).reshape(NM, 256)

    xs, tgt = x[0], loss_target[0]
    t = xs.shape[0]
    wfin = final_norm_w.reshape(1, D)
    metapad = jnp.concatenate([meta, jnp.zeros((TM - NM, D), F32)], axis=0)
    wgu_p = _bf(jnp.concatenate([wgu, jnp.zeros((128 - 16, 256), F32)], axis=0))
    tabs = _rope_tables(t)

    w1s, w2s = _bf(w_ff1[0]), _bf(w_ff2[0])
    proj, (g_out, w1a, w1b) = _proj_fwd(xs, metapad, norm_mix_w, wt3, tabs,
                                        _gather_shards([_bf(w_out[0]), w1s[:HK], w1s[HK:]], [True] * 3))
    (oswa, lse), (w2a, w2b) = _swa_fwd(proj, sinks, t, _gather_shards([w2s[:HK], w2s[HK:]], [True] * 2))
    (ogla, oraw, sst, bcum, dgate), _ = _gla_fwd(proj, wgu_p, b_gate, gla_norm_w, t)
    wo, w1, w2 = g_out.reshape(D, D), (w1a, w1b), (w2a, w2b)
    h1, f, a, dh2, loss, gfin = _mlp_fwd(xs, metapad, tgt, ogla, oswa, wo, norm_ff_w, w1, w2, wfin)

    da, dh2b, dh1, do, dwo, gff = _mlp_bwd(h1, a, dh2, ogla, oswa, wo, norm_ff_w, w1, w2)
    dw1, dw2 = _ffn_wgrad(f, a, da, dh2b)
    big = [dwo.reshape(4, 2, 128, D).transpose(1, 0, 2, 3), dw1, dw2]
    (dsq, dsk, dsv, dsink), theirs = _swa_bwd(proj, sinks, lse, do, t, _swap_halves(big))
    sums, sums_bf = _sum_parts(core, [b.reshape((2, -1) + b.shape[3:]) for b in big],
                               [s.reshape((1, -1) + s.shape[2:]) for s in theirs], "sum_cores", True)
    sums = [s.reshape(b.shape[1:]) for s, b in zip(sums, big)]
    sums_bf = [s.reshape(b.shape[1:]) for s, b in zip(sums_bf, big)]
    (dgla, dlr, dwgu, dbg, dgnw), arrived = _gla_bwd(proj, oraw, sst, bcum, dgate, do, wgu_p, gla_norm_w, t,
                                                     _scatter_shards(sums_bf))
    halves, _ = _sum_parts(shard, sums, arrived, "sum_chips", False)
    (gx, gmeta, dwt, gmix), _ = _proj_bwd(xs, metapad, norm_mix_w, wt3, tabs, dgla, dsq, dsk, dsv, dlr, dh1)

    gwt_in, joined = _reduce_w_in(dwt, _join_halves(halves))
    gw_out, gw_1, gw_2 = [j.reshape((-1, j.shape[2])) for j in joined]

    tail = jnp.concatenate([dbg, dgnw, dsink, loss, jnp.zeros((1, D - 256 - 128 - 8 - 1), F32)], axis=1)
    pack = jnp.concatenate([gmeta, gmix, gff, gfin, tail, dwgu[:16].reshape(4, D)], axis=0)
    tot = _allreduce_small(pack)
    g_meta = lax.dynamic_slice_in_dim(tot[0:NM], shard[0] * 256, 256, axis=1)
    g_mix, g_ff, g_fin = tot[16:17], tot[17:18], tot[18]
    g_bg, g_gnw, g_sinks, loss_tot = tot[19:20, 0:256], tot[19:20, 256:384], tot[19:20, 384:392], tot[19, 392]
    g_wgu = lax.dynamic_slice_in_dim(tot[20:24].reshape(NM, 256), shard[0] * 64, 64, axis=1)

    bo = _adamw_big([w_out[0], w_ff1[0], w_ff2[0]], [gw_out, gw_1, gw_2], [m_w_out[0], m_w_ff1[0], m_w_ff2[0]],
                    [v_w_out[0], v_w_ff1[0], v_w_ff2[0]])

    fin2 = lambda a: a.reshape(1, D)
    sw = [meta_tokens, norm_mix_w, w_gate_up[0], b_gate, gla_norm_w, sinks, norm_ff_w, fin2(final_norm_w), w_in[0].T]
    sg = [g_meta, g_mix, g_wgu, g_bg, g_gnw, g_sinks, g_ff, fin2(g_fin), gwt_in]
    sm = [m_meta_tokens, m_norm_mix_w, m_w_gate_up[0], m_b_gate, m_gla_norm_w, m_sinks, m_norm_ff_w, fin2(m_final_norm_w),
          m_w_in[0].T]
    sv = [v_meta_tokens, v_norm_mix_w, v_w_gate_up[0], v_b_gate, v_gla_norm_w, v_sinks, v_norm_ff_w, fin2(v_final_norm_w),
          v_w_in[0].T]
    so = _adamw_small(sw, sg, sm, sv)

    def ordered(small_o, big_o):
        meta_, mix_, wgu_, bg_, gnw_, sinks_, ff_, fin_, wt_ = small_o
        w_out_, w_1_, w_2_ = big_o
        return (meta_, mix_, wt_.T[None], wgu_[None], bg_, gnw_, sinks_, w_out_[None], ff_, w_1_[None], w_2_[None],
                fin_.reshape(D))

    grads = ordered(sg, [gw_out, gw_1, gw_2])
    deltas = ordered(so[0:9], bo[0:3])
    new_m = ordered(so[9:18], bo[3:6])
    new_v = ordered(so[18:27], bo[6:9])
    return (loss_tot, gx[None], *grads, *deltas, *new_m, *new_v)
```

```python
import functools
from typing import Callable, NamedTuple

import jax
import jax.numpy as jnp
import numpy as np
from jax import lax
from jax.experimental import pallas as pl
from jax.experimental.pallas import tpu as pltpu

F32 = jnp.float32
BF16 = jnp.bfloat16

D = 1024
DFF = 4096
NM = 16
TM = 256
CH = 64
SB = 128
QB = 2 * SB
EPS = 1e-5
C_GQ, C_GK, C_GV, C_GR, C_SQ, C_SK, C_SV, C_LR, DINP = 0, 256, 512, 1024, 1536, 2048, 2176, 2304, 2432
DIN = 2320
R_LR = 1536
ROPE_THETA = 500000.0
ADAM_LR, ADAM_B1, ADAM_B2, ADAM_EPS, ADAM_WD, ADAM_STEP = 0.001, 0.9, 0.999, 1e-08, 0.01, 10
NEG = -1e30
MESH = pl.DeviceIdType.MESH
VMEM_SPEC = pl.BlockSpec(memory_space=pltpu.VMEM)
ANY_SPEC = pl.BlockSpec(memory_space=pl.ANY)
SMEM_SPEC = pl.BlockSpec(memory_space=pltpu.SMEM)


def _cp(vmem_mb, sem=("arbitrary",)):
    return pltpu.CompilerParams(dimension_semantics=sem, vmem_limit_bytes=vmem_mb << 20)


def _dot(a, b):
    return jnp.dot(a, b, preferred_element_type=F32)


def _dot_nt(a, b):
    return lax.dot_general(a, b, (((1,), (1,)), ((), ())), preferred_element_type=F32)


def _dot_tn(a, b):
    return lax.dot_general(a, b, (((0,), (0,)), ((), ())), preferred_element_type=F32)


def _bf(x):
    return x.astype(BF16)


def _dot3(m01, x):
    x1 = _bf(x)
    r1 = x - x1.astype(F32)
    x2 = _bf(r1)
    x3 = _bf(r1 - x2.astype(F32))
    return _dot(m01, x1) + _dot(m01, x2) + _dot(m01, x3)


def _rms(h):
    rs = lax.rsqrt(jnp.mean(h * h, axis=-1, keepdims=True) + EPS)
    return h * rs, rs


def _rms_bwd(dy, yhat, rs, w):
    dyh = dy * w
    return rs * (dyh - yhat * jnp.mean(dyh * yhat, axis=-1, keepdims=True))


class _Comm(NamedTuple):
    ins: tuple
    outs: tuple
    sems: tuple
    phases: int
    plan: Callable


def _run_phase(fns):
    for fn in fns:
        fn()


def _call(body, name, grid, in_specs, out_specs, out_shape, scratch, params, args, comm=None):
    if comm is None:
        outs = pl.pallas_call(body, name=name, grid=grid, in_specs=in_specs, out_specs=out_specs, out_shape=out_shape,
                              scratch_shapes=scratch, compiler_params=params)(*args)
        return outs, None
    n_in, n_out, n_scr = len(in_specs), len(out_specs), len(scratch)
    ci, co = len(comm.ins), len(comm.outs)
    last = grid[0] - 1
    marks = [0, max(1, last - max(2, (last + 1) // 6))][:comm.phases]

    def wrapped(*refs):
        own_in, c_in = refs[:n_in], refs[n_in:n_in + ci]
        refs = refs[n_in + ci:]
        own_out, c_out = refs[:n_out], refs[n_out:n_out + co]
        refs = refs[n_out + co:]
        own_scr, c_sem = refs[:n_scr], refs[n_scr:]
        i = pl.program_id(0)

        for p, mark in enumerate(marks):
            @pl.when(i == mark)
            def _():
                plan = comm.plan(c_in, c_out, c_sem)
                if p > 0:
                    _run_phase(plan[p - 1][1])
                _run_phase(plan[p][0])

        body(*own_in, *own_out, *own_scr)

        @pl.when(i == last)
        def _():
            _run_phase(comm.plan(c_in, c_out, c_sem)[-1][1])

    outs = pl.pallas_call(
        wrapped, name=name, grid=grid, in_specs=list(in_specs) + [ANY_SPEC] * ci, out_specs=list(out_specs) + [ANY_SPEC] * co,
        out_shape=list(out_shape) + list(comm.outs), scratch_shapes=list(scratch) + list(comm.sems), compiler_params=params,
    )(*args, *comm.ins)
    return outs[:n_out], outs[n_out:]


def _run_comm(comm, name):
    ci, co = len(comm.ins), len(comm.outs)

    def body(*refs):
        for starts, waits in comm.plan(refs[:ci], refs[ci:ci + co], refs[ci + co:]):
            _run_phase(starts)
            _run_phase(waits)

    return pl.pallas_call(body, name=name, in_specs=[ANY_SPEC] * ci, out_specs=[ANY_SPEC] * co, out_shape=list(comm.outs),
                          scratch_shapes=list(comm.sems))(*comm.ins)


def _join_shards(w3_ref, w_ref):
    for s in range(4):
        w_ref[(DIN // 4) * s:(DIN // 4) * (s + 1), :] = w3_ref[s]


def _proj_fwd(x, metapad, wm, wt3, tabs, comm=None):
    t = x.shape[0]
    nblk = t // TM

    def body(x_ref, mp_ref, wm_ref, w3_ref, tab_ref, proj_ref, w_ref):
        i = pl.program_id(0)

        @pl.when(i == 0)
        def _():
            _join_shards(w3_ref, w_ref)

        h = jnp.where(i == nblk, mp_ref[...], x_ref[...])
        u, _ = _rms(h)
        ub = _bf(u * wm_ref[...])
        proj_ref[:, 0:C_SQ] = _dot_nt(ub, w_ref[0:R_LR, :])
        att = _dot_nt(ub, w_ref[R_LR + 16:DIN, :])
        tab = tab_ref[...]
        proj_ref[:, C_SQ:C_SK] = _rope(att[:, 0:512], tab, 1.0) * 0.125
        proj_ref[:, C_SK:C_SV] = _rope(att[:, 512:640], tab, 1.0)
        proj_ref[:, C_SV:C_LR] = att[:, 640:768]
        proj_ref[:, C_LR:DINP] = jnp.zeros((TM, DINP - C_LR), F32)
        proj_ref[:, C_LR:C_LR + 16] = _dot_nt(ub, w_ref[R_LR:R_LR + 16, :])

    (proj,), got = _call(
        body, "proj_fwd", (nblk + 1,),
        [pl.BlockSpec((TM, D), lambda i: (jnp.minimum(i, nblk - 1), 0)), VMEM_SPEC, VMEM_SPEC, VMEM_SPEC,
         pl.BlockSpec((TM, 128), lambda i: (i, 0))],
        [pl.BlockSpec((TM, DINP), lambda i: (i, 0))], [jax.ShapeDtypeStruct((t + TM, DINP), F32)],
        [pltpu.VMEM((DIN, D), BF16)], _cp(48), (x, metapad, wm, wt3, tabs), comm)
    return proj, got


def _chunk_masks():
    r = lax.broadcasted_iota(jnp.int32, (TM, TM), 0)
    c = lax.broadcasted_iota(jnp.int32, (TM, TM), 1)
    same = (r // CH) == (c // CH)
    lower = _bf(jnp.where(same & (c <= r), 1.0, 0.0))
    upper = _bf(jnp.where(same & (c >= r), 1.0, 0.0))
    return lower, upper


def _gla_gate(lr, wgu, bg, valid, lower):
    z = _dot(_bf(lr), wgu) + bg
    g = (jnp.minimum(z, 0.0) - jnp.log(1.0 + jnp.exp(-jnp.abs(z)))) * (1.0 / 16.0)
    g = jnp.where(valid, g, 0.0)
    return z, _dot3(lower, g)


def _gla_decays(q, k, b):
    nc = TM // CH
    b3 = b.reshape(nc, CH, 256)
    blast = b3[:, CH - 1:CH, :]
    eb = jnp.exp(b)
    enb = jnp.exp(-b)
    ebl = jnp.exp(blast - b3).reshape(TM, 256)
    return eb, enb, ebl, jnp.exp(blast)


def _tri(lower_incl):
    r = lax.broadcasted_iota(jnp.int32, (CH, CH), 0)
    c = lax.broadcasted_iota(jnp.int32, (CH, CH), 1)
    return ((c <= r) if lower_incl else (c >= r))[None]


def _gla_fwd(proj, wgu, bg, gnw, t, comm=None):
    nblk = t // TM
    nt = nblk + 1
    nc = TM // CH

    def blk(i):
        return (i + nblk) % nt

    def body(q_ref, k_ref, v_ref, r_ref, lr_ref, wgu_ref, bg_ref, gnw_ref, o_ref, oraw_ref, sst_ref, b_ref, dgate_ref,
             st_scr):
        i = pl.program_id(0)

        @pl.when(i == 0)
        def _():
            st_scr[...] = jnp.zeros_like(st_scr)

        rows = blk(i) * TM + lax.broadcasted_iota(jnp.int32, (TM, 1), 0)
        lower, _ = _chunk_masks()
        valid = rows < t + NM
        z, b = _gla_gate(lr_ref[...], wgu_ref[...], bg_ref[...], valid, lower)
        b_ref[...] = b
        dgate_ref[...] = jnp.where(valid, (1.0 / 16.0) / (1.0 + jnp.exp(z)), 0.0)
        q = q_ref[...]
        k = k_ref[...]
        eb, enb, ebl, eblast = _gla_decays(q, k, b)
        qt = q * 0.125 * eb
        kt = k * enb
        kh = k * ebl
        tril = _tri(True)
        heads = range(4)
        hs = [slice(h * CH, (h + 1) * CH) for h in heads]
        qh = [_bf(qt[:, hs[h]]).reshape(nc, CH, CH) for h in heads]
        kth = [_bf(kt[:, hs[h]]).reshape(nc, CH, CH) for h in heads]
        khh = [_bf(kh[:, hs[h]]).reshape(nc, CH, CH) for h in heads]
        vh = [_bf(v_ref[:, h * 128:(h + 1) * 128]).reshape(nc, CH, 128) for h in heads]
        a = [jnp.einsum('cid,cjd->cij', qh[h], kth[h], preferred_element_type=F32) for h in heads]
        kv = [jnp.einsum('cjv,cjd->cvd', vh[h], khh[h], preferred_element_type=F32) for h in heads]
        o = [jnp.einsum('cij,cjv->civ', _bf(jnp.where(tril, a[h], 0.0)), vh[h], preferred_element_type=F32) for h in heads]
        states = []
        for h in heads:
            st = st_scr[h]
            per_chunk = []
            for c in range(nc):
                sst_ref[c, h] = st
                per_chunk.append(_bf(st))
                st = st * eblast[c, :, hs[h]] + kv[h][c]
            st_scr[h] = st
            states.append(per_chunk)
        o_inter = [[_dot_nt(qh[h][c], states[h][c]) for c in range(nc)] for h in heads]
        oraw = jnp.concatenate([(o[h] + jnp.stack(o_inter[h])).reshape(TM, 128) for h in heads], axis=1)
        oraw_ref[...] = oraw
        gn = gnw_ref[...]
        res = []
        for h in range(4):
            on, _ = _rms(oraw[:, h * 128:(h + 1) * 128])
            r = r_ref[:, h * 128:(h + 1) * 128]
            res.append(on * gn * (r * jax.nn.sigmoid(r)))
        o_ref[...] = _bf(jnp.concatenate(res, axis=1))

    def spec(w, cb):
        return pl.BlockSpec((TM, w), lambda i: (blk(i), cb))

    return _call(
        body, "gla_fwd", (nt,),
        [spec(256, 0), spec(256, 1), spec(512, 1), spec(512, 2), spec(128, C_LR // 128), VMEM_SPEC, VMEM_SPEC, VMEM_SPEC],
        [spec(512, 0), spec(512, 0), pl.BlockSpec((nc, 4, 128, CH), lambda i: (blk(i), 0, 0, 0)), spec(256, 0), spec(256, 0)],
        [jax.ShapeDtypeStruct((t + TM, 512), BF16), jax.ShapeDtypeStruct((t + TM, 512), F32),
         jax.ShapeDtypeStruct((nt * nc, 4, 128, CH), F32), jax.ShapeDtypeStruct((t + TM, 256), F32),
         jax.ShapeDtypeStruct((t + TM, 256), F32)],
        [pltpu.VMEM((4, 128, CH), F32)], _cp(40), (proj, proj, proj, proj, proj, wgu, bg, gnw), comm)


def _gla_bwd(proj, oraw, sst, bcum, dgate, do, wgu, gnw, t, comm=None):
    nblk = t // TM
    nt = nblk + 1
    nc = TM // CH

    def blk(i):
        return (2 * nblk - i) % nt

    def body(q_ref, k_ref, v_ref, r_ref, lr_ref, oraw_ref, sst_ref, b_ref, dgate_ref, do_ref, wgu_ref, gnw_ref,
             dgla_ref, dlr_ref, dwgu_ref, dbg_ref, dgnw_ref, dst_scr):
        i = pl.program_id(0)

        @pl.when(i == 0)
        def _():
            dst_scr[...] = jnp.zeros_like(dst_scr)
            dwgu_ref[...] = jnp.zeros_like(dwgu_ref)
            dbg_ref[...] = jnp.zeros_like(dbg_ref)
            dgnw_ref[...] = jnp.zeros_like(dgnw_ref)

        _, upper = _chunk_masks()
        lr = lr_ref[...]
        b = b_ref[...]
        q = q_ref[...]
        k = k_ref[...]
        eb, enb, ebl, eblast = _gla_decays(q, k, b)
        qt = q * 0.125 * eb
        kt = k * enb
        kh = k * ebl
        gn = gnw_ref[...]
        tril = _tri(True)
        triu = _tri(False)
        heads = range(4)
        hs = [slice(h * CH, (h + 1) * CH) for h in heads]
        vs = [slice(h * 128, (h + 1) * 128) for h in heads]
        ein = functools.partial(jnp.einsum, preferred_element_type=F32)
        dr_l, doh = [], []
        dgn = jnp.zeros((1, 128), F32)
        for h in heads:
            on, rs = _rms(oraw_ref[:, vs[h]])
            r = r_ref[:, vs[h]]
            sig = jax.nn.sigmoid(r)
            sil = r * sig
            dy = do_ref[:, vs[h]]
            dr_l.append(dy * on * gn * (sig * (1.0 + r * (1.0 - sig))))
            dgn = dgn + jnp.sum(dy * sil * on, axis=0, keepdims=True)
            doh.append(_bf(_rms_bwd(dy * sil, on, rs, gn)).reshape(nc, CH, 128))
        dgnw_ref[...] += dgn
        qh = [_bf(qt[:, hs[h]]).reshape(nc, CH, CH) for h in heads]
        kth = [_bf(kt[:, hs[h]]).reshape(nc, CH, CH) for h in heads]
        khh = [_bf(kh[:, hs[h]]).reshape(nc, CH, CH) for h in heads]
        vh = [_bf(v_ref[:, vs[h]]).reshape(nc, CH, 128) for h in heads]
        at = [ein('cjd,cid->cji', kth[h], qh[h]) for h in heads]
        da = [ein('civ,cjv->cij', doh[h], vh[h]) for h in heads]
        dat = [ein('cjv,civ->cji', vh[h], doh[h]) for h in heads]
        gq = [ein('civ,cid->cvd', doh[h], qh[h]) for h in heads]
        stf = [sst_ref[:, h] for h in heads]
        dqs = [ein('civ,cvd->cid', doh[h], _bf(stf[h])) for h in heads]
        dv = [ein('cji,civ->cjv', _bf(jnp.where(triu, at[h], 0.0)), doh[h]) for h in heads]
        dqt = [ein('cij,cjd->cid', _bf(jnp.where(tril, da[h], 0.0)), kth[h]) + dqs[h] for h in heads]
        dkt = [ein('cji,cid->cjd', _bf(jnp.where(triu, dat[h], 0.0)), qh[h]) for h in heads]
        dse = []
        for h in heads:
            dst = dst_scr[h]
            dsend = [None] * nc
            for c in reversed(range(nc)):
                dsend[c] = dst
                dst = dst * eblast[c, :, hs[h]] + gq[h][c]
            dst_scr[h] = dst
            dse.append(jnp.stack(dsend))
        dseb = [_bf(d) for d in dse]
        dv = [dv[h] + ein('cjd,cvd->cjv', khh[h], dseb[h]) for h in heads]
        dkh = [ein('cjv,cvd->cjd', vh[h], dseb[h]) for h in heads]
        carried = jnp.concatenate([jnp.sum(dse[h] * stf[h], axis=1, keepdims=True) for h in heads], axis=2)
        wide = lambda parts: jnp.concatenate([p.reshape(TM, CH) for p in parts], axis=1)
        dqt_w, dkt_w, dkh_w = wide(dqt), wide(dkt), wide(dkh)
        dkh_kh = dkh_w * kh
        extra = jnp.sum(dkh_kh.reshape(nc, CH, 256), axis=1, keepdims=True) + eblast * carried
        db = dqt_w * qt - dkt_w * kt - dkh_kh
        dg = _dot3(upper, db) + jnp.broadcast_to(extra, (nc, CH, 256)).reshape(TM, 256)
        dz = dg * dgate_ref[...]
        dzb = _bf(dz)
        dlr_ref[...] = _bf(_dot_nt(dzb, wgu_ref[...]))
        dwgu_ref[...] += _dot_tn(_bf(lr), dzb)
        dbg_ref[...] += jnp.sum(dz, axis=0, keepdims=True)
        dq = dqt_w * eb * 0.125
        dk = dkt_w * enb + dkh_w * ebl
        dgla_ref[...] = _bf(jnp.concatenate([dq, dk] + [d.reshape(TM, 128) for d in dv] + dr_l, axis=1))

    def spec(w, cb):
        return pl.BlockSpec((TM, w), lambda i: (blk(i), cb))

    def acc(shape):
        return pl.BlockSpec(shape, lambda i: (0, 0))

    return _call(
        body, "gla_bwd", (nt,),
        [spec(256, 0), spec(256, 1), spec(512, 1), spec(512, 2), spec(128, C_LR // 128), spec(512, 0),
         pl.BlockSpec((nc, 4, 128, CH), lambda i: (blk(i), 0, 0, 0)), spec(256, 0), spec(256, 0), spec(512, 0),
         VMEM_SPEC, VMEM_SPEC],
        [spec(1536, 0), spec(128, 0), acc((128, 256)), acc((1, 256)), acc((1, 128))],
        [jax.ShapeDtypeStruct((t + TM, 1536), BF16), jax.ShapeDtypeStruct((t + TM, 128), BF16),
         jax.ShapeDtypeStruct((128, 256), F32), jax.ShapeDtypeStruct((1, 256), F32), jax.ShapeDtypeStruct((1, 128), F32)],
        [pltpu.VMEM((4, 128, CH), F32)], _cp(48), (proj, proj, proj, proj, proj, oraw, sst, bcum, dgate, do, wgu, gnw), comm)


def _rope_tables(t):
    r = t + TM
    row = np.arange(r)
    pos = np.where(row < t, row + NM, np.where(row < t + NM, row - t, 0)).astype(np.float32)
    inv_freq = (1.0 / (np.float32(ROPE_THETA) ** (np.arange(0, 16, 2, dtype=np.float32) / np.float32(16)))).astype(np.float32)
    ang = (pos[:, None] * inv_freq[None, :]).astype(np.float32)
    cos, sin = np.cos(ang).astype(np.float32), np.sin(ang).astype(np.float32)
    one, zero = np.ones((r, 48), np.float32), np.zeros((r, 48), np.float32)
    return jnp.asarray(np.concatenate([cos, cos, one, -sin, sin, zero], axis=1))


def _rope(x, tab, sign):
    w = x.shape[1]
    rep = w // 64
    c = jnp.concatenate([tab[:, 0:64]] * rep, axis=1)
    s = jnp.concatenate([tab[:, 64:128]] * rep, axis=1)
    lane = lax.rem(lax.broadcasted_iota(jnp.int32, x.shape, 1), 64)
    partner = jnp.where(lane < 8, pltpu.roll(x, w - 8, 1), jnp.where(lane < 16, pltpu.roll(x, 8, 1), 0.0))
    return x * c + sign * (partner * s)


HG_FWD = 1
HB_BWD = 4


def _stack(x, hg):
    w = x.shape[1] // hg
    return x if hg == 1 else jnp.concatenate([x[:, g * w:(g + 1) * w] for g in range(hg)], axis=0)


def _unstack(x, hg):
    return x if hg == 1 else jnp.concatenate([x[g * SB:(g + 1) * SB] for g in range(hg)], axis=1)


def _swa_masks(b, nsb, hg):
    r = lax.rem(lax.broadcasted_iota(jnp.int32, (hg * SB, SB), 0), SB)
    c = lax.broadcasted_iota(jnp.int32, (hg * SB, SB), 1)
    real = b < nsb
    return c <= r, (c > r) & (b > 0) & real, (c < NM) & real


def _swa_specs(nsb):
    def rows(h, w, cb, f):
        return pl.BlockSpec((h, w), lambda i: (f(i), cb))
    pair = lambda i: i
    prev = lambda i: jnp.maximum(2 * i - 1, 0)
    meta = lambda i: nsb
    return rows, pair, prev, meta


def _swa_scores(b, nsb, hg, sink_ref, q, kc, kp, km):
    mc, mp, mm = _swa_masks(b, nsb, hg)
    groups = []
    for kv in range(2):
        ks = slice(kv * 64, (kv + 1) * 64)
        kcb, kpb, kmb = _bf(kc[:, ks]), _bf(kp[:, ks]), _bf(km[:, ks])
        for h0 in range(4 * kv, 4 * kv + 4, hg):
            qg = _bf(_stack(q[:, h0 * 64:(h0 + hg) * 64], hg))
            s_c = jnp.where(mc, _dot_nt(qg, kcb), NEG)
            s_p = jnp.where(mp, _dot_nt(qg, kpb), NEG)
            s_m = jnp.where(mm, _dot_nt(qg, kmb), NEG)
            sink = jnp.concatenate([jnp.full((SB, 1), sink_ref[0, h0 + g], F32) for g in range(hg)], axis=0)
            groups.append((kv, h0, qg, kcb, kpb, kmb, s_c, s_p, s_m, sink))
    return groups


def _swa_fwd(proj, sinks, t, comm=None):
    nsb = t // SB
    r_tot = t + TM
    rows, pair, prev, meta = _swa_specs(nsb)

    def body(sink_ref, q_ref, kc_ref, kp_ref, km_ref, vc_ref, vp_ref, vm_ref, o_ref, lse_ref):
        i = pl.program_id(0)
        km, vm = km_ref[...], vm_ref[...]
        for j in range(2):
            b = 2 * i + j
            rs = slice(j * SB, (j + 1) * SB)
            kp = kp_ref[...] if j == 0 else kc_ref[0:SB, :]
            vp = vp_ref[...] if j == 0 else vc_ref[0:SB, :]
            vc = vc_ref[rs, :]
            o_l, lse_l = [], []
            for kv, h0, qg, kcb, kpb, kmb, s_c, s_p, s_m, sink in _swa_scores(
                    b, nsb, HG_FWD, sink_ref, q_ref[rs, :], kc_ref[rs, :], kp, km):
                ks = slice(kv * 64, (kv + 1) * 64)
                m = jnp.maximum(jnp.max(jnp.maximum(jnp.maximum(s_c, s_p), s_m), -1, keepdims=True), sink)
                p_c, p_p, p_m = jnp.exp(s_c - m), jnp.exp(s_p - m), jnp.exp(s_m - m)
                l = jnp.sum(p_c + p_p + p_m, -1, keepdims=True) + jnp.exp(sink - m)
                o = _dot(_bf(p_c), _bf(vc[:, ks])) + _dot(_bf(p_p), _bf(vp[:, ks])) + _dot(_bf(p_m), _bf(vm[:, ks]))
                o_l.append(_unstack(o * (1.0 / l), HG_FWD))
                lse_l.append(_unstack(m + jnp.log(l), HG_FWD))
            valid = b * SB + lax.broadcasted_iota(jnp.int32, (SB, 1), 0) < t + NM
            o_ref[rs, :] = _bf(jnp.where(valid, jnp.concatenate(o_l, axis=1), 0.0))
            lse_ref[:, rs] = jnp.concatenate(lse_l, axis=1).T

    ck, cv = C_SK // 128, C_SV // 128
    return _call(
        body, "swa_fwd", (r_tot // QB,),
        [SMEM_SPEC, rows(QB, 512, C_SQ // 512, pair),
         rows(QB, 128, ck, pair), rows(SB, 128, ck, prev), rows(SB, 128, ck, meta),
         rows(QB, 128, cv, pair), rows(SB, 128, cv, prev), rows(SB, 128, cv, meta)],
        [rows(QB, 512, 0, pair), pl.BlockSpec((8, QB), lambda i: (0, i))],
        [jax.ShapeDtypeStruct((r_tot, 512), BF16), jax.ShapeDtypeStruct((8, r_tot), F32)],
        [], _cp(32), (sinks, proj, proj, proj, proj, proj, proj, proj), comm)


def _swa_bwd(proj, sinks, lse_t, do, t, comm=None):
    nsb = t // SB
    r_tot = t + TM
    rows, pair, prev, meta = _swa_specs(nsb)
    hb = HB_BWD
    lanes = hb * SB

    def body(sink_ref, q_ref, kc_ref, kp_ref, km_ref, vc_ref, vp_ref, vm_ref, lse_ref, do_ref,
             dq_ref, dk_ref, dv_ref, dsink_ref):
        i = pl.program_id(0)

        @pl.when(i == 0)
        def _():
            dk_ref[...] = jnp.zeros_like(dk_ref)
            dv_ref[...] = jnp.zeros_like(dv_ref)
            dsink_ref[...] = jnp.zeros_like(dsink_ref)

        key = lax.broadcasted_iota(jnp.int32, (SB, lanes), 0)
        qry = lax.rem(lax.broadcasted_iota(jnp.int32, (SB, lanes), 1), SB)
        km, vm = km_ref[...], vm_ref[...]
        dsink_l = []
        for j in range(2):
            b = 2 * i + j
            rs = slice(j * SB, (j + 1) * SB)
            real = b < nsb
            masks = (key <= qry, (key > qry) & (b > 0) & real, (key < NM) & real)
            k3 = (kc_ref[rs, :], kp_ref[...] if j == 0 else kc_ref[0:SB, :], km)
            v3 = (vc_ref[rs, :], vp_ref[...] if j == 0 else vc_ref[0:SB, :], vm)
            zero = jnp.zeros((SB, 64), F32)
            dq_l, ds_blk = [], []
            dk_l, dv_l = [[zero, zero] for _ in range(3)], [[zero, zero] for _ in range(3)]
            for h0 in range(0, 8, hb):
                kv = h0 // 4
                ks, hs = slice(kv * 64, (kv + 1) * 64), slice(h0 * 64, (h0 + hb) * 64)
                qg = _bf(_stack(q_ref[rs, hs], hb))
                dog = _bf(_stack(do_ref[rs, hs], hb))
                lse_row = jnp.concatenate([lse_ref[h:h + 1, rs] for h in range(h0, h0 + hb)], axis=1)
                sink_row = jnp.concatenate([jnp.full((1, SB), sink_ref[0, h], F32) for h in range(h0, h0 + hb)], axis=1)
                kb = [_bf(k[:, ks]) for k in k3]
                vb = [_bf(v[:, ks]) for v in v3]
                s = [_dot_nt(k, qg) for k in kb]
                dp = [_dot_nt(v, dog) for v in vb]
                p = [jnp.exp(jnp.where(m, sx, NEG) - lse_row) for m, sx in zip(masks, s)]
                delta = jnp.sum(p[0] * dp[0] + p[1] * dp[1] + p[2] * dp[2], axis=0, keepdims=True)
                ds = [_bf(pp * (dd - delta)) for pp, dd in zip(p, dp)]
                dq_t = _dot_tn(kb[0], ds[0]) + _dot_tn(kb[1], ds[1]) + _dot_tn(kb[2], ds[2])
                dq_l.append(_unstack(dq_t.T, hb))
                for x in range(3):
                    dk_l[x][kv] = dk_l[x][kv] + _dot(ds[x], qg)
                    dv_l[x][kv] = dv_l[x][kv] + _dot(_bf(p[x]), dog)
                ds_row = -jnp.exp(sink_row - lse_row) * delta
                ds_blk += [jnp.sum(ds_row[:, g * SB:(g + 1) * SB], axis=1, keepdims=True) for g in range(hb)]
            dsink_l.append(jnp.concatenate(ds_blk, axis=1))
            dq_ref[rs, :] = jnp.concatenate(dq_l, axis=1)
            starts = (pl.multiple_of(b * SB, SB), pl.multiple_of(jnp.maximum(b - 1, 0) * SB, SB), t)
            for x in range(3):
                dk_ref[pl.ds(starts[x], SB), :] += jnp.concatenate(dk_l[x], axis=1)
                dv_ref[pl.ds(starts[x], SB), :] += jnp.concatenate(dv_l[x], axis=1)
        dsink_ref[...] += dsink_l[0] + dsink_l[1]

    ck, cv = C_SK // 128, C_SV // 128
    whole = lambda w: pl.BlockSpec((r_tot, w), lambda i: (0, 0))
    return _call(
        body, "swa_bwd", (r_tot // QB,),
        [SMEM_SPEC, rows(QB, 512, C_SQ // 512, pair),
         rows(QB, 128, ck, pair), rows(SB, 128, ck, prev), rows(SB, 128, ck, meta),
         rows(QB, 128, cv, pair), rows(SB, 128, cv, prev), rows(SB, 128, cv, meta),
         pl.BlockSpec((8, QB), lambda i: (0, i)), rows(QB, 512, 1, pair)],
        [rows(QB, 512, 0, pair), whole(128), whole(128), pl.BlockSpec((1, 8), lambda i: (0, 0))],
        [jax.ShapeDtypeStruct((r_tot, 512), F32), jax.ShapeDtypeStruct((r_tot, 128), F32),
         jax.ShapeDtypeStruct((r_tot, 128), F32), jax.ShapeDtypeStruct((1, 8), F32)],
        [], _cp(48), (sinks, proj, proj, proj, proj, proj, proj, proj, lse_t, do), comm)


HK = D // 2


def _mlp_fwd(x, metapad, tgt, ogla, oswa, wo, wff, w1, w2, wfin):
    t = x.shape[0]
    nblk = t // TM

    def body(x_ref, mp_ref, tgt_ref, og_ref, os_ref, wo_ref, wff_ref, w1a_ref, w1b_ref, w2a_ref, w2b_ref, wfin_ref,
             h1_ref, f_ref, a_ref, dh2_ref, loss_ref, gfin_ref):
        i = pl.program_id(0)

        @pl.when(i == 0)
        def _():
            loss_ref[...] = jnp.zeros_like(loss_ref)
            gfin_ref[...] = jnp.zeros_like(gfin_ref)

        h0 = jnp.where(i == nblk, mp_ref[...], x_ref[...])
        h1 = h0 + _dot(og_ref[...], wo_ref[0:512, :]) + _dot(os_ref[...], wo_ref[512:1024, :])
        h1_ref[...] = h1
        fh, _ = _rms(h1)
        f = _bf(fh * wff_ref[...])
        f_ref[...] = f
        acc = jnp.zeros((TM, D), F32)
        for n in range(4):
            a = _dot(f[:, 0:HK], w1a_ref[n]) + _dot(f[:, HK:D], w1b_ref[n])
            a_ref[:, n * D:(n + 1) * D] = _bf(a)
            zr = jnp.maximum(a, 0.0)
            z = _bf(zr * zr)
            acc = acc + _dot(z[:, 0:HK], w2a_ref[n]) + _dot(z[:, HK:D], w2b_ref[n])
        h2 = h1 + acc
        yh, rs2 = _rms(h2)
        wf = wfin_ref[...]
        real = i < nblk
        e = jnp.where(real, yh * wf - tgt_ref[...], 0.0)
        loss_ref[...] += jnp.sum(jnp.sum(e * e, axis=0, keepdims=True), axis=1, keepdims=True) * (0.5 / D)
        dy = e * (1.0 / D)
        gfin_ref[...] += jnp.sum(dy * yh, axis=0, keepdims=True)
        dh2_ref[...] = _rms_bwd(dy, yh, rs2, wf)

    xs = pl.BlockSpec((TM, D), lambda i: (jnp.minimum(i, nblk - 1), 0))
    rs = lambda w: pl.BlockSpec((TM, w), lambda i: (i, 0))
    r_tot = t + TM
    return pl.pallas_call(
        body, name="mlp_fwd", grid=(nblk + 1,),
        in_specs=[xs, VMEM_SPEC, xs, rs(512), rs(512)] + [VMEM_SPEC] * 7,
        out_specs=[rs(D), rs(D), rs(DFF), rs(D), pl.BlockSpec((1, 1), lambda i: (0, 0)), pl.BlockSpec((1, D), lambda i: (0, 0))],
        out_shape=[jax.ShapeDtypeStruct((r_tot, D), F32), jax.ShapeDtypeStruct((r_tot, D), BF16),
                   jax.ShapeDtypeStruct((r_tot, DFF), BF16), jax.ShapeDtypeStruct((r_tot, D), F32),
                   jax.ShapeDtypeStruct((1, 1), F32), jax.ShapeDtypeStruct((1, D), F32)],
        compiler_params=_cp(56),
    )(x, metapad, tgt, ogla, oswa, wo, wff, *w1, *w2, wfin)


def _mlp_bwd(h1, a, dh2, ogla, oswa, wo, wff, w1, w2):
    r_tot = h1.shape[0]
    nt = r_tot // TM

    def body(h1_ref, a_ref, dh2_ref, og_ref, os_ref, wo_ref, wff_ref, w1a_ref, w1b_ref, w2a_ref, w2b_ref,
             da_ref, dh2b_ref, dh1_ref, do_ref, dwo_ref, gff_ref, dwo_acc):
        i = pl.program_id(0)

        @pl.when(i == 0)
        def _():
            dwo_acc[...] = jnp.zeros_like(dwo_acc)
            gff_ref[...] = jnp.zeros_like(gff_ref)

        dh2 = dh2_ref[...]
        dh2b = _bf(dh2)
        dh2b_ref[...] = dh2b
        dfa = jnp.zeros((TM, HK), F32)
        dfb = jnp.zeros((TM, HK), F32)
        for n in range(4):
            dz = jnp.concatenate([_dot_nt(dh2b, w2a_ref[n]), _dot_nt(dh2b, w2b_ref[n])], axis=1)
            da = _bf(dz * (2.0 * jnp.maximum(a_ref[:, n * D:(n + 1) * D].astype(F32), 0.0)))
            da_ref[:, n * D:(n + 1) * D] = da
            dfa = dfa + _dot_nt(da, w1a_ref[n])
            dfb = dfb + _dot_nt(da, w1b_ref[n])
        df = jnp.concatenate([dfa, dfb], axis=1)
        fh, rs1 = _rms(h1_ref[...])
        gff_ref[...] += jnp.sum(df * fh, axis=0, keepdims=True)
        dh1 = dh2 + _rms_bwd(df, fh, rs1, wff_ref[...])
        dh1_ref[...] = dh1
        dh1b = _bf(dh1)
        do_ref[...] = _dot_nt(dh1b, wo_ref[...])
        dwo_acc[0:512, :] += _dot_tn(og_ref[...], dh1b)
        dwo_acc[512:1024, :] += _dot_tn(os_ref[...], dh1b)

        @pl.when(i == nt - 1)
        def _():
            for s in range(4):
                for hh in range(2):
                    dwo_ref[hh, s] = dwo_acc[(2 * s + hh) * 128:(2 * s + hh + 1) * 128, :]

    rs = lambda w: pl.BlockSpec((TM, w), lambda i: (i, 0))
    return pl.pallas_call(
        body, name="mlp_bwd", grid=(nt,),
        in_specs=[rs(D), rs(DFF), rs(D), rs(512), rs(512)] + [VMEM_SPEC] * 6,
        out_specs=[rs(DFF), rs(D), rs(D), rs(D), VMEM_SPEC, pl.BlockSpec((1, D), lambda i: (0, 0))],
        out_shape=[jax.ShapeDtypeStruct((r_tot, DFF), BF16), jax.ShapeDtypeStruct((r_tot, D), BF16),
                   jax.ShapeDtypeStruct((r_tot, D), F32), jax.ShapeDtypeStruct((r_tot, D), F32),
                   jax.ShapeDtypeStruct((2, 4, 128, D), F32), jax.ShapeDtypeStruct((1, D), F32)],
        scratch_shapes=[pltpu.VMEM((D, D), F32)],
        compiler_params=_cp(56),
    )(h1, a, dh2, ogla, oswa, wo, wff, *w1, *w2)


def _ffn_wgrad(f, a, da, dh2b):
    r_tot = f.shape[0]
    kt = 768 if r_tot % 768 == 0 else TM
    nk = r_tot // kt

    def body(f_ref, a_ref, da_ref, dh2_ref, dw1_ref, dw2_ref, acc1, acc2):
        k = pl.program_id(1)

        @pl.when(k == 0)
        def _():
            acc1[...] = jnp.zeros_like(acc1)
            acc2[...] = jnp.zeros_like(acc2)

        zr = jnp.maximum(a_ref[...], 0.0)
        acc1[...] += _dot_tn(f_ref[...], da_ref[...])
        acc2[...] += _dot_tn(zr * zr, dh2_ref[...])

        @pl.when(k == nk - 1)
        def _():
            for hh in range(2):
                dw1_ref[hh, 0] = acc1[hh * 512:(hh + 1) * 512, :]
                dw2_ref[hh, 0] = acc2[hh * 512:(hh + 1) * 512, :]

    out = pl.BlockSpec((2, 1, 512, D), lambda n, k: (0, n, 0, 0))
    return pl.pallas_call(
        body, name="ffn_wgrad", grid=(4, nk),
        in_specs=[pl.BlockSpec((kt, D), lambda n, k: (k, 0)), pl.BlockSpec((kt, D), lambda n, k: (k, n)),
                  pl.BlockSpec((kt, D), lambda n, k: (k, n)), pl.BlockSpec((kt, D), lambda n, k: (k, 0))],
        out_specs=[out, out],
        out_shape=[jax.ShapeDtypeStruct((2, 4, 512, D), F32)] * 2,
        scratch_shapes=[pltpu.VMEM((D, D), F32), pltpu.VMEM((D, D), F32)],
        compiler_params=_cp(48, ("arbitrary", "arbitrary")),
    )(f, a, da, dh2b)


def _proj_bwd(x, metapad, wm, wt3, tabs, dgla, dswa_q, dsk, dsv, dlr, dh1, comm=None):
    t = x.shape[0]
    nblk = t // TM

    def body(x_ref, mp_ref, wm_ref, w3_ref, tab_ref, dg_ref, dq_ref, dk_ref, dv_ref, dlr_ref, dh1_ref,
             gx_ref, gmeta_ref, dw_ref, gmix_ref, w_ref, acc):
        i = pl.program_id(0)

        @pl.when(i == 0)
        def _():
            _join_shards(w3_ref, w_ref)
            acc[...] = jnp.zeros_like(acc)
            gmix_ref[...] = jnp.zeros_like(gmix_ref)

        h = jnp.where(i == nblk, mp_ref[...], x_ref[...])
        uh, rs = _rms(h)
        wm_v = wm_ref[...]
        u = _bf(uh * wm_v)
        tab = tab_ref[...]
        dq = _bf(_rope(dq_ref[...] * 0.125, tab, -1.0))
        dk = _bf(_rope(dk_ref[...], tab, -1.0))
        parts = ((dg_ref[...], 0, R_LR), (dlr_ref[:, 0:16], R_LR, 16), (dq, R_LR + 16, 512),
                 (dk, R_LR + 528, 128), (_bf(dv_ref[...]), R_LR + 656, 128))
        du = jnp.zeros((TM, D), F32)
        for val, r0, w in parts:
            du = du + _dot(val, w_ref[r0:r0 + w, :])
            acc[r0:r0 + w, :] += _dot_tn(val, u)
        gmix_ref[...] += jnp.sum(du * uh, axis=0, keepdims=True)
        dh0 = dh1_ref[...] + _rms_bwd(du, uh, rs, wm_v)

        @pl.when(i < nblk)
        def _():
            gx_ref[...] = dh0

        @pl.when(i == nblk)
        def _():
            gmeta_ref[...] = dh0[:NM]
            for s in range(4):
                dw_ref[s] = acc[(DIN // 4) * s:(DIN // 4) * (s + 1), :]

    xs = pl.BlockSpec((TM, D), lambda i: (jnp.minimum(i, nblk - 1), 0))
    rs_ = lambda w: pl.BlockSpec((TM, w), lambda i: (i, 0))
    return _call(
        body, "proj_bwd", (nblk + 1,),
        [xs, VMEM_SPEC, VMEM_SPEC, VMEM_SPEC, rs_(128), rs_(1536), rs_(512), rs_(128), rs_(128), rs_(128), rs_(D)],
        [xs, pl.BlockSpec((NM, D), lambda i: (0, 0)), VMEM_SPEC, pl.BlockSpec((1, D), lambda i: (0, 0))],
        [jax.ShapeDtypeStruct((t, D), F32), jax.ShapeDtypeStruct((NM, D), F32),
         jax.ShapeDtypeStruct((4, DIN // 4, D), F32), jax.ShapeDtypeStruct((1, D), F32)],
        [pltpu.VMEM((DIN, D), BF16), pltpu.VMEM((DIN, D), F32)], _cp(56),
        (x, metapad, wm, wt3, tabs, dgla, dswa_q, dsk, dsv, dlr, dh1), comm)


def _place():
    return lax.axis_index("x"), lax.axis_index("y"), lax.axis_index("c")


def _other_chips(x, y):
    return [(1 - x, y), (x, 1 - y), (1 - x, 1 - y)]


def _dma_sems(*counts):
    return tuple(pltpu.SemaphoreType.DMA((k,)) for k in counts)


def _gather_shards(shards, split):
    n = len(shards)
    two = [a for a in range(n) if split[a]]

    def plan(ins, outs, sems):
        isend, irecv, dsend, drecv, loc = sems
        x, y, c = _place()
        chips = _other_chips(x, y)

        def part(ref, a, half):
            if not split[a]:
                return ref
            w = shards[a].shape[1] // 2
            return ref.at[:, pl.ds(pl.multiple_of(half * w, 128), w)]

        def over_ici(a, k, shard_of):
            tx, ty = chips[k]
            sx, sy = shard_of
            return pltpu.make_async_remote_copy(
                src_ref=part(ins[a], a, c), dst_ref=part(outs[a].at[2 * sx + sy], a, c), send_sem=isend.at[3 * a + k],
                recv_sem=irecv.at[3 * a + k], device_id=(tx, ty, c), device_id_type=MESH)

        def over_d2d(a, k, half):
            tx, ty = chips[k]
            ref = part(outs[a].at[2 * tx + ty], a, half)
            return pltpu.make_async_remote_copy(
                src_ref=ref, dst_ref=ref, send_sem=dsend.at[3 * a + k], recv_sem=drecv.at[3 * a + k],
                device_id=(x, y, 1 - c), device_id_type=MESH)

        def local(a):
            return pltpu.make_async_copy(ins[a], outs[a].at[2 * x + y], loc.at[a])

        pairs = [(a, k) for a in range(n) for k in range(3)]
        first = ([lambda a=a: local(a).start() for a in range(n)]
                 + [lambda a=a, k=k: over_ici(a, k, (x, y)).start() for a, k in pairs],
                 [lambda a=a, k=k: over_ici(a, k, chips[k]).wait_recv() for a, k in pairs]
                 + [lambda a=a, k=k: over_ici(a, k, (x, y)).wait_send() for a, k in pairs]
                 + [lambda a=a: local(a).wait() for a in range(n)])
        pairs2 = [(a, k) for a in two for k in range(3)]
        second = ([lambda a=a, k=k: over_d2d(a, k, c).start() for a, k in pairs2],
                  [lambda a=a, k=k: over_d2d(a, k, 1 - c).wait_recv() for a, k in pairs2]
                  + [lambda a=a, k=k: over_d2d(a, k, c).wait_send() for a, k in pairs2])
        return [first, second] if two else [first]

    return _Comm(tuple(shards), tuple(jax.ShapeDtypeStruct((4,) + s.shape, s.dtype) for s in shards),
                 _dma_sems(3 * n, 3 * n, 3 * n, 3 * n, n), 2 if two else 1, plan)


def _swap_halves(grads):
    n = len(grads)

    def plan(ins, outs, sems):
        send, recv = sems
        x, y, c = _place()

        def swap(a):
            return pltpu.make_async_remote_copy(
                src_ref=ins[a].at[1 - c], dst_ref=outs[a], send_sem=send.at[a], recv_sem=recv.at[a],
                device_id=(x, y, 1 - c), device_id_type=MESH)

        return [([lambda a=a: swap(a).start() for a in range(n)], [lambda a=a: swap(a).wait() for a in range(n)])]

    return _Comm(tuple(grads), tuple(jax.ShapeDtypeStruct(g.shape[1:], g.dtype) for g in grads), _dma_sems(n, n), 1, plan)


def _scatter_shards(parts):
    n = len(parts)

    def plan(ins, outs, sems):
        send, recv = sems
        x, y, c = _place()
        chips = _other_chips(x, y)

        def scatter(a, k):
            tx, ty = chips[k]
            return pltpu.make_async_remote_copy(
                src_ref=ins[a].at[2 * tx + ty], dst_ref=outs[a].at[k], send_sem=send.at[3 * a + k],
                recv_sem=recv.at[3 * a + k], device_id=(tx, ty, c), device_id_type=MESH)

        pairs = [(a, k) for a in range(n) for k in range(3)]
        return [([lambda a=a, k=k: scatter(a, k).start() for a, k in pairs],
                 [lambda a=a, k=k: scatter(a, k).wait() for a, k in pairs])]

    return _Comm(tuple(parts), tuple(jax.ShapeDtypeStruct((3,) + p.shape[1:], p.dtype) for p in parts),
                 _dma_sems(3 * n, 3 * n), 1, plan)


def _join_halves(halves):
    n = len(halves)

    def plan(ins, outs, sems):
        send, recv, loc = sems
        x, y, c = _place()

        def remote(a, half):
            return pltpu.make_async_remote_copy(
                src_ref=ins[a], dst_ref=outs[a].at[half], send_sem=send.at[a], recv_sem=recv.at[a],
                device_id=(x, y, 1 - c), device_id_type=MESH)

        def local(a):
            return pltpu.make_async_copy(ins[a], outs[a].at[c], loc.at[a])

        every = range(n)
        return [([lambda a=a: local(a).start() for a in every] + [lambda a=a: remote(a, c).start() for a in every],
                 [lambda a=a: remote(a, 1 - c).wait_recv() for a in every]
                 + [lambda a=a: remote(a, c).wait_send() for a in every] + [lambda a=a: local(a).wait() for a in every])]

    return _Comm(tuple(halves), tuple(jax.ShapeDtypeStruct((2,) + h.shape, h.dtype) for h in halves),
                 _dma_sems(n, n, n), 1, plan)


def _reduce_w_in(dwt, comm):
    rows, hw = DIN // 4, D // 2
    ci, co = len(comm.ins), len(comm.outs)

    def body(*refs):
        dw_ref, c_in, out_ref, c_out = refs[0], refs[1:1 + ci], refs[1 + ci], refs[2 + ci:2 + ci + co]
        mine, sib, tosend, rbuf, qbuf, full, send, recv, loc = refs[2 + ci + co:11 + ci + co]
        c_sem = refs[11 + ci + co:]
        x, y, c = _place()
        sibling = (x, y, 1 - c)
        (starts, waits), = comm.plan(c_in, c_out, c_sem)
        _run_phase(starts)

        def cols(ref, half):
            window = pl.ds(pl.multiple_of(half * hw, 128), hw)
            return ref.at[:, :, window] if len(ref.shape) == 3 else ref.at[:, window]

        load = pltpu.make_async_copy(cols(dw_ref, c), mine, loc.at[0])
        give = pltpu.make_async_remote_copy(src_ref=cols(dw_ref, 1 - c), dst_ref=sib, send_sem=send.at[3], recv_sem=recv.at[3],
                                            device_id=sibling, device_id_type=MESH)
        load.start()
        give.start()
        load.wait()
        give.wait()
        mine[...] = mine[...] + sib[...]
        cps = []
        for k, (tx, ty) in enumerate(_other_chips(x, y)):
            tosend[k] = _bf(mine[2 * tx + ty])
            cps.append(pltpu.make_async_remote_copy(
                src_ref=tosend.at[k], dst_ref=rbuf.at[k], send_sem=send.at[k], recv_sem=recv.at[k],
                device_id=(tx, ty, c), device_id_type=MESH))
            cps[-1].start()
        for cp in cps:
            cp.wait()
        qbuf[...] = mine[2 * x + y] + rbuf[0].astype(F32) + rbuf[1].astype(F32) + rbuf[2].astype(F32)
        keep = pltpu.make_async_copy(qbuf, cols(full, c), loc.at[1])
        pass_on = pltpu.make_async_remote_copy(src_ref=qbuf, dst_ref=cols(full, c), send_sem=send.at[4], recv_sem=recv.at[4],
                                               device_id=sibling, device_id_type=MESH)
        keep.start()
        pass_on.start()
        keep.wait()
        pass_on.wait_send()
        pltpu.make_async_remote_copy(src_ref=qbuf, dst_ref=cols(full, 1 - c), send_sem=send.at[4], recv_sem=recv.at[4],
                                     device_id=sibling, device_id_type=MESH).wait_recv()
        out_ref[...] = full[...]
        _run_phase(waits)

    outs = pl.pallas_call(
        body, name="reduce_w_in",
        in_specs=[ANY_SPEC] * (1 + ci), out_specs=[VMEM_SPEC] + [ANY_SPEC] * co,
        out_shape=[jax.ShapeDtypeStruct((rows, D), F32)] + list(comm.outs),
        scratch_shapes=[pltpu.VMEM((4, rows, hw), F32), pltpu.VMEM((4, rows, hw), F32), pltpu.VMEM((3, rows, hw), BF16),
                        pltpu.VMEM((3, rows, hw), BF16), pltpu.VMEM((rows, hw), F32), pltpu.VMEM((rows, D), F32),
                        *_dma_sems(5, 5, 2), *comm.sems],
        compiler_params=pltpu.CompilerParams(vmem_limit_bytes=48 << 20),
    )(dwt, *comm.ins)
    return outs[0], outs[1:]


def _allreduce_small(pack):
    p = pack.shape[0]

    def body(in_ref, out_ref, buf, send, recv):
        x, y, c = _place()
        me = 4 * x + 2 * y + c
        buf[me] = in_ref[...]

        def peer_of(k):
            return x ^ (k >> 2), y ^ ((k >> 1) & 1), c ^ (k & 1)

        sends = [pltpu.make_async_remote_copy(
            src_ref=in_ref, dst_ref=buf.at[me], send_sem=send.at[k - 1], recv_sem=recv.at[k - 1],
            device_id=peer_of(k), device_id_type=MESH) for k in range(1, 8)]
        for cp in sends:
            cp.start()
        for k in range(1, 8):
            px, py, pc = peer_of(k)
            pltpu.make_async_remote_copy(
                src_ref=in_ref, dst_ref=buf.at[4 * px + 2 * py + pc], send_sem=send.at[k - 1], recv_sem=recv.at[k - 1],
                device_id=(x, y, c), device_id_type=MESH).wait_recv()
        for cp in sends:
            cp.wait_send()
        acc = buf[0]
        for d in range(1, 8):
            acc = acc + buf[d]
        out_ref[...] = acc

    return pl.pallas_call(
        body, name="allreduce_small",
        in_specs=[VMEM_SPEC], out_specs=VMEM_SPEC, out_shape=jax.ShapeDtypeStruct(pack.shape, F32),
        scratch_shapes=[pltpu.VMEM((8, p, D), F32), *_dma_sems(7, 7)],
    )(pack)


GRID4 = 4


def _sum_cores(core_shard, mine, theirs):
    n = len(mine)

    def body(cs_ref, *refs):
        ms, ts, bfs, owns = refs[:n], refs[n:2 * n], refs[2 * n:3 * n], refs[3 * n:]
        keep = pl.program_id(0) == cs_ref[1]
        for a in range(n):
            acc = ms[a][0, 0] + ts[a][0]
            bfs[a][0] = _bf(acc)

            @pl.when(keep)
            def _():
                owns[a][...] = acc

    shapes = [m.shape[2:] for m in mine]
    in_specs = ([pl.BlockSpec((1, 1) + s, lambda i, cs: (cs[0], i, 0, 0)) for s in shapes]
                + [pl.BlockSpec((1,) + s, lambda i, cs: (i, 0, 0)) for s in shapes])
    out_specs = ([pl.BlockSpec((1,) + s, lambda i, cs: (i, 0, 0)) for s in shapes]
                 + [pl.BlockSpec(s, lambda i, cs: (0, 0)) for s in shapes])
    outs = pl.pallas_call(
        body, name="sum_cores",
        grid_spec=pltpu.PrefetchScalarGridSpec(num_scalar_prefetch=1, grid=(4,), in_specs=in_specs, out_specs=out_specs),
        out_shape=[jax.ShapeDtypeStruct((4,) + s, BF16) for s in shapes] + [jax.ShapeDtypeStruct(s, F32) for s in shapes],
        compiler_params=_cp(48),
    )(core_shard, *mine, *theirs)
    return outs[:n], outs[n:]


def _sum_chips(own, arrived):
    n = len(own)

    def body(*refs):
        os_, ars, outs = refs[:n], refs[n:2 * n], refs[2 * n:]
        for a in range(n):
            outs[a][...] = os_[a][...] + ars[a][0].astype(F32) + ars[a][1].astype(F32) + ars[a][2].astype(F32)

    blocks = [(o.shape[0] // GRID4, o.shape[1]) for o in own]
    return pl.pallas_call(
        body, name="sum_chips", grid=(GRID4,),
        in_specs=([pl.BlockSpec(b, lambda i: (i, 0)) for b in blocks]
                  + [pl.BlockSpec((3,) + b, lambda i: (0, i, 0)) for b in blocks]),
        out_specs=[pl.BlockSpec(b, lambda i: (i, 0)) for b in blocks],
        out_shape=[jax.ShapeDtypeStruct(o.shape, F32) for o in own],
        compiler_params=_cp(32),
    )(*own, *arrived)


def _adamw_math(w, g, m, v):
    m2 = ADAM_B1 * m + (1.0 - ADAM_B1) * g
    v2 = ADAM_B2 * v + (1.0 - ADAM_B2) * (g * g)
    m_hat = m2 / (1.0 - ADAM_B1 ** ADAM_STEP)
    v_hat = v2 / (1.0 - ADAM_B2 ** ADAM_STEP)
    return -ADAM_LR * (m_hat / (jnp.sqrt(v_hat) + ADAM_EPS) + ADAM_WD * w), m2, v2


def _adamw_big(ws, gs, ms, vs):
    n = len(ws)

    def body(*refs):
        for a in range(n):
            d, m2, v2 = _adamw_math(refs[a][...], refs[n + a][...], refs[2 * n + a][...], refs[3 * n + a][...])
            refs[4 * n + a][...] = d
            refs[5 * n + a][...] = m2
            refs[6 * n + a][...] = v2

    specs = [pl.BlockSpec((w.shape[0] // GRID4, w.shape[1]), lambda i: (i, 0)) for w in ws]
    return pl.pallas_call(
        body, name="adamw_big", grid=(GRID4,),
        in_specs=specs * 4, out_specs=specs * 3,
        out_shape=[jax.ShapeDtypeStruct(w.shape, F32) for w in ws] * 3,
        compiler_params=_cp(48),
    )(*ws, *gs, *ms, *vs)


def _adamw_small(ws, gs, ms, vs):
    n = len(ws)

    def body(*refs):
        for a in range(n):
            d, m2, v2 = _adamw_math(refs[a][...], refs[n + a][...], refs[2 * n + a][...], refs[3 * n + a][...])
            refs[4 * n + a][...] = d
            refs[5 * n + a][...] = m2
            refs[6 * n + a][...] = v2

    return pl.pallas_call(
        body, name="adamw_small",
        in_specs=[VMEM_SPEC] * (4 * n), out_specs=[VMEM_SPEC] * (3 * n),
        out_shape=[jax.ShapeDtypeStruct(w.shape, F32) for w in ws] * 3,
        compiler_params=pltpu.CompilerParams(vmem_limit_bytes=40 << 20),
    )(*ws, *gs, *ms, *vs)


def kernel(x, meta_tokens, norm_mix_w, w_in, w_gate_up, b_gate, gla_norm_w, sinks, w_out, norm_ff_w, w_ff1, w_ff2, final_norm_w, loss_target, m_meta_tokens, m_norm_mix_w, m_w_in, m_w_gate_up, m_b_gate, m_gla_norm_w, m_sinks, m_w_out, m_norm_ff_w, m_w_ff1, m_w_ff2, m_final_norm_w, v_meta_tokens, v_norm_mix_w, v_w_in, v_w_gate_up, v_b_gate, v_gla_norm_w, v_sinks, v_w_out, v_norm_ff_w, v_w_ff1, v_w_ff2, v_final_norm_w):
    xi, yi, ci = _place()
    shard = (2 * xi + yi).astype(jnp.int32).reshape(1)
    core = ci.astype(jnp.int32).reshape(1)

    small = jnp.concatenate([meta_tokens, w_gate_up[0], jnp.zeros((NM, 64), F32)], axis=1)
    wt3, g_small = _run_comm(_gather_shards([_bf(w_in[0].T), small], [True, False]), "gather_w_in")
    meta = g_small[:, :, 0:256].transpose(1, 0, 2).reshape(NM, D)
    wgu = g_small[:, :, 256:320].transpose(1, 0, 2).reshape(NM, 256)

    xs, tgt = x[0], loss_target[0]
    t = xs.shape[0]
    wfin = final_norm_w.reshape(1, D)
    metapad = jnp.concatenate([meta, jnp.zeros((TM - NM, D), F32)], axis=0)
    wgu_p = _bf(jnp.concatenate([wgu, jnp.zeros((128 - 16, 256), F32)], axis=0))
    tabs = _rope_tables(t)

    w1s, w2s = _bf(w_ff1[0]), _bf(w_ff2[0])
    proj, (g_out, w1a) = _proj_fwd(xs, metapad, norm_mix_w, wt3, tabs,
                                   _gather_shards([_bf(w_out[0]), w1s[:HK]], [True] * 2))
    (oswa, lse), (w1b, w2a, w2b) = _swa_fwd(proj, sinks, t, _gather_shards([w1s[HK:], w2s[:HK], w2s[HK:]], [True] * 3))
    (ogla, oraw, sst, bcum, dgate), _ = _gla_fwd(proj, wgu_p, b_gate, gla_norm_w, t)
    wo, w1, w2 = g_out.reshape(D, D), (w1a, w1b), (w2a, w2b)
    h1, f, a, dh2, loss, gfin = _mlp_fwd(xs, metapad, tgt, ogla, oswa, wo, norm_ff_w, w1, w2, wfin)

    da, dh2b, dh1, do, dwo, gff = _mlp_bwd(h1, a, dh2, ogla, oswa, wo, norm_ff_w, w1, w2)
    dw1, dw2 = _ffn_wgrad(f, a, da, dh2b)
    big = [dwo, dw1, dw2]
    (dsq, dsk, dsv, dsink), theirs = _swa_bwd(proj, sinks, lse, do, t, _swap_halves(big))
    sums_bf, own = _sum_cores(jnp.concatenate([core, shard]), big, theirs)
    (dgla, dlr, dwgu, dbg, dgnw), arrived = _gla_bwd(proj, oraw, sst, bcum, dgate, do, wgu_p, gla_norm_w, t,
                                                     _scatter_shards(sums_bf))
    halves = _sum_chips(own, arrived)
    (gx, gmeta, dwt, gmix), _ = _proj_bwd(xs, metapad, norm_mix_w, wt3, tabs, dgla, dsq, dsk, dsv, dlr, dh1)

    gwt_in, joined = _reduce_w_in(dwt, _join_halves(halves))
    gw_out, gw_1, gw_2 = [j.reshape((-1, j.shape[2])) for j in joined]

    tail = jnp.concatenate([dbg, dgnw, dsink, loss, jnp.zeros((1, D - 256 - 128 - 8 - 1), F32)], axis=1)
    pack = jnp.concatenate([gmeta, gmix, gff, gfin, tail, dwgu[:16].reshape(4, D)], axis=0)
    tot = _allreduce_small(pack)
    g_meta = lax.dynamic_slice_in_dim(tot[0:NM], shard[0] * 256, 256, axis=1)
    g_mix, g_ff, g_fin = tot[16:17], tot[17:18], tot[18]
    g_bg, g_gnw, g_sinks, loss_tot = tot[19:20, 0:256], tot[19:20, 256:384], tot[19:20, 384:392], tot[19, 392]
    g_wgu = lax.dynamic_slice_in_dim(tot[20:24].reshape(NM, 256), shard[0] * 64, 64, axis=1)

    bo = _adamw_big([w_out[0], w_ff1[0], w_ff2[0]], [gw_out, gw_1, gw_2], [m_w_out[0], m_w_ff1[0], m_w_ff2[0]],
                    [v_w_out[0], v_w_ff1[0], v_w_ff2[0]])

    fin2 = lambda a: a.reshape(1, D)
    sw = [meta_tokens, norm_mix_w, w_gate_up[0], b_gate, gla_norm_w, sinks, norm_ff_w, fin2(final_norm_w), w_in[0].T]
    sg = [g_meta, g_mix, g_wgu, g_bg, g_gnw, g_sinks, g_ff, fin2(g_fin), gwt_in]
    sm = [m_meta_tokens, m_norm_mix_w, m_w_gate_up[0], m_b_gate, m_gla_norm_w, m_sinks, m_norm_ff_w, fin2(m_final_norm_w),
          m_w_in[0].T]
    sv = [v_meta_tokens, v_norm_mix_w, v_w_gate_up[0], v_b_gate, v_gla_norm_w, v_sinks, v_norm_ff_w, fin2(v_final_norm_w),
          v_w_in[0].T]
    so = _adamw_small(sw, sg, sm, sv)

    def ordered(small_o, big_o):
        meta_, mix_, wgu_, bg_, gnw_, sinks_, ff_, fin_, wt_ = small_o
        w_out_, w_1_, w_2_ = big_o
        return (meta_, mix_, wt_.T[None], wgu_[None], bg_, gnw_, sinks_, w_out_[None], ff_, w_1_[None], w_2_[None],
                fin_.reshape(D))

    grads = ordered(sg, [gw_out, gw_1, gw_2])
    deltas = ordered(so[0:9], bo[0:3])
    new_m = ordered(so[9:18], bo[3:6])
    new_v = ordered(so[18:27], bo[6:9])
    return (loss_tot, gx[None], *grads, *deltas, *new_m, *new_v)
```

```python
import functools
from typing import Callable, NamedTuple

import jax
import jax.numpy as jnp
import numpy as np
from jax import lax
from jax.experimental import pallas as pl
from jax.experimental.pallas import tpu as pltpu

F32 = jnp.float32
BF16 = jnp.bfloat16

D = 1024
DFF = 4096
NM = 16
TM = 256
CH = 64
SB = 128
QB = 2 * SB
EPS = 1e-5
C_GQ, C_GK, C_GV, C_GR, C_SQ, C_SK, C_SV, C_LR, DINP = 0, 256, 512, 1024, 1536, 2048, 2176, 2304, 2432
DIN = 2320
R_LR = 1536
ROPE_THETA = 500000.0
ADAM_LR, ADAM_B1, ADAM_B2, ADAM_EPS, ADAM_WD, ADAM_STEP = 0.001, 0.9, 0.999, 1e-08, 0.01, 10
NEG = -1e30
MESH = pl.DeviceIdType.MESH
VMEM_SPEC = pl.BlockSpec(memory_space=pltpu.VMEM)
ANY_SPEC = pl.BlockSpec(memory_space=pl.ANY)
SMEM_SPEC = pl.BlockSpec(memory_space=pltpu.SMEM)


def _cp(vmem_mb, sem=("arbitrary",)):
    return pltpu.CompilerParams(dimension_semantics=sem, vmem_limit_bytes=vmem_mb << 20)


def _dot(a, b):
    return jnp.dot(a, b, preferred_element_type=F32)


def _dot_nt(a, b):
    return lax.dot_general(a, b, (((1,), (1,)), ((), ())), preferred_element_type=F32)


def _dot_tn(a, b):
    return lax.dot_general(a, b, (((0,), (0,)), ((), ())), preferred_element_type=F32)


def _bf(x):
    return x.astype(BF16)


def _dot3(m01, x):
    x1 = _bf(x)
    r1 = x - x1.astype(F32)
    x2 = _bf(r1)
    x3 = _bf(r1 - x2.astype(F32))
    return _dot(m01, x1) + _dot(m01, x2) + _dot(m01, x3)


def _rms(h):
    rs = lax.rsqrt(jnp.mean(h * h, axis=-1, keepdims=True) + EPS)
    return h * rs, rs


def _rms_bwd(dy, yhat, rs, w):
    dyh = dy * w
    return rs * (dyh - yhat * jnp.mean(dyh * yhat, axis=-1, keepdims=True))


class _Comm(NamedTuple):
    ins: tuple
    outs: tuple
    sems: tuple
    phases: int
    plan: Callable


def _run_phase(fns):
    for fn in fns:
        fn()


def _call(body, name, grid, in_specs, out_specs, out_shape, scratch, params, args, comm=None):
    if comm is None:
        outs = pl.pallas_call(body, name=name, grid=grid, in_specs=in_specs, out_specs=out_specs, out_shape=out_shape,
                              scratch_shapes=scratch, compiler_params=params)(*args)
        return outs, None
    n_in, n_out, n_scr = len(in_specs), len(out_specs), len(scratch)
    ci, co = len(comm.ins), len(comm.outs)
    last = grid[0] - 1
    marks = [0, max(1, last - max(2, (last + 1) // 6))][:comm.phases]

    def wrapped(*refs):
        own_in, c_in = refs[:n_in], refs[n_in:n_in + ci]
        refs = refs[n_in + ci:]
        own_out, c_out = refs[:n_out], refs[n_out:n_out + co]
        refs = refs[n_out + co:]
        own_scr, c_sem = refs[:n_scr], refs[n_scr:]
        i = pl.program_id(0)

        for p, mark in enumerate(marks):
            @pl.when(i == mark)
            def _():
                plan = comm.plan(c_in, c_out, c_sem)
                if p > 0:
                    _run_phase(plan[p - 1][1])
                _run_phase(plan[p][0])

        body(*own_in, *own_out, *own_scr)

        @pl.when(i == last)
        def _():
            _run_phase(comm.plan(c_in, c_out, c_sem)[-1][1])

    outs = pl.pallas_call(
        wrapped, name=name, grid=grid, in_specs=list(in_specs) + [ANY_SPEC] * ci, out_specs=list(out_specs) + [ANY_SPEC] * co,
        out_shape=list(out_shape) + list(comm.outs), scratch_shapes=list(scratch) + list(comm.sems), compiler_params=params,
    )(*args, *comm.ins)
    return outs[:n_out], outs[n_out:]


def _run_comm(comm, name):
    ci, co = len(comm.ins), len(comm.outs)

    def body(*refs):
        for starts, waits in comm.plan(refs[:ci], refs[ci:ci + co], refs[ci + co:]):
            _run_phase(starts)
            _run_phase(waits)

    return pl.pallas_call(body, name=name, in_specs=[ANY_SPEC] * ci, out_specs=[ANY_SPEC] * co, out_shape=list(comm.outs),
                          scratch_shapes=list(comm.sems))(*comm.ins)


def _join_shards(w3_ref, w_ref):
    for s in range(4):
        w_ref[(DIN // 4) * s:(DIN // 4) * (s + 1), :] = w3_ref[s]


def _proj_fwd(x, metapad, wm, wt3, tabs, comm=None):
    t = x.shape[0]
    nblk = t // TM

    def body(x_ref, mp_ref, wm_ref, w3_ref, tab_ref, proj_ref, w_ref):
        i = pl.program_id(0)

        @pl.when(i == 0)
        def _():
            _join_shards(w3_ref, w_ref)

        h = jnp.where(i == nblk, mp_ref[...], x_ref[...])
        u, _ = _rms(h)
        ub = _bf(u * wm_ref[...])
        proj_ref[:, 0:C_SQ] = _dot_nt(ub, w_ref[0:R_LR, :])
        att = _dot_nt(ub, w_ref[R_LR + 16:DIN, :])
        tab = tab_ref[...]
        proj_ref[:, C_SQ:C_SK] = _rope(att[:, 0:512], tab, 1.0) * 0.125
        proj_ref[:, C_SK:C_SV] = _rope(att[:, 512:640], tab, 1.0)
        proj_ref[:, C_SV:C_LR] = att[:, 640:768]
        proj_ref[:, C_LR:DINP] = jnp.zeros((TM, DINP - C_LR), F32)
        proj_ref[:, C_LR:C_LR + 16] = _dot_nt(ub, w_ref[R_LR:R_LR + 16, :])

    (proj,), got = _call(
        body, "proj_fwd", (nblk + 1,),
        [pl.BlockSpec((TM, D), lambda i: (jnp.minimum(i, nblk - 1), 0)), VMEM_SPEC, VMEM_SPEC, VMEM_SPEC,
         pl.BlockSpec((TM, 128), lambda i: (i, 0))],
        [pl.BlockSpec((TM, DINP), lambda i: (i, 0))], [jax.ShapeDtypeStruct((t + TM, DINP), F32)],
        [pltpu.VMEM((DIN, D), BF16)], _cp(48), (x, metapad, wm, wt3, tabs), comm)
    return proj, got


def _chunk_masks():
    r = lax.broadcasted_iota(jnp.int32, (TM, TM), 0)
    c = lax.broadcasted_iota(jnp.int32, (TM, TM), 1)
    same = (r // CH) == (c // CH)
    lower = _bf(jnp.where(same & (c <= r), 1.0, 0.0))
    upper = _bf(jnp.where(same & (c >= r), 1.0, 0.0))
    return lower, upper


def _gla_gate(lr, wgu, bg, valid, lower):
    z = _dot(_bf(lr), wgu) + bg
    g = (jnp.minimum(z, 0.0) - jnp.log(1.0 + jnp.exp(-jnp.abs(z)))) * (1.0 / 16.0)
    g = jnp.where(valid, g, 0.0)
    return z, _dot3(lower, g)


def _gla_decays(q, k, b):
    nc = TM // CH
    b3 = b.reshape(nc, CH, 256)
    blast = b3[:, CH - 1:CH, :]
    eb = jnp.exp(b)
    enb = jnp.exp(-b)
    ebl = jnp.exp(blast - b3).reshape(TM, 256)
    return eb, enb, ebl, jnp.exp(blast)


def _tri(lower_incl):
    r = lax.broadcasted_iota(jnp.int32, (CH, CH), 0)
    c = lax.broadcasted_iota(jnp.int32, (CH, CH), 1)
    return ((c <= r) if lower_incl else (c >= r))[None]


def _gla_fwd(proj, wgu, bg, gnw, t, comm=None):
    nblk = t // TM
    nt = nblk + 1
    nc = TM // CH

    def blk(i):
        return (i + nblk) % nt

    def body(q_ref, k_ref, v_ref, r_ref, lr_ref, wgu_ref, bg_ref, gnw_ref, o_ref, oraw_ref, sst_ref, b_ref, dgate_ref,
             st_scr):
        i = pl.program_id(0)

        @pl.when(i == 0)
        def _():
            st_scr[...] = jnp.zeros_like(st_scr)

        rows = blk(i) * TM + lax.broadcasted_iota(jnp.int32, (TM, 1), 0)
        lower, _ = _chunk_masks()
        valid = rows < t + NM
        z, b = _gla_gate(lr_ref[...], wgu_ref[...], bg_ref[...], valid, lower)
        b_ref[...] = b
        dgate_ref[...] = jnp.where(valid, (1.0 / 16.0) / (1.0 + jnp.exp(z)), 0.0)
        q = q_ref[...]
        k = k_ref[...]
        eb, enb, ebl, eblast = _gla_decays(q, k, b)
        qt = q * 0.125 * eb
        kt = k * enb
        kh = k * ebl
        tril = _tri(True)
        heads = range(4)
        hs = [slice(h * CH, (h + 1) * CH) for h in heads]
        qh = [_bf(qt[:, hs[h]]).reshape(nc, CH, CH) for h in heads]
        kth = [_bf(kt[:, hs[h]]).reshape(nc, CH, CH) for h in heads]
        khh = [_bf(kh[:, hs[h]]).reshape(nc, CH, CH) for h in heads]
        vh = [_bf(v_ref[:, h * 128:(h + 1) * 128]).reshape(nc, CH, 128) for h in heads]
        a = [jnp.einsum('cid,cjd->cij', qh[h], kth[h], preferred_element_type=F32) for h in heads]
        kv = [jnp.einsum('cjv,cjd->cvd', vh[h], khh[h], preferred_element_type=F32) for h in heads]
        o = [jnp.einsum('cij,cjv->civ', _bf(jnp.where(tril, a[h], 0.0)), vh[h], preferred_element_type=F32) for h in heads]
        states = []
        for h in heads:
            st = st_scr[h]
            per_chunk = []
            for c in range(nc):
                sst_ref[c, h] = st
                per_chunk.append(_bf(st))
                st = st * eblast[c, :, hs[h]] + kv[h][c]
            st_scr[h] = st
            states.append(per_chunk)
        o_inter = [[_dot_nt(qh[h][c], states[h][c]) for c in range(nc)] for h in heads]
        oraw = jnp.concatenate([(o[h] + jnp.stack(o_inter[h])).reshape(TM, 128) for h in heads], axis=1)
        oraw_ref[...] = oraw
        gn = gnw_ref[...]
        res = []
        for h in range(4):
            on, _ = _rms(oraw[:, h * 128:(h + 1) * 128])
            r = r_ref[:, h * 128:(h + 1) * 128]
            res.append(on * gn * (r * jax.nn.sigmoid(r)))
        o_ref[...] = _bf(jnp.concatenate(res, axis=1))

    def spec(w, cb):
        return pl.BlockSpec((TM, w), lambda i: (blk(i), cb))

    return _call(
        body, "gla_fwd", (nt,),
        [spec(256, 0), spec(256, 1), spec(512, 1), spec(512, 2), spec(128, C_LR // 128), VMEM_SPEC, VMEM_SPEC, VMEM_SPEC],
        [spec(512, 0), spec(512, 0), pl.BlockSpec((nc, 4, 128, CH), lambda i: (blk(i), 0, 0, 0)), spec(256, 0), spec(256, 0)],
        [jax.ShapeDtypeStruct((t + TM, 512), BF16), jax.ShapeDtypeStruct((t + TM, 512), F32),
         jax.ShapeDtypeStruct((nt * nc, 4, 128, CH), F32), jax.ShapeDtypeStruct((t + TM, 256), F32),
         jax.ShapeDtypeStruct((t + TM, 256), F32)],
        [pltpu.VMEM((4, 128, CH), F32)], _cp(40), (proj, proj, proj, proj, proj, wgu, bg, gnw), comm)


def _gla_bwd(proj, oraw, sst, bcum, dgate, do, wgu, gnw, t, comm=None):
    nblk = t // TM
    nt = nblk + 1
    nc = TM // CH

    def blk(i):
        return (2 * nblk - i) % nt

    def body(q_ref, k_ref, v_ref, r_ref, lr_ref, oraw_ref, sst_ref, b_ref, dgate_ref, do_ref, wgu_ref, gnw_ref,
             dgla_ref, dlr_ref, dwgu_ref, dbg_ref, dgnw_ref, dst_scr):
        i = pl.program_id(0)

        @pl.when(i == 0)
        def _():
            dst_scr[...] = jnp.zeros_like(dst_scr)
            dwgu_ref[...] = jnp.zeros_like(dwgu_ref)
            dbg_ref[...] = jnp.zeros_like(dbg_ref)
            dgnw_ref[...] = jnp.zeros_like(dgnw_ref)

        _, upper = _chunk_masks()
        lr = lr_ref[...]
        b = b_ref[...]
        q = q_ref[...]
        k = k_ref[...]
        eb, enb, ebl, eblast = _gla_decays(q, k, b)
        qt = q * 0.125 * eb
        kt = k * enb
        kh = k * ebl
        gn = gnw_ref[...]
        tril = _tri(True)
        triu = _tri(False)
        heads = range(4)
        hs = [slice(h * CH, (h + 1) * CH) for h in heads]
        vs = [slice(h * 128, (h + 1) * 128) for h in heads]
        ein = functools.partial(jnp.einsum, preferred_element_type=F32)
        dr_l, doh = [], []
        dgn = jnp.zeros((1, 128), F32)
        for h in heads:
            on, rs = _rms(oraw_ref[:, vs[h]])
            r = r_ref[:, vs[h]]
            sig = jax.nn.sigmoid(r)
            sil = r * sig
            dy = do_ref[:, vs[h]]
            dr_l.append(dy * on * gn * (sig * (1.0 + r * (1.0 - sig))))
            dgn = dgn + jnp.sum(dy * sil * on, axis=0, keepdims=True)
            doh.append(_bf(_rms_bwd(dy * sil, on, rs, gn)).reshape(nc, CH, 128))
        dgnw_ref[...] += dgn
        qh = [_bf(qt[:, hs[h]]).reshape(nc, CH, CH) for h in heads]
        kth = [_bf(kt[:, hs[h]]).reshape(nc, CH, CH) for h in heads]
        khh = [_bf(kh[:, hs[h]]).reshape(nc, CH, CH) for h in heads]
        vh = [_bf(v_ref[:, vs[h]]).reshape(nc, CH, 128) for h in heads]
        at = [ein('cjd,cid->cji', kth[h], qh[h]) for h in heads]
        da = [ein('civ,cjv->cij', doh[h], vh[h]) for h in heads]
        dat = [ein('cjv,civ->cji', vh[h], doh[h]) for h in heads]
        gq = [ein('civ,cid->cvd', doh[h], qh[h]) for h in heads]
        stf = [sst_ref[:, h] for h in heads]
        dqs = [ein('civ,cvd->cid', doh[h], _bf(stf[h])) for h in heads]
        dv = [ein('cji,civ->cjv', _bf(jnp.where(triu, at[h], 0.0)), doh[h]) for h in heads]
        dqt = [ein('cij,cjd->cid', _bf(jnp.where(tril, da[h], 0.0)), kth[h]) + dqs[h] for h in heads]
        dkt = [ein('cji,cid->cjd', _bf(jnp.where(triu, dat[h], 0.0)), qh[h]) for h in heads]
        dse = []
        for h in heads:
            dst = dst_scr[h]
            dsend = [None] * nc
            for c in reversed(range(nc)):
                dsend[c] = dst
                dst = dst * eblast[c, :, hs[h]] + gq[h][c]
            dst_scr[h] = dst
            dse.append(jnp.stack(dsend))
        dseb = [_bf(d) for d in dse]
        dv = [dv[h] + ein('cjd,cvd->cjv', khh[h], dseb[h]) for h in heads]
        dkh = [ein('cjv,cvd->cjd', vh[h], dseb[h]) for h in heads]
        carried = jnp.concatenate([jnp.sum(dse[h] * stf[h], axis=1, keepdims=True) for h in heads], axis=2)
        wide = lambda parts: jnp.concatenate([p.reshape(TM, CH) for p in parts], axis=1)
        dqt_w, dkt_w, dkh_w = wide(dqt), wide(dkt), wide(dkh)
        dkh_kh = dkh_w * kh
        extra = jnp.sum(dkh_kh.reshape(nc, CH, 256), axis=1, keepdims=True) + eblast * carried
        db = dqt_w * qt - dkt_w * kt - dkh_kh
        dg = _dot3(upper, db) + jnp.broadcast_to(extra, (nc, CH, 256)).reshape(TM, 256)
        dz = dg * dgate_ref[...]
        dzb = _bf(dz)
        dlr_ref[...] = _bf(_dot_nt(dzb, wgu_ref[...]))
        dwgu_ref[...] += _dot_tn(_bf(lr), dzb)
        dbg_ref[...] += jnp.sum(dz, axis=0, keepdims=True)
        dq = dqt_w * eb * 0.125
        dk = dkt_w * enb + dkh_w * ebl
        dgla_ref[...] = _bf(jnp.concatenate([dq, dk] + [d.reshape(TM, 128) for d in dv] + dr_l, axis=1))

    def spec(w, cb):
        return pl.BlockSpec((TM, w), lambda i: (blk(i), cb))

    def acc(shape):
        return pl.BlockSpec(shape, lambda i: (0, 0))

    return _call(
        body, "gla_bwd", (nt,),
        [spec(256, 0), spec(256, 1), spec(512, 1), spec(512, 2), spec(128, C_LR // 128), spec(512, 0),
         pl.BlockSpec((nc, 4, 128, CH), lambda i: (blk(i), 0, 0, 0)), spec(256, 0), spec(256, 0), spec(512, 0),
         VMEM_SPEC, VMEM_SPEC],
        [spec(1536, 0), spec(128, 0), acc((128, 256)), acc((1, 256)), acc((1, 128))],
        [jax.ShapeDtypeStruct((t + TM, 1536), BF16), jax.ShapeDtypeStruct((t + TM, 128), BF16),
         jax.ShapeDtypeStruct((128, 256), F32), jax.ShapeDtypeStruct((1, 256), F32), jax.ShapeDtypeStruct((1, 128), F32)],
        [pltpu.VMEM((4, 128, CH), F32)], _cp(48), (proj, proj, proj, proj, proj, oraw, sst, bcum, dgate, do, wgu, gnw), comm)


def _rope_tables(t):
    r = t + TM
    row = np.arange(r)
    pos = np.where(row < t, row + NM, np.where(row < t + NM, row - t, 0)).astype(np.float32)
    inv_freq = (1.0 / (np.float32(ROPE_THETA) ** (np.arange(0, 16, 2, dtype=np.float32) / np.float32(16)))).astype(np.float32)
    ang = (pos[:, None] * inv_freq[None, :]).astype(np.float32)
    cos, sin = np.cos(ang).astype(np.float32), np.sin(ang).astype(np.float32)
    one, zero = np.ones((r, 48), np.float32), np.zeros((r, 48), np.float32)
    return jnp.asarray(np.concatenate([cos, cos, one, -sin, sin, zero], axis=1))


def _rope(x, tab, sign):
    w = x.shape[1]
    rep = w // 64
    c = jnp.concatenate([tab[:, 0:64]] * rep, axis=1)
    s = jnp.concatenate([tab[:, 64:128]] * rep, axis=1)
    lane = lax.rem(lax.broadcasted_iota(jnp.int32, x.shape, 1), 64)
    partner = jnp.where(lane < 8, pltpu.roll(x, w - 8, 1), jnp.where(lane < 16, pltpu.roll(x, 8, 1), 0.0))
    return x * c + sign * (partner * s)


HG_FWD = 1
HB_BWD = 4


def _stack(x, hg):
    w = x.shape[1] // hg
    return x if hg == 1 else jnp.concatenate([x[:, g * w:(g + 1) * w] for g in range(hg)], axis=0)


def _unstack(x, hg):
    return x if hg == 1 else jnp.concatenate([x[g * SB:(g + 1) * SB] for g in range(hg)], axis=1)


def _swa_masks(b, nsb, hg):
    r = lax.rem(lax.broadcasted_iota(jnp.int32, (hg * SB, SB), 0), SB)
    c = lax.broadcasted_iota(jnp.int32, (hg * SB, SB), 1)
    real = b < nsb
    return c <= r, (c > r) & (b > 0) & real, (c < NM) & real


def _swa_specs(nsb):
    def rows(h, w, cb, f):
        return pl.BlockSpec((h, w), lambda i: (f(i), cb))
    pair = lambda i: i
    prev = lambda i: jnp.maximum(2 * i - 1, 0)
    meta = lambda i: nsb
    return rows, pair, prev, meta


def _swa_scores(b, nsb, hg, sink_ref, q, kc, kp, km):
    mc, mp, mm = _swa_masks(b, nsb, hg)
    groups = []
    for kv in range(2):
        ks = slice(kv * 64, (kv + 1) * 64)
        kcb, kpb, kmb = _bf(kc[:, ks]), _bf(kp[:, ks]), _bf(km[:, ks])
        for h0 in range(4 * kv, 4 * kv + 4, hg):
            qg = _bf(_stack(q[:, h0 * 64:(h0 + hg) * 64], hg))
            s_c = jnp.where(mc, _dot_nt(qg, kcb), NEG)
            s_p = jnp.where(mp, _dot_nt(qg, kpb), NEG)
            s_m = jnp.where(mm, _dot_nt(qg, kmb), NEG)
            sink = jnp.concatenate([jnp.full((SB, 1), sink_ref[0, h0 + g], F32) for g in range(hg)], axis=0)
            groups.append((kv, h0, qg, kcb, kpb, kmb, s_c, s_p, s_m, sink))
    return groups


def _swa_fwd(proj, sinks, t, comm=None):
    nsb = t // SB
    r_tot = t + TM
    rows, pair, prev, meta = _swa_specs(nsb)

    def body(sink_ref, q_ref, kc_ref, kp_ref, km_ref, vc_ref, vp_ref, vm_ref, o_ref, lse_ref):
        i = pl.program_id(0)
        km, vm = km_ref[...], vm_ref[...]
        for j in range(2):
            b = 2 * i + j
            rs = slice(j * SB, (j + 1) * SB)
            kp = kp_ref[...] if j == 0 else kc_ref[0:SB, :]
            vp = vp_ref[...] if j == 0 else vc_ref[0:SB, :]
            vc = vc_ref[rs, :]
            o_l, lse_l = [], []
            for kv, h0, qg, kcb, kpb, kmb, s_c, s_p, s_m, sink in _swa_scores(
                    b, nsb, HG_FWD, sink_ref, q_ref[rs, :], kc_ref[rs, :], kp, km):
                ks = slice(kv * 64, (kv + 1) * 64)
                m = jnp.maximum(jnp.max(jnp.maximum(jnp.maximum(s_c, s_p), s_m), -1, keepdims=True), sink)
                p_c, p_p, p_m = jnp.exp(s_c - m), jnp.exp(s_p - m), jnp.exp(s_m - m)
                l = jnp.sum(p_c + p_p + p_m, -1, keepdims=True) + jnp.exp(sink - m)
                o = _dot(_bf(p_c), _bf(vc[:, ks])) + _dot(_bf(p_p), _bf(vp[:, ks])) + _dot(_bf(p_m), _bf(vm[:, ks]))
                o_l.append(_unstack(o * (1.0 / l), HG_FWD))
                lse_l.append(_unstack(m + jnp.log(l), HG_FWD))
            valid = b * SB + lax.broadcasted_iota(jnp.int32, (SB, 1), 0) < t + NM
            o_ref[rs, :] = _bf(jnp.where(valid, jnp.concatenate(o_l, axis=1), 0.0))
            lse_ref[:, rs] = jnp.concatenate(lse_l, axis=1).T

    ck, cv = C_SK // 128, C_SV // 128
    return _call(
        body, "swa_fwd", (r_tot // QB,),
        [SMEM_SPEC, rows(QB, 512, C_SQ // 512, pair),
         rows(QB, 128, ck, pair), rows(SB, 128, ck, prev), rows(SB, 128, ck, meta),
         rows(QB, 128, cv, pair), rows(SB, 128, cv, prev), rows(SB, 128, cv, meta)],
        [rows(QB, 512, 0, pair), pl.BlockSpec((8, QB), lambda i: (0, i))],
        [jax.ShapeDtypeStruct((r_tot, 512), BF16), jax.ShapeDtypeStruct((8, r_tot), F32)],
        [], _cp(32), (sinks, proj, proj, proj, proj, proj, proj, proj), comm)


def _swa_bwd(proj, sinks, lse_t, do, t, comm=None):
    nsb = t // SB
    r_tot = t + TM
    rows, pair, prev, meta = _swa_specs(nsb)
    hb = HB_BWD
    lanes = hb * SB

    def body(sink_ref, q_ref, kc_ref, kp_ref, km_ref, vc_ref, vp_ref, vm_ref, lse_ref, do_ref,
             dq_ref, dk_ref, dv_ref, dsink_ref):
        i = pl.program_id(0)

        @pl.when(i == 0)
        def _():
            dk_ref[...] = jnp.zeros_like(dk_ref)
            dv_ref[...] = jnp.zeros_like(dv_ref)
            dsink_ref[...] = jnp.zeros_like(dsink_ref)

        key = lax.broadcasted_iota(jnp.int32, (SB, lanes), 0)
        qry = lax.rem(lax.broadcasted_iota(jnp.int32, (SB, lanes), 1), SB)
        km, vm = km_ref[...], vm_ref[...]
        dsink_l = []
        for j in range(2):
            b = 2 * i + j
            rs = slice(j * SB, (j + 1) * SB)
            real = b < nsb
            masks = (key <= qry, (key > qry) & (b > 0) & real, (key < NM) & real)
            k3 = (kc_ref[rs, :], kp_ref[...] if j == 0 else kc_ref[0:SB, :], km)
            v3 = (vc_ref[rs, :], vp_ref[...] if j == 0 else vc_ref[0:SB, :], vm)
            zero = jnp.zeros((SB, 64), F32)
            dq_l, ds_blk = [], []
            dk_l, dv_l = [[zero, zero] for _ in range(3)], [[zero, zero] for _ in range(3)]
            for h0 in range(0, 8, hb):
                kv = h0 // 4
                ks, hs = slice(kv * 64, (kv + 1) * 64), slice(h0 * 64, (h0 + hb) * 64)
                qg = _bf(_stack(q_ref[rs, hs], hb))
                dog = _bf(_stack(do_ref[rs, hs], hb))
                lse_row = jnp.concatenate([lse_ref[h:h + 1, rs] for h in range(h0, h0 + hb)], axis=1)
                sink_row = jnp.concatenate([jnp.full((1, SB), sink_ref[0, h], F32) for h in range(h0, h0 + hb)], axis=1)
                kb = [_bf(k[:, ks]) for k in k3]
                vb = [_bf(v[:, ks]) for v in v3]
                s = [_dot_nt(k, qg) for k in kb]
                dp = [_dot_nt(v, dog) for v in vb]
                p = [jnp.exp(jnp.where(m, sx, NEG) - lse_row) for m, sx in zip(masks, s)]
                delta = jnp.sum(p[0] * dp[0] + p[1] * dp[1] + p[2] * dp[2], axis=0, keepdims=True)
                ds = [_bf(pp * (dd - delta)) for pp, dd in zip(p, dp)]
                dq_t = _dot_tn(kb[0], ds[0]) + _dot_tn(kb[1], ds[1]) + _dot_tn(kb[2], ds[2])
                dq_l.append(_unstack(dq_t.T, hb))
                for x in range(3):
                    dk_l[x][kv] = dk_l[x][kv] + _dot(ds[x], qg)
                    dv_l[x][kv] = dv_l[x][kv] + _dot(_bf(p[x]), dog)
                ds_row = -jnp.exp(sink_row - lse_row) * delta
                ds_blk += [jnp.sum(ds_row[:, g * SB:(g + 1) * SB], axis=1, keepdims=True) for g in range(hb)]
            dsink_l.append(jnp.concatenate(ds_blk, axis=1))
            dq_ref[rs, :] = jnp.concatenate(dq_l, axis=1)
            starts = (pl.multiple_of(b * SB, SB), pl.multiple_of(jnp.maximum(b - 1, 0) * SB, SB), t)
            for x in range(3):
                dk_ref[pl.ds(starts[x], SB), :] += jnp.concatenate(dk_l[x], axis=1)
                dv_ref[pl.ds(starts[x], SB), :] += jnp.concatenate(dv_l[x], axis=1)
        dsink_ref[...] += dsink_l[0] + dsink_l[1]

    ck, cv = C_SK // 128, C_SV // 128
    whole = lambda w: pl.BlockSpec((r_tot, w), lambda i: (0, 0))
    return _call(
        body, "swa_bwd", (r_tot // QB,),
        [SMEM_SPEC, rows(QB, 512, C_SQ // 512, pair),
         rows(QB, 128, ck, pair), rows(SB, 128, ck, prev), rows(SB, 128, ck, meta),
         rows(QB, 128, cv, pair), rows(SB, 128, cv, prev), rows(SB, 128, cv, meta),
         pl.BlockSpec((8, QB), lambda i: (0, i)), rows(QB, 512, 1, pair)],
        [rows(QB, 512, 0, pair), whole(128), whole(128), pl.BlockSpec((1, 8), lambda i: (0, 0))],
        [jax.ShapeDtypeStruct((r_tot, 512), F32), jax.ShapeDtypeStruct((r_tot, 128), F32),
         jax.ShapeDtypeStruct((r_tot, 128), F32), jax.ShapeDtypeStruct((1, 8), F32)],
        [], _cp(48), (sinks, proj, proj, proj, proj, proj, proj, proj, lse_t, do), comm)


HK = D // 2


def _mlp_fwd(x, metapad, tgt, ogla, oswa, wo, wff, w1, w2, wfin):
    t = x.shape[0]
    nblk = t // TM

    def body(x_ref, mp_ref, tgt_ref, og_ref, os_ref, wo_ref, wff_ref, w1a_ref, w1b_ref, w2a_ref, w2b_ref, wfin_ref,
             h1_ref, f_ref, a_ref, dh2_ref, loss_ref, gfin_ref):
        i = pl.program_id(0)

        @pl.when(i == 0)
        def _():
            loss_ref[...] = jnp.zeros_like(loss_ref)
            gfin_ref[...] = jnp.zeros_like(gfin_ref)

        h0 = jnp.where(i == nblk, mp_ref[...], x_ref[...])
        h1 = h0 + _dot(og_ref[...], wo_ref[0:512, :]) + _dot(os_ref[...], wo_ref[512:1024, :])
        h1_ref[...] = h1
        fh, _ = _rms(h1)
        f = _bf(fh * wff_ref[...])
        f_ref[...] = f
        acc = jnp.zeros((TM, D), F32)
        for n in range(4):
            a = _dot(f[:, 0:HK], w1a_ref[n]) + _dot(f[:, HK:D], w1b_ref[n])
            a_ref[:, n * D:(n + 1) * D] = _bf(a)
            zr = jnp.maximum(a, 0.0)
            z = _bf(zr * zr)
            acc = acc + _dot(z[:, 0:HK], w2a_ref[n]) + _dot(z[:, HK:D], w2b_ref[n])
        h2 = h1 + acc
        yh, rs2 = _rms(h2)
        wf = wfin_ref[...]
        real = i < nblk
        e = jnp.where(real, yh * wf - tgt_ref[...], 0.0)
        loss_ref[...] += jnp.sum(jnp.sum(e * e, axis=0, keepdims=True), axis=1, keepdims=True) * (0.5 / D)
        dy = e * (1.0 / D)
        gfin_ref[...] += jnp.sum(dy * yh, axis=0, keepdims=True)
        dh2_ref[...] = _rms_bwd(dy, yh, rs2, wf)

    xs = pl.BlockSpec((TM, D), lambda i: (jnp.minimum(i, nblk - 1), 0))
    rs = lambda w: pl.BlockSpec((TM, w), lambda i: (i, 0))
    r_tot = t + TM
    return pl.pallas_call(
        body, name="mlp_fwd", grid=(nblk + 1,),
        in_specs=[xs, VMEM_SPEC, xs, rs(512), rs(512)] + [VMEM_SPEC] * 7,
        out_specs=[rs(D), rs(D), rs(DFF), rs(D), pl.BlockSpec((1, 1), lambda i: (0, 0)), pl.BlockSpec((1, D), lambda i: (0, 0))],
        out_shape=[jax.ShapeDtypeStruct((r_tot, D), F32), jax.ShapeDtypeStruct((r_tot, D), BF16),
                   jax.ShapeDtypeStruct((r_tot, DFF), BF16), jax.ShapeDtypeStruct((r_tot, D), F32),
                   jax.ShapeDtypeStruct((1, 1), F32), jax.ShapeDtypeStruct((1, D), F32)],
        compiler_params=_cp(56),
    )(x, metapad, tgt, ogla, oswa, wo, wff, *w1, *w2, wfin)


def _mlp_bwd(h1, a, dh2, ogla, oswa, wo, wff, w1, w2):
    r_tot = h1.shape[0]
    nt = r_tot // TM

    def body(h1_ref, a_ref, dh2_ref, og_ref, os_ref, wo_ref, wff_ref, w1a_ref, w1b_ref, w2a_ref, w2b_ref,
             da_ref, dh2b_ref, dh1_ref, do_ref, dwo_ref, gff_ref, dwo_acc):
        i = pl.program_id(0)

        @pl.when(i == 0)
        def _():
            dwo_acc[...] = jnp.zeros_like(dwo_acc)
            gff_ref[...] = jnp.zeros_like(gff_ref)

        dh2 = dh2_ref[...]
        dh2b = _bf(dh2)
        dh2b_ref[...] = dh2b
        dfa = jnp.zeros((TM, HK), F32)
        dfb = jnp.zeros((TM, HK), F32)
        for n in range(4):
            dz = jnp.concatenate([_dot_nt(dh2b, w2a_ref[n]), _dot_nt(dh2b, w2b_ref[n])], axis=1)
            da = _bf(dz * (2.0 * jnp.maximum(a_ref[:, n * D:(n + 1) * D].astype(F32), 0.0)))
            da_ref[:, n * D:(n + 1) * D] = da
            dfa = dfa + _dot_nt(da, w1a_ref[n])
            dfb = dfb + _dot_nt(da, w1b_ref[n])
        df = jnp.concatenate([dfa, dfb], axis=1)
        fh, rs1 = _rms(h1_ref[...])
        gff_ref[...] += jnp.sum(df * fh, axis=0, keepdims=True)
        dh1 = dh2 + _rms_bwd(df, fh, rs1, wff_ref[...])
        dh1_ref[...] = dh1
        dh1b = _bf(dh1)
        do_ref[...] = _dot_nt(dh1b, wo_ref[...])
        dwo_acc[0:512, :] += _dot_tn(og_ref[...], dh1b)
        dwo_acc[512:1024, :] += _dot_tn(os_ref[...], dh1b)

        @pl.when(i == nt - 1)
        def _():
            for s in range(4):
                for hh in range(2):
                    dwo_ref[hh, s] = dwo_acc[(2 * s + hh) * 128:(2 * s + hh + 1) * 128, :]

    rs = lambda w: pl.BlockSpec((TM, w), lambda i: (i, 0))
    return pl.pallas_call(
        body, name="mlp_bwd", grid=(nt,),
        in_specs=[rs(D), rs(DFF), rs(D), rs(512), rs(512)] + [VMEM_SPEC] * 6,
        out_specs=[rs(DFF), rs(D), rs(D), rs(D), VMEM_SPEC, pl.BlockSpec((1, D), lambda i: (0, 0))],
        out_shape=[jax.ShapeDtypeStruct((r_tot, DFF), BF16), jax.ShapeDtypeStruct((r_tot, D), BF16),
                   jax.ShapeDtypeStruct((r_tot, D), F32), jax.ShapeDtypeStruct((r_tot, D), F32),
                   jax.ShapeDtypeStruct((2, 4, 128, D), F32), jax.ShapeDtypeStruct((1, D), F32)],
        scratch_shapes=[pltpu.VMEM((D, D), F32)],
        compiler_params=_cp(56),
    )(h1, a, dh2, ogla, oswa, wo, wff, *w1, *w2)


def _ffn_wgrad(f, a, da, dh2b):
    r_tot = f.shape[0]
    kt = 768 if r_tot % 768 == 0 else TM
    nk = r_tot // kt

    def body(f_ref, a_ref, da_ref, dh2_ref, dw1_ref, dw2_ref, acc1, acc2):
        k = pl.program_id(1)

        @pl.when(k == 0)
        def _():
            acc1[...] = jnp.zeros_like(acc1)
            acc2[...] = jnp.zeros_like(acc2)

        zr = jnp.maximum(a_ref[...], 0.0)
        acc1[...] += _dot_tn(f_ref[...], da_ref[...])
        acc2[...] += _dot_tn(zr * zr, dh2_ref[...])

        @pl.when(k == nk - 1)
        def _():
            for hh in range(2):
                dw1_ref[hh, 0] = acc1[hh * 512:(hh + 1) * 512, :]
                dw2_ref[hh, 0] = acc2[hh * 512:(hh + 1) * 512, :]

    out = pl.BlockSpec((2, 1, 512, D), lambda n, k: (0, n, 0, 0))
    return pl.pallas_call(
        body, name="ffn_wgrad", grid=(4, nk),
        in_specs=[pl.BlockSpec((kt, D), lambda n, k: (k, 0)), pl.BlockSpec((kt, D), lambda n, k: (k, n)),
                  pl.BlockSpec((kt, D), lambda n, k: (k, n)), pl.BlockSpec((kt, D), lambda n, k: (k, 0))],
        out_specs=[out, out],
        out_shape=[jax.ShapeDtypeStruct((2, 4, 512, D), F32)] * 2,
        scratch_shapes=[pltpu.VMEM((D, D), F32), pltpu.VMEM((D, D), F32)],
        compiler_params=_cp(48, ("arbitrary", "arbitrary")),
    )(f, a, da, dh2b)


def _proj_bwd(x, metapad, wm, wt3, tabs, dgla, dswa_q, dsk, dsv, dlr, dh1, comm=None):
    t = x.shape[0]
    nblk = t // TM

    def body(x_ref, mp_ref, wm_ref, w3_ref, tab_ref, dg_ref, dq_ref, dk_ref, dv_ref, dlr_ref, dh1_ref,
             gx_ref, gmeta_ref, dw_ref, gmix_ref, w_ref, acc):
        i = pl.program_id(0)

        @pl.when(i == 0)
        def _():
            _join_shards(w3_ref, w_ref)
            acc[...] = jnp.zeros_like(acc)
            gmix_ref[...] = jnp.zeros_like(gmix_ref)

        h = jnp.where(i == nblk, mp_ref[...], x_ref[...])
        uh, rs = _rms(h)
        wm_v = wm_ref[...]
        u = _bf(uh * wm_v)
        tab = tab_ref[...]
        dq = _bf(_rope(dq_ref[...] * 0.125, tab, -1.0))
        dk = _bf(_rope(dk_ref[...], tab, -1.0))
        parts = ((dg_ref[...], 0, R_LR), (dlr_ref[:, 0:16], R_LR, 16), (dq, R_LR + 16, 512),
                 (dk, R_LR + 528, 128), (_bf(dv_ref[...]), R_LR + 656, 128))
        du = jnp.zeros((TM, D), F32)
        for val, r0, w in parts:
            du = du + _dot(val, w_ref[r0:r0 + w, :])
            acc[r0:r0 + w, :] += _dot_tn(val, u)
        gmix_ref[...] += jnp.sum(du * uh, axis=0, keepdims=True)
        dh0 = dh1_ref[...] + _rms_bwd(du, uh, rs, wm_v)

        @pl.when(i < nblk)
        def _():
            gx_ref[...] = dh0

        @pl.when(i == nblk)
        def _():
            gmeta_ref[...] = dh0[:NM]
            for s in range(4):
                dw_ref[s] = acc[(DIN // 4) * s:(DIN // 4) * (s + 1), :]

    xs = pl.BlockSpec((TM, D), lambda i: (jnp.minimum(i, nblk - 1), 0))
    rs_ = lambda w: pl.BlockSpec((TM, w), lambda i: (i, 0))
    return _call(
        body, "proj_bwd", (nblk + 1,),
        [xs, VMEM_SPEC, VMEM_SPEC, VMEM_SPEC, rs_(128), rs_(1536), rs_(512), rs_(128), rs_(128), rs_(128), rs_(D)],
        [xs, pl.BlockSpec((NM, D), lambda i: (0, 0)), VMEM_SPEC, pl.BlockSpec((1, D), lambda i: (0, 0))],
        [jax.ShapeDtypeStruct((t, D), F32), jax.ShapeDtypeStruct((NM, D), F32),
         jax.ShapeDtypeStruct((4, DIN // 4, D), F32), jax.ShapeDtypeStruct((1, D), F32)],
        [pltpu.VMEM((DIN, D), BF16), pltpu.VMEM((DIN, D), F32)], _cp(56),
        (x, metapad, wm, wt3, tabs, dgla, dswa_q, dsk, dsv, dlr, dh1), comm)


def _place():
    return lax.axis_index("x"), lax.axis_index("y"), lax.axis_index("c")


def _other_chips(x, y):
    return [(1 - x, y), (x, 1 - y), (1 - x, 1 - y)]


def _dma_sems(*counts):
    return tuple(pltpu.SemaphoreType.DMA((k,)) for k in counts)


def _gather_shards(shards, split):
    n = len(shards)
    two = [a for a in range(n) if split[a]]

    def plan(ins, outs, sems):
        isend, irecv, dsend, drecv, loc = sems
        x, y, c = _place()
        chips = _other_chips(x, y)

        def part(ref, a, half):
            if not split[a]:
                return ref
            w = shards[a].shape[1] // 2
            return ref.at[:, pl.ds(pl.multiple_of(half * w, 128), w)]

        def over_ici(a, k, shard_of):
            tx, ty = chips[k]
            sx, sy = shard_of
            return pltpu.make_async_remote_copy(
                src_ref=part(ins[a], a, c), dst_ref=part(outs[a].at[2 * sx + sy], a, c), send_sem=isend.at[3 * a + k],
                recv_sem=irecv.at[3 * a + k], device_id=(tx, ty, c), device_id_type=MESH)

        def over_d2d(a, k, half):
            tx, ty = chips[k]
            ref = part(outs[a].at[2 * tx + ty], a, half)
            return pltpu.make_async_remote_copy(
                src_ref=ref, dst_ref=ref, send_sem=dsend.at[3 * a + k], recv_sem=drecv.at[3 * a + k],
                device_id=(x, y, 1 - c), device_id_type=MESH)

        def local(a):
            return pltpu.make_async_copy(ins[a], outs[a].at[2 * x + y], loc.at[a])

        pairs = [(a, k) for a in range(n) for k in range(3)]
        first = ([lambda a=a: local(a).start() for a in range(n)]
                 + [lambda a=a, k=k: over_ici(a, k, (x, y)).start() for a, k in pairs],
                 [lambda a=a, k=k: over_ici(a, k, chips[k]).wait_recv() for a, k in pairs]
                 + [lambda a=a, k=k: over_ici(a, k, (x, y)).wait_send() for a, k in pairs]
                 + [lambda a=a: local(a).wait() for a in range(n)])
        pairs2 = [(a, k) for a in two for k in range(3)]
        second = ([lambda a=a, k=k: over_d2d(a, k, c).start() for a, k in pairs2],
                  [lambda a=a, k=k: over_d2d(a, k, 1 - c).wait_recv() for a, k in pairs2]
                  + [lambda a=a, k=k: over_d2d(a, k, c).wait_send() for a, k in pairs2])
        return [first, second] if two else [first]

    return _Comm(tuple(shards), tuple(jax.ShapeDtypeStruct((4,) + s.shape, s.dtype) for s in shards),
                 _dma_sems(3 * n, 3 * n, 3 * n, 3 * n, n), 2 if two else 1, plan)


def _swap_halves(grads):
    n = len(grads)

    def plan(ins, outs, sems):
        send, recv = sems
        x, y, c = _place()

        def swap(a):
            return pltpu.make_async_remote_copy(
                src_ref=ins[a].at[1 - c], dst_ref=outs[a], send_sem=send.at[a], recv_sem=recv.at[a],
                device_id=(x, y, 1 - c), device_id_type=MESH)

        return [([lambda a=a: swap(a).start() for a in range(n)], [lambda a=a: swap(a).wait() for a in range(n)])]

    return _Comm(tuple(grads), tuple(jax.ShapeDtypeStruct(g.shape[1:], g.dtype) for g in grads), _dma_sems(n, n), 1, plan)


def _scatter_shards(parts):
    n = len(parts)

    def plan(ins, outs, sems):
        send, recv = sems
        x, y, c = _place()
        chips = _other_chips(x, y)

        def scatter(a, k):
            tx, ty = chips[k]
            return pltpu.make_async_remote_copy(
                src_ref=ins[a].at[2 * tx + ty], dst_ref=outs[a].at[k], send_sem=send.at[3 * a + k],
                recv_sem=recv.at[3 * a + k], device_id=(tx, ty, c), device_id_type=MESH)

        pairs = [(a, k) for a in range(n) for k in range(3)]
        return [([lambda a=a, k=k: scatter(a, k).start() for a, k in pairs],
                 [lambda a=a, k=k: scatter(a, k).wait() for a, k in pairs])]

    return _Comm(tuple(parts), tuple(jax.ShapeDtypeStruct((3,) + p.shape[1:], p.dtype) for p in parts),
                 _dma_sems(3 * n, 3 * n), 1, plan)


def _join_halves(halves):
    n = len(halves)

    def plan(ins, outs, sems):
        send, recv, loc = sems
        x, y, c = _place()

        def remote(a, half):
            return pltpu.make_async_remote_copy(
                src_ref=ins[a], dst_ref=outs[a].at[half], send_sem=send.at[a], recv_sem=recv.at[a],
                device_id=(x, y, 1 - c), device_id_type=MESH)

        def local(a):
            return pltpu.make_async_copy(ins[a], outs[a].at[c], loc.at[a])

        every = range(n)
        return [([lambda a=a: local(a).start() for a in every] + [lambda a=a: remote(a, c).start() for a in every],
                 [lambda a=a: remote(a, 1 - c).wait_recv() for a in every]
                 + [lambda a=a: remote(a, c).wait_send() for a in every] + [lambda a=a: local(a).wait() for a in every])]

    return _Comm(tuple(halves), tuple(jax.ShapeDtypeStruct((2,) + h.shape, h.dtype) for h in halves),
                 _dma_sems(n, n, n), 1, plan)


def _reduce_w_in(dwt, comm):
    rows, hw = DIN // 4, D // 2
    ci, co = len(comm.ins), len(comm.outs)

    def body(*refs):
        dw_ref, c_in, out_ref, c_out = refs[0], refs[1:1 + ci], refs[1 + ci], refs[2 + ci:2 + ci + co]
        mine, sib, tosend, rbuf, qbuf, full, send, recv, loc = refs[2 + ci + co:11 + ci + co]
        c_sem = refs[11 + ci + co:]
        x, y, c = _place()
        sibling = (x, y, 1 - c)
        (starts, waits), = comm.plan(c_in, c_out, c_sem)
        _run_phase(starts)

        def cols(ref, half):
            window = pl.ds(pl.multiple_of(half * hw, 128), hw)
            return ref.at[:, :, window] if len(ref.shape) == 3 else ref.at[:, window]

        load = pltpu.make_async_copy(cols(dw_ref, c), mine, loc.at[0])
        give = pltpu.make_async_remote_copy(src_ref=cols(dw_ref, 1 - c), dst_ref=sib, send_sem=send.at[3], recv_sem=recv.at[3],
                                            device_id=sibling, device_id_type=MESH)
        load.start()
        give.start()
        load.wait()
        give.wait()
        mine[...] = mine[...] + sib[...]
        cps = []
        for k, (tx, ty) in enumerate(_other_chips(x, y)):
            tosend[k] = _bf(mine[2 * tx + ty])
            cps.append(pltpu.make_async_remote_copy(
                src_ref=tosend.at[k], dst_ref=rbuf.at[k], send_sem=send.at[k], recv_sem=recv.at[k],
                device_id=(tx, ty, c), device_id_type=MESH))
            cps[-1].start()
        for cp in cps:
            cp.wait()
        qbuf[...] = mine[2 * x + y] + rbuf[0].astype(F32) + rbuf[1].astype(F32) + rbuf[2].astype(F32)
        keep = pltpu.make_async_copy(qbuf, cols(full, c), loc.at[1])
        pass_on = pltpu.make_async_remote_copy(src_ref=qbuf, dst_ref=cols(full, c), send_sem=send.at[4], recv_sem=recv.at[4],
                                               device_id=sibling, device_id_type=MESH)
        keep.start()
        pass_on.start()
        keep.wait()
        pass_on.wait_send()
        pltpu.make_async_remote_copy(src_ref=qbuf, dst_ref=cols(full, 1 - c), send_sem=send.at[4], recv_sem=recv.at[4],
                                     device_id=sibling, device_id_type=MESH).wait_recv()
        out_ref[...] = full[...]
        _run_phase(waits)

    outs = pl.pallas_call(
        body, name="reduce_w_in",
        in_specs=[ANY_SPEC] * (1 + ci), out_specs=[VMEM_SPEC] + [ANY_SPEC] * co,
        out_shape=[jax.ShapeDtypeStruct((rows, D), F32)] + list(comm.outs),
        scratch_shapes=[pltpu.VMEM((4, rows, hw), F32), pltpu.VMEM((4, rows, hw), F32), pltpu.VMEM((3, rows, hw), BF16),
                        pltpu.VMEM((3, rows, hw), BF16), pltpu.VMEM((rows, hw), F32), pltpu.VMEM((rows, D), F32),
                        *_dma_sems(5, 5, 2), *comm.sems],
        compiler_params=pltpu.CompilerParams(vmem_limit_bytes=48 << 20),
    )(dwt, *comm.ins)
    return outs[0], outs[1:]


def _allreduce_small(pack):
    p = pack.shape[0]

    def body(in_ref, out_ref, buf, send, recv):
        x, y, c = _place()
        me = 4 * x + 2 * y + c
        buf[me] = in_ref[...]

        def peer_of(k):
            return x ^ (k >> 2), y ^ ((k >> 1) & 1), c ^ (k & 1)

        sends = [pltpu.make_async_remote_copy(
            src_ref=in_ref, dst_ref=buf.at[me], send_sem=send.at[k - 1], recv_sem=recv.at[k - 1],
            device_id=peer_of(k), device_id_type=MESH) for k in range(1, 8)]
        for cp in sends:
            cp.start()
        for k in range(1, 8):
            px, py, pc = peer_of(k)
            pltpu.make_async_remote_copy(
                src_ref=in_ref, dst_ref=buf.at[4 * px + 2 * py + pc], send_sem=send.at[k - 1], recv_sem=recv.at[k - 1],
                device_id=(x, y, c), device_id_type=MESH).wait_recv()
        for cp in sends:
            cp.wait_send()
        acc = buf[0]
        for d in range(1, 8):
            acc = acc + buf[d]
        out_ref[...] = acc

    return pl.pallas_call(
        body, name="allreduce_small",
        in_specs=[VMEM_SPEC], out_specs=VMEM_SPEC, out_shape=jax.ShapeDtypeStruct(pack.shape, F32),
        scratch_shapes=[pltpu.VMEM((8, p, D), F32), *_dma_sems(7, 7)],
    )(pack)


GRID4 = 4


def _sum_cores(core_shard, mine, theirs):
    n = len(mine)

    def body(cs_ref, *refs):
        ms, ts, bfs, owns = refs[:n], refs[n:2 * n], refs[2 * n:3 * n], refs[3 * n:]
        keep = pl.program_id(0) == cs_ref[1]
        for a in range(n):
            acc = ms[a][0, 0] + ts[a][0]
            bfs[a][0] = _bf(acc)

            @pl.when(keep)
            def _():
                owns[a][...] = acc

    shapes = [m.shape[2:] for m in mine]
    in_specs = ([pl.BlockSpec((1, 1) + s, lambda i, cs: (cs[0], i, 0, 0)) for s in shapes]
                + [pl.BlockSpec((1,) + s, lambda i, cs: (i, 0, 0)) for s in shapes])
    out_specs = ([pl.BlockSpec((1,) + s, lambda i, cs: (i, 0, 0)) for s in shapes]
                 + [pl.BlockSpec(s, lambda i, cs: (0, 0)) for s in shapes])
    outs = pl.pallas_call(
        body, name="sum_cores",
        grid_spec=pltpu.PrefetchScalarGridSpec(num_scalar_prefetch=1, grid=(4,), in_specs=in_specs, out_specs=out_specs),
        out_shape=[jax.ShapeDtypeStruct((4,) + s, BF16) for s in shapes] + [jax.ShapeDtypeStruct(s, F32) for s in shapes],
        compiler_params=_cp(48),
    )(core_shard, *mine, *theirs)
    return outs[:n], outs[n:]


def _sum_chips(own, arrived):
    n = len(own)

    def body(*refs):
        os_, ars, outs = refs[:n], refs[n:2 * n], refs[2 * n:]
        for a in range(n):
            outs[a][...] = os_[a][...] + ars[a][0].astype(F32) + ars[a][1].astype(F32) + ars[a][2].astype(F32)

    blocks = [(o.shape[0] // GRID4, o.shape[1]) for o in own]
    return pl.pallas_call(
        body, name="sum_chips", grid=(GRID4,),
        in_specs=([pl.BlockSpec(b, lambda i: (i, 0)) for b in blocks]
                  + [pl.BlockSpec((3,) + b, lambda i: (0, i, 0)) for b in blocks]),
        out_specs=[pl.BlockSpec(b, lambda i: (i, 0)) for b in blocks],
        out_shape=[jax.ShapeDtypeStruct(o.shape, F32) for o in own],
        compiler_params=_cp(32),
    )(*own, *arrived)


def _adamw_math(w, g, m, v):
    m2 = ADAM_B1 * m + (1.0 - ADAM_B1) * g
    v2 = ADAM_B2 * v + (1.0 - ADAM_B2) * (g * g)
    m_hat = m2 / (1.0 - ADAM_B1 ** ADAM_STEP)
    v_hat = v2 / (1.0 - ADAM_B2 ** ADAM_STEP)
    return -ADAM_LR * (m_hat / (jnp.sqrt(v_hat) + ADAM_EPS) + ADAM_WD * w), m2, v2


def _adamw_big(ws, gs, ms, vs):
    n = len(ws)

    def body(*refs):
        for a in range(n):
            d, m2, v2 = _adamw_math(refs[a][...], refs[n + a][...], refs[2 * n + a][...], refs[3 * n + a][...])
            refs[4 * n + a][...] = d
            refs[5 * n + a][...] = m2
            refs[6 * n + a][...] = v2

    specs = [pl.BlockSpec((w.shape[0] // GRID4, w.shape[1]), lambda i: (i, 0)) for w in ws]
    return pl.pallas_call(
        body, name="adamw_big", grid=(GRID4,),
        in_specs=specs * 4, out_specs=specs * 3,
        out_shape=[jax.ShapeDtypeStruct(w.shape, F32) for w in ws] * 3,
        compiler_params=_cp(48),
    )(*ws, *gs, *ms, *vs)


def _adamw_small(ws, gs, ms, vs):
    n = len(ws)

    def body(*refs):
        for a in range(n):
            d, m2, v2 = _adamw_math(refs[a][...], refs[n + a][...], refs[2 * n + a][...], refs[3 * n + a][...])
            refs[4 * n + a][...] = d
            refs[5 * n + a][...] = m2
            refs[6 * n + a][...] = v2

    return pl.pallas_call(
        body, name="adamw_small",
        in_specs=[VMEM_SPEC] * (4 * n), out_specs=[VMEM_SPEC] * (3 * n),
        out_shape=[jax.ShapeDtypeStruct(w.shape, F32) for w in ws] * 3,
        compiler_params=pltpu.CompilerParams(vmem_limit_bytes=40 << 20),
    )(*ws, *gs, *ms, *vs)


def kernel(x, meta_tokens, norm_mix_w, w_in, w_gate_up, b_gate, gla_norm_w, sinks, w_out, norm_ff_w, w_ff1, w_ff2, final_norm_w, loss_target, m_meta_tokens, m_norm_mix_w, m_w_in, m_w_gate_up, m_b_gate, m_gla_norm_w, m_sinks, m_w_out, m_norm_ff_w, m_w_ff1, m_w_ff2, m_final_norm_w, v_meta_tokens, v_norm_mix_w, v_w_in, v_w_gate_up, v_b_gate, v_gla_norm_w, v_sinks, v_w_out, v_norm_ff_w, v_w_ff1, v_w_ff2, v_final_norm_w):
    xi, yi, ci = _place()
    shard = (2 * xi + yi).astype(jnp.int32).reshape(1)
    core = ci.astype(jnp.int32).reshape(1)

    small = jnp.concatenate([meta_tokens, w_gate_up[0], jnp.zeros((NM, 64), F32)], axis=1)
    wt3, g_small = _run_comm(_gather_shards([_bf(w_in[0].T), small], [True, False]), "gather_w_in")
    meta = g_small[:, :, 0:256].transpose(1, 0, 2).reshape(NM, D)
    wgu = g_small[:, :, 256:320].transpose(1, 0, 2).reshape(NM, 256)

    xs, tgt = x[0], loss_target[0]
    t = xs.shape[0]
    wfin = final_norm_w.reshape(1, D)
    metapad = jnp.concatenate([meta, jnp.zeros((TM - NM, D), F32)], axis=0)
    wgu_p = _bf(jnp.concatenate([wgu, jnp.zeros((128 - 16, 256), F32)], axis=0))
    tabs = _rope_tables(t)

    w1s, w2s = _bf(w_ff1[0]), _bf(w_ff2[0])
    proj, (g_out, w1a) = _proj_fwd(xs, metapad, norm_mix_w, wt3, tabs,
                                   _gather_shards([_bf(w_out[0]), w1s[:HK]], [True] * 2))
    (oswa, lse), (w2a, w2b) = _swa_fwd(proj, sinks, t, _gather_shards([w2s[:HK], w2s[HK:]], [True] * 2))
    (ogla, oraw, sst, bcum, dgate), (w1b,) = _gla_fwd(proj, wgu_p, b_gate, gla_norm_w, t,
                                                      _gather_shards([w1s[HK:]], [True]))
    wo, w1, w2 = g_out.reshape(D, D), (w1a, w1b), (w2a, w2b)
    h1, f, a, dh2, loss, gfin = _mlp_fwd(xs, metapad, tgt, ogla, oswa, wo, norm_ff_w, w1, w2, wfin)

    da, dh2b, dh1, do, dwo, gff = _mlp_bwd(h1, a, dh2, ogla, oswa, wo, norm_ff_w, w1, w2)
    dw1, dw2 = _ffn_wgrad(f, a, da, dh2b)
    big = [dwo, dw1, dw2]
    (dsq, dsk, dsv, dsink), theirs = _swa_bwd(proj, sinks, lse, do, t, _swap_halves(big))
    sums_bf, own = _sum_cores(jnp.concatenate([core, shard]), big, theirs)
    (dgla, dlr, dwgu, dbg, dgnw), arrived = _gla_bwd(proj, oraw, sst, bcum, dgate, do, wgu_p, gla_norm_w, t,
                                                     _scatter_shards(sums_bf))
    halves = _sum_chips(own, arrived)
    (gx, gmeta, dwt, gmix), _ = _proj_bwd(xs, metapad, norm_mix_w, wt3, tabs, dgla, dsq, dsk, dsv, dlr, dh1)

    gwt_in, joined = _reduce_w_in(dwt, _join_halves(halves))
    gw_out, gw_1, gw_2 = [j.reshape((-1, j.shape[2])) for j in joined]

    tail = jnp.concatenate([dbg, dgnw, dsink, loss, jnp.zeros((1, D - 256 - 128 - 8 - 1), F32)], axis=1)
    pack = jnp.concatenate([gmeta, gmix, gff, gfin, tail, dwgu[:16].reshape(4, D)], axis=0)
    tot = _allreduce_small(pack)
    g_meta = lax.dynamic_slice_in_dim(tot[0:NM], shard[0] * 256, 256, axis=1)
    g_mix, g_ff, g_fin = tot[16:17], tot[17:18], tot[18]
    g_bg, g_gnw, g_sinks, loss_tot = tot[19:20, 0:256], tot[19:20, 256:384], tot[19:20, 384:392], tot[19, 392]
    g_wgu = lax.dynamic_slice_in_dim(tot[20:24].reshape(NM, 256), shard[0] * 64, 64, axis=1)

    bo = _adamw_big([w_out[0], w_ff1[0], w_ff2[0]], [gw_out, gw_1, gw_2], [m_w_out[0], m_w_ff1[0], m_w_ff2[0]],
                    [v_w_out[0], v_w_ff1[0], v_w_ff2[0]])

    fin2 = lambda a: a.reshape(1, D)
    sw = [meta_tokens, norm_mix_w, w_gate_up[0], b_gate, gla_norm_w, sinks, norm_ff_w, fin2(final_norm_w), w_in[0].T]
    sg = [g_meta, g_mix, g_wgu, g_bg, g_gnw, g_sinks, g_ff, fin2(g_fin), gwt_in]
    sm = [m_meta_tokens, m_norm_mix_w, m_w_gate_up[0], m_b_gate, m_gla_norm_w, m_sinks, m_norm_ff_w, fin2(m_final_norm_w),
          m_w_in[0].T]
    sv = [v_meta_tokens, v_norm_mix_w, v_w_gate_up[0], v_b_gate, v_gla_norm_w, v_sinks, v_norm_ff_w, fin2(v_final_norm_w),
          v_w_in[0].T]
    so = _adamw_small(sw, sg, sm, sv)

    def ordered(small_o, big_o):
        meta_, mix_, wgu_, bg_, gnw_, sinks_, ff_, fin_, wt_ = small_o
        w_out_, w_1_, w_2_ = big_o
        return (meta_, mix_, wt_.T[None], wgu_[None], bg_, gnw_, sinks_, w_out_[None], ff_, w_1_[None], w_2_[None],
                fin_.reshape(D))

    grads = ordered(sg, [gw_out, gw_1, gw_2])
    deltas = ordered(so[0:9], bo[0:3])
    new_m = ordered(so[9:18], bo[3:6])
    new_v = ordered(so[18:27], bo[6:9])
    return (loss_tot, gx[None], *grads, *deltas, *new_m, *new_v)
```

```python
import functools
from typing import Callable, NamedTuple

import jax
import jax.numpy as jnp
import numpy as np
from jax import lax
from jax.experimental import pallas as pl
from jax.experimental.pallas import tpu as pltpu

F32 = jnp.float32
BF16 = jnp.bfloat16

D = 1024
DFF = 4096
NM = 16
TM = 256
CH = 64
SB = 128
QB = 2 * SB
EPS = 1e-5
C_GQ, C_GK, C_GV, C_GR, C_SQ, C_SK, C_SV, C_LR, DINP = 0, 256, 512, 1024, 1536, 2048, 2176, 2304, 2432
DIN = 2320
R_LR = 1536
ROPE_THETA = 500000.0
ADAM_LR, ADAM_B1, ADAM_B2, ADAM_EPS, ADAM_WD, ADAM_STEP = 0.001, 0.9, 0.999, 1e-08, 0.01, 10
NEG = -1e30
MESH = pl.DeviceIdType.MESH
VMEM_SPEC = pl.BlockSpec(memory_space=pltpu.VMEM)
ANY_SPEC = pl.BlockSpec(memory_space=pl.ANY)
SMEM_SPEC = pl.BlockSpec(memory_space=pltpu.SMEM)


def _cp(vmem_mb, sem=("arbitrary",)):
    return pltpu.CompilerParams(dimension_semantics=sem, vmem_limit_bytes=vmem_mb << 20)


def _dot(a, b):
    return jnp.dot(a, b, preferred_element_type=F32)


def _dot_nt(a, b):
    return lax.dot_general(a, b, (((1,), (1,)), ((), ())), preferred_element_type=F32)


def _dot_tn(a, b):
    return lax.dot_general(a, b, (((0,), (0,)), ((), ())), preferred_element_type=F32)


def _bf(x):
    return x.astype(BF16)


def _dot3(m01, x):
    x1 = _bf(x)
    r1 = x - x1.astype(F32)
    x2 = _bf(r1)
    x3 = _bf(r1 - x2.astype(F32))
    return _dot(m01, x1) + _dot(m01, x2) + _dot(m01, x3)


def _rms(h):
    rs = lax.rsqrt(jnp.mean(h * h, axis=-1, keepdims=True) + EPS)
    return h * rs, rs


def _rms_bwd(dy, yhat, rs, w):
    dyh = dy * w
    return rs * (dyh - yhat * jnp.mean(dyh * yhat, axis=-1, keepdims=True))


class _Comm(NamedTuple):
    ins: tuple
    outs: tuple
    sems: tuple
    phases: int
    plan: Callable


def _run_phase(fns):
    for fn in fns:
        fn()


def _call(body, name, grid, in_specs, out_specs, out_shape, scratch, params, args, comm=None):
    if comm is None:
        outs = pl.pallas_call(body, name=name, grid=grid, in_specs=in_specs, out_specs=out_specs, out_shape=out_shape,
                              scratch_shapes=scratch, compiler_params=params)(*args)
        return outs, None
    n_in, n_out, n_scr = len(in_specs), len(out_specs), len(scratch)
    ci, co = len(comm.ins), len(comm.outs)
    last = grid[0] - 1
    marks = [0, max(1, last - max(2, (last + 1) // 6))][:comm.phases]

    def wrapped(*refs):
        own_in, c_in = refs[:n_in], refs[n_in:n_in + ci]
        refs = refs[n_in + ci:]
        own_out, c_out = refs[:n_out], refs[n_out:n_out + co]
        refs = refs[n_out + co:]
        own_scr, c_sem = refs[:n_scr], refs[n_scr:]
        i = pl.program_id(0)

        for p, mark in enumerate(marks):
            @pl.when(i == mark)
            def _():
                plan = comm.plan(c_in, c_out, c_sem)
                if p > 0:
                    _run_phase(plan[p - 1][1])
                _run_phase(plan[p][0])

        body(*own_in, *own_out, *own_scr)

        @pl.when(i == last)
        def _():
            _run_phase(comm.plan(c_in, c_out, c_sem)[-1][1])

    outs = pl.pallas_call(
        wrapped, name=name, grid=grid, in_specs=list(in_specs) + [ANY_SPEC] * ci, out_specs=list(out_specs) + [ANY_SPEC] * co,
        out_shape=list(out_shape) + list(comm.outs), scratch_shapes=list(scratch) + list(comm.sems), compiler_params=params,
    )(*args, *comm.ins)
    return outs[:n_out], outs[n_out:]


def _run_comm(comm, name):
    ci, co = len(comm.ins), len(comm.outs)

    def body(*refs):
        for starts, waits in comm.plan(refs[:ci], refs[ci:ci + co], refs[ci + co:]):
            _run_phase(starts)
            _run_phase(waits)

    return pl.pallas_call(body, name=name, in_specs=[ANY_SPEC] * ci, out_specs=[ANY_SPEC] * co, out_shape=list(comm.outs),
                          scratch_shapes=list(comm.sems))(*comm.ins)


def _join_shards(w3_ref, w_ref):
    for s in range(4):
        w_ref[(DIN // 4) * s:(DIN // 4) * (s + 1), :] = w3_ref[s]


def _proj_fwd(x, metapad, wm, wt3, tabs, comm=None):
    t = x.shape[0]
    nblk = t // TM

    def body(x_ref, mp_ref, wm_ref, w3_ref, tab_ref, proj_ref, w_ref):
        i = pl.program_id(0)

        @pl.when(i == 0)
        def _():
            _join_shards(w3_ref, w_ref)

        h = jnp.where(i == nblk, mp_ref[...], x_ref[...])
        u, _ = _rms(h)
        ub = _bf(u * wm_ref[...])
        proj_ref[:, 0:C_SQ] = _dot_nt(ub, w_ref[0:R_LR, :])
        att = _dot_nt(ub, w_ref[R_LR + 16:DIN, :])
        tab = tab_ref[...]
        proj_ref[:, C_SQ:C_SK] = _rope(att[:, 0:512], tab, 1.0) * 0.125
        proj_ref[:, C_SK:C_SV] = _rope(att[:, 512:640], tab, 1.0)
        proj_ref[:, C_SV:C_LR] = att[:, 640:768]
        proj_ref[:, C_LR:DINP] = jnp.zeros((TM, DINP - C_LR), F32)
        proj_ref[:, C_LR:C_LR + 16] = _dot_nt(ub, w_ref[R_LR:R_LR + 16, :])

    (proj,), got = _call(
        body, "proj_fwd", (nblk + 1,),
        [pl.BlockSpec((TM, D), lambda i: (jnp.minimum(i, nblk - 1), 0)), VMEM_SPEC, VMEM_SPEC, VMEM_SPEC,
         pl.BlockSpec((TM, 128), lambda i: (i, 0))],
        [pl.BlockSpec((TM, DINP), lambda i: (i, 0))], [jax.ShapeDtypeStruct((t + TM, DINP), F32)],
        [pltpu.VMEM((DIN, D), BF16)], _cp(48), (x, metapad, wm, wt3, tabs), comm)
    return proj, got


def _chunk_masks():
    r = lax.broadcasted_iota(jnp.int32, (TM, TM), 0)
    c = lax.broadcasted_iota(jnp.int32, (TM, TM), 1)
    same = (r // CH) == (c // CH)
    lower = _bf(jnp.where(same & (c <= r), 1.0, 0.0))
    upper = _bf(jnp.where(same & (c >= r), 1.0, 0.0))
    return lower, upper


def _gla_gate(lr, wgu, bg, valid, lower):
    z = _dot(_bf(lr), wgu) + bg
    g = (jnp.minimum(z, 0.0) - jnp.log(1.0 + jnp.exp(-jnp.abs(z)))) * (1.0 / 16.0)
    g = jnp.where(valid, g, 0.0)
    return z, _dot3(lower, g)


def _gla_decays(q, k, b):
    nc = TM // CH
    b3 = b.reshape(nc, CH, 256)
    blast = b3[:, CH - 1:CH, :]
    eb = jnp.exp(b)
    enb = jnp.exp(-b)
    ebl = jnp.exp(blast - b3).reshape(TM, 256)
    return eb, enb, ebl, jnp.exp(blast)


def _tri(lower_incl):
    r = lax.broadcasted_iota(jnp.int32, (CH, CH), 0)
    c = lax.broadcasted_iota(jnp.int32, (CH, CH), 1)
    return ((c <= r) if lower_incl else (c >= r))[None]


def _gla_fwd(proj, wgu, bg, gnw, t, comm=None):
    nblk = t // TM
    nt = nblk + 1
    nc = TM // CH

    def blk(i):
        return (i + nblk) % nt

    def body(q_ref, k_ref, v_ref, r_ref, lr_ref, wgu_ref, bg_ref, gnw_ref, o_ref, oraw_ref, sst_ref, b_ref, dgate_ref,
             st_scr):
        i = pl.program_id(0)

        @pl.when(i == 0)
        def _():
            st_scr[...] = jnp.zeros_like(st_scr)

        rows = blk(i) * TM + lax.broadcasted_iota(jnp.int32, (TM, 1), 0)
        lower, _ = _chunk_masks()
        valid = rows < t + NM
        z, b = _gla_gate(lr_ref[...], wgu_ref[...], bg_ref[...], valid, lower)
        b_ref[...] = b
        dgate_ref[...] = jnp.where(valid, (1.0 / 16.0) / (1.0 + jnp.exp(z)), 0.0)
        q = q_ref[...]
        k = k_ref[...]
        eb, enb, ebl, eblast = _gla_decays(q, k, b)
        qt = q * 0.125 * eb
        kt = k * enb
        kh = k * ebl
        tril = _tri(True)
        heads = range(4)
        hs = [slice(h * CH, (h + 1) * CH) for h in heads]
        qh = [_bf(qt[:, hs[h]]).reshape(nc, CH, CH) for h in heads]
        kth = [_bf(kt[:, hs[h]]).reshape(nc, CH, CH) for h in heads]
        khh = [_bf(kh[:, hs[h]]).reshape(nc, CH, CH) for h in heads]
        vh = [_bf(v_ref[:, h * 128:(h + 1) * 128]).reshape(nc, CH, 128) for h in heads]
        a = [jnp.einsum('cid,cjd->cij', qh[h], kth[h], preferred_element_type=F32) for h in heads]
        kv = [jnp.einsum('cjv,cjd->cvd', vh[h], khh[h], preferred_element_type=F32) for h in heads]
        o = [jnp.einsum('cij,cjv->civ', _bf(jnp.where(tril, a[h], 0.0)), vh[h], preferred_element_type=F32) for h in heads]
        states = []
        for h in heads:
            st = st_scr[h]
            per_chunk = []
            for c in range(nc):
                sst_ref[c, h] = st
                per_chunk.append(_bf(st))
                st = st * eblast[c, :, hs[h]] + kv[h][c]
            st_scr[h] = st
            states.append(per_chunk)
        o_inter = [[_dot_nt(qh[h][c], states[h][c]) for c in range(nc)] for h in heads]
        oraw = jnp.concatenate([(o[h] + jnp.stack(o_inter[h])).reshape(TM, 128) for h in heads], axis=1)
        oraw_ref[...] = oraw
        gn = gnw_ref[...]
        res = []
        for h in range(4):
            on, _ = _rms(oraw[:, h * 128:(h + 1) * 128])
            r = r_ref[:, h * 128:(h + 1) * 128]
            res.append(on * gn * (r * jax.nn.sigmoid(r)))
        o_ref[...] = _bf(jnp.concatenate(res, axis=1))

    def spec(w, cb):
        return pl.BlockSpec((TM, w), lambda i: (blk(i), cb))

    return _call(
        body, "gla_fwd", (nt,),
        [spec(256, 0), spec(256, 1), spec(512, 1), spec(512, 2), spec(128, C_LR // 128), VMEM_SPEC, VMEM_SPEC, VMEM_SPEC],
        [spec(512, 0), spec(512, 0), pl.BlockSpec((nc, 4, 128, CH), lambda i: (blk(i), 0, 0, 0)), spec(256, 0), spec(256, 0)],
        [jax.ShapeDtypeStruct((t + TM, 512), BF16), jax.ShapeDtypeStruct((t + TM, 512), F32),
         jax.ShapeDtypeStruct((nt * nc, 4, 128, CH), F32), jax.ShapeDtypeStruct((t + TM, 256), F32),
         jax.ShapeDtypeStruct((t + TM, 256), F32)],
        [pltpu.VMEM((4, 128, CH), F32)], _cp(40), (proj, proj, proj, proj, proj, wgu, bg, gnw), comm)


def _gla_bwd(proj, oraw, sst, bcum, dgate, do, wgu, gnw, t, comm=None):
    nblk = t // TM
    nt = nblk + 1
    nc = TM // CH

    def blk(i):
        return (2 * nblk - i) % nt

    def body(q_ref, k_ref, v_ref, r_ref, lr_ref, oraw_ref, sst_ref, b_ref, dgate_ref, do_ref, wgu_ref, gnw_ref,
             dgla_ref, dlr_ref, dwgu_ref, dbg_ref, dgnw_ref, dst_scr):
        i = pl.program_id(0)

        @pl.when(i == 0)
        def _():
            dst_scr[...] = jnp.zeros_like(dst_scr)
            dwgu_ref[...] = jnp.zeros_like(dwgu_ref)
            dbg_ref[...] = jnp.zeros_like(dbg_ref)
            dgnw_ref[...] = jnp.zeros_like(dgnw_ref)

        _, upper = _chunk_masks()
        lr = lr_ref[...]
        b = b_ref[...]
        q = q_ref[...]
        k = k_ref[...]
        eb, enb, ebl, eblast = _gla_decays(q, k, b)
        qt = q * 0.125 * eb
        kt = k * enb
        kh = k * ebl
        gn = gnw_ref[...]
        tril = _tri(True)
        triu = _tri(False)
        heads = range(4)
        hs = [slice(h * CH, (h + 1) * CH) for h in heads]
        vs = [slice(h * 128, (h + 1) * 128) for h in heads]
        ein = functools.partial(jnp.einsum, preferred_element_type=F32)
        dr_l, doh = [], []
        dgn = jnp.zeros((1, 128), F32)
        for h in heads:
            on, rs = _rms(oraw_ref[:, vs[h]])
            r = r_ref[:, vs[h]]
            sig = jax.nn.sigmoid(r)
            sil = r * sig
            dy = do_ref[:, vs[h]]
            dr_l.append(dy * on * gn * (sig * (1.0 + r * (1.0 - sig))))
            dgn = dgn + jnp.sum(dy * sil * on, axis=0, keepdims=True)
            doh.append(_bf(_rms_bwd(dy * sil, on, rs, gn)).reshape(nc, CH, 128))
        dgnw_ref[...] += dgn
        qh = [_bf(qt[:, hs[h]]).reshape(nc, CH, CH) for h in heads]
        kth = [_bf(kt[:, hs[h]]).reshape(nc, CH, CH) for h in heads]
        khh = [_bf(kh[:, hs[h]]).reshape(nc, CH, CH) for h in heads]
        vh = [_bf(v_ref[:, vs[h]]).reshape(nc, CH, 128) for h in heads]
        at = [ein('cjd,cid->cji', kth[h], qh[h]) for h in heads]
        da = [ein('civ,cjv->cij', doh[h], vh[h]) for h in heads]
        dat = [ein('cjv,civ->cji', vh[h], doh[h]) for h in heads]
        gq = [ein('civ,cid->cvd', doh[h], qh[h]) for h in heads]
        stf = [sst_ref[:, h] for h in heads]
        dqs = [ein('civ,cvd->cid', doh[h], _bf(stf[h])) for h in heads]
        dv = [ein('cji,civ->cjv', _bf(jnp.where(triu, at[h], 0.0)), doh[h]) for h in heads]
        dqt = [ein('cij,cjd->cid', _bf(jnp.where(tril, da[h], 0.0)), kth[h]) + dqs[h] for h in heads]
        dkt = [ein('cji,cid->cjd', _bf(jnp.where(triu, dat[h], 0.0)), qh[h]) for h in heads]
        dse = []
        for h in heads:
            dst = dst_scr[h]
            dsend = [None] * nc
            for c in reversed(range(nc)):
                dsend[c] = dst
                dst = dst * eblast[c, :, hs[h]] + gq[h][c]
            dst_scr[h] = dst
            dse.append(jnp.stack(dsend))
        dseb = [_bf(d) for d in dse]
        dv = [dv[h] + ein('cjd,cvd->cjv', khh[h], dseb[h]) for h in heads]
        dkh = [ein('cjv,cvd->cjd', vh[h], dseb[h]) for h in heads]
        carried = jnp.concatenate([jnp.sum(dse[h] * stf[h], axis=1, keepdims=True) for h in heads], axis=2)
        wide = lambda parts: jnp.concatenate([p.reshape(TM, CH) for p in parts], axis=1)
        dqt_w, dkt_w, dkh_w = wide(dqt), wide(dkt), wide(dkh)
        dkh_kh = dkh_w * kh
        extra = jnp.sum(dkh_kh.reshape(nc, CH, 256), axis=1, keepdims=True) + eblast * carried
        db = dqt_w * qt - dkt_w * kt - dkh_kh
        dg = _dot3(upper, db) + jnp.broadcast_to(extra, (nc, CH, 256)).reshape(TM, 256)
        dz = dg * dgate_ref[...]
        dzb = _bf(dz)
        dlr_ref[...] = _bf(_dot_nt(dzb, wgu_ref[...]))
        dwgu_ref[...] += _dot_tn(_bf(lr), dzb)
        dbg_ref[...] += jnp.sum(dz, axis=0, keepdims=True)
        dq = dqt_w * eb * 0.125
        dk = dkt_w * enb + dkh_w * ebl
        dgla_ref[...] = _bf(jnp.concatenate([dq, dk] + [d.reshape(TM, 128) for d in dv] + dr_l, axis=1))

    def spec(w, cb):
        return pl.BlockSpec((TM, w), lambda i: (blk(i), cb))

    def acc(shape):
        return pl.BlockSpec(shape, lambda i: (0, 0))

    return _call(
        body, "gla_bwd", (nt,),
        [spec(256, 0), spec(256, 1), spec(512, 1), spec(512, 2), spec(128, C_LR // 128), spec(512, 0),
         pl.BlockSpec((nc, 4, 128, CH), lambda i: (blk(i), 0, 0, 0)), spec(256, 0), spec(256, 0), spec(512, 0),
         VMEM_SPEC, VMEM_SPEC],
        [spec(1536, 0), spec(128, 0), acc((128, 256)), acc((1, 256)), acc((1, 128))],
        [jax.ShapeDtypeStruct((t + TM, 1536), BF16), jax.ShapeDtypeStruct((t + TM, 128), BF16),
         jax.ShapeDtypeStruct((128, 256), F32), jax.ShapeDtypeStruct((1, 256), F32), jax.ShapeDtypeStruct((1, 128), F32)],
        [pltpu.VMEM((4, 128, CH), F32)], _cp(48), (proj, proj, proj, proj, proj, oraw, sst, bcum, dgate, do, wgu, gnw), comm)


def _rope_tables(t):
    r = t + TM
    row = np.arange(r)
    pos = np.where(row < t, row + NM, np.where(row < t + NM, row - t, 0)).astype(np.float32)
    inv_freq = (1.0 / (np.float32(ROPE_THETA) ** (np.arange(0, 16, 2, dtype=np.float32) / np.float32(16)))).astype(np.float32)
    ang = (pos[:, None] * inv_freq[None, :]).astype(np.float32)
    cos, sin = np.cos(ang).astype(np.float32), np.sin(ang).astype(np.float32)
    one, zero = np.ones((r, 48), np.float32), np.zeros((r, 48), np.float32)
    return jnp.asarray(np.concatenate([cos, cos, one, -sin, sin, zero], axis=1))


def _rope(x, tab, sign):
    w = x.shape[1]
    rep = w // 64
    c = jnp.concatenate([tab[:, 0:64]] * rep, axis=1)
    s = jnp.concatenate([tab[:, 64:128]] * rep, axis=1)
    lane = lax.rem(lax.broadcasted_iota(jnp.int32, x.shape, 1), 64)
    partner = jnp.where(lane < 8, pltpu.roll(x, w - 8, 1), jnp.where(lane < 16, pltpu.roll(x, 8, 1), 0.0))
    return x * c + sign * (partner * s)


HG_FWD = 1
HB_BWD = 4


def _stack(x, hg):
    w = x.shape[1] // hg
    return x if hg == 1 else jnp.concatenate([x[:, g * w:(g + 1) * w] for g in range(hg)], axis=0)


def _unstack(x, hg):
    return x if hg == 1 else jnp.concatenate([x[g * SB:(g + 1) * SB] for g in range(hg)], axis=1)


def _swa_masks(b, nsb, hg):
    r = lax.rem(lax.broadcasted_iota(jnp.int32, (hg * SB, SB), 0), SB)
    c = lax.broadcasted_iota(jnp.int32, (hg * SB, SB), 1)
    real = b < nsb
    return c <= r, (c > r) & (b > 0) & real, (c < NM) & real


def _swa_specs(nsb):
    def rows(h, w, cb, f):
        return pl.BlockSpec((h, w), lambda i: (f(i), cb))
    pair = lambda i: i
    prev = lambda i: jnp.maximum(2 * i - 1, 0)
    meta = lambda i: nsb
    return rows, pair, prev, meta


def _swa_scores(b, nsb, hg, sink_ref, q, kc, kp, km):
    mc, mp, mm = _swa_masks(b, nsb, hg)
    groups = []
    for kv in range(2):
        ks = slice(kv * 64, (kv + 1) * 64)
        kcb, kpb, kmb = _bf(kc[:, ks]), _bf(kp[:, ks]), _bf(km[:, ks])
        for h0 in range(4 * kv, 4 * kv + 4, hg):
            qg = _bf(_stack(q[:, h0 * 64:(h0 + hg) * 64], hg))
            s_c = jnp.where(mc, _dot_nt(qg, kcb), NEG)
            s_p = jnp.where(mp, _dot_nt(qg, kpb), NEG)
            s_m = jnp.where(mm, _dot_nt(qg, kmb), NEG)
            sink = jnp.concatenate([jnp.full((SB, 1), sink_ref[0, h0 + g], F32) for g in range(hg)], axis=0)
            groups.append((kv, h0, qg, kcb, kpb, kmb, s_c, s_p, s_m, sink))
    return groups


def _swa_fwd(proj, sinks, t, comm=None):
    nsb = t // SB
    r_tot = t + TM
    rows, pair, prev, meta = _swa_specs(nsb)

    def body(sink_ref, q_ref, kc_ref, kp_ref, km_ref, vc_ref, vp_ref, vm_ref, o_ref, lse_ref):
        i = pl.program_id(0)
        km, vm = km_ref[...], vm_ref[...]
        for j in range(2):
            b = 2 * i + j
            rs = slice(j * SB, (j + 1) * SB)
            kp = kp_ref[...] if j == 0 else kc_ref[0:SB, :]
            vp = vp_ref[...] if j == 0 else vc_ref[0:SB, :]
            vc = vc_ref[rs, :]
            o_l, lse_l = [], []
            for kv, h0, qg, kcb, kpb, kmb, s_c, s_p, s_m, sink in _swa_scores(
                    b, nsb, HG_FWD, sink_ref, q_ref[rs, :], kc_ref[rs, :], kp, km):
                ks = slice(kv * 64, (kv + 1) * 64)
                m = jnp.maximum(jnp.max(jnp.maximum(jnp.maximum(s_c, s_p), s_m), -1, keepdims=True), sink)
                p_c, p_p, p_m = jnp.exp(s_c - m), jnp.exp(s_p - m), jnp.exp(s_m - m)
                l = jnp.sum(p_c + p_p + p_m, -1, keepdims=True) + jnp.exp(sink - m)
                o = _dot(_bf(p_c), _bf(vc[:, ks])) + _dot(_bf(p_p), _bf(vp[:, ks])) + _dot(_bf(p_m), _bf(vm[:, ks]))
                o_l.append(_unstack(o * (1.0 / l), HG_FWD))
                lse_l.append(_unstack(m + jnp.log(l), HG_FWD))
            valid = b * SB + lax.broadcasted_iota(jnp.int32, (SB, 1), 0) < t + NM
            o_ref[rs, :] = _bf(jnp.where(valid, jnp.concatenate(o_l, axis=1), 0.0))
            lse_ref[:, rs] = jnp.concatenate(lse_l, axis=1).T

    ck, cv = C_SK // 128, C_SV // 128
    return _call(
        body, "swa_fwd", (r_tot // QB,),
        [SMEM_SPEC, rows(QB, 512, C_SQ // 512, pair),
         rows(QB, 128, ck, pair), rows(SB, 128, ck, prev), rows(SB, 128, ck, meta),
         rows(QB, 128, cv, pair), rows(SB, 128, cv, prev), rows(SB, 128, cv, meta)],
        [rows(QB, 512, 0, pair), pl.BlockSpec((8, QB), lambda i: (0, i))],
        [jax.ShapeDtypeStruct((r_tot, 512), BF16), jax.ShapeDtypeStruct((8, r_tot), F32)],
        [], _cp(32), (sinks, proj, proj, proj, proj, proj, proj, proj), comm)


def _swa_bwd(proj, sinks, lse_t, do, t, comm=None):
    nsb = t // SB
    r_tot = t + TM
    rows, pair, prev, meta = _swa_specs(nsb)
    hb = HB_BWD
    lanes = hb * SB

    def body(sink_ref, q_ref, kc_ref, kp_ref, km_ref, vc_ref, vp_ref, vm_ref, lse_ref, do_ref,
             dq_ref, dk_ref, dv_ref, dsink_ref):
        i = pl.program_id(0)

        @pl.when(i == 0)
        def _():
            dk_ref[...] = jnp.zeros_like(dk_ref)
            dv_ref[...] = jnp.zeros_like(dv_ref)
            dsink_ref[...] = jnp.zeros_like(dsink_ref)

        key = lax.broadcasted_iota(jnp.int32, (SB, lanes), 0)
        qry = lax.rem(lax.broadcasted_iota(jnp.int32, (SB, lanes), 1), SB)
        km, vm = km_ref[...], vm_ref[...]
        dsink_l = []
        for j in range(2):
            b = 2 * i + j
            rs = slice(j * SB, (j + 1) * SB)
            real = b < nsb
            masks = (key <= qry, (key > qry) & (b > 0) & real, (key < NM) & real)
            k3 = (kc_ref[rs, :], kp_ref[...] if j == 0 else kc_ref[0:SB, :], km)
            v3 = (vc_ref[rs, :], vp_ref[...] if j == 0 else vc_ref[0:SB, :], vm)
            zero = jnp.zeros((SB, 64), F32)
            dq_l, ds_blk = [], []
            dk_l, dv_l = [[zero, zero] for _ in range(3)], [[zero, zero] for _ in range(3)]
            for h0 in range(0, 8, hb):
                kv = h0 // 4
                ks, hs = slice(kv * 64, (kv + 1) * 64), slice(h0 * 64, (h0 + hb) * 64)
                qg = _bf(_stack(q_ref[rs, hs], hb))
                dog = _bf(_stack(do_ref[rs, hs], hb))
                lse_row = jnp.concatenate([lse_ref[h:h + 1, rs] for h in range(h0, h0 + hb)], axis=1)
                sink_row = jnp.concatenate([jnp.full((1, SB), sink_ref[0, h], F32) for h in range(h0, h0 + hb)], axis=1)
                kb = [_bf(k[:, ks]) for k in k3]
                vb = [_bf(v[:, ks]) for v in v3]
                s = [_dot_nt(k, qg) for k in kb]
                dp = [_dot_nt(v, dog) for v in vb]
                p = [jnp.exp(jnp.where(m, sx, NEG) - lse_row) for m, sx in zip(masks, s)]
                delta = jnp.sum(p[0] * dp[0] + p[1] * dp[1] + p[2] * dp[2], axis=0, keepdims=True)
                ds = [_bf(pp * (dd - delta)) for pp, dd in zip(p, dp)]
                dq_t = _dot_tn(kb[0], ds[0]) + _dot_tn(kb[1], ds[1]) + _dot_tn(kb[2], ds[2])
                dq_l.append(_unstack(dq_t.T, hb))
                for x in range(3):
                    dk_l[x][kv] = dk_l[x][kv] + _dot(ds[x], qg)
                    dv_l[x][kv] = dv_l[x][kv] + _dot(_bf(p[x]), dog)
                ds_row = -jnp.exp(sink_row - lse_row) * delta
                ds_blk += [jnp.sum(ds_row[:, g * SB:(g + 1) * SB], axis=1, keepdims=True) for g in range(hb)]
            dsink_l.append(jnp.concatenate(ds_blk, axis=1))
            dq_ref[rs, :] = jnp.concatenate(dq_l, axis=1)
            starts = (pl.multiple_of(b * SB, SB), pl.multiple_of(jnp.maximum(b - 1, 0) * SB, SB), t)
            for x in range(3):
                dk_ref[pl.ds(starts[x], SB), :] += jnp.concatenate(dk_l[x], axis=1)
                dv_ref[pl.ds(starts[x], SB), :] += jnp.concatenate(dv_l[x], axis=1)
        dsink_ref[...] += dsink_l[0] + dsink_l[1]

    ck, cv = C_SK // 128, C_SV // 128
    whole = lambda w: pl.BlockSpec((r_tot, w), lambda i: (0, 0))
    return _call(
        body, "swa_bwd", (r_tot // QB,),
        [SMEM_SPEC, rows(QB, 512, C_SQ // 512, pair),
         rows(QB, 128, ck, pair), rows(SB, 128, ck, prev), rows(SB, 128, ck, meta),
         rows(QB, 128, cv, pair), rows(SB, 128, cv, prev), rows(SB, 128, cv, meta),
         pl.BlockSpec((8, QB), lambda i: (0, i)), rows(QB, 512, 1, pair)],
        [rows(QB, 512, 0, pair), whole(128), whole(128), pl.BlockSpec((1, 8), lambda i: (0, 0))],
        [jax.ShapeDtypeStruct((r_tot, 512), F32), jax.ShapeDtypeStruct((r_tot, 128), F32),
         jax.ShapeDtypeStruct((r_tot, 128), F32), jax.ShapeDtypeStruct((1, 8), F32)],
        [], _cp(48), (sinks, proj, proj, proj, proj, proj, proj, proj, lse_t, do), comm)


HK = D // 2


def _mlp_fwd(x, metapad, tgt, ogla, oswa, wo, wff, w1, w2, wfin):
    t = x.shape[0]
    nblk = t // TM

    def body(x_ref, mp_ref, tgt_ref, og_ref, os_ref, wo_ref, wff_ref, w1a_ref, w1b_ref, w2a_ref, w2b_ref, wfin_ref,
             h1_ref, f_ref, a_ref, dh2_ref, loss_ref, gfin_ref):
        i = pl.program_id(0)

        @pl.when(i == 0)
        def _():
            loss_ref[...] = jnp.zeros_like(loss_ref)
            gfin_ref[...] = jnp.zeros_like(gfin_ref)

        h0 = jnp.where(i == nblk, mp_ref[...], x_ref[...])
        h1 = h0 + _dot(og_ref[...], wo_ref[0:512, :]) + _dot(os_ref[...], wo_ref[512:1024, :])
        h1_ref[...] = h1
        fh, _ = _rms(h1)
        f = _bf(fh * wff_ref[...])
        f_ref[...] = f
        acc = jnp.zeros((TM, D), F32)
        for n in range(4):
            a = _dot(f[:, 0:HK], w1a_ref[n]) + _dot(f[:, HK:D], w1b_ref[n])
            a_ref[:, n * D:(n + 1) * D] = _bf(a)
            zr = jnp.maximum(a, 0.0)
            z = _bf(zr * zr)
            acc = acc + _dot(z[:, 0:HK], w2a_ref[n]) + _dot(z[:, HK:D], w2b_ref[n])
        h2 = h1 + acc
        yh, rs2 = _rms(h2)
        wf = wfin_ref[...]
        real = i < nblk
        e = jnp.where(real, yh * wf - tgt_ref[...], 0.0)
        loss_ref[...] += jnp.sum(jnp.sum(e * e, axis=0, keepdims=True), axis=1, keepdims=True) * (0.5 / D)
        dy = e * (1.0 / D)
        gfin_ref[...] += jnp.sum(dy * yh, axis=0, keepdims=True)
        dh2_ref[...] = _rms_bwd(dy, yh, rs2, wf)

    xs = pl.BlockSpec((TM, D), lambda i: (jnp.minimum(i, nblk - 1), 0))
    rs = lambda w: pl.BlockSpec((TM, w), lambda i: (i, 0))
    r_tot = t + TM
    return pl.pallas_call(
        body, name="mlp_fwd", grid=(nblk + 1,),
        in_specs=[xs, VMEM_SPEC, xs, rs(512), rs(512)] + [VMEM_SPEC] * 7,
        out_specs=[rs(D), rs(D), rs(DFF), rs(D), pl.BlockSpec((1, 1), lambda i: (0, 0)), pl.BlockSpec((1, D), lambda i: (0, 0))],
        out_shape=[jax.ShapeDtypeStruct((r_tot, D), F32), jax.ShapeDtypeStruct((r_tot, D), BF16),
                   jax.ShapeDtypeStruct((r_tot, DFF), BF16), jax.ShapeDtypeStruct((r_tot, D), F32),
                   jax.ShapeDtypeStruct((1, 1), F32), jax.ShapeDtypeStruct((1, D), F32)],
        compiler_params=_cp(56),
    )(x, metapad, tgt, ogla, oswa, wo, wff, *w1, *w2, wfin)


def _mlp_bwd(h1, a, dh2, ogla, oswa, wo, wff, w1, w2):
    r_tot = h1.shape[0]
    nt = r_tot // TM

    def body(h1_ref, a_ref, dh2_ref, og_ref, os_ref, wo_ref, wff_ref, w1a_ref, w1b_ref, w2a_ref, w2b_ref,
             da_ref, dh2b_ref, dh1_ref, do_ref, dwo_ref, gff_ref, dwo_acc):
        i = pl.program_id(0)

        @pl.when(i == 0)
        def _():
            dwo_acc[...] = jnp.zeros_like(dwo_acc)
            gff_ref[...] = jnp.zeros_like(gff_ref)

        dh2 = dh2_ref[...]
        dh2b = _bf(dh2)
        dh2b_ref[...] = dh2b
        dfa = jnp.zeros((TM, HK), F32)
        dfb = jnp.zeros((TM, HK), F32)
        for n in range(4):
            dz = jnp.concatenate([_dot_nt(dh2b, w2a_ref[n]), _dot_nt(dh2b, w2b_ref[n])], axis=1)
            da = _bf(dz * (2.0 * jnp.maximum(a_ref[:, n * D:(n + 1) * D].astype(F32), 0.0)))
            da_ref[:, n * D:(n + 1) * D] = da
            dfa = dfa + _dot_nt(da, w1a_ref[n])
            dfb = dfb + _dot_nt(da, w1b_ref[n])
        df = jnp.concatenate([dfa, dfb], axis=1)
        fh, rs1 = _rms(h1_ref[...])
        gff_ref[...] += jnp.sum(df * fh, axis=0, keepdims=True)
        dh1 = dh2 + _rms_bwd(df, fh, rs1, wff_ref[...])
        dh1_ref[...] = dh1
        dh1b = _bf(dh1)
        do_ref[...] = _dot_nt(dh1b, wo_ref[...])
        dwo_acc[0:512, :] += _dot_tn(og_ref[...], dh1b)
        dwo_acc[512:1024, :] += _dot_tn(os_ref[...], dh1b)

        @pl.when(i == nt - 1)
        def _():
            for s in range(4):
                for hh in range(2):
                    dwo_ref[hh, s] = dwo_acc[(2 * s + hh) * 128:(2 * s + hh + 1) * 128, :]

    rs = lambda w: pl.BlockSpec((TM, w), lambda i: (i, 0))
    return pl.pallas_call(
        body, name="mlp_bwd", grid=(nt,),
        in_specs=[rs(D), rs(DFF), rs(D), rs(512), rs(512)] + [VMEM_SPEC] * 6,
        out_specs=[rs(DFF), rs(D), rs(D), rs(D), VMEM_SPEC, pl.BlockSpec((1, D), lambda i: (0, 0))],
        out_shape=[jax.ShapeDtypeStruct((r_tot, DFF), BF16), jax.ShapeDtypeStruct((r_tot, D), BF16),
                   jax.ShapeDtypeStruct((r_tot, D), F32), jax.ShapeDtypeStruct((r_tot, D), F32),
                   jax.ShapeDtypeStruct((2, 4, 128, D), F32), jax.ShapeDtypeStruct((1, D), F32)],
        scratch_shapes=[pltpu.VMEM((D, D), F32)],
        compiler_params=_cp(56),
    )(h1, a, dh2, ogla, oswa, wo, wff, *w1, *w2)


def _ffn_wgrad(f, a, da, dh2b):
    r_tot = f.shape[0]
    kt = 768 if r_tot % 768 == 0 else TM
    nk = r_tot // kt

    def body(f_ref, a_ref, da_ref, dh2_ref, dw1_ref, dw2_ref, acc1, acc2):
        k = pl.program_id(1)

        @pl.when(k == 0)
        def _():
            acc1[...] = jnp.zeros_like(acc1)
            acc2[...] = jnp.zeros_like(acc2)

        zr = jnp.maximum(a_ref[...], 0.0)
        acc1[...] += _dot_tn(f_ref[...], da_ref[...])
        acc2[...] += _dot_tn(zr * zr, dh2_ref[...])

        @pl.when(k == nk - 1)
        def _():
            for hh in range(2):
                dw1_ref[hh, 0] = acc1[hh * 512:(hh + 1) * 512, :]
                dw2_ref[hh, 0] = acc2[hh * 512:(hh + 1) * 512, :]

    out = pl.BlockSpec((2, 1, 512, D), lambda n, k: (0, n, 0, 0))
    return pl.pallas_call(
        body, name="ffn_wgrad", grid=(4, nk),
        in_specs=[pl.BlockSpec((kt, D), lambda n, k: (k, 0)), pl.BlockSpec((kt, D), lambda n, k: (k, n)),
                  pl.BlockSpec((kt, D), lambda n, k: (k, n)), pl.BlockSpec((kt, D), lambda n, k: (k, 0))],
        out_specs=[out, out],
        out_shape=[jax.ShapeDtypeStruct((2, 4, 512, D), F32)] * 2,
        scratch_shapes=[pltpu.VMEM((D, D), F32), pltpu.VMEM((D, D), F32)],
        compiler_params=_cp(48, ("arbitrary", "arbitrary")),
    )(f, a, da, dh2b)


def _proj_bwd(x, metapad, wm, wt3, tabs, dgla, dswa_q, dsk, dsv, dlr, dh1, comm=None):
    t = x.shape[0]
    nblk = t // TM

    def body(x_ref, mp_ref, wm_ref, w3_ref, tab_ref, dg_ref, dq_ref, dk_ref, dv_ref, dlr_ref, dh1_ref,
             gx_ref, gmeta_ref, dw_ref, gmix_ref, w_ref, acc):
        i = pl.program_id(0)

        @pl.when(i == 0)
        def _():
            _join_shards(w3_ref, w_ref)
            acc[...] = jnp.zeros_like(acc)
            gmix_ref[...] = jnp.zeros_like(gmix_ref)

        h = jnp.where(i == nblk, mp_ref[...], x_ref[...])
        uh, rs = _rms(h)
        wm_v = wm_ref[...]
        u = _bf(uh * wm_v)
        tab = tab_ref[...]
        dq = _bf(_rope(dq_ref[...] * 0.125, tab, -1.0))
        dk = _bf(_rope(dk_ref[...], tab, -1.0))
        parts = ((dg_ref[...], 0, R_LR), (dlr_ref[:, 0:16], R_LR, 16), (dq, R_LR + 16, 512),
                 (dk, R_LR + 528, 128), (_bf(dv_ref[...]), R_LR + 656, 128))
        du = jnp.zeros((TM, D), F32)
        for val, r0, w in parts:
            du = du + _dot(val, w_ref[r0:r0 + w, :])
            acc[r0:r0 + w, :] += _dot_tn(val, u)
        gmix_ref[...] += jnp.sum(du * uh, axis=0, keepdims=True)
        dh0 = dh1_ref[...] + _rms_bwd(du, uh, rs, wm_v)

        @pl.when(i < nblk)
        def _():
            gx_ref[...] = dh0

        @pl.when(i == nblk)
        def _():
            gmeta_ref[...] = dh0[:NM]
            for s in range(4):
                dw_ref[s] = acc[(DIN // 4) * s:(DIN // 4) * (s + 1), :]

    xs = pl.BlockSpec((TM, D), lambda i: (jnp.minimum(i, nblk - 1), 0))
    rs_ = lambda w: pl.BlockSpec((TM, w), lambda i: (i, 0))
    return _call(
        body, "proj_bwd", (nblk + 1,),
        [xs, VMEM_SPEC, VMEM_SPEC, VMEM_SPEC, rs_(128), rs_(1536), rs_(512), rs_(128), rs_(128), rs_(128), rs_(D)],
        [xs, pl.BlockSpec((NM, D), lambda i: (0, 0)), VMEM_SPEC, pl.BlockSpec((1, D), lambda i: (0, 0))],
        [jax.ShapeDtypeStruct((t, D), F32), jax.ShapeDtypeStruct((NM, D), F32),
         jax.ShapeDtypeStruct((4, DIN // 4, D), F32), jax.ShapeDtypeStruct((1, D), F32)],
        [pltpu.VMEM((DIN, D), BF16), pltpu.VMEM((DIN, D), F32)], _cp(56),
        (x, metapad, wm, wt3, tabs, dgla, dswa_q, dsk, dsv, dlr, dh1), comm)


def _place():
    return lax.axis_index("x"), lax.axis_index("y"), lax.axis_index("c")


def _other_chips(x, y):
    return [(1 - x, y), (x, 1 - y), (1 - x, 1 - y)]


def _dma_sems(*counts):
    return tuple(pltpu.SemaphoreType.DMA((k,)) for k in counts)


def _gather_shards(shards, split):
    n = len(shards)
    two = [a for a in range(n) if split[a]]

    def plan(ins, outs, sems):
        isend, irecv, dsend, drecv, loc = sems
        x, y, c = _place()
        chips = _other_chips(x, y)

        def part(ref, a, half):
            if not split[a]:
                return ref
            w = shards[a].shape[1] // 2
            return ref.at[:, pl.ds(pl.multiple_of(half * w, 128), w)]

        def over_ici(a, k, shard_of):
            tx, ty = chips[k]
            sx, sy = shard_of
            return pltpu.make_async_remote_copy(
                src_ref=part(ins[a], a, c), dst_ref=part(outs[a].at[2 * sx + sy], a, c), send_sem=isend.at[3 * a + k],
                recv_sem=irecv.at[3 * a + k], device_id=(tx, ty, c), device_id_type=MESH)

        def over_d2d(a, k, half):
            tx, ty = chips[k]
            ref = part(outs[a].at[2 * tx + ty], a, half)
            return pltpu.make_async_remote_copy(
                src_ref=ref, dst_ref=ref, send_sem=dsend.at[3 * a + k], recv_sem=drecv.at[3 * a + k],
                device_id=(x, y, 1 - c), device_id_type=MESH)

        def local(a):
            return pltpu.make_async_copy(ins[a], outs[a].at[2 * x + y], loc.at[a])

        pairs = [(a, k) for a in range(n) for k in range(3)]
        first = ([lambda a=a: local(a).start() for a in range(n)]
                 + [lambda a=a, k=k: over_ici(a, k, (x, y)).start() for a, k in pairs],
                 [lambda a=a, k=k: over_ici(a, k, chips[k]).wait_recv() for a, k in pairs]
                 + [lambda a=a, k=k: over_ici(a, k, (x, y)).wait_send() for a, k in pairs]
                 + [lambda a=a: local(a).wait() for a in range(n)])
        pairs2 = [(a, k) for a in two for k in range(3)]
        second = ([lambda a=a, k=k: over_d2d(a, k, c).start() for a, k in pairs2],
                  [lambda a=a, k=k: over_d2d(a, k, 1 - c).wait_recv() for a, k in pairs2]
                  + [lambda a=a, k=k: over_d2d(a, k, c).wait_send() for a, k in pairs2])
        return [first, second] if two else [first]

    return _Comm(tuple(shards), tuple(jax.ShapeDtypeStruct((4,) + s.shape, s.dtype) for s in shards),
                 _dma_sems(3 * n, 3 * n, 3 * n, 3 * n, n), 2 if two else 1, plan)


def _swap_halves(grads):
    n = len(grads)

    def plan(ins, outs, sems):
        send, recv = sems
        x, y, c = _place()

        def swap(a):
            return pltpu.make_async_remote_copy(
                src_ref=ins[a].at[1 - c], dst_ref=outs[a], send_sem=send.at[a], recv_sem=recv.at[a],
                device_id=(x, y, 1 - c), device_id_type=MESH)

        return [([lambda a=a: swap(a).start() for a in range(n)], [lambda a=a: swap(a).wait() for a in range(n)])]

    return _Comm(tuple(grads), tuple(jax.ShapeDtypeStruct(g.shape[1:], g.dtype) for g in grads), _dma_sems(n, n), 1, plan)


def _scatter_shards(parts):
    n = len(parts)

    def plan(ins, outs, sems):
        send, recv = sems
        x, y, c = _place()
        chips = _other_chips(x, y)

        def scatter(a, k):
            tx, ty = chips[k]
            return pltpu.make_async_remote_copy(
                src_ref=ins[a].at[2 * tx + ty], dst_ref=outs[a].at[k], send_sem=send.at[3 * a + k],
                recv_sem=recv.at[3 * a + k], device_id=(tx, ty, c), device_id_type=MESH)

        pairs = [(a, k) for a in range(n) for k in range(3)]
        return [([lambda a=a, k=k: scatter(a, k).start() for a, k in pairs],
                 [lambda a=a, k=k: scatter(a, k).wait() for a, k in pairs])]

    return _Comm(tuple(parts), tuple(jax.ShapeDtypeStruct((3,) + p.shape[1:], p.dtype) for p in parts),
                 _dma_sems(3 * n, 3 * n), 1, plan)


def _join_halves(halves):
    n = len(halves)

    def plan(ins, outs, sems):
        send, recv, loc = sems
        x, y, c = _place()

        def remote(a, half):
            return pltpu.make_async_remote_copy(
                src_ref=ins[a], dst_ref=outs[a].at[half], send_sem=send.at[a], recv_sem=recv.at[a],
                device_id=(x, y, 1 - c), device_id_type=MESH)

        def local(a):
            return pltpu.make_async_copy(ins[a], outs[a].at[c], loc.at[a])

        every = range(n)
        return [([lambda a=a: local(a).start() for a in every] + [lambda a=a: remote(a, c).start() for a in every],
                 [lambda a=a: remote(a, 1 - c).wait_recv() for a in every]
                 + [lambda a=a: remote(a, c).wait_send() for a in every] + [lambda a=a: local(a).wait() for a in every])]

    return _Comm(tuple(halves), tuple(jax.ShapeDtypeStruct((2,) + h.shape, h.dtype) for h in halves),
                 _dma_sems(n, n, n), 1, plan)


def _reduce_w_in(dwt, comm):
    rows, hw = DIN // 4, D // 2
    ci, co = len(comm.ins), len(comm.outs)

    def body(*refs):
        dw_ref, c_in, out_ref, c_out = refs[0], refs[1:1 + ci], refs[1 + ci], refs[2 + ci:2 + ci + co]
        mine, sib, tosend, rbuf, qbuf, full, send, recv, loc = refs[2 + ci + co:11 + ci + co]
        c_sem = refs[11 + ci + co:]
        x, y, c = _place()
        sibling = (x, y, 1 - c)
        (starts, waits), = comm.plan(c_in, c_out, c_sem)
        _run_phase(starts)

        def cols(ref, half):
            window = pl.ds(pl.multiple_of(half * hw, 128), hw)
            return ref.at[:, :, window] if len(ref.shape) == 3 else ref.at[:, window]

        load = pltpu.make_async_copy(cols(dw_ref, c), mine, loc.at[0])
        give = pltpu.make_async_remote_copy(src_ref=cols(dw_ref, 1 - c), dst_ref=sib, send_sem=send.at[3], recv_sem=recv.at[3],
                                            device_id=sibling, device_id_type=MESH)
        load.start()
        give.start()
        load.wait()
        give.wait()
        mine[...] = mine[...] + sib[...]
        cps = []
        for k, (tx, ty) in enumerate(_other_chips(x, y)):
            tosend[k] = _bf(mine[2 * tx + ty])
            cps.append(pltpu.make_async_remote_copy(
                src_ref=tosend.at[k], dst_ref=rbuf.at[k], send_sem=send.at[k], recv_sem=recv.at[k],
                device_id=(tx, ty, c), device_id_type=MESH))
            cps[-1].start()
        for cp in cps:
            cp.wait()
        qbuf[...] = mine[2 * x + y] + rbuf[0].astype(F32) + rbuf[1].astype(F32) + rbuf[2].astype(F32)
        keep = pltpu.make_async_copy(qbuf, cols(full, c), loc.at[1])
        pass_on = pltpu.make_async_remote_copy(src_ref=qbuf, dst_ref=cols(full, c), send_sem=send.at[4], recv_sem=recv.at[4],
                                               device_id=sibling, device_id_type=MESH)
        keep.start()
        pass_on.start()
        keep.wait()
        pass_on.wait_send()
        pltpu.make_async_remote_copy(src_ref=qbuf, dst_ref=cols(full, 1 - c), send_sem=send.at[4], recv_sem=recv.at[4],
                                     device_id=sibling, device_id_type=MESH).wait_recv()
        out_ref[...] = full[...]
        _run_phase(waits)

    outs = pl.pallas_call(
        body, name="reduce_w_in",
        in_specs=[ANY_SPEC] * (1 + ci), out_specs=[VMEM_SPEC] + [ANY_SPEC] * co,
        out_shape=[jax.ShapeDtypeStruct((rows, D), F32)] + list(comm.outs),
        scratch_shapes=[pltpu.VMEM((4, rows, hw), F32), pltpu.VMEM((4, rows, hw), F32), pltpu.VMEM((3, rows, hw), BF16),
                        pltpu.VMEM((3, rows, hw), BF16), pltpu.VMEM((rows, hw), F32), pltpu.VMEM((rows, D), F32),
                        *_dma_sems(5, 5, 2), *comm.sems],
        compiler_params=pltpu.CompilerParams(vmem_limit_bytes=48 << 20),
    )(dwt, *comm.ins)
    return outs[0], outs[1:]


def _allreduce_small(pack):
    p = pack.shape[0]

    def body(in_ref, out_ref, buf, send, recv):
        x, y, c = _place()
        me = 4 * x + 2 * y + c
        buf[me] = in_ref[...]

        def peer_of(k):
            return x ^ (k >> 2), y ^ ((k >> 1) & 1), c ^ (k & 1)

        sends = [pltpu.make_async_remote_copy(
            src_ref=in_ref, dst_ref=buf.at[me], send_sem=send.at[k - 1], recv_sem=recv.at[k - 1],
            device_id=peer_of(k), device_id_type=MESH) for k in range(1, 8)]
        for cp in sends:
            cp.start()
        for k in range(1, 8):
            px, py, pc = peer_of(k)
            pltpu.make_async_remote_copy(
                src_ref=in_ref, dst_ref=buf.at[4 * px + 2 * py + pc], send_sem=send.at[k - 1], recv_sem=recv.at[k - 1],
                device_id=(x, y, c), device_id_type=MESH).wait_recv()
        for cp in sends:
            cp.wait_send()
        acc = buf[0]
        for d in range(1, 8):
            acc = acc + buf[d]
        out_ref[...] = acc

    return pl.pallas_call(
        body, name="allreduce_small",
        in_specs=[VMEM_SPEC], out_specs=VMEM_SPEC, out_shape=jax.ShapeDtypeStruct(pack.shape, F32),
        scratch_shapes=[pltpu.VMEM((8, p, D), F32), *_dma_sems(7, 7)],
    )(pack)


GRID4 = 4


def _sum_cores(core_shard, mine, theirs):
    n = len(mine)

    def body(cs_ref, *refs):
        ms, ts, bfs, owns = refs[:n], refs[n:2 * n], refs[2 * n:3 * n], refs[3 * n:]
        keep = pl.program_id(0) == cs_ref[1]
        for a in range(n):
            acc = ms[a][0, 0] + ts[a][0]
            bfs[a][0] = _bf(acc)

            @pl.when(keep)
            def _():
                owns[a][...] = acc

    shapes = [m.shape[2:] for m in mine]
    in_specs = ([pl.BlockSpec((1, 1) + s, lambda i, cs: (cs[0], i, 0, 0)) for s in shapes]
                + [pl.BlockSpec((1,) + s, lambda i, cs: (i, 0, 0)) for s in shapes])
    out_specs = ([pl.BlockSpec((1,) + s, lambda i, cs: (i, 0, 0)) for s in shapes]
                 + [pl.BlockSpec(s, lambda i, cs: (0, 0)) for s in shapes])
    outs = pl.pallas_call(
        body, name="sum_cores",
        grid_spec=pltpu.PrefetchScalarGridSpec(num_scalar_prefetch=1, grid=(4,), in_specs=in_specs, out_specs=out_specs),
        out_shape=[jax.ShapeDtypeStruct((4,) + s, BF16) for s in shapes] + [jax.ShapeDtypeStruct(s, F32) for s in shapes],
        compiler_params=_cp(48),
    )(core_shard, *mine, *theirs)
    return outs[:n], outs[n:]


def _sum_chips(own, arrived):
    n = len(own)

    def body(*refs):
        os_, ars, outs = refs[:n], refs[n:2 * n], refs[2 * n:]
        for a in range(n):
            outs[a][...] = os_[a][...] + ars[a][0].astype(F32) + ars[a][1].astype(F32) + ars[a][2].astype(F32)

    blocks = [(o.shape[0] // GRID4, o.shape[1]) for o in own]
    return pl.pallas_call(
        body, name="sum_chips", grid=(GRID4,),
        in_specs=([pl.BlockSpec(b, lambda i: (i, 0)) for b in blocks]
                  + [pl.BlockSpec((3,) + b, lambda i: (0, i, 0)) for b in blocks]),
        out_specs=[pl.BlockSpec(b, lambda i: (i, 0)) for b in blocks],
        out_shape=[jax.ShapeDtypeStruct(o.shape, F32) for o in own],
        compiler_params=_cp(32),
    )(*own, *arrived)


def _adamw_math(w, g, m, v):
    m2 = ADAM_B1 * m + (1.0 - ADAM_B1) * g
    v2 = ADAM_B2 * v + (1.0 - ADAM_B2) * (g * g)
    m_hat = m2 / (1.0 - ADAM_B1 ** ADAM_STEP)
    v_hat = v2 / (1.0 - ADAM_B2 ** ADAM_STEP)
    return -ADAM_LR * (m_hat / (jnp.sqrt(v_hat) + ADAM_EPS) + ADAM_WD * w), m2, v2


def _adamw_big(ws, gs, ms, vs):
    n = len(ws)

    def body(*refs):
        for a in range(n):
            d, m2, v2 = _adamw_math(refs[a][...], refs[n + a][...], refs[2 * n + a][...], refs[3 * n + a][...])
            refs[4 * n + a][...] = d
            refs[5 * n + a][...] = m2
            refs[6 * n + a][...] = v2

    specs = [pl.BlockSpec((w.shape[0] // GRID4, w.shape[1]), lambda i: (i, 0)) for w in ws]
    return pl.pallas_call(
        body, name="adamw_big", grid=(GRID4,),
        in_specs=specs * 4, out_specs=specs * 3,
        out_shape=[jax.ShapeDtypeStruct(w.shape, F32) for w in ws] * 3,
        compiler_params=_cp(48),
    )(*ws, *gs, *ms, *vs)


def _adamw_small(ws, gs, ms, vs):
    n = len(ws)

    def body(*refs):
        for a in range(n):
            d, m2, v2 = _adamw_math(refs[a][...], refs[n + a][...], refs[2 * n + a][...], refs[3 * n + a][...])
            refs[4 * n + a][...] = d
            refs[5 * n + a][...] = m2
            refs[6 * n + a][...] = v2

    return pl.pallas_call(
        body, name="adamw_small",
        in_specs=[VMEM_SPEC] * (4 * n), out_specs=[VMEM_SPEC] * (3 * n),
        out_shape=[jax.ShapeDtypeStruct(w.shape, F32) for w in ws] * 3,
        compiler_params=pltpu.CompilerParams(vmem_limit_bytes=40 << 20),
    )(*ws, *gs, *ms, *vs)


def kernel(x, meta_tokens, norm_mix_w, w_in, w_gate_up, b_gate, gla_norm_w, sinks, w_out, norm_ff_w, w_ff1, w_ff2, final_norm_w, loss_target, m_meta_tokens, m_norm_mix_w, m_w_in, m_w_gate_up, m_b_gate, m_gla_norm_w, m_sinks, m_w_out, m_norm_ff_w, m_w_ff1, m_w_ff2, m_final_norm_w, v_meta_tokens, v_norm_mix_w, v_w_in, v_w_gate_up, v_b_gate, v_gla_norm_w, v_sinks, v_w_out, v_norm_ff_w, v_w_ff1, v_w_ff2, v_final_norm_w):
    xi, yi, ci = _place()
    shard = (2 * xi + yi).astype(jnp.int32).reshape(1)
    core = ci.astype(jnp.int32).reshape(1)

    small = jnp.concatenate([meta_tokens, w_gate_up[0], jnp.zeros((NM, 64), F32)], axis=1)
    wt3, g_small = _run_comm(_gather_shards([_bf(w_in[0].T), small], [True, False]), "gather_w_in")
    meta = g_small[:, :, 0:256].transpose(1, 0, 2).reshape(NM, D)
    wgu = g_small[:, :, 256:320].transpose(1, 0, 2).reshape(NM, 256)

    xs, tgt = x[0], loss_target[0]
    t = xs.shape[0]
    wfin = final_norm_w.reshape(1, D)
    metapad = jnp.concatenate([meta, jnp.zeros((TM - NM, D), F32)], axis=0)
    wgu_p = _bf(jnp.concatenate([wgu, jnp.zeros((128 - 16, 256), F32)], axis=0))
    tabs = _rope_tables(t)

    w1s, w2s = _bf(w_ff1[0]), _bf(w_ff2[0])
    proj, (g_out, w1a, w1b) = _proj_fwd(xs, metapad, norm_mix_w, wt3, tabs,
                                        _gather_shards([_bf(w_out[0]), w1s[:HK], w1s[HK:]], [True] * 3))
    (oswa, lse), (w2a, w2b) = _swa_fwd(proj, sinks, t, _gather_shards([w2s[:HK], w2s[HK:]], [True] * 2))
    (ogla, oraw, sst, bcum, dgate), _ = _gla_fwd(proj, wgu_p, b_gate, gla_norm_w, t)
    wo, w1, w2 = g_out.reshape(D, D), (w1a, w1b), (w2a, w2b)
    h1, f, a, dh2, loss, gfin = _mlp_fwd(xs, metapad, tgt, ogla, oswa, wo, norm_ff_w, w1, w2, wfin)

    da, dh2b, dh1, do, dwo, gff = _mlp_bwd(h1, a, dh2, ogla, oswa, wo, norm_ff_w, w1, w2)
    dw1, dw2 = _ffn_wgrad(f, a, da, dh2b)
    big = [dwo, dw1, dw2]
    (dsq, dsk, dsv, dsink), theirs = _swa_bwd(proj, sinks, lse, do, t, _swap_halves(big))
    sums_bf, own = _sum_cores(jnp.concatenate([core, shard]), big, theirs)
    (dgla, dlr, dwgu, dbg, dgnw), arrived = _gla_bwd(proj, oraw, sst, bcum, dgate, do, wgu_p, gla_norm_w, t,
                                                     _scatter_shards(sums_bf))
    halves = _sum_chips(own, arrived)
    (gx, gmeta, dwt, gmix), _ = _proj_bwd(xs, metapad, norm_mix_w, wt3, tabs, dgla, dsq, dsk, dsv, dlr, dh1)

    gwt_in, joined = _reduce_w_in(dwt, _join_halves(halves))
    gw_out, gw_1, gw_2 = [j.reshape((-1, j.shape[2])) for j in joined]

    tail = jnp.concatenate([dbg, dgnw, dsink, loss, jnp.zeros((1, D - 256 - 128 - 8 - 1), F32)], axis=1)
    pack = jnp.concatenate([gmeta, gmix, gff, gfin, tail, dwgu[:16].reshape(4, D)], axis=0)
    tot = _allreduce_small(pack)
    g_meta = lax.dynamic_slice_in_dim(tot[0:NM], shard[0] * 256, 256, axis=1)
    g_mix, g_ff, g_fin = tot[16:17], tot[17:18], tot[18]
    g_bg, g_gnw, g_sinks, loss_tot = tot[19:20, 0:256], tot[19:20, 256:384], tot[19:20, 384:392], tot[19, 392]
    g_wgu = lax.dynamic_slice_in_dim(tot[20:24].reshape(NM, 256), shard[0] * 64, 64, axis=1)

    bo = _adamw_big([w_out[0], w_ff1[0], w_ff2[0]], [gw_out, gw_1, gw_2], [m_w_out[0], m_w_ff1[0], m_w_ff2[0]],
                    [v_w_out[0], v_w_ff1[0], v_w_ff2[0]])

    fin2 = lambda a: a.reshape(1, D)
    sw = [meta_tokens, norm_mix_w, w_gate_up[0], b_gate, gla_norm_w, sinks, norm_ff_w, fin2(final_norm_w), w_in[0].T]
    sg = [g_meta, g_mix, g_wgu, g_bg, g_gnw, g_sinks, g_ff, fin2(g_fin), gwt_in]
    sm = [m_meta_tokens, m_norm_mix_w, m_w_gate_up[0], m_b_gate, m_gla_norm_w, m_sinks, m_norm_ff_w, fin2(m_final_norm_w),
          m_w_in[0].T]
    sv = [v_meta_tokens, v_norm_mix_w, v_w_gate_up[0], v_b_gate, v_gla_norm_w, v_sinks, v_norm_ff_w, fin2(v_final_norm_w),
          v_w_in[0].T]
    so = _adamw_small(sw, sg, sm, sv)

    def ordered(small_o, big_o):
        meta_, mix_, wgu_, bg_, gnw_, sinks_, ff_, fin_, wt_ = small_o
        w_out_, w_1_, w_2_ = big_o
        return (meta_, mix_, wt_.T[None], wgu_[None], bg_, gnw_, sinks_, w_out_[None], ff_, w_1_[None], w_2_[None],
                fin_.reshape(D))

    grads = ordered(sg, [gw_out, gw_1, gw_2])
    deltas = ordered(so[0:9], bo[0:3])
    new_m = ordered(so[9:18], bo[3:6])
    new_v = ordered(so[18:27], bo[6:9])
    return (loss_tot, gx[None], *grads, *deltas, *new_m, *new_v)
```

```python
import functools
from typing import Callable, NamedTuple

import jax
import jax.numpy as jnp
import numpy as np
from jax import lax
from jax.experimental import pallas as pl
from jax.experimental.pallas import tpu as pltpu

F32 = jnp.float32
BF16 = jnp.bfloat16

D = 1024
DFF = 4096
NM = 16
TM = 256
CH = 64
SB = 128
QB = 2 * SB
EPS = 1e-5
C_GQ, C_GK, C_GV, C_GR, C_SQ, C_SK, C_SV, C_LR, DINP = 0, 256, 512, 1024, 1536, 2048, 2176, 2304, 2432
DIN = 2320
R_LR = 1536
ROPE_THETA = 500000.0
ADAM_LR, ADAM_B1, ADAM_B2, ADAM_EPS, ADAM_WD, ADAM_STEP = 0.001, 0.9, 0.999, 1e-08, 0.01, 10
NEG = -1e30
MESH = pl.DeviceIdType.MESH
VMEM_SPEC = pl.BlockSpec(memory_space=pltpu.VMEM)
ANY_SPEC = pl.BlockSpec(memory_space=pl.ANY)
SMEM_SPEC = pl.BlockSpec(memory_space=pltpu.SMEM)


def _cp(vmem_mb, sem=("arbitrary",)):
    return pltpu.CompilerParams(dimension_semantics=sem, vmem_limit_bytes=vmem_mb << 20)


def _dot(a, b):
    return jnp.dot(a, b, preferred_element_type=F32)


def _dot_nt(a, b):
    return lax.dot_general(a, b, (((1,), (1,)), ((), ())), preferred_element_type=F32)


def _dot_tn(a, b):
    return lax.dot_general(a, b, (((0,), (0,)), ((), ())), preferred_element_type=F32)


def _bf(x):
    return x.astype(BF16)


def _dot3(m01, x):
    x1 = _bf(x)
    r1 = x - x1.astype(F32)
    x2 = _bf(r1)
    x3 = _bf(r1 - x2.astype(F32))
    return _dot(m01, x1) + _dot(m01, x2) + _dot(m01, x3)


def _rms(h):
    rs = lax.rsqrt(jnp.mean(h * h, axis=-1, keepdims=True) + EPS)
    return h * rs, rs


def _rms_bwd(dy, yhat, rs, w):
    dyh = dy * w
    return rs * (dyh - yhat * jnp.mean(dyh * yhat, axis=-1, keepdims=True))


class _Comm(NamedTuple):
    ins: tuple
    outs: tuple
    sems: tuple
    phases: int
    plan: Callable


def _run_phase(fns):
    for fn in fns:
        fn()


def _call(body, name, grid, in_specs, out_specs, out_shape, scratch, params, args, comm=None):
    if comm is None:
        outs = pl.pallas_call(body, name=name, grid=grid, in_specs=in_specs, out_specs=out_specs, out_shape=out_shape,
                              scratch_shapes=scratch, compiler_params=params)(*args)
        return outs, None
    n_in, n_out, n_scr = len(in_specs), len(out_specs), len(scratch)
    ci, co = len(comm.ins), len(comm.outs)
    last = grid[0] - 1
    marks = [0, max(1, last - max(2, (last + 1) // 6))][:comm.phases]

    def wrapped(*refs):
        own_in, c_in = refs[:n_in], refs[n_in:n_in + ci]
        refs = refs[n_in + ci:]
        own_out, c_out = refs[:n_out], refs[n_out:n_out + co]
        refs = refs[n_out + co:]
        own_scr, c_sem = refs[:n_scr], refs[n_scr:]
        i = pl.program_id(0)

        for p, mark in enumerate(marks):
            @pl.when(i == mark)
            def _():
                plan = comm.plan(c_in, c_out, c_sem)
                if p > 0:
                    _run_phase(plan[p - 1][1])
                _run_phase(plan[p][0])

        body(*own_in, *own_out, *own_scr)

        @pl.when(i == last)
        def _():
            _run_phase(comm.plan(c_in, c_out, c_sem)[-1][1])

    outs = pl.pallas_call(
        wrapped, name=name, grid=grid, in_specs=list(in_specs) + [ANY_SPEC] * ci, out_specs=list(out_specs) + [ANY_SPEC] * co,
        out_shape=list(out_shape) + list(comm.outs), scratch_shapes=list(scratch) + list(comm.sems), compiler_params=params,
    )(*args, *comm.ins)
    return outs[:n_out], outs[n_out:]


def _run_comm(comm, name):
    ci, co = len(comm.ins), len(comm.outs)

    def body(*refs):
        for starts, waits in comm.plan(refs[:ci], refs[ci:ci + co], refs[ci + co:]):
            _run_phase(starts)
            _run_phase(waits)

    return pl.pallas_call(body, name=name, in_specs=[ANY_SPEC] * ci, out_specs=[ANY_SPEC] * co, out_shape=list(comm.outs),
                          scratch_shapes=list(comm.sems))(*comm.ins)


def _join_shards(w3_ref, w_ref):
    for s in range(4):
        w_ref[(DIN // 4) * s:(DIN // 4) * (s + 1), :] = w3_ref[s]


def _proj_fwd(x, metapad, wm, wt3, tabs, comm=None):
    t = x.shape[0]
    nblk = t // TM

    def body(x_ref, mp_ref, wm_ref, w3_ref, tab_ref, proj_ref, w_ref):
        i = pl.program_id(0)

        @pl.when(i == 0)
        def _():
            _join_shards(w3_ref, w_ref)

        h = jnp.where(i == nblk, mp_ref[...], x_ref[...])
        u, _ = _rms(h)
        ub = _bf(u * wm_ref[...])
        proj_ref[:, 0:C_SQ] = _dot_nt(ub, w_ref[0:R_LR, :])
        att = _dot_nt(ub, w_ref[R_LR + 16:DIN, :])
        tab = tab_ref[...]
        proj_ref[:, C_SQ:C_SK] = _rope(att[:, 0:512], tab, 1.0) * 0.125
        proj_ref[:, C_SK:C_SV] = _rope(att[:, 512:640], tab, 1.0)
        proj_ref[:, C_SV:C_LR] = att[:, 640:768]
        proj_ref[:, C_LR:DINP] = jnp.zeros((TM, DINP - C_LR), F32)
        proj_ref[:, C_LR:C_LR + 16] = _dot_nt(ub, w_ref[R_LR:R_LR + 16, :])

    (proj,), got = _call(
        body, "proj_fwd", (nblk + 1,),
        [pl.BlockSpec((TM, D), lambda i: (jnp.minimum(i, nblk - 1), 0)), VMEM_SPEC, VMEM_SPEC, VMEM_SPEC,
         pl.BlockSpec((TM, 128), lambda i: (i, 0))],
        [pl.BlockSpec((TM, DINP), lambda i: (i, 0))], [jax.ShapeDtypeStruct((t + TM, DINP), F32)],
        [pltpu.VMEM((DIN, D), BF16)], _cp(48), (x, metapad, wm, wt3, tabs), comm)
    return proj, got


def _chunk_masks():
    r = lax.broadcasted_iota(jnp.int32, (TM, TM), 0)
    c = lax.broadcasted_iota(jnp.int32, (TM, TM), 1)
    same = (r // CH) == (c // CH)
    lower = _bf(jnp.where(same & (c <= r), 1.0, 0.0))
    upper = _bf(jnp.where(same & (c >= r), 1.0, 0.0))
    return lower, upper


def _gla_gate(lr, wgu, bg, valid, lower):
    z = _dot(_bf(lr), wgu) + bg
    g = (jnp.minimum(z, 0.0) - jnp.log(1.0 + jnp.exp(-jnp.abs(z)))) * (1.0 / 16.0)
    g = jnp.where(valid, g, 0.0)
    return z, _dot3(lower, g)


def _gla_decays(q, k, b):
    nc = TM // CH
    b3 = b.reshape(nc, CH, 256)
    blast = b3[:, CH - 1:CH, :]
    eb = jnp.exp(b)
    enb = jnp.exp(-b)
    ebl = jnp.exp(blast - b3).reshape(TM, 256)
    return eb, enb, ebl, jnp.exp(blast)


def _tri(lower_incl):
    r = lax.broadcasted_iota(jnp.int32, (CH, CH), 0)
    c = lax.broadcasted_iota(jnp.int32, (CH, CH), 1)
    return ((c <= r) if lower_incl else (c >= r))[None]


def _gla_fwd(proj, wgu, bg, gnw, t, comm=None):
    nblk = t // TM
    nt = nblk + 1
    nc = TM // CH

    def blk(i):
        return (i + nblk) % nt

    def body(q_ref, k_ref, v_ref, r_ref, lr_ref, wgu_ref, bg_ref, gnw_ref, o_ref, oraw_ref, sst_ref, b_ref, dgate_ref,
             st_scr):
        i = pl.program_id(0)

        @pl.when(i == 0)
        def _():
            st_scr[...] = jnp.zeros_like(st_scr)

        rows = blk(i) * TM + lax.broadcasted_iota(jnp.int32, (TM, 1), 0)
        lower, _ = _chunk_masks()
        valid = rows < t + NM
        z, b = _gla_gate(lr_ref[...], wgu_ref[...], bg_ref[...], valid, lower)
        b_ref[...] = b
        dgate_ref[...] = jnp.where(valid, (1.0 / 16.0) / (1.0 + jnp.exp(z)), 0.0)
        q = q_ref[...]
        k = k_ref[...]
        eb, enb, ebl, eblast = _gla_decays(q, k, b)
        qt = q * 0.125 * eb
        kt = k * enb
        kh = k * ebl
        tril = _tri(True)
        heads = range(4)
        hs = [slice(h * CH, (h + 1) * CH) for h in heads]
        qh = [_bf(qt[:, hs[h]]).reshape(nc, CH, CH) for h in heads]
        kth = [_bf(kt[:, hs[h]]).reshape(nc, CH, CH) for h in heads]
        khh = [_bf(kh[:, hs[h]]).reshape(nc, CH, CH) for h in heads]
        vh = [_bf(v_ref[:, h * 128:(h + 1) * 128]).reshape(nc, CH, 128) for h in heads]
        a = [jnp.einsum('cid,cjd->cij', qh[h], kth[h], preferred_element_type=F32) for h in heads]
        kv = [jnp.einsum('cjv,cjd->cvd', vh[h], khh[h], preferred_element_type=F32) for h in heads]
        o = [jnp.einsum('cij,cjv->civ', _bf(jnp.where(tril, a[h], 0.0)), vh[h], preferred_element_type=F32) for h in heads]
        states = []
        for h in heads:
            st = st_scr[h]
            per_chunk = []
            for c in range(nc):
                sst_ref[c, h] = st
                per_chunk.append(_bf(st))
                st = st * eblast[c, :, hs[h]] + kv[h][c]
            st_scr[h] = st
            states.append(per_chunk)
        o_inter = [[_dot_nt(qh[h][c], states[h][c]) for c in range(nc)] for h in heads]
        oraw = jnp.concatenate([(o[h] + jnp.stack(o_inter[h])).reshape(TM, 128) for h in heads], axis=1)
        oraw_ref[...] = oraw
        gn = gnw_ref[...]
        res = []
        for h in range(4):
            on, _ = _rms(oraw[:, h * 128:(h + 1) * 128])
            r = r_ref[:, h * 128:(h + 1) * 128]
            res.append(on * gn * (r * jax.nn.sigmoid(r)))
        o_ref[...] = _bf(jnp.concatenate(res, axis=1))

    def spec(w, cb):
        return pl.BlockSpec((TM, w), lambda i: (blk(i), cb))

    return _call(
        body, "gla_fwd", (nt,),
        [spec(256, 0), spec(256, 1), spec(512, 1), spec(512, 2), spec(128, C_LR // 128), VMEM_SPEC, VMEM_SPEC, VMEM_SPEC],
        [spec(512, 0), spec(512, 0), pl.BlockSpec((nc, 4, 128, CH), lambda i: (blk(i), 0, 0, 0)), spec(256, 0), spec(256, 0)],
        [jax.ShapeDtypeStruct((t + TM, 512), BF16), jax.ShapeDtypeStruct((t + TM, 512), F32),
         jax.ShapeDtypeStruct((nt * nc, 4, 128, CH), F32), jax.ShapeDtypeStruct((t + TM, 256), F32),
         jax.ShapeDtypeStruct((t + TM, 256), F32)],
        [pltpu.VMEM((4, 128, CH), F32)], _cp(40), (proj, proj, proj, proj, proj, wgu, bg, gnw), comm)


def _gla_bwd(proj, oraw, sst, bcum, dgate, do, wgu, gnw, t, comm=None):
    nblk = t // TM
    nt = nblk + 1
    nc = TM // CH

    def blk(i):
        return (2 * nblk - i) % nt

    def body(q_ref, k_ref, v_ref, r_ref, lr_ref, oraw_ref, sst_ref, b_ref, dgate_ref, do_ref, wgu_ref, gnw_ref,
             dgla_ref, dlr_ref, dwgu_ref, dbg_ref, dgnw_ref, dst_scr):
        i = pl.program_id(0)

        @pl.when(i == 0)
        def _():
            dst_scr[...] = jnp.zeros_like(dst_scr)
            dwgu_ref[...] = jnp.zeros_like(dwgu_ref)
            dbg_ref[...] = jnp.zeros_like(dbg_ref)
            dgnw_ref[...] = jnp.zeros_like(dgnw_ref)

        _, upper = _chunk_masks()
        lr = lr_ref[...]
        b = b_ref[...]
        q = q_ref[...]
        k = k_ref[...]
        eb, enb, ebl, eblast = _gla_decays(q, k, b)
        qt = q * 0.125 * eb
        kt = k * enb
        kh = k * ebl
        gn = gnw_ref[...]
        tril = _tri(True)
        triu = _tri(False)
        heads = range(4)
        hs = [slice(h * CH, (h + 1) * CH) for h in heads]
        vs = [slice(h * 128, (h + 1) * 128) for h in heads]
        ein = functools.partial(jnp.einsum, preferred_element_type=F32)
        dr_l, doh = [], []
        dgn = jnp.zeros((1, 128), F32)
        for h in heads:
            on, rs = _rms(oraw_ref[:, vs[h]])
            r = r_ref[:, vs[h]]
            sig = jax.nn.sigmoid(r)
            sil = r * sig
            dy = do_ref[:, vs[h]]
            dr_l.append(dy * on * gn * (sig * (1.0 + r * (1.0 - sig))))
            dgn = dgn + jnp.sum(dy * sil * on, axis=0, keepdims=True)
            doh.append(_bf(_rms_bwd(dy * sil, on, rs, gn)).reshape(nc, CH, 128))
        dgnw_ref[...] += dgn
        qh = [_bf(qt[:, hs[h]]).reshape(nc, CH, CH) for h in heads]
        kth = [_bf(kt[:, hs[h]]).reshape(nc, CH, CH) for h in heads]
        khh = [_bf(kh[:, hs[h]]).reshape(nc, CH, CH) for h in heads]
        vh = [_bf(v_ref[:, vs[h]]).reshape(nc, CH, 128) for h in heads]
        at = [ein('cjd,cid->cji', kth[h], qh[h]) for h in heads]
        da = [ein('civ,cjv->cij', doh[h], vh[h]) for h in heads]
        dat = [ein('cjv,civ->cji', vh[h], doh[h]) for h in heads]
        gq = [ein('civ,cid->cvd', doh[h], qh[h]) for h in heads]
        stf = [sst_ref[:, h] for h in heads]
        dqs = [ein('civ,cvd->cid', doh[h], _bf(stf[h])) for h in heads]
        dv = [ein('cji,civ->cjv', _bf(jnp.where(triu, at[h], 0.0)), doh[h]) for h in heads]
        dqt = [ein('cij,cjd->cid', _bf(jnp.where(tril, da[h], 0.0)), kth[h]) + dqs[h] for h in heads]
        dkt = [ein('cji,cid->cjd', _bf(jnp.where(triu, dat[h], 0.0)), qh[h]) for h in heads]
        dse = []
        for h in heads:
            dst = dst_scr[h]
            dsend = [None] * nc
            for c in reversed(range(nc)):
                dsend[c] = dst
                dst = dst * eblast[c, :, hs[h]] + gq[h][c]
            dst_scr[h] = dst
            dse.append(jnp.stack(dsend))
        dseb = [_bf(d) for d in dse]
        dv = [dv[h] + ein('cjd,cvd->cjv', khh[h], dseb[h]) for h in heads]
        dkh = [ein('cjv,cvd->cjd', vh[h], dseb[h]) for h in heads]
        carried = jnp.concatenate([jnp.sum(dse[h] * stf[h], axis=1, keepdims=True) for h in heads], axis=2)
        wide = lambda parts: jnp.concatenate([p.reshape(TM, CH) for p in parts], axis=1)
        dqt_w, dkt_w, dkh_w = wide(dqt), wide(dkt), wide(dkh)
        dkh_kh = dkh_w * kh
        extra = jnp.sum(dkh_kh.reshape(nc, CH, 256), axis=1, keepdims=True) + eblast * carried
        db = dqt_w * qt - dkt_w * kt - dkh_kh
        dg = _dot3(upper, db) + jnp.broadcast_to(extra, (nc, CH, 256)).reshape(TM, 256)
        dz = dg * dgate_ref[...]
        dzb = _bf(dz)
        dlr_ref[...] = _bf(_dot_nt(dzb, wgu_ref[...]))
        dwgu_ref[...] += _dot_tn(_bf(lr), dzb)
        dbg_ref[...] += jnp.sum(dz, axis=0, keepdims=True)
        dq = dqt_w * eb * 0.125
        dk = dkt_w * enb + dkh_w * ebl
        dgla_ref[...] = _bf(jnp.concatenate([dq, dk] + [d.reshape(TM, 128) for d in dv] + dr_l, axis=1))

    def spec(w, cb):
        return pl.BlockSpec((TM, w), lambda i: (blk(i), cb))

    def acc(shape):
        return pl.BlockSpec(shape, lambda i: (0, 0))

    return _call(
        body, "gla_bwd", (nt,),
        [spec(256, 0), spec(256, 1), spec(512, 1), spec(512, 2), spec(128, C_LR // 128), spec(512, 0),
         pl.BlockSpec((nc, 4, 128, CH), lambda i: (blk(i), 0, 0, 0)), spec(256, 0), spec(256, 0), spec(512, 0),
         VMEM_SPEC, VMEM_SPEC],
        [spec(1536, 0), spec(128, 0), acc((128, 256)), acc((1, 256)), acc((1, 128))],
        [jax.ShapeDtypeStruct((t + TM, 1536), BF16), jax.ShapeDtypeStruct((t + TM, 128), BF16),
         jax.ShapeDtypeStruct((128, 256), F32), jax.ShapeDtypeStruct((1, 256), F32), jax.ShapeDtypeStruct((1, 128), F32)],
        [pltpu.VMEM((4, 128, CH), F32)], _cp(48), (proj, proj, proj, proj, proj, oraw, sst, bcum, dgate, do, wgu, gnw), comm)


def _rope_tables(t):
    r = t + TM
    row = np.arange(r)
    pos = np.where(row < t, row + NM, np.where(row < t + NM, row - t, 0)).astype(np.float32)
    inv_freq = (1.0 / (np.float32(ROPE_THETA) ** (np.arange(0, 16, 2, dtype=np.float32) / np.float32(16)))).astype(np.float32)
    ang = (pos[:, None] * inv_freq[None, :]).astype(np.float32)
    cos, sin = np.cos(ang).astype(np.float32), np.sin(ang).astype(np.float32)
    one, zero = np.ones((r, 48), np.float32), np.zeros((r, 48), np.float32)
    return jnp.asarray(np.concatenate([cos, cos, one, -sin, sin, zero], axis=1))


def _rope(x, tab, sign):
    w = x.shape[1]
    rep = w // 64
    c = jnp.concatenate([tab[:, 0:64]] * rep, axis=1)
    s = jnp.concatenate([tab[:, 64:128]] * rep, axis=1)
    lane = lax.rem(lax.broadcasted_iota(jnp.int32, x.shape, 1), 64)
    partner = jnp.where(lane < 8, pltpu.roll(x, w - 8, 1), jnp.where(lane < 16, pltpu.roll(x, 8, 1), 0.0))
    return x * c + sign * (partner * s)


HB_BWD = 4


def _stack(x, hg):
    w = x.shape[1] // hg
    return x if hg == 1 else jnp.concatenate([x[:, g * w:(g + 1) * w] for g in range(hg)], axis=0)


def _unstack(x, hg):
    return x if hg == 1 else jnp.concatenate([x[g * SB:(g + 1) * SB] for g in range(hg)], axis=1)


def _swa_specs(nsb):
    def rows(h, w, cb, f):
        return pl.BlockSpec((h, w), lambda i: (f(i), cb))
    pair = lambda i: i
    prev = lambda i: jnp.maximum(2 * i - 1, 0)
    meta = lambda i: nsb
    return rows, pair, prev, meta


def _swa_fwd(proj, sinks, t, comm=None):
    nsb = t // SB
    r_tot = t + TM
    rows, pair, prev, meta = _swa_specs(nsb)

    def body(sink_ref, q_ref, kc_ref, kp_ref, km_ref, vc_ref, vp_ref, vm_ref, o_ref, lse_ref):
        i = pl.program_id(0)
        key = lax.broadcasted_iota(jnp.int32, (SB, SB), 0)
        qry = lax.broadcasted_iota(jnp.int32, (SB, SB), 1)
        km, vm = km_ref[0:NM, :], vm_ref[0:NM, :]
        for j in range(2):
            b = 2 * i + j
            rs = slice(j * SB, (j + 1) * SB)
            real = b < nsb
            masks = (key <= qry, (key > qry) & (b > 0) & real, real)
            k3 = (kc_ref[rs, :], kp_ref[...] if j == 0 else kc_ref[0:SB, :], km)
            v3 = (vc_ref[rs, :], vp_ref[...] if j == 0 else vc_ref[0:SB, :], vm)
            valid = b * SB + lax.broadcasted_iota(jnp.int32, (1, SB), 1) < t + NM
            heads = range(8)
            kb = [[_bf(k[:, kv * 64:(kv + 1) * 64]) for k in k3] for kv in range(2)]
            vt = [[_bf(v[:, kv * 64:(kv + 1) * 64].T) for v in v3] for kv in range(2)]
            raw = [[_dot_nt(k, _bf(q_ref[rs, h * 64:(h + 1) * 64])) for k in kb[h // 4]] for h in heads]
            probs, inv_l, lse_l = [], [], []
            for h in heads:
                s = [jnp.where(m, sx, NEG) for m, sx in zip(masks, raw[h])]
                sink = sink_ref[0, h]
                top = jnp.maximum(jnp.max(jnp.maximum(s[0], s[1]), axis=0, keepdims=True),
                                  jnp.maximum(jnp.max(s[2], axis=0, keepdims=True), sink))
                p = [jnp.exp(sx - top) for sx in s]
                l = (jnp.sum(p[0] + p[1], axis=0, keepdims=True) + jnp.sum(p[2], axis=0, keepdims=True)
                     + jnp.exp(sink - top))
                probs.append([_bf(px) for px in p])
                inv_l.append(1.0 / l)
                lse_l.append(top + jnp.log(l))
            o_t = [_dot(vt[h // 4][0], probs[h][0]) + _dot(vt[h // 4][1], probs[h][1]) + _dot(vt[h // 4][2], probs[h][2])
                   for h in heads]
            o_ref[rs, :] = _bf(jnp.concatenate([jnp.where(valid, o_t[h] * inv_l[h], 0.0).T for h in heads], axis=1))
            lse_ref[:, rs] = jnp.concatenate(lse_l, axis=0)

    ck, cv = C_SK // 128, C_SV // 128
    return _call(
        body, "swa_fwd", (r_tot // QB,),
        [SMEM_SPEC, rows(QB, 512, C_SQ // 512, pair),
         rows(QB, 128, ck, pair), rows(SB, 128, ck, prev), rows(SB, 128, ck, meta),
         rows(QB, 128, cv, pair), rows(SB, 128, cv, prev), rows(SB, 128, cv, meta)],
        [rows(QB, 512, 0, pair), pl.BlockSpec((8, QB), lambda i: (0, i))],
        [jax.ShapeDtypeStruct((r_tot, 512), BF16), jax.ShapeDtypeStruct((8, r_tot), F32)],
        [], _cp(32), (sinks, proj, proj, proj, proj, proj, proj, proj), comm)


def _swa_bwd(proj, sinks, lse_t, do, t, comm=None):
    nsb = t // SB
    r_tot = t + TM
    rows, pair, prev, meta = _swa_specs(nsb)
    hb = HB_BWD
    lanes = hb * SB

    def body(sink_ref, q_ref, kc_ref, kp_ref, km_ref, vc_ref, vp_ref, vm_ref, lse_ref, do_ref,
             dq_ref, dk_ref, dv_ref, dsink_ref):
        i = pl.program_id(0)

        @pl.when(i == 0)
        def _():
            dk_ref[...] = jnp.zeros_like(dk_ref)
            dv_ref[...] = jnp.zeros_like(dv_ref)
            dsink_ref[...] = jnp.zeros_like(dsink_ref)

        key = lax.broadcasted_iota(jnp.int32, (SB, lanes), 0)
        qry = lax.rem(lax.broadcasted_iota(jnp.int32, (SB, lanes), 1), SB)
        km, vm = km_ref[0:NM, :], vm_ref[0:NM, :]
        dsink_l = []
        for j in range(2):
            b = 2 * i + j
            rs = slice(j * SB, (j + 1) * SB)
            real = b < nsb
            masks = (key <= qry, (key > qry) & (b > 0) & real, real)
            k3 = (kc_ref[rs, :], kp_ref[...] if j == 0 else kc_ref[0:SB, :], km)
            v3 = (vc_ref[rs, :], vp_ref[...] if j == 0 else vc_ref[0:SB, :], vm)
            groups = list(range(0, 8, hb))
            kvs = [h0 // 4 for h0 in groups]
            qg = [_bf(_stack(q_ref[rs, h0 * 64:(h0 + hb) * 64], hb)) for h0 in groups]
            dog = [_bf(_stack(do_ref[rs, h0 * 64:(h0 + hb) * 64], hb)) for h0 in groups]
            kb = [[_bf(k[:, kv * 64:(kv + 1) * 64]) for k in k3] for kv in kvs]
            vb = [[_bf(v[:, kv * 64:(kv + 1) * 64]) for v in v3] for kv in kvs]
            s = [[_dot_nt(k, qg[g]) for k in kb[g]] for g in range(len(groups))]
            dp = [[_dot_nt(v, dog[g]) for v in vb[g]] for g in range(len(groups))]
            p, ds, ds_blk = [], [], []
            for g, h0 in enumerate(groups):
                lse_row = jnp.concatenate([lse_ref[h:h + 1, rs] for h in range(h0, h0 + hb)], axis=1)
                sink_row = jnp.concatenate([jnp.full((1, SB), sink_ref[0, h], F32) for h in range(h0, h0 + hb)], axis=1)
                pg = [jnp.exp(jnp.where(m, sx, NEG) - lse_row) for m, sx in zip(masks, s[g])]
                delta = (jnp.sum(pg[0] * dp[g][0] + pg[1] * dp[g][1], axis=0, keepdims=True)
                         + jnp.sum(pg[2] * dp[g][2], axis=0, keepdims=True))
                ds.append([_bf(pp * (dd - delta)) for pp, dd in zip(pg, dp[g])])
                p.append([_bf(pp) for pp in pg])
                ds_row = -jnp.exp(sink_row - lse_row) * delta
                ds_blk += [jnp.sum(ds_row[:, q0 * SB:(q0 + 1) * SB], axis=1, keepdims=True) for q0 in range(hb)]
            dsink_l.append(jnp.concatenate(ds_blk, axis=1))
            dq_t = [_dot_tn(kb[g][0], ds[g][0]) + _dot_tn(kb[g][1], ds[g][1]) + _dot_tn(kb[g][2], ds[g][2])
                    for g in range(len(groups))]
            dq_ref[rs, :] = jnp.concatenate([_unstack(d.T, hb) for d in dq_t], axis=1)
            windows = (pl.ds(pl.multiple_of(b * SB, SB), SB), pl.ds(pl.multiple_of(jnp.maximum(b - 1, 0) * SB, SB), SB),
                       pl.ds(t, NM))
            for x in range(3):
                dk_kv, dv_kv = [], []
                for kv in range(2):
                    mine = [g for g in range(len(groups)) if kvs[g] == kv]
                    dk_kv.append(sum(_dot(ds[g][x], qg[g]) for g in mine))
                    dv_kv.append(sum(_dot(p[g][x], dog[g]) for g in mine))
                dk_ref[windows[x], :] += jnp.concatenate(dk_kv, axis=1)
                dv_ref[windows[x], :] += jnp.concatenate(dv_kv, axis=1)
        dsink_ref[...] += dsink_l[0] + dsink_l[1]

    ck, cv = C_SK // 128, C_SV // 128
    whole = lambda w: pl.BlockSpec((r_tot, w), lambda i: (0, 0))
    return _call(
        body, "swa_bwd", (r_tot // QB,),
        [SMEM_SPEC, rows(QB, 512, C_SQ // 512, pair),
         rows(QB, 128, ck, pair), rows(SB, 128, ck, prev), rows(SB, 128, ck, meta),
         rows(QB, 128, cv, pair), rows(SB, 128, cv, prev), rows(SB, 128, cv, meta),
         pl.BlockSpec((8, QB), lambda i: (0, i)), rows(QB, 512, 1, pair)],
        [rows(QB, 512, 0, pair), whole(128), whole(128), pl.BlockSpec((1, 8), lambda i: (0, 0))],
        [jax.ShapeDtypeStruct((r_tot, 512), F32), jax.ShapeDtypeStruct((r_tot, 128), F32),
         jax.ShapeDtypeStruct((r_tot, 128), F32), jax.ShapeDtypeStruct((1, 8), F32)],
        [], _cp(48), (sinks, proj, proj, proj, proj, proj, proj, proj, lse_t, do), comm)


HK = D // 2


def _mlp_fwd(x, metapad, tgt, ogla, oswa, wo, wff, w1, w2, wfin):
    t = x.shape[0]
    nblk = t // TM

    def body(x_ref, mp_ref, tgt_ref, og_ref, os_ref, wo_ref, wff_ref, w1a_ref, w1b_ref, w2a_ref, w2b_ref, wfin_ref,
             h1_ref, f_ref, a_ref, dh2_ref, loss_ref, gfin_ref):
        i = pl.program_id(0)

        @pl.when(i == 0)
        def _():
            loss_ref[...] = jnp.zeros_like(loss_ref)
            gfin_ref[...] = jnp.zeros_like(gfin_ref)

        h0 = jnp.where(i == nblk, mp_ref[...], x_ref[...])
        h1 = h0 + _dot(og_ref[...], wo_ref[0:512, :]) + _dot(os_ref[...], wo_ref[512:1024, :])
        h1_ref[...] = h1
        fh, _ = _rms(h1)
        f = _bf(fh * wff_ref[...])
        f_ref[...] = f
        acc = jnp.zeros((TM, D), F32)
        for n in range(4):
            a = _dot(f[:, 0:HK], w1a_ref[n]) + _dot(f[:, HK:D], w1b_ref[n])
            a_ref[:, n * D:(n + 1) * D] = _bf(a)
            zr = jnp.maximum(a, 0.0)
            z = _bf(zr * zr)
            acc = acc + _dot(z[:, 0:HK], w2a_ref[n]) + _dot(z[:, HK:D], w2b_ref[n])
        h2 = h1 + acc
        yh, rs2 = _rms(h2)
        wf = wfin_ref[...]
        real = i < nblk
        e = jnp.where(real, yh * wf - tgt_ref[...], 0.0)
        loss_ref[...] += jnp.sum(jnp.sum(e * e, axis=0, keepdims=True), axis=1, keepdims=True) * (0.5 / D)
        dy = e * (1.0 / D)
        gfin_ref[...] += jnp.sum(dy * yh, axis=0, keepdims=True)
        dh2_ref[...] = _rms_bwd(dy, yh, rs2, wf)

    xs = pl.BlockSpec((TM, D), lambda i: (jnp.minimum(i, nblk - 1), 0))
    rs = lambda w: pl.BlockSpec((TM, w), lambda i: (i, 0))
    r_tot = t + TM
    return pl.pallas_call(
        body, name="mlp_fwd", grid=(nblk + 1,),
        in_specs=[xs, VMEM_SPEC, xs, rs(512), rs(512)] + [VMEM_SPEC] * 7,
        out_specs=[rs(D), rs(D), rs(DFF), rs(D), pl.BlockSpec((1, 1), lambda i: (0, 0)), pl.BlockSpec((1, D), lambda i: (0, 0))],
        out_shape=[jax.ShapeDtypeStruct((r_tot, D), F32), jax.ShapeDtypeStruct((r_tot, D), BF16),
                   jax.ShapeDtypeStruct((r_tot, DFF), BF16), jax.ShapeDtypeStruct((r_tot, D), F32),
                   jax.ShapeDtypeStruct((1, 1), F32), jax.ShapeDtypeStruct((1, D), F32)],
        compiler_params=_cp(56),
    )(x, metapad, tgt, ogla, oswa, wo, wff, *w1, *w2, wfin)


def _mlp_bwd(h1, a, dh2, ogla, oswa, wo, wff, w1, w2):
    r_tot = h1.shape[0]
    nt = r_tot // TM

    def body(h1_ref, a_ref, dh2_ref, og_ref, os_ref, wo_ref, wff_ref, w1a_ref, w1b_ref, w2a_ref, w2b_ref,
             da_ref, dh2b_ref, dh1_ref, do_ref, dwo_ref, gff_ref, dwo_acc):
        i = pl.program_id(0)

        @pl.when(i == 0)
        def _():
            dwo_acc[...] = jnp.zeros_like(dwo_acc)
            gff_ref[...] = jnp.zeros_like(gff_ref)

        dh2 = dh2_ref[...]
        dh2b = _bf(dh2)
        dh2b_ref[...] = dh2b
        dfa = jnp.zeros((TM, HK), F32)
        dfb = jnp.zeros((TM, HK), F32)
        for n in range(4):
            dz = jnp.concatenate([_dot_nt(dh2b, w2a_ref[n]), _dot_nt(dh2b, w2b_ref[n])], axis=1)
            da = _bf(dz * (2.0 * jnp.maximum(a_ref[:, n * D:(n + 1) * D].astype(F32), 0.0)))
            da_ref[:, n * D:(n + 1) * D] = da
            dfa = dfa + _dot_nt(da, w1a_ref[n])
            dfb = dfb + _dot_nt(da, w1b_ref[n])
        df = jnp.concatenate([dfa, dfb], axis=1)
        fh, rs1 = _rms(h1_ref[...])
        gff_ref[...] += jnp.sum(df * fh, axis=0, keepdims=True)
        dh1 = dh2 + _rms_bwd(df, fh, rs1, wff_ref[...])
        dh1_ref[...] = dh1
        dh1b = _bf(dh1)
        do_ref[...] = _dot_nt(dh1b, wo_ref[...])
        dwo_acc[0:512, :] += _dot_tn(og_ref[...], dh1b)
        dwo_acc[512:1024, :] += _dot_tn(os_ref[...], dh1b)

        @pl.when(i == nt - 1)
        def _():
            for s in range(4):
                for hh in range(2):
                    dwo_ref[hh, s] = dwo_acc[(2 * s + hh) * 128:(2 * s + hh + 1) * 128, :]

    rs = lambda w: pl.BlockSpec((TM, w), lambda i: (i, 0))
    return pl.pallas_call(
        body, name="mlp_bwd", grid=(nt,),
        in_specs=[rs(D), rs(DFF), rs(D), rs(512), rs(512)] + [VMEM_SPEC] * 6,
        out_specs=[rs(DFF), rs(D), rs(D), rs(D), VMEM_SPEC, pl.BlockSpec((1, D), lambda i: (0, 0))],
        out_shape=[jax.ShapeDtypeStruct((r_tot, DFF), BF16), jax.ShapeDtypeStruct((r_tot, D), BF16),
                   jax.ShapeDtypeStruct((r_tot, D), F32), jax.ShapeDtypeStruct((r_tot, D), F32),
                   jax.ShapeDtypeStruct((2, 4, 128, D), F32), jax.ShapeDtypeStruct((1, D), F32)],
        scratch_shapes=[pltpu.VMEM((D, D), F32)],
        compiler_params=_cp(56),
    )(h1, a, dh2, ogla, oswa, wo, wff, *w1, *w2)


def _ffn_wgrad(f, a, da, dh2b):
    r_tot = f.shape[0]
    kt = 768 if r_tot % 768 == 0 else TM
    nk = r_tot // kt

    def body(f_ref, a_ref, da_ref, dh2_ref, dw1_ref, dw2_ref, acc1, acc2):
        k = pl.program_id(1)

        @pl.when(k == 0)
        def _():
            acc1[...] = jnp.zeros_like(acc1)
            acc2[...] = jnp.zeros_like(acc2)

        zr = jnp.maximum(a_ref[...], 0.0)
        acc1[...] += _dot_tn(f_ref[...], da_ref[...])
        acc2[...] += _dot_tn(zr * zr, dh2_ref[...])

        @pl.when(k == nk - 1)
        def _():
            for hh in range(2):
                dw1_ref[hh, 0] = acc1[hh * 512:(hh + 1) * 512, :]
                dw2_ref[hh, 0] = acc2[hh * 512:(hh + 1) * 512, :]

    out = pl.BlockSpec((2, 1, 512, D), lambda n, k: (0, n, 0, 0))
    return pl.pallas_call(
        body, name="ffn_wgrad", grid=(4, nk),
        in_specs=[pl.BlockSpec((kt, D), lambda n, k: (k, 0)), pl.BlockSpec((kt, D), lambda n, k: (k, n)),
                  pl.BlockSpec((kt, D), lambda n, k: (k, n)), pl.BlockSpec((kt, D), lambda n, k: (k, 0))],
        out_specs=[out, out],
        out_shape=[jax.ShapeDtypeStruct((2, 4, 512, D), F32)] * 2,
        scratch_shapes=[pltpu.VMEM((D, D), F32), pltpu.VMEM((D, D), F32)],
        compiler_params=_cp(48, ("arbitrary", "arbitrary")),
    )(f, a, da, dh2b)


def _proj_bwd(x, metapad, wm, wt3, tabs, dgla, dswa_q, dsk, dsv, dlr, dh1, comm=None):
    t = x.shape[0]
    nblk = t // TM

    def body(x_ref, mp_ref, wm_ref, w3_ref, tab_ref, dg_ref, dq_ref, dk_ref, dv_ref, dlr_ref, dh1_ref,
             gx_ref, gmeta_ref, dw_ref, gmix_ref, w_ref, acc):
        i = pl.program_id(0)

        @pl.when(i == 0)
        def _():
            _join_shards(w3_ref, w_ref)
            acc[...] = jnp.zeros_like(acc)
            gmix_ref[...] = jnp.zeros_like(gmix_ref)

        h = jnp.where(i == nblk, mp_ref[...], x_ref[...])
        uh, rs = _rms(h)
        wm_v = wm_ref[...]
        u = _bf(uh * wm_v)
        tab = tab_ref[...]
        dq = _bf(_rope(dq_ref[...] * 0.125, tab, -1.0))
        dk = _bf(_rope(dk_ref[...], tab, -1.0))
        parts = ((dg_ref[...], 0, R_LR), (dlr_ref[:, 0:16], R_LR, 16), (dq, R_LR + 16, 512),
                 (dk, R_LR + 528, 128), (_bf(dv_ref[...]), R_LR + 656, 128))
        du = jnp.zeros((TM, D), F32)
        for val, r0, w in parts:
            du = du + _dot(val, w_ref[r0:r0 + w, :])
            acc[r0:r0 + w, :] += _dot_tn(val, u)
        gmix_ref[...] += jnp.sum(du * uh, axis=0, keepdims=True)
        dh0 = dh1_ref[...] + _rms_bwd(du, uh, rs, wm_v)

        @pl.when(i < nblk)
        def _():
            gx_ref[...] = dh0

        @pl.when(i == nblk)
        def _():
            gmeta_ref[...] = dh0[:NM]
            for s in range(4):
                dw_ref[s] = acc[(DIN // 4) * s:(DIN // 4) * (s + 1), :]

    xs = pl.BlockSpec((TM, D), lambda i: (jnp.minimum(i, nblk - 1), 0))
    rs_ = lambda w: pl.BlockSpec((TM, w), lambda i: (i, 0))
    return _call(
        body, "proj_bwd", (nblk + 1,),
        [xs, VMEM_SPEC, VMEM_SPEC, VMEM_SPEC, rs_(128), rs_(1536), rs_(512), rs_(128), rs_(128), rs_(128), rs_(D)],
        [xs, pl.BlockSpec((NM, D), lambda i: (0, 0)), VMEM_SPEC, pl.BlockSpec((1, D), lambda i: (0, 0))],
        [jax.ShapeDtypeStruct((t, D), F32), jax.ShapeDtypeStruct((NM, D), F32),
         jax.ShapeDtypeStruct((4, DIN // 4, D), F32), jax.ShapeDtypeStruct((1, D), F32)],
        [pltpu.VMEM((DIN, D), BF16), pltpu.VMEM((DIN, D), F32)], _cp(56),
        (x, metapad, wm, wt3, tabs, dgla, dswa_q, dsk, dsv, dlr, dh1), comm)


def _place():
    return lax.axis_index("x"), lax.axis_index("y"), lax.axis_index("c")


def _other_chips(x, y):
    return [(1 - x, y), (x, 1 - y), (1 - x, 1 - y)]


def _dma_sems(*counts):
    return tuple(pltpu.SemaphoreType.DMA((k,)) for k in counts)


def _gather_shards(shards, split):
    n = len(shards)
    two = [a for a in range(n) if split[a]]

    def plan(ins, outs, sems):
        isend, irecv, dsend, drecv, loc = sems
        x, y, c = _place()
        chips = _other_chips(x, y)

        def part(ref, a, half):
            if not split[a]:
                return ref
            w = shards[a].shape[1] // 2
            return ref.at[:, pl.ds(pl.multiple_of(half * w, 128), w)]

        def over_ici(a, k, shard_of):
            tx, ty = chips[k]
            sx, sy = shard_of
            return pltpu.make_async_remote_copy(
                src_ref=part(ins[a], a, c), dst_ref=part(outs[a].at[2 * sx + sy], a, c), send_sem=isend.at[3 * a + k],
                recv_sem=irecv.at[3 * a + k], device_id=(tx, ty, c), device_id_type=MESH)

        def over_d2d(a, k, half):
            tx, ty = chips[k]
            ref = part(outs[a].at[2 * tx + ty], a, half)
            return pltpu.make_async_remote_copy(
                src_ref=ref, dst_ref=ref, send_sem=dsend.at[3 * a + k], recv_sem=drecv.at[3 * a + k],
                device_id=(x, y, 1 - c), device_id_type=MESH)

        def local(a):
            return pltpu.make_async_copy(ins[a], outs[a].at[2 * x + y], loc.at[a])

        pairs = [(a, k) for a in range(n) for k in range(3)]
        first = ([lambda a=a: local(a).start() for a in range(n)]
                 + [lambda a=a, k=k: over_ici(a, k, (x, y)).start() for a, k in pairs],
                 [lambda a=a, k=k: over_ici(a, k, chips[k]).wait_recv() for a, k in pairs]
                 + [lambda a=a, k=k: over_ici(a, k, (x, y)).wait_send() for a, k in pairs]
                 + [lambda a=a: local(a).wait() for a in range(n)])
        pairs2 = [(a, k) for a in two for k in range(3)]
        second = ([lambda a=a, k=k: over_d2d(a, k, c).start() for a, k in pairs2],
                  [lambda a=a, k=k: over_d2d(a, k, 1 - c).wait_recv() for a, k in pairs2]
                  + [lambda a=a, k=k: over_d2d(a, k, c).wait_send() for a, k in pairs2])
        return [first, second] if two else [first]

    return _Comm(tuple(shards), tuple(jax.ShapeDtypeStruct((4,) + s.shape, s.dtype) for s in shards),
                 _dma_sems(3 * n, 3 * n, 3 * n, 3 * n, n), 2 if two else 1, plan)


def _swap_halves(grads):
    n = len(grads)

    def plan(ins, outs, sems):
        send, recv = sems
        x, y, c = _place()

        def swap(a):
            return pltpu.make_async_remote_copy(
                src_ref=ins[a].at[1 - c], dst_ref=outs[a], send_sem=send.at[a], recv_sem=recv.at[a],
                device_id=(x, y, 1 - c), device_id_type=MESH)

        return [([lambda a=a: swap(a).start() for a in range(n)], [lambda a=a: swap(a).wait() for a in range(n)])]

    return _Comm(tuple(grads), tuple(jax.ShapeDtypeStruct(g.shape[1:], g.dtype) for g in grads), _dma_sems(n, n), 1, plan)


def _scatter_shards(parts):
    n = len(parts)

    def plan(ins, outs, sems):
        send, recv = sems
        x, y, c = _place()
        chips = _other_chips(x, y)

        def scatter(a, k):
            tx, ty = chips[k]
            return pltpu.make_async_remote_copy(
                src_ref=ins[a].at[2 * tx + ty], dst_ref=outs[a].at[k], send_sem=send.at[3 * a + k],
                recv_sem=recv.at[3 * a + k], device_id=(tx, ty, c), device_id_type=MESH)

        pairs = [(a, k) for a in range(n) for k in range(3)]
        return [([lambda a=a, k=k: scatter(a, k).start() for a, k in pairs],
                 [lambda a=a, k=k: scatter(a, k).wait() for a, k in pairs])]

    return _Comm(tuple(parts), tuple(jax.ShapeDtypeStruct((3,) + p.shape[1:], p.dtype) for p in parts),
                 _dma_sems(3 * n, 3 * n), 1, plan)


def _join_halves(halves):
    n = len(halves)

    def plan(ins, outs, sems):
        send, recv, loc = sems
        x, y, c = _place()

        def remote(a, half):
            return pltpu.make_async_remote_copy(
                src_ref=ins[a], dst_ref=outs[a].at[half], send_sem=send.at[a], recv_sem=recv.at[a],
                device_id=(x, y, 1 - c), device_id_type=MESH)

        def local(a):
            return pltpu.make_async_copy(ins[a], outs[a].at[c], loc.at[a])

        every = range(n)
        return [([lambda a=a: local(a).start() for a in every] + [lambda a=a: remote(a, c).start() for a in every],
                 [lambda a=a: remote(a, 1 - c).wait_recv() for a in every]
                 + [lambda a=a: remote(a, c).wait_send() for a in every] + [lambda a=a: local(a).wait() for a in every])]

    return _Comm(tuple(halves), tuple(jax.ShapeDtypeStruct((2,) + h.shape, h.dtype) for h in halves),
                 _dma_sems(n, n, n), 1, plan)


def _reduce_w_in(dwt, comm):
    rows, hw = DIN // 4, D // 2
    ci, co = len(comm.ins), len(comm.outs)

    def body(*refs):
        dw_ref, c_in, out_ref, c_out = refs[0], refs[1:1 + ci], refs[1 + ci], refs[2 + ci:2 + ci + co]
        mine, sib, tosend, rbuf, qbuf, full, send, recv, loc = refs[2 + ci + co:11 + ci + co]
        c_sem = refs[11 + ci + co:]
        x, y, c = _place()
        sibling = (x, y, 1 - c)
        (starts, waits), = comm.plan(c_in, c_out, c_sem)
        _run_phase(starts)

        def cols(ref, half):
            window = pl.ds(pl.multiple_of(half * hw, 128), hw)
            return ref.at[:, :, window] if len(ref.shape) == 3 else ref.at[:, window]

        load = pltpu.make_async_copy(cols(dw_ref, c), mine, loc.at[0])
        give = pltpu.make_async_remote_copy(src_ref=cols(dw_ref, 1 - c), dst_ref=sib, send_sem=send.at[3], recv_sem=recv.at[3],
                                            device_id=sibling, device_id_type=MESH)
        load.start()
        give.start()
        load.wait()
        give.wait()
        mine[...] = mine[...] + sib[...]
        cps = []
        for k, (tx, ty) in enumerate(_other_chips(x, y)):
            tosend[k] = _bf(mine[2 * tx + ty])
            cps.append(pltpu.make_async_remote_copy(
                src_ref=tosend.at[k], dst_ref=rbuf.at[k], send_sem=send.at[k], recv_sem=recv.at[k],
                device_id=(tx, ty, c), device_id_type=MESH))
            cps[-1].start()
        for cp in cps:
            cp.wait()
        qbuf[...] = mine[2 * x + y] + rbuf[0].astype(F32) + rbuf[1].astype(F32) + rbuf[2].astype(F32)
        keep = pltpu.make_async_copy(qbuf, cols(full, c), loc.at[1])
        pass_on = pltpu.make_async_remote_copy(src_ref=qbuf, dst_ref=cols(full, c), send_sem=send.at[4], recv_sem=recv.at[4],
                                               device_id=sibling, device_id_type=MESH)
        keep.start()
        pass_on.start()
        keep.wait()
        pass_on.wait_send()
        pltpu.make_async_remote_copy(src_ref=qbuf, dst_ref=cols(full, 1 - c), send_sem=send.at[4], recv_sem=recv.at[4],
                                     device_id=sibling, device_id_type=MESH).wait_recv()
        out_ref[...] = full[...]
        _run_phase(waits)

    outs = pl.pallas_call(
        body, name="reduce_w_in",
        in_specs=[ANY_SPEC] * (1 + ci), out_specs=[VMEM_SPEC] + [ANY_SPEC] * co,
        out_shape=[jax.ShapeDtypeStruct((rows, D), F32)] + list(comm.outs),
        scratch_shapes=[pltpu.VMEM((4, rows, hw), F32), pltpu.VMEM((4, rows, hw), F32), pltpu.VMEM((3, rows, hw), BF16),
                        pltpu.VMEM((3, rows, hw), BF16), pltpu.VMEM((rows, hw), F32), pltpu.VMEM((rows, D), F32),
                        *_dma_sems(5, 5, 2), *comm.sems],
        compiler_params=pltpu.CompilerParams(vmem_limit_bytes=48 << 20),
    )(dwt, *comm.ins)
    return outs[0], outs[1:]


def _allreduce_small(pack):
    p = pack.shape[0]

    def body(in_ref, out_ref, buf, send, recv):
        x, y, c = _place()
        me = 4 * x + 2 * y + c
        buf[me] = in_ref[...]

        def peer_of(k):
            return x ^ (k >> 2), y ^ ((k >> 1) & 1), c ^ (k & 1)

        sends = [pltpu.make_async_remote_copy(
            src_ref=in_ref, dst_ref=buf.at[me], send_sem=send.at[k - 1], recv_sem=recv.at[k - 1],
            device_id=peer_of(k), device_id_type=MESH) for k in range(1, 8)]
        for cp in sends:
            cp.start()
        for k in range(1, 8):
            px, py, pc = peer_of(k)
            pltpu.make_async_remote_copy(
                src_ref=in_ref, dst_ref=buf.at[4 * px + 2 * py + pc], send_sem=send.at[k - 1], recv_sem=recv.at[k - 1],
                device_id=(x, y, c), device_id_type=MESH).wait_recv()
        for cp in sends:
            cp.wait_send()
        acc = buf[0]
        for d in range(1, 8):
            acc = acc + buf[d]
        out_ref[...] = acc

    return pl.pallas_call(
        body, name="allreduce_small",
        in_specs=[VMEM_SPEC], out_specs=VMEM_SPEC, out_shape=jax.ShapeDtypeStruct(pack.shape, F32),
        scratch_shapes=[pltpu.VMEM((8, p, D), F32), *_dma_sems(7, 7)],
    )(pack)


GRID4 = 4


def _sum_cores(core_shard, mine, theirs):
    n = len(mine)

    def body(cs_ref, *refs):
        ms, ts, bfs, owns = refs[:n], refs[n:2 * n], refs[2 * n:3 * n], refs[3 * n:]
        keep = pl.program_id(0) == cs_ref[1]
        for a in range(n):
            acc = ms[a][0, 0] + ts[a][0]
            bfs[a][0] = _bf(acc)

            @pl.when(keep)
            def _():
                owns[a][...] = acc

    shapes = [m.shape[2:] for m in mine]
    in_specs = ([pl.BlockSpec((1, 1) + s, lambda i, cs: (cs[0], i, 0, 0)) for s in shapes]
                + [pl.BlockSpec((1,) + s, lambda i, cs: (i, 0, 0)) for s in shapes])
    out_specs = ([pl.BlockSpec((1,) + s, lambda i, cs: (i, 0, 0)) for s in shapes]
                 + [pl.BlockSpec(s, lambda i, cs: (0, 0)) for s in shapes])
    outs = pl.pallas_call(
        body, name="sum_cores",
        grid_spec=pltpu.PrefetchScalarGridSpec(num_scalar_prefetch=1, grid=(4,), in_specs=in_specs, out_specs=out_specs),
        out_shape=[jax.ShapeDtypeStruct((4,) + s, BF16) for s in shapes] + [jax.ShapeDtypeStruct(s, F32) for s in shapes],
        compiler_params=_cp(48),
    )(core_shard, *mine, *theirs)
    return outs[:n], outs[n:]


def _sum_chips(own, arrived):
    n = len(own)

    def body(*refs):
        os_, ars, outs = refs[:n], refs[n:2 * n], refs[2 * n:]
        for a in range(n):
            outs[a][...] = os_[a][...] + ars[a][0].astype(F32) + ars[a][1].astype(F32) + ars[a][2].astype(F32)

    blocks = [(o.shape[0] // GRID4, o.shape[1]) for o in own]
    return pl.pallas_call(
        body, name="sum_chips", grid=(GRID4,),
        in_specs=([pl.BlockSpec(b, lambda i: (i, 0)) for b in blocks]
                  + [pl.BlockSpec((3,) + b, lambda i: (0, i, 0)) for b in blocks]),
        out_specs=[pl.BlockSpec(b, lambda i: (i, 0)) for b in blocks],
        out_shape=[jax.ShapeDtypeStruct(o.shape, F32) for o in own],
        compiler_params=_cp(32),
    )(*own, *arrived)


def _adamw_math(w, g, m, v):
    m2 = ADAM_B1 * m + (1.0 - ADAM_B1) * g
    v2 = ADAM_B2 * v + (1.0 - ADAM_B2) * (g * g)
    m_hat = m2 / (1.0 - ADAM_B1 ** ADAM_STEP)
    v_hat = v2 / (1.0 - ADAM_B2 ** ADAM_STEP)
    return -ADAM_LR * (m_hat / (jnp.sqrt(v_hat) + ADAM_EPS) + ADAM_WD * w), m2, v2


def _adamw_big(ws, gs, ms, vs):
    n = len(ws)

    def body(*refs):
        for a in range(n):
            d, m2, v2 = _adamw_math(refs[a][...], refs[n + a][...], refs[2 * n + a][...], refs[3 * n + a][...])
            refs[4 * n + a][...] = d
            refs[5 * n + a][...] = m2
            refs[6 * n + a][...] = v2

    specs = [pl.BlockSpec((w.shape[0] // GRID4, w.shape[1]), lambda i: (i, 0)) for w in ws]
    return pl.pallas_call(
        body, name="adamw_big", grid=(GRID4,),
        in_specs=specs * 4, out_specs=specs * 3,
        out_shape=[jax.ShapeDtypeStruct(w.shape, F32) for w in ws] * 3,
        compiler_params=_cp(48),
    )(*ws, *gs, *ms, *vs)


def _adamw_small(ws, gs, ms, vs):
    n = len(ws)

    def body(*refs):
        for a in range(n):
            d, m2, v2 = _adamw_math(refs[a][...], refs[n + a][...], refs[2 * n + a][...], refs[3 * n + a][...])
            refs[4 * n + a][...] = d
            refs[5 * n + a][...] = m2
            refs[6 * n + a][...] = v2

    return pl.pallas_call(
        body, name="adamw_small",
        in_specs=[VMEM_SPEC] * (4 * n), out_specs=[VMEM_SPEC] * (3 * n),
        out_shape=[jax.ShapeDtypeStruct(w.shape, F32) for w in ws] * 3,
        compiler_params=pltpu.CompilerParams(vmem_limit_bytes=40 << 20),
    )(*ws, *gs, *ms, *vs)


def kernel(x, meta_tokens, norm_mix_w, w_in, w_gate_up, b_gate, gla_norm_w, sinks, w_out, norm_ff_w, w_ff1, w_ff2, final_norm_w, loss_target, m_meta_tokens, m_norm_mix_w, m_w_in, m_w_gate_up, m_b_gate, m_gla_norm_w, m_sinks, m_w_out, m_norm_ff_w, m_w_ff1, m_w_ff2, m_final_norm_w, v_meta_tokens, v_norm_mix_w, v_w_in, v_w_gate_up, v_b_gate, v_gla_norm_w, v_sinks, v_w_out, v_norm_ff_w, v_w_ff1, v_w_ff2, v_final_norm_w):
    xi, yi, ci = _place()
    shard = (2 * xi + yi).astype(jnp.int32).reshape(1)
    core = ci.astype(jnp.int32).reshape(1)

    small = jnp.concatenate([meta_tokens, w_gate_up[0], jnp.zeros((NM, 64), F32)], axis=1)
    wt3, g_small = _run_comm(_gather_shards([_bf(w_in[0].T), small], [True, False]), "gather_w_in")
    meta = g_small[:, :, 0:256].transpose(1, 0, 2).reshape(NM, D)
    wgu = g_small[:, :, 256:320].transpose(1, 0, 2).reshape(NM, 256)

    xs, tgt = x[0], loss_target[0]
    t = xs.shape[0]
    wfin = final_norm_w.reshape(1, D)
    metapad = jnp.concatenate([meta, jnp.zeros((TM - NM, D), F32)], axis=0)
    wgu_p = _bf(jnp.concatenate([wgu, jnp.zeros((128 - 16, 256), F32)], axis=0))
    tabs = _rope_tables(t)

    w1s, w2s = _bf(w_ff1[0]), _bf(w_ff2[0])
    proj, (g_out, w1a, w1b) = _proj_fwd(xs, metapad, norm_mix_w, wt3, tabs,
                                        _gather_shards([_bf(w_out[0]), w1s[:HK], w1s[HK:]], [True] * 3))
    (oswa, lse), (w2a, w2b) = _swa_fwd(proj, sinks, t, _gather_shards([w2s[:HK], w2s[HK:]], [True] * 2))
    (ogla, oraw, sst, bcum, dgate), _ = _gla_fwd(proj, wgu_p, b_gate, gla_norm_w, t)
    wo, w1, w2 = g_out.reshape(D, D), (w1a, w1b), (w2a, w2b)
    h1, f, a, dh2, loss, gfin = _mlp_fwd(xs, metapad, tgt, ogla, oswa, wo, norm_ff_w, w1, w2, wfin)

    da, dh2b, dh1, do, dwo, gff = _mlp_bwd(h1, a, dh2, ogla, oswa, wo, norm_ff_w, w1, w2)
    dw1, dw2 = _ffn_wgrad(f, a, da, dh2b)
    big = [dwo, dw1, dw2]
    (dsq, dsk, dsv, dsink), theirs = _swa_bwd(proj, sinks, lse, do, t, _swap_halves(big))
    sums_bf, own = _sum_cores(jnp.concatenate([core, shard]), big, theirs)
    (dgla, dlr, dwgu, dbg, dgnw), arrived = _gla_bwd(proj, oraw, sst, bcum, dgate, do, wgu_p, gla_norm_w, t,
                                                     _scatter_shards(sums_bf))
    halves = _sum_chips(own, arrived)
    (gx, gmeta, dwt, gmix), _ = _proj_bwd(xs, metapad, norm_mix_w, wt3, tabs, dgla, dsq, dsk, dsv, dlr, dh1)

    gwt_in, joined = _reduce_w_in(dwt, _join_halves(halves))
    gw_out, gw_1, gw_2 = [j.reshape((-1, j.shape[2])) for j in joined]

    tail = jnp.concatenate([dbg, dgnw, dsink, loss, jnp.zeros((1, D - 256 - 128 - 8 - 1), F32)], axis=1)
    pack = jnp.concatenate([gmeta, gmix, gff, gfin, tail, dwgu[:16].reshape(4, D)], axis=0)
    tot = _allreduce_small(pack)
    g_meta = lax.dynamic_slice_in_dim(tot[0:NM], shard[0] * 256, 256, axis=1)
    g_mix, g_ff, g_fin = tot[16:17], tot[17:18], tot[18]
    g_bg, g_gnw, g_sinks, loss_tot = tot[19:20, 0:256], tot[19:20, 256:384], tot[19:20, 384:392], tot[19, 392]
    g_wgu = lax.dynamic_slice_in_dim(tot[20:24].reshape(NM, 256), shard[0] * 64, 64, axis=1)

    bo = _adamw_big([w_out[0], w_ff1[0], w_ff2[0]], [gw_out, gw_1, gw_2], [m_w_out[0], m_w_ff1[0], m_w_ff2[0]],
                    [v_w_out[0], v_w_ff1[0], v_w_ff2[0]])

    fin2 = lambda a: a.reshape(1, D)
    sw = [meta_tokens, norm_mix_w, w_gate_up[0], b_gate, gla_norm_w, sinks, norm_ff_w, fin2(final_norm_w), w_in[0].T]
    sg = [g_meta, g_mix, g_wgu, g_bg, g_gnw, g_sinks, g_ff, fin2(g_fin), gwt_in]
    sm = [m_meta_tokens, m_norm_mix_w, m_w_gate_up[0], m_b_gate, m_gla_norm_w, m_sinks, m_norm_ff_w, fin2(m_final_norm_w),
          m_w_in[0].T]
    sv = [v_meta_tokens, v_norm_mix_w, v_w_gate_up[0], v_b_gate, v_gla_norm_w, v_sinks, v_norm_ff_w, fin2(v_final_norm_w),
          v_w_in[0].T]
    so = _adamw_small(sw, sg, sm, sv)

    def ordered(small_o, big_o):
        meta_, mix_, wgu_, bg_, gnw_, sinks_, ff_, fin_, wt_ = small_o
        w_out_, w_1_, w_2_ = big_o
        return (meta_, mix_, wt_.T[None], wgu_[None], bg_, gnw_, sinks_, w_out_[None], ff_, w_1_[None], w_2_[None],
                fin_.reshape(D))

    grads = ordered(sg, [gw_out, gw_1, gw_2])
    deltas = ordered(so[0:9], bo[0:3])
    new_m = ordered(so[9:18], bo[3:6])
    new_v = ordered(so[18:27], bo[6:9])
    return (loss_tot, gx[None], *grads, *deltas, *new_m, *new_v)
```

```python
import functools
from typing import Callable, NamedTuple

import jax
import jax.numpy as jnp
import numpy as np
from jax import lax
from jax.experimental import pallas as pl
from jax.experimental.pallas import tpu as pltpu

F32 = jnp.float32
BF16 = jnp.bfloat16

D = 1024
DFF = 4096
NM = 16
TM = 256
DK = 64
CH = 128
SB = 128
QB = 2 * SB
EPS = 1e-5
C_GQ, C_GK, C_GV, C_GR, C_SQ, C_SK, C_SV, C_LR, DINP = 0, 256, 512, 1024, 1536, 2048, 2176, 2304, 2432
DIN = 2320
R_LR = 1536
ROPE_THETA = 500000.0
ADAM_LR, ADAM_B1, ADAM_B2, ADAM_EPS, ADAM_WD, ADAM_STEP = 0.001, 0.9, 0.999, 1e-08, 0.01, 10
NEG = -1e30
MESH = pl.DeviceIdType.MESH
VMEM_SPEC = pl.BlockSpec(memory_space=pltpu.VMEM)
ANY_SPEC = pl.BlockSpec(memory_space=pl.ANY)
SMEM_SPEC = pl.BlockSpec(memory_space=pltpu.SMEM)


def _cp(vmem_mb, sem=("arbitrary",)):
    return pltpu.CompilerParams(dimension_semantics=sem, vmem_limit_bytes=vmem_mb << 20)


def _dot(a, b):
    return jnp.dot(a, b, preferred_element_type=F32)


def _dot_nt(a, b):
    return lax.dot_general(a, b, (((1,), (1,)), ((), ())), preferred_element_type=F32)


def _dot_tn(a, b):
    return lax.dot_general(a, b, (((0,), (0,)), ((), ())), preferred_element_type=F32)


def _bf(x):
    return x.astype(BF16)


def _dot3(m01, x):
    x1 = _bf(x)
    r1 = x - x1.astype(F32)
    x2 = _bf(r1)
    x3 = _bf(r1 - x2.astype(F32))
    return _dot(m01, x1) + _dot(m01, x2) + _dot(m01, x3)


def _rms(h):
    rs = lax.rsqrt(jnp.mean(h * h, axis=-1, keepdims=True) + EPS)
    return h * rs, rs


def _rms_bwd(dy, yhat, rs, w):
    dyh = dy * w
    return rs * (dyh - yhat * jnp.mean(dyh * yhat, axis=-1, keepdims=True))


class _Comm(NamedTuple):
    ins: tuple
    outs: tuple
    sems: tuple
    phases: int
    plan: Callable


def _run_phase(fns):
    for fn in fns:
        fn()


def _call(body, name, grid, in_specs, out_specs, out_shape, scratch, params, args, comm=None):
    if comm is None:
        outs = pl.pallas_call(body, name=name, grid=grid, in_specs=in_specs, out_specs=out_specs, out_shape=out_shape,
                              scratch_shapes=scratch, compiler_params=params)(*args)
        return outs, None
    n_in, n_out, n_scr = len(in_specs), len(out_specs), len(scratch)
    ci, co = len(comm.ins), len(comm.outs)
    last = grid[0] - 1
    marks = [0, max(1, last - max(2, (last + 1) // 6))][:comm.phases]

    def wrapped(*refs):
        own_in, c_in = refs[:n_in], refs[n_in:n_in + ci]
        refs = refs[n_in + ci:]
        own_out, c_out = refs[:n_out], refs[n_out:n_out + co]
        refs = refs[n_out + co:]
        own_scr, c_sem = refs[:n_scr], refs[n_scr:]
        i = pl.program_id(0)

        for p, mark in enumerate(marks):
            @pl.when(i == mark)
            def _():
                plan = comm.plan(c_in, c_out, c_sem)
                if p > 0:
                    _run_phase(plan[p - 1][1])
                _run_phase(plan[p][0])

        body(*own_in, *own_out, *own_scr)

        @pl.when(i == last)
        def _():
            _run_phase(comm.plan(c_in, c_out, c_sem)[-1][1])

    outs = pl.pallas_call(
        wrapped, name=name, grid=grid, in_specs=list(in_specs) + [ANY_SPEC] * ci, out_specs=list(out_specs) + [ANY_SPEC] * co,
        out_shape=list(out_shape) + list(comm.outs), scratch_shapes=list(scratch) + list(comm.sems), compiler_params=params,
    )(*args, *comm.ins)
    return outs[:n_out], outs[n_out:]


def _run_comm(comm, name):
    ci, co = len(comm.ins), len(comm.outs)

    def body(*refs):
        for starts, waits in comm.plan(refs[:ci], refs[ci:ci + co], refs[ci + co:]):
            _run_phase(starts)
            _run_phase(waits)

    return pl.pallas_call(body, name=name, in_specs=[ANY_SPEC] * ci, out_specs=[ANY_SPEC] * co, out_shape=list(comm.outs),
                          scratch_shapes=list(comm.sems))(*comm.ins)


def _join_shards(w3_ref, w_ref):
    for s in range(4):
        w_ref[(DIN // 4) * s:(DIN // 4) * (s + 1), :] = w3_ref[s]


def _proj_fwd(x, metapad, wm, wt3, tabs, comm=None):
    t = x.shape[0]
    nblk = t // TM

    def body(x_ref, mp_ref, wm_ref, w3_ref, tab_ref, proj_ref, w_ref):
        i = pl.program_id(0)

        @pl.when(i == 0)
        def _():
            _join_shards(w3_ref, w_ref)

        h = jnp.where(i == nblk, mp_ref[...], x_ref[...])
        u, _ = _rms(h)
        ub = _bf(u * wm_ref[...])
        proj_ref[:, 0:C_SQ] = _dot_nt(ub, w_ref[0:R_LR, :])
        att = _dot_nt(ub, w_ref[R_LR + 16:DIN, :])
        tab = tab_ref[...]
        proj_ref[:, C_SQ:C_SK] = _rope(att[:, 0:512], tab, 1.0) * 0.125
        proj_ref[:, C_SK:C_SV] = _rope(att[:, 512:640], tab, 1.0)
        proj_ref[:, C_SV:C_LR] = att[:, 640:768]
        proj_ref[:, C_LR:DINP] = jnp.zeros((TM, DINP - C_LR), F32)
        proj_ref[:, C_LR:C_LR + 16] = _dot_nt(ub, w_ref[R_LR:R_LR + 16, :])

    (proj,), got = _call(
        body, "proj_fwd", (nblk + 1,),
        [pl.BlockSpec((TM, D), lambda i: (jnp.minimum(i, nblk - 1), 0)), VMEM_SPEC, VMEM_SPEC, VMEM_SPEC,
         pl.BlockSpec((TM, 128), lambda i: (i, 0))],
        [pl.BlockSpec((TM, DINP), lambda i: (i, 0))], [jax.ShapeDtypeStruct((t + TM, DINP), F32)],
        [pltpu.VMEM((DIN, D), BF16)], _cp(48), (x, metapad, wm, wt3, tabs), comm)
    return proj, got


def _chunk_masks():
    r = lax.broadcasted_iota(jnp.int32, (TM, TM), 0)
    c = lax.broadcasted_iota(jnp.int32, (TM, TM), 1)
    same = (r // CH) == (c // CH)
    lower = _bf(jnp.where(same & (c <= r), 1.0, 0.0))
    upper = _bf(jnp.where(same & (c >= r), 1.0, 0.0))
    return lower, upper


def _gla_gate(lr, wgu, bg, valid, lower):
    z = _dot(_bf(lr), wgu) + bg
    g = (jnp.minimum(z, 0.0) - jnp.log(1.0 + jnp.exp(-jnp.abs(z)))) * (1.0 / 16.0)
    g = jnp.where(valid, g, 0.0)
    return z, _dot3(lower, g)


def _gla_decays(q, k, b):
    nc = TM // CH
    b3 = b.reshape(nc, CH, 256)
    blast = b3[:, CH - 1:CH, :]
    eb = jnp.exp(b)
    enb = jnp.exp(-b)
    ebl = jnp.exp(blast - b3).reshape(TM, 256)
    return eb, enb, ebl, jnp.exp(blast)


def _tri(lower_incl):
    r = lax.broadcasted_iota(jnp.int32, (CH, CH), 0)
    c = lax.broadcasted_iota(jnp.int32, (CH, CH), 1)
    return ((c <= r) if lower_incl else (c >= r))[None]


def _gla_fwd(proj, wgu, bg, gnw, t, comm=None):
    nblk = t // TM
    nt = nblk + 1
    nc = TM // CH

    def blk(i):
        return (i + nblk) % nt

    def body(q_ref, k_ref, v_ref, r_ref, lr_ref, wgu_ref, bg_ref, gnw_ref, o_ref, oraw_ref, sst_ref, b_ref, dgate_ref,
             st_scr):
        i = pl.program_id(0)

        @pl.when(i == 0)
        def _():
            st_scr[...] = jnp.zeros_like(st_scr)

        rows = blk(i) * TM + lax.broadcasted_iota(jnp.int32, (TM, 1), 0)
        lower, _ = _chunk_masks()
        valid = rows < t + NM
        z, b = _gla_gate(lr_ref[...], wgu_ref[...], bg_ref[...], valid, lower)
        b_ref[...] = b
        dgate_ref[...] = jnp.where(valid, (1.0 / 16.0) / (1.0 + jnp.exp(z)), 0.0)
        q = q_ref[...]
        k = k_ref[...]
        eb, enb, ebl, eblast = _gla_decays(q, k, b)
        qt = q * 0.125 * eb
        kt = k * enb
        kh = k * ebl
        tril = _tri(True)
        heads = range(4)
        hs = [slice(h * DK, (h + 1) * DK) for h in heads]
        qh = [_bf(qt[:, hs[h]]).reshape(nc, CH, DK) for h in heads]
        kth = [_bf(kt[:, hs[h]]).reshape(nc, CH, DK) for h in heads]
        khh = [_bf(kh[:, hs[h]]).reshape(nc, CH, DK) for h in heads]
        vh = [_bf(v_ref[:, h * 128:(h + 1) * 128]).reshape(nc, CH, 128) for h in heads]
        a = [jnp.einsum('cid,cjd->cij', qh[h], kth[h], preferred_element_type=F32) for h in heads]
        kv = [jnp.einsum('cjv,cjd->cvd', vh[h], khh[h], preferred_element_type=F32) for h in heads]
        o = [jnp.einsum('cij,cjv->civ', _bf(jnp.where(tril, a[h], 0.0)), vh[h], preferred_element_type=F32) for h in heads]
        states = []
        for h in heads:
            st = st_scr[h]
            per_chunk = []
            for c in range(nc):
                sst_ref[c, h] = st
                per_chunk.append(_bf(st))
                st = st * eblast[c, :, hs[h]] + kv[h][c]
            st_scr[h] = st
            states.append(per_chunk)
        o_inter = [[_dot_nt(qh[h][c], states[h][c]) for c in range(nc)] for h in heads]
        oraw = jnp.concatenate([(o[h] + jnp.stack(o_inter[h])).reshape(TM, 128) for h in heads], axis=1)
        oraw_ref[...] = oraw
        gn = gnw_ref[...]
        res = []
        for h in range(4):
            on, _ = _rms(oraw[:, h * 128:(h + 1) * 128])
            r = r_ref[:, h * 128:(h + 1) * 128]
            res.append(on * gn * (r * jax.nn.sigmoid(r)))
        o_ref[...] = _bf(jnp.concatenate(res, axis=1))

    def spec(w, cb):
        return pl.BlockSpec((TM, w), lambda i: (blk(i), cb))

    return _call(
        body, "gla_fwd", (nt,),
        [spec(256, 0), spec(256, 1), spec(512, 1), spec(512, 2), spec(128, C_LR // 128), VMEM_SPEC, VMEM_SPEC, VMEM_SPEC],
        [spec(512, 0), spec(512, 0), pl.BlockSpec((nc, 4, 128, DK), lambda i: (blk(i), 0, 0, 0)), spec(256, 0), spec(256, 0)],
        [jax.ShapeDtypeStruct((t + TM, 512), BF16), jax.ShapeDtypeStruct((t + TM, 512), F32),
         jax.ShapeDtypeStruct((nt * nc, 4, 128, DK), F32), jax.ShapeDtypeStruct((t + TM, 256), F32),
         jax.ShapeDtypeStruct((t + TM, 256), F32)],
        [pltpu.VMEM((4, 128, DK), F32)], _cp(40), (proj, proj, proj, proj, proj, wgu, bg, gnw), comm)


def _gla_bwd(proj, oraw, sst, bcum, dgate, do, wgu, gnw, t, comm=None):
    nblk = t // TM
    nt = nblk + 1
    nc = TM // CH

    def blk(i):
        return (2 * nblk - i) % nt

    def body(q_ref, k_ref, v_ref, r_ref, lr_ref, oraw_ref, sst_ref, b_ref, dgate_ref, do_ref, wgu_ref, gnw_ref,
             dgla_ref, dlr_ref, dwgu_ref, dbg_ref, dgnw_ref, dst_scr):
        i = pl.program_id(0)

        @pl.when(i == 0)
        def _():
            dst_scr[...] = jnp.zeros_like(dst_scr)
            dwgu_ref[...] = jnp.zeros_like(dwgu_ref)
            dbg_ref[...] = jnp.zeros_like(dbg_ref)
            dgnw_ref[...] = jnp.zeros_like(dgnw_ref)

        _, upper = _chunk_masks()
        lr = lr_ref[...]
        b = b_ref[...]
        q = q_ref[...]
        k = k_ref[...]
        eb, enb, ebl, eblast = _gla_decays(q, k, b)
        qt = q * 0.125 * eb
        kt = k * enb
        kh = k * ebl
        gn = gnw_ref[...]
        tril = _tri(True)
        triu = _tri(False)
        heads = range(4)
        hs = [slice(h * DK, (h + 1) * DK) for h in heads]
        vs = [slice(h * 128, (h + 1) * 128) for h in heads]
        ein = functools.partial(jnp.einsum, preferred_element_type=F32)
        dr_l, doh = [], []
        dgn = jnp.zeros((1, 128), F32)
        for h in heads:
            on, rs = _rms(oraw_ref[:, vs[h]])
            r = r_ref[:, vs[h]]
            sig = jax.nn.sigmoid(r)
            sil = r * sig
            dy = do_ref[:, vs[h]]
            dr_l.append(dy * on * gn * (sig * (1.0 + r * (1.0 - sig))))
            dgn = dgn + jnp.sum(dy * sil * on, axis=0, keepdims=True)
            doh.append(_bf(_rms_bwd(dy * sil, on, rs, gn)).reshape(nc, CH, 128))
        dgnw_ref[...] += dgn
        qh = [_bf(qt[:, hs[h]]).reshape(nc, CH, DK) for h in heads]
        kth = [_bf(kt[:, hs[h]]).reshape(nc, CH, DK) for h in heads]
        khh = [_bf(kh[:, hs[h]]).reshape(nc, CH, DK) for h in heads]
        vh = [_bf(v_ref[:, vs[h]]).reshape(nc, CH, 128) for h in heads]
        at = [ein('cjd,cid->cji', kth[h], qh[h]) for h in heads]
        da = [ein('civ,cjv->cij', doh[h], vh[h]) for h in heads]
        dat = [ein('cjv,civ->cji', vh[h], doh[h]) for h in heads]
        gq = [ein('civ,cid->cvd', doh[h], qh[h]) for h in heads]
        stf = [sst_ref[:, h] for h in heads]
        dqs = [ein('civ,cvd->cid', doh[h], _bf(stf[h])) for h in heads]
        dv = [ein('cji,civ->cjv', _bf(jnp.where(triu, at[h], 0.0)), doh[h]) for h in heads]
        dqt = [ein('cij,cjd->cid', _bf(jnp.where(tril, da[h], 0.0)), kth[h]) + dqs[h] for h in heads]
        dkt = [ein('cji,cid->cjd', _bf(jnp.where(triu, dat[h], 0.0)), qh[h]) for h in heads]
        dse = []
        for h in heads:
            dst = dst_scr[h]
            dsend = [None] * nc
            for c in reversed(range(nc)):
                dsend[c] = dst
                dst = dst * eblast[c, :, hs[h]] + gq[h][c]
            dst_scr[h] = dst
            dse.append(jnp.stack(dsend))
        dseb = [_bf(d) for d in dse]
        dv = [dv[h] + ein('cjd,cvd->cjv', khh[h], dseb[h]) for h in heads]
        dkh = [ein('cjv,cvd->cjd', vh[h], dseb[h]) for h in heads]
        carried = jnp.concatenate([jnp.sum(dse[h] * stf[h], axis=1, keepdims=True) for h in heads], axis=2)
        wide = lambda parts: jnp.concatenate([p.reshape(TM, DK) for p in parts], axis=1)
        dqt_w, dkt_w, dkh_w = wide(dqt), wide(dkt), wide(dkh)
        dkh_kh = dkh_w * kh
        extra = jnp.sum(dkh_kh.reshape(nc, CH, 256), axis=1, keepdims=True) + eblast * carried
        db = dqt_w * qt - dkt_w * kt - dkh_kh
        dg = _dot3(upper, db) + jnp.broadcast_to(extra, (nc, CH, 256)).reshape(TM, 256)
        dz = dg * dgate_ref[...]
        dzb = _bf(dz)
        dlr_ref[...] = _bf(_dot_nt(dzb, wgu_ref[...]))
        dwgu_ref[...] += _dot_tn(_bf(lr), dzb)
        dbg_ref[...] += jnp.sum(dz, axis=0, keepdims=True)
        dq = dqt_w * eb * 0.125
        dk = dkt_w * enb + dkh_w * ebl
        dgla_ref[...] = _bf(jnp.concatenate([dq, dk] + [d.reshape(TM, 128) for d in dv] + dr_l, axis=1))

    def spec(w, cb):
        return pl.BlockSpec((TM, w), lambda i: (blk(i), cb))

    def acc(shape):
        return pl.BlockSpec(shape, lambda i: (0, 0))

    return _call(
        body, "gla_bwd", (nt,),
        [spec(256, 0), spec(256, 1), spec(512, 1), spec(512, 2), spec(128, C_LR // 128), spec(512, 0),
         pl.BlockSpec((nc, 4, 128, DK), lambda i: (blk(i), 0, 0, 0)), spec(256, 0), spec(256, 0), spec(512, 0),
         VMEM_SPEC, VMEM_SPEC],
        [spec(1536, 0), spec(128, 0), acc((128, 256)), acc((1, 256)), acc((1, 128))],
        [jax.ShapeDtypeStruct((t + TM, 1536), BF16), jax.ShapeDtypeStruct((t + TM, 128), BF16),
         jax.ShapeDtypeStruct((128, 256), F32), jax.ShapeDtypeStruct((1, 256), F32), jax.ShapeDtypeStruct((1, 128), F32)],
        [pltpu.VMEM((4, 128, DK), F32)], _cp(48), (proj, proj, proj, proj, proj, oraw, sst, bcum, dgate, do, wgu, gnw), comm)


def _rope_tables(t):
    r = t + TM
    row = np.arange(r)
    pos = np.where(row < t, row + NM, np.where(row < t + NM, row - t, 0)).astype(np.float32)
    inv_freq = (1.0 / (np.float32(ROPE_THETA) ** (np.arange(0, 16, 2, dtype=np.float32) / np.float32(16)))).astype(np.float32)
    ang = (pos[:, None] * inv_freq[None, :]).astype(np.float32)
    cos, sin = np.cos(ang).astype(np.float32), np.sin(ang).astype(np.float32)
    one, zero = np.ones((r, 48), np.float32), np.zeros((r, 48), np.float32)
    return jnp.asarray(np.concatenate([cos, cos, one, -sin, sin, zero], axis=1))


def _rope(x, tab, sign):
    w = x.shape[1]
    rep = w // 64
    c = jnp.concatenate([tab[:, 0:64]] * rep, axis=1)
    s = jnp.concatenate([tab[:, 64:128]] * rep, axis=1)
    lane = lax.rem(lax.broadcasted_iota(jnp.int32, x.shape, 1), 64)
    partner = jnp.where(lane < 8, pltpu.roll(x, w - 8, 1), jnp.where(lane < 16, pltpu.roll(x, 8, 1), 0.0))
    return x * c + sign * (partner * s)


HB_BWD = 4


def _stack(x, hg):
    w = x.shape[1] // hg
    return x if hg == 1 else jnp.concatenate([x[:, g * w:(g + 1) * w] for g in range(hg)], axis=0)


def _unstack(x, hg):
    return x if hg == 1 else jnp.concatenate([x[g * SB:(g + 1) * SB] for g in range(hg)], axis=1)


def _swa_specs(nsb):
    def rows(h, w, cb, f):
        return pl.BlockSpec((h, w), lambda i: (f(i), cb))
    pair = lambda i: i
    prev = lambda i: jnp.maximum(2 * i - 1, 0)
    meta = lambda i: nsb
    return rows, pair, prev, meta


def _swa_fwd(proj, sinks, t, comm=None):
    nsb = t // SB
    r_tot = t + TM
    rows, pair, prev, meta = _swa_specs(nsb)

    def body(sink_ref, q_ref, kc_ref, kp_ref, km_ref, vc_ref, vp_ref, vm_ref, o_ref, lse_ref):
        i = pl.program_id(0)
        key = lax.broadcasted_iota(jnp.int32, (SB, SB), 0)
        qry = lax.broadcasted_iota(jnp.int32, (SB, SB), 1)
        km, vm = km_ref[0:NM, :], vm_ref[0:NM, :]
        for j in range(2):
            b = 2 * i + j
            rs = slice(j * SB, (j + 1) * SB)
            real = b < nsb
            masks = (key <= qry, (key > qry) & (b > 0) & real, real)
            k3 = (kc_ref[rs, :], kp_ref[...] if j == 0 else kc_ref[0:SB, :], km)
            v3 = (vc_ref[rs, :], vp_ref[...] if j == 0 else vc_ref[0:SB, :], vm)
            valid = b * SB + lax.broadcasted_iota(jnp.int32, (1, SB), 1) < t + NM
            heads = range(8)
            kb = [[_bf(k[:, kv * 64:(kv + 1) * 64]) for k in k3] for kv in range(2)]
            vt = [[_bf(v[:, kv * 64:(kv + 1) * 64].T) for v in v3] for kv in range(2)]
            raw = [[_dot_nt(k, _bf(q_ref[rs, h * 64:(h + 1) * 64])) for k in kb[h // 4]] for h in heads]
            probs, inv_l, lse_l = [], [], []
            for h in heads:
                s = [jnp.where(m, sx, NEG) for m, sx in zip(masks, raw[h])]
                sink = sink_ref[0, h]
                top = jnp.maximum(jnp.max(jnp.maximum(s[0], s[1]), axis=0, keepdims=True),
                                  jnp.maximum(jnp.max(s[2], axis=0, keepdims=True), sink))
                p = [jnp.exp(sx - top) for sx in s]
                l = (jnp.sum(p[0] + p[1], axis=0, keepdims=True) + jnp.sum(p[2], axis=0, keepdims=True)
                     + jnp.exp(sink - top))
                probs.append([_bf(px) for px in p])
                inv_l.append(1.0 / l)
                lse_l.append(top + jnp.log(l))
            o_t = [_dot(vt[h // 4][0], probs[h][0]) + _dot(vt[h // 4][1], probs[h][1]) + _dot(vt[h // 4][2], probs[h][2])
                   for h in heads]
            o_ref[rs, :] = _bf(jnp.concatenate([jnp.where(valid, o_t[h] * inv_l[h], 0.0).T for h in heads], axis=1))
            lse_ref[:, rs] = jnp.concatenate(lse_l, axis=0)

    ck, cv = C_SK // 128, C_SV // 128
    return _call(
        body, "swa_fwd", (r_tot // QB,),
        [SMEM_SPEC, rows(QB, 512, C_SQ // 512, pair),
         rows(QB, 128, ck, pair), rows(SB, 128, ck, prev), rows(SB, 128, ck, meta),
         rows(QB, 128, cv, pair), rows(SB, 128, cv, prev), rows(SB, 128, cv, meta)],
        [rows(QB, 512, 0, pair), pl.BlockSpec((8, QB), lambda i: (0, i))],
        [jax.ShapeDtypeStruct((r_tot, 512), BF16), jax.ShapeDtypeStruct((8, r_tot), F32)],
        [], _cp(32), (sinks, proj, proj, proj, proj, proj, proj, proj), comm)


def _swa_bwd(proj, sinks, lse_t, do, t, comm=None):
    nsb = t // SB
    r_tot = t + TM
    rows, pair, prev, meta = _swa_specs(nsb)
    hb = HB_BWD
    lanes = hb * SB

    def body(sink_ref, q_ref, kc_ref, kp_ref, km_ref, vc_ref, vp_ref, vm_ref, lse_ref, do_ref,
             dq_ref, dk_ref, dv_ref, dsink_ref):
        i = pl.program_id(0)

        @pl.when(i == 0)
        def _():
            dk_ref[...] = jnp.zeros_like(dk_ref)
            dv_ref[...] = jnp.zeros_like(dv_ref)
            dsink_ref[...] = jnp.zeros_like(dsink_ref)

        key = lax.broadcasted_iota(jnp.int32, (SB, lanes), 0)
        qry = lax.rem(lax.broadcasted_iota(jnp.int32, (SB, lanes), 1), SB)
        km, vm = km_ref[0:NM, :], vm_ref[0:NM, :]
        dsink_l = []
        for j in range(2):
            b = 2 * i + j
            rs = slice(j * SB, (j + 1) * SB)
            real = b < nsb
            masks = (key <= qry, (key > qry) & (b > 0) & real, real)
            k3 = (kc_ref[rs, :], kp_ref[...] if j == 0 else kc_ref[0:SB, :], km)
            v3 = (vc_ref[rs, :], vp_ref[...] if j == 0 else vc_ref[0:SB, :], vm)
            groups = list(range(0, 8, hb))
            kvs = [h0 // 4 for h0 in groups]
            qg = [_bf(_stack(q_ref[rs, h0 * 64:(h0 + hb) * 64], hb)) for h0 in groups]
            dog = [_bf(_stack(do_ref[rs, h0 * 64:(h0 + hb) * 64], hb)) for h0 in groups]
            kb = [[_bf(k[:, kv * 64:(kv + 1) * 64]) for k in k3] for kv in kvs]
            vb = [[_bf(v[:, kv * 64:(kv + 1) * 64]) for v in v3] for kv in kvs]
            s = [[_dot_nt(k, qg[g]) for k in kb[g]] for g in range(len(groups))]
            dp = [[_dot_nt(v, dog[g]) for v in vb[g]] for g in range(len(groups))]
            p, ds, ds_blk = [], [], []
            for g, h0 in enumerate(groups):
                lse_row = jnp.concatenate([lse_ref[h:h + 1, rs] for h in range(h0, h0 + hb)], axis=1)
                sink_row = jnp.concatenate([jnp.full((1, SB), sink_ref[0, h], F32) for h in range(h0, h0 + hb)], axis=1)
                pg = [jnp.exp(jnp.where(m, sx, NEG) - lse_row) for m, sx in zip(masks, s[g])]
                delta = (jnp.sum(pg[0] * dp[g][0] + pg[1] * dp[g][1], axis=0, keepdims=True)
                         + jnp.sum(pg[2] * dp[g][2], axis=0, keepdims=True))
                ds.append([_bf(pp * (dd - delta)) for pp, dd in zip(pg, dp[g])])
                p.append([_bf(pp) for pp in pg])
                ds_row = -jnp.exp(sink_row - lse_row) * delta
                ds_blk += [jnp.sum(ds_row[:, q0 * SB:(q0 + 1) * SB], axis=1, keepdims=True) for q0 in range(hb)]
            dsink_l.append(jnp.concatenate(ds_blk, axis=1))
            dq_t = [_dot_tn(kb[g][0], ds[g][0]) + _dot_tn(kb[g][1], ds[g][1]) + _dot_tn(kb[g][2], ds[g][2])
                    for g in range(len(groups))]
            dq_ref[rs, :] = jnp.concatenate([_unstack(d.T, hb) for d in dq_t], axis=1)
            windows = (pl.ds(pl.multiple_of(b * SB, SB), SB), pl.ds(pl.multiple_of(jnp.maximum(b - 1, 0) * SB, SB), SB),
                       pl.ds(t, NM))
            for x in range(3):
                dk_kv, dv_kv = [], []
                for kv in range(2):
                    mine = [g for g in range(len(groups)) if kvs[g] == kv]
                    dk_kv.append(sum(_dot(ds[g][x], qg[g]) for g in mine))
                    dv_kv.append(sum(_dot(p[g][x], dog[g]) for g in mine))
                dk_ref[windows[x], :] += jnp.concatenate(dk_kv, axis=1)
                dv_ref[windows[x], :] += jnp.concatenate(dv_kv, axis=1)
        dsink_ref[...] += dsink_l[0] + dsink_l[1]

    ck, cv = C_SK // 128, C_SV // 128
    whole = lambda w: pl.BlockSpec((r_tot, w), lambda i: (0, 0))
    return _call(
        body, "swa_bwd", (r_tot // QB,),
        [SMEM_SPEC, rows(QB, 512, C_SQ // 512, pair),
         rows(QB, 128, ck, pair), rows(SB, 128, ck, prev), rows(SB, 128, ck, meta),
         rows(QB, 128, cv, pair), rows(SB, 128, cv, prev), rows(SB, 128, cv, meta),
         pl.BlockSpec((8, QB), lambda i: (0, i)), rows(QB, 512, 1, pair)],
        [rows(QB, 512, 0, pair), whole(128), whole(128), pl.BlockSpec((1, 8), lambda i: (0, 0))],
        [jax.ShapeDtypeStruct((r_tot, 512), F32), jax.ShapeDtypeStruct((r_tot, 128), F32),
         jax.ShapeDtypeStruct((r_tot, 128), F32), jax.ShapeDtypeStruct((1, 8), F32)],
        [], _cp(48), (sinks, proj, proj, proj, proj, proj, proj, proj, lse_t, do), comm)


HK = D // 2


def _mlp_fwd(x, metapad, tgt, ogla, oswa, wo, wff, w1, w2, wfin):
    t = x.shape[0]
    nblk = t // TM

    def body(x_ref, mp_ref, tgt_ref, og_ref, os_ref, wo_ref, wff_ref, w1a_ref, w1b_ref, w2a_ref, w2b_ref, wfin_ref,
             h1_ref, f_ref, a_ref, dh2_ref, loss_ref, gfin_ref):
        i = pl.program_id(0)

        @pl.when(i == 0)
        def _():
            loss_ref[...] = jnp.zeros_like(loss_ref)
            gfin_ref[...] = jnp.zeros_like(gfin_ref)

        h0 = jnp.where(i == nblk, mp_ref[...], x_ref[...])
        h1 = h0 + _dot(og_ref[...], wo_ref[0:512, :]) + _dot(os_ref[...], wo_ref[512:1024, :])
        h1_ref[...] = h1
        fh, _ = _rms(h1)
        f = _bf(fh * wff_ref[...])
        f_ref[...] = f
        acc = jnp.zeros((TM, D), F32)
        for n in range(4):
            a = _dot(f[:, 0:HK], w1a_ref[n]) + _dot(f[:, HK:D], w1b_ref[n])
            a_ref[:, n * D:(n + 1) * D] = _bf(a)
            zr = jnp.maximum(a, 0.0)
            z = _bf(zr * zr)
            acc = acc + _dot(z[:, 0:HK], w2a_ref[n]) + _dot(z[:, HK:D], w2b_ref[n])
        h2 = h1 + acc
        yh, rs2 = _rms(h2)
        wf = wfin_ref[...]
        real = i < nblk
        e = jnp.where(real, yh * wf - tgt_ref[...], 0.0)
        loss_ref[...] += jnp.sum(jnp.sum(e * e, axis=0, keepdims=True), axis=1, keepdims=True) * (0.5 / D)
        dy = e * (1.0 / D)
        gfin_ref[...] += jnp.sum(dy * yh, axis=0, keepdims=True)
        dh2_ref[...] = _rms_bwd(dy, yh, rs2, wf)

    xs = pl.BlockSpec((TM, D), lambda i: (jnp.minimum(i, nblk - 1), 0))
    rs = lambda w: pl.BlockSpec((TM, w), lambda i: (i, 0))
    r_tot = t + TM
    return pl.pallas_call(
        body, name="mlp_fwd", grid=(nblk + 1,),
        in_specs=[xs, VMEM_SPEC, xs, rs(512), rs(512)] + [VMEM_SPEC] * 7,
        out_specs=[rs(D), rs(D), rs(DFF), rs(D), pl.BlockSpec((1, 1), lambda i: (0, 0)), pl.BlockSpec((1, D), lambda i: (0, 0))],
        out_shape=[jax.ShapeDtypeStruct((r_tot, D), F32), jax.ShapeDtypeStruct((r_tot, D), BF16),
                   jax.ShapeDtypeStruct((r_tot, DFF), BF16), jax.ShapeDtypeStruct((r_tot, D), F32),
                   jax.ShapeDtypeStruct((1, 1), F32), jax.ShapeDtypeStruct((1, D), F32)],
        compiler_params=_cp(56),
    )(x, metapad, tgt, ogla, oswa, wo, wff, *w1, *w2, wfin)


def _mlp_bwd(h1, a, dh2, ogla, oswa, wo, wff, w1, w2):
    r_tot = h1.shape[0]
    nt = r_tot // TM

    def body(h1_ref, a_ref, dh2_ref, og_ref, os_ref, wo_ref, wff_ref, w1a_ref, w1b_ref, w2a_ref, w2b_ref,
             da_ref, dh2b_ref, dh1_ref, do_ref, dwo_ref, gff_ref, dwo_acc):
        i = pl.program_id(0)

        @pl.when(i == 0)
        def _():
            dwo_acc[...] = jnp.zeros_like(dwo_acc)
            gff_ref[...] = jnp.zeros_like(gff_ref)

        dh2 = dh2_ref[...]
        dh2b = _bf(dh2)
        dh2b_ref[...] = dh2b
        dfa = jnp.zeros((TM, HK), F32)
        dfb = jnp.zeros((TM, HK), F32)
        for n in range(4):
            dz = jnp.concatenate([_dot_nt(dh2b, w2a_ref[n]), _dot_nt(dh2b, w2b_ref[n])], axis=1)
            da = _bf(dz * (2.0 * jnp.maximum(a_ref[:, n * D:(n + 1) * D].astype(F32), 0.0)))
            da_ref[:, n * D:(n + 1) * D] = da
            dfa = dfa + _dot_nt(da, w1a_ref[n])
            dfb = dfb + _dot_nt(da, w1b_ref[n])
        df = jnp.concatenate([dfa, dfb], axis=1)
        fh, rs1 = _rms(h1_ref[...])
        gff_ref[...] += jnp.sum(df * fh, axis=0, keepdims=True)
        dh1 = dh2 + _rms_bwd(df, fh, rs1, wff_ref[...])
        dh1_ref[...] = dh1
        dh1b = _bf(dh1)
        do_ref[...] = _dot_nt(dh1b, wo_ref[...])
        dwo_acc[0:512, :] += _dot_tn(og_ref[...], dh1b)
        dwo_acc[512:1024, :] += _dot_tn(os_ref[...], dh1b)

        @pl.when(i == nt - 1)
        def _():
            for s in range(4):
                for hh in range(2):
                    dwo_ref[hh, s] = dwo_acc[(2 * s + hh) * 128:(2 * s + hh + 1) * 128, :]

    rs = lambda w: pl.BlockSpec((TM, w), lambda i: (i, 0))
    return pl.pallas_call(
        body, name="mlp_bwd", grid=(nt,),
        in_specs=[rs(D), rs(DFF), rs(D), rs(512), rs(512)] + [VMEM_SPEC] * 6,
        out_specs=[rs(DFF), rs(D), rs(D), rs(D), VMEM_SPEC, pl.BlockSpec((1, D), lambda i: (0, 0))],
        out_shape=[jax.ShapeDtypeStruct((r_tot, DFF), BF16), jax.ShapeDtypeStruct((r_tot, D), BF16),
                   jax.ShapeDtypeStruct((r_tot, D), F32), jax.ShapeDtypeStruct((r_tot, D), F32),
                   jax.ShapeDtypeStruct((2, 4, 128, D), F32), jax.ShapeDtypeStruct((1, D), F32)],
        scratch_shapes=[pltpu.VMEM((D, D), F32)],
        compiler_params=_cp(56),
    )(h1, a, dh2, ogla, oswa, wo, wff, *w1, *w2)


def _ffn_wgrad(f, a, da, dh2b):
    r_tot = f.shape[0]
    kt = 768 if r_tot % 768 == 0 else TM
    nk = r_tot // kt

    def body(f_ref, a_ref, da_ref, dh2_ref, dw1_ref, dw2_ref, acc1, acc2):
        k = pl.program_id(1)

        @pl.when(k == 0)
        def _():
            acc1[...] = jnp.zeros_like(acc1)
            acc2[...] = jnp.zeros_like(acc2)

        zr = jnp.maximum(a_ref[...], 0.0)
        acc1[...] += _dot_tn(f_ref[...], da_ref[...])
        acc2[...] += _dot_tn(zr * zr, dh2_ref[...])

        @pl.when(k == nk - 1)
        def _():
            for hh in range(2):
                dw1_ref[hh, 0] = acc1[hh * 512:(hh + 1) * 512, :]
                dw2_ref[hh, 0] = acc2[hh * 512:(hh + 1) * 512, :]

    out = pl.BlockSpec((2, 1, 512, D), lambda n, k: (0, n, 0, 0))
    return pl.pallas_call(
        body, name="ffn_wgrad", grid=(4, nk),
        in_specs=[pl.BlockSpec((kt, D), lambda n, k: (k, 0)), pl.BlockSpec((kt, D), lambda n, k: (k, n)),
                  pl.BlockSpec((kt, D), lambda n, k: (k, n)), pl.BlockSpec((kt, D), lambda n, k: (k, 0))],
        out_specs=[out, out],
        out_shape=[jax.ShapeDtypeStruct((2, 4, 512, D), F32)] * 2,
        scratch_shapes=[pltpu.VMEM((D, D), F32), pltpu.VMEM((D, D), F32)],
        compiler_params=_cp(48, ("arbitrary", "arbitrary")),
    )(f, a, da, dh2b)


def _proj_bwd(x, metapad, wm, wt3, tabs, dgla, dswa_q, dsk, dsv, dlr, dh1, comm=None):
    t = x.shape[0]
    nblk = t // TM

    def body(x_ref, mp_ref, wm_ref, w3_ref, tab_ref, dg_ref, dq_ref, dk_ref, dv_ref, dlr_ref, dh1_ref,
             gx_ref, gmeta_ref, dw_ref, gmix_ref, w_ref, acc):
        i = pl.program_id(0)

        @pl.when(i == 0)
        def _():
            _join_shards(w3_ref, w_ref)
            acc[...] = jnp.zeros_like(acc)
            gmix_ref[...] = jnp.zeros_like(gmix_ref)

        h = jnp.where(i == nblk, mp_ref[...], x_ref[...])
        uh, rs = _rms(h)
        wm_v = wm_ref[...]
        u = _bf(uh * wm_v)
        tab = tab_ref[...]
        dq = _bf(_rope(dq_ref[...] * 0.125, tab, -1.0))
        dk = _bf(_rope(dk_ref[...], tab, -1.0))
        parts = ((dg_ref[...], 0, R_LR), (dlr_ref[:, 0:16], R_LR, 16), (dq, R_LR + 16, 512),
                 (dk, R_LR + 528, 128), (_bf(dv_ref[...]), R_LR + 656, 128))
        du = jnp.zeros((TM, D), F32)
        for val, r0, w in parts:
            du = du + _dot(val, w_ref[r0:r0 + w, :])
            acc[r0:r0 + w, :] += _dot_tn(val, u)
        gmix_ref[...] += jnp.sum(du * uh, axis=0, keepdims=True)
        dh0 = dh1_ref[...] + _rms_bwd(du, uh, rs, wm_v)

        @pl.when(i < nblk)
        def _():
            gx_ref[...] = dh0

        @pl.when(i == nblk)
        def _():
            gmeta_ref[...] = dh0[:NM]
            for s in range(4):
                dw_ref[s] = acc[(DIN // 4) * s:(DIN // 4) * (s + 1), :]

    xs = pl.BlockSpec((TM, D), lambda i: (jnp.minimum(i, nblk - 1), 0))
    rs_ = lambda w: pl.BlockSpec((TM, w), lambda i: (i, 0))
    return _call(
        body, "proj_bwd", (nblk + 1,),
        [xs, VMEM_SPEC, VMEM_SPEC, VMEM_SPEC, rs_(128), rs_(1536), rs_(512), rs_(128), rs_(128), rs_(128), rs_(D)],
        [xs, pl.BlockSpec((NM, D), lambda i: (0, 0)), VMEM_SPEC, pl.BlockSpec((1, D), lambda i: (0, 0))],
        [jax.ShapeDtypeStruct((t, D), F32), jax.ShapeDtypeStruct((NM, D), F32),
         jax.ShapeDtypeStruct((4, DIN // 4, D), F32), jax.ShapeDtypeStruct((1, D), F32)],
        [pltpu.VMEM((DIN, D), BF16), pltpu.VMEM((DIN, D), F32)], _cp(56),
        (x, metapad, wm, wt3, tabs, dgla, dswa_q, dsk, dsv, dlr, dh1), comm)


def _place():
    return lax.axis_index("x"), lax.axis_index("y"), lax.axis_index("c")


def _other_chips(x, y):
    return [(1 - x, y), (x, 1 - y), (1 - x, 1 - y)]


def _dma_sems(*counts):
    return tuple(pltpu.SemaphoreType.DMA((k,)) for k in counts)


def _gather_shards(shards, split):
    n = len(shards)
    two = [a for a in range(n) if split[a]]

    def plan(ins, outs, sems):
        isend, irecv, dsend, drecv, loc = sems
        x, y, c = _place()
        chips = _other_chips(x, y)

        def part(ref, a, half):
            if not split[a]:
                return ref
            w = shards[a].shape[1] // 2
            return ref.at[:, pl.ds(pl.multiple_of(half * w, 128), w)]

        def over_ici(a, k, shard_of):
            tx, ty = chips[k]
            sx, sy = shard_of
            return pltpu.make_async_remote_copy(
                src_ref=part(ins[a], a, c), dst_ref=part(outs[a].at[2 * sx + sy], a, c), send_sem=isend.at[3 * a + k],
                recv_sem=irecv.at[3 * a + k], device_id=(tx, ty, c), device_id_type=MESH)

        def over_d2d(a, k, half):
            tx, ty = chips[k]
            ref = part(outs[a].at[2 * tx + ty], a, half)
            return pltpu.make_async_remote_copy(
                src_ref=ref, dst_ref=ref, send_sem=dsend.at[3 * a + k], recv_sem=drecv.at[3 * a + k],
                device_id=(x, y, 1 - c), device_id_type=MESH)

        def local(a):
            return pltpu.make_async_copy(ins[a], outs[a].at[2 * x + y], loc.at[a])

        pairs = [(a, k) for a in range(n) for k in range(3)]
        first = ([lambda a=a: local(a).start() for a in range(n)]
                 + [lambda a=a, k=k: over_ici(a, k, (x, y)).start() for a, k in pairs],
                 [lambda a=a, k=k: over_ici(a, k, chips[k]).wait_recv() for a, k in pairs]
                 + [lambda a=a, k=k: over_ici(a, k, (x, y)).wait_send() for a, k in pairs]
                 + [lambda a=a: local(a).wait() for a in range(n)])
        pairs2 = [(a, k) for a in two for k in range(3)]
        second = ([lambda a=a, k=k: over_d2d(a, k, c).start() for a, k in pairs2],
                  [lambda a=a, k=k: over_d2d(a, k, 1 - c).wait_recv() for a, k in pairs2]
                  + [lambda a=a, k=k: over_d2d(a, k, c).wait_send() for a, k in pairs2])
        return [first, second] if two else [first]

    return _Comm(tuple(shards), tuple(jax.ShapeDtypeStruct((4,) + s.shape, s.dtype) for s in shards),
                 _dma_sems(3 * n, 3 * n, 3 * n, 3 * n, n), 2 if two else 1, plan)


def _swap_halves(grads):
    n = len(grads)

    def plan(ins, outs, sems):
        send, recv = sems
        x, y, c = _place()

        def swap(a):
            return pltpu.make_async_remote_copy(
                src_ref=ins[a].at[1 - c], dst_ref=outs[a], send_sem=send.at[a], recv_sem=recv.at[a],
                device_id=(x, y, 1 - c), device_id_type=MESH)

        return [([lambda a=a: swap(a).start() for a in range(n)], [lambda a=a: swap(a).wait() for a in range(n)])]

    return _Comm(tuple(grads), tuple(jax.ShapeDtypeStruct(g.shape[1:], g.dtype) for g in grads), _dma_sems(n, n), 1, plan)


def _scatter_shards(parts):
    n = len(parts)

    def plan(ins, outs, sems):
        send, recv = sems
        x, y, c = _place()
        chips = _other_chips(x, y)

        def scatter(a, k):
            tx, ty = chips[k]
            return pltpu.make_async_remote_copy(
                src_ref=ins[a].at[2 * tx + ty], dst_ref=outs[a].at[k], send_sem=send.at[3 * a + k],
                recv_sem=recv.at[3 * a + k], device_id=(tx, ty, c), device_id_type=MESH)

        pairs = [(a, k) for a in range(n) for k in range(3)]
        return [([lambda a=a, k=k: scatter(a, k).start() for a, k in pairs],
                 [lambda a=a, k=k: scatter(a, k).wait() for a, k in pairs])]

    return _Comm(tuple(parts), tuple(jax.ShapeDtypeStruct((3,) + p.shape[1:], p.dtype) for p in parts),
                 _dma_sems(3 * n, 3 * n), 1, plan)


def _join_halves(halves):
    n = len(halves)

    def plan(ins, outs, sems):
        send, recv, loc = sems
        x, y, c = _place()

        def remote(a, half):
            return pltpu.make_async_remote_copy(
                src_ref=ins[a], dst_ref=outs[a].at[half], send_sem=send.at[a], recv_sem=recv.at[a],
                device_id=(x, y, 1 - c), device_id_type=MESH)

        def local(a):
            return pltpu.make_async_copy(ins[a], outs[a].at[c], loc.at[a])

        every = range(n)
        return [([lambda a=a: local(a).start() for a in every] + [lambda a=a: remote(a, c).start() for a in every],
                 [lambda a=a: remote(a, 1 - c).wait_recv() for a in every]
                 + [lambda a=a: remote(a, c).wait_send() for a in every] + [lambda a=a: local(a).wait() for a in every])]

    return _Comm(tuple(halves), tuple(jax.ShapeDtypeStruct((2,) + h.shape, h.dtype) for h in halves),
                 _dma_sems(n, n, n), 1, plan)


def _reduce_w_in(dwt, comm):
    rows, hw = DIN // 4, D // 2
    ci, co = len(comm.ins), len(comm.outs)

    def body(*refs):
        dw_ref, c_in, out_ref, c_out = refs[0], refs[1:1 + ci], refs[1 + ci], refs[2 + ci:2 + ci + co]
        mine, sib, tosend, rbuf, qbuf, full, send, recv, loc = refs[2 + ci + co:11 + ci + co]
        c_sem = refs[11 + ci + co:]
        x, y, c = _place()
        sibling = (x, y, 1 - c)
        (starts, waits), = comm.plan(c_in, c_out, c_sem)
        _run_phase(starts)

        def cols(ref, half):
            window = pl.ds(pl.multiple_of(half * hw, 128), hw)
            return ref.at[:, :, window] if len(ref.shape) == 3 else ref.at[:, window]

        load = pltpu.make_async_copy(cols(dw_ref, c), mine, loc.at[0])
        give = pltpu.make_async_remote_copy(src_ref=cols(dw_ref, 1 - c), dst_ref=sib, send_sem=send.at[3], recv_sem=recv.at[3],
                                            device_id=sibling, device_id_type=MESH)
        load.start()
        give.start()
        load.wait()
        give.wait()
        mine[...] = mine[...] + sib[...]
        cps = []
        for k, (tx, ty) in enumerate(_other_chips(x, y)):
            tosend[k] = _bf(mine[2 * tx + ty])
            cps.append(pltpu.make_async_remote_copy(
                src_ref=tosend.at[k], dst_ref=rbuf.at[k], send_sem=send.at[k], recv_sem=recv.at[k],
                device_id=(tx, ty, c), device_id_type=MESH))
            cps[-1].start()
        for cp in cps:
            cp.wait()
        qbuf[...] = mine[2 * x + y] + rbuf[0].astype(F32) + rbuf[1].astype(F32) + rbuf[2].astype(F32)
        keep = pltpu.make_async_copy(qbuf, cols(full, c), loc.at[1])
        pass_on = pltpu.make_async_remote_copy(src_ref=qbuf, dst_ref=cols(full, c), send_sem=send.at[4], recv_sem=recv.at[4],
                                               device_id=sibling, device_id_type=MESH)
        keep.start()
        pass_on.start()
        keep.wait()
        pass_on.wait_send()
        pltpu.make_async_remote_copy(src_ref=qbuf, dst_ref=cols(full, 1 - c), send_sem=send.at[4], recv_sem=recv.at[4],
                                     device_id=sibling, device_id_type=MESH).wait_recv()
        out_ref[...] = full[...]
        _run_phase(waits)

    outs = pl.pallas_call(
        body, name="reduce_w_in",
        in_specs=[ANY_SPEC] * (1 + ci), out_specs=[VMEM_SPEC] + [ANY_SPEC] * co,
        out_shape=[jax.ShapeDtypeStruct((rows, D), F32)] + list(comm.outs),
        scratch_shapes=[pltpu.VMEM((4, rows, hw), F32), pltpu.VMEM((4, rows, hw), F32), pltpu.VMEM((3, rows, hw), BF16),
                        pltpu.VMEM((3, rows, hw), BF16), pltpu.VMEM((rows, hw), F32), pltpu.VMEM((rows, D), F32),
                        *_dma_sems(5, 5, 2), *comm.sems],
        compiler_params=pltpu.CompilerParams(vmem_limit_bytes=48 << 20),
    )(dwt, *comm.ins)
    return outs[0], outs[1:]


def _allreduce_small(pack):
    p = pack.shape[0]

    def body(in_ref, out_ref, buf, send, recv):
        x, y, c = _place()
        me = 4 * x + 2 * y + c
        buf[me] = in_ref[...]

        def peer_of(k):
            return x ^ (k >> 2), y ^ ((k >> 1) & 1), c ^ (k & 1)

        sends = [pltpu.make_async_remote_copy(
            src_ref=in_ref, dst_ref=buf.at[me], send_sem=send.at[k - 1], recv_sem=recv.at[k - 1],
            device_id=peer_of(k), device_id_type=MESH) for k in range(1, 8)]
        for cp in sends:
            cp.start()
        for k in range(1, 8):
            px, py, pc = peer_of(k)
            pltpu.make_async_remote_copy(
                src_ref=in_ref, dst_ref=buf.at[4 * px + 2 * py + pc], send_sem=send.at[k - 1], recv_sem=recv.at[k - 1],
                device_id=(x, y, c), device_id_type=MESH).wait_recv()
        for cp in sends:
            cp.wait_send()
        acc = buf[0]
        for d in range(1, 8):
            acc = acc + buf[d]
        out_ref[...] = acc

    return pl.pallas_call(
        body, name="allreduce_small",
        in_specs=[VMEM_SPEC], out_specs=VMEM_SPEC, out_shape=jax.ShapeDtypeStruct(pack.shape, F32),
        scratch_shapes=[pltpu.VMEM((8, p, D), F32), *_dma_sems(7, 7)],
    )(pack)


GRID4 = 4


def _sum_cores(core_shard, mine, theirs):
    n = len(mine)

    def body(cs_ref, *refs):
        ms, ts, bfs, owns = refs[:n], refs[n:2 * n], refs[2 * n:3 * n], refs[3 * n:]
        keep = pl.program_id(0) == cs_ref[1]
        for a in range(n):
            acc = ms[a][0, 0] + ts[a][0]
            bfs[a][0] = _bf(acc)

            @pl.when(keep)
            def _():
                owns[a][...] = acc

    shapes = [m.shape[2:] for m in mine]
    in_specs = ([pl.BlockSpec((1, 1) + s, lambda i, cs: (cs[0], i, 0, 0)) for s in shapes]
                + [pl.BlockSpec((1,) + s, lambda i, cs: (i, 0, 0)) for s in shapes])
    out_specs = ([pl.BlockSpec((1,) + s, lambda i, cs: (i, 0, 0)) for s in shapes]
                 + [pl.BlockSpec(s, lambda i, cs: (0, 0)) for s in shapes])
    outs = pl.pallas_call(
        body, name="sum_cores",
        grid_spec=pltpu.PrefetchScalarGridSpec(num_scalar_prefetch=1, grid=(4,), in_specs=in_specs, out_specs=out_specs),
        out_shape=[jax.ShapeDtypeStruct((4,) + s, BF16) for s in shapes] + [jax.ShapeDtypeStruct(s, F32) for s in shapes],
        compiler_params=_cp(48),
    )(core_shard, *mine, *theirs)
    return outs[:n], outs[n:]


def _sum_chips(own, arrived):
    n = len(own)

    def body(*refs):
        os_, ars, outs = refs[:n], refs[n:2 * n], refs[2 * n:]
        for a in range(n):
            outs[a][...] = os_[a][...] + ars[a][0].astype(F32) + ars[a][1].astype(F32) + ars[a][2].astype(F32)

    blocks = [(o.shape[0] // GRID4, o.shape[1]) for o in own]
    return pl.pallas_call(
        body, name="sum_chips", grid=(GRID4,),
        in_specs=([pl.BlockSpec(b, lambda i: (i, 0)) for b in blocks]
                  + [pl.BlockSpec((3,) + b, lambda i: (0, i, 0)) for b in blocks]),
        out_specs=[pl.BlockSpec(b, lambda i: (i, 0)) for b in blocks],
        out_shape=[jax.ShapeDtypeStruct(o.shape, F32) for o in own],
        compiler_params=_cp(32),
    )(*own, *arrived)


def _adamw_math(w, g, m, v):
    m2 = ADAM_B1 * m + (1.0 - ADAM_B1) * g
    v2 = ADAM_B2 * v + (1.0 - ADAM_B2) * (g * g)
    m_hat = m2 / (1.0 - ADAM_B1 ** ADAM_STEP)
    v_hat = v2 / (1.0 - ADAM_B2 ** ADAM_STEP)
    return -ADAM_LR * (m_hat / (jnp.sqrt(v_hat) + ADAM_EPS) + ADAM_WD * w), m2, v2


def _adamw_big(ws, gs, ms, vs):
    n = len(ws)

    def body(*refs):
        for a in range(n):
            d, m2, v2 = _adamw_math(refs[a][...], refs[n + a][...], refs[2 * n + a][...], refs[3 * n + a][...])
            refs[4 * n + a][...] = d
            refs[5 * n + a][...] = m2
            refs[6 * n + a][...] = v2

    specs = [pl.BlockSpec((w.shape[0] // GRID4, w.shape[1]), lambda i: (i, 0)) for w in ws]
    return pl.pallas_call(
        body, name="adamw_big", grid=(GRID4,),
        in_specs=specs * 4, out_specs=specs * 3,
        out_shape=[jax.ShapeDtypeStruct(w.shape, F32) for w in ws] * 3,
        compiler_params=_cp(48),
    )(*ws, *gs, *ms, *vs)


def _adamw_small(ws, gs, ms, vs):
    n = len(ws)

    def body(*refs):
        for a in range(n):
            d, m2, v2 = _adamw_math(refs[a][...], refs[n + a][...], refs[2 * n + a][...], refs[3 * n + a][...])
            refs[4 * n + a][...] = d
            refs[5 * n + a][...] = m2
            refs[6 * n + a][...] = v2

    return pl.pallas_call(
        body, name="adamw_small",
        in_specs=[VMEM_SPEC] * (4 * n), out_specs=[VMEM_SPEC] * (3 * n),
        out_shape=[jax.ShapeDtypeStruct(w.shape, F32) for w in ws] * 3,
        compiler_params=pltpu.CompilerParams(vmem_limit_bytes=40 << 20),
    )(*ws, *gs, *ms, *vs)


def kernel(x, meta_tokens, norm_mix_w, w_in, w_gate_up, b_gate, gla_norm_w, sinks, w_out, norm_ff_w, w_ff1, w_ff2, final_norm_w, loss_target, m_meta_tokens, m_norm_mix_w, m_w_in, m_w_gate_up, m_b_gate, m_gla_norm_w, m_sinks, m_w_out, m_norm_ff_w, m_w_ff1, m_w_ff2, m_final_norm_w, v_meta_tokens, v_norm_mix_w, v_w_in, v_w_gate_up, v_b_gate, v_gla_norm_w, v_sinks, v_w_out, v_norm_ff_w, v_w_ff1, v_w_ff2, v_final_norm_w):
    xi, yi, ci = _place()
    shard = (2 * xi + yi).astype(jnp.int32).reshape(1)
    core = ci.astype(jnp.int32).reshape(1)

    small = jnp.concatenate([meta_tokens, w_gate_up[0], jnp.zeros((NM, 64), F32)], axis=1)
    wt3, g_small = _run_comm(_gather_shards([_bf(w_in[0].T), small], [True, False]), "gather_w_in")
    meta = g_small[:, :, 0:256].transpose(1, 0, 2).reshape(NM, D)
    wgu = g_small[:, :, 256:320].transpose(1, 0, 2).reshape(NM, 256)

    xs, tgt = x[0], loss_target[0]
    t = xs.shape[0]
    wfin = final_norm_w.reshape(1, D)
    metapad = jnp.concatenate([meta, jnp.zeros((TM - NM, D), F32)], axis=0)
    wgu_p = _bf(jnp.concatenate([wgu, jnp.zeros((128 - 16, 256), F32)], axis=0))
    tabs = _rope_tables(t)

    w1s, w2s = _bf(w_ff1[0]), _bf(w_ff2[0])
    proj, (g_out, w1a, w1b) = _proj_fwd(xs, metapad, norm_mix_w, wt3, tabs,
                                        _gather_shards([_bf(w_out[0]), w1s[:HK], w1s[HK:]], [True] * 3))
    (oswa, lse), (w2a, w2b) = _swa_fwd(proj, sinks, t, _gather_shards([w2s[:HK], w2s[HK:]], [True] * 2))
    (ogla, oraw, sst, bcum, dgate), _ = _gla_fwd(proj, wgu_p, b_gate, gla_norm_w, t)
    wo, w1, w2 = g_out.reshape(D, D), (w1a, w1b), (w2a, w2b)
    h1, f, a, dh2, loss, gfin = _mlp_fwd(xs, metapad, tgt, ogla, oswa, wo, norm_ff_w, w1, w2, wfin)

    da, dh2b, dh1, do, dwo, gff = _mlp_bwd(h1, a, dh2, ogla, oswa, wo, norm_ff_w, w1, w2)
    dw1, dw2 = _ffn_wgrad(f, a, da, dh2b)
    big = [dwo, dw1, dw2]
    (dsq, dsk, dsv, dsink), theirs = _swa_bwd(proj, sinks, lse, do, t, _swap_halves(big))
    sums_bf, own = _sum_cores(jnp.concatenate([core, shard]), big, theirs)
    (dgla, dlr, dwgu, dbg, dgnw), arrived = _gla_bwd(proj, oraw, sst, bcum, dgate, do, wgu_p, gla_norm_w, t,
                                                     _scatter_shards(sums_bf))
    halves = _sum_chips(own, arrived)
    (gx, gmeta, dwt, gmix), _ = _proj_bwd(xs, metapad, norm_mix_w, wt3, tabs, dgla, dsq, dsk, dsv, dlr, dh1)

    gwt_in, joined = _reduce_w_in(dwt, _join_halves(halves))
    gw_out, gw_1, gw_2 = [j.reshape((-1, j.shape[2])) for j in joined]

    tail = jnp.concatenate([dbg, dgnw, dsink, loss, jnp.zeros((1, D - 256 - 128 - 8 - 1), F32)], axis=1)
    pack = jnp.concatenate([gmeta, gmix, gff, gfin, tail, dwgu[:16].reshape(4, D)], axis=0)
    tot = _allreduce_small(pack)
    g_meta = lax.dynamic_slice_in_dim(tot[0:NM], shard[0] * 256, 256, axis=1)
    g_mix, g_ff, g_fin = tot[16:17], tot[17:18], tot[18]
    g_bg, g_gnw, g_sinks, loss_tot = tot[19:20, 0:256], tot[19:20, 256:384], tot[19:20, 384:392], tot[19, 392]
    g_wgu = lax.dynamic_slice_in_dim(tot[20:24].reshape(NM, 256), shard[0] * 64, 64, axis=1)

    bo = _adamw_big([w_out[0], w_ff1[0], w_ff2[0]], [gw_out, gw_1, gw_2], [m_w_out[0], m_w_ff1[0], m_w_ff2[0]],
                    [v_w_out[0], v_w_ff1[0], v_w_ff2[0]])

    fin2 = lambda a: a.reshape(1, D)
    sw = [meta_tokens, norm_mix_w, w_gate_up[0], b_gate, gla_norm_w, sinks, norm_ff_w, fin2(final_norm_w), w_in[0].T]
    sg = [g_meta, g_mix, g_wgu, g_bg, g_gnw, g_sinks, g_ff, fin2(g_fin), gwt_in]
    sm = [m_meta_tokens, m_norm_mix_w, m_w_gate_up[0], m_b_gate, m_gla_norm_w, m_sinks, m_norm_ff_w, fin2(m_final_norm_w),
          m_w_in[0].T]
    sv = [v_meta_tokens, v_norm_mix_w, v_w_gate_up[0], v_b_gate, v_gla_norm_w, v_sinks, v_norm_ff_w, fin2(v_final_norm_w),
          v_w_in[0].T]
    so = _adamw_small(sw, sg, sm, sv)

    def ordered(small_o, big_o):
        meta_, mix_, wgu_, bg_, gnw_, sinks_, ff_, fin_, wt_ = small_o
        w_out_, w_1_, w_2_ = big_o
        return (meta_, mix_, wt_.T[None], wgu_[None], bg_, gnw_, sinks_, w_out_[None], ff_, w_1_[None], w_2_[None],
                fin_.reshape(D))

    grads = ordered(sg, [gw_out, gw_1, gw_2])
    deltas = ordered(so[0:9], bo[0:3])
    new_m = ordered(so[9:18], bo[3:6])
    new_v = ordered(so[18:27], bo[6:9])
    return (loss_tot, gx[None], *grads, *deltas, *new_m, *new_v)
```

```python
import functools
from typing import Callable, NamedTuple

import jax
import jax.numpy as jnp
import numpy as np
from jax import lax
from jax.experimental import pallas as pl
from jax.experimental.pallas import tpu as pltpu

F32 = jnp.float32
BF16 = jnp.bfloat16

D = 1024
DFF = 4096
NM = 16
TM = 256
DK = 64
CH = 128
SB = 128
QB = 2 * SB
EPS = 1e-5
C_GQ, C_GK, C_GV, C_GR, C_SQ, C_SK, C_SV, C_LR, DINP = 0, 256, 512, 1024, 1536, 2048, 2176, 2304, 2432
DIN = 2320
R_LR = 1536
ROPE_THETA = 500000.0
ADAM_LR, ADAM_B1, ADAM_B2, ADAM_EPS, ADAM_WD, ADAM_STEP = 0.001, 0.9, 0.999, 1e-08, 0.01, 10
NEG = -1e30
MESH = pl.DeviceIdType.MESH
VMEM_SPEC = pl.BlockSpec(memory_space=pltpu.VMEM)
ANY_SPEC = pl.BlockSpec(memory_space=pl.ANY)
SMEM_SPEC = pl.BlockSpec(memory_space=pltpu.SMEM)


def _cp(vmem_mb, sem=("arbitrary",)):
    return pltpu.CompilerParams(dimension_semantics=sem, vmem_limit_bytes=vmem_mb << 20)


def _dot(a, b):
    return jnp.dot(a, b, preferred_element_type=F32)


def _dot_nt(a, b):
    return lax.dot_general(a, b, (((1,), (1,)), ((), ())), preferred_element_type=F32)


def _dot_tn(a, b):
    return lax.dot_general(a, b, (((0,), (0,)), ((), ())), preferred_element_type=F32)


def _bf(x):
    return x.astype(BF16)


def _dot3(m01, x):
    x1 = _bf(x)
    r1 = x - x1.astype(F32)
    x2 = _bf(r1)
    x3 = _bf(r1 - x2.astype(F32))
    return _dot(m01, x1) + _dot(m01, x2) + _dot(m01, x3)


def _rms(h):
    rs = lax.rsqrt(jnp.mean(h * h, axis=-1, keepdims=True) + EPS)
    return h * rs, rs


def _rms_bwd(dy, yhat, rs, w):
    dyh = dy * w
    return rs * (dyh - yhat * jnp.mean(dyh * yhat, axis=-1, keepdims=True))


class _Comm(NamedTuple):
    ins: tuple
    outs: tuple
    sems: tuple
    phases: int
    plan: Callable


def _run_phase(fns):
    for fn in fns:
        fn()


def _call(body, name, grid, in_specs, out_specs, out_shape, scratch, params, args, comm=None):
    if comm is None:
        outs = pl.pallas_call(body, name=name, grid=grid, in_specs=in_specs, out_specs=out_specs, out_shape=out_shape,
                              scratch_shapes=scratch, compiler_params=params)(*args)
        return outs, None
    n_in, n_out, n_scr = len(in_specs), len(out_specs), len(scratch)
    ci, co = len(comm.ins), len(comm.outs)
    last = grid[0] - 1
    marks = [0, max(1, last - max(2, (last + 1) // 6))][:comm.phases]

    def wrapped(*refs):
        own_in, c_in = refs[:n_in], refs[n_in:n_in + ci]
        refs = refs[n_in + ci:]
        own_out, c_out = refs[:n_out], refs[n_out:n_out + co]
        refs = refs[n_out + co:]
        own_scr, c_sem = refs[:n_scr], refs[n_scr:]
        i = pl.program_id(0)

        for p, mark in enumerate(marks):
            @pl.when(i == mark)
            def _():
                plan = comm.plan(c_in, c_out, c_sem)
                if p > 0:
                    _run_phase(plan[p - 1][1])
                _run_phase(plan[p][0])

        body(*own_in, *own_out, *own_scr)

        @pl.when(i == last)
        def _():
            _run_phase(comm.plan(c_in, c_out, c_sem)[-1][1])

    outs = pl.pallas_call(
        wrapped, name=name, grid=grid, in_specs=list(in_specs) + [ANY_SPEC] * ci, out_specs=list(out_specs) + [ANY_SPEC] * co,
        out_shape=list(out_shape) + list(comm.outs), scratch_shapes=list(scratch) + list(comm.sems), compiler_params=params,
    )(*args, *comm.ins)
    return outs[:n_out], outs[n_out:]


def _run_comm(comm, name):
    ci, co = len(comm.ins), len(comm.outs)

    def body(*refs):
        for starts, waits in comm.plan(refs[:ci], refs[ci:ci + co], refs[ci + co:]):
            _run_phase(starts)
            _run_phase(waits)

    return pl.pallas_call(body, name=name, in_specs=[ANY_SPEC] * ci, out_specs=[ANY_SPEC] * co, out_shape=list(comm.outs),
                          scratch_shapes=list(comm.sems))(*comm.ins)


def _join_shards(w3_ref, w_ref):
    for s in range(4):
        w_ref[(DIN // 4) * s:(DIN // 4) * (s + 1), :] = w3_ref[s]


def _proj_fwd(x, metapad, wm, wt3, tabs, comm=None):
    t = x.shape[0]
    nblk = t // TM

    def body(x_ref, mp_ref, wm_ref, w3_ref, tab_ref, proj_ref, w_ref):
        i = pl.program_id(0)

        @pl.when(i == 0)
        def _():
            _join_shards(w3_ref, w_ref)

        h = jnp.where(i == nblk, mp_ref[...], x_ref[...])
        u, _ = _rms(h)
        ub = _bf(u * wm_ref[...])
        proj_ref[:, 0:C_SQ] = _dot_nt(ub, w_ref[0:R_LR, :])
        att = _dot_nt(ub, w_ref[R_LR + 16:DIN, :])
        tab = tab_ref[...]
        proj_ref[:, C_SQ:C_SK] = _rope(att[:, 0:512], tab, 1.0) * 0.125
        proj_ref[:, C_SK:C_SV] = _rope(att[:, 512:640], tab, 1.0)
        proj_ref[:, C_SV:C_LR] = att[:, 640:768]
        proj_ref[:, C_LR:DINP] = jnp.zeros((TM, DINP - C_LR), F32)
        proj_ref[:, C_LR:C_LR + 16] = _dot_nt(ub, w_ref[R_LR:R_LR + 16, :])

    (proj,), got = _call(
        body, "proj_fwd", (nblk + 1,),
        [pl.BlockSpec((TM, D), lambda i: (jnp.minimum(i, nblk - 1), 0)), VMEM_SPEC, VMEM_SPEC, VMEM_SPEC,
         pl.BlockSpec((TM, 128), lambda i: (i, 0))],
        [pl.BlockSpec((TM, DINP), lambda i: (i, 0))], [jax.ShapeDtypeStruct((t + TM, DINP), F32)],
        [pltpu.VMEM((DIN, D), BF16)], _cp(48), (x, metapad, wm, wt3, tabs), comm)
    return proj, got


def _chunk_masks():
    r = lax.broadcasted_iota(jnp.int32, (TM, TM), 0)
    c = lax.broadcasted_iota(jnp.int32, (TM, TM), 1)
    same = (r // CH) == (c // CH)
    lower = _bf(jnp.where(same & (c <= r), 1.0, 0.0))
    upper = _bf(jnp.where(same & (c >= r), 1.0, 0.0))
    return lower, upper


def _gla_gate(lr, wgu, bg, valid, lower):
    z = _dot(_bf(lr), wgu) + bg
    g = (jnp.minimum(z, 0.0) - jnp.log(1.0 + jnp.exp(-jnp.abs(z)))) * (1.0 / 16.0)
    g = jnp.where(valid, g, 0.0)
    return z, _dot3(lower, g)


def _gla_decays(q, k, b):
    nc = TM // CH
    b3 = b.reshape(nc, CH, 256)
    blast = b3[:, CH - 1:CH, :]
    eb = jnp.exp(b)
    enb = jnp.exp(-b)
    ebl = jnp.exp(blast - b3).reshape(TM, 256)
    return eb, enb, ebl, jnp.exp(blast)


def _tri(lower_incl):
    r = lax.broadcasted_iota(jnp.int32, (CH, CH), 0)
    c = lax.broadcasted_iota(jnp.int32, (CH, CH), 1)
    return ((c <= r) if lower_incl else (c >= r))[None]


def _gla_fwd(proj, wgu, bg, gnw, t, comm=None):
    nblk = t // TM
    nt = nblk + 1
    nc = TM // CH

    def blk(i):
        return (i + nblk) % nt

    def body(q_ref, k_ref, v_ref, r_ref, lr_ref, wgu_ref, bg_ref, gnw_ref, o_ref, oraw_ref, sst_ref, b_ref, dgate_ref,
             st_scr):
        i = pl.program_id(0)

        @pl.when(i == 0)
        def _():
            st_scr[...] = jnp.zeros_like(st_scr)

        rows = blk(i) * TM + lax.broadcasted_iota(jnp.int32, (TM, 1), 0)
        lower, _ = _chunk_masks()
        valid = rows < t + NM
        z, b = _gla_gate(lr_ref[...], wgu_ref[...], bg_ref[...], valid, lower)
        b_ref[...] = b
        dgate_ref[...] = jnp.where(valid, (1.0 / 16.0) / (1.0 + jnp.exp(z)), 0.0)
        q = q_ref[...]
        k = k_ref[...]
        eb, enb, ebl, eblast = _gla_decays(q, k, b)
        qt = q * 0.125 * eb
        kt = k * enb
        kh = k * ebl
        tril = _tri(True)
        heads = range(4)
        hs = [slice(h * DK, (h + 1) * DK) for h in heads]
        qh = [_bf(qt[:, hs[h]]).reshape(nc, CH, DK) for h in heads]
        kth = [_bf(kt[:, hs[h]]).reshape(nc, CH, DK) for h in heads]
        khh = [_bf(kh[:, hs[h]]).reshape(nc, CH, DK) for h in heads]
        vh = [_bf(v_ref[:, h * 128:(h + 1) * 128]).reshape(nc, CH, 128) for h in heads]
        a = [jnp.einsum('cid,cjd->cij', qh[h], kth[h], preferred_element_type=F32) for h in heads]
        kv = [jnp.einsum('cjv,cjd->cvd', vh[h], khh[h], preferred_element_type=F32) for h in heads]
        o = [jnp.einsum('cij,cjv->civ', _bf(jnp.where(tril, a[h], 0.0)), vh[h], preferred_element_type=F32) for h in heads]
        states = []
        for h in heads:
            st = st_scr[h]
            per_chunk = []
            for c in range(nc):
                sst_ref[c, h] = st
                per_chunk.append(_bf(st))
                st = st * eblast[c, :, hs[h]] + kv[h][c]
            st_scr[h] = st
            states.append(per_chunk)
        o_inter = [[_dot_nt(qh[h][c], states[h][c]) for c in range(nc)] for h in heads]
        oraw = jnp.concatenate([(o[h] + jnp.stack(o_inter[h])).reshape(TM, 128) for h in heads], axis=1)
        oraw_ref[...] = oraw
        gn = gnw_ref[...]
        res = []
        for h in range(4):
            on, _ = _rms(oraw[:, h * 128:(h + 1) * 128])
            r = r_ref[:, h * 128:(h + 1) * 128]
            res.append(on * gn * (r * jax.nn.sigmoid(r)))
        o_ref[...] = _bf(jnp.concatenate(res, axis=1))

    def spec(w, cb):
        return pl.BlockSpec((TM, w), lambda i: (blk(i), cb))

    return _call(
        body, "gla_fwd", (nt,),
        [spec(256, 0), spec(256, 1), spec(512, 1), spec(512, 2), spec(128, C_LR // 128), VMEM_SPEC, VMEM_SPEC, VMEM_SPEC],
        [spec(512, 0), spec(512, 0), pl.BlockSpec((nc, 4, 128, DK), lambda i: (blk(i), 0, 0, 0)), spec(256, 0), spec(256, 0)],
        [jax.ShapeDtypeStruct((t + TM, 512), BF16), jax.ShapeDtypeStruct((t + TM, 512), F32),
         jax.ShapeDtypeStruct((nt * nc, 4, 128, DK), F32), jax.ShapeDtypeStruct((t + TM, 256), F32),
         jax.ShapeDtypeStruct((t + TM, 256), F32)],
        [pltpu.VMEM((4, 128, DK), F32)], _cp(40), (proj, proj, proj, proj, proj, wgu, bg, gnw), comm)


def _gla_bwd(proj, oraw, sst, bcum, dgate, do, wgu, gnw, t, comm=None):
    nblk = t // TM
    nt = nblk + 1
    nc = TM // CH

    def blk(i):
        return (2 * nblk - i) % nt

    def body(q_ref, k_ref, v_ref, r_ref, lr_ref, oraw_ref, sst_ref, b_ref, dgate_ref, do_ref, wgu_ref, gnw_ref,
             dgla_ref, dlr_ref, dwgu_ref, dbg_ref, dgnw_ref, dst_scr):
        i = pl.program_id(0)

        @pl.when(i == 0)
        def _():
            dst_scr[...] = jnp.zeros_like(dst_scr)
            dwgu_ref[...] = jnp.zeros_like(dwgu_ref)
            dbg_ref[...] = jnp.zeros_like(dbg_ref)
            dgnw_ref[...] = jnp.zeros_like(dgnw_ref)

        _, upper = _chunk_masks()
        lr = lr_ref[...]
        b = b_ref[...]
        q = q_ref[...]
        k = k_ref[...]
        eb, enb, ebl, eblast = _gla_decays(q, k, b)
        qt = q * 0.125 * eb
        kt = k * enb
        kh = k * ebl
        gn = gnw_ref[...]
        tril = _tri(True)
        triu = _tri(False)
        heads = range(4)
        hs = [slice(h * DK, (h + 1) * DK) for h in heads]
        vs = [slice(h * 128, (h + 1) * 128) for h in heads]
        ein = functools.partial(jnp.einsum, preferred_element_type=F32)
        dr_l, doh = [], []
        dgn = jnp.zeros((1, 128), F32)
        for h in heads:
            on, rs = _rms(oraw_ref[:, vs[h]])
            r = r_ref[:, vs[h]]
            sig = jax.nn.sigmoid(r)
            sil = r * sig
            dy = do_ref[:, vs[h]]
            dr_l.append(dy * on * gn * (sig * (1.0 + r * (1.0 - sig))))
            dgn = dgn + jnp.sum(dy * sil * on, axis=0, keepdims=True)
            doh.append(_bf(_rms_bwd(dy * sil, on, rs, gn)).reshape(nc, CH, 128))
        dgnw_ref[...] += dgn
        qh = [_bf(qt[:, hs[h]]).reshape(nc, CH, DK) for h in heads]
        kth = [_bf(kt[:, hs[h]]).reshape(nc, CH, DK) for h in heads]
        khh = [_bf(kh[:, hs[h]]).reshape(nc, CH, DK) for h in heads]
        vh = [_bf(v_ref[:, vs[h]]).reshape(nc, CH, 128) for h in heads]
        at = [ein('cjd,cid->cji', kth[h], qh[h]) for h in heads]
        da = [ein('civ,cjv->cij', doh[h], vh[h]) for h in heads]
        dat = [ein('cjv,civ->cji', vh[h], doh[h]) for h in heads]
        gq = [ein('civ,cid->cvd', doh[h], qh[h]) for h in heads]
        stf = [sst_ref[:, h] for h in heads]
        dqs = [ein('civ,cvd->cid', doh[h], _bf(stf[h])) for h in heads]
        dv = [ein('cji,civ->cjv', _bf(jnp.where(triu, at[h], 0.0)), doh[h]) for h in heads]
        dqt = [ein('cij,cjd->cid', _bf(jnp.where(tril, da[h], 0.0)), kth[h]) + dqs[h] for h in heads]
        dkt = [ein('cji,cid->cjd', _bf(jnp.where(triu, dat[h], 0.0)), qh[h]) for h in heads]
        dse = []
        for h in heads:
            dst = dst_scr[h]
            dsend = [None] * nc
            for c in reversed(range(nc)):
                dsend[c] = dst
                dst = dst * eblast[c, :, hs[h]] + gq[h][c]
            dst_scr[h] = dst
            dse.append(jnp.stack(dsend))
        dseb = [_bf(d) for d in dse]
        dv = [dv[h] + ein('cjd,cvd->cjv', khh[h], dseb[h]) for h in heads]
        dkh = [ein('cjv,cvd->cjd', vh[h], dseb[h]) for h in heads]
        carried = jnp.concatenate([jnp.sum(dse[h] * stf[h], axis=1, keepdims=True) for h in heads], axis=2)
        wide = lambda parts: jnp.concatenate([p.reshape(TM, DK) for p in parts], axis=1)
        dqt_w, dkt_w, dkh_w = wide(dqt), wide(dkt), wide(dkh)
        dkh_kh = dkh_w * kh
        extra = jnp.sum(dkh_kh.reshape(nc, CH, 256), axis=1, keepdims=True) + eblast * carried
        db = dqt_w * qt - dkt_w * kt - dkh_kh
        dg = _dot3(upper, db) + jnp.broadcast_to(extra, (nc, CH, 256)).reshape(TM, 256)
        dz = dg * dgate_ref[...]
        dzb = _bf(dz)
        dlr_ref[...] = _bf(_dot_nt(dzb, wgu_ref[...]))
        dwgu_ref[...] += _dot_tn(_bf(lr), dzb)
        dbg_ref[...] += jnp.sum(dz, axis=0, keepdims=True)
        dq = dqt_w * eb * 0.125
        dk = dkt_w * enb + dkh_w * ebl
        dgla_ref[...] = _bf(jnp.concatenate([dq, dk] + [d.reshape(TM, 128) for d in dv] + dr_l, axis=1))

    def spec(w, cb):
        return pl.BlockSpec((TM, w), lambda i: (blk(i), cb))

    def acc(shape):
        return pl.BlockSpec(shape, lambda i: (0, 0))

    return _call(
        body, "gla_bwd", (nt,),
        [spec(256, 0), spec(256, 1), spec(512, 1), spec(512, 2), spec(128, C_LR // 128), spec(512, 0),
         pl.BlockSpec((nc, 4, 128, DK), lambda i: (blk(i), 0, 0, 0)), spec(256, 0), spec(256, 0), spec(512, 0),
         VMEM_SPEC, VMEM_SPEC],
        [spec(1536, 0), spec(128, 0), acc((128, 256)), acc((1, 256)), acc((1, 128))],
        [jax.ShapeDtypeStruct((t + TM, 1536), BF16), jax.ShapeDtypeStruct((t + TM, 128), BF16),
         jax.ShapeDtypeStruct((128, 256), F32), jax.ShapeDtypeStruct((1, 256), F32), jax.ShapeDtypeStruct((1, 128), F32)],
        [pltpu.VMEM((4, 128, DK), F32)], _cp(48), (proj, proj, proj, proj, proj, oraw, sst, bcum, dgate, do, wgu, gnw), comm)


def _rope_tables(t):
    r = t + TM
    row = np.arange(r)
    pos = np.where(row < t, row + NM, np.where(row < t + NM, row - t, 0)).astype(np.float32)
    inv_freq = (1.0 / (np.float32(ROPE_THETA) ** (np.arange(0, 16, 2, dtype=np.float32) / np.float32(16)))).astype(np.float32)
    ang = (pos[:, None] * inv_freq[None, :]).astype(np.float32)
    cos, sin = np.cos(ang).astype(np.float32), np.sin(ang).astype(np.float32)
    one, zero = np.ones((r, 48), np.float32), np.zeros((r, 48), np.float32)
    return jnp.asarray(np.concatenate([cos, cos, one, -sin, sin, zero], axis=1))


def _rope(x, tab, sign):
    w = x.shape[1]
    rep = w // 64
    c = jnp.concatenate([tab[:, 0:64]] * rep, axis=1)
    s = jnp.concatenate([tab[:, 64:128]] * rep, axis=1)
    lane = lax.rem(lax.broadcasted_iota(jnp.int32, x.shape, 1), 64)
    partner = jnp.where(lane < 8, pltpu.roll(x, w - 8, 1), jnp.where(lane < 16, pltpu.roll(x, 8, 1), 0.0))
    return x * c + sign * (partner * s)


HB_BWD = 4


def _stack(x, hg):
    w = x.shape[1] // hg
    return x if hg == 1 else jnp.concatenate([x[:, g * w:(g + 1) * w] for g in range(hg)], axis=0)


def _unstack(x, hg):
    return x if hg == 1 else jnp.concatenate([x[g * SB:(g + 1) * SB] for g in range(hg)], axis=1)


def _swa_specs(nsb):
    def rows(h, w, cb, f):
        return pl.BlockSpec((h, w), lambda i: (f(i), cb))
    pair = lambda i: i
    prev = lambda i: jnp.maximum(2 * i - 1, 0)
    meta = lambda i: nsb
    return rows, pair, prev, meta


def _swa_fwd(proj, sinks, t, comm=None):
    nsb = t // SB
    r_tot = t + TM
    rows, pair, prev, meta = _swa_specs(nsb)

    def body(sink_ref, q_ref, kc_ref, kp_ref, km_ref, vc_ref, vp_ref, vm_ref, o_ref, lse_ref):
        i = pl.program_id(0)
        key = lax.broadcasted_iota(jnp.int32, (SB, SB), 0)
        qry = lax.broadcasted_iota(jnp.int32, (SB, SB), 1)
        km, vm = km_ref[0:NM, :], vm_ref[0:NM, :]
        for j in range(2):
            b = 2 * i + j
            rs = slice(j * SB, (j + 1) * SB)
            real = b < nsb
            masks = (key <= qry, (key > qry) & (b > 0) & real, real)
            k3 = (kc_ref[rs, :], kp_ref[...] if j == 0 else kc_ref[0:SB, :], km)
            v3 = (vc_ref[rs, :], vp_ref[...] if j == 0 else vc_ref[0:SB, :], vm)
            valid = b * SB + lax.broadcasted_iota(jnp.int32, (1, SB), 1) < t + NM
            heads = range(8)
            kb = [[_bf(k[:, kv * 64:(kv + 1) * 64]) for k in k3] for kv in range(2)]
            vt = [[_bf(v[:, kv * 64:(kv + 1) * 64].T) for v in v3] for kv in range(2)]
            raw = [[_dot_nt(k, _bf(q_ref[rs, h * 64:(h + 1) * 64])) for k in kb[h // 4]] for h in heads]
            probs, inv_l, lse_l = [], [], []
            for h in heads:
                s = [jnp.where(m, sx, NEG) for m, sx in zip(masks, raw[h])]
                sink = sink_ref[0, h]
                top = jnp.maximum(jnp.max(jnp.maximum(s[0], s[1]), axis=0, keepdims=True),
                                  jnp.maximum(jnp.max(s[2], axis=0, keepdims=True), sink))
                p = [jnp.exp(sx - top) for sx in s]
                l = (jnp.sum(p[0] + p[1], axis=0, keepdims=True) + jnp.sum(p[2], axis=0, keepdims=True)
                     + jnp.exp(sink - top))
                probs.append([_bf(px) for px in p])
                inv_l.append(1.0 / l)
                lse_l.append(top + jnp.log(l))
            o_t = [_dot(vt[h // 4][0], probs[h][0]) + _dot(vt[h // 4][1], probs[h][1]) + _dot(vt[h // 4][2], probs[h][2])
                   for h in heads]
            o_ref[rs, :] = _bf(jnp.concatenate([jnp.where(valid, o_t[h] * inv_l[h], 0.0).T for h in heads], axis=1))
            lse_ref[:, rs] = jnp.concatenate(lse_l, axis=0)

    ck, cv = C_SK // 128, C_SV // 128
    return _call(
        body, "swa_fwd", (r_tot // QB,),
        [SMEM_SPEC, rows(QB, 512, C_SQ // 512, pair),
         rows(QB, 128, ck, pair), rows(SB, 128, ck, prev), rows(SB, 128, ck, meta),
         rows(QB, 128, cv, pair), rows(SB, 128, cv, prev), rows(SB, 128, cv, meta)],
        [rows(QB, 512, 0, pair), pl.BlockSpec((8, QB), lambda i: (0, i))],
        [jax.ShapeDtypeStruct((r_tot, 512), BF16), jax.ShapeDtypeStruct((8, r_tot), F32)],
        [], _cp(32), (sinks, proj, proj, proj, proj, proj, proj, proj), comm)


def _swa_bwd(proj, sinks, lse_t, do, t, comm=None):
    nsb = t // SB
    r_tot = t + TM
    rows, pair, prev, meta = _swa_specs(nsb)
    hb = HB_BWD
    lanes = hb * SB

    def body(sink_ref, q_ref, kc_ref, kp_ref, km_ref, vc_ref, vp_ref, vm_ref, lse_ref, do_ref,
             dq_ref, dk_ref, dv_ref, dsink_ref):
        i = pl.program_id(0)

        @pl.when(i == 0)
        def _():
            dk_ref[...] = jnp.zeros_like(dk_ref)
            dv_ref[...] = jnp.zeros_like(dv_ref)
            dsink_ref[...] = jnp.zeros_like(dsink_ref)

        key = lax.broadcasted_iota(jnp.int32, (SB, lanes), 0)
        qry = lax.rem(lax.broadcasted_iota(jnp.int32, (SB, lanes), 1), SB)
        km, vm = km_ref[0:NM, :], vm_ref[0:NM, :]
        dsink_l = []
        for j in range(2):
            b = 2 * i + j
            rs = slice(j * SB, (j + 1) * SB)
            real = b < nsb
            masks = (key <= qry, (key > qry) & (b > 0) & real, real)
            k3 = (kc_ref[rs, :], kp_ref[...] if j == 0 else kc_ref[0:SB, :], km)
            v3 = (vc_ref[rs, :], vp_ref[...] if j == 0 else vc_ref[0:SB, :], vm)
            groups = list(range(0, 8, hb))
            kvs = [h0 // 4 for h0 in groups]
            qg = [_bf(_stack(q_ref[rs, h0 * 64:(h0 + hb) * 64], hb)) for h0 in groups]
            dog = [_bf(_stack(do_ref[rs, h0 * 64:(h0 + hb) * 64], hb)) for h0 in groups]
            kb = [[_bf(k[:, kv * 64:(kv + 1) * 64]) for k in k3] for kv in kvs]
            vb = [[_bf(v[:, kv * 64:(kv + 1) * 64]) for v in v3] for kv in kvs]
            s = [[_dot_nt(k, qg[g]) for k in kb[g]] for g in range(len(groups))]
            dp = [[_dot_nt(v, dog[g]) for v in vb[g]] for g in range(len(groups))]
            p, ds, ds_blk = [], [], []
            for g, h0 in enumerate(groups):
                lse_row = jnp.concatenate([lse_ref[h:h + 1, rs] for h in range(h0, h0 + hb)], axis=1)
                sink_row = jnp.concatenate([jnp.full((1, SB), sink_ref[0, h], F32) for h in range(h0, h0 + hb)], axis=1)
                pg = [jnp.exp(jnp.where(m, sx, NEG) - lse_row) for m, sx in zip(masks, s[g])]
                delta = (jnp.sum(pg[0] * dp[g][0] + pg[1] * dp[g][1], axis=0, keepdims=True)
                         + jnp.sum(pg[2] * dp[g][2], axis=0, keepdims=True))
                ds.append([_bf(pp * (dd - delta)) for pp, dd in zip(pg, dp[g])])
                p.append([_bf(pp) for pp in pg])
                ds_row = -jnp.exp(sink_row - lse_row) * delta
                ds_blk += [jnp.sum(ds_row[:, q0 * SB:(q0 + 1) * SB], axis=1, keepdims=True) for q0 in range(hb)]
            dsink_l.append(jnp.concatenate(ds_blk, axis=1))
            dq_t = [_dot_tn(kb[g][0], ds[g][0]) + _dot_tn(kb[g][1], ds[g][1]) + _dot_tn(kb[g][2], ds[g][2])
                    for g in range(len(groups))]
            dq_ref[rs, :] = jnp.concatenate([_unstack(d.T, hb) for d in dq_t], axis=1)
            windows = (pl.ds(pl.multiple_of(b * SB, SB), SB), pl.ds(pl.multiple_of(jnp.maximum(b - 1, 0) * SB, SB), SB),
                       pl.ds(t, NM))
            for x in range(3):
                dk_kv, dv_kv = [], []
                for kv in range(2):
                    mine = [g for g in range(len(groups)) if kvs[g] == kv]
                    dk_kv.append(sum(_dot(ds[g][x], qg[g]) for g in mine))
                    dv_kv.append(sum(_dot(p[g][x], dog[g]) for g in mine))
                dk_ref[windows[x], :] += jnp.concatenate(dk_kv, axis=1)
                dv_ref[windows[x], :] += jnp.concatenate(dv_kv, axis=1)
        dsink_ref[...] += dsink_l[0] + dsink_l[1]

    ck, cv = C_SK // 128, C_SV // 128
    whole = lambda w: pl.BlockSpec((r_tot, w), lambda i: (0, 0))
    return _call(
        body, "swa_bwd", (r_tot // QB,),
        [SMEM_SPEC, rows(QB, 512, C_SQ // 512, pair),
         rows(QB, 128, ck, pair), rows(SB, 128, ck, prev), rows(SB, 128, ck, meta),
         rows(QB, 128, cv, pair), rows(SB, 128, cv, prev), rows(SB, 128, cv, meta),
         pl.BlockSpec((8, QB), lambda i: (0, i)), rows(QB, 512, 1, pair)],
        [rows(QB, 512, 0, pair), whole(128), whole(128), pl.BlockSpec((1, 8), lambda i: (0, 0))],
        [jax.ShapeDtypeStruct((r_tot, 512), F32), jax.ShapeDtypeStruct((r_tot, 128), F32),
         jax.ShapeDtypeStruct((r_tot, 128), F32), jax.ShapeDtypeStruct((1, 8), F32)],
        [], _cp(48), (sinks, proj, proj, proj, proj, proj, proj, proj, lse_t, do), comm)


HK = D // 2


def _mlp_fwd(x, metapad, tgt, ogla, oswa, wo, wff, w1, w2, wfin):
    t = x.shape[0]
    nblk = t // TM

    def body(x_ref, mp_ref, tgt_ref, og_ref, os_ref, wo_ref, wff_ref, w1a_ref, w1b_ref, w2a_ref, w2b_ref, wfin_ref,
             h1_ref, f_ref, a_ref, dh2_ref, loss_ref, gfin_ref):
        i = pl.program_id(0)

        @pl.when(i == 0)
        def _():
            loss_ref[...] = jnp.zeros_like(loss_ref)
            gfin_ref[...] = jnp.zeros_like(gfin_ref)

        h0 = jnp.where(i == nblk, mp_ref[...], x_ref[...])
        h1 = h0 + _dot(og_ref[...], wo_ref[0:512, :]) + _dot(os_ref[...], wo_ref[512:1024, :])
        h1_ref[...] = h1
        fh, _ = _rms(h1)
        f = _bf(fh * wff_ref[...])
        f_ref[...] = f
        acc = jnp.zeros((TM, D), F32)
        for n in range(4):
            a = _dot(f[:, 0:HK], w1a_ref[n]) + _dot(f[:, HK:D], w1b_ref[n])
            a_ref[:, n * D:(n + 1) * D] = _bf(a)
            zr = jnp.maximum(a, 0.0)
            z = _bf(zr * zr)
            acc = acc + _dot(z[:, 0:HK], w2a_ref[n]) + _dot(z[:, HK:D], w2b_ref[n])
        h2 = h1 + acc
        yh, rs2 = _rms(h2)
        wf = wfin_ref[...]
        real = i < nblk
        e = jnp.where(real, yh * wf - tgt_ref[...], 0.0)
        loss_ref[...] += jnp.sum(jnp.sum(e * e, axis=0, keepdims=True), axis=1, keepdims=True) * (0.5 / D)
        dy = e * (1.0 / D)
        gfin_ref[...] += jnp.sum(dy * yh, axis=0, keepdims=True)
        dh2_ref[...] = _rms_bwd(dy, yh, rs2, wf)

    xs = pl.BlockSpec((TM, D), lambda i: (jnp.minimum(i, nblk - 1), 0))
    rs = lambda w: pl.BlockSpec((TM, w), lambda i: (i, 0))
    r_tot = t + TM
    return pl.pallas_call(
        body, name="mlp_fwd", grid=(nblk + 1,),
        in_specs=[xs, VMEM_SPEC, xs, rs(512), rs(512)] + [VMEM_SPEC] * 7,
        out_specs=[rs(D), rs(D), rs(DFF), rs(D), pl.BlockSpec((1, 1), lambda i: (0, 0)), pl.BlockSpec((1, D), lambda i: (0, 0))],
        out_shape=[jax.ShapeDtypeStruct((r_tot, D), F32), jax.ShapeDtypeStruct((r_tot, D), BF16),
                   jax.ShapeDtypeStruct((r_tot, DFF), BF16), jax.ShapeDtypeStruct((r_tot, D), F32),
                   jax.ShapeDtypeStruct((1, 1), F32), jax.ShapeDtypeStruct((1, D), F32)],
        compiler_params=_cp(56),
    )(x, metapad, tgt, ogla, oswa, wo, wff, *w1, *w2, wfin)


def _mlp_bwd(h1, a, dh2, ogla, oswa, wo, wff, w1, w2):
    r_tot = h1.shape[0]
    nt = r_tot // TM

    def body(h1_ref, a_ref, dh2_ref, og_ref, os_ref, wo_ref, wff_ref, w1a_ref, w1b_ref, w2a_ref, w2b_ref,
             da_ref, dh2b_ref, dh1_ref, do_ref, dwo_ref, gff_ref, dwo_acc):
        i = pl.program_id(0)

        @pl.when(i == 0)
        def _():
            dwo_acc[...] = jnp.zeros_like(dwo_acc)
            gff_ref[...] = jnp.zeros_like(gff_ref)

        dh2 = dh2_ref[...]
        dh2b = _bf(dh2)
        dh2b_ref[...] = dh2b
        dfa = jnp.zeros((TM, HK), F32)
        dfb = jnp.zeros((TM, HK), F32)
        for n in range(4):
            dz = jnp.concatenate([_dot_nt(dh2b, w2a_ref[n]), _dot_nt(dh2b, w2b_ref[n])], axis=1)
            da = _bf(dz * (2.0 * jnp.maximum(a_ref[:, n * D:(n + 1) * D].astype(F32), 0.0)))
            da_ref[:, n * D:(n + 1) * D] = da
            dfa = dfa + _dot_nt(da, w1a_ref[n])
            dfb = dfb + _dot_nt(da, w1b_ref[n])
        df = jnp.concatenate([dfa, dfb], axis=1)
        fh, rs1 = _rms(h1_ref[...])
        gff_ref[...] += jnp.sum(df * fh, axis=0, keepdims=True)
        dh1 = dh2 + _rms_bwd(df, fh, rs1, wff_ref[...])
        dh1_ref[...] = dh1
        dh1b = _bf(dh1)
        do_ref[...] = _dot_nt(dh1b, wo_ref[...])
        dwo_acc[0:512, :] += _dot_tn(og_ref[...], dh1b)
        dwo_acc[512:1024, :] += _dot_tn(os_ref[...], dh1b)

        @pl.when(i == nt - 1)
        def _():
            for s in range(4):
                for hh in range(2):
                    dwo_ref[hh, s] = dwo_acc[(2 * s + hh) * 128:(2 * s + hh + 1) * 128, :]

    rs = lambda w: pl.BlockSpec((TM, w), lambda i: (i, 0))
    return pl.pallas_call(
        body, name="mlp_bwd", grid=(nt,),
        in_specs=[rs(D), rs(DFF), rs(D), rs(512), rs(512)] + [VMEM_SPEC] * 6,
        out_specs=[rs(DFF), rs(D), rs(D), rs(D), VMEM_SPEC, pl.BlockSpec((1, D), lambda i: (0, 0))],
        out_shape=[jax.ShapeDtypeStruct((r_tot, DFF), BF16), jax.ShapeDtypeStruct((r_tot, D), BF16),
                   jax.ShapeDtypeStruct((r_tot, D), F32), jax.ShapeDtypeStruct((r_tot, D), F32),
                   jax.ShapeDtypeStruct((2, 4, 128, D), F32), jax.ShapeDtypeStruct((1, D), F32)],
        scratch_shapes=[pltpu.VMEM((D, D), F32)],
        compiler_params=_cp(56),
    )(h1, a, dh2, ogla, oswa, wo, wff, *w1, *w2)


def _ffn_wgrad(f, a, da, dh2b):
    r_tot = f.shape[0]
    kt = 768 if r_tot % 768 == 0 else TM
    nk = r_tot // kt

    def body(f_ref, a_ref, da_ref, dh2_ref, dw1_ref, dw2_ref, acc1, acc2):
        k = pl.program_id(1)

        @pl.when(k == 0)
        def _():
            acc1[...] = jnp.zeros_like(acc1)
            acc2[...] = jnp.zeros_like(acc2)

        zr = jnp.maximum(a_ref[...], 0.0)
        acc1[...] += _dot_tn(f_ref[...], da_ref[...])
        acc2[...] += _dot_tn(zr * zr, dh2_ref[...])

        @pl.when(k == nk - 1)
        def _():
            for hh in range(2):
                dw1_ref[hh, 0] = acc1[hh * 512:(hh + 1) * 512, :]
                dw2_ref[hh, 0] = acc2[hh * 512:(hh + 1) * 512, :]

    out = pl.BlockSpec((2, 1, 512, D), lambda n, k: (0, n, 0, 0))
    return pl.pallas_call(
        body, name="ffn_wgrad", grid=(4, nk),
        in_specs=[pl.BlockSpec((kt, D), lambda n, k: (k, 0)), pl.BlockSpec((kt, D), lambda n, k: (k, n)),
                  pl.BlockSpec((kt, D), lambda n, k: (k, n)), pl.BlockSpec((kt, D), lambda n, k: (k, 0))],
        out_specs=[out, out],
        out_shape=[jax.ShapeDtypeStruct((2, 4, 512, D), F32)] * 2,
        scratch_shapes=[pltpu.VMEM((D, D), F32), pltpu.VMEM((D, D), F32)],
        compiler_params=_cp(48, ("arbitrary", "arbitrary")),
    )(f, a, da, dh2b)


def _proj_bwd(x, metapad, wm, wt3, tabs, dgla, dswa_q, dsk, dsv, dlr, dh1, comm=None):
    t = x.shape[0]
    nblk = t // TM

    def body(x_ref, mp_ref, wm_ref, w3_ref, tab_ref, dg_ref, dq_ref, dk_ref, dv_ref, dlr_ref, dh1_ref,
             gx_ref, gmeta_ref, dw_ref, gmix_ref, w_ref, acc):
        i = pl.program_id(0)

        @pl.when(i == 0)
        def _():
            _join_shards(w3_ref, w_ref)
            acc[...] = jnp.zeros_like(acc)
            gmix_ref[...] = jnp.zeros_like(gmix_ref)

        h = jnp.where(i == nblk, mp_ref[...], x_ref[...])
        uh, rs = _rms(h)
        wm_v = wm_ref[...]
        u = _bf(uh * wm_v)
        tab = tab_ref[...]
        dq = _bf(_rope(dq_ref[...] * 0.125, tab, -1.0))
        dk = _bf(_rope(dk_ref[...], tab, -1.0))
        parts = ((dg_ref[...], 0, R_LR), (dlr_ref[:, 0:16], R_LR, 16), (dq, R_LR + 16, 512),
                 (dk, R_LR + 528, 128), (_bf(dv_ref[...]), R_LR + 656, 128))
        du = jnp.zeros((TM, D), F32)
        for val, r0, w in parts:
            du = du + _dot(val, w_ref[r0:r0 + w, :])
            acc[r0:r0 + w, :] += _dot_tn(val, u)
        gmix_ref[...] += jnp.sum(du * uh, axis=0, keepdims=True)
        dh0 = dh1_ref[...] + _rms_bwd(du, uh, rs, wm_v)

        @pl.when(i < nblk)
        def _():
            gx_ref[...] = dh0

        @pl.when(i == nblk)
        def _():
            gmeta_ref[...] = dh0[:NM]
            for s in range(4):
                dw_ref[s] = acc[(DIN // 4) * s:(DIN // 4) * (s + 1), :]

    xs = pl.BlockSpec((TM, D), lambda i: (jnp.minimum(i, nblk - 1), 0))
    rs_ = lambda w: pl.BlockSpec((TM, w), lambda i: (i, 0))
    return _call(
        body, "proj_bwd", (nblk + 1,),
        [xs, VMEM_SPEC, VMEM_SPEC, VMEM_SPEC, rs_(128), rs_(1536), rs_(512), rs_(128), rs_(128), rs_(128), rs_(D)],
        [xs, pl.BlockSpec((NM, D), lambda i: (0, 0)), VMEM_SPEC, pl.BlockSpec((1, D), lambda i: (0, 0))],
        [jax.ShapeDtypeStruct((t, D), F32), jax.ShapeDtypeStruct((NM, D), F32),
         jax.ShapeDtypeStruct((4, DIN // 4, D), F32), jax.ShapeDtypeStruct((1, D), F32)],
        [pltpu.VMEM((DIN, D), BF16), pltpu.VMEM((DIN, D), F32)], _cp(56),
        (x, metapad, wm, wt3, tabs, dgla, dswa_q, dsk, dsv, dlr, dh1), comm)


def _place():
    return lax.axis_index("x"), lax.axis_index("y"), lax.axis_index("c")


def _other_chips(x, y):
    return [(1 - x, y), (x, 1 - y), (1 - x, 1 - y)]


def _dma_sems(*counts):
    return tuple(pltpu.SemaphoreType.DMA((k,)) for k in counts)


def _gather_shards(shards, split):
    n = len(shards)
    two = [a for a in range(n) if split[a]]

    def plan(ins, outs, sems):
        isend, irecv, dsend, drecv, loc = sems
        x, y, c = _place()
        chips = _other_chips(x, y)

        def part(ref, a, half):
            if not split[a]:
                return ref
            w = shards[a].shape[1] // 2
            return ref.at[:, pl.ds(pl.multiple_of(half * w, 128), w)]

        def over_ici(a, k, shard_of):
            tx, ty = chips[k]
            sx, sy = shard_of
            return pltpu.make_async_remote_copy(
                src_ref=part(ins[a], a, c), dst_ref=part(outs[a].at[2 * sx + sy], a, c), send_sem=isend.at[3 * a + k],
                recv_sem=irecv.at[3 * a + k], device_id=(tx, ty, c), device_id_type=MESH)

        def over_d2d(a, k, half):
            tx, ty = chips[k]
            ref = part(outs[a].at[2 * tx + ty], a, half)
            return pltpu.make_async_remote_copy(
                src_ref=ref, dst_ref=ref, send_sem=dsend.at[3 * a + k], recv_sem=drecv.at[3 * a + k],
                device_id=(x, y, 1 - c), device_id_type=MESH)

        def local(a):
            return pltpu.make_async_copy(ins[a], outs[a].at[2 * x + y], loc.at[a])

        pairs = [(a, k) for a in range(n) for k in range(3)]
        first = ([lambda a=a: local(a).start() for a in range(n)]
                 + [lambda a=a, k=k: over_ici(a, k, (x, y)).start() for a, k in pairs],
                 [lambda a=a, k=k: over_ici(a, k, chips[k]).wait_recv() for a, k in pairs]
                 + [lambda a=a, k=k: over_ici(a, k, (x, y)).wait_send() for a, k in pairs]
                 + [lambda a=a: local(a).wait() for a in range(n)])
        pairs2 = [(a, k) for a in two for k in range(3)]
        second = ([lambda a=a, k=k: over_d2d(a, k, c).start() for a, k in pairs2],
                  [lambda a=a, k=k: over_d2d(a, k, 1 - c).wait_recv() for a, k in pairs2]
                  + [lambda a=a, k=k: over_d2d(a, k, c).wait_send() for a, k in pairs2])
        return [first, second] if two else [first]

    return _Comm(tuple(shards), tuple(jax.ShapeDtypeStruct((4,) + s.shape, s.dtype) for s in shards),
                 _dma_sems(3 * n, 3 * n, 3 * n, 3 * n, n), 2 if two else 1, plan)


def _swap_halves(grads):
    n = len(grads)

    def plan(ins, outs, sems):
        send, recv = sems
        x, y, c = _place()

        def swap(a):
            return pltpu.make_async_remote_copy(
                src_ref=ins[a].at[1 - c], dst_ref=outs[a], send_sem=send.at[a], recv_sem=recv.at[a],
                device_id=(x, y, 1 - c), device_id_type=MESH)

        return [([lambda a=a: swap(a).start() for a in range(n)], [lambda a=a: swap(a).wait() for a in range(n)])]

    return _Comm(tuple(grads), tuple(jax.ShapeDtypeStruct(g.shape[1:], g.dtype) for g in grads), _dma_sems(n, n), 1, plan)


def _scatter_shards(parts):
    n = len(parts)

    def plan(ins, outs, sems):
        send, recv = sems
        x, y, c = _place()
        chips = _other_chips(x, y)

        def scatter(a, k):
            tx, ty = chips[k]
            return pltpu.make_async_remote_copy(
                src_ref=ins[a].at[2 * tx + ty], dst_ref=outs[a].at[k], send_sem=send.at[3 * a + k],
                recv_sem=recv.at[3 * a + k], device_id=(tx, ty, c), device_id_type=MESH)

        pairs = [(a, k) for a in range(n) for k in range(3)]
        return [([lambda a=a, k=k: scatter(a, k).start() for a, k in pairs],
                 [lambda a=a, k=k: scatter(a, k).wait() for a, k in pairs])]

    return _Comm(tuple(parts), tuple(jax.ShapeDtypeStruct((3,) + p.shape[1:], p.dtype) for p in parts),
                 _dma_sems(3 * n, 3 * n), 1, plan)


def _join_halves(halves):
    n = len(halves)

    def plan(ins, outs, sems):
        send, recv, loc = sems
        x, y, c = _place()

        def remote(a, half):
            return pltpu.make_async_remote_copy(
                src_ref=ins[a], dst_ref=outs[a].at[half], send_sem=send.at[a], recv_sem=recv.at[a],
                device_id=(x, y, 1 - c), device_id_type=MESH)

        def local(a):
            return pltpu.make_async_copy(ins[a], outs[a].at[c], loc.at[a])

        every = range(n)
        return [([lambda a=a: local(a).start() for a in every] + [lambda a=a: remote(a, c).start() for a in every],
                 [lambda a=a: remote(a, 1 - c).wait_recv() for a in every]
                 + [lambda a=a: remote(a, c).wait_send() for a in every] + [lambda a=a: local(a).wait() for a in every])]

    return _Comm(tuple(halves), tuple(jax.ShapeDtypeStruct((2,) + h.shape, h.dtype) for h in halves),
                 _dma_sems(n, n, n), 1, plan)


def _reduce_w_in(dwt, comm):
    rows, hw = DIN // 4, D // 2
    ci, co = len(comm.ins), len(comm.outs)

    def body(*refs):
        dw_ref, c_in, out_ref, c_out = refs[0], refs[1:1 + ci], refs[1 + ci], refs[2 + ci:2 + ci + co]
        mine, sib, tosend, rbuf, qbuf, full, send, recv, loc = refs[2 + ci + co:11 + ci + co]
        c_sem = refs[11 + ci + co:]
        x, y, c = _place()
        sibling = (x, y, 1 - c)
        (starts, waits), = comm.plan(c_in, c_out, c_sem)
        _run_phase(starts)

        def cols(ref, half):
            window = pl.ds(pl.multiple_of(half * hw, 128), hw)
            return ref.at[:, :, window] if len(ref.shape) == 3 else ref.at[:, window]

        load = pltpu.make_async_copy(cols(dw_ref, c), mine, loc.at[0])
        give = pltpu.make_async_remote_copy(src_ref=cols(dw_ref, 1 - c), dst_ref=sib, send_sem=send.at[3], recv_sem=recv.at[3],
                                            device_id=sibling, device_id_type=MESH)
        load.start()
        give.start()
        load.wait()
        give.wait()
        mine[...] = mine[...] + sib[...]
        cps = []
        for k, (tx, ty) in enumerate(_other_chips(x, y)):
            tosend[k] = _bf(mine[2 * tx + ty])
            cps.append(pltpu.make_async_remote_copy(
                src_ref=tosend.at[k], dst_ref=rbuf.at[k], send_sem=send.at[k], recv_sem=recv.at[k],
                device_id=(tx, ty, c), device_id_type=MESH))
            cps[-1].start()
        for cp in cps:
            cp.wait()
        qbuf[...] = mine[2 * x + y] + rbuf[0].astype(F32) + rbuf[1].astype(F32) + rbuf[2].astype(F32)
        keep = pltpu.make_async_copy(qbuf, cols(full, c), loc.at[1])
        pass_on = pltpu.make_async_remote_copy(src_ref=qbuf, dst_ref=cols(full, c), send_sem=send.at[4], recv_sem=recv.at[4],
                                               device_id=sibling, device_id_type=MESH)
        keep.start()
        pass_on.start()
        keep.wait()
        pass_on.wait_send()
        pltpu.make_async_remote_copy(src_ref=qbuf, dst_ref=cols(full, 1 - c), send_sem=send.at[4], recv_sem=recv.at[4],
                                     device_id=sibling, device_id_type=MESH).wait_recv()
        out_ref[...] = full[...]
        _run_phase(waits)

    outs = pl.pallas_call(
        body, name="reduce_w_in",
        in_specs=[ANY_SPEC] * (1 + ci), out_specs=[VMEM_SPEC] + [ANY_SPEC] * co,
        out_shape=[jax.ShapeDtypeStruct((rows, D), F32)] + list(comm.outs),
        scratch_shapes=[pltpu.VMEM((4, rows, hw), F32), pltpu.VMEM((4, rows, hw), F32), pltpu.VMEM((3, rows, hw), BF16),
                        pltpu.VMEM((3, rows, hw), BF16), pltpu.VMEM((rows, hw), F32), pltpu.VMEM((rows, D), F32),
                        *_dma_sems(5, 5, 2), *comm.sems],
        compiler_params=pltpu.CompilerParams(vmem_limit_bytes=48 << 20),
    )(dwt, *comm.ins)
    return outs[0], outs[1:]


def _allreduce_small(pack):
    p = pack.shape[0]

    def body(in_ref, out_ref, buf, send, recv):
        x, y, c = _place()
        me = 4 * x + 2 * y + c
        buf[me] = in_ref[...]

        def peer_of(k):
            return x ^ (k >> 2), y ^ ((k >> 1) & 1), c ^ (k & 1)

        sends = [pltpu.make_async_remote_copy(
            src_ref=in_ref, dst_ref=buf.at[me], send_sem=send.at[k - 1], recv_sem=recv.at[k - 1],
            device_id=peer_of(k), device_id_type=MESH) for k in range(1, 8)]
        for cp in sends:
            cp.start()
        for k in range(1, 8):
            px, py, pc = peer_of(k)
            pltpu.make_async_remote_copy(
                src_ref=in_ref, dst_ref=buf.at[4 * px + 2 * py + pc], send_sem=send.at[k - 1], recv_sem=recv.at[k - 1],
                device_id=(x, y, c), device_id_type=MESH).wait_recv()
        for cp in sends:
            cp.wait_send()
        acc = buf[0]
        for d in range(1, 8):
            acc = acc + buf[d]
        out_ref[...] = acc

    return pl.pallas_call(
        body, name="allreduce_small",
        in_specs=[VMEM_SPEC], out_specs=VMEM_SPEC, out_shape=jax.ShapeDtypeStruct(pack.shape, F32),
        scratch_shapes=[pltpu.VMEM((8, p, D), F32), *_dma_sems(7, 7)],
    )(pack)


GRID4 = 4


def _sum_cores(core_shard, mine, theirs):
    n = len(mine)

    def body(cs_ref, *refs):
        ms, ts, bfs, owns = refs[:n], refs[n:2 * n], refs[2 * n:3 * n], refs[3 * n:]
        keep = pl.program_id(0) == cs_ref[1]
        for a in range(n):
            acc = ms[a][0, 0] + ts[a][0]
            bfs[a][0] = _bf(acc)

            @pl.when(keep)
            def _():
                owns[a][...] = acc

    shapes = [m.shape[2:] for m in mine]
    in_specs = ([pl.BlockSpec((1, 1) + s, lambda i, cs: (cs[0], i, 0, 0)) for s in shapes]
                + [pl.BlockSpec((1,) + s, lambda i, cs: (i, 0, 0)) for s in shapes])
    out_specs = ([pl.BlockSpec((1,) + s, lambda i, cs: (i, 0, 0)) for s in shapes]
                 + [pl.BlockSpec(s, lambda i, cs: (0, 0)) for s in shapes])
    outs = pl.pallas_call(
        body, name="sum_cores",
        grid_spec=pltpu.PrefetchScalarGridSpec(num_scalar_prefetch=1, grid=(4,), in_specs=in_specs, out_specs=out_specs),
        out_shape=[jax.ShapeDtypeStruct((4,) + s, BF16) for s in shapes] + [jax.ShapeDtypeStruct(s, F32) for s in shapes],
        compiler_params=_cp(48),
    )(core_shard, *mine, *theirs)
    return outs[:n], outs[n:]


def _sum_chips(own, arrived):
    n = len(own)

    def body(*refs):
        os_, ars, outs = refs[:n], refs[n:2 * n], refs[2 * n:]
        for a in range(n):
            outs[a][...] = os_[a][...] + ars[a][0].astype(F32) + ars[a][1].astype(F32) + ars[a][2].astype(F32)

    blocks = [(o.shape[0] // GRID4, o.shape[1]) for o in own]
    return pl.pallas_call(
        body, name="sum_chips", grid=(GRID4,),
        in_specs=([pl.BlockSpec(b, lambda i: (i, 0)) for b in blocks]
                  + [pl.BlockSpec((3,) + b, lambda i: (0, i, 0)) for b in blocks]),
        out_specs=[pl.BlockSpec(b, lambda i: (i, 0)) for b in blocks],
        out_shape=[jax.ShapeDtypeStruct(o.shape, F32) for o in own],
        compiler_params=_cp(32),
    )(*own, *arrived)


def _adamw_math(w, g, m, v):
    m2 = ADAM_B1 * m + (1.0 - ADAM_B1) * g
    v2 = ADAM_B2 * v + (1.0 - ADAM_B2) * (g * g)
    m_hat = m2 / (1.0 - ADAM_B1 ** ADAM_STEP)
    v_hat = v2 / (1.0 - ADAM_B2 ** ADAM_STEP)
    return -ADAM_LR * (m_hat / (jnp.sqrt(v_hat) + ADAM_EPS) + ADAM_WD * w), m2, v2


def _adamw_big(ws, gs, ms, vs):
    n = len(ws)

    def body(*refs):
        for a in range(n):
            d, m2, v2 = _adamw_math(refs[a][...], refs[n + a][...], refs[2 * n + a][...], refs[3 * n + a][...])
            refs[4 * n + a][...] = d
            refs[5 * n + a][...] = m2
            refs[6 * n + a][...] = v2

    specs = [pl.BlockSpec((w.shape[0] // GRID4, w.shape[1]), lambda i: (i, 0)) for w in ws]
    return pl.pallas_call(
        body, name="adamw_big", grid=(GRID4,),
        in_specs=specs * 4, out_specs=specs * 3,
        out_shape=[jax.ShapeDtypeStruct(w.shape, F32) for w in ws] * 3,
        compiler_params=_cp(48),
    )(*ws, *gs, *ms, *vs)


def _adamw_small(ws, gs, ms, vs):
    n = len(ws)

    def body(*refs):
        for a in range(n):
            d, m2, v2 = _adamw_math(refs[a][...], refs[n + a][...], refs[2 * n + a][...], refs[3 * n + a][...])
            refs[4 * n + a][...] = d
            refs[5 * n + a][...] = m2
            refs[6 * n + a][...] = v2

    return pl.pallas_call(
        body, name="adamw_small",
        in_specs=[VMEM_SPEC] * (4 * n), out_specs=[VMEM_SPEC] * (3 * n),
        out_shape=[jax.ShapeDtypeStruct(w.shape, F32) for w in ws] * 3,
        compiler_params=pltpu.CompilerParams(vmem_limit_bytes=40 << 20),
    )(*ws, *gs, *ms, *vs)


def kernel(x, meta_tokens, norm_mix_w, w_in, w_gate_up, b_gate, gla_norm_w, sinks, w_out, norm_ff_w, w_ff1, w_ff2, final_norm_w, loss_target, m_meta_tokens, m_norm_mix_w, m_w_in, m_w_gate_up, m_b_gate, m_gla_norm_w, m_sinks, m_w_out, m_norm_ff_w, m_w_ff1, m_w_ff2, m_final_norm_w, v_meta_tokens, v_norm_mix_w, v_w_in, v_w_gate_up, v_b_gate, v_gla_norm_w, v_sinks, v_w_out, v_norm_ff_w, v_w_ff1, v_w_ff2, v_final_norm_w):
    xi, yi, ci = _place()
    shard = (2 * xi + yi).astype(jnp.int32).reshape(1)
    core = ci.astype(jnp.int32).reshape(1)

    small = jnp.concatenate([meta_tokens, w_gate_up[0], jnp.zeros((NM, 64), F32)], axis=1)
    wt3, g_small = _run_comm(_gather_shards([_bf(w_in[0].T), small], [True, False]), "gather_w_in")
    meta = g_small[:, :, 0:256].transpose(1, 0, 2).reshape(NM, D)
    wgu = g_small[:, :, 256:320].transpose(1, 0, 2).reshape(NM, 256)

    xs, tgt = x[0], loss_target[0]
    t = xs.shape[0]
    wfin = final_norm_w.reshape(1, D)
    metapad = jnp.concatenate([meta, jnp.zeros((TM - NM, D), F32)], axis=0)
    wgu_p = _bf(jnp.concatenate([wgu, jnp.zeros((128 - 16, 256), F32)], axis=0))
    tabs = _rope_tables(t)

    w1s, w2s = _bf(w_ff1[0]), _bf(w_ff2[0])
    proj, (g_out, w1a, w1b) = _proj_fwd(xs, metapad, norm_mix_w, wt3, tabs,
                                        _gather_shards([_bf(w_out[0]), w1s[:HK], w1s[HK:]], [True] * 3))
    (oswa, lse), (w2a, w2b) = _swa_fwd(proj, sinks, t, _gather_shards([w2s[:HK], w2s[HK:]], [True] * 2))
    (ogla, oraw, sst, bcum, dgate), _ = _gla_fwd(proj, wgu_p, b_gate, gla_norm_w, t)
    wo, w1, w2 = g_out.reshape(D, D), (w1a, w1b), (w2a, w2b)
    h1, f, a, dh2, loss, gfin = _mlp_fwd(xs, metapad, tgt, ogla, oswa, wo, norm_ff_w, w1, w2, wfin)

    da, dh2b, dh1, do, dwo, gff = _mlp_bwd(h1, a, dh2, ogla, oswa, wo, norm_ff_w, w1, w2)
    dw1, dw2 = _ffn_wgrad(f, a, da, dh2b)
    big = [dwo, dw1, dw2]
    (dgla, dlr, dwgu, dbg, dgnw), theirs = _gla_bwd(proj, oraw, sst, bcum, dgate, do, wgu_p, gla_norm_w, t,
                                                    _swap_halves(big))
    sums_bf, own = _sum_cores(jnp.concatenate([core, shard]), big, theirs)
    (dsq, dsk, dsv, dsink), arrived = _swa_bwd(proj, sinks, lse, do, t, _scatter_shards(sums_bf))
    halves = _sum_chips(own, arrived)
    (gx, gmeta, dwt, gmix), _ = _proj_bwd(xs, metapad, norm_mix_w, wt3, tabs, dgla, dsq, dsk, dsv, dlr, dh1)

    gwt_in, joined = _reduce_w_in(dwt, _join_halves(halves))
    gw_out, gw_1, gw_2 = [j.reshape((-1, j.shape[2])) for j in joined]

    tail = jnp.concatenate([dbg, dgnw, dsink, loss, jnp.zeros((1, D - 256 - 128 - 8 - 1), F32)], axis=1)
    pack = jnp.concatenate([gmeta, gmix, gff, gfin, tail, dwgu[:16].reshape(4, D)], axis=0)
    tot = _allreduce_small(pack)
    g_meta = lax.dynamic_slice_in_dim(tot[0:NM], shard[0] * 256, 256, axis=1)
    g_mix, g_ff, g_fin = tot[16:17], tot[17:18], tot[18]
    g_bg, g_gnw, g_sinks, loss_tot = tot[19:20, 0:256], tot[19:20, 256:384], tot[19:20, 384:392], tot[19, 392]
    g_wgu = lax.dynamic_slice_in_dim(tot[20:24].reshape(NM, 256), shard[0] * 64, 64, axis=1)

    bo = _adamw_big([w_out[0], w_ff1[0], w_ff2[0]], [gw_out, gw_1, gw_2], [m_w_out[0], m_w_ff1[0], m_w_ff2[0]],
                    [v_w_out[0], v_w_ff1[0], v_w_ff2[0]])

    fin2 = lambda a: a.reshape(1, D)
    sw = [meta_tokens, norm_mix_w, w_gate_up[0], b_gate, gla_norm_w, sinks, norm_ff_w, fin2(final_norm_w), w_in[0].T]
    sg = [g_meta, g_mix, g_wgu, g_bg, g_gnw, g_sinks, g_ff, fin2(g_fin), gwt_in]
    sm = [m_meta_tokens, m_norm_mix_w, m_w_gate_up[0], m_b_gate, m_gla_norm_w, m_sinks, m_norm_ff_w, fin2(m_final_norm_w),
          m_w_in[0].T]
    sv = [v_meta_tokens, v_norm_mix_w, v_w_gate_up[0], v_b_gate, v_gla_norm_w, v_sinks, v_norm_ff_w, fin2(v_final_norm_w),
          v_w_in[0].T]
    so = _adamw_small(sw, sg, sm, sv)

    def ordered(small_o, big_o):
        meta_, mix_, wgu_, bg_, gnw_, sinks_, ff_, fin_, wt_ = small_o
        w_out_, w_1_, w_2_ = big_o
        return (meta_, mix_, wt_.T[None], wgu_[None], bg_, gnw_, sinks_, w_out_[None], ff_, w_1_[None], w_2_[None],
                fin_.reshape(D))

    grads = ordered(sg, [gw_out, gw_1, gw_2])
    deltas = ordered(so[0:9], bo[0:3])
    new_m = ordered(so[9:18], bo[3:6])
    new_v = ordered(so[18:27], bo[6:9])
    return (loss_tot, gx[None], *grads, *deltas, *new_m, *new_v)
```

```python
import functools
from typing import Callable, NamedTuple

import jax
import jax.numpy as jnp
import numpy as np
from jax import lax
from jax.experimental import pallas as pl
from jax.experimental.pallas import tpu as pltpu

F32 = jnp.float32
BF16 = jnp.bfloat16

D = 1024
DFF = 4096
NM = 16
TM = 256
DK = 64
CH = 128
SB = 128
EPS = 1e-5
C_GQ, C_GK, C_GV, C_GR, C_SQ, C_SK, C_SV, C_LR, DINP = 0, 256, 512, 1024, 1536, 2048, 2176, 2304, 2432
DIN = 2320
R_LR = 1536
ROPE_THETA = 500000.0
ADAM_LR, ADAM_B1, ADAM_B2, ADAM_EPS, ADAM_WD, ADAM_STEP = 0.001, 0.9, 0.999, 1e-08, 0.01, 10
NEG = -1e30
MESH = pl.DeviceIdType.MESH
VMEM_SPEC = pl.BlockSpec(memory_space=pltpu.VMEM)
ANY_SPEC = pl.BlockSpec(memory_space=pl.ANY)
SMEM_SPEC = pl.BlockSpec(memory_space=pltpu.SMEM)


def _cp(vmem_mb, sem=("arbitrary",)):
    return pltpu.CompilerParams(dimension_semantics=sem, vmem_limit_bytes=vmem_mb << 20)


def _dot(a, b):
    return jnp.dot(a, b, preferred_element_type=F32)


def _dot_nt(a, b):
    return lax.dot_general(a, b, (((1,), (1,)), ((), ())), preferred_element_type=F32)


def _dot_tn(a, b):
    return lax.dot_general(a, b, (((0,), (0,)), ((), ())), preferred_element_type=F32)


def _bf(x):
    return x.astype(BF16)


def _dot3(m01, x):
    x1 = _bf(x)
    r1 = x - x1.astype(F32)
    x2 = _bf(r1)
    x3 = _bf(r1 - x2.astype(F32))
    return _dot(m01, x1) + _dot(m01, x2) + _dot(m01, x3)


def _rms(h):
    rs = lax.rsqrt(jnp.mean(h * h, axis=-1, keepdims=True) + EPS)
    return h * rs, rs


def _rms_bwd(dy, yhat, rs, w):
    dyh = dy * w
    return rs * (dyh - yhat * jnp.mean(dyh * yhat, axis=-1, keepdims=True))


class _Comm(NamedTuple):
    ins: tuple
    outs: tuple
    sems: tuple
    phases: int
    plan: Callable


def _run_phase(fns):
    for fn in fns:
        fn()


def _call(body, name, grid, in_specs, out_specs, out_shape, scratch, params, args, comm=None):
    if comm is None:
        outs = pl.pallas_call(body, name=name, grid=grid, in_specs=in_specs, out_specs=out_specs, out_shape=out_shape,
                              scratch_shapes=scratch, compiler_params=params)(*args)
        return outs, None
    n_in, n_out, n_scr = len(in_specs), len(out_specs), len(scratch)
    ci, co = len(comm.ins), len(comm.outs)
    last = grid[0] - 1
    marks = [0, max(1, last - max(2, (last + 1) // 6))][:comm.phases]

    def wrapped(*refs):
        own_in, c_in = refs[:n_in], refs[n_in:n_in + ci]
        refs = refs[n_in + ci:]
        own_out, c_out = refs[:n_out], refs[n_out:n_out + co]
        refs = refs[n_out + co:]
        own_scr, c_sem = refs[:n_scr], refs[n_scr:]
        i = pl.program_id(0)

        for p, mark in enumerate(marks):
            @pl.when(i == mark)
            def _():
                plan = comm.plan(c_in, c_out, c_sem)
                if p > 0:
                    _run_phase(plan[p - 1][1])
                _run_phase(plan[p][0])

        body(*own_in, *own_out, *own_scr)

        @pl.when(i == last)
        def _():
            _run_phase(comm.plan(c_in, c_out, c_sem)[-1][1])

    outs = pl.pallas_call(
        wrapped, name=name, grid=grid, in_specs=list(in_specs) + [ANY_SPEC] * ci, out_specs=list(out_specs) + [ANY_SPEC] * co,
        out_shape=list(out_shape) + list(comm.outs), scratch_shapes=list(scratch) + list(comm.sems), compiler_params=params,
    )(*args, *comm.ins)
    return outs[:n_out], outs[n_out:]


def _run_comm(comm, name):
    ci, co = len(comm.ins), len(comm.outs)

    def body(*refs):
        for starts, waits in comm.plan(refs[:ci], refs[ci:ci + co], refs[ci + co:]):
            _run_phase(starts)
            _run_phase(waits)

    return pl.pallas_call(body, name=name, in_specs=[ANY_SPEC] * ci, out_specs=[ANY_SPEC] * co, out_shape=list(comm.outs),
                          scratch_shapes=list(comm.sems))(*comm.ins)


def _join_shards(w3_ref, w_ref):
    for s in range(4):
        w_ref[(DIN // 4) * s:(DIN // 4) * (s + 1), :] = w3_ref[s]


def _proj_fwd(x, metapad, wm, wt3, tabs, comm=None):
    t = x.shape[0]
    nblk = t // TM

    def body(x_ref, mp_ref, wm_ref, w3_ref, tab_ref, proj_ref, w_ref):
        i = pl.program_id(0)

        @pl.when(i == 0)
        def _():
            _join_shards(w3_ref, w_ref)

        h = jnp.where(i == nblk, mp_ref[...], x_ref[...])
        u, _ = _rms(h)
        ub = _bf(u * wm_ref[...])
        proj_ref[:, 0:C_SQ] = _dot_nt(ub, w_ref[0:R_LR, :])
        att = _dot_nt(ub, w_ref[R_LR + 16:DIN, :])
        tab = tab_ref[...]
        proj_ref[:, C_SQ:C_SK] = _rope(att[:, 0:512], tab, 1.0) * 0.125
        proj_ref[:, C_SK:C_SV] = _rope(att[:, 512:640], tab, 1.0)
        proj_ref[:, C_SV:C_LR] = att[:, 640:768]
        proj_ref[:, C_LR:DINP] = jnp.zeros((TM, DINP - C_LR), F32)
        proj_ref[:, C_LR:C_LR + 16] = _dot_nt(ub, w_ref[R_LR:R_LR + 16, :])

    (proj,), got = _call(
        body, "proj_fwd", (nblk + 1,),
        [pl.BlockSpec((TM, D), lambda i: (jnp.minimum(i, nblk - 1), 0)), VMEM_SPEC, VMEM_SPEC, VMEM_SPEC,
         pl.BlockSpec((TM, 128), lambda i: (i, 0))],
        [pl.BlockSpec((TM, DINP), lambda i: (i, 0))], [jax.ShapeDtypeStruct((t + TM, DINP), F32)],
        [pltpu.VMEM((DIN, D), BF16)], _cp(48), (x, metapad, wm, wt3, tabs), comm)
    return proj, got


def _chunk_masks():
    r = lax.broadcasted_iota(jnp.int32, (TM, TM), 0)
    c = lax.broadcasted_iota(jnp.int32, (TM, TM), 1)
    same = (r // CH) == (c // CH)
    lower = _bf(jnp.where(same & (c <= r), 1.0, 0.0))
    upper = _bf(jnp.where(same & (c >= r), 1.0, 0.0))
    return lower, upper


def _gla_gate(lr, wgu, bg, valid, lower):
    z = _dot(_bf(lr), wgu) + bg
    g = (jnp.minimum(z, 0.0) - jnp.log(1.0 + jnp.exp(-jnp.abs(z)))) * (1.0 / 16.0)
    g = jnp.where(valid, g, 0.0)
    return z, _dot3(lower, g)


def _gla_decays(q, k, b):
    nc = TM // CH
    b3 = b.reshape(nc, CH, 256)
    blast = b3[:, CH - 1:CH, :]
    eb = jnp.exp(b)
    enb = jnp.exp(-b)
    ebl = jnp.exp(blast - b3).reshape(TM, 256)
    return eb, enb, ebl, jnp.exp(blast)


def _tri(lower_incl):
    r = lax.broadcasted_iota(jnp.int32, (CH, CH), 0)
    c = lax.broadcasted_iota(jnp.int32, (CH, CH), 1)
    return ((c <= r) if lower_incl else (c >= r))[None]


def _gla_fwd(proj, wgu, bg, gnw, t, comm=None):
    nblk = t // TM
    nt = nblk + 1
    nc = TM // CH

    def blk(i):
        return (i + nblk) % nt

    def body(q_ref, k_ref, v_ref, r_ref, lr_ref, wgu_ref, bg_ref, gnw_ref, o_ref, oraw_ref, sst_ref, b_ref, dgate_ref,
             st_scr):
        i = pl.program_id(0)

        @pl.when(i == 0)
        def _():
            st_scr[...] = jnp.zeros_like(st_scr)

        rows = blk(i) * TM + lax.broadcasted_iota(jnp.int32, (TM, 1), 0)
        lower, _ = _chunk_masks()
        valid = rows < t + NM
        z, b = _gla_gate(lr_ref[...], wgu_ref[...], bg_ref[...], valid, lower)
        b_ref[...] = b
        dgate_ref[...] = jnp.where(valid, (1.0 / 16.0) / (1.0 + jnp.exp(z)), 0.0)
        q = q_ref[...]
        k = k_ref[...]
        eb, enb, ebl, eblast = _gla_decays(q, k, b)
        qt = q * 0.125 * eb
        kt = k * enb
        kh = k * ebl
        tril = _tri(True)
        heads = range(4)
        hs = [slice(h * DK, (h + 1) * DK) for h in heads]
        qh = [_bf(qt[:, hs[h]]).reshape(nc, CH, DK) for h in heads]
        kth = [_bf(kt[:, hs[h]]).reshape(nc, CH, DK) for h in heads]
        khh = [_bf(kh[:, hs[h]]).reshape(nc, CH, DK) for h in heads]
        vh = [_bf(v_ref[:, h * 128:(h + 1) * 128]).reshape(nc, CH, 128) for h in heads]
        a = [jnp.einsum('cid,cjd->cij', qh[h], kth[h], preferred_element_type=F32) for h in heads]
        kv = [jnp.einsum('cjv,cjd->cvd', vh[h], khh[h], preferred_element_type=F32) for h in heads]
        o = [jnp.einsum('cij,cjv->civ', _bf(jnp.where(tril, a[h], 0.0)), vh[h], preferred_element_type=F32) for h in heads]
        states = []
        for h in heads:
            st = st_scr[h]
            per_chunk = []
            for c in range(nc):
                sst_ref[c, h] = st
                per_chunk.append(_bf(st))
                st = st * eblast[c, :, hs[h]] + kv[h][c]
            st_scr[h] = st
            states.append(per_chunk)
        o_inter = [[_dot_nt(qh[h][c], states[h][c]) for c in range(nc)] for h in heads]
        oraw = jnp.concatenate([(o[h] + jnp.stack(o_inter[h])).reshape(TM, 128) for h in heads], axis=1)
        oraw_ref[...] = oraw
        gn = gnw_ref[...]
        res = []
        for h in range(4):
            on, _ = _rms(oraw[:, h * 128:(h + 1) * 128])
            r = r_ref[:, h * 128:(h + 1) * 128]
            res.append(on * gn * (r * jax.nn.sigmoid(r)))
        o_ref[...] = _bf(jnp.concatenate(res, axis=1))

    def spec(w, cb):
        return pl.BlockSpec((TM, w), lambda i: (blk(i), cb))

    return _call(
        body, "gla_fwd", (nt,),
        [spec(256, 0), spec(256, 1), spec(512, 1), spec(512, 2), spec(128, C_LR // 128), VMEM_SPEC, VMEM_SPEC, VMEM_SPEC],
        [spec(512, 0), spec(512, 0), pl.BlockSpec((nc, 4, 128, DK), lambda i: (blk(i), 0, 0, 0)), spec(256, 0), spec(256, 0)],
        [jax.ShapeDtypeStruct((t + TM, 512), BF16), jax.ShapeDtypeStruct((t + TM, 512), F32),
         jax.ShapeDtypeStruct((nt * nc, 4, 128, DK), F32), jax.ShapeDtypeStruct((t + TM, 256), F32),
         jax.ShapeDtypeStruct((t + TM, 256), F32)],
        [pltpu.VMEM((4, 128, DK), F32)], _cp(40), (proj, proj, proj, proj, proj, wgu, bg, gnw), comm)


def _gla_bwd(proj, oraw, sst, bcum, dgate, do, wgu, gnw, t, comm=None):
    nblk = t // TM
    nt = nblk + 1
    nc = TM // CH

    def blk(i):
        return (2 * nblk - i) % nt

    def body(q_ref, k_ref, v_ref, r_ref, lr_ref, oraw_ref, sst_ref, b_ref, dgate_ref, do_ref, wgu_ref, gnw_ref,
             dgla_ref, dlr_ref, dwgu_ref, dbg_ref, dgnw_ref, dst_scr):
        i = pl.program_id(0)

        @pl.when(i == 0)
        def _():
            dst_scr[...] = jnp.zeros_like(dst_scr)
            dwgu_ref[...] = jnp.zeros_like(dwgu_ref)
            dbg_ref[...] = jnp.zeros_like(dbg_ref)
            dgnw_ref[...] = jnp.zeros_like(dgnw_ref)

        _, upper = _chunk_masks()
        lr = lr_ref[...]
        b = b_ref[...]
        q = q_ref[...]
        k = k_ref[...]
        eb, enb, ebl, eblast = _gla_decays(q, k, b)
        qt = q * 0.125 * eb
        kt = k * enb
        kh = k * ebl
        gn = gnw_ref[...]
        tril = _tri(True)
        triu = _tri(False)
        heads = range(4)
        hs = [slice(h * DK, (h + 1) * DK) for h in heads]
        vs = [slice(h * 128, (h + 1) * 128) for h in heads]
        ein = functools.partial(jnp.einsum, preferred_element_type=F32)
        dr_l, doh = [], []
        dgn = jnp.zeros((1, 128), F32)
        for h in heads:
            on, rs = _rms(oraw_ref[:, vs[h]])
            r = r_ref[:, vs[h]]
            sig = jax.nn.sigmoid(r)
            sil = r * sig
            dy = do_ref[:, vs[h]]
            dr_l.append(dy * on * gn * (sig * (1.0 + r * (1.0 - sig))))
            dgn = dgn + jnp.sum(dy * sil * on, axis=0, keepdims=True)
            doh.append(_bf(_rms_bwd(dy * sil, on, rs, gn)).reshape(nc, CH, 128))
        dgnw_ref[...] += dgn
        qh = [_bf(qt[:, hs[h]]).reshape(nc, CH, DK) for h in heads]
        kth = [_bf(kt[:, hs[h]]).reshape(nc, CH, DK) for h in heads]
        khh = [_bf(kh[:, hs[h]]).reshape(nc, CH, DK) for h in heads]
        vh = [_bf(v_ref[:, vs[h]]).reshape(nc, CH, 128) for h in heads]
        at = [ein('cjd,cid->cji', kth[h], qh[h]) for h in heads]
        da = [ein('civ,cjv->cij', doh[h], vh[h]) for h in heads]
        dat = [ein('cjv,civ->cji', vh[h], doh[h]) for h in heads]
        gq = [ein('civ,cid->cvd', doh[h], qh[h]) for h in heads]
        stf = [sst_ref[:, h] for h in heads]
        dqs = [ein('civ,cvd->cid', doh[h], _bf(stf[h])) for h in heads]
        dv = [ein('cji,civ->cjv', _bf(jnp.where(triu, at[h], 0.0)), doh[h]) for h in heads]
        dqt = [ein('cij,cjd->cid', _bf(jnp.where(tril, da[h], 0.0)), kth[h]) + dqs[h] for h in heads]
        dkt = [ein('cji,cid->cjd', _bf(jnp.where(triu, dat[h], 0.0)), qh[h]) for h in heads]
        dse = []
        for h in heads:
            dst = dst_scr[h]
            dsend = [None] * nc
            for c in reversed(range(nc)):
                dsend[c] = dst
                dst = dst * eblast[c, :, hs[h]] + gq[h][c]
            dst_scr[h] = dst
            dse.append(jnp.stack(dsend))
        dseb = [_bf(d) for d in dse]
        dv = [dv[h] + ein('cjd,cvd->cjv', khh[h], dseb[h]) for h in heads]
        dkh = [ein('cjv,cvd->cjd', vh[h], dseb[h]) for h in heads]
        carried = jnp.concatenate([jnp.sum(dse[h] * stf[h], axis=1, keepdims=True) for h in heads], axis=2)
        wide = lambda parts: jnp.concatenate([p.reshape(TM, DK) for p in parts], axis=1)
        dqt_w, dkt_w, dkh_w = wide(dqt), wide(dkt), wide(dkh)
        dkh_kh = dkh_w * kh
        extra = jnp.sum(dkh_kh.reshape(nc, CH, 256), axis=1, keepdims=True) + eblast * carried
        db = dqt_w * qt - dkt_w * kt - dkh_kh
        dg = _dot3(upper, db) + jnp.broadcast_to(extra, (nc, CH, 256)).reshape(TM, 256)
        dz = dg * dgate_ref[...]
        dzb = _bf(dz)
        dlr_ref[...] = _bf(_dot_nt(dzb, wgu_ref[...]))
        dwgu_ref[...] += _dot_tn(_bf(lr), dzb)
        dbg_ref[...] += jnp.sum(dz, axis=0, keepdims=True)
        dq = dqt_w * eb * 0.125
        dk = dkt_w * enb + dkh_w * ebl
        dgla_ref[...] = _bf(jnp.concatenate([dq, dk] + [d.reshape(TM, 128) for d in dv] + dr_l, axis=1))

    def spec(w, cb):
        return pl.BlockSpec((TM, w), lambda i: (blk(i), cb))

    def acc(shape):
        return pl.BlockSpec(shape, lambda i: (0, 0))

    return _call(
        body, "gla_bwd", (nt,),
        [spec(256, 0), spec(256, 1), spec(512, 1), spec(512, 2), spec(128, C_LR // 128), spec(512, 0),
         pl.BlockSpec((nc, 4, 128, DK), lambda i: (blk(i), 0, 0, 0)), spec(256, 0), spec(256, 0), spec(512, 0),
         VMEM_SPEC, VMEM_SPEC],
        [spec(1536, 0), spec(128, 0), acc((128, 256)), acc((1, 256)), acc((1, 128))],
        [jax.ShapeDtypeStruct((t + TM, 1536), BF16), jax.ShapeDtypeStruct((t + TM, 128), BF16),
         jax.ShapeDtypeStruct((128, 256), F32), jax.ShapeDtypeStruct((1, 256), F32), jax.ShapeDtypeStruct((1, 128), F32)],
        [pltpu.VMEM((4, 128, DK), F32)], _cp(48), (proj, proj, proj, proj, proj, oraw, sst, bcum, dgate, do, wgu, gnw), comm)


def _rope_tables(t):
    r = t + TM
    row = np.arange(r)
    pos = np.where(row < t, row + NM, np.where(row < t + NM, row - t, 0)).astype(np.float32)
    inv_freq = (1.0 / (np.float32(ROPE_THETA) ** (np.arange(0, 16, 2, dtype=np.float32) / np.float32(16)))).astype(np.float32)
    ang = (pos[:, None] * inv_freq[None, :]).astype(np.float32)
    cos, sin = np.cos(ang).astype(np.float32), np.sin(ang).astype(np.float32)
    one, zero = np.ones((r, 48), np.float32), np.zeros((r, 48), np.float32)
    return jnp.asarray(np.concatenate([cos, cos, one, -sin, sin, zero], axis=1))


def _rope(x, tab, sign):
    w = x.shape[1]
    rep = w // 64
    c = jnp.concatenate([tab[:, 0:64]] * rep, axis=1)
    s = jnp.concatenate([tab[:, 64:128]] * rep, axis=1)
    lane = lax.rem(lax.broadcasted_iota(jnp.int32, x.shape, 1), 64)
    partner = jnp.where(lane < 8, pltpu.roll(x, w - 8, 1), jnp.where(lane < 16, pltpu.roll(x, 8, 1), 0.0))
    return x * c + sign * (partner * s)


HB_BWD = 4


def _stack(x, hg):
    w = x.shape[1] // hg
    return x if hg == 1 else jnp.concatenate([x[:, g * w:(g + 1) * w] for g in range(hg)], axis=0)


def _unstack(x, hg):
    return x if hg == 1 else jnp.concatenate([x[g * SB:(g + 1) * SB] for g in range(hg)], axis=1)


def _swa_steps(r_tot):
    blocks = r_tot // SB
    return next(n for n in (6, 3, 2) if blocks % n == 0 and blocks // n >= 2)


def _swa_specs(nsb, nbq):
    def rows(h, w, cb, f):
        return pl.BlockSpec((h, w), lambda i: (f(i), cb))
    pair = lambda i: i
    prev = lambda i: jnp.maximum(nbq * i - 1, 0)
    meta = lambda i: nsb
    return rows, pair, prev, meta


def _swa_fwd(proj, sinks, t, comm=None):
    nsb = t // SB
    r_tot = t + TM
    nbq = _swa_steps(r_tot)
    qb = nbq * SB
    rows, pair, prev, meta = _swa_specs(nsb, nbq)

    def body(sink_ref, q_ref, kc_ref, kp_ref, km_ref, vc_ref, vp_ref, vm_ref, o_ref, lse_ref):
        i = pl.program_id(0)
        key = lax.broadcasted_iota(jnp.int32, (SB, SB), 0)
        qry = lax.broadcasted_iota(jnp.int32, (SB, SB), 1)
        km, vm = km_ref[0:NM, :], vm_ref[0:NM, :]
        for j in range(nbq):
            b = nbq * i + j
            rs = slice(j * SB, (j + 1) * SB)
            before = slice((j - 1) * SB, j * SB)
            real = b < nsb
            masks = (key <= qry, (key > qry) & (b > 0) & real, real)
            k3 = (kc_ref[rs, :], kp_ref[...] if j == 0 else kc_ref[before, :], km)
            v3 = (vc_ref[rs, :], vp_ref[...] if j == 0 else vc_ref[before, :], vm)
            valid = b * SB + lax.broadcasted_iota(jnp.int32, (1, SB), 1) < t + NM
            heads = range(8)
            kb = [[_bf(k[:, kv * 64:(kv + 1) * 64]) for k in k3] for kv in range(2)]
            vt = [[_bf(v[:, kv * 64:(kv + 1) * 64].T) for v in v3] for kv in range(2)]
            raw = [[_dot_nt(k, _bf(q_ref[rs, h * 64:(h + 1) * 64])) for k in kb[h // 4]] for h in heads]
            probs, inv_l, lse_l = [], [], []
            for h in heads:
                s = [jnp.where(m, sx, NEG) for m, sx in zip(masks, raw[h])]
                sink = sink_ref[0, h]
                top = jnp.maximum(jnp.max(jnp.maximum(s[0], s[1]), axis=0, keepdims=True),
                                  jnp.maximum(jnp.max(s[2], axis=0, keepdims=True), sink))
                p = [jnp.exp(sx - top) for sx in s]
                l = (jnp.sum(p[0] + p[1], axis=0, keepdims=True) + jnp.sum(p[2], axis=0, keepdims=True)
                     + jnp.exp(sink - top))
                probs.append([_bf(px) for px in p])
                inv_l.append(1.0 / l)
                lse_l.append(top + jnp.log(l))
            o_t = [_dot(vt[h // 4][0], probs[h][0]) + _dot(vt[h // 4][1], probs[h][1]) + _dot(vt[h // 4][2], probs[h][2])
                   for h in heads]
            o_ref[rs, :] = _bf(jnp.concatenate([jnp.where(valid, o_t[h] * inv_l[h], 0.0).T for h in heads], axis=1))
            lse_ref[:, rs] = jnp.concatenate(lse_l, axis=0)

    ck, cv = C_SK // 128, C_SV // 128
    return _call(
        body, "swa_fwd", (r_tot // qb,),
        [SMEM_SPEC, rows(qb, 512, C_SQ // 512, pair),
         rows(qb, 128, ck, pair), rows(SB, 128, ck, prev), rows(SB, 128, ck, meta),
         rows(qb, 128, cv, pair), rows(SB, 128, cv, prev), rows(SB, 128, cv, meta)],
        [rows(qb, 512, 0, pair), pl.BlockSpec((8, qb), lambda i: (0, i))],
        [jax.ShapeDtypeStruct((r_tot, 512), BF16), jax.ShapeDtypeStruct((8, r_tot), F32)],
        [], _cp(32), (sinks, proj, proj, proj, proj, proj, proj, proj), comm)


def _swa_bwd(proj, sinks, lse_t, do, t, comm=None):
    nsb = t // SB
    r_tot = t + TM
    nbq = _swa_steps(r_tot)
    qb = nbq * SB
    rows, pair, prev, meta = _swa_specs(nsb, nbq)
    hb = HB_BWD
    lanes = hb * SB

    def body(sink_ref, q_ref, kc_ref, kp_ref, km_ref, vc_ref, vp_ref, vm_ref, lse_ref, do_ref,
             dq_ref, dk_ref, dv_ref, dsink_ref):
        i = pl.program_id(0)

        @pl.when(i == 0)
        def _():
            dk_ref[...] = jnp.zeros_like(dk_ref)
            dv_ref[...] = jnp.zeros_like(dv_ref)
            dsink_ref[...] = jnp.zeros_like(dsink_ref)

        key = lax.broadcasted_iota(jnp.int32, (SB, lanes), 0)
        qry = lax.rem(lax.broadcasted_iota(jnp.int32, (SB, lanes), 1), SB)
        km, vm = km_ref[0:NM, :], vm_ref[0:NM, :]
        dsink_l = []
        for j in range(nbq):
            b = nbq * i + j
            rs = slice(j * SB, (j + 1) * SB)
            before = slice((j - 1) * SB, j * SB)
            real = b < nsb
            masks = (key <= qry, (key > qry) & (b > 0) & real, real)
            k3 = (kc_ref[rs, :], kp_ref[...] if j == 0 else kc_ref[before, :], km)
            v3 = (vc_ref[rs, :], vp_ref[...] if j == 0 else vc_ref[before, :], vm)
            groups = list(range(0, 8, hb))
            kvs = [h0 // 4 for h0 in groups]
            qg = [_bf(_stack(q_ref[rs, h0 * 64:(h0 + hb) * 64], hb)) for h0 in groups]
            dog = [_bf(_stack(do_ref[rs, h0 * 64:(h0 + hb) * 64], hb)) for h0 in groups]
            kb = [[_bf(k[:, kv * 64:(kv + 1) * 64]) for k in k3] for kv in kvs]
            vb = [[_bf(v[:, kv * 64:(kv + 1) * 64]) for v in v3] for kv in kvs]
            s = [[_dot_nt(k, qg[g]) for k in kb[g]] for g in range(len(groups))]
            dp = [[_dot_nt(v, dog[g]) for v in vb[g]] for g in range(len(groups))]
            p, ds, ds_blk = [], [], []
            for g, h0 in enumerate(groups):
                lse_row = jnp.concatenate([lse_ref[h:h + 1, rs] for h in range(h0, h0 + hb)], axis=1)
                sink_row = jnp.concatenate([jnp.full((1, SB), sink_ref[0, h], F32) for h in range(h0, h0 + hb)], axis=1)
                pg = [jnp.exp(jnp.where(m, sx, NEG) - lse_row) for m, sx in zip(masks, s[g])]
                delta = (jnp.sum(pg[0] * dp[g][0] + pg[1] * dp[g][1], axis=0, keepdims=True)
                         + jnp.sum(pg[2] * dp[g][2], axis=0, keepdims=True))
                ds.append([_bf(pp * (dd - delta)) for pp, dd in zip(pg, dp[g])])
                p.append([_bf(pp) for pp in pg])
                ds_row = -jnp.exp(sink_row - lse_row) * delta
                ds_blk += [jnp.sum(ds_row[:, q0 * SB:(q0 + 1) * SB], axis=1, keepdims=True) for q0 in range(hb)]
            dsink_l.append(jnp.concatenate(ds_blk, axis=1))
            dq_t = [_dot_tn(kb[g][0], ds[g][0]) + _dot_tn(kb[g][1], ds[g][1]) + _dot_tn(kb[g][2], ds[g][2])
                    for g in range(len(groups))]
            dq_ref[rs, :] = jnp.concatenate([_unstack(d.T, hb) for d in dq_t], axis=1)
            windows = (pl.ds(pl.multiple_of(b * SB, SB), SB), pl.ds(pl.multiple_of(jnp.maximum(b - 1, 0) * SB, SB), SB),
                       pl.ds(t, NM))
            for x in range(3):
                dk_kv, dv_kv = [], []
                for kv in range(2):
                    mine = [g for g in range(len(groups)) if kvs[g] == kv]
                    dk_kv.append(sum(_dot(ds[g][x], qg[g]) for g in mine))
                    dv_kv.append(sum(_dot(p[g][x], dog[g]) for g in mine))
                dk_ref[windows[x], :] += jnp.concatenate(dk_kv, axis=1)
                dv_ref[windows[x], :] += jnp.concatenate(dv_kv, axis=1)
        dsink_ref[...] += sum(dsink_l)

    ck, cv = C_SK // 128, C_SV // 128
    whole = lambda w: pl.BlockSpec((r_tot, w), lambda i: (0, 0))
    return _call(
        body, "swa_bwd", (r_tot // qb,),
        [SMEM_SPEC, rows(qb, 512, C_SQ // 512, pair),
         rows(qb, 128, ck, pair), rows(SB, 128, ck, prev), rows(SB, 128, ck, meta),
         rows(qb, 128, cv, pair), rows(SB, 128, cv, prev), rows(SB, 128, cv, meta),
         pl.BlockSpec((8, qb), lambda i: (0, i)), rows(qb, 512, 1, pair)],
        [rows(qb, 512, 0, pair), whole(128), whole(128), pl.BlockSpec((1, 8), lambda i: (0, 0))],
        [jax.ShapeDtypeStruct((r_tot, 512), F32), jax.ShapeDtypeStruct((r_tot, 128), F32),
         jax.ShapeDtypeStruct((r_tot, 128), F32), jax.ShapeDtypeStruct((1, 8), F32)],
        [], _cp(48), (sinks, proj, proj, proj, proj, proj, proj, proj, lse_t, do), comm)


HK = D // 2


def _mlp_fwd(x, metapad, tgt, ogla, oswa, wo, wff, w1, w2, wfin):
    t = x.shape[0]
    nblk = t // TM

    def body(x_ref, mp_ref, tgt_ref, og_ref, os_ref, wo_ref, wff_ref, w1a_ref, w1b_ref, w2a_ref, w2b_ref, wfin_ref,
             h1_ref, f_ref, a_ref, dh2_ref, loss_ref, gfin_ref):
        i = pl.program_id(0)

        @pl.when(i == 0)
        def _():
            loss_ref[...] = jnp.zeros_like(loss_ref)
            gfin_ref[...] = jnp.zeros_like(gfin_ref)

        h0 = jnp.where(i == nblk, mp_ref[...], x_ref[...])
        h1 = h0 + _dot(og_ref[...], wo_ref[0:512, :]) + _dot(os_ref[...], wo_ref[512:1024, :])
        h1_ref[...] = h1
        fh, _ = _rms(h1)
        f = _bf(fh * wff_ref[...])
        f_ref[...] = f
        acc = jnp.zeros((TM, D), F32)
        for n in range(4):
            a = _dot(f[:, 0:HK], w1a_ref[n]) + _dot(f[:, HK:D], w1b_ref[n])
            a_ref[:, n * D:(n + 1) * D] = _bf(a)
            zr = jnp.maximum(a, 0.0)
            z = _bf(zr * zr)
            acc = acc + _dot(z[:, 0:HK], w2a_ref[n]) + _dot(z[:, HK:D], w2b_ref[n])
        h2 = h1 + acc
        yh, rs2 = _rms(h2)
        wf = wfin_ref[...]
        real = i < nblk
        e = jnp.where(real, yh * wf - tgt_ref[...], 0.0)
        loss_ref[...] += jnp.sum(jnp.sum(e * e, axis=0, keepdims=True), axis=1, keepdims=True) * (0.5 / D)
        dy = e * (1.0 / D)
        gfin_ref[...] += jnp.sum(dy * yh, axis=0, keepdims=True)
        dh2_ref[...] = _rms_bwd(dy, yh, rs2, wf)

    xs = pl.BlockSpec((TM, D), lambda i: (jnp.minimum(i, nblk - 1), 0))
    rs = lambda w: pl.BlockSpec((TM, w), lambda i: (i, 0))
    r_tot = t + TM
    return pl.pallas_call(
        body, name="mlp_fwd", grid=(nblk + 1,),
        in_specs=[xs, VMEM_SPEC, xs, rs(512), rs(512)] + [VMEM_SPEC] * 7,
        out_specs=[rs(D), rs(D), rs(DFF), rs(D), pl.BlockSpec((1, 1), lambda i: (0, 0)), pl.BlockSpec((1, D), lambda i: (0, 0))],
        out_shape=[jax.ShapeDtypeStruct((r_tot, D), F32), jax.ShapeDtypeStruct((r_tot, D), BF16),
                   jax.ShapeDtypeStruct((r_tot, DFF), BF16), jax.ShapeDtypeStruct((r_tot, D), F32),
                   jax.ShapeDtypeStruct((1, 1), F32), jax.ShapeDtypeStruct((1, D), F32)],
        compiler_params=_cp(56),
    )(x, metapad, tgt, ogla, oswa, wo, wff, *w1, *w2, wfin)


def _mlp_bwd(h1, a, dh2, ogla, oswa, wo, wff, w1, w2):
    r_tot = h1.shape[0]
    nt = r_tot // TM

    def body(h1_ref, a_ref, dh2_ref, og_ref, os_ref, wo_ref, wff_ref, w1a_ref, w1b_ref, w2a_ref, w2b_ref,
             da_ref, dh2b_ref, dh1_ref, do_ref, dwo_ref, gff_ref, dwo_acc):
        i = pl.program_id(0)

        @pl.when(i == 0)
        def _():
            dwo_acc[...] = jnp.zeros_like(dwo_acc)
            gff_ref[...] = jnp.zeros_like(gff_ref)

        dh2 = dh2_ref[...]
        dh2b = _bf(dh2)
        dh2b_ref[...] = dh2b
        dfa = jnp.zeros((TM, HK), F32)
        dfb = jnp.zeros((TM, HK), F32)
        for n in range(4):
            dz = jnp.concatenate([_dot_nt(dh2b, w2a_ref[n]), _dot_nt(dh2b, w2b_ref[n])], axis=1)
            da = _bf(dz * (2.0 * jnp.maximum(a_ref[:, n * D:(n + 1) * D].astype(F32), 0.0)))
            da_ref[:, n * D:(n + 1) * D] = da
            dfa = dfa + _dot_nt(da, w1a_ref[n])
            dfb = dfb + _dot_nt(da, w1b_ref[n])
        df = jnp.concatenate([dfa, dfb], axis=1)
        fh, rs1 = _rms(h1_ref[...])
        gff_ref[...] += jnp.sum(df * fh, axis=0, keepdims=True)
        dh1 = dh2 + _rms_bwd(df, fh, rs1, wff_ref[...])
        dh1_ref[...] = dh1
        dh1b = _bf(dh1)
        do_ref[...] = _dot_nt(dh1b, wo_ref[...])
        dwo_acc[0:512, :] += _dot_tn(og_ref[...], dh1b)
        dwo_acc[512:1024, :] += _dot_tn(os_ref[...], dh1b)

        @pl.when(i == nt - 1)
        def _():
            for s in range(4):
                for hh in range(2):
                    dwo_ref[hh, s] = dwo_acc[(2 * s + hh) * 128:(2 * s + hh + 1) * 128, :]

    rs = lambda w: pl.BlockSpec((TM, w), lambda i: (i, 0))
    return pl.pallas_call(
        body, name="mlp_bwd", grid=(nt,),
        in_specs=[rs(D), rs(DFF), rs(D), rs(512), rs(512)] + [VMEM_SPEC] * 6,
        out_specs=[rs(DFF), rs(D), rs(D), rs(D), VMEM_SPEC, pl.BlockSpec((1, D), lambda i: (0, 0))],
        out_shape=[jax.ShapeDtypeStruct((r_tot, DFF), BF16), jax.ShapeDtypeStruct((r_tot, D), BF16),
                   jax.ShapeDtypeStruct((r_tot, D), F32), jax.ShapeDtypeStruct((r_tot, D), F32),
                   jax.ShapeDtypeStruct((2, 4, 128, D), F32), jax.ShapeDtypeStruct((1, D), F32)],
        scratch_shapes=[pltpu.VMEM((D, D), F32)],
        compiler_params=_cp(56),
    )(h1, a, dh2, ogla, oswa, wo, wff, *w1, *w2)


def _ffn_wgrad(f, a, da, dh2b):
    r_tot = f.shape[0]
    kt = 768 if r_tot % 768 == 0 else TM
    nk = r_tot // kt

    def body(f_ref, a_ref, da_ref, dh2_ref, dw1_ref, dw2_ref, acc1, acc2):
        k = pl.program_id(1)

        @pl.when(k == 0)
        def _():
            acc1[...] = jnp.zeros_like(acc1)
            acc2[...] = jnp.zeros_like(acc2)

        zr = jnp.maximum(a_ref[...], 0.0)
        acc1[...] += _dot_tn(f_ref[...], da_ref[...])
        acc2[...] += _dot_tn(zr * zr, dh2_ref[...])

        @pl.when(k == nk - 1)
        def _():
            for hh in range(2):
                dw1_ref[hh, 0] = acc1[hh * 512:(hh + 1) * 512, :]
                dw2_ref[hh, 0] = acc2[hh * 512:(hh + 1) * 512, :]

    out = pl.BlockSpec((2, 1, 512, D), lambda n, k: (0, n, 0, 0))
    return pl.pallas_call(
        body, name="ffn_wgrad", grid=(4, nk),
        in_specs=[pl.BlockSpec((kt, D), lambda n, k: (k, 0)), pl.BlockSpec((kt, D), lambda n, k: (k, n)),
                  pl.BlockSpec((kt, D), lambda n, k: (k, n)), pl.BlockSpec((kt, D), lambda n, k: (k, 0))],
        out_specs=[out, out],
        out_shape=[jax.ShapeDtypeStruct((2, 4, 512, D), F32)] * 2,
        scratch_shapes=[pltpu.VMEM((D, D), F32), pltpu.VMEM((D, D), F32)],
        compiler_params=_cp(48, ("arbitrary", "arbitrary")),
    )(f, a, da, dh2b)


def _proj_bwd(x, metapad, wm, wt3, tabs, dgla, dswa_q, dsk, dsv, dlr, dh1, comm=None):
    t = x.shape[0]
    nblk = t // TM

    def body(x_ref, mp_ref, wm_ref, w3_ref, tab_ref, dg_ref, dq_ref, dk_ref, dv_ref, dlr_ref, dh1_ref,
             gx_ref, gmeta_ref, dw_ref, gmix_ref, w_ref, acc):
        i = pl.program_id(0)

        @pl.when(i == 0)
        def _():
            _join_shards(w3_ref, w_ref)
            acc[...] = jnp.zeros_like(acc)
            gmix_ref[...] = jnp.zeros_like(gmix_ref)

        h = jnp.where(i == nblk, mp_ref[...], x_ref[...])
        uh, rs = _rms(h)
        wm_v = wm_ref[...]
        u = _bf(uh * wm_v)
        tab = tab_ref[...]
        dq = _bf(_rope(dq_ref[...] * 0.125, tab, -1.0))
        dk = _bf(_rope(dk_ref[...], tab, -1.0))
        parts = ((dg_ref[...], 0, R_LR), (dlr_ref[:, 0:16], R_LR, 16), (dq, R_LR + 16, 512),
                 (dk, R_LR + 528, 128), (_bf(dv_ref[...]), R_LR + 656, 128))
        du = jnp.zeros((TM, D), F32)
        for val, r0, w in parts:
            du = du + _dot(val, w_ref[r0:r0 + w, :])
            acc[r0:r0 + w, :] += _dot_tn(val, u)
        gmix_ref[...] += jnp.sum(du * uh, axis=0, keepdims=True)
        dh0 = dh1_ref[...] + _rms_bwd(du, uh, rs, wm_v)

        @pl.when(i < nblk)
        def _():
            gx_ref[...] = dh0

        @pl.when(i == nblk)
        def _():
            gmeta_ref[...] = dh0[:NM]
            for s in range(4):
                dw_ref[s] = acc[(DIN // 4) * s:(DIN // 4) * (s + 1), :]

    xs = pl.BlockSpec((TM, D), lambda i: (jnp.minimum(i, nblk - 1), 0))
    rs_ = lambda w: pl.BlockSpec((TM, w), lambda i: (i, 0))
    return _call(
        body, "proj_bwd", (nblk + 1,),
        [xs, VMEM_SPEC, VMEM_SPEC, VMEM_SPEC, rs_(128), rs_(1536), rs_(512), rs_(128), rs_(128), rs_(128), rs_(D)],
        [xs, pl.BlockSpec((NM, D), lambda i: (0, 0)), VMEM_SPEC, pl.BlockSpec((1, D), lambda i: (0, 0))],
        [jax.ShapeDtypeStruct((t, D), F32), jax.ShapeDtypeStruct((NM, D), F32),
         jax.ShapeDtypeStruct((4, DIN // 4, D), F32), jax.ShapeDtypeStruct((1, D), F32)],
        [pltpu.VMEM((DIN, D), BF16), pltpu.VMEM((DIN, D), F32)], _cp(56),
        (x, metapad, wm, wt3, tabs, dgla, dswa_q, dsk, dsv, dlr, dh1), comm)


def _place():
    return lax.axis_index("x"), lax.axis_index("y"), lax.axis_index("c")


def _other_chips(x, y):
    return [(1 - x, y), (x, 1 - y), (1 - x, 1 - y)]


def _dma_sems(*counts):
    return tuple(pltpu.SemaphoreType.DMA((k,)) for k in counts)


def _gather_shards(shards, split):
    n = len(shards)
    two = [a for a in range(n) if split[a]]

    def plan(ins, outs, sems):
        isend, irecv, dsend, drecv, loc = sems
        x, y, c = _place()
        chips = _other_chips(x, y)

        def part(ref, a, half):
            if not split[a]:
                return ref
            w = shards[a].shape[1] // 2
            return ref.at[:, pl.ds(pl.multiple_of(half * w, 128), w)]

        def over_ici(a, k, shard_of):
            tx, ty = chips[k]
            sx, sy = shard_of
            return pltpu.make_async_remote_copy(
                src_ref=part(ins[a], a, c), dst_ref=part(outs[a].at[2 * sx + sy], a, c), send_sem=isend.at[3 * a + k],
                recv_sem=irecv.at[3 * a + k], device_id=(tx, ty, c), device_id_type=MESH)

        def over_d2d(a, k, half):
            tx, ty = chips[k]
            ref = part(outs[a].at[2 * tx + ty], a, half)
            return pltpu.make_async_remote_copy(
                src_ref=ref, dst_ref=ref, send_sem=dsend.at[3 * a + k], recv_sem=drecv.at[3 * a + k],
                device_id=(x, y, 1 - c), device_id_type=MESH)

        def local(a):
            return pltpu.make_async_copy(ins[a], outs[a].at[2 * x + y], loc.at[a])

        pairs = [(a, k) for a in range(n) for k in range(3)]
        first = ([lambda a=a: local(a).start() for a in range(n)]
                 + [lambda a=a, k=k: over_ici(a, k, (x, y)).start() for a, k in pairs],
                 [lambda a=a, k=k: over_ici(a, k, chips[k]).wait_recv() for a, k in pairs]
                 + [lambda a=a, k=k: over_ici(a, k, (x, y)).wait_send() for a, k in pairs]
                 + [lambda a=a: local(a).wait() for a in range(n)])
        pairs2 = [(a, k) for a in two for k in range(3)]
        second = ([lambda a=a, k=k: over_d2d(a, k, c).start() for a, k in pairs2],
                  [lambda a=a, k=k: over_d2d(a, k, 1 - c).wait_recv() for a, k in pairs2]
                  + [lambda a=a, k=k: over_d2d(a, k, c).wait_send() for a, k in pairs2])
        return [first, second] if two else [first]

    return _Comm(tuple(shards), tuple(jax.ShapeDtypeStruct((4,) + s.shape, s.dtype) for s in shards),
                 _dma_sems(3 * n, 3 * n, 3 * n, 3 * n, n), 2 if two else 1, plan)


def _swap_halves(grads):
    n = len(grads)

    def plan(ins, outs, sems):
        send, recv = sems
        x, y, c = _place()

        def swap(a):
            return pltpu.make_async_remote_copy(
                src_ref=ins[a].at[1 - c], dst_ref=outs[a], send_sem=send.at[a], recv_sem=recv.at[a],
                device_id=(x, y, 1 - c), device_id_type=MESH)

        return [([lambda a=a: swap(a).start() for a in range(n)], [lambda a=a: swap(a).wait() for a in range(n)])]

    return _Comm(tuple(grads), tuple(jax.ShapeDtypeStruct(g.shape[1:], g.dtype) for g in grads), _dma_sems(n, n), 1, plan)


def _scatter_shards(parts):
    n = len(parts)

    def plan(ins, outs, sems):
        send, recv = sems
        x, y, c = _place()
        chips = _other_chips(x, y)

        def scatter(a, k):
            tx, ty = chips[k]
            return pltpu.make_async_remote_copy(
                src_ref=ins[a].at[2 * tx + ty], dst_ref=outs[a].at[k], send_sem=send.at[3 * a + k],
                recv_sem=recv.at[3 * a + k], device_id=(tx, ty, c), device_id_type=MESH)

        pairs = [(a, k) for a in range(n) for k in range(3)]
        return [([lambda a=a, k=k: scatter(a, k).start() for a, k in pairs],
                 [lambda a=a, k=k: scatter(a, k).wait() for a, k in pairs])]

    return _Comm(tuple(parts), tuple(jax.ShapeDtypeStruct((3,) + p.shape[1:], p.dtype) for p in parts),
                 _dma_sems(3 * n, 3 * n), 1, plan)


def _join_halves(halves):
    n = len(halves)

    def plan(ins, outs, sems):
        send, recv, loc = sems
        x, y, c = _place()

        def remote(a, half):
            return pltpu.make_async_remote_copy(
                src_ref=ins[a], dst_ref=outs[a].at[half], send_sem=send.at[a], recv_sem=recv.at[a],
                device_id=(x, y, 1 - c), device_id_type=MESH)

        def local(a):
            return pltpu.make_async_copy(ins[a], outs[a].at[c], loc.at[a])

        every = range(n)
        return [([lambda a=a: local(a).start() for a in every] + [lambda a=a: remote(a, c).start() for a in every],
                 [lambda a=a: remote(a, 1 - c).wait_recv() for a in every]
                 + [lambda a=a: remote(a, c).wait_send() for a in every] + [lambda a=a: local(a).wait() for a in every])]

    return _Comm(tuple(halves), tuple(jax.ShapeDtypeStruct((2,) + h.shape, h.dtype) for h in halves),
                 _dma_sems(n, n, n), 1, plan)


def _reduce_w_in(dwt, comm):
    rows, hw = DIN // 4, D // 2
    ci, co = len(comm.ins), len(comm.outs)

    def body(*refs):
        dw_ref, c_in, out_ref, c_out = refs[0], refs[1:1 + ci], refs[1 + ci], refs[2 + ci:2 + ci + co]
        mine, sib, tosend, rbuf, qbuf, full, send, recv, loc = refs[2 + ci + co:11 + ci + co]
        c_sem = refs[11 + ci + co:]
        x, y, c = _place()
        sibling = (x, y, 1 - c)
        (starts, waits), = comm.plan(c_in, c_out, c_sem)
        _run_phase(starts)

        def cols(ref, half):
            window = pl.ds(pl.multiple_of(half * hw, 128), hw)
            return ref.at[:, :, window] if len(ref.shape) == 3 else ref.at[:, window]

        load = pltpu.make_async_copy(cols(dw_ref, c), mine, loc.at[0])
        give = pltpu.make_async_remote_copy(src_ref=cols(dw_ref, 1 - c), dst_ref=sib, send_sem=send.at[3], recv_sem=recv.at[3],
                                            device_id=sibling, device_id_type=MESH)
        load.start()
        give.start()
        load.wait()
        give.wait()
        mine[...] = mine[...] + sib[...]
        cps = []
        for k, (tx, ty) in enumerate(_other_chips(x, y)):
            tosend[k] = _bf(mine[2 * tx + ty])
            cps.append(pltpu.make_async_remote_copy(
                src_ref=tosend.at[k], dst_ref=rbuf.at[k], send_sem=send.at[k], recv_sem=recv.at[k],
                device_id=(tx, ty, c), device_id_type=MESH))
            cps[-1].start()
        for cp in cps:
            cp.wait()
        qbuf[...] = mine[2 * x + y] + rbuf[0].astype(F32) + rbuf[1].astype(F32) + rbuf[2].astype(F32)
        keep = pltpu.make_async_copy(qbuf, cols(full, c), loc.at[1])
        pass_on = pltpu.make_async_remote_copy(src_ref=qbuf, dst_ref=cols(full, c), send_sem=send.at[4], recv_sem=recv.at[4],
                                               device_id=sibling, device_id_type=MESH)
        keep.start()
        pass_on.start()
        keep.wait()
        pass_on.wait_send()
        pltpu.make_async_remote_copy(src_ref=qbuf, dst_ref=cols(full, 1 - c), send_sem=send.at[4], recv_sem=recv.at[4],
                                     device_id=sibling, device_id_type=MESH).wait_recv()
        out_ref[...] = full[...]
        _run_phase(waits)

    outs = pl.pallas_call(
        body, name="reduce_w_in",
        in_specs=[ANY_SPEC] * (1 + ci), out_specs=[VMEM_SPEC] + [ANY_SPEC] * co,
        out_shape=[jax.ShapeDtypeStruct((rows, D), F32)] + list(comm.outs),
        scratch_shapes=[pltpu.VMEM((4, rows, hw), F32), pltpu.VMEM((4, rows, hw), F32), pltpu.VMEM((3, rows, hw), BF16),
                        pltpu.VMEM((3, rows, hw), BF16), pltpu.VMEM((rows, hw), F32), pltpu.VMEM((rows, D), F32),
                        *_dma_sems(5, 5, 2), *comm.sems],
        compiler_params=pltpu.CompilerParams(vmem_limit_bytes=48 << 20),
    )(dwt, *comm.ins)
    return outs[0], outs[1:]


def _allreduce_small(pack):
    p = pack.shape[0]

    def body(in_ref, out_ref, buf, send, recv):
        x, y, c = _place()
        me = 4 * x + 2 * y + c
        buf[me] = in_ref[...]

        def peer_of(k):
            return x ^ (k >> 2), y ^ ((k >> 1) & 1), c ^ (k & 1)

        sends = [pltpu.make_async_remote_copy(
            src_ref=in_ref, dst_ref=buf.at[me], send_sem=send.at[k - 1], recv_sem=recv.at[k - 1],
            device_id=peer_of(k), device_id_type=MESH) for k in range(1, 8)]
        for cp in sends:
            cp.start()
        for k in range(1, 8):
            px, py, pc = peer_of(k)
            pltpu.make_async_remote_copy(
                src_ref=in_ref, dst_ref=buf.at[4 * px + 2 * py + pc], send_sem=send.at[k - 1], recv_sem=recv.at[k - 1],
                device_id=(x, y, c), device_id_type=MESH).wait_recv()
        for cp in sends:
            cp.wait_send()
        acc = buf[0]
        for d in range(1, 8):
            acc = acc + buf[d]
        out_ref[...] = acc

    return pl.pallas_call(
        body, name="allreduce_small",
        in_specs=[VMEM_SPEC], out_specs=VMEM_SPEC, out_shape=jax.ShapeDtypeStruct(pack.shape, F32),
        scratch_shapes=[pltpu.VMEM((8, p, D), F32), *_dma_sems(7, 7)],
    )(pack)


GRID4 = 4


def _sum_cores(core_shard, mine, theirs):
    n = len(mine)

    def body(cs_ref, *refs):
        ms, ts, bfs, owns = refs[:n], refs[n:2 * n], refs[2 * n:3 * n], refs[3 * n:]
        keep = pl.program_id(0) == cs_ref[1]
        for a in range(n):
            acc = ms[a][0, 0] + ts[a][0]
            bfs[a][0] = _bf(acc)

            @pl.when(keep)
            def _():
                owns[a][...] = acc

    shapes = [m.shape[2:] for m in mine]
    in_specs = ([pl.BlockSpec((1, 1) + s, lambda i, cs: (cs[0], i, 0, 0)) for s in shapes]
                + [pl.BlockSpec((1,) + s, lambda i, cs: (i, 0, 0)) for s in shapes])
    out_specs = ([pl.BlockSpec((1,) + s, lambda i, cs: (i, 0, 0)) for s in shapes]
                 + [pl.BlockSpec(s, lambda i, cs: (0, 0)) for s in shapes])
    outs = pl.pallas_call(
        body, name="sum_cores",
        grid_spec=pltpu.PrefetchScalarGridSpec(num_scalar_prefetch=1, grid=(4,), in_specs=in_specs, out_specs=out_specs),
        out_shape=[jax.ShapeDtypeStruct((4,) + s, BF16) for s in shapes] + [jax.ShapeDtypeStruct(s, F32) for s in shapes],
        compiler_params=_cp(48),
    )(core_shard, *mine, *theirs)
    return outs[:n], outs[n:]


def _sum_chips(own, arrived):
    n = len(own)

    def body(*refs):
        os_, ars, outs = refs[:n], refs[n:2 * n], refs[2 * n:]
        for a in range(n):
            outs[a][...] = os_[a][...] + ars[a][0].astype(F32) + ars[a][1].astype(F32) + ars[a][2].astype(F32)

    blocks = [(o.shape[0] // GRID4, o.shape[1]) for o in own]
    return pl.pallas_call(
        body, name="sum_chips", grid=(GRID4,),
        in_specs=([pl.BlockSpec(b, lambda i: (i, 0)) for b in blocks]
                  + [pl.BlockSpec((3,) + b, lambda i: (0, i, 0)) for b in blocks]),
        out_specs=[pl.BlockSpec(b, lambda i: (i, 0)) for b in blocks],
        out_shape=[jax.ShapeDtypeStruct(o.shape, F32) for o in own],
        compiler_params=_cp(32),
    )(*own, *arrived)


def _adamw_math(w, g, m, v):
    m2 = ADAM_B1 * m + (1.0 - ADAM_B1) * g
    v2 = ADAM_B2 * v + (1.0 - ADAM_B2) * (g * g)
    m_hat = m2 / (1.0 - ADAM_B1 ** ADAM_STEP)
    v_hat = v2 / (1.0 - ADAM_B2 ** ADAM_STEP)
    return -ADAM_LR * (m_hat / (jnp.sqrt(v_hat) + ADAM_EPS) + ADAM_WD * w), m2, v2


def _adamw_big(ws, gs, ms, vs):
    n = len(ws)

    def body(*refs):
        for a in range(n):
            d, m2, v2 = _adamw_math(refs[a][...], refs[n + a][...], refs[2 * n + a][...], refs[3 * n + a][...])
            refs[4 * n + a][...] = d
            refs[5 * n + a][...] = m2
            refs[6 * n + a][...] = v2

    specs = [pl.BlockSpec((w.shape[0] // GRID4, w.shape[1]), lambda i: (i, 0)) for w in ws]
    return pl.pallas_call(
        body, name="adamw_big", grid=(GRID4,),
        in_specs=specs * 4, out_specs=specs * 3,
        out_shape=[jax.ShapeDtypeStruct(w.shape, F32) for w in ws] * 3,
        compiler_params=_cp(48),
    )(*ws, *gs, *ms, *vs)


def _adamw_small(ws, gs, ms, vs):
    n = len(ws)

    def body(*refs):
        for a in range(n):
            d, m2, v2 = _adamw_math(refs[a][...], refs[n + a][...], refs[2 * n + a][...], refs[3 * n + a][...])
            refs[4 * n + a][...] = d
            refs[5 * n + a][...] = m2
            refs[6 * n + a][...] = v2

    return pl.pallas_call(
        body, name="adamw_small",
        in_specs=[VMEM_SPEC] * (4 * n), out_specs=[VMEM_SPEC] * (3 * n),
        out_shape=[jax.ShapeDtypeStruct(w.shape, F32) for w in ws] * 3,
        compiler_params=pltpu.CompilerParams(vmem_limit_bytes=40 << 20),
    )(*ws, *gs, *ms, *vs)


def kernel(x, meta_tokens, norm_mix_w, w_in, w_gate_up, b_gate, gla_norm_w, sinks, w_out, norm_ff_w, w_ff1, w_ff2, final_norm_w, loss_target, m_meta_tokens, m_norm_mix_w, m_w_in, m_w_gate_up, m_b_gate, m_gla_norm_w, m_sinks, m_w_out, m_norm_ff_w, m_w_ff1, m_w_ff2, m_final_norm_w, v_meta_tokens, v_norm_mix_w, v_w_in, v_w_gate_up, v_b_gate, v_gla_norm_w, v_sinks, v_w_out, v_norm_ff_w, v_w_ff1, v_w_ff2, v_final_norm_w):
    xi, yi, ci = _place()
    shard = (2 * xi + yi).astype(jnp.int32).reshape(1)
    core = ci.astype(jnp.int32).reshape(1)

    small = jnp.concatenate([meta_tokens, w_gate_up[0], jnp.zeros((NM, 64), F32)], axis=1)
    wt3, g_small = _run_comm(_gather_shards([_bf(w_in[0].T), small], [True, False]), "gather_w_in")
    meta = g_small[:, :, 0:256].transpose(1, 0, 2).reshape(NM, D)
    wgu = g_small[:, :, 256:320].transpose(1, 0, 2).reshape(NM, 256)

    xs, tgt = x[0], loss_target[0]
    t = xs.shape[0]
    wfin = final_norm_w.reshape(1, D)
    metapad = jnp.concatenate([meta, jnp.zeros((TM - NM, D), F32)], axis=0)
    wgu_p = _bf(jnp.concatenate([wgu, jnp.zeros((128 - 16, 256), F32)], axis=0))
    tabs = _rope_tables(t)

    w1s, w2s = _bf(w_ff1[0]), _bf(w_ff2[0])
    proj, (g_out, w1a, w1b) = _proj_fwd(xs, metapad, norm_mix_w, wt3, tabs,
                                        _gather_shards([_bf(w_out[0]), w1s[:HK], w1s[HK:]], [True] * 3))
    (oswa, lse), (w2a, w2b) = _swa_fwd(proj, sinks, t, _gather_shards([w2s[:HK], w2s[HK:]], [True] * 2))
    (ogla, oraw, sst, bcum, dgate), _ = _gla_fwd(proj, wgu_p, b_gate, gla_norm_w, t)
    wo, w1, w2 = g_out.reshape(D, D), (w1a, w1b), (w2a, w2b)
    h1, f, a, dh2, loss, gfin = _mlp_fwd(xs, metapad, tgt, ogla, oswa, wo, norm_ff_w, w1, w2, wfin)

    da, dh2b, dh1, do, dwo, gff = _mlp_bwd(h1, a, dh2, ogla, oswa, wo, norm_ff_w, w1, w2)
    dw1, dw2 = _ffn_wgrad(f, a, da, dh2b)
    big = [dwo, dw1, dw2]
    (dgla, dlr, dwgu, dbg, dgnw), theirs = _gla_bwd(proj, oraw, sst, bcum, dgate, do, wgu_p, gla_norm_w, t,
                                                    _swap_halves(big))
    sums_bf, own = _sum_cores(jnp.concatenate([core, shard]), big, theirs)
    (dsq, dsk, dsv, dsink), arrived = _swa_bwd(proj, sinks, lse, do, t, _scatter_shards(sums_bf))
    halves = _sum_chips(own, arrived)
    (gx, gmeta, dwt, gmix), _ = _proj_bwd(xs, metapad, norm_mix_w, wt3, tabs, dgla, dsq, dsk, dsv, dlr, dh1)

    gwt_in, joined = _reduce_w_in(dwt, _join_halves(halves))
    gw_out, gw_1, gw_2 = [j.reshape((-1, j.shape[2])) for j in joined]

    tail = jnp.concatenate([dbg, dgnw, dsink, loss, jnp.zeros((1, D - 256 - 128 - 8 - 1), F32)], axis=1)
    pack = jnp.concatenate([gmeta, gmix, gff, gfin, tail, dwgu[:16].reshape(4, D)], axis=0)
    tot = _allreduce_small(pack)
    g_meta = lax.dynamic_slice_in_dim(tot[0:NM], shard[0] * 256, 256, axis=1)
    g_mix, g_ff, g_fin = tot[16:17], tot[17:18], tot[18]
    g_bg, g_gnw, g_sinks, loss_tot = tot[19:20, 0:256], tot[19:20, 256:384], tot[19:20, 384:392], tot[19, 392]
    g_wgu = lax.dynamic_slice_in_dim(tot[20:24].reshape(NM, 256), shard[0] * 64, 64, axis=1)

    bo = _adamw_big([w_out[0], w_ff1[0], w_ff2[0]], [gw_out, gw_1, gw_2], [m_w_out[0], m_w_ff1[0], m_w_ff2[0]],
                    [v_w_out[0], v_w_ff1[0], v_w_ff2[0]])

    fin2 = lambda a: a.reshape(1, D)
    sw = [meta_tokens, norm_mix_w, w_gate_up[0], b_gate, gla_norm_w, sinks, norm_ff_w, fin2(final_norm_w), w_in[0].T]
    sg = [g_meta, g_mix, g_wgu, g_bg, g_gnw, g_sinks, g_ff, fin2(g_fin), gwt_in]
    sm = [m_meta_tokens, m_norm_mix_w, m_w_gate_up[0], m_b_gate, m_gla_norm_w, m_sinks, m_norm_ff_w, fin2(m_final_norm_w),
          m_w_in[0].T]
    sv = [v_meta_tokens, v_norm_mix_w, v_w_gate_up[0], v_b_gate, v_gla_norm_w, v_sinks, v_norm_ff_w, fin2(v_final_norm_w),
          v_w_in[0].T]
    so = _adamw_small(sw, sg, sm, sv)

    def ordered(small_o, big_o):
        meta_, mix_, wgu_, bg_, gnw_, sinks_, ff_, fin_, wt_ = small_o
        w_out_, w_1_, w_2_ = big_o
        return (meta_, mix_, wt_.T[None], wgu_[None], bg_, gnw_, sinks_, w_out_[None], ff_, w_1_[None], w_2_[None],
                fin_.reshape(D))

    grads = ordered(sg, [gw_out, gw_1, gw_2])
    deltas = ordered(so[0:9], bo[0:3])
    new_m = ordered(so[9:18], bo[3:6])
    new_v = ordered(so[18:27], bo[6:9])
    return (loss_tot, gx[None], *grads, *deltas, *new_m, *new_v)
```

```python
import functools
from typing import Callable, NamedTuple

import jax
import jax.numpy as jnp
import numpy as np
from jax import lax
from jax.experimental import pallas as pl
from jax.experimental.pallas import tpu as pltpu

F32 = jnp.float32
BF16 = jnp.bfloat16

D = 1024
DFF = 4096
NM = 16
TM = 256
DK = 64
CH = 128
SB = 128
EPS = 1e-5
C_GQ, C_GK, C_GV, C_GR, C_SQ, C_SK, C_SV, C_LR, DINP = 0, 256, 512, 1024, 1536, 2048, 2176, 2304, 2432
DIN = 2320
R_LR = 1536
ROPE_THETA = 500000.0
ADAM_LR, ADAM_B1, ADAM_B2, ADAM_EPS, ADAM_WD, ADAM_STEP = 0.001, 0.9, 0.999, 1e-08, 0.01, 10
NEG = -1e30
MESH = pl.DeviceIdType.MESH
VMEM_SPEC = pl.BlockSpec(memory_space=pltpu.VMEM)
ANY_SPEC = pl.BlockSpec(memory_space=pl.ANY)
SMEM_SPEC = pl.BlockSpec(memory_space=pltpu.SMEM)


def _cp(vmem_mb, sem=("arbitrary",)):
    return pltpu.CompilerParams(dimension_semantics=sem, vmem_limit_bytes=vmem_mb << 20)


def _dot(a, b):
    return jnp.dot(a, b, preferred_element_type=F32)


def _dot_nt(a, b):
    return lax.dot_general(a, b, (((1,), (1,)), ((), ())), preferred_element_type=F32)


def _dot_tn(a, b):
    return lax.dot_general(a, b, (((0,), (0,)), ((), ())), preferred_element_type=F32)


def _bf(x):
    return x.astype(BF16)


def _dot3(m01, x):
    x1 = _bf(x)
    r1 = x - x1.astype(F32)
    x2 = _bf(r1)
    x3 = _bf(r1 - x2.astype(F32))
    return _dot(m01, x1) + _dot(m01, x2) + _dot(m01, x3)


def _rms(h):
    rs = lax.rsqrt(jnp.mean(h * h, axis=-1, keepdims=True) + EPS)
    return h * rs, rs


def _rms_bwd(dy, yhat, rs, w):
    dyh = dy * w
    return rs * (dyh - yhat * jnp.mean(dyh * yhat, axis=-1, keepdims=True))


class _Comm(NamedTuple):
    ins: tuple
    outs: tuple
    sems: tuple
    phases: int
    plan: Callable


def _run_phase(fns):
    for fn in fns:
        fn()


def _call(body, name, grid, in_specs, out_specs, out_shape, scratch, params, args, comm=None):
    if comm is None:
        outs = pl.pallas_call(body, name=name, grid=grid, in_specs=in_specs, out_specs=out_specs, out_shape=out_shape,
                              scratch_shapes=scratch, compiler_params=params)(*args)
        return outs, None
    n_in, n_out, n_scr = len(in_specs), len(out_specs), len(scratch)
    ci, co = len(comm.ins), len(comm.outs)
    last = grid[0] - 1
    marks = [0, max(1, last - max(2, (last + 1) // 6))][:comm.phases]

    def wrapped(*refs):
        own_in, c_in = refs[:n_in], refs[n_in:n_in + ci]
        refs = refs[n_in + ci:]
        own_out, c_out = refs[:n_out], refs[n_out:n_out + co]
        refs = refs[n_out + co:]
        own_scr, c_sem = refs[:n_scr], refs[n_scr:]
        i = pl.program_id(0)

        for p, mark in enumerate(marks):
            @pl.when(i == mark)
            def _():
                plan = comm.plan(c_in, c_out, c_sem)
                if p > 0:
                    _run_phase(plan[p - 1][1])
                _run_phase(plan[p][0])

        body(*own_in, *own_out, *own_scr)

        @pl.when(i == last)
        def _():
            _run_phase(comm.plan(c_in, c_out, c_sem)[-1][1])

    outs = pl.pallas_call(
        wrapped, name=name, grid=grid, in_specs=list(in_specs) + [ANY_SPEC] * ci, out_specs=list(out_specs) + [ANY_SPEC] * co,
        out_shape=list(out_shape) + list(comm.outs), scratch_shapes=list(scratch) + list(comm.sems), compiler_params=params,
    )(*args, *comm.ins)
    return outs[:n_out], outs[n_out:]


def _run_comm(comm, name):
    ci, co = len(comm.ins), len(comm.outs)

    def body(*refs):
        for starts, waits in comm.plan(refs[:ci], refs[ci:ci + co], refs[ci + co:]):
            _run_phase(starts)
            _run_phase(waits)

    return pl.pallas_call(body, name=name, in_specs=[ANY_SPEC] * ci, out_specs=[ANY_SPEC] * co, out_shape=list(comm.outs),
                          scratch_shapes=list(comm.sems))(*comm.ins)


def _join_shards(w3_ref, w_ref):
    for s in range(4):
        w_ref[(DIN // 4) * s:(DIN // 4) * (s + 1), :] = w3_ref[s]


def _proj_fwd(x, metapad, wm, wt3, tabs, comm=None):
    t = x.shape[0]
    nblk = t // TM

    def body(x_ref, mp_ref, wm_ref, w3_ref, tab_ref, proj_ref, w_ref):
        i = pl.program_id(0)

        @pl.when(i == 0)
        def _():
            _join_shards(w3_ref, w_ref)

        h = jnp.where(i == nblk, mp_ref[...], x_ref[...])
        u, _ = _rms(h)
        ub = _bf(u * wm_ref[...])
        proj_ref[:, 0:C_SQ] = _dot_nt(ub, w_ref[0:R_LR, :])
        att = _dot_nt(ub, w_ref[R_LR + 16:DIN, :])
        tab = tab_ref[...]
        proj_ref[:, C_SQ:C_SK] = _rope(att[:, 0:512], tab, 1.0) * 0.125
        proj_ref[:, C_SK:C_SV] = _rope(att[:, 512:640], tab, 1.0)
        proj_ref[:, C_SV:C_LR] = att[:, 640:768]
        proj_ref[:, C_LR:DINP] = jnp.zeros((TM, DINP - C_LR), F32)
        proj_ref[:, C_LR:C_LR + 16] = _dot_nt(ub, w_ref[R_LR:R_LR + 16, :])

    (proj,), got = _call(
        body, "proj_fwd", (nblk + 1,),
        [pl.BlockSpec((TM, D), lambda i: (jnp.minimum(i, nblk - 1), 0)), VMEM_SPEC, VMEM_SPEC, VMEM_SPEC,
         pl.BlockSpec((TM, 128), lambda i: (i, 0))],
        [pl.BlockSpec((TM, DINP), lambda i: (i, 0))], [jax.ShapeDtypeStruct((t + TM, DINP), F32)],
        [pltpu.VMEM((DIN, D), BF16)], _cp(48), (x, metapad, wm, wt3, tabs), comm)
    return proj, got


def _chunk_masks():
    r = lax.broadcasted_iota(jnp.int32, (TM, TM), 0)
    c = lax.broadcasted_iota(jnp.int32, (TM, TM), 1)
    same = (r // CH) == (c // CH)
    lower = _bf(jnp.where(same & (c <= r), 1.0, 0.0))
    upper = _bf(jnp.where(same & (c >= r), 1.0, 0.0))
    return lower, upper


def _gla_gate(lr, wgu, bg, valid, lower):
    z = _dot(_bf(lr), wgu) + bg
    g = (jnp.minimum(z, 0.0) - jnp.log(1.0 + jnp.exp(-jnp.abs(z)))) * (1.0 / 16.0)
    g = jnp.where(valid, g, 0.0)
    return z, _dot3(lower, g)


def _gla_decays(q, k, b):
    nc = TM // CH
    b3 = b.reshape(nc, CH, 256)
    blast = b3[:, CH - 1:CH, :]
    eb = jnp.exp(b)
    enb = jnp.exp(-b)
    ebl = jnp.exp(blast - b3).reshape(TM, 256)
    return eb, enb, ebl, jnp.exp(blast)


def _tri(lower_incl):
    r = lax.broadcasted_iota(jnp.int32, (CH, CH), 0)
    c = lax.broadcasted_iota(jnp.int32, (CH, CH), 1)
    return ((c <= r) if lower_incl else (c >= r))[None]


def _gla_fwd(proj, wgu, bg, gnw, t, comm=None):
    nblk = t // TM
    nt = nblk + 1
    nc = TM // CH

    def blk(i):
        return (i + nblk) % nt

    def body(q_ref, k_ref, v_ref, r_ref, lr_ref, wgu_ref, bg_ref, gnw_ref, o_ref, oraw_ref, sst_ref, b_ref, dgate_ref,
             st_scr):
        i = pl.program_id(0)

        @pl.when(i == 0)
        def _():
            st_scr[...] = jnp.zeros_like(st_scr)

        rows = blk(i) * TM + lax.broadcasted_iota(jnp.int32, (TM, 1), 0)
        lower, _ = _chunk_masks()
        valid = rows < t + NM
        z, b = _gla_gate(lr_ref[...], wgu_ref[...], bg_ref[...], valid, lower)
        b_ref[...] = b
        dgate_ref[...] = jnp.where(valid, (1.0 / 16.0) / (1.0 + jnp.exp(z)), 0.0)
        q = q_ref[...]
        k = k_ref[...]
        eb, enb, ebl, eblast = _gla_decays(q, k, b)
        qt = q * 0.125 * eb
        kt = k * enb
        kh = k * ebl
        tril = _tri(True)
        heads = range(4)
        hs = [slice(h * DK, (h + 1) * DK) for h in heads]
        qh = [_bf(qt[:, hs[h]]).reshape(nc, CH, DK) for h in heads]
        kth = [_bf(kt[:, hs[h]]).reshape(nc, CH, DK) for h in heads]
        khh = [_bf(kh[:, hs[h]]).reshape(nc, CH, DK) for h in heads]
        vh = [_bf(v_ref[:, h * 128:(h + 1) * 128]).reshape(nc, CH, 128) for h in heads]
        a = [jnp.einsum('cid,cjd->cij', qh[h], kth[h], preferred_element_type=F32) for h in heads]
        kv = [jnp.einsum('cjv,cjd->cvd', vh[h], khh[h], preferred_element_type=F32) for h in heads]
        o = [jnp.einsum('cij,cjv->civ', _bf(jnp.where(tril, a[h], 0.0)), vh[h], preferred_element_type=F32) for h in heads]
        states = []
        for h in heads:
            st = st_scr[h]
            per_chunk = []
            for c in range(nc):
                sst_ref[c, h] = st
                per_chunk.append(_bf(st))
                st = st * eblast[c, :, hs[h]] + kv[h][c]
            st_scr[h] = st
            states.append(per_chunk)
        o_inter = [[_dot_nt(qh[h][c], states[h][c]) for c in range(nc)] for h in heads]
        oraw = jnp.concatenate([(o[h] + jnp.stack(o_inter[h])).reshape(TM, 128) for h in heads], axis=1)
        oraw_ref[...] = oraw
        gn = gnw_ref[...]
        res = []
        for h in range(4):
            on, _ = _rms(oraw[:, h * 128:(h + 1) * 128])
            r = r_ref[:, h * 128:(h + 1) * 128]
            res.append(on * gn * (r * jax.nn.sigmoid(r)))
        o_ref[...] = _bf(jnp.concatenate(res, axis=1))

    def spec(w, cb):
        return pl.BlockSpec((TM, w), lambda i: (blk(i), cb))

    return _call(
        body, "gla_fwd", (nt,),
        [spec(256, 0), spec(256, 1), spec(512, 1), spec(512, 2), spec(128, C_LR // 128), VMEM_SPEC, VMEM_SPEC, VMEM_SPEC],
        [spec(512, 0), spec(512, 0), pl.BlockSpec((nc, 4, 128, DK), lambda i: (blk(i), 0, 0, 0)), spec(256, 0), spec(256, 0)],
        [jax.ShapeDtypeStruct((t + TM, 512), BF16), jax.ShapeDtypeStruct((t + TM, 512), F32),
         jax.ShapeDtypeStruct((nt * nc, 4, 128, DK), F32), jax.ShapeDtypeStruct((t + TM, 256), F32),
         jax.ShapeDtypeStruct((t + TM, 256), F32)],
        [pltpu.VMEM((4, 128, DK), F32)], _cp(40), (proj, proj, proj, proj, proj, wgu, bg, gnw), comm)


def _gla_bwd(proj, oraw, sst, bcum, dgate, do, wgu, gnw, t, comm=None):
    nblk = t // TM
    nt = nblk + 1
    nc = TM // CH

    def blk(i):
        return (2 * nblk - i) % nt

    def body(q_ref, k_ref, v_ref, r_ref, lr_ref, oraw_ref, sst_ref, b_ref, dgate_ref, do_ref, wgu_ref, gnw_ref,
             dgla_ref, dlr_ref, dwgu_ref, dbg_ref, dgnw_ref, dst_scr):
        i = pl.program_id(0)

        @pl.when(i == 0)
        def _():
            dst_scr[...] = jnp.zeros_like(dst_scr)
            dwgu_ref[...] = jnp.zeros_like(dwgu_ref)
            dbg_ref[...] = jnp.zeros_like(dbg_ref)
            dgnw_ref[...] = jnp.zeros_like(dgnw_ref)

        _, upper = _chunk_masks()
        lr = lr_ref[...]
        b = b_ref[...]
        q = q_ref[...]
        k = k_ref[...]
        eb, enb, ebl, eblast = _gla_decays(q, k, b)
        qt = q * 0.125 * eb
        kt = k * enb
        kh = k * ebl
        gn = gnw_ref[...]
        tril = _tri(True)
        triu = _tri(False)
        heads = range(4)
        hs = [slice(h * DK, (h + 1) * DK) for h in heads]
        vs = [slice(h * 128, (h + 1) * 128) for h in heads]
        ein = functools.partial(jnp.einsum, preferred_element_type=F32)
        dr_l, doh = [], []
        dgn = jnp.zeros((1, 128), F32)
        for h in heads:
            on, rs = _rms(oraw_ref[:, vs[h]])
            r = r_ref[:, vs[h]]
            sig = jax.nn.sigmoid(r)
            sil = r * sig
            dy = do_ref[:, vs[h]]
            dr_l.append(dy * on * gn * (sig * (1.0 + r * (1.0 - sig))))
            dgn = dgn + jnp.sum(dy * sil * on, axis=0, keepdims=True)
            doh.append(_bf(_rms_bwd(dy * sil, on, rs, gn)).reshape(nc, CH, 128))
        dgnw_ref[...] += dgn
        qh = [_bf(qt[:, hs[h]]).reshape(nc, CH, DK) for h in heads]
        kth = [_bf(kt[:, hs[h]]).reshape(nc, CH, DK) for h in heads]
        khh = [_bf(kh[:, hs[h]]).reshape(nc, CH, DK) for h in heads]
        vh = [_bf(v_ref[:, vs[h]]).reshape(nc, CH, 128) for h in heads]
        at = [ein('cjd,cid->cji', kth[h], qh[h]) for h in heads]
        da = [ein('civ,cjv->cij', doh[h], vh[h]) for h in heads]
        dat = [ein('cjv,civ->cji', vh[h], doh[h]) for h in heads]
        gq = [ein('civ,cid->cvd', doh[h], qh[h]) for h in heads]
        stf = [sst_ref[:, h] for h in heads]
        dqs = [ein('civ,cvd->cid', doh[h], _bf(stf[h])) for h in heads]
        dv = [ein('cji,civ->cjv', _bf(jnp.where(triu, at[h], 0.0)), doh[h]) for h in heads]
        dqt = [ein('cij,cjd->cid', _bf(jnp.where(tril, da[h], 0.0)), kth[h]) + dqs[h] for h in heads]
        dkt = [ein('cji,cid->cjd', _bf(jnp.where(triu, dat[h], 0.0)), qh[h]) for h in heads]
        dse = []
        for h in heads:
            dst = dst_scr[h]
            dsend = [None] * nc
            for c in reversed(range(nc)):
                dsend[c] = dst
                dst = dst * eblast[c, :, hs[h]] + gq[h][c]
            dst_scr[h] = dst
            dse.append(jnp.stack(dsend))
        dseb = [_bf(d) for d in dse]
        dv = [dv[h] + ein('cjd,cvd->cjv', khh[h], dseb[h]) for h in heads]
        dkh = [ein('cjv,cvd->cjd', vh[h], dseb[h]) for h in heads]
        carried = jnp.concatenate([jnp.sum(dse[h] * stf[h], axis=1, keepdims=True) for h in heads], axis=2)
        wide = lambda parts: jnp.concatenate([p.reshape(TM, DK) for p in parts], axis=1)
        dqt_w, dkt_w, dkh_w = wide(dqt), wide(dkt), wide(dkh)
        dkh_kh = dkh_w * kh
        extra = jnp.sum(dkh_kh.reshape(nc, CH, 256), axis=1, keepdims=True) + eblast * carried
        db = dqt_w * qt - dkt_w * kt - dkh_kh
        dg = _dot3(upper, db) + jnp.broadcast_to(extra, (nc, CH, 256)).reshape(TM, 256)
        dz = dg * dgate_ref[...]
        dzb = _bf(dz)
        dlr_ref[...] = _bf(_dot_nt(dzb, wgu_ref[...]))
        dwgu_ref[...] += _dot_tn(_bf(lr), dzb)
        dbg_ref[...] += jnp.sum(dz, axis=0, keepdims=True)
        dq = dqt_w * eb * 0.125
        dk = dkt_w * enb + dkh_w * ebl
        dgla_ref[...] = _bf(jnp.concatenate([dq, dk] + [d.reshape(TM, 128) for d in dv] + dr_l, axis=1))

    def spec(w, cb):
        return pl.BlockSpec((TM, w), lambda i: (blk(i), cb))

    def acc(shape):
        return pl.BlockSpec(shape, lambda i: (0, 0))

    return _call(
        body, "gla_bwd", (nt,),
        [spec(256, 0), spec(256, 1), spec(512, 1), spec(512, 2), spec(128, C_LR // 128), spec(512, 0),
         pl.BlockSpec((nc, 4, 128, DK), lambda i: (blk(i), 0, 0, 0)), spec(256, 0), spec(256, 0), spec(512, 0),
         VMEM_SPEC, VMEM_SPEC],
        [spec(1536, 0), spec(128, 0), acc((128, 256)), acc((1, 256)), acc((1, 128))],
        [jax.ShapeDtypeStruct((t + TM, 1536), BF16), jax.ShapeDtypeStruct((t + TM, 128), BF16),
         jax.ShapeDtypeStruct((128, 256), F32), jax.ShapeDtypeStruct((1, 256), F32), jax.ShapeDtypeStruct((1, 128), F32)],
        [pltpu.VMEM((4, 128, DK), F32)], _cp(48), (proj, proj, proj, proj, proj, oraw, sst, bcum, dgate, do, wgu, gnw), comm)


def _rope_tables(t):
    r = t + TM
    row = np.arange(r)
    pos = np.where(row < t, row + NM, np.where(row < t + NM, row - t, 0)).astype(np.float32)
    inv_freq = (1.0 / (np.float32(ROPE_THETA) ** (np.arange(0, 16, 2, dtype=np.float32) / np.float32(16)))).astype(np.float32)
    ang = (pos[:, None] * inv_freq[None, :]).astype(np.float32)
    cos, sin = np.cos(ang).astype(np.float32), np.sin(ang).astype(np.float32)
    one, zero = np.ones((r, 48), np.float32), np.zeros((r, 48), np.float32)
    return jnp.asarray(np.concatenate([cos, cos, one, -sin, sin, zero], axis=1))


def _rope(x, tab, sign):
    w = x.shape[1]
    rep = w // 64
    c = jnp.concatenate([tab[:, 0:64]] * rep, axis=1)
    s = jnp.concatenate([tab[:, 64:128]] * rep, axis=1)
    lane = lax.rem(lax.broadcasted_iota(jnp.int32, x.shape, 1), 64)
    partner = jnp.where(lane < 8, pltpu.roll(x, w - 8, 1), jnp.where(lane < 16, pltpu.roll(x, 8, 1), 0.0))
    return x * c + sign * (partner * s)


HB_BWD = 4


def _stack(x, hg):
    w = x.shape[1] // hg
    return x if hg == 1 else jnp.concatenate([x[:, g * w:(g + 1) * w] for g in range(hg)], axis=0)


def _unstack(x, hg):
    return x if hg == 1 else jnp.concatenate([x[g * SB:(g + 1) * SB] for g in range(hg)], axis=1)


def _swa_steps(r_tot):
    blocks = r_tot // SB
    return next(n for n in (6, 3, 2) if blocks % n == 0 and blocks // n >= 2)


def _swa_specs(nsb, nbq):
    def rows(h, w, cb, f):
        return pl.BlockSpec((h, w), lambda i: (f(i), cb))
    pair = lambda i: i
    prev = lambda i: jnp.maximum(nbq * i - 1, 0)
    meta = lambda i: nsb
    return rows, pair, prev, meta


def _swa_fwd(proj, sinks, t, comm=None):
    nsb = t // SB
    r_tot = t + TM
    nbq = _swa_steps(r_tot)
    qb = nbq * SB
    rows, pair, prev, meta = _swa_specs(nsb, nbq)

    def body(sink_ref, q_ref, kc_ref, kp_ref, km_ref, vc_ref, vp_ref, vm_ref, o_ref, lse_ref):
        i = pl.program_id(0)
        key = lax.broadcasted_iota(jnp.int32, (SB, SB), 0)
        qry = lax.broadcasted_iota(jnp.int32, (SB, SB), 1)
        km, vm = km_ref[0:NM, :], vm_ref[0:NM, :]
        for j in range(nbq):
            b = nbq * i + j
            rs = slice(j * SB, (j + 1) * SB)
            before = slice((j - 1) * SB, j * SB)
            real = b < nsb
            masks = (key <= qry, (key > qry) & (b > 0) & real, real)
            k3 = (kc_ref[rs, :], kp_ref[...] if j == 0 else kc_ref[before, :], km)
            v3 = (vc_ref[rs, :], vp_ref[...] if j == 0 else vc_ref[before, :], vm)
            valid = b * SB + lax.broadcasted_iota(jnp.int32, (1, SB), 1) < t + NM
            heads = range(8)
            kb = [[_bf(k[:, kv * 64:(kv + 1) * 64]) for k in k3] for kv in range(2)]
            vt = [[_bf(v[:, kv * 64:(kv + 1) * 64].T) for v in v3] for kv in range(2)]
            raw = [[_dot_nt(k, _bf(q_ref[rs, h * 64:(h + 1) * 64])) for k in kb[h // 4]] for h in heads]
            probs, inv_l, lse_l = [], [], []
            for h in heads:
                s = [jnp.where(m, sx, NEG) for m, sx in zip(masks, raw[h])]
                sink = sink_ref[0, h]
                top = jnp.maximum(jnp.max(jnp.maximum(s[0], s[1]), axis=0, keepdims=True),
                                  jnp.maximum(jnp.max(s[2], axis=0, keepdims=True), sink))
                p = [jnp.exp(sx - top) for sx in s]
                l = (jnp.sum(p[0] + p[1], axis=0, keepdims=True) + jnp.sum(p[2], axis=0, keepdims=True)
                     + jnp.exp(sink - top))
                probs.append([_bf(px) for px in p])
                inv_l.append(1.0 / l)
                lse_l.append(top + jnp.log(l))
            o_t = [_dot(vt[h // 4][0], probs[h][0]) + _dot(vt[h // 4][1], probs[h][1]) + _dot(vt[h // 4][2], probs[h][2])
                   for h in heads]
            o_ref[rs, :] = _bf(jnp.concatenate([jnp.where(valid, o_t[h] * inv_l[h], 0.0).T for h in heads], axis=1))
            lse_ref[:, rs] = jnp.concatenate(lse_l, axis=0)

    ck, cv = C_SK // 128, C_SV // 128
    return _call(
        body, "swa_fwd", (r_tot // qb,),
        [SMEM_SPEC, rows(qb, 512, C_SQ // 512, pair),
         rows(qb, 128, ck, pair), rows(SB, 128, ck, prev), rows(SB, 128, ck, meta),
         rows(qb, 128, cv, pair), rows(SB, 128, cv, prev), rows(SB, 128, cv, meta)],
        [rows(qb, 512, 0, pair), pl.BlockSpec((8, qb), lambda i: (0, i))],
        [jax.ShapeDtypeStruct((r_tot, 512), BF16), jax.ShapeDtypeStruct((8, r_tot), F32)],
        [], _cp(32), (sinks, proj, proj, proj, proj, proj, proj, proj), comm)


def _swa_bwd(proj, sinks, lse_t, do, t, comm=None):
    nsb = t // SB
    r_tot = t + TM
    nbq = _swa_steps(r_tot)
    qb = nbq * SB
    rows, pair, prev, meta = _swa_specs(nsb, nbq)
    hb = HB_BWD
    lanes = hb * SB

    def body(sink_ref, q_ref, kc_ref, kp_ref, km_ref, vc_ref, vp_ref, vm_ref, lse_ref, do_ref,
             dq_ref, dk_ref, dv_ref, dsink_ref):
        i = pl.program_id(0)

        @pl.when(i == 0)
        def _():
            dk_ref[...] = jnp.zeros_like(dk_ref)
            dv_ref[...] = jnp.zeros_like(dv_ref)
            dsink_ref[...] = jnp.zeros_like(dsink_ref)

        key = lax.broadcasted_iota(jnp.int32, (SB, lanes), 0)
        qry = lax.rem(lax.broadcasted_iota(jnp.int32, (SB, lanes), 1), SB)
        km, vm = km_ref[0:NM, :], vm_ref[0:NM, :]
        dsink_l = []
        for j in range(nbq):
            b = nbq * i + j
            rs = slice(j * SB, (j + 1) * SB)
            before = slice((j - 1) * SB, j * SB)
            real = b < nsb
            masks = (key <= qry, (key > qry) & (b > 0) & real, real)
            k3 = (kc_ref[rs, :], kp_ref[...] if j == 0 else kc_ref[before, :], km)
            v3 = (vc_ref[rs, :], vp_ref[...] if j == 0 else vc_ref[before, :], vm)
            groups = list(range(0, 8, hb))
            kvs = [h0 // 4 for h0 in groups]
            qg = [_bf(_stack(q_ref[rs, h0 * 64:(h0 + hb) * 64], hb)) for h0 in groups]
            dog = [_bf(_stack(do_ref[rs, h0 * 64:(h0 + hb) * 64], hb)) for h0 in groups]
            kb = [[_bf(k[:, kv * 64:(kv + 1) * 64]) for k in k3] for kv in kvs]
            vb = [[_bf(v[:, kv * 64:(kv + 1) * 64]) for v in v3] for kv in kvs]
            s = [[_dot_nt(k, qg[g]) for k in kb[g]] for g in range(len(groups))]
            dp = [[_dot_nt(v, dog[g]) for v in vb[g]] for g in range(len(groups))]
            p, ds, ds_blk = [], [], []
            for g, h0 in enumerate(groups):
                lse_row = jnp.concatenate([lse_ref[h:h + 1, rs] for h in range(h0, h0 + hb)], axis=1)
                sink_row = jnp.concatenate([jnp.full((1, SB), sink_ref[0, h], F32) for h in range(h0, h0 + hb)], axis=1)
                pg = [jnp.exp(jnp.where(m, sx, NEG) - lse_row) for m, sx in zip(masks, s[g])]
                delta = (jnp.sum(pg[0] * dp[g][0] + pg[1] * dp[g][1], axis=0, keepdims=True)
                         + jnp.sum(pg[2] * dp[g][2], axis=0, keepdims=True))
                ds.append([_bf(pp * (dd - delta)) for pp, dd in zip(pg, dp[g])])
                p.append([_bf(pp) for pp in pg])
                ds_row = -jnp.exp(sink_row - lse_row) * delta
                ds_blk += [jnp.sum(ds_row[:, q0 * SB:(q0 + 1) * SB], axis=1, keepdims=True) for q0 in range(hb)]
            dsink_l.append(jnp.concatenate(ds_blk, axis=1))
            dq_t = [_dot_tn(kb[g][0], ds[g][0]) + _dot_tn(kb[g][1], ds[g][1]) + _dot_tn(kb[g][2], ds[g][2])
                    for g in range(len(groups))]
            dq_ref[rs, :] = jnp.concatenate([_unstack(d.T, hb) for d in dq_t], axis=1)
            windows = (pl.ds(pl.multiple_of(b * SB, SB), SB), pl.ds(pl.multiple_of(jnp.maximum(b - 1, 0) * SB, SB), SB),
                       pl.ds(t, NM))
            for x in range(3):
                dk_kv, dv_kv = [], []
                for kv in range(2):
                    mine = [g for g in range(len(groups)) if kvs[g] == kv]
                    dk_kv.append(sum(_dot(ds[g][x], qg[g]) for g in mine))
                    dv_kv.append(sum(_dot(p[g][x], dog[g]) for g in mine))
                dk_ref[windows[x], :] += jnp.concatenate(dk_kv, axis=1)
                dv_ref[windows[x], :] += jnp.concatenate(dv_kv, axis=1)
        dsink_ref[...] += sum(dsink_l)

    ck, cv = C_SK // 128, C_SV // 128
    whole = lambda w: pl.BlockSpec((r_tot, w), lambda i: (0, 0))
    return _call(
        body, "swa_bwd", (r_tot // qb,),
        [SMEM_SPEC, rows(qb, 512, C_SQ // 512, pair),
         rows(qb, 128, ck, pair), rows(SB, 128, ck, prev), rows(SB, 128, ck, meta),
         rows(qb, 128, cv, pair), rows(SB, 128, cv, prev), rows(SB, 128, cv, meta),
         pl.BlockSpec((8, qb), lambda i: (0, i)), rows(qb, 512, 1, pair)],
        [rows(qb, 512, 0, pair), whole(128), whole(128), pl.BlockSpec((1, 8), lambda i: (0, 0))],
        [jax.ShapeDtypeStruct((r_tot, 512), F32), jax.ShapeDtypeStruct((r_tot, 128), F32),
         jax.ShapeDtypeStruct((r_tot, 128), F32), jax.ShapeDtypeStruct((1, 8), F32)],
        [], _cp(48), (sinks, proj, proj, proj, proj, proj, proj, proj, lse_t, do), comm)


HK = D // 2


def _mlp_fwd(x, metapad, tgt, ogla, oswa, wo, wff, w1, w2, wfin):
    t = x.shape[0]
    nblk = t // TM

    def body(x_ref, mp_ref, tgt_ref, og_ref, os_ref, wo_ref, wff_ref, w1a_ref, w1b_ref, w2a_ref, w2b_ref, wfin_ref,
             h1_ref, f_ref, a_ref, dh2_ref, loss_ref, gfin_ref):
        i = pl.program_id(0)

        @pl.when(i == 0)
        def _():
            loss_ref[...] = jnp.zeros_like(loss_ref)
            gfin_ref[...] = jnp.zeros_like(gfin_ref)

        h0 = jnp.where(i == nblk, mp_ref[...], x_ref[...])
        h1 = h0 + _dot(og_ref[...], wo_ref[0:512, :]) + _dot(os_ref[...], wo_ref[512:1024, :])
        h1_ref[...] = h1
        fh, _ = _rms(h1)
        f = _bf(fh * wff_ref[...])
        f_ref[...] = f
        acc = jnp.zeros((TM, D), F32)
        for n in range(4):
            a = _dot(f[:, 0:HK], w1a_ref[n]) + _dot(f[:, HK:D], w1b_ref[n])
            a_ref[:, n * D:(n + 1) * D] = _bf(a)
            zr = jnp.maximum(a, 0.0)
            z = _bf(zr * zr)
            acc = acc + _dot(z[:, 0:HK], w2a_ref[n]) + _dot(z[:, HK:D], w2b_ref[n])
        h2 = h1 + acc
        yh, rs2 = _rms(h2)
        wf = wfin_ref[...]
        real = i < nblk
        e = jnp.where(real, yh * wf - tgt_ref[...], 0.0)
        loss_ref[...] += jnp.sum(jnp.sum(e * e, axis=0, keepdims=True), axis=1, keepdims=True) * (0.5 / D)
        dy = e * (1.0 / D)
        gfin_ref[...] += jnp.sum(dy * yh, axis=0, keepdims=True)
        dh2_ref[...] = _rms_bwd(dy, yh, rs2, wf)

    xs = pl.BlockSpec((TM, D), lambda i: (jnp.minimum(i, nblk - 1), 0))
    rs = lambda w: pl.BlockSpec((TM, w), lambda i: (i, 0))
    r_tot = t + TM
    return pl.pallas_call(
        body, name="mlp_fwd", grid=(nblk + 1,),
        in_specs=[xs, VMEM_SPEC, xs, rs(512), rs(512)] + [VMEM_SPEC] * 7,
        out_specs=[rs(D), rs(D), rs(DFF), rs(D), pl.BlockSpec((1, 1), lambda i: (0, 0)), pl.BlockSpec((1, D), lambda i: (0, 0))],
        out_shape=[jax.ShapeDtypeStruct((r_tot, D), F32), jax.ShapeDtypeStruct((r_tot, D), BF16),
                   jax.ShapeDtypeStruct((r_tot, DFF), BF16), jax.ShapeDtypeStruct((r_tot, D), F32),
                   jax.ShapeDtypeStruct((1, 1), F32), jax.ShapeDtypeStruct((1, D), F32)],
        compiler_params=_cp(56),
    )(x, metapad, tgt, ogla, oswa, wo, wff, *w1, *w2, wfin)


def _mlp_bwd(h1, a, dh2, ogla, oswa, wo, wff, w1, w2):
    r_tot = h1.shape[0]
    nt = r_tot // TM

    def body(h1_ref, a_ref, dh2_ref, og_ref, os_ref, wo_ref, wff_ref, w1a_ref, w1b_ref, w2a_ref, w2b_ref,
             da_ref, dh2b_ref, dh1_ref, do_ref, dwo_ref, gff_ref, dwo_acc):
        i = pl.program_id(0)

        @pl.when(i == 0)
        def _():
            dwo_acc[...] = jnp.zeros_like(dwo_acc)
            gff_ref[...] = jnp.zeros_like(gff_ref)

        dh2 = dh2_ref[...]
        dh2b = _bf(dh2)
        dh2b_ref[...] = dh2b
        dfa = jnp.zeros((TM, HK), F32)
        dfb = jnp.zeros((TM, HK), F32)
        for n in range(4):
            dz = jnp.concatenate([_dot_nt(dh2b, w2a_ref[n]), _dot_nt(dh2b, w2b_ref[n])], axis=1)
            da = _bf(dz * (2.0 * jnp.maximum(a_ref[:, n * D:(n + 1) * D].astype(F32), 0.0)))
            da_ref[:, n * D:(n + 1) * D] = da
            dfa = dfa + _dot_nt(da, w1a_ref[n])
            dfb = dfb + _dot_nt(da, w1b_ref[n])
        df = jnp.concatenate([dfa, dfb], axis=1)
        fh, rs1 = _rms(h1_ref[...])
        gff_ref[...] += jnp.sum(df * fh, axis=0, keepdims=True)
        dh1 = dh2 + _rms_bwd(df, fh, rs1, wff_ref[...])
        dh1_ref[...] = dh1
        dh1b = _bf(dh1)
        do_ref[...] = _dot_nt(dh1b, wo_ref[...])
        dwo_acc[0:512, :] += _dot_tn(og_ref[...], dh1b)
        dwo_acc[512:1024, :] += _dot_tn(os_ref[...], dh1b)

        @pl.when(i == nt - 1)
        def _():
            for s in range(4):
                for hh in range(2):
                    dwo_ref[hh, s] = dwo_acc[(2 * s + hh) * 128:(2 * s + hh + 1) * 128, :]

    rs = lambda w: pl.BlockSpec((TM, w), lambda i: (i, 0))
    return pl.pallas_call(
        body, name="mlp_bwd", grid=(nt,),
        in_specs=[rs(D), rs(DFF), rs(D), rs(512), rs(512)] + [VMEM_SPEC] * 6,
        out_specs=[rs(DFF), rs(D), rs(D), rs(D), VMEM_SPEC, pl.BlockSpec((1, D), lambda i: (0, 0))],
        out_shape=[jax.ShapeDtypeStruct((r_tot, DFF), BF16), jax.ShapeDtypeStruct((r_tot, D), BF16),
                   jax.ShapeDtypeStruct((r_tot, D), F32), jax.ShapeDtypeStruct((r_tot, D), F32),
                   jax.ShapeDtypeStruct((2, 4, 128, D), F32), jax.ShapeDtypeStruct((1, D), F32)],
        scratch_shapes=[pltpu.VMEM((D, D), F32)],
        compiler_params=_cp(56),
    )(h1, a, dh2, ogla, oswa, wo, wff, *w1, *w2)


def _ffn_wgrad(f, a, da, dh2b):
    r_tot = f.shape[0]
    kt = 768 if r_tot % 768 == 0 else TM
    nk = r_tot // kt

    def body(f_ref, a_ref, da_ref, dh2_ref, dw1_ref, dw2_ref, acc1, acc2):
        k = pl.program_id(1)

        @pl.when(k == 0)
        def _():
            acc1[...] = jnp.zeros_like(acc1)
            acc2[...] = jnp.zeros_like(acc2)

        zr = jnp.maximum(a_ref[...], 0.0)
        acc1[...] += _dot_tn(f_ref[...], da_ref[...])
        acc2[...] += _dot_tn(zr * zr, dh2_ref[...])

        @pl.when(k == nk - 1)
        def _():
            for hh in range(2):
                dw1_ref[hh, 0] = acc1[hh * 512:(hh + 1) * 512, :]
                dw2_ref[hh, 0] = acc2[hh * 512:(hh + 1) * 512, :]

    out = pl.BlockSpec((2, 1, 512, D), lambda n, k: (0, n, 0, 0))
    return pl.pallas_call(
        body, name="ffn_wgrad", grid=(4, nk),
        in_specs=[pl.BlockSpec((kt, D), lambda n, k: (k, 0)), pl.BlockSpec((kt, D), lambda n, k: (k, n)),
                  pl.BlockSpec((kt, D), lambda n, k: (k, n)), pl.BlockSpec((kt, D), lambda n, k: (k, 0))],
        out_specs=[out, out],
        out_shape=[jax.ShapeDtypeStruct((2, 4, 512, D), F32)] * 2,
        scratch_shapes=[pltpu.VMEM((D, D), F32), pltpu.VMEM((D, D), F32)],
        compiler_params=_cp(48, ("arbitrary", "arbitrary")),
    )(f, a, da, dh2b)


def _proj_bwd(x, metapad, wm, wt3, tabs, dgla, dswa_q, dsk, dsv, dlr, dh1, comm=None):
    t = x.shape[0]
    nblk = t // TM

    def body(x_ref, mp_ref, wm_ref, w3_ref, tab_ref, dg_ref, dq_ref, dk_ref, dv_ref, dlr_ref, dh1_ref,
             gx_ref, gmeta_ref, dw_ref, gmix_ref, w_ref, acc):
        i = pl.program_id(0)

        @pl.when(i == 0)
        def _():
            _join_shards(w3_ref, w_ref)
            acc[...] = jnp.zeros_like(acc)
            gmix_ref[...] = jnp.zeros_like(gmix_ref)

        h = jnp.where(i == nblk, mp_ref[...], x_ref[...])
        uh, rs = _rms(h)
        wm_v = wm_ref[...]
        u = _bf(uh * wm_v)
        tab = tab_ref[...]
        dq = _bf(_rope(dq_ref[...] * 0.125, tab, -1.0))
        dk = _bf(_rope(dk_ref[...], tab, -1.0))
        parts = ((dg_ref[...], 0, R_LR), (dlr_ref[:, 0:16], R_LR, 16), (dq, R_LR + 16, 512),
                 (dk, R_LR + 528, 128), (_bf(dv_ref[...]), R_LR + 656, 128))
        du = jnp.zeros((TM, D), F32)
        for val, r0, w in parts:
            du = du + _dot(val, w_ref[r0:r0 + w, :])
            acc[r0:r0 + w, :] += _dot_tn(val, u)
        gmix_ref[...] += jnp.sum(du * uh, axis=0, keepdims=True)
        dh0 = dh1_ref[...] + _rms_bwd(du, uh, rs, wm_v)

        @pl.when(i < nblk)
        def _():
            gx_ref[...] = dh0

        @pl.when(i == nblk)
        def _():
            gmeta_ref[...] = dh0[:NM]
            for s in range(4):
                dw_ref[s] = acc[(DIN // 4) * s:(DIN // 4) * (s + 1), :]

    xs = pl.BlockSpec((TM, D), lambda i: (jnp.minimum(i, nblk - 1), 0))
    rs_ = lambda w: pl.BlockSpec((TM, w), lambda i: (i, 0))
    return _call(
        body, "proj_bwd", (nblk + 1,),
        [xs, VMEM_SPEC, VMEM_SPEC, VMEM_SPEC, rs_(128), rs_(1536), rs_(512), rs_(128), rs_(128), rs_(128), rs_(D)],
        [xs, pl.BlockSpec((NM, D), lambda i: (0, 0)), VMEM_SPEC, pl.BlockSpec((1, D), lambda i: (0, 0))],
        [jax.ShapeDtypeStruct((t, D), F32), jax.ShapeDtypeStruct((NM, D), F32),
         jax.ShapeDtypeStruct((4, DIN // 4, D), F32), jax.ShapeDtypeStruct((1, D), F32)],
        [pltpu.VMEM((DIN, D), BF16), pltpu.VMEM((DIN, D), F32)], _cp(56),
        (x, metapad, wm, wt3, tabs, dgla, dswa_q, dsk, dsv, dlr, dh1), comm)


def _place():
    return lax.axis_index("x"), lax.axis_index("y"), lax.axis_index("c")


def _other_chips(x, y):
    return [(1 - x, y), (x, 1 - y), (1 - x, 1 - y)]


def _dma_sems(*counts):
    return tuple(pltpu.SemaphoreType.DMA((k,)) for k in counts)


def _gather_shards(shards, split):
    n = len(shards)
    two = [a for a in range(n) if split[a]]

    def plan(ins, outs, sems):
        isend, irecv, dsend, drecv, loc = sems
        x, y, c = _place()
        chips = _other_chips(x, y)

        def part(ref, a, half):
            if not split[a]:
                return ref
            w = shards[a].shape[1] // 2
            return ref.at[:, pl.ds(pl.multiple_of(half * w, 128), w)]

        def over_ici(a, k, shard_of):
            tx, ty = chips[k]
            sx, sy = shard_of
            return pltpu.make_async_remote_copy(
                src_ref=part(ins[a], a, c), dst_ref=part(outs[a].at[2 * sx + sy], a, c), send_sem=isend.at[3 * a + k],
                recv_sem=irecv.at[3 * a + k], device_id=(tx, ty, c), device_id_type=MESH)

        def over_d2d(a, k, half):
            tx, ty = chips[k]
            ref = part(outs[a].at[2 * tx + ty], a, half)
            return pltpu.make_async_remote_copy(
                src_ref=ref, dst_ref=ref, send_sem=dsend.at[3 * a + k], recv_sem=drecv.at[3 * a + k],
                device_id=(x, y, 1 - c), device_id_type=MESH)

        def local(a):
            return pltpu.make_async_copy(ins[a], outs[a].at[2 * x + y], loc.at[a])

        pairs = [(a, k) for a in range(n) for k in range(3)]
        first = ([lambda a=a: local(a).start() for a in range(n)]
                 + [lambda a=a, k=k: over_ici(a, k, (x, y)).start() for a, k in pairs],
                 [lambda a=a, k=k: over_ici(a, k, chips[k]).wait_recv() for a, k in pairs]
                 + [lambda a=a, k=k: over_ici(a, k, (x, y)).wait_send() for a, k in pairs]
                 + [lambda a=a: local(a).wait() for a in range(n)])
        pairs2 = [(a, k) for a in two for k in range(3)]
        second = ([lambda a=a, k=k: over_d2d(a, k, c).start() for a, k in pairs2],
                  [lambda a=a, k=k: over_d2d(a, k, 1 - c).wait_recv() for a, k in pairs2]
                  + [lambda a=a, k=k: over_d2d(a, k, c).wait_send() for a, k in pairs2])
        return [first, second] if two else [first]

    return _Comm(tuple(shards), tuple(jax.ShapeDtypeStruct((4,) + s.shape, s.dtype) for s in shards),
                 _dma_sems(3 * n, 3 * n, 3 * n, 3 * n, n), 2 if two else 1, plan)


def _swap_halves(grads):
    n = len(grads)

    def plan(ins, outs, sems):
        send, recv = sems
        x, y, c = _place()

        def swap(a):
            return pltpu.make_async_remote_copy(
                src_ref=ins[a].at[1 - c], dst_ref=outs[a], send_sem=send.at[a], recv_sem=recv.at[a],
                device_id=(x, y, 1 - c), device_id_type=MESH)

        return [([lambda a=a: swap(a).start() for a in range(n)], [lambda a=a: swap(a).wait() for a in range(n)])]

    return _Comm(tuple(grads), tuple(jax.ShapeDtypeStruct(g.shape[1:], g.dtype) for g in grads), _dma_sems(n, n), 1, plan)


def _scatter_shards(parts):
    n = len(parts)

    def plan(ins, outs, sems):
        send, recv = sems
        x, y, c = _place()
        chips = _other_chips(x, y)

        def scatter(a, k):
            tx, ty = chips[k]
            return pltpu.make_async_remote_copy(
                src_ref=ins[a].at[2 * tx + ty], dst_ref=outs[a].at[k], send_sem=send.at[3 * a + k],
                recv_sem=recv.at[3 * a + k], device_id=(tx, ty, c), device_id_type=MESH)

        pairs = [(a, k) for a in range(n) for k in range(3)]
        return [([lambda a=a, k=k: scatter(a, k).start() for a, k in pairs],
                 [lambda a=a, k=k: scatter(a, k).wait() for a, k in pairs])]

    return _Comm(tuple(parts), tuple(jax.ShapeDtypeStruct((3,) + p.shape[1:], p.dtype) for p in parts),
                 _dma_sems(3 * n, 3 * n), 1, plan)


def _join_halves(halves):
    n = len(halves)

    def plan(ins, outs, sems):
        send, recv, loc = sems
        x, y, c = _place()

        def remote(a, half):
            return pltpu.make_async_remote_copy(
                src_ref=ins[a], dst_ref=outs[a].at[half], send_sem=send.at[a], recv_sem=recv.at[a],
                device_id=(x, y, 1 - c), device_id_type=MESH)

        def local(a):
            return pltpu.make_async_copy(ins[a], outs[a].at[c], loc.at[a])

        every = range(n)
        return [([lambda a=a: local(a).start() for a in every] + [lambda a=a: remote(a, c).start() for a in every],
                 [lambda a=a: remote(a, 1 - c).wait_recv() for a in every]
                 + [lambda a=a: remote(a, c).wait_send() for a in every] + [lambda a=a: local(a).wait() for a in every])]

    return _Comm(tuple(halves), tuple(jax.ShapeDtypeStruct((2,) + h.shape, h.dtype) for h in halves),
                 _dma_sems(n, n, n), 1, plan)


def _reduce_w_in(dwt, comm):
    rows, hw = DIN // 4, D // 2
    ci, co = len(comm.ins), len(comm.outs)

    def body(*refs):
        dw_ref, c_in, out_ref, c_out = refs[0], refs[1:1 + ci], refs[1 + ci], refs[2 + ci:2 + ci + co]
        mine, sib, tosend, rbuf, qbuf, full, send, recv, loc = refs[2 + ci + co:11 + ci + co]
        c_sem = refs[11 + ci + co:]
        x, y, c = _place()
        sibling = (x, y, 1 - c)
        (starts, waits), = comm.plan(c_in, c_out, c_sem)
        _run_phase(starts)

        def cols(ref, half):
            window = pl.ds(pl.multiple_of(half * hw, 128), hw)
            return ref.at[:, :, window] if len(ref.shape) == 3 else ref.at[:, window]

        load = pltpu.make_async_copy(cols(dw_ref, c), mine, loc.at[0])
        give = pltpu.make_async_remote_copy(src_ref=cols(dw_ref, 1 - c), dst_ref=sib, send_sem=send.at[3], recv_sem=recv.at[3],
                                            device_id=sibling, device_id_type=MESH)
        load.start()
        give.start()
        load.wait()
        give.wait()
        mine[...] = mine[...] + sib[...]
        cps = []
        for k, (tx, ty) in enumerate(_other_chips(x, y)):
            tosend[k] = _bf(mine[2 * tx + ty])
            cps.append(pltpu.make_async_remote_copy(
                src_ref=tosend.at[k], dst_ref=rbuf.at[k], send_sem=send.at[k], recv_sem=recv.at[k],
                device_id=(tx, ty, c), device_id_type=MESH))
            cps[-1].start()
        for cp in cps:
            cp.wait()
        qbuf[...] = mine[2 * x + y] + rbuf[0].astype(F32) + rbuf[1].astype(F32) + rbuf[2].astype(F32)
        keep = pltpu.make_async_copy(qbuf, cols(full, c), loc.at[1])
        pass_on = pltpu.make_async_remote_copy(src_ref=qbuf, dst_ref=cols(full, c), send_sem=send.at[4], recv_sem=recv.at[4],
                                               device_id=sibling, device_id_type=MESH)
        keep.start()
        pass_on.start()
        keep.wait()
        pass_on.wait_send()
        pltpu.make_async_remote_copy(src_ref=qbuf, dst_ref=cols(full, 1 - c), send_sem=send.at[4], recv_sem=recv.at[4],
                                     device_id=sibling, device_id_type=MESH).wait_recv()
        out_ref[...] = full[...]
        _run_phase(waits)

    outs = pl.pallas_call(
        body, name="reduce_w_in",
        in_specs=[ANY_SPEC] * (1 + ci), out_specs=[VMEM_SPEC] + [ANY_SPEC] * co,
        out_shape=[jax.ShapeDtypeStruct((rows, D), F32)] + list(comm.outs),
        scratch_shapes=[pltpu.VMEM((4, rows, hw), F32), pltpu.VMEM((4, rows, hw), F32), pltpu.VMEM((3, rows, hw), BF16),
                        pltpu.VMEM((3, rows, hw), BF16), pltpu.VMEM((rows, hw), F32), pltpu.VMEM((rows, D), F32),
                        *_dma_sems(5, 5, 2), *comm.sems],
        compiler_params=pltpu.CompilerParams(vmem_limit_bytes=48 << 20),
    )(dwt, *comm.ins)
    return outs[0], outs[1:]


def _allreduce_small(pack):
    p = pack.shape[0]

    def body(in_ref, out_ref, buf, send, recv):
        x, y, c = _place()
        me = 4 * x + 2 * y + c
        buf[me] = in_ref[...]

        def peer_of(k):
            return x ^ (k >> 2), y ^ ((k >> 1) & 1), c ^ (k & 1)

        sends = [pltpu.make_async_remote_copy(
            src_ref=in_ref, dst_ref=buf.at[me], send_sem=send.at[k - 1], recv_sem=recv.at[k - 1],
            device_id=peer_of(k), device_id_type=MESH) for k in range(1, 8)]
        for cp in sends:
            cp.start()
        for k in range(1, 8):
            px, py, pc = peer_of(k)
            pltpu.make_async_remote_copy(
                src_ref=in_ref, dst_ref=buf.at[4 * px + 2 * py + pc], send_sem=send.at[k - 1], recv_sem=recv.at[k - 1],
                device_id=(x, y, c), device_id_type=MESH).wait_recv()
        for cp in sends:
            cp.wait_send()
        acc = buf[0]
        for d in range(1, 8):
            acc = acc + buf[d]
        out_ref[...] = acc

    return pl.pallas_call(
        body, name="allreduce_small",
        in_specs=[VMEM_SPEC], out_specs=VMEM_SPEC, out_shape=jax.ShapeDtypeStruct(pack.shape, F32),
        scratch_shapes=[pltpu.VMEM((8, p, D), F32), *_dma_sems(7, 7)],
    )(pack)


GRID4 = 4


def _sum_cores(core_shard, mine, theirs):
    n = len(mine)

    def body(cs_ref, *refs):
        ms, ts, bfs, owns = refs[:n], refs[n:2 * n], refs[2 * n:3 * n], refs[3 * n:]
        keep = pl.program_id(0) == cs_ref[1]
        for a in range(n):
            acc = ms[a][0, 0] + ts[a][0]
            bfs[a][0] = _bf(acc)

            @pl.when(keep)
            def _():
                owns[a][...] = acc

    shapes = [m.shape[2:] for m in mine]
    in_specs = ([pl.BlockSpec((1, 1) + s, lambda i, cs: (cs[0], i, 0, 0)) for s in shapes]
                + [pl.BlockSpec((1,) + s, lambda i, cs: (i, 0, 0)) for s in shapes])
    out_specs = ([pl.BlockSpec((1,) + s, lambda i, cs: (i, 0, 0)) for s in shapes]
                 + [pl.BlockSpec(s, lambda i, cs: (0, 0)) for s in shapes])
    outs = pl.pallas_call(
        body, name="sum_cores",
        grid_spec=pltpu.PrefetchScalarGridSpec(num_scalar_prefetch=1, grid=(4,), in_specs=in_specs, out_specs=out_specs),
        out_shape=[jax.ShapeDtypeStruct((4,) + s, BF16) for s in shapes] + [jax.ShapeDtypeStruct(s, F32) for s in shapes],
        compiler_params=_cp(48),
    )(core_shard, *mine, *theirs)
    return outs[:n], outs[n:]


def _sum_chips(own, arrived):
    n = len(own)

    def body(*refs):
        os_, ars, outs = refs[:n], refs[n:2 * n], refs[2 * n:]
        for a in range(n):
            outs[a][...] = os_[a][...] + ars[a][0].astype(F32) + ars[a][1].astype(F32) + ars[a][2].astype(F32)

    blocks = [(o.shape[0] // GRID4, o.shape[1]) for o in own]
    return pl.pallas_call(
        body, name="sum_chips", grid=(GRID4,),
        in_specs=([pl.BlockSpec(b, lambda i: (i, 0)) for b in blocks]
                  + [pl.BlockSpec((3,) + b, lambda i: (0, i, 0)) for b in blocks]),
        out_specs=[pl.BlockSpec(b, lambda i: (i, 0)) for b in blocks],
        out_shape=[jax.ShapeDtypeStruct(o.shape, F32) for o in own],
        compiler_params=_cp(32),
    )(*own, *arrived)


def _adamw_math(w, g, m, v):
    m2 = ADAM_B1 * m + (1.0 - ADAM_B1) * g
    v2 = ADAM_B2 * v + (1.0 - ADAM_B2) * (g * g)
    m_hat = m2 / (1.0 - ADAM_B1 ** ADAM_STEP)
    v_hat = v2 / (1.0 - ADAM_B2 ** ADAM_STEP)
    return -ADAM_LR * (m_hat / (jnp.sqrt(v_hat) + ADAM_EPS) + ADAM_WD * w), m2, v2


def _adamw_big(ws, gs, ms, vs):
    n = len(ws)

    def body(*refs):
        for a in range(n):
            d, m2, v2 = _adamw_math(refs[a][...], refs[n + a][...], refs[2 * n + a][...], refs[3 * n + a][...])
            refs[4 * n + a][...] = d
            refs[5 * n + a][...] = m2
            refs[6 * n + a][...] = v2

    specs = [pl.BlockSpec((w.shape[0] // GRID4, w.shape[1]), lambda i: (i, 0)) for w in ws]
    return pl.pallas_call(
        body, name="adamw_big", grid=(GRID4,),
        in_specs=specs * 4, out_specs=specs * 3,
        out_shape=[jax.ShapeDtypeStruct(w.shape, F32) for w in ws] * 3,
        compiler_params=_cp(48),
    )(*ws, *gs, *ms, *vs)


def _adamw_small(ws, gs, ms, vs):
    n = len(ws)

    def body(*refs):
        for a in range(n):
            d, m2, v2 = _adamw_math(refs[a][...], refs[n + a][...], refs[2 * n + a][...], refs[3 * n + a][...])
            refs[4 * n + a][...] = d
            refs[5 * n + a][...] = m2
            refs[6 * n + a][...] = v2

    return pl.pallas_call(
        body, name="adamw_small",
        in_specs=[VMEM_SPEC] * (4 * n), out_specs=[VMEM_SPEC] * (3 * n),
        out_shape=[jax.ShapeDtypeStruct(w.shape, F32) for w in ws] * 3,
        compiler_params=pltpu.CompilerParams(vmem_limit_bytes=40 << 20),
    )(*ws, *gs, *ms, *vs)


def kernel(x, meta_tokens, norm_mix_w, w_in, w_gate_up, b_gate, gla_norm_w, sinks, w_out, norm_ff_w, w_ff1, w_ff2, final_norm_w, loss_target, m_meta_tokens, m_norm_mix_w, m_w_in, m_w_gate_up, m_b_gate, m_gla_norm_w, m_sinks, m_w_out, m_norm_ff_w, m_w_ff1, m_w_ff2, m_final_norm_w, v_meta_tokens, v_norm_mix_w, v_w_in, v_w_gate_up, v_b_gate, v_gla_norm_w, v_sinks, v_w_out, v_norm_ff_w, v_w_ff1, v_w_ff2, v_final_norm_w):
    xi, yi, ci = _place()
    shard = (2 * xi + yi).astype(jnp.int32).reshape(1)
    core = ci.astype(jnp.int32).reshape(1)

    small = jnp.concatenate([meta_tokens, w_gate_up[0], jnp.zeros((NM, 64), F32)], axis=1)
    wt3, g_small = _run_comm(_gather_shards([_bf(w_in[0].T), small], [True, False]), "gather_w_in")
    meta = g_small[:, :, 0:256].transpose(1, 0, 2).reshape(NM, D)
    wgu = g_small[:, :, 256:320].transpose(1, 0, 2).reshape(NM, 256)

    xs, tgt = x[0], loss_target[0]
    t = xs.shape[0]
    wfin = final_norm_w.reshape(1, D)
    metapad = jnp.concatenate([meta, jnp.zeros((TM - NM, D), F32)], axis=0)
    wgu_p = _bf(jnp.concatenate([wgu, jnp.zeros((128 - 16, 256), F32)], axis=0))
    tabs = _rope_tables(t)

    w1s, w2s = _bf(w_ff1[0]), _bf(w_ff2[0])
    proj, (g_out, w1a) = _proj_fwd(xs, metapad, norm_mix_w, wt3, tabs,
                                   _gather_shards([_bf(w_out[0]), w1s[:HK]], [True] * 2))
    (ogla, oraw, sst, bcum, dgate), (w1b, w2a) = _gla_fwd(proj, wgu_p, b_gate, gla_norm_w, t,
                                                          _gather_shards([w1s[HK:], w2s[:HK]], [True] * 2))
    (oswa, lse), (w2b,) = _swa_fwd(proj, sinks, t, _gather_shards([w2s[HK:]], [True]))
    wo, w1, w2 = g_out.reshape(D, D), (w1a, w1b), (w2a, w2b)
    h1, f, a, dh2, loss, gfin = _mlp_fwd(xs, metapad, tgt, ogla, oswa, wo, norm_ff_w, w1, w2, wfin)

    da, dh2b, dh1, do, dwo, gff = _mlp_bwd(h1, a, dh2, ogla, oswa, wo, norm_ff_w, w1, w2)
    dw1, dw2 = _ffn_wgrad(f, a, da, dh2b)
    big = [dwo, dw1, dw2]
    (dgla, dlr, dwgu, dbg, dgnw), theirs = _gla_bwd(proj, oraw, sst, bcum, dgate, do, wgu_p, gla_norm_w, t,
                                                    _swap_halves(big))
    sums_bf, own = _sum_cores(jnp.concatenate([core, shard]), big, theirs)
    (dsq, dsk, dsv, dsink), arrived = _swa_bwd(proj, sinks, lse, do, t, _scatter_shards(sums_bf))
    halves = _sum_chips(own, arrived)
    (gx, gmeta, dwt, gmix), _ = _proj_bwd(xs, metapad, norm_mix_w, wt3, tabs, dgla, dsq, dsk, dsv, dlr, dh1)

    gwt_in, joined = _reduce_w_in(dwt, _join_halves(halves))
    gw_out, gw_1, gw_2 = [j.reshape((-1, j.shape[2])) for j in joined]

    tail = jnp.concatenate([dbg, dgnw, dsink, loss, jnp.zeros((1, D - 256 - 128 - 8 - 1), F32)], axis=1)
    pack = jnp.concatenate([gmeta, gmix, gff, gfin, tail, dwgu[:16].reshape(4, D)], axis=0)
    tot = _allreduce_small(pack)
    g_meta = lax.dynamic_slice_in_dim(tot[0:NM], shard[0] * 256, 256, axis=1)
    g_mix, g_ff, g_fin = tot[16:17], tot[17:18], tot[18]
    g_bg, g_gnw, g_sinks, loss_tot = tot[19:20, 0:256], tot[19:20, 256:384], tot[19:20, 384:392], tot[19, 392]
    g_wgu = lax.dynamic_slice_in_dim(tot[20:24].reshape(NM, 256), shard[0] * 64, 64, axis=1)

    bo = _adamw_big([w_out[0], w_ff1[0], w_ff2[0]], [gw_out, gw_1, gw_2], [m_w_out[0], m_w_ff1[0], m_w_ff2[0]],
                    [v_w_out[0], v_w_ff1[0], v_w_ff2[0]])

    fin2 = lambda a: a.reshape(1, D)
    sw = [meta_tokens, norm_mix_w, w_gate_up[0], b_gate, gla_norm_w, sinks, norm_ff_w, fin2(final_norm_w), w_in[0].T]
    sg = [g_meta, g_mix, g_wgu, g_bg, g_gnw, g_sinks, g_ff, fin2(g_fin), gwt_in]
    sm = [m_meta_tokens, m_norm_mix_w, m_w_gate_up[0], m_b_gate, m_gla_norm_w, m_sinks, m_norm_ff_w, fin2(m_final_norm_w),
          m_w_in[0].T]
    sv = [v_meta_tokens, v_norm_mix_w, v_w_gate_up[0], v_b_gate, v_gla_norm_w, v_sinks, v_norm_ff_w, fin2(v_final_norm_w),
          v_w_in[0].T]
    so = _adamw_small(sw, sg, sm, sv)

    def ordered(small_o, big_o):
        meta_, mix_, wgu_, bg_, gnw_, sinks_, ff_, fin_, wt_ = small_o
        w_out_, w_1_, w_2_ = big_o
        return (meta_, mix_, wt_.T[None], wgu_[None], bg_, gnw_, sinks_, w_out_[None], ff_, w_1_[None], w_2_[None],
                fin_.reshape(D))

    grads = ordered(sg, [gw_out, gw_1, gw_2])
    deltas = ordered(so[0:9], bo[0:3])
    new_m = ordered(so[9:18], bo[3:6])
    new_v = ordered(so[18:27], bo[6:9])
    return (loss_tot, gx[None], *grads, *deltas, *new_m, *new_v)
```

```python
import functools
from typing import Callable, NamedTuple

import jax
import jax.numpy as jnp
import numpy as np
from jax import lax
from jax.experimental import pallas as pl
from jax.experimental.pallas import tpu as pltpu

F32 = jnp.float32
BF16 = jnp.bfloat16

D = 1024
DFF = 4096
NM = 16
TM = 256
DK = 64
CH = 128
SB = 128
EPS = 1e-5
C_GQ, C_GK, C_GV, C_GR, C_SQ, C_SK, C_SV, C_LR, DINP = 0, 256, 512, 1024, 1536, 2048, 2176, 2304, 2432
DIN = 2320
R_LR = 1536
ROPE_THETA = 500000.0
ADAM_LR, ADAM_B1, ADAM_B2, ADAM_EPS, ADAM_WD, ADAM_STEP = 0.001, 0.9, 0.999, 1e-08, 0.01, 10
NEG = -1e30
MESH = pl.DeviceIdType.MESH
VMEM_SPEC = pl.BlockSpec(memory_space=pltpu.VMEM)
ANY_SPEC = pl.BlockSpec(memory_space=pl.ANY)
SMEM_SPEC = pl.BlockSpec(memory_space=pltpu.SMEM)


def _cp(vmem_mb, sem=("arbitrary",)):
    return pltpu.CompilerParams(dimension_semantics=sem, vmem_limit_bytes=vmem_mb << 20)


def _dot(a, b):
    return jnp.dot(a, b, preferred_element_type=F32)


def _dot_nt(a, b):
    return lax.dot_general(a, b, (((1,), (1,)), ((), ())), preferred_element_type=F32)


def _dot_tn(a, b):
    return lax.dot_general(a, b, (((0,), (0,)), ((), ())), preferred_element_type=F32)


def _bf(x):
    return x.astype(BF16)


def _dot3(m01, x):
    x1 = _bf(x)
    r1 = x - x1.astype(F32)
    x2 = _bf(r1)
    x3 = _bf(r1 - x2.astype(F32))
    return _dot(m01, x1) + _dot(m01, x2) + _dot(m01, x3)


def _rms(h):
    rs = lax.rsqrt(jnp.mean(h * h, axis=-1, keepdims=True) + EPS)
    return h * rs, rs


def _rms_bwd(dy, yhat, rs, w):
    dyh = dy * w
    return rs * (dyh - yhat * jnp.mean(dyh * yhat, axis=-1, keepdims=True))


class _Comm(NamedTuple):
    ins: tuple
    outs: tuple
    sems: tuple
    phases: int
    plan: Callable
    late: int = 0


def _run_phase(fns):
    for fn in fns:
        fn()


def _call(body, name, grid, in_specs, out_specs, out_shape, scratch, params, args, comm=None):
    if comm is None:
        outs = pl.pallas_call(body, name=name, grid=grid, in_specs=in_specs, out_specs=out_specs, out_shape=out_shape,
                              scratch_shapes=scratch, compiler_params=params)(*args)
        return outs, None
    n_in, n_out, n_scr = len(in_specs), len(out_specs), len(scratch)
    ci, co = len(comm.ins), len(comm.outs)
    last = grid[0] - 1
    marks = [0, max(1, last - (comm.late or max(2, (last + 1) // 6)))][:comm.phases]

    def wrapped(*refs):
        own_in, c_in = refs[:n_in], refs[n_in:n_in + ci]
        refs = refs[n_in + ci:]
        own_out, c_out = refs[:n_out], refs[n_out:n_out + co]
        refs = refs[n_out + co:]
        own_scr, c_sem = refs[:n_scr], refs[n_scr:]
        i = pl.program_id(0)

        for p, mark in enumerate(marks):
            @pl.when(i == mark)
            def _():
                plan = comm.plan(c_in, c_out, c_sem)
                if p > 0:
                    _run_phase(plan[p - 1][1])
                _run_phase(plan[p][0])

        body(*own_in, *own_out, *own_scr)

        @pl.when(i == last)
        def _():
            _run_phase(comm.plan(c_in, c_out, c_sem)[-1][1])

    outs = pl.pallas_call(
        wrapped, name=name, grid=grid, in_specs=list(in_specs) + [ANY_SPEC] * ci, out_specs=list(out_specs) + [ANY_SPEC] * co,
        out_shape=list(out_shape) + list(comm.outs), scratch_shapes=list(scratch) + list(comm.sems), compiler_params=params,
    )(*args, *comm.ins)
    return outs[:n_out], outs[n_out:]


def _run_comm(comm, name):
    ci, co = len(comm.ins), len(comm.outs)

    def body(*refs):
        for starts, waits in comm.plan(refs[:ci], refs[ci:ci + co], refs[ci + co:]):
            _run_phase(starts)
            _run_phase(waits)

    return pl.pallas_call(body, name=name, in_specs=[ANY_SPEC] * ci, out_specs=[ANY_SPEC] * co, out_shape=list(comm.outs),
                          scratch_shapes=list(comm.sems))(*comm.ins)


def _join_shards(w3_ref, w_ref):
    for s in range(4):
        w_ref[(DIN // 4) * s:(DIN // 4) * (s + 1), :] = w3_ref[s]


def _proj_fwd(x, metapad, wm, wt3, tabs, comm=None):
    t = x.shape[0]
    nblk = t // TM

    def body(x_ref, mp_ref, wm_ref, w3_ref, tab_ref, proj_ref, w_ref):
        i = pl.program_id(0)

        @pl.when(i == 0)
        def _():
            _join_shards(w3_ref, w_ref)

        h = jnp.where(i == nblk, mp_ref[...], x_ref[...])
        u, _ = _rms(h)
        ub = _bf(u * wm_ref[...])
        proj_ref[:, 0:C_SQ] = _dot_nt(ub, w_ref[0:R_LR, :])
        att = _dot_nt(ub, w_ref[R_LR + 16:DIN, :])
        tab = tab_ref[...]
        proj_ref[:, C_SQ:C_SK] = _rope(att[:, 0:512], tab, 1.0) * 0.125
        proj_ref[:, C_SK:C_SV] = _rope(att[:, 512:640], tab, 1.0)
        proj_ref[:, C_SV:C_LR] = att[:, 640:768]
        proj_ref[:, C_LR:DINP] = jnp.zeros((TM, DINP - C_LR), F32)
        proj_ref[:, C_LR:C_LR + 16] = _dot_nt(ub, w_ref[R_LR:R_LR + 16, :])

    (proj,), got = _call(
        body, "proj_fwd", (nblk + 1,),
        [pl.BlockSpec((TM, D), lambda i: (jnp.minimum(i, nblk - 1), 0)), VMEM_SPEC, VMEM_SPEC, VMEM_SPEC,
         pl.BlockSpec((TM, 128), lambda i: (i, 0))],
        [pl.BlockSpec((TM, DINP), lambda i: (i, 0))], [jax.ShapeDtypeStruct((t + TM, DINP), F32)],
        [pltpu.VMEM((DIN, D), BF16)], _cp(48), (x, metapad, wm, wt3, tabs), comm)
    return proj, got


def _chunk_masks():
    r = lax.broadcasted_iota(jnp.int32, (TM, TM), 0)
    c = lax.broadcasted_iota(jnp.int32, (TM, TM), 1)
    same = (r // CH) == (c // CH)
    lower = _bf(jnp.where(same & (c <= r), 1.0, 0.0))
    upper = _bf(jnp.where(same & (c >= r), 1.0, 0.0))
    return lower, upper


def _gla_gate(lr, wgu, bg, valid, lower):
    z = _dot(_bf(lr), wgu) + bg
    g = (jnp.minimum(z, 0.0) - jnp.log(1.0 + jnp.exp(-jnp.abs(z)))) * (1.0 / 16.0)
    g = jnp.where(valid, g, 0.0)
    return z, _dot3(lower, g)


def _gla_decays(q, k, b):
    nc = TM // CH
    b3 = b.reshape(nc, CH, 256)
    blast = b3[:, CH - 1:CH, :]
    eb = jnp.exp(b)
    enb = jnp.exp(-b)
    ebl = jnp.exp(blast - b3).reshape(TM, 256)
    return eb, enb, ebl, jnp.exp(blast)


def _tri(lower_incl):
    r = lax.broadcasted_iota(jnp.int32, (CH, CH), 0)
    c = lax.broadcasted_iota(jnp.int32, (CH, CH), 1)
    return ((c <= r) if lower_incl else (c >= r))[None]


def _gla_fwd(proj, wgu, bg, gnw, t, comm=None):
    nblk = t // TM
    nt = nblk + 1
    nc = TM // CH

    def blk(i):
        return (i + nblk) % nt

    def body(q_ref, k_ref, v_ref, r_ref, lr_ref, wgu_ref, bg_ref, gnw_ref, o_ref, oraw_ref, sst_ref, b_ref, dgate_ref,
             st_scr):
        i = pl.program_id(0)

        @pl.when(i == 0)
        def _():
            st_scr[...] = jnp.zeros_like(st_scr)

        rows = blk(i) * TM + lax.broadcasted_iota(jnp.int32, (TM, 1), 0)
        lower, _ = _chunk_masks()
        valid = rows < t + NM
        z, b = _gla_gate(lr_ref[...], wgu_ref[...], bg_ref[...], valid, lower)
        b_ref[...] = b
        dgate_ref[...] = jnp.where(valid, (1.0 / 16.0) / (1.0 + jnp.exp(z)), 0.0)
        q = q_ref[...]
        k = k_ref[...]
        eb, enb, ebl, eblast = _gla_decays(q, k, b)
        qt = q * 0.125 * eb
        kt = k * enb
        kh = k * ebl
        tril = _tri(True)
        heads = range(4)
        hs = [slice(h * DK, (h + 1) * DK) for h in heads]
        qh = [_bf(qt[:, hs[h]]).reshape(nc, CH, DK) for h in heads]
        kth = [_bf(kt[:, hs[h]]).reshape(nc, CH, DK) for h in heads]
        khh = [_bf(kh[:, hs[h]]).reshape(nc, CH, DK) for h in heads]
        vh = [_bf(v_ref[:, h * 128:(h + 1) * 128]).reshape(nc, CH, 128) for h in heads]
        a = [jnp.einsum('cid,cjd->cij', qh[h], kth[h], preferred_element_type=F32) for h in heads]
        kv = [jnp.einsum('cjv,cjd->cvd', vh[h], khh[h], preferred_element_type=F32) for h in heads]
        o = [jnp.einsum('cij,cjv->civ', _bf(jnp.where(tril, a[h], 0.0)), vh[h], preferred_element_type=F32) for h in heads]
        states = []
        for h in heads:
            st = st_scr[h]
            per_chunk = []
            for c in range(nc):
                sst_ref[c, h] = st
                per_chunk.append(_bf(st))
                st = st * eblast[c, :, hs[h]] + kv[h][c]
            st_scr[h] = st
            states.append(per_chunk)
        o_inter = [[_dot_nt(qh[h][c], states[h][c]) for c in range(nc)] for h in heads]
        oraw = jnp.concatenate([(o[h] + jnp.stack(o_inter[h])).reshape(TM, 128) for h in heads], axis=1)
        oraw_ref[...] = oraw
        gn = gnw_ref[...]
        res = []
        for h in range(4):
            on, _ = _rms(oraw[:, h * 128:(h + 1) * 128])
            r = r_ref[:, h * 128:(h + 1) * 128]
            res.append(on * gn * (r * jax.nn.sigmoid(r)))
        o_ref[...] = _bf(jnp.concatenate(res, axis=1))

    def spec(w, cb):
        return pl.BlockSpec((TM, w), lambda i: (blk(i), cb))

    return _call(
        body, "gla_fwd", (nt,),
        [spec(256, 0), spec(256, 1), spec(512, 1), spec(512, 2), spec(128, C_LR // 128), VMEM_SPEC, VMEM_SPEC, VMEM_SPEC],
        [spec(512, 0), spec(512, 0), pl.BlockSpec((nc, 4, 128, DK), lambda i: (blk(i), 0, 0, 0)), spec(256, 0), spec(256, 0)],
        [jax.ShapeDtypeStruct((t + TM, 512), BF16), jax.ShapeDtypeStruct((t + TM, 512), F32),
         jax.ShapeDtypeStruct((nt * nc, 4, 128, DK), F32), jax.ShapeDtypeStruct((t + TM, 256), F32),
         jax.ShapeDtypeStruct((t + TM, 256), F32)],
        [pltpu.VMEM((4, 128, DK), F32)], _cp(40), (proj, proj, proj, proj, proj, wgu, bg, gnw), comm)


def _gla_bwd(proj, oraw, sst, bcum, dgate, do, wgu, gnw, t, comm=None):
    nblk = t // TM
    nt = nblk + 1
    nc = TM // CH

    def blk(i):
        return (2 * nblk - i) % nt

    def body(q_ref, k_ref, v_ref, r_ref, lr_ref, oraw_ref, sst_ref, b_ref, dgate_ref, do_ref, wgu_ref, gnw_ref,
             dgla_ref, dlr_ref, dwgu_ref, dbg_ref, dgnw_ref, dst_scr):
        i = pl.program_id(0)

        @pl.when(i == 0)
        def _():
            dst_scr[...] = jnp.zeros_like(dst_scr)
            dwgu_ref[...] = jnp.zeros_like(dwgu_ref)
            dbg_ref[...] = jnp.zeros_like(dbg_ref)
            dgnw_ref[...] = jnp.zeros_like(dgnw_ref)

        _, upper = _chunk_masks()
        lr = lr_ref[...]
        b = b_ref[...]
        q = q_ref[...]
        k = k_ref[...]
        eb, enb, ebl, eblast = _gla_decays(q, k, b)
        qt = q * 0.125 * eb
        kt = k * enb
        kh = k * ebl
        gn = gnw_ref[...]
        tril = _tri(True)
        triu = _tri(False)
        heads = range(4)
        hs = [slice(h * DK, (h + 1) * DK) for h in heads]
        vs = [slice(h * 128, (h + 1) * 128) for h in heads]
        ein = functools.partial(jnp.einsum, preferred_element_type=F32)
        dr_l, doh = [], []
        dgn = jnp.zeros((1, 128), F32)
        for h in heads:
            on, rs = _rms(oraw_ref[:, vs[h]])
            r = r_ref[:, vs[h]]
            sig = jax.nn.sigmoid(r)
            sil = r * sig
            dy = do_ref[:, vs[h]]
            dr_l.append(dy * on * gn * (sig * (1.0 + r * (1.0 - sig))))
            dgn = dgn + jnp.sum(dy * sil * on, axis=0, keepdims=True)
            doh.append(_bf(_rms_bwd(dy * sil, on, rs, gn)).reshape(nc, CH, 128))
        dgnw_ref[...] += dgn
        qh = [_bf(qt[:, hs[h]]).reshape(nc, CH, DK) for h in heads]
        kth = [_bf(kt[:, hs[h]]).reshape(nc, CH, DK) for h in heads]
        khh = [_bf(kh[:, hs[h]]).reshape(nc, CH, DK) for h in heads]
        vh = [_bf(v_ref[:, vs[h]]).reshape(nc, CH, 128) for h in heads]
        at = [ein('cjd,cid->cji', kth[h], qh[h]) for h in heads]
        da = [ein('civ,cjv->cij', doh[h], vh[h]) for h in heads]
        dat = [ein('cjv,civ->cji', vh[h], doh[h]) for h in heads]
        gq = [ein('civ,cid->cvd', doh[h], qh[h]) for h in heads]
        stf = [sst_ref[:, h] for h in heads]
        dqs = [ein('civ,cvd->cid', doh[h], _bf(stf[h])) for h in heads]
        dv = [ein('cji,civ->cjv', _bf(jnp.where(triu, at[h], 0.0)), doh[h]) for h in heads]
        dqt = [ein('cij,cjd->cid', _bf(jnp.where(tril, da[h], 0.0)), kth[h]) + dqs[h] for h in heads]
        dkt = [ein('cji,cid->cjd', _bf(jnp.where(triu, dat[h], 0.0)), qh[h]) for h in heads]
        dse = []
        for h in heads:
            dst = dst_scr[h]
            dsend = [None] * nc
            for c in reversed(range(nc)):
                dsend[c] = dst
                dst = dst * eblast[c, :, hs[h]] + gq[h][c]
            dst_scr[h] = dst
            dse.append(jnp.stack(dsend))
        dseb = [_bf(d) for d in dse]
        dv = [dv[h] + ein('cjd,cvd->cjv', khh[h], dseb[h]) for h in heads]
        dkh = [ein('cjv,cvd->cjd', vh[h], dseb[h]) for h in heads]
        carried = jnp.concatenate([jnp.sum(dse[h] * stf[h], axis=1, keepdims=True) for h in heads], axis=2)
        wide = lambda parts: jnp.concatenate([p.reshape(TM, DK) for p in parts], axis=1)
        dqt_w, dkt_w, dkh_w = wide(dqt), wide(dkt), wide(dkh)
        dkh_kh = dkh_w * kh
        extra = jnp.sum(dkh_kh.reshape(nc, CH, 256), axis=1, keepdims=True) + eblast * carried
        db = dqt_w * qt - dkt_w * kt - dkh_kh
        dg = _dot3(upper, db) + jnp.broadcast_to(extra, (nc, CH, 256)).reshape(TM, 256)
        dz = dg * dgate_ref[...]
        dzb = _bf(dz)
        dlr_ref[...] = _bf(_dot_nt(dzb, wgu_ref[...]))
        dwgu_ref[...] += _dot_tn(_bf(lr), dzb)
        dbg_ref[...] += jnp.sum(dz, axis=0, keepdims=True)
        dq = dqt_w * eb * 0.125
        dk = dkt_w * enb + dkh_w * ebl
        dgla_ref[...] = _bf(jnp.concatenate([dq, dk] + [d.reshape(TM, 128) for d in dv] + dr_l, axis=1))

    def spec(w, cb):
        return pl.BlockSpec((TM, w), lambda i: (blk(i), cb))

    def acc(shape):
        return pl.BlockSpec(shape, lambda i: (0, 0))

    return _call(
        body, "gla_bwd", (nt,),
        [spec(256, 0), spec(256, 1), spec(512, 1), spec(512, 2), spec(128, C_LR // 128), spec(512, 0),
         pl.BlockSpec((nc, 4, 128, DK), lambda i: (blk(i), 0, 0, 0)), spec(256, 0), spec(256, 0), spec(512, 0),
         VMEM_SPEC, VMEM_SPEC],
        [spec(1536, 0), spec(128, 0), acc((128, 256)), acc((1, 256)), acc((1, 128))],
        [jax.ShapeDtypeStruct((t + TM, 1536), BF16), jax.ShapeDtypeStruct((t + TM, 128), BF16),
         jax.ShapeDtypeStruct((128, 256), F32), jax.ShapeDtypeStruct((1, 256), F32), jax.ShapeDtypeStruct((1, 128), F32)],
        [pltpu.VMEM((4, 128, DK), F32)], _cp(48), (proj, proj, proj, proj, proj, oraw, sst, bcum, dgate, do, wgu, gnw), comm)


def _rope_tables(t):
    r = t + TM
    row = np.arange(r)
    pos = np.where(row < t, row + NM, np.where(row < t + NM, row - t, 0)).astype(np.float32)
    inv_freq = (1.0 / (np.float32(ROPE_THETA) ** (np.arange(0, 16, 2, dtype=np.float32) / np.float32(16)))).astype(np.float32)
    ang = (pos[:, None] * inv_freq[None, :]).astype(np.float32)
    cos, sin = np.cos(ang).astype(np.float32), np.sin(ang).astype(np.float32)
    one, zero = np.ones((r, 48), np.float32), np.zeros((r, 48), np.float32)
    return jnp.asarray(np.concatenate([cos, cos, one, -sin, sin, zero], axis=1))


def _rope(x, tab, sign):
    w = x.shape[1]
    rep = w // 64
    c = jnp.concatenate([tab[:, 0:64]] * rep, axis=1)
    s = jnp.concatenate([tab[:, 64:128]] * rep, axis=1)
    lane = lax.rem(lax.broadcasted_iota(jnp.int32, x.shape, 1), 64)
    partner = jnp.where(lane < 8, pltpu.roll(x, w - 8, 1), jnp.where(lane < 16, pltpu.roll(x, 8, 1), 0.0))
    return x * c + sign * (partner * s)


HB_BWD = 4


def _stack(x, hg):
    w = x.shape[1] // hg
    return x if hg == 1 else jnp.concatenate([x[:, g * w:(g + 1) * w] for g in range(hg)], axis=0)


def _unstack(x, hg):
    return x if hg == 1 else jnp.concatenate([x[g * SB:(g + 1) * SB] for g in range(hg)], axis=1)


def _swa_steps(r_tot):
    blocks = r_tot // SB
    return next(n for n in (6, 3, 2) if blocks % n == 0 and blocks // n >= 2)


def _swa_specs(nsb, nbq):
    def rows(h, w, cb, f):
        return pl.BlockSpec((h, w), lambda i: (f(i), cb))
    pair = lambda i: i
    prev = lambda i: jnp.maximum(nbq * i - 1, 0)
    meta = lambda i: nsb
    return rows, pair, prev, meta


def _swa_fwd(proj, sinks, t, comm=None):
    nsb = t // SB
    r_tot = t + TM
    nbq = _swa_steps(r_tot)
    qb = nbq * SB
    rows, pair, prev, meta = _swa_specs(nsb, nbq)

    def body(sink_ref, q_ref, kc_ref, kp_ref, km_ref, vc_ref, vp_ref, vm_ref, o_ref, lse_ref):
        i = pl.program_id(0)
        key = lax.broadcasted_iota(jnp.int32, (SB, SB), 0)
        qry = lax.broadcasted_iota(jnp.int32, (SB, SB), 1)
        km, vm = km_ref[0:NM, :], vm_ref[0:NM, :]
        for j in range(nbq):
            b = nbq * i + j
            rs = slice(j * SB, (j + 1) * SB)
            before = slice((j - 1) * SB, j * SB)
            real = b < nsb
            masks = (key <= qry, (key > qry) & (b > 0) & real, real)
            k3 = (kc_ref[rs, :], kp_ref[...] if j == 0 else kc_ref[before, :], km)
            v3 = (vc_ref[rs, :], vp_ref[...] if j == 0 else vc_ref[before, :], vm)
            valid = b * SB + lax.broadcasted_iota(jnp.int32, (1, SB), 1) < t + NM
            heads = range(8)
            kb = [[_bf(k[:, kv * 64:(kv + 1) * 64]) for k in k3] for kv in range(2)]
            vt = [[_bf(v[:, kv * 64:(kv + 1) * 64].T) for v in v3] for kv in range(2)]
            raw = [[_dot_nt(k, _bf(q_ref[rs, h * 64:(h + 1) * 64])) for k in kb[h // 4]] for h in heads]
            probs, inv_l, lse_l = [], [], []
            for h in heads:
                s = [jnp.where(m, sx, NEG) for m, sx in zip(masks, raw[h])]
                sink = sink_ref[0, h]
                top = jnp.maximum(jnp.max(jnp.maximum(s[0], s[1]), axis=0, keepdims=True),
                                  jnp.maximum(jnp.max(s[2], axis=0, keepdims=True), sink))
                p = [jnp.exp(sx - top) for sx in s]
                l = (jnp.sum(p[0] + p[1], axis=0, keepdims=True) + jnp.sum(p[2], axis=0, keepdims=True)
                     + jnp.exp(sink - top))
                probs.append([_bf(px) for px in p])
                inv_l.append(1.0 / l)
                lse_l.append(top + jnp.log(l))
            o_t = [_dot(vt[h // 4][0], probs[h][0]) + _dot(vt[h // 4][1], probs[h][1]) + _dot(vt[h // 4][2], probs[h][2])
                   for h in heads]
            o_ref[rs, :] = _bf(jnp.concatenate([jnp.where(valid, o_t[h] * inv_l[h], 0.0).T for h in heads], axis=1))
            lse_ref[:, rs] = jnp.concatenate(lse_l, axis=0)

    ck, cv = C_SK // 128, C_SV // 128
    return _call(
        body, "swa_fwd", (r_tot // qb,),
        [SMEM_SPEC, rows(qb, 512, C_SQ // 512, pair),
         rows(qb, 128, ck, pair), rows(SB, 128, ck, prev), rows(SB, 128, ck, meta),
         rows(qb, 128, cv, pair), rows(SB, 128, cv, prev), rows(SB, 128, cv, meta)],
        [rows(qb, 512, 0, pair), pl.BlockSpec((8, qb), lambda i: (0, i))],
        [jax.ShapeDtypeStruct((r_tot, 512), BF16), jax.ShapeDtypeStruct((8, r_tot), F32)],
        [], _cp(32), (sinks, proj, proj, proj, proj, proj, proj, proj), comm)


def _swa_bwd(proj, sinks, lse_t, do, t, comm=None):
    nsb = t // SB
    r_tot = t + TM
    nbq = _swa_steps(r_tot)
    qb = nbq * SB
    rows, pair, prev, meta = _swa_specs(nsb, nbq)
    hb = HB_BWD
    lanes = hb * SB

    def body(sink_ref, q_ref, kc_ref, kp_ref, km_ref, vc_ref, vp_ref, vm_ref, lse_ref, do_ref,
             dq_ref, dk_ref, dv_ref, dsink_ref):
        i = pl.program_id(0)

        @pl.when(i == 0)
        def _():
            dk_ref[...] = jnp.zeros_like(dk_ref)
            dv_ref[...] = jnp.zeros_like(dv_ref)
            dsink_ref[...] = jnp.zeros_like(dsink_ref)

        key = lax.broadcasted_iota(jnp.int32, (SB, lanes), 0)
        qry = lax.rem(lax.broadcasted_iota(jnp.int32, (SB, lanes), 1), SB)
        km, vm = km_ref[0:NM, :], vm_ref[0:NM, :]
        dsink_l = []
        for j in range(nbq):
            b = nbq * i + j
            rs = slice(j * SB, (j + 1) * SB)
            before = slice((j - 1) * SB, j * SB)
            real = b < nsb
            masks = (key <= qry, (key > qry) & (b > 0) & real, real)
            k3 = (kc_ref[rs, :], kp_ref[...] if j == 0 else kc_ref[before, :], km)
            v3 = (vc_ref[rs, :], vp_ref[...] if j == 0 else vc_ref[before, :], vm)
            groups = list(range(0, 8, hb))
            kvs = [h0 // 4 for h0 in groups]
            qg = [_bf(_stack(q_ref[rs, h0 * 64:(h0 + hb) * 64], hb)) for h0 in groups]
            dog = [_bf(_stack(do_ref[rs, h0 * 64:(h0 + hb) * 64], hb)) for h0 in groups]
            kb = [[_bf(k[:, kv * 64:(kv + 1) * 64]) for k in k3] for kv in kvs]
            vb = [[_bf(v[:, kv * 64:(kv + 1) * 64]) for v in v3] for kv in kvs]
            s = [[_dot_nt(k, qg[g]) for k in kb[g]] for g in range(len(groups))]
            dp = [[_dot_nt(v, dog[g]) for v in vb[g]] for g in range(len(groups))]
            p, ds, ds_blk = [], [], []
            for g, h0 in enumerate(groups):
                lse_row = jnp.concatenate([lse_ref[h:h + 1, rs] for h in range(h0, h0 + hb)], axis=1)
                sink_row = jnp.concatenate([jnp.full((1, SB), sink_ref[0, h], F32) for h in range(h0, h0 + hb)], axis=1)
                pg = [jnp.exp(jnp.where(m, sx, NEG) - lse_row) for m, sx in zip(masks, s[g])]
                delta = (jnp.sum(pg[0] * dp[g][0] + pg[1] * dp[g][1], axis=0, keepdims=True)
                         + jnp.sum(pg[2] * dp[g][2], axis=0, keepdims=True))
                ds.append([_bf(pp * (dd - delta)) for pp, dd in zip(pg, dp[g])])
                p.append([_bf(pp) for pp in pg])
                ds_row = -jnp.exp(sink_row - lse_row) * delta
                ds_blk += [jnp.sum(ds_row[:, q0 * SB:(q0 + 1) * SB], axis=1, keepdims=True) for q0 in range(hb)]
            dsink_l.append(jnp.concatenate(ds_blk, axis=1))
            dq_t = [_dot_tn(kb[g][0], ds[g][0]) + _dot_tn(kb[g][1], ds[g][1]) + _dot_tn(kb[g][2], ds[g][2])
                    for g in range(len(groups))]
            dq_ref[rs, :] = jnp.concatenate([_unstack(d.T, hb) for d in dq_t], axis=1)
            windows = (pl.ds(pl.multiple_of(b * SB, SB), SB), pl.ds(pl.multiple_of(jnp.maximum(b - 1, 0) * SB, SB), SB),
                       pl.ds(t, NM))
            for x in range(3):
                dk_kv, dv_kv = [], []
                for kv in range(2):
                    mine = [g for g in range(len(groups)) if kvs[g] == kv]
                    dk_kv.append(sum(_dot(ds[g][x], qg[g]) for g in mine))
                    dv_kv.append(sum(_dot(p[g][x], dog[g]) for g in mine))
                dk_ref[windows[x], :] += jnp.concatenate(dk_kv, axis=1)
                dv_ref[windows[x], :] += jnp.concatenate(dv_kv, axis=1)
        dsink_ref[...] += sum(dsink_l)

    ck, cv = C_SK // 128, C_SV // 128
    whole = lambda w: pl.BlockSpec((r_tot, w), lambda i: (0, 0))
    return _call(
        body, "swa_bwd", (r_tot // qb,),
        [SMEM_SPEC, rows(qb, 512, C_SQ // 512, pair),
         rows(qb, 128, ck, pair), rows(SB, 128, ck, prev), rows(SB, 128, ck, meta),
         rows(qb, 128, cv, pair), rows(SB, 128, cv, prev), rows(SB, 128, cv, meta),
         pl.BlockSpec((8, qb), lambda i: (0, i)), rows(qb, 512, 1, pair)],
        [rows(qb, 512, 0, pair), whole(128), whole(128), pl.BlockSpec((1, 8), lambda i: (0, 0))],
        [jax.ShapeDtypeStruct((r_tot, 512), F32), jax.ShapeDtypeStruct((r_tot, 128), F32),
         jax.ShapeDtypeStruct((r_tot, 128), F32), jax.ShapeDtypeStruct((1, 8), F32)],
        [], _cp(48), (sinks, proj, proj, proj, proj, proj, proj, proj, lse_t, do), comm)


HK = D // 2


def _mlp_fwd(x, metapad, tgt, ogla, oswa, wo, wff, w1, w2, wfin):
    t = x.shape[0]
    nblk = t // TM

    def body(x_ref, mp_ref, tgt_ref, og_ref, os_ref, wo_ref, wff_ref, w1a_ref, w1b_ref, w2a_ref, w2b_ref, wfin_ref,
             h1_ref, f_ref, a_ref, dh2_ref, loss_ref, gfin_ref):
        i = pl.program_id(0)

        @pl.when(i == 0)
        def _():
            loss_ref[...] = jnp.zeros_like(loss_ref)
            gfin_ref[...] = jnp.zeros_like(gfin_ref)

        h0 = jnp.where(i == nblk, mp_ref[...], x_ref[...])
        h1 = h0 + _dot(og_ref[...], wo_ref[0:512, :]) + _dot(os_ref[...], wo_ref[512:1024, :])
        h1_ref[...] = h1
        fh, _ = _rms(h1)
        f = _bf(fh * wff_ref[...])
        f_ref[...] = f
        acc = jnp.zeros((TM, D), F32)
        for n in range(4):
            a = _dot(f[:, 0:HK], w1a_ref[n]) + _dot(f[:, HK:D], w1b_ref[n])
            a_ref[:, n * D:(n + 1) * D] = _bf(a)
            zr = jnp.maximum(a, 0.0)
            z = _bf(zr * zr)
            acc = acc + _dot(z[:, 0:HK], w2a_ref[n]) + _dot(z[:, HK:D], w2b_ref[n])
        h2 = h1 + acc
        yh, rs2 = _rms(h2)
        wf = wfin_ref[...]
        real = i < nblk
        e = jnp.where(real, yh * wf - tgt_ref[...], 0.0)
        loss_ref[...] += jnp.sum(jnp.sum(e * e, axis=0, keepdims=True), axis=1, keepdims=True) * (0.5 / D)
        dy = e * (1.0 / D)
        gfin_ref[...] += jnp.sum(dy * yh, axis=0, keepdims=True)
        dh2_ref[...] = _rms_bwd(dy, yh, rs2, wf)

    xs = pl.BlockSpec((TM, D), lambda i: (jnp.minimum(i, nblk - 1), 0))
    rs = lambda w: pl.BlockSpec((TM, w), lambda i: (i, 0))
    r_tot = t + TM
    return pl.pallas_call(
        body, name="mlp_fwd", grid=(nblk + 1,),
        in_specs=[xs, VMEM_SPEC, xs, rs(512), rs(512)] + [VMEM_SPEC] * 7,
        out_specs=[rs(D), rs(D), rs(DFF), rs(D), pl.BlockSpec((1, 1), lambda i: (0, 0)), pl.BlockSpec((1, D), lambda i: (0, 0))],
        out_shape=[jax.ShapeDtypeStruct((r_tot, D), F32), jax.ShapeDtypeStruct((r_tot, D), BF16),
                   jax.ShapeDtypeStruct((r_tot, DFF), BF16), jax.ShapeDtypeStruct((r_tot, D), F32),
                   jax.ShapeDtypeStruct((1, 1), F32), jax.ShapeDtypeStruct((1, D), F32)],
        compiler_params=_cp(56),
    )(x, metapad, tgt, ogla, oswa, wo, wff, *w1, *w2, wfin)


def _mlp_bwd(h1, a, dh2, ogla, oswa, wo, wff, w1, w2):
    r_tot = h1.shape[0]
    nt = r_tot // TM

    def body(h1_ref, a_ref, dh2_ref, og_ref, os_ref, wo_ref, wff_ref, w1a_ref, w1b_ref, w2a_ref, w2b_ref,
             da_ref, dh2b_ref, dh1_ref, do_ref, dwo_ref, gff_ref, dwo_acc):
        i = pl.program_id(0)

        @pl.when(i == 0)
        def _():
            dwo_acc[...] = jnp.zeros_like(dwo_acc)
            gff_ref[...] = jnp.zeros_like(gff_ref)

        dh2 = dh2_ref[...]
        dh2b = _bf(dh2)
        dh2b_ref[...] = dh2b
        dfa = jnp.zeros((TM, HK), F32)
        dfb = jnp.zeros((TM, HK), F32)
        for n in range(4):
            dz = jnp.concatenate([_dot_nt(dh2b, w2a_ref[n]), _dot_nt(dh2b, w2b_ref[n])], axis=1)
            da = _bf(dz * (2.0 * jnp.maximum(a_ref[:, n * D:(n + 1) * D].astype(F32), 0.0)))
            da_ref[:, n * D:(n + 1) * D] = da
            dfa = dfa + _dot_nt(da, w1a_ref[n])
            dfb = dfb + _dot_nt(da, w1b_ref[n])
        df = jnp.concatenate([dfa, dfb], axis=1)
        fh, rs1 = _rms(h1_ref[...])
        gff_ref[...] += jnp.sum(df * fh, axis=0, keepdims=True)
        dh1 = dh2 + _rms_bwd(df, fh, rs1, wff_ref[...])
        dh1_ref[...] = dh1
        dh1b = _bf(dh1)
        do_ref[...] = _dot_nt(dh1b, wo_ref[...])
        dwo_acc[0:512, :] += _dot_tn(og_ref[...], dh1b)
        dwo_acc[512:1024, :] += _dot_tn(os_ref[...], dh1b)

        @pl.when(i == nt - 1)
        def _():
            for s in range(4):
                for hh in range(2):
                    dwo_ref[hh, s] = dwo_acc[(2 * s + hh) * 128:(2 * s + hh + 1) * 128, :]

    rs = lambda w: pl.BlockSpec((TM, w), lambda i: (i, 0))
    return pl.pallas_call(
        body, name="mlp_bwd", grid=(nt,),
        in_specs=[rs(D), rs(DFF), rs(D), rs(512), rs(512)] + [VMEM_SPEC] * 6,
        out_specs=[rs(DFF), rs(D), rs(D), rs(D), VMEM_SPEC, pl.BlockSpec((1, D), lambda i: (0, 0))],
        out_shape=[jax.ShapeDtypeStruct((r_tot, DFF), BF16), jax.ShapeDtypeStruct((r_tot, D), BF16),
                   jax.ShapeDtypeStruct((r_tot, D), F32), jax.ShapeDtypeStruct((r_tot, D), F32),
                   jax.ShapeDtypeStruct((2, 4, 128, D), F32), jax.ShapeDtypeStruct((1, D), F32)],
        scratch_shapes=[pltpu.VMEM((D, D), F32)],
        compiler_params=_cp(56),
    )(h1, a, dh2, ogla, oswa, wo, wff, *w1, *w2)


def _ffn_wgrad(f, a, da, dh2b):
    r_tot = f.shape[0]
    kt = 768 if r_tot % 768 == 0 else TM
    nk = r_tot // kt

    def body(f_ref, a_ref, da_ref, dh2_ref, dw1_ref, dw2_ref, acc1, acc2):
        k = pl.program_id(1)

        @pl.when(k == 0)
        def _():
            acc1[...] = jnp.zeros_like(acc1)
            acc2[...] = jnp.zeros_like(acc2)

        zr = jnp.maximum(a_ref[...], 0.0)
        acc1[...] += _dot_tn(f_ref[...], da_ref[...])
        acc2[...] += _dot_tn(zr * zr, dh2_ref[...])

        @pl.when(k == nk - 1)
        def _():
            for hh in range(2):
                dw1_ref[hh, 0] = acc1[hh * 512:(hh + 1) * 512, :]
                dw2_ref[hh, 0] = acc2[hh * 512:(hh + 1) * 512, :]

    out = pl.BlockSpec((2, 1, 512, D), lambda n, k: (0, n, 0, 0))
    return pl.pallas_call(
        body, name="ffn_wgrad", grid=(4, nk),
        in_specs=[pl.BlockSpec((kt, D), lambda n, k: (k, 0)), pl.BlockSpec((kt, D), lambda n, k: (k, n)),
                  pl.BlockSpec((kt, D), lambda n, k: (k, n)), pl.BlockSpec((kt, D), lambda n, k: (k, 0))],
        out_specs=[out, out],
        out_shape=[jax.ShapeDtypeStruct((2, 4, 512, D), F32)] * 2,
        scratch_shapes=[pltpu.VMEM((D, D), F32), pltpu.VMEM((D, D), F32)],
        compiler_params=_cp(48, ("arbitrary", "arbitrary")),
    )(f, a, da, dh2b)


def _proj_bwd(x, metapad, wm, wt3, tabs, dgla, dswa_q, dsk, dsv, dlr, dh1, comm=None):
    t = x.shape[0]
    nblk = t // TM

    def body(x_ref, mp_ref, wm_ref, w3_ref, tab_ref, dg_ref, dq_ref, dk_ref, dv_ref, dlr_ref, dh1_ref,
             gx_ref, gmeta_ref, dw_ref, gmix_ref, w_ref, acc):
        i = pl.program_id(0)

        @pl.when(i == 0)
        def _():
            _join_shards(w3_ref, w_ref)
            acc[...] = jnp.zeros_like(acc)
            gmix_ref[...] = jnp.zeros_like(gmix_ref)

        h = jnp.where(i == nblk, mp_ref[...], x_ref[...])
        uh, rs = _rms(h)
        wm_v = wm_ref[...]
        u = _bf(uh * wm_v)
        tab = tab_ref[...]
        dq = _bf(_rope(dq_ref[...] * 0.125, tab, -1.0))
        dk = _bf(_rope(dk_ref[...], tab, -1.0))
        parts = ((dg_ref[...], 0, R_LR), (dlr_ref[:, 0:16], R_LR, 16), (dq, R_LR + 16, 512),
                 (dk, R_LR + 528, 128), (_bf(dv_ref[...]), R_LR + 656, 128))
        du = jnp.zeros((TM, D), F32)
        for val, r0, w in parts:
            du = du + _dot(val, w_ref[r0:r0 + w, :])
            acc[r0:r0 + w, :] += _dot_tn(val, u)
        gmix_ref[...] += jnp.sum(du * uh, axis=0, keepdims=True)
        dh0 = dh1_ref[...] + _rms_bwd(du, uh, rs, wm_v)

        @pl.when(i < nblk)
        def _():
            gx_ref[...] = dh0

        @pl.when(i == nblk)
        def _():
            gmeta_ref[...] = dh0[:NM]
            for s in range(4):
                dw_ref[s] = acc[(DIN // 4) * s:(DIN // 4) * (s + 1), :]

    xs = pl.BlockSpec((TM, D), lambda i: (jnp.minimum(i, nblk - 1), 0))
    rs_ = lambda w: pl.BlockSpec((TM, w), lambda i: (i, 0))
    return _call(
        body, "proj_bwd", (nblk + 1,),
        [xs, VMEM_SPEC, VMEM_SPEC, VMEM_SPEC, rs_(128), rs_(1536), rs_(512), rs_(128), rs_(128), rs_(128), rs_(D)],
        [xs, pl.BlockSpec((NM, D), lambda i: (0, 0)), VMEM_SPEC, pl.BlockSpec((1, D), lambda i: (0, 0))],
        [jax.ShapeDtypeStruct((t, D), F32), jax.ShapeDtypeStruct((NM, D), F32),
         jax.ShapeDtypeStruct((4, DIN // 4, D), F32), jax.ShapeDtypeStruct((1, D), F32)],
        [pltpu.VMEM((DIN, D), BF16), pltpu.VMEM((DIN, D), F32)], _cp(56),
        (x, metapad, wm, wt3, tabs, dgla, dswa_q, dsk, dsv, dlr, dh1), comm)


def _place():
    return lax.axis_index("x"), lax.axis_index("y"), lax.axis_index("c")


def _other_chips(x, y):
    return [(1 - x, y), (x, 1 - y), (1 - x, 1 - y)]


def _dma_sems(*counts):
    return tuple(pltpu.SemaphoreType.DMA((k,)) for k in counts)


def _gather_shards(shards, split):
    n = len(shards)
    two = [a for a in range(n) if split[a]]

    def plan(ins, outs, sems):
        isend, irecv, dsend, drecv, loc = sems
        x, y, c = _place()
        chips = _other_chips(x, y)

        def part(ref, a, half):
            if not split[a]:
                return ref
            w = shards[a].shape[1] // 2
            return ref.at[:, pl.ds(pl.multiple_of(half * w, 128), w)]

        def over_ici(a, k, shard_of):
            tx, ty = chips[k]
            sx, sy = shard_of
            return pltpu.make_async_remote_copy(
                src_ref=part(ins[a], a, c), dst_ref=part(outs[a].at[2 * sx + sy], a, c), send_sem=isend.at[3 * a + k],
                recv_sem=irecv.at[3 * a + k], device_id=(tx, ty, c), device_id_type=MESH)

        def over_d2d(a, k, half):
            tx, ty = chips[k]
            ref = part(outs[a].at[2 * tx + ty], a, half)
            return pltpu.make_async_remote_copy(
                src_ref=ref, dst_ref=ref, send_sem=dsend.at[3 * a + k], recv_sem=drecv.at[3 * a + k],
                device_id=(x, y, 1 - c), device_id_type=MESH)

        def local(a):
            return pltpu.make_async_copy(ins[a], outs[a].at[2 * x + y], loc.at[a])

        pairs = [(a, k) for a in range(n) for k in range(3)]
        first = ([lambda a=a: local(a).start() for a in range(n)]
                 + [lambda a=a, k=k: over_ici(a, k, (x, y)).start() for a, k in pairs],
                 [lambda a=a, k=k: over_ici(a, k, chips[k]).wait_recv() for a, k in pairs]
                 + [lambda a=a, k=k: over_ici(a, k, (x, y)).wait_send() for a, k in pairs]
                 + [lambda a=a: local(a).wait() for a in range(n)])
        pairs2 = [(a, k) for a in two for k in range(3)]
        second = ([lambda a=a, k=k: over_d2d(a, k, c).start() for a, k in pairs2],
                  [lambda a=a, k=k: over_d2d(a, k, 1 - c).wait_recv() for a, k in pairs2]
                  + [lambda a=a, k=k: over_d2d(a, k, c).wait_send() for a, k in pairs2])
        return [first, second] if two else [first]

    return _Comm(tuple(shards), tuple(jax.ShapeDtypeStruct((4,) + s.shape, s.dtype) for s in shards),
                 _dma_sems(3 * n, 3 * n, 3 * n, 3 * n, n), 2 if two else 1, plan)


def _swap_halves(grads):
    n = len(grads)

    def plan(ins, outs, sems):
        send, recv = sems
        x, y, c = _place()

        def swap(a):
            return pltpu.make_async_remote_copy(
                src_ref=ins[a].at[1 - c], dst_ref=outs[a], send_sem=send.at[a], recv_sem=recv.at[a],
                device_id=(x, y, 1 - c), device_id_type=MESH)

        return [([lambda a=a: swap(a).start() for a in range(n)], [lambda a=a: swap(a).wait() for a in range(n)])]

    return _Comm(tuple(grads), tuple(jax.ShapeDtypeStruct(g.shape[1:], g.dtype) for g in grads), _dma_sems(n, n), 1, plan)


SCATTER_ADD_ROWS = 128


def _scatter_shards(parts, late):
    n = len(parts)

    def plan(ins, outs, sems):
        send, recv, loc = sems[:3]
        onward, got = sems[3:3 + n], sems[3 + n:]
        x, y, c = _place()
        x_first = c == 0
        near = (jnp.where(x_first, 1 - x, x), jnp.where(x_first, y, 1 - y))
        far = (jnp.where(x_first, x, 1 - x), jnp.where(x_first, 1 - y, y))
        diagonal = (1 - x, 1 - y)
        shard = lambda chip: 2 * chip[0] + chip[1]

        def hop(src, dst, k, chip):
            return pltpu.make_async_remote_copy(src_ref=src, dst_ref=dst, send_sem=send.at[k], recv_sem=recv.at[k],
                                                device_id=(chip[0], chip[1], c), device_id_type=MESH)

        theirs = lambda a: hop(ins[a].at[shard(near)], outs[a].at[0], 3 * a, near)
        passing = lambda a: hop(ins[a].at[shard(diagonal)], got[a], 3 * a + 1, near)
        summed = lambda a: hop(onward[a], outs[a].at[1], 3 * a + 2, far)
        mine = lambda a: pltpu.make_async_copy(ins[a].at[shard(far)], onward[a], loc.at[a])

        def add(a):
            for r in range(0, parts[a].shape[1], SCATTER_ADD_ROWS):
                rows = slice(r, r + SCATTER_ADD_ROWS)
                onward[a][rows, :] = _bf(onward[a][rows, :].astype(F32) + got[a][rows, :].astype(F32))

        every = range(n)
        first = ([lambda a=a: mine(a).start() for a in every] + [lambda a=a: passing(a).start() for a in every]
                 + [lambda a=a: theirs(a).start() for a in every],
                 [lambda a=a: mine(a).wait() for a in every] + [lambda a=a: passing(a).wait_recv() for a in every]
                 + [lambda a=a: add(a) for a in every])
        second = ([lambda a=a: summed(a).start() for a in every],
                  [lambda a=a: passing(a).wait_send() for a in every] + [lambda a=a: theirs(a).wait() for a in every]
                  + [lambda a=a: summed(a).wait() for a in every])
        return [first, second]

    assert all(p.shape[1] % SCATTER_ADD_ROWS == 0 for p in parts)
    buffers = [pltpu.VMEM(p.shape[1:], p.dtype) for p in parts]
    return _Comm(tuple(parts), tuple(jax.ShapeDtypeStruct((2,) + p.shape[1:], p.dtype) for p in parts),
                 _dma_sems(3 * n, 3 * n, n) + tuple(buffers) * 2, 2, plan, late)


def _join_halves(halves):
    n = len(halves)

    def plan(ins, outs, sems):
        send, recv, loc = sems
        x, y, c = _place()

        def remote(a, half):
            return pltpu.make_async_remote_copy(
                src_ref=ins[a], dst_ref=outs[a].at[half], send_sem=send.at[a], recv_sem=recv.at[a],
                device_id=(x, y, 1 - c), device_id_type=MESH)

        def local(a):
            return pltpu.make_async_copy(ins[a], outs[a].at[c], loc.at[a])

        every = range(n)
        return [([lambda a=a: local(a).start() for a in every] + [lambda a=a: remote(a, c).start() for a in every],
                 [lambda a=a: remote(a, 1 - c).wait_recv() for a in every]
                 + [lambda a=a: remote(a, c).wait_send() for a in every] + [lambda a=a: local(a).wait() for a in every])]

    return _Comm(tuple(halves), tuple(jax.ShapeDtypeStruct((2,) + h.shape, h.dtype) for h in halves),
                 _dma_sems(n, n, n), 1, plan)


def _reduce_w_in(dwt, comm):
    rows, hw = DIN // 4, D // 2
    ci, co = len(comm.ins), len(comm.outs)

    def body(*refs):
        dw_ref, c_in, out_ref, c_out = refs[0], refs[1:1 + ci], refs[1 + ci], refs[2 + ci:2 + ci + co]
        mine, sib, tosend, rbuf, qbuf, full, send, recv, loc = refs[2 + ci + co:11 + ci + co]
        c_sem = refs[11 + ci + co:]
        x, y, c = _place()
        sibling = (x, y, 1 - c)
        (starts, waits), = comm.plan(c_in, c_out, c_sem)
        _run_phase(starts)

        def cols(ref, half):
            window = pl.ds(pl.multiple_of(half * hw, 128), hw)
            return ref.at[:, :, window] if len(ref.shape) == 3 else ref.at[:, window]

        load = pltpu.make_async_copy(cols(dw_ref, c), mine, loc.at[0])
        give = pltpu.make_async_remote_copy(src_ref=cols(dw_ref, 1 - c), dst_ref=sib, send_sem=send.at[3], recv_sem=recv.at[3],
                                            device_id=sibling, device_id_type=MESH)
        load.start()
        give.start()
        load.wait()
        give.wait()
        mine[...] = mine[...] + sib[...]
        cps = []
        for k, (tx, ty) in enumerate(_other_chips(x, y)):
            tosend[k] = _bf(mine[2 * tx + ty])
            cps.append(pltpu.make_async_remote_copy(
                src_ref=tosend.at[k], dst_ref=rbuf.at[k], send_sem=send.at[k], recv_sem=recv.at[k],
                device_id=(tx, ty, c), device_id_type=MESH))
            cps[-1].start()
        for cp in cps:
            cp.wait()
        qbuf[...] = mine[2 * x + y] + rbuf[0].astype(F32) + rbuf[1].astype(F32) + rbuf[2].astype(F32)
        keep = pltpu.make_async_copy(qbuf, cols(full, c), loc.at[1])
        pass_on = pltpu.make_async_remote_copy(src_ref=qbuf, dst_ref=cols(full, c), send_sem=send.at[4], recv_sem=recv.at[4],
                                               device_id=sibling, device_id_type=MESH)
        keep.start()
        pass_on.start()
        keep.wait()
        pass_on.wait_send()
        pltpu.make_async_remote_copy(src_ref=qbuf, dst_ref=cols(full, 1 - c), send_sem=send.at[4], recv_sem=recv.at[4],
                                     device_id=sibling, device_id_type=MESH).wait_recv()
        out_ref[...] = full[...]
        _run_phase(waits)

    outs = pl.pallas_call(
        body, name="reduce_w_in",
        in_specs=[ANY_SPEC] * (1 + ci), out_specs=[VMEM_SPEC] + [ANY_SPEC] * co,
        out_shape=[jax.ShapeDtypeStruct((rows, D), F32)] + list(comm.outs),
        scratch_shapes=[pltpu.VMEM((4, rows, hw), F32), pltpu.VMEM((4, rows, hw), F32), pltpu.VMEM((3, rows, hw), BF16),
                        pltpu.VMEM((3, rows, hw), BF16), pltpu.VMEM((rows, hw), F32), pltpu.VMEM((rows, D), F32),
                        *_dma_sems(5, 5, 2), *comm.sems],
        compiler_params=pltpu.CompilerParams(vmem_limit_bytes=48 << 20),
    )(dwt, *comm.ins)
    return outs[0], outs[1:]


def _allreduce_small(pack):
    p = pack.shape[0]

    def body(in_ref, out_ref, buf, send, recv):
        x, y, c = _place()
        me = 4 * x + 2 * y + c
        buf[me] = in_ref[...]

        def peer_of(k):
            return x ^ (k >> 2), y ^ ((k >> 1) & 1), c ^ (k & 1)

        sends = [pltpu.make_async_remote_copy(
            src_ref=in_ref, dst_ref=buf.at[me], send_sem=send.at[k - 1], recv_sem=recv.at[k - 1],
            device_id=peer_of(k), device_id_type=MESH) for k in range(1, 8)]
        for cp in sends:
            cp.start()
        for k in range(1, 8):
            px, py, pc = peer_of(k)
            pltpu.make_async_remote_copy(
                src_ref=in_ref, dst_ref=buf.at[4 * px + 2 * py + pc], send_sem=send.at[k - 1], recv_sem=recv.at[k - 1],
                device_id=(x, y, c), device_id_type=MESH).wait_recv()
        for cp in sends:
            cp.wait_send()
        acc = buf[0]
        for d in range(1, 8):
            acc = acc + buf[d]
        out_ref[...] = acc

    return pl.pallas_call(
        body, name="allreduce_small",
        in_specs=[VMEM_SPEC], out_specs=VMEM_SPEC, out_shape=jax.ShapeDtypeStruct(pack.shape, F32),
        scratch_shapes=[pltpu.VMEM((8, p, D), F32), *_dma_sems(7, 7)],
    )(pack)


GRID4 = 4


def _sum_cores(core_shard, mine, theirs):
    n = len(mine)

    def body(cs_ref, *refs):
        ms, ts, bfs, owns = refs[:n], refs[n:2 * n], refs[2 * n:3 * n], refs[3 * n:]
        keep = pl.program_id(0) == cs_ref[1]
        for a in range(n):
            acc = ms[a][0, 0] + ts[a][0]
            bfs[a][0] = _bf(acc)

            @pl.when(keep)
            def _():
                owns[a][...] = acc

    shapes = [m.shape[2:] for m in mine]
    in_specs = ([pl.BlockSpec((1, 1) + s, lambda i, cs: (cs[0], i, 0, 0)) for s in shapes]
                + [pl.BlockSpec((1,) + s, lambda i, cs: (i, 0, 0)) for s in shapes])
    out_specs = ([pl.BlockSpec((1,) + s, lambda i, cs: (i, 0, 0)) for s in shapes]
                 + [pl.BlockSpec(s, lambda i, cs: (0, 0)) for s in shapes])
    outs = pl.pallas_call(
        body, name="sum_cores",
        grid_spec=pltpu.PrefetchScalarGridSpec(num_scalar_prefetch=1, grid=(4,), in_specs=in_specs, out_specs=out_specs),
        out_shape=[jax.ShapeDtypeStruct((4,) + s, BF16) for s in shapes] + [jax.ShapeDtypeStruct(s, F32) for s in shapes],
        compiler_params=_cp(48),
    )(core_shard, *mine, *theirs)
    return outs[:n], outs[n:]


def _sum_chips(own, arrived):
    n = len(own)

    def body(*refs):
        os_, ars, outs = refs[:n], refs[n:2 * n], refs[2 * n:]
        for a in range(n):
            outs[a][...] = os_[a][...] + ars[a][0].astype(F32) + ars[a][1].astype(F32)

    blocks = [(o.shape[0] // GRID4, o.shape[1]) for o in own]
    return pl.pallas_call(
        body, name="sum_chips", grid=(GRID4,),
        in_specs=([pl.BlockSpec(b, lambda i: (i, 0)) for b in blocks]
                  + [pl.BlockSpec((2,) + b, lambda i: (0, i, 0)) for b in blocks]),
        out_specs=[pl.BlockSpec(b, lambda i: (i, 0)) for b in blocks],
        out_shape=[jax.ShapeDtypeStruct(o.shape, F32) for o in own],
        compiler_params=_cp(32),
    )(*own, *arrived)


def _adamw_math(w, g, m, v):
    m2 = ADAM_B1 * m + (1.0 - ADAM_B1) * g
    v2 = ADAM_B2 * v + (1.0 - ADAM_B2) * (g * g)
    m_hat = m2 / (1.0 - ADAM_B1 ** ADAM_STEP)
    v_hat = v2 / (1.0 - ADAM_B2 ** ADAM_STEP)
    return -ADAM_LR * (m_hat / (jnp.sqrt(v_hat) + ADAM_EPS) + ADAM_WD * w), m2, v2


def _adamw_big(ws, gs, ms, vs):
    n = len(ws)

    def body(*refs):
        for a in range(n):
            d, m2, v2 = _adamw_math(refs[a][...], refs[n + a][...], refs[2 * n + a][...], refs[3 * n + a][...])
            refs[4 * n + a][...] = d
            refs[5 * n + a][...] = m2
            refs[6 * n + a][...] = v2

    specs = [pl.BlockSpec((w.shape[0] // GRID4, w.shape[1]), lambda i: (i, 0)) for w in ws]
    return pl.pallas_call(
        body, name="adamw_big", grid=(GRID4,),
        in_specs=specs * 4, out_specs=specs * 3,
        out_shape=[jax.ShapeDtypeStruct(w.shape, F32) for w in ws] * 3,
        compiler_params=_cp(48),
    )(*ws, *gs, *ms, *vs)


def _adamw_small(ws, gs, ms, vs):
    n = len(ws)

    def body(*refs):
        for a in range(n):
            d, m2, v2 = _adamw_math(refs[a][...], refs[n + a][...], refs[2 * n + a][...], refs[3 * n + a][...])
            refs[4 * n + a][...] = d
            refs[5 * n + a][...] = m2
            refs[6 * n + a][...] = v2

    return pl.pallas_call(
        body, name="adamw_small",
        in_specs=[VMEM_SPEC] * (4 * n), out_specs=[VMEM_SPEC] * (3 * n),
        out_shape=[jax.ShapeDtypeStruct(w.shape, F32) for w in ws] * 3,
        compiler_params=pltpu.CompilerParams(vmem_limit_bytes=40 << 20),
    )(*ws, *gs, *ms, *vs)


def kernel(x, meta_tokens, norm_mix_w, w_in, w_gate_up, b_gate, gla_norm_w, sinks, w_out, norm_ff_w, w_ff1, w_ff2, final_norm_w, loss_target, m_meta_tokens, m_norm_mix_w, m_w_in, m_w_gate_up, m_b_gate, m_gla_norm_w, m_sinks, m_w_out, m_norm_ff_w, m_w_ff1, m_w_ff2, m_final_norm_w, v_meta_tokens, v_norm_mix_w, v_w_in, v_w_gate_up, v_b_gate, v_gla_norm_w, v_sinks, v_w_out, v_norm_ff_w, v_w_ff1, v_w_ff2, v_final_norm_w):
    xi, yi, ci = _place()
    shard = (2 * xi + yi).astype(jnp.int32).reshape(1)
    core = ci.astype(jnp.int32).reshape(1)

    small = jnp.concatenate([meta_tokens, w_gate_up[0], jnp.zeros((NM, 64), F32)], axis=1)
    wt3, g_small = _run_comm(_gather_shards([_bf(w_in[0].T), small], [True, False]), "gather_w_in")
    meta = g_small[:, :, 0:256].transpose(1, 0, 2).reshape(NM, D)
    wgu = g_small[:, :, 256:320].transpose(1, 0, 2).reshape(NM, 256)

    xs, tgt = x[0], loss_target[0]
    t = xs.shape[0]
    wfin = final_norm_w.reshape(1, D)
    metapad = jnp.concatenate([meta, jnp.zeros((TM - NM, D), F32)], axis=0)
    wgu_p = _bf(jnp.concatenate([wgu, jnp.zeros((128 - 16, 256), F32)], axis=0))
    tabs = _rope_tables(t)

    w1s, w2s = _bf(w_ff1[0]), _bf(w_ff2[0])
    proj, (g_out, w1a) = _proj_fwd(xs, metapad, norm_mix_w, wt3, tabs,
                                   _gather_shards([_bf(w_out[0]), w1s[:HK]], [True] * 2))
    (ogla, oraw, sst, bcum, dgate), (w1b, w2a) = _gla_fwd(proj, wgu_p, b_gate, gla_norm_w, t,
                                                          _gather_shards([w1s[HK:], w2s[:HK]], [True] * 2))
    (oswa, lse), (w2b,) = _swa_fwd(proj, sinks, t, _gather_shards([w2s[HK:]], [True]))
    wo, w1, w2 = g_out.reshape(D, D), (w1a, w1b), (w2a, w2b)
    h1, f, a, dh2, loss, gfin = _mlp_fwd(xs, metapad, tgt, ogla, oswa, wo, norm_ff_w, w1, w2, wfin)

    da, dh2b, dh1, do, dwo, gff = _mlp_bwd(h1, a, dh2, ogla, oswa, wo, norm_ff_w, w1, w2)
    dw1, dw2 = _ffn_wgrad(f, a, da, dh2b)
    big = [dwo, dw1, dw2]
    (dgla, dlr, dwgu, dbg, dgnw), theirs = _gla_bwd(proj, oraw, sst, bcum, dgate, do, wgu_p, gla_norm_w, t,
                                                    _swap_halves(big))
    sums_bf, own = _sum_cores(jnp.concatenate([core, shard]), big, theirs)
    swa_grid = (t + TM) // SB // _swa_steps(t + TM)
    (dsq, dsk, dsv, dsink), arrived = _swa_bwd(proj, sinks, lse, do, t, _scatter_shards(sums_bf, swa_grid // 3))
    halves = _sum_chips(own, arrived)
    (gx, gmeta, dwt, gmix), _ = _proj_bwd(xs, metapad, norm_mix_w, wt3, tabs, dgla, dsq, dsk, dsv, dlr, dh1)

    gwt_in, joined = _reduce_w_in(dwt, _join_halves(halves))
    gw_out, gw_1, gw_2 = [j.reshape((-1, j.shape[2])) for j in joined]

    tail = jnp.concatenate([dbg, dgnw, dsink, loss, jnp.zeros((1, D - 256 - 128 - 8 - 1), F32)], axis=1)
    pack = jnp.concatenate([gmeta, gmix, gff, gfin, tail, dwgu[:16].reshape(4, D)], axis=0)
    tot = _allreduce_small(pack)
    g_meta = lax.dynamic_slice_in_dim(tot[0:NM], shard[0] * 256, 256, axis=1)
    g_mix, g_ff, g_fin = tot[16:17], tot[17:18], tot[18]
    g_bg, g_gnw, g_sinks, loss_tot = tot[19:20, 0:256], tot[19:20, 256:384], tot[19:20, 384:392], tot[19, 392]
    g_wgu = lax.dynamic_slice_in_dim(tot[20:24].reshape(NM, 256), shard[0] * 64, 64, axis=1)

    bo = _adamw_big([w_out[0], w_ff1[0], w_ff2[0]], [gw_out, gw_1, gw_2], [m_w_out[0], m_w_ff1[0], m_w_ff2[0]],
                    [v_w_out[0], v_w_ff1[0], v_w_ff2[0]])

    fin2 = lambda a: a.reshape(1, D)
    sw = [meta_tokens, norm_mix_w, w_gate_up[0], b_gate, gla_norm_w, sinks, norm_ff_w, fin2(final_norm_w), w_in[0].T]
    sg = [g_meta, g_mix, g_wgu, g_bg, g_gnw, g_sinks, g_ff, fin2(g_fin), gwt_in]
    sm = [m_meta_tokens, m_norm_mix_w, m_w_gate_up[0], m_b_gate, m_gla_norm_w, m_sinks, m_norm_ff_w, fin2(m_final_norm_w),
          m_w_in[0].T]
    sv = [v_meta_tokens, v_norm_mix_w, v_w_gate_up[0], v_b_gate, v_gla_norm_w, v_sinks, v_norm_ff_w, fin2(v_final_norm_w),
          v_w_in[0].T]
    so = _adamw_small(sw, sg, sm, sv)

    def ordered(small_o, big_o):
        meta_, mix_, wgu_, bg_, gnw_, sinks_, ff_, fin_, wt_ = small_o
        w_out_, w_1_, w_2_ = big_o
        return (meta_, mix_, wt_.T[None], wgu_[None], bg_, gnw_, sinks_, w_out_[None], ff_, w_1_[None], w_2_[None],
                fin_.reshape(D))

    grads = ordered(sg, [gw_out, gw_1, gw_2])
    deltas = ordered(so[0:9], bo[0:3])
    new_m = ordered(so[9:18], bo[3:6])
    new_v = ordered(so[18:27], bo[6:9])
    return (loss_tot, gx[None], *grads, *deltas, *new_m, *new_v)
```

```python
import functools
from typing import Callable, NamedTuple

import jax
import jax.numpy as jnp
import numpy as np
from jax import lax
from jax.experimental import pallas as pl
from jax.experimental.pallas import tpu as pltpu

F32 = jnp.float32
BF16 = jnp.bfloat16

D = 1024
DFF = 4096
NM = 16
TM = 256
DK = 64
CH = 128
SB = 128
EPS = 1e-5
C_GQ, C_GK, C_GV, C_GR, C_SQ, C_SK, C_SV, C_LR, DINP = 0, 256, 512, 1024, 1536, 2048, 2176, 2304, 2432
DIN = 2320
R_LR = 1536
ROPE_THETA = 500000.0
ADAM_LR, ADAM_B1, ADAM_B2, ADAM_EPS, ADAM_WD, ADAM_STEP = 0.001, 0.9, 0.999, 1e-08, 0.01, 10
NEG = -1e30
MESH = pl.DeviceIdType.MESH
VMEM_SPEC = pl.BlockSpec(memory_space=pltpu.VMEM)
ANY_SPEC = pl.BlockSpec(memory_space=pl.ANY)
SMEM_SPEC = pl.BlockSpec(memory_space=pltpu.SMEM)


def _cp(vmem_mb, sem=("arbitrary",)):
    return pltpu.CompilerParams(dimension_semantics=sem, vmem_limit_bytes=vmem_mb << 20)


def _dot(a, b):
    return jnp.dot(a, b, preferred_element_type=F32)


def _dot_nt(a, b):
    return lax.dot_general(a, b, (((1,), (1,)), ((), ())), preferred_element_type=F32)


def _dot_tn(a, b):
    return lax.dot_general(a, b, (((0,), (0,)), ((), ())), preferred_element_type=F32)


def _bf(x):
    return x.astype(BF16)


def _dot3(m01, x):
    x1 = _bf(x)
    r1 = x - x1.astype(F32)
    x2 = _bf(r1)
    x3 = _bf(r1 - x2.astype(F32))
    return _dot(m01, x1) + _dot(m01, x2) + _dot(m01, x3)


def _rms(h):
    rs = lax.rsqrt(jnp.mean(h * h, axis=-1, keepdims=True) + EPS)
    return h * rs, rs


def _rms_bwd(dy, yhat, rs, w):
    dyh = dy * w
    return rs * (dyh - yhat * jnp.mean(dyh * yhat, axis=-1, keepdims=True))


class _Comm(NamedTuple):
    ins: tuple
    outs: tuple
    sems: tuple
    phases: int
    plan: Callable
    late: int = 0


def _run_phase(fns):
    for fn in fns:
        fn()


def _call(body, name, grid, in_specs, out_specs, out_shape, scratch, params, args, comm=None):
    if comm is None:
        outs = pl.pallas_call(body, name=name, grid=grid, in_specs=in_specs, out_specs=out_specs, out_shape=out_shape,
                              scratch_shapes=scratch, compiler_params=params)(*args)
        return outs, None
    n_in, n_out, n_scr = len(in_specs), len(out_specs), len(scratch)
    ci, co = len(comm.ins), len(comm.outs)
    last = grid[0] - 1
    marks = [0, max(1, last - (comm.late or max(2, (last + 1) // 6)))][:comm.phases]

    def wrapped(*refs):
        own_in, c_in = refs[:n_in], refs[n_in:n_in + ci]
        refs = refs[n_in + ci:]
        own_out, c_out = refs[:n_out], refs[n_out:n_out + co]
        refs = refs[n_out + co:]
        own_scr, c_sem = refs[:n_scr], refs[n_scr:]
        i = pl.program_id(0)

        for p, mark in enumerate(marks):
            @pl.when(i == mark)
            def _():
                plan = comm.plan(c_in, c_out, c_sem)
                if p > 0:
                    _run_phase(plan[p - 1][1])
                _run_phase(plan[p][0])

        body(*own_in, *own_out, *own_scr)

        @pl.when(i == last)
        def _():
            _run_phase(comm.plan(c_in, c_out, c_sem)[-1][1])

    outs = pl.pallas_call(
        wrapped, name=name, grid=grid, in_specs=list(in_specs) + [ANY_SPEC] * ci, out_specs=list(out_specs) + [ANY_SPEC] * co,
        out_shape=list(out_shape) + list(comm.outs), scratch_shapes=list(scratch) + list(comm.sems), compiler_params=params,
    )(*args, *comm.ins)
    return outs[:n_out], outs[n_out:]


def _run_comm(comm, name):
    ci, co = len(comm.ins), len(comm.outs)

    def body(*refs):
        for starts, waits in comm.plan(refs[:ci], refs[ci:ci + co], refs[ci + co:]):
            _run_phase(starts)
            _run_phase(waits)

    return pl.pallas_call(body, name=name, in_specs=[ANY_SPEC] * ci, out_specs=[ANY_SPEC] * co, out_shape=list(comm.outs),
                          scratch_shapes=list(comm.sems))(*comm.ins)


def _join_shards(w3_ref, w_ref):
    for s in range(4):
        w_ref[(DIN // 4) * s:(DIN // 4) * (s + 1), :] = w3_ref[s]


def _proj_fwd(x, metapad, wm, wt3, tabs, comm=None):
    t = x.shape[0]
    nblk = t // TM

    def body(x_ref, mp_ref, wm_ref, w3_ref, tab_ref, proj_ref, w_ref):
        i = pl.program_id(0)

        @pl.when(i == 0)
        def _():
            _join_shards(w3_ref, w_ref)

        h = jnp.where(i == nblk, mp_ref[...], x_ref[...])
        u, _ = _rms(h)
        ub = _bf(u * wm_ref[...])
        att = _dot_nt(ub, w_ref[R_LR + 16:DIN, :])
        proj_ref[:, 0:C_SQ] = _dot_nt(ub, w_ref[0:R_LR, :])
        tab = tab_ref[...]
        proj_ref[:, C_SQ:C_SK] = _rope(att[:, 0:512], tab, 1.0) * 0.125
        proj_ref[:, C_SK:C_SV] = _rope(att[:, 512:640], tab, 1.0)
        proj_ref[:, C_SV:C_LR] = att[:, 640:768]
        proj_ref[:, C_LR:DINP] = jnp.zeros((TM, DINP - C_LR), F32)
        proj_ref[:, C_LR:C_LR + 16] = _dot_nt(ub, w_ref[R_LR:R_LR + 16, :])

    (proj,), got = _call(
        body, "proj_fwd", (nblk + 1,),
        [pl.BlockSpec((TM, D), lambda i: (jnp.minimum(i, nblk - 1), 0)), VMEM_SPEC, VMEM_SPEC, VMEM_SPEC,
         pl.BlockSpec((TM, 128), lambda i: (i, 0))],
        [pl.BlockSpec((TM, DINP), lambda i: (i, 0))], [jax.ShapeDtypeStruct((t + TM, DINP), F32)],
        [pltpu.VMEM((DIN, D), BF16)], _cp(48), (x, metapad, wm, wt3, tabs), comm)
    return proj, got


def _chunk_masks():
    r = lax.broadcasted_iota(jnp.int32, (TM, TM), 0)
    c = lax.broadcasted_iota(jnp.int32, (TM, TM), 1)
    same = (r // CH) == (c // CH)
    lower = _bf(jnp.where(same & (c <= r), 1.0, 0.0))
    upper = _bf(jnp.where(same & (c >= r), 1.0, 0.0))
    return lower, upper


def _gla_gate(lr, wgu, bg, valid, lower):
    z = _dot(_bf(lr), wgu) + bg
    g = (jnp.minimum(z, 0.0) - jnp.log(1.0 + jnp.exp(-jnp.abs(z)))) * (1.0 / 16.0)
    g = jnp.where(valid, g, 0.0)
    return z, _dot3(lower, g)


def _gla_decays(q, k, b):
    nc = TM // CH
    b3 = b.reshape(nc, CH, 256)
    blast = b3[:, CH - 1:CH, :]
    eb = jnp.exp(b)
    enb = jnp.exp(-b)
    ebl = jnp.exp(blast - b3).reshape(TM, 256)
    return eb, enb, ebl, jnp.exp(blast)


def _tri(lower_incl):
    r = lax.broadcasted_iota(jnp.int32, (CH, CH), 0)
    c = lax.broadcasted_iota(jnp.int32, (CH, CH), 1)
    return ((c <= r) if lower_incl else (c >= r))[None]


def _gla_fwd(proj, wgu, bg, gnw, t, comm=None):
    nblk = t // TM
    nt = nblk + 1
    nc = TM // CH

    def blk(i):
        return (i + nblk) % nt

    def body(q_ref, k_ref, v_ref, r_ref, lr_ref, wgu_ref, bg_ref, gnw_ref, o_ref, oraw_ref, sst_ref, b_ref, dgate_ref,
             st_scr):
        i = pl.program_id(0)

        @pl.when(i == 0)
        def _():
            st_scr[...] = jnp.zeros_like(st_scr)

        rows = blk(i) * TM + lax.broadcasted_iota(jnp.int32, (TM, 1), 0)
        lower, _ = _chunk_masks()
        valid = rows < t + NM
        z, b = _gla_gate(lr_ref[...], wgu_ref[...], bg_ref[...], valid, lower)
        b_ref[...] = b
        dgate_ref[...] = jnp.where(valid, (1.0 / 16.0) / (1.0 + jnp.exp(z)), 0.0)
        q = q_ref[...]
        k = k_ref[...]
        eb, enb, ebl, eblast = _gla_decays(q, k, b)
        qt = q * 0.125 * eb
        kt = k * enb
        kh = k * ebl
        tril = _tri(True)
        heads = range(4)
        hs = [slice(h * DK, (h + 1) * DK) for h in heads]
        qh = [_bf(qt[:, hs[h]]).reshape(nc, CH, DK) for h in heads]
        kth = [_bf(kt[:, hs[h]]).reshape(nc, CH, DK) for h in heads]
        khh = [_bf(kh[:, hs[h]]).reshape(nc, CH, DK) for h in heads]
        vh = [_bf(v_ref[:, h * 128:(h + 1) * 128]).reshape(nc, CH, 128) for h in heads]
        a = [jnp.einsum('cid,cjd->cij', qh[h], kth[h], preferred_element_type=F32) for h in heads]
        kv = [jnp.einsum('cjv,cjd->cvd', vh[h], khh[h], preferred_element_type=F32) for h in heads]
        o = [jnp.einsum('cij,cjv->civ', _bf(jnp.where(tril, a[h], 0.0)), vh[h], preferred_element_type=F32) for h in heads]
        states = []
        for h in heads:
            st = st_scr[h]
            per_chunk = []
            for c in range(nc):
                sst_ref[c, h] = st
                per_chunk.append(_bf(st))
                st = st * eblast[c, :, hs[h]] + kv[h][c]
            st_scr[h] = st
            states.append(per_chunk)
        o_inter = [[_dot_nt(qh[h][c], states[h][c]) for c in range(nc)] for h in heads]
        oraw = jnp.concatenate([(o[h] + jnp.stack(o_inter[h])).reshape(TM, 128) for h in heads], axis=1)
        oraw_ref[...] = oraw
        gn = gnw_ref[...]
        res = []
        for h in range(4):
            on, _ = _rms(oraw[:, h * 128:(h + 1) * 128])
            r = r_ref[:, h * 128:(h + 1) * 128]
            res.append(on * gn * (r * jax.nn.sigmoid(r)))
        o_ref[...] = _bf(jnp.concatenate(res, axis=1))

    def spec(w, cb):
        return pl.BlockSpec((TM, w), lambda i: (blk(i), cb))

    return _call(
        body, "gla_fwd", (nt,),
        [spec(256, 0), spec(256, 1), spec(512, 1), spec(512, 2), spec(128, C_LR // 128), VMEM_SPEC, VMEM_SPEC, VMEM_SPEC],
        [spec(512, 0), spec(512, 0), pl.BlockSpec((nc, 4, 128, DK), lambda i: (blk(i), 0, 0, 0)), spec(256, 0), spec(256, 0)],
        [jax.ShapeDtypeStruct((t + TM, 512), BF16), jax.ShapeDtypeStruct((t + TM, 512), F32),
         jax.ShapeDtypeStruct((nt * nc, 4, 128, DK), F32), jax.ShapeDtypeStruct((t + TM, 256), F32),
         jax.ShapeDtypeStruct((t + TM, 256), F32)],
        [pltpu.VMEM((4, 128, DK), F32)], _cp(40), (proj, proj, proj, proj, proj, wgu, bg, gnw), comm)


def _gla_bwd(proj, oraw, sst, bcum, dgate, do, wgu, gnw, t, comm=None):
    nblk = t // TM
    nt = nblk + 1
    nc = TM // CH

    def blk(i):
        return (2 * nblk - i) % nt

    def body(q_ref, k_ref, v_ref, r_ref, lr_ref, oraw_ref, sst_ref, b_ref, dgate_ref, do_ref, wgu_ref, gnw_ref,
             dgla_ref, dlr_ref, dwgu_ref, dbg_ref, dgnw_ref, dst_scr):
        i = pl.program_id(0)

        @pl.when(i == 0)
        def _():
            dst_scr[...] = jnp.zeros_like(dst_scr)
            dwgu_ref[...] = jnp.zeros_like(dwgu_ref)
            dbg_ref[...] = jnp.zeros_like(dbg_ref)
            dgnw_ref[...] = jnp.zeros_like(dgnw_ref)

        _, upper = _chunk_masks()
        lr = lr_ref[...]
        b = b_ref[...]
        q = q_ref[...]
        k = k_ref[...]
        eb, enb, ebl, eblast = _gla_decays(q, k, b)
        qt = q * 0.125 * eb
        kt = k * enb
        kh = k * ebl
        gn = gnw_ref[...]
        tril = _tri(True)
        triu = _tri(False)
        heads = range(4)
        hs = [slice(h * DK, (h + 1) * DK) for h in heads]
        vs = [slice(h * 128, (h + 1) * 128) for h in heads]
        ein = functools.partial(jnp.einsum, preferred_element_type=F32)
        dr_l, doh = [], []
        dgn = jnp.zeros((1, 128), F32)
        for h in heads:
            on, rs = _rms(oraw_ref[:, vs[h]])
            r = r_ref[:, vs[h]]
            sig = jax.nn.sigmoid(r)
            sil = r * sig
            dy = do_ref[:, vs[h]]
            dr_l.append(dy * on * gn * (sig * (1.0 + r * (1.0 - sig))))
            dgn = dgn + jnp.sum(dy * sil * on, axis=0, keepdims=True)
            doh.append(_bf(_rms_bwd(dy * sil, on, rs, gn)).reshape(nc, CH, 128))
        dgnw_ref[...] += dgn
        qh = [_bf(qt[:, hs[h]]).reshape(nc, CH, DK) for h in heads]
        kth = [_bf(kt[:, hs[h]]).reshape(nc, CH, DK) for h in heads]
        khh = [_bf(kh[:, hs[h]]).reshape(nc, CH, DK) for h in heads]
        vh = [_bf(v_ref[:, vs[h]]).reshape(nc, CH, 128) for h in heads]
        at = [ein('cjd,cid->cji', kth[h], qh[h]) for h in heads]
        da = [ein('civ,cjv->cij', doh[h], vh[h]) for h in heads]
        dat = [ein('cjv,civ->cji', vh[h], doh[h]) for h in heads]
        gq = [ein('civ,cid->cvd', doh[h], qh[h]) for h in heads]
        stf = [sst_ref[:, h] for h in heads]
        dqs = [ein('civ,cvd->cid', doh[h], _bf(stf[h])) for h in heads]
        dv = [ein('cji,civ->cjv', _bf(jnp.where(triu, at[h], 0.0)), doh[h]) for h in heads]
        dqt = [ein('cij,cjd->cid', _bf(jnp.where(tril, da[h], 0.0)), kth[h]) + dqs[h] for h in heads]
        dkt = [ein('cji,cid->cjd', _bf(jnp.where(triu, dat[h], 0.0)), qh[h]) for h in heads]
        dse = []
        for h in heads:
            dst = dst_scr[h]
            dsend = [None] * nc
            for c in reversed(range(nc)):
                dsend[c] = dst
                dst = dst * eblast[c, :, hs[h]] + gq[h][c]
            dst_scr[h] = dst
            dse.append(jnp.stack(dsend))
        dseb = [_bf(d) for d in dse]
        dv = [dv[h] + ein('cjd,cvd->cjv', khh[h], dseb[h]) for h in heads]
        dkh = [ein('cjv,cvd->cjd', vh[h], dseb[h]) for h in heads]
        carried = jnp.concatenate([jnp.sum(dse[h] * stf[h], axis=1, keepdims=True) for h in heads], axis=2)
        wide = lambda parts: jnp.concatenate([p.reshape(TM, DK) for p in parts], axis=1)
        dqt_w, dkt_w, dkh_w = wide(dqt), wide(dkt), wide(dkh)
        dkh_kh = dkh_w * kh
        extra = jnp.sum(dkh_kh.reshape(nc, CH, 256), axis=1, keepdims=True) + eblast * carried
        db = dqt_w * qt - dkt_w * kt - dkh_kh
        dg = _dot3(upper, db) + jnp.broadcast_to(extra, (nc, CH, 256)).reshape(TM, 256)
        dz = dg * dgate_ref[...]
        dzb = _bf(dz)
        dlr_ref[...] = _bf(_dot_nt(dzb, wgu_ref[...]))
        dwgu_ref[...] += _dot_tn(_bf(lr), dzb)
        dbg_ref[...] += jnp.sum(dz, axis=0, keepdims=True)
        dq = dqt_w * eb * 0.125
        dk = dkt_w * enb + dkh_w * ebl
        dgla_ref[...] = _bf(jnp.concatenate([dq, dk] + [d.reshape(TM, 128) for d in dv] + dr_l, axis=1))

    def spec(w, cb):
        return pl.BlockSpec((TM, w), lambda i: (blk(i), cb))

    def acc(shape):
        return pl.BlockSpec(shape, lambda i: (0, 0))

    return _call(
        body, "gla_bwd", (nt,),
        [spec(256, 0), spec(256, 1), spec(512, 1), spec(512, 2), spec(128, C_LR // 128), spec(512, 0),
         pl.BlockSpec((nc, 4, 128, DK), lambda i: (blk(i), 0, 0, 0)), spec(256, 0), spec(256, 0), spec(512, 0),
         VMEM_SPEC, VMEM_SPEC],
        [spec(1536, 0), spec(128, 0), acc((128, 256)), acc((1, 256)), acc((1, 128))],
        [jax.ShapeDtypeStruct((t + TM, 1536), BF16), jax.ShapeDtypeStruct((t + TM, 128), BF16),
         jax.ShapeDtypeStruct((128, 256), F32), jax.ShapeDtypeStruct((1, 256), F32), jax.ShapeDtypeStruct((1, 128), F32)],
        [pltpu.VMEM((4, 128, DK), F32)], _cp(48), (proj, proj, proj, proj, proj, oraw, sst, bcum, dgate, do, wgu, gnw), comm)


def _rope_tables(t):
    r = t + TM
    row = np.arange(r)
    pos = np.where(row < t, row + NM, np.where(row < t + NM, row - t, 0)).astype(np.float32)
    inv_freq = (1.0 / (np.float32(ROPE_THETA) ** (np.arange(0, 16, 2, dtype=np.float32) / np.float32(16)))).astype(np.float32)
    ang = (pos[:, None] * inv_freq[None, :]).astype(np.float32)
    cos, sin = np.cos(ang).astype(np.float32), np.sin(ang).astype(np.float32)
    one, zero = np.ones((r, 48), np.float32), np.zeros((r, 48), np.float32)
    return jnp.asarray(np.concatenate([cos, cos, one, -sin, sin, zero], axis=1))


def _rope(x, tab, sign):
    w = x.shape[1]
    rep = w // 64
    c = jnp.concatenate([tab[:, 0:64]] * rep, axis=1)
    s = jnp.concatenate([tab[:, 64:128]] * rep, axis=1)
    lane = lax.rem(lax.broadcasted_iota(jnp.int32, x.shape, 1), 64)
    partner = jnp.where(lane < 8, pltpu.roll(x, w - 8, 1), jnp.where(lane < 16, pltpu.roll(x, 8, 1), 0.0))
    return x * c + sign * (partner * s)


HB_BWD = 4


def _stack(x, hg):
    w = x.shape[1] // hg
    return x if hg == 1 else jnp.concatenate([x[:, g * w:(g + 1) * w] for g in range(hg)], axis=0)


def _unstack(x, hg):
    return x if hg == 1 else jnp.concatenate([x[g * SB:(g + 1) * SB] for g in range(hg)], axis=1)


def _swa_steps(r_tot):
    blocks = r_tot // SB
    return next(n for n in (6, 3, 2) if blocks % n == 0 and blocks // n >= 2)


def _swa_specs(nsb, nbq):
    def rows(h, w, cb, f):
        return pl.BlockSpec((h, w), lambda i: (f(i), cb))
    pair = lambda i: i
    prev = lambda i: jnp.maximum(nbq * i - 1, 0)
    meta = lambda i: nsb
    return rows, pair, prev, meta


def _swa_fwd(proj, sinks, t, comm=None):
    nsb = t // SB
    r_tot = t + TM
    nbq = _swa_steps(r_tot)
    qb = nbq * SB
    rows, pair, prev, meta = _swa_specs(nsb, nbq)

    def body(sink_ref, q_ref, kc_ref, kp_ref, km_ref, vc_ref, vp_ref, vm_ref, o_ref, lse_ref):
        i = pl.program_id(0)
        key = lax.broadcasted_iota(jnp.int32, (SB, SB), 0)
        qry = lax.broadcasted_iota(jnp.int32, (SB, SB), 1)
        km, vm = km_ref[0:NM, :], vm_ref[0:NM, :]
        for j in range(nbq):
            b = nbq * i + j
            rs = slice(j * SB, (j + 1) * SB)
            before = slice((j - 1) * SB, j * SB)
            real = b < nsb
            masks = (key <= qry, (key > qry) & (b > 0) & real, real)
            k3 = (kc_ref[rs, :], kp_ref[...] if j == 0 else kc_ref[before, :], km)
            v3 = (vc_ref[rs, :], vp_ref[...] if j == 0 else vc_ref[before, :], vm)
            valid = b * SB + lax.broadcasted_iota(jnp.int32, (1, SB), 1) < t + NM
            heads = range(8)
            kb = [[_bf(k[:, kv * 64:(kv + 1) * 64]) for k in k3] for kv in range(2)]
            vt = [[_bf(v[:, kv * 64:(kv + 1) * 64].T) for v in v3] for kv in range(2)]
            raw = [[_dot_nt(k, _bf(q_ref[rs, h * 64:(h + 1) * 64])) for k in kb[h // 4]] for h in heads]
            probs, inv_l, lse_l = [], [], []
            for h in heads:
                s = [jnp.where(m, sx, NEG) for m, sx in zip(masks, raw[h])]
                sink = sink_ref[0, h]
                top = jnp.maximum(jnp.max(jnp.maximum(s[0], s[1]), axis=0, keepdims=True),
                                  jnp.maximum(jnp.max(s[2], axis=0, keepdims=True), sink))
                p = [jnp.exp(sx - top) for sx in s]
                l = (jnp.sum(p[0] + p[1], axis=0, keepdims=True) + jnp.sum(p[2], axis=0, keepdims=True)
                     + jnp.exp(sink - top))
                probs.append([_bf(px) for px in p])
                inv_l.append(1.0 / l)
                lse_l.append(top + jnp.log(l))
            o_t = [_dot(vt[h // 4][0], probs[h][0]) + _dot(vt[h // 4][1], probs[h][1]) + _dot(vt[h // 4][2], probs[h][2])
                   for h in heads]
            o_ref[rs, :] = _bf(jnp.concatenate([jnp.where(valid, o_t[h] * inv_l[h], 0.0).T for h in heads], axis=1))
            lse_ref[:, rs] = jnp.concatenate(lse_l, axis=0)

    ck, cv = C_SK // 128, C_SV // 128
    return _call(
        body, "swa_fwd", (r_tot // qb,),
        [SMEM_SPEC, rows(qb, 512, C_SQ // 512, pair),
         rows(qb, 128, ck, pair), rows(SB, 128, ck, prev), rows(SB, 128, ck, meta),
         rows(qb, 128, cv, pair), rows(SB, 128, cv, prev), rows(SB, 128, cv, meta)],
        [rows(qb, 512, 0, pair), pl.BlockSpec((8, qb), lambda i: (0, i))],
        [jax.ShapeDtypeStruct((r_tot, 512), BF16), jax.ShapeDtypeStruct((8, r_tot), F32)],
        [], _cp(32), (sinks, proj, proj, proj, proj, proj, proj, proj), comm)


def _swa_bwd(proj, sinks, lse_t, do, t, comm=None):
    nsb = t // SB
    r_tot = t + TM
    nbq = _swa_steps(r_tot)
    qb = nbq * SB
    rows, pair, prev, meta = _swa_specs(nsb, nbq)
    hb = HB_BWD
    lanes = hb * SB

    def body(sink_ref, q_ref, kc_ref, kp_ref, km_ref, vc_ref, vp_ref, vm_ref, lse_ref, do_ref,
             dq_ref, dk_ref, dv_ref, dsink_ref):
        i = pl.program_id(0)

        @pl.when(i == 0)
        def _():
            dk_ref[...] = jnp.zeros_like(dk_ref)
            dv_ref[...] = jnp.zeros_like(dv_ref)
            dsink_ref[...] = jnp.zeros_like(dsink_ref)

        key = lax.broadcasted_iota(jnp.int32, (SB, lanes), 0)
        qry = lax.rem(lax.broadcasted_iota(jnp.int32, (SB, lanes), 1), SB)
        km, vm = km_ref[0:NM, :], vm_ref[0:NM, :]
        dsink_l = []
        for j in range(nbq):
            b = nbq * i + j
            rs = slice(j * SB, (j + 1) * SB)
            before = slice((j - 1) * SB, j * SB)
            real = b < nsb
            masks = (key <= qry, (key > qry) & (b > 0) & real, real)
            k3 = (kc_ref[rs, :], kp_ref[...] if j == 0 else kc_ref[before, :], km)
            v3 = (vc_ref[rs, :], vp_ref[...] if j == 0 else vc_ref[before, :], vm)
            groups = list(range(0, 8, hb))
            kvs = [h0 // 4 for h0 in groups]
            qg = [_bf(_stack(q_ref[rs, h0 * 64:(h0 + hb) * 64], hb)) for h0 in groups]
            dog = [_bf(_stack(do_ref[rs, h0 * 64:(h0 + hb) * 64], hb)) for h0 in groups]
            kb = [[_bf(k[:, kv * 64:(kv + 1) * 64]) for k in k3] for kv in kvs]
            vb = [[_bf(v[:, kv * 64:(kv + 1) * 64]) for v in v3] for kv in kvs]
            s = [[_dot_nt(k, qg[g]) for k in kb[g]] for g in range(len(groups))]
            dp = [[_dot_nt(v, dog[g]) for v in vb[g]] for g in range(len(groups))]
            p, ds, ds_blk = [], [], []
            for g, h0 in enumerate(groups):
                lse_row = jnp.concatenate([lse_ref[h:h + 1, rs] for h in range(h0, h0 + hb)], axis=1)
                sink_row = jnp.concatenate([jnp.full((1, SB), sink_ref[0, h], F32) for h in range(h0, h0 + hb)], axis=1)
                pg = [jnp.exp(jnp.where(m, sx, NEG) - lse_row) for m, sx in zip(masks, s[g])]
                delta = (jnp.sum(pg[0] * dp[g][0] + pg[1] * dp[g][1], axis=0, keepdims=True)
                         + jnp.sum(pg[2] * dp[g][2], axis=0, keepdims=True))
                ds.append([_bf(pp * (dd - delta)) for pp, dd in zip(pg, dp[g])])
                p.append([_bf(pp) for pp in pg])
                ds_row = -jnp.exp(sink_row - lse_row) * delta
                ds_blk += [jnp.sum(ds_row[:, q0 * SB:(q0 + 1) * SB], axis=1, keepdims=True) for q0 in range(hb)]
            dsink_l.append(jnp.concatenate(ds_blk, axis=1))
            dq_t = [_dot_tn(kb[g][0], ds[g][0]) + _dot_tn(kb[g][1], ds[g][1]) + _dot_tn(kb[g][2], ds[g][2])
                    for g in range(len(groups))]
            dq_ref[rs, :] = jnp.concatenate([_unstack(d.T, hb) for d in dq_t], axis=1)
            windows = (pl.ds(pl.multiple_of(b * SB, SB), SB), pl.ds(pl.multiple_of(jnp.maximum(b - 1, 0) * SB, SB), SB),
                       pl.ds(t, NM))
            for x in range(3):
                dk_kv, dv_kv = [], []
                for kv in range(2):
                    mine = [g for g in range(len(groups)) if kvs[g] == kv]
                    dk_kv.append(sum(_dot(ds[g][x], qg[g]) for g in mine))
                    dv_kv.append(sum(_dot(p[g][x], dog[g]) for g in mine))
                dk_ref[windows[x], :] += jnp.concatenate(dk_kv, axis=1)
                dv_ref[windows[x], :] += jnp.concatenate(dv_kv, axis=1)
        dsink_ref[...] += sum(dsink_l)

    ck, cv = C_SK // 128, C_SV // 128
    whole = lambda w: pl.BlockSpec((r_tot, w), lambda i: (0, 0))
    return _call(
        body, "swa_bwd", (r_tot // qb,),
        [SMEM_SPEC, rows(qb, 512, C_SQ // 512, pair),
         rows(qb, 128, ck, pair), rows(SB, 128, ck, prev), rows(SB, 128, ck, meta),
         rows(qb, 128, cv, pair), rows(SB, 128, cv, prev), rows(SB, 128, cv, meta),
         pl.BlockSpec((8, qb), lambda i: (0, i)), rows(qb, 512, 1, pair)],
        [rows(qb, 512, 0, pair), whole(128), whole(128), pl.BlockSpec((1, 8), lambda i: (0, 0))],
        [jax.ShapeDtypeStruct((r_tot, 512), F32), jax.ShapeDtypeStruct((r_tot, 128), F32),
         jax.ShapeDtypeStruct((r_tot, 128), F32), jax.ShapeDtypeStruct((1, 8), F32)],
        [], _cp(48), (sinks, proj, proj, proj, proj, proj, proj, proj, lse_t, do), comm)


HK = D // 2


def _mlp_fwd(x, metapad, tgt, ogla, oswa, wo, wff, w1, w2, wfin):
    t = x.shape[0]
    nblk = t // TM

    def body(x_ref, mp_ref, tgt_ref, og_ref, os_ref, wo_ref, wff_ref, w1a_ref, w1b_ref, w2a_ref, w2b_ref, wfin_ref,
             h1_ref, f_ref, a_ref, dh2_ref, loss_ref, gfin_ref):
        i = pl.program_id(0)

        @pl.when(i == 0)
        def _():
            loss_ref[...] = jnp.zeros_like(loss_ref)
            gfin_ref[...] = jnp.zeros_like(gfin_ref)

        h0 = jnp.where(i == nblk, mp_ref[...], x_ref[...])
        h1 = h0 + _dot(og_ref[...], wo_ref[0:512, :]) + _dot(os_ref[...], wo_ref[512:1024, :])
        h1_ref[...] = h1
        fh, _ = _rms(h1)
        f = _bf(fh * wff_ref[...])
        f_ref[...] = f
        acc = jnp.zeros((TM, D), F32)
        for n in range(4):
            a = _dot(f[:, 0:HK], w1a_ref[n]) + _dot(f[:, HK:D], w1b_ref[n])
            a_ref[:, n * D:(n + 1) * D] = _bf(a)
            zr = jnp.maximum(a, 0.0)
            z = _bf(zr * zr)
            acc = acc + _dot(z[:, 0:HK], w2a_ref[n]) + _dot(z[:, HK:D], w2b_ref[n])
        h2 = h1 + acc
        yh, rs2 = _rms(h2)
        wf = wfin_ref[...]
        real = i < nblk
        e = jnp.where(real, yh * wf - tgt_ref[...], 0.0)
        loss_ref[...] += jnp.sum(jnp.sum(e * e, axis=0, keepdims=True), axis=1, keepdims=True) * (0.5 / D)
        dy = e * (1.0 / D)
        gfin_ref[...] += jnp.sum(dy * yh, axis=0, keepdims=True)
        dh2_ref[...] = _rms_bwd(dy, yh, rs2, wf)

    xs = pl.BlockSpec((TM, D), lambda i: (jnp.minimum(i, nblk - 1), 0))
    rs = lambda w: pl.BlockSpec((TM, w), lambda i: (i, 0))
    r_tot = t + TM
    return pl.pallas_call(
        body, name="mlp_fwd", grid=(nblk + 1,),
        in_specs=[xs, VMEM_SPEC, xs, rs(512), rs(512)] + [VMEM_SPEC] * 7,
        out_specs=[rs(D), rs(D), rs(DFF), rs(D), pl.BlockSpec((1, 1), lambda i: (0, 0)), pl.BlockSpec((1, D), lambda i: (0, 0))],
        out_shape=[jax.ShapeDtypeStruct((r_tot, D), F32), jax.ShapeDtypeStruct((r_tot, D), BF16),
                   jax.ShapeDtypeStruct((r_tot, DFF), BF16), jax.ShapeDtypeStruct((r_tot, D), F32),
                   jax.ShapeDtypeStruct((1, 1), F32), jax.ShapeDtypeStruct((1, D), F32)],
        compiler_params=_cp(56),
    )(x, metapad, tgt, ogla, oswa, wo, wff, *w1, *w2, wfin)


def _mlp_bwd(h1, a, dh2, ogla, oswa, wo, wff, w1, w2):
    r_tot = h1.shape[0]
    nt = r_tot // TM

    def body(h1_ref, a_ref, dh2_ref, og_ref, os_ref, wo_ref, wff_ref, w1a_ref, w1b_ref, w2a_ref, w2b_ref,
             da_ref, dh2b_ref, dh1_ref, do_ref, dwo_ref, gff_ref, dwo_acc):
        i = pl.program_id(0)

        @pl.when(i == 0)
        def _():
            dwo_acc[...] = jnp.zeros_like(dwo_acc)
            gff_ref[...] = jnp.zeros_like(gff_ref)

        dh2 = dh2_ref[...]
        dh2b = _bf(dh2)
        dh2b_ref[...] = dh2b
        dfa = jnp.zeros((TM, HK), F32)
        dfb = jnp.zeros((TM, HK), F32)
        for n in range(4):
            dz = jnp.concatenate([_dot_nt(dh2b, w2a_ref[n]), _dot_nt(dh2b, w2b_ref[n])], axis=1)
            da = _bf(dz * (2.0 * jnp.maximum(a_ref[:, n * D:(n + 1) * D].astype(F32), 0.0)))
            da_ref[:, n * D:(n + 1) * D] = da
            dfa = dfa + _dot_nt(da, w1a_ref[n])
            dfb = dfb + _dot_nt(da, w1b_ref[n])
        df = jnp.concatenate([dfa, dfb], axis=1)
        fh, rs1 = _rms(h1_ref[...])
        gff_ref[...] += jnp.sum(df * fh, axis=0, keepdims=True)
        dh1 = dh2 + _rms_bwd(df, fh, rs1, wff_ref[...])
        dh1_ref[...] = dh1
        dh1b = _bf(dh1)
        do_ref[...] = _dot_nt(dh1b, wo_ref[...])
        dwo_acc[0:512, :] += _dot_tn(og_ref[...], dh1b)
        dwo_acc[512:1024, :] += _dot_tn(os_ref[...], dh1b)

        @pl.when(i == nt - 1)
        def _():
            for s in range(4):
                for hh in range(2):
                    dwo_ref[hh, s] = dwo_acc[(2 * s + hh) * 128:(2 * s + hh + 1) * 128, :]

    rs = lambda w: pl.BlockSpec((TM, w), lambda i: (i, 0))
    return pl.pallas_call(
        body, name="mlp_bwd", grid=(nt,),
        in_specs=[rs(D), rs(DFF), rs(D), rs(512), rs(512)] + [VMEM_SPEC] * 6,
        out_specs=[rs(DFF), rs(D), rs(D), rs(D), VMEM_SPEC, pl.BlockSpec((1, D), lambda i: (0, 0))],
        out_shape=[jax.ShapeDtypeStruct((r_tot, DFF), BF16), jax.ShapeDtypeStruct((r_tot, D), BF16),
                   jax.ShapeDtypeStruct((r_tot, D), F32), jax.ShapeDtypeStruct((r_tot, D), F32),
                   jax.ShapeDtypeStruct((2, 4, 128, D), F32), jax.ShapeDtypeStruct((1, D), F32)],
        scratch_shapes=[pltpu.VMEM((D, D), F32)],
        compiler_params=_cp(56),
    )(h1, a, dh2, ogla, oswa, wo, wff, *w1, *w2)


def _ffn_wgrad(f, a, da, dh2b):
    r_tot = f.shape[0]
    kt = 768 if r_tot % 768 == 0 else TM
    nk = r_tot // kt

    def body(f_ref, a_ref, da_ref, dh2_ref, dw1_ref, dw2_ref, acc1, acc2):
        k = pl.program_id(1)

        @pl.when(k == 0)
        def _():
            acc1[...] = jnp.zeros_like(acc1)
            acc2[...] = jnp.zeros_like(acc2)

        zr = jnp.maximum(a_ref[...], 0.0)
        acc1[...] += _dot_tn(f_ref[...], da_ref[...])
        acc2[...] += _dot_tn(zr * zr, dh2_ref[...])

        @pl.when(k == nk - 1)
        def _():
            for hh in range(2):
                dw1_ref[hh, 0] = acc1[hh * 512:(hh + 1) * 512, :]
                dw2_ref[hh, 0] = acc2[hh * 512:(hh + 1) * 512, :]

    out = pl.BlockSpec((2, 1, 512, D), lambda n, k: (0, n, 0, 0))
    return pl.pallas_call(
        body, name="ffn_wgrad", grid=(4, nk),
        in_specs=[pl.BlockSpec((kt, D), lambda n, k: (k, 0)), pl.BlockSpec((kt, D), lambda n, k: (k, n)),
                  pl.BlockSpec((kt, D), lambda n, k: (k, n)), pl.BlockSpec((kt, D), lambda n, k: (k, 0))],
        out_specs=[out, out],
        out_shape=[jax.ShapeDtypeStruct((2, 4, 512, D), F32)] * 2,
        scratch_shapes=[pltpu.VMEM((D, D), F32), pltpu.VMEM((D, D), F32)],
        compiler_params=_cp(48, ("arbitrary", "arbitrary")),
    )(f, a, da, dh2b)


def _proj_bwd(x, metapad, wm, wt3, tabs, dgla, dswa_q, dsk, dsv, dlr, dh1, comm=None):
    t = x.shape[0]
    nblk = t // TM

    def body(x_ref, mp_ref, wm_ref, w3_ref, tab_ref, dg_ref, dq_ref, dk_ref, dv_ref, dlr_ref, dh1_ref,
             gx_ref, gmeta_ref, dw_ref, gmix_ref, w_ref, acc):
        i = pl.program_id(0)

        @pl.when(i == 0)
        def _():
            _join_shards(w3_ref, w_ref)
            acc[...] = jnp.zeros_like(acc)
            gmix_ref[...] = jnp.zeros_like(gmix_ref)

        h = jnp.where(i == nblk, mp_ref[...], x_ref[...])
        uh, rs = _rms(h)
        wm_v = wm_ref[...]
        u = _bf(uh * wm_v)
        tab = tab_ref[...]
        dq = _bf(_rope(dq_ref[...] * 0.125, tab, -1.0))
        dk = _bf(_rope(dk_ref[...], tab, -1.0))
        parts = ((dg_ref[...], 0, R_LR), (dlr_ref[:, 0:16], R_LR, 16), (dq, R_LR + 16, 512),
                 (dk, R_LR + 528, 128), (_bf(dv_ref[...]), R_LR + 656, 128))
        du = jnp.zeros((TM, D), F32)
        for val, r0, w in parts:
            du = du + _dot(val, w_ref[r0:r0 + w, :])
            acc[r0:r0 + w, :] += _dot_tn(val, u)
        gmix_ref[...] += jnp.sum(du * uh, axis=0, keepdims=True)
        dh0 = dh1_ref[...] + _rms_bwd(du, uh, rs, wm_v)

        @pl.when(i < nblk)
        def _():
            gx_ref[...] = dh0

        @pl.when(i == nblk)
        def _():
            gmeta_ref[...] = dh0[:NM]
            for s in range(4):
                dw_ref[s] = acc[(DIN // 4) * s:(DIN // 4) * (s + 1), :]

    xs = pl.BlockSpec((TM, D), lambda i: (jnp.minimum(i, nblk - 1), 0))
    rs_ = lambda w: pl.BlockSpec((TM, w), lambda i: (i, 0))
    return _call(
        body, "proj_bwd", (nblk + 1,),
        [xs, VMEM_SPEC, VMEM_SPEC, VMEM_SPEC, rs_(128), rs_(1536), rs_(512), rs_(128), rs_(128), rs_(128), rs_(D)],
        [xs, pl.BlockSpec((NM, D), lambda i: (0, 0)), VMEM_SPEC, pl.BlockSpec((1, D), lambda i: (0, 0))],
        [jax.ShapeDtypeStruct((t, D), F32), jax.ShapeDtypeStruct((NM, D), F32),
         jax.ShapeDtypeStruct((4, DIN // 4, D), F32), jax.ShapeDtypeStruct((1, D), F32)],
        [pltpu.VMEM((DIN, D), BF16), pltpu.VMEM((DIN, D), F32)], _cp(56),
        (x, metapad, wm, wt3, tabs, dgla, dswa_q, dsk, dsv, dlr, dh1), comm)


def _place():
    return lax.axis_index("x"), lax.axis_index("y"), lax.axis_index("c")


def _other_chips(x, y):
    return [(1 - x, y), (x, 1 - y), (1 - x, 1 - y)]


def _dma_sems(*counts):
    return tuple(pltpu.SemaphoreType.DMA((k,)) for k in counts)


def _gather_shards(shards, split):
    n = len(shards)
    two = [a for a in range(n) if split[a]]

    def plan(ins, outs, sems):
        isend, irecv, dsend, drecv, loc = sems
        x, y, c = _place()
        chips = _other_chips(x, y)

        def part(ref, a, half):
            if not split[a]:
                return ref
            w = shards[a].shape[1] // 2
            return ref.at[:, pl.ds(pl.multiple_of(half * w, 128), w)]

        def over_ici(a, k, shard_of):
            tx, ty = chips[k]
            sx, sy = shard_of
            return pltpu.make_async_remote_copy(
                src_ref=part(ins[a], a, c), dst_ref=part(outs[a].at[2 * sx + sy], a, c), send_sem=isend.at[3 * a + k],
                recv_sem=irecv.at[3 * a + k], device_id=(tx, ty, c), device_id_type=MESH)

        def over_d2d(a, k, half):
            tx, ty = chips[k]
            ref = part(outs[a].at[2 * tx + ty], a, half)
            return pltpu.make_async_remote_copy(
                src_ref=ref, dst_ref=ref, send_sem=dsend.at[3 * a + k], recv_sem=drecv.at[3 * a + k],
                device_id=(x, y, 1 - c), device_id_type=MESH)

        def local(a):
            return pltpu.make_async_copy(ins[a], outs[a].at[2 * x + y], loc.at[a])

        pairs = [(a, k) for a in range(n) for k in range(3)]
        first = ([lambda a=a: local(a).start() for a in range(n)]
                 + [lambda a=a, k=k: over_ici(a, k, (x, y)).start() for a, k in pairs],
                 [lambda a=a, k=k: over_ici(a, k, chips[k]).wait_recv() for a, k in pairs]
                 + [lambda a=a, k=k: over_ici(a, k, (x, y)).wait_send() for a, k in pairs]
                 + [lambda a=a: local(a).wait() for a in range(n)])
        pairs2 = [(a, k) for a in two for k in range(3)]
        second = ([lambda a=a, k=k: over_d2d(a, k, c).start() for a, k in pairs2],
                  [lambda a=a, k=k: over_d2d(a, k, 1 - c).wait_recv() for a, k in pairs2]
                  + [lambda a=a, k=k: over_d2d(a, k, c).wait_send() for a, k in pairs2])
        return [first, second] if two else [first]

    return _Comm(tuple(shards), tuple(jax.ShapeDtypeStruct((4,) + s.shape, s.dtype) for s in shards),
                 _dma_sems(3 * n, 3 * n, 3 * n, 3 * n, n), 2 if two else 1, plan)


def _swap_halves(grads):
    n = len(grads)

    def plan(ins, outs, sems):
        send, recv = sems
        x, y, c = _place()

        def swap(a):
            return pltpu.make_async_remote_copy(
                src_ref=ins[a].at[1 - c], dst_ref=outs[a], send_sem=send.at[a], recv_sem=recv.at[a],
                device_id=(x, y, 1 - c), device_id_type=MESH)

        return [([lambda a=a: swap(a).start() for a in range(n)], [lambda a=a: swap(a).wait() for a in range(n)])]

    return _Comm(tuple(grads), tuple(jax.ShapeDtypeStruct(g.shape[1:], g.dtype) for g in grads), _dma_sems(n, n), 1, plan)


SCATTER_ADD_ROWS = 128


def _scatter_shards(parts, late):
    n = len(parts)

    def plan(ins, outs, sems):
        send, recv, loc = sems[:3]
        onward, got = sems[3:3 + n], sems[3 + n:]
        x, y, c = _place()
        x_first = c == 0
        near = (jnp.where(x_first, 1 - x, x), jnp.where(x_first, y, 1 - y))
        far = (jnp.where(x_first, x, 1 - x), jnp.where(x_first, 1 - y, y))
        diagonal = (1 - x, 1 - y)
        shard = lambda chip: 2 * chip[0] + chip[1]

        def hop(src, dst, k, chip):
            return pltpu.make_async_remote_copy(src_ref=src, dst_ref=dst, send_sem=send.at[k], recv_sem=recv.at[k],
                                                device_id=(chip[0], chip[1], c), device_id_type=MESH)

        theirs = lambda a: hop(ins[a].at[shard(near)], outs[a].at[0], 3 * a, near)
        passing = lambda a: hop(ins[a].at[shard(diagonal)], got[a], 3 * a + 1, near)
        summed = lambda a: hop(onward[a], outs[a].at[1], 3 * a + 2, far)
        mine = lambda a: pltpu.make_async_copy(ins[a].at[shard(far)], onward[a], loc.at[a])

        def add(a):
            for r in range(0, parts[a].shape[1], SCATTER_ADD_ROWS):
                rows = slice(r, r + SCATTER_ADD_ROWS)
                onward[a][rows, :] = _bf(onward[a][rows, :].astype(F32) + got[a][rows, :].astype(F32))

        every = range(n)
        first = ([lambda a=a: mine(a).start() for a in every] + [lambda a=a: passing(a).start() for a in every]
                 + [lambda a=a: theirs(a).start() for a in every],
                 [lambda a=a: mine(a).wait() for a in every] + [lambda a=a: passing(a).wait_recv() for a in every]
                 + [lambda a=a: add(a) for a in every])
        second = ([lambda a=a: summed(a).start() for a in every],
                  [lambda a=a: passing(a).wait_send() for a in every] + [lambda a=a: theirs(a).wait() for a in every]
                  + [lambda a=a: summed(a).wait() for a in every])
        return [first, second]

    assert all(p.shape[1] % SCATTER_ADD_ROWS == 0 for p in parts)
    buffers = [pltpu.VMEM(p.shape[1:], p.dtype) for p in parts]
    return _Comm(tuple(parts), tuple(jax.ShapeDtypeStruct((2,) + p.shape[1:], p.dtype) for p in parts),
                 _dma_sems(3 * n, 3 * n, n) + tuple(buffers) * 2, 2, plan, late)


def _join_halves(halves):
    n = len(halves)

    def plan(ins, outs, sems):
        send, recv, loc = sems
        x, y, c = _place()

        def remote(a, half):
            return pltpu.make_async_remote_copy(
                src_ref=ins[a], dst_ref=outs[a].at[half], send_sem=send.at[a], recv_sem=recv.at[a],
                device_id=(x, y, 1 - c), device_id_type=MESH)

        def local(a):
            return pltpu.make_async_copy(ins[a], outs[a].at[c], loc.at[a])

        every = range(n)
        return [([lambda a=a: local(a).start() for a in every] + [lambda a=a: remote(a, c).start() for a in every],
                 [lambda a=a: remote(a, 1 - c).wait_recv() for a in every]
                 + [lambda a=a: remote(a, c).wait_send() for a in every] + [lambda a=a: local(a).wait() for a in every])]

    return _Comm(tuple(halves), tuple(jax.ShapeDtypeStruct((2,) + h.shape, h.dtype) for h in halves),
                 _dma_sems(n, n, n), 1, plan)


def _reduce_w_in(dwt, comm):
    rows, hw = DIN // 4, D // 2
    ci, co = len(comm.ins), len(comm.outs)

    def body(*refs):
        dw_ref, c_in, out_ref, c_out = refs[0], refs[1:1 + ci], refs[1 + ci], refs[2 + ci:2 + ci + co]
        mine, sib, tosend, rbuf, qbuf, full, send, recv, loc = refs[2 + ci + co:11 + ci + co]
        c_sem = refs[11 + ci + co:]
        x, y, c = _place()
        sibling = (x, y, 1 - c)
        (starts, waits), = comm.plan(c_in, c_out, c_sem)
        _run_phase(starts)

        def cols(ref, half):
            window = pl.ds(pl.multiple_of(half * hw, 128), hw)
            return ref.at[:, :, window] if len(ref.shape) == 3 else ref.at[:, window]

        load = pltpu.make_async_copy(cols(dw_ref, c), mine, loc.at[0])
        give = pltpu.make_async_remote_copy(src_ref=cols(dw_ref, 1 - c), dst_ref=sib, send_sem=send.at[3], recv_sem=recv.at[3],
                                            device_id=sibling, device_id_type=MESH)
        load.start()
        give.start()
        load.wait()
        give.wait()
        mine[...] = mine[...] + sib[...]
        cps = []
        for k, (tx, ty) in enumerate(_other_chips(x, y)):
            tosend[k] = _bf(mine[2 * tx + ty])
            cps.append(pltpu.make_async_remote_copy(
                src_ref=tosend.at[k], dst_ref=rbuf.at[k], send_sem=send.at[k], recv_sem=recv.at[k],
                device_id=(tx, ty, c), device_id_type=MESH))
            cps[-1].start()
        for cp in cps:
            cp.wait()
        qbuf[...] = mine[2 * x + y] + rbuf[0].astype(F32) + rbuf[1].astype(F32) + rbuf[2].astype(F32)
        keep = pltpu.make_async_copy(qbuf, cols(full, c), loc.at[1])
        pass_on = pltpu.make_async_remote_copy(src_ref=qbuf, dst_ref=cols(full, c), send_sem=send.at[4], recv_sem=recv.at[4],
                                               device_id=sibling, device_id_type=MESH)
        keep.start()
        pass_on.start()
        keep.wait()
        pass_on.wait_send()
        pltpu.make_async_remote_copy(src_ref=qbuf, dst_ref=cols(full, 1 - c), send_sem=send.at[4], recv_sem=recv.at[4],
                                     device_id=sibling, device_id_type=MESH).wait_recv()
        out_ref[...] = full[...]
        _run_phase(waits)

    outs = pl.pallas_call(
        body, name="reduce_w_in",
        in_specs=[ANY_SPEC] * (1 + ci), out_specs=[VMEM_SPEC] + [ANY_SPEC] * co,
        out_shape=[jax.ShapeDtypeStruct((rows, D), F32)] + list(comm.outs),
        scratch_shapes=[pltpu.VMEM((4, rows, hw), F32), pltpu.VMEM((4, rows, hw), F32), pltpu.VMEM((3, rows, hw), BF16),
                        pltpu.VMEM((3, rows, hw), BF16), pltpu.VMEM((rows, hw), F32), pltpu.VMEM((rows, D), F32),
                        *_dma_sems(5, 5, 2), *comm.sems],
        compiler_params=pltpu.CompilerParams(vmem_limit_bytes=48 << 20),
    )(dwt, *comm.ins)
    return outs[0], outs[1:]


def _allreduce_small(pack):
    p = pack.shape[0]

    def body(in_ref, out_ref, buf, send, recv):
        x, y, c = _place()
        me = 4 * x + 2 * y + c
        buf[me] = in_ref[...]

        def peer_of(k):
            return x ^ (k >> 2), y ^ ((k >> 1) & 1), c ^ (k & 1)

        sends = [pltpu.make_async_remote_copy(
            src_ref=in_ref, dst_ref=buf.at[me], send_sem=send.at[k - 1], recv_sem=recv.at[k - 1],
            device_id=peer_of(k), device_id_type=MESH) for k in range(1, 8)]
        for cp in sends:
            cp.start()
        for k in range(1, 8):
            px, py, pc = peer_of(k)
            pltpu.make_async_remote_copy(
                src_ref=in_ref, dst_ref=buf.at[4 * px + 2 * py + pc], send_sem=send.at[k - 1], recv_sem=recv.at[k - 1],
                device_id=(x, y, c), device_id_type=MESH).wait_recv()
        for cp in sends:
            cp.wait_send()
        acc = buf[0]
        for d in range(1, 8):
            acc = acc + buf[d]
        out_ref[...] = acc

    return pl.pallas_call(
        body, name="allreduce_small",
        in_specs=[VMEM_SPEC], out_specs=VMEM_SPEC, out_shape=jax.ShapeDtypeStruct(pack.shape, F32),
        scratch_shapes=[pltpu.VMEM((8, p, D), F32), *_dma_sems(7, 7)],
    )(pack)


GRID4 = 4


def _sum_cores(core_shard, mine, theirs):
    n = len(mine)

    def body(cs_ref, *refs):
        ms, ts, bfs, owns = refs[:n], refs[n:2 * n], refs[2 * n:3 * n], refs[3 * n:]
        keep = pl.program_id(0) == cs_ref[1]
        for a in range(n):
            acc = ms[a][0, 0] + ts[a][0]
            bfs[a][0] = _bf(acc)

            @pl.when(keep)
            def _():
                owns[a][...] = acc

    shapes = [m.shape[2:] for m in mine]
    in_specs = ([pl.BlockSpec((1, 1) + s, lambda i, cs: (cs[0], i, 0, 0)) for s in shapes]
                + [pl.BlockSpec((1,) + s, lambda i, cs: (i, 0, 0)) for s in shapes])
    out_specs = ([pl.BlockSpec((1,) + s, lambda i, cs: (i, 0, 0)) for s in shapes]
                 + [pl.BlockSpec(s, lambda i, cs: (0, 0)) for s in shapes])
    outs = pl.pallas_call(
        body, name="sum_cores",
        grid_spec=pltpu.PrefetchScalarGridSpec(num_scalar_prefetch=1, grid=(4,), in_specs=in_specs, out_specs=out_specs),
        out_shape=[jax.ShapeDtypeStruct((4,) + s, BF16) for s in shapes] + [jax.ShapeDtypeStruct(s, F32) for s in shapes],
        compiler_params=_cp(48),
    )(core_shard, *mine, *theirs)
    return outs[:n], outs[n:]


def _sum_chips(own, arrived):
    n = len(own)

    def body(*refs):
        os_, ars, outs = refs[:n], refs[n:2 * n], refs[2 * n:]
        for a in range(n):
            outs[a][...] = os_[a][...] + ars[a][0].astype(F32) + ars[a][1].astype(F32)

    blocks = [(o.shape[0] // GRID4, o.shape[1]) for o in own]
    return pl.pallas_call(
        body, name="sum_chips", grid=(GRID4,),
        in_specs=([pl.BlockSpec(b, lambda i: (i, 0)) for b in blocks]
                  + [pl.BlockSpec((2,) + b, lambda i: (0, i, 0)) for b in blocks]),
        out_specs=[pl.BlockSpec(b, lambda i: (i, 0)) for b in blocks],
        out_shape=[jax.ShapeDtypeStruct(o.shape, F32) for o in own],
        compiler_params=_cp(32),
    )(*own, *arrived)


def _adamw_math(w, g, m, v):
    m2 = ADAM_B1 * m + (1.0 - ADAM_B1) * g
    v2 = ADAM_B2 * v + (1.0 - ADAM_B2) * (g * g)
    m_hat = m2 / (1.0 - ADAM_B1 ** ADAM_STEP)
    v_hat = v2 / (1.0 - ADAM_B2 ** ADAM_STEP)
    return -ADAM_LR * (m_hat / (jnp.sqrt(v_hat) + ADAM_EPS) + ADAM_WD * w), m2, v2


def _adamw_big(ws, gs, ms, vs):
    n = len(ws)

    def body(*refs):
        for a in range(n):
            d, m2, v2 = _adamw_math(refs[a][...], refs[n + a][...], refs[2 * n + a][...], refs[3 * n + a][...])
            refs[4 * n + a][...] = d
            refs[5 * n + a][...] = m2
            refs[6 * n + a][...] = v2

    specs = [pl.BlockSpec((w.shape[0] // GRID4, w.shape[1]), lambda i: (i, 0)) for w in ws]
    return pl.pallas_call(
        body, name="adamw_big", grid=(GRID4,),
        in_specs=specs * 4, out_specs=specs * 3,
        out_shape=[jax.ShapeDtypeStruct(w.shape, F32) for w in ws] * 3,
        compiler_params=_cp(48),
    )(*ws, *gs, *ms, *vs)


def _adamw_small(ws, gs, ms, vs):
    n = len(ws)

    def body(*refs):
        for a in range(n):
            d, m2, v2 = _adamw_math(refs[a][...], refs[n + a][...], refs[2 * n + a][...], refs[3 * n + a][...])
            refs[4 * n + a][...] = d
            refs[5 * n + a][...] = m2
            refs[6 * n + a][...] = v2

    return pl.pallas_call(
        body, name="adamw_small",
        in_specs=[VMEM_SPEC] * (4 * n), out_specs=[VMEM_SPEC] * (3 * n),
        out_shape=[jax.ShapeDtypeStruct(w.shape, F32) for w in ws] * 3,
        compiler_params=pltpu.CompilerParams(vmem_limit_bytes=40 << 20),
    )(*ws, *gs, *ms, *vs)


def kernel(x, meta_tokens, norm_mix_w, w_in, w_gate_up, b_gate, gla_norm_w, sinks, w_out, norm_ff_w, w_ff1, w_ff2, final_norm_w, loss_target, m_meta_tokens, m_norm_mix_w, m_w_in, m_w_gate_up, m_b_gate, m_gla_norm_w, m_sinks, m_w_out, m_norm_ff_w, m_w_ff1, m_w_ff2, m_final_norm_w, v_meta_tokens, v_norm_mix_w, v_w_in, v_w_gate_up, v_b_gate, v_gla_norm_w, v_sinks, v_w_out, v_norm_ff_w, v_w_ff1, v_w_ff2, v_final_norm_w):
    xi, yi, ci = _place()
    shard = (2 * xi + yi).astype(jnp.int32).reshape(1)
    core = ci.astype(jnp.int32).reshape(1)

    small = jnp.concatenate([meta_tokens, w_gate_up[0], jnp.zeros((NM, 64), F32)], axis=1)
    wt3, g_small = _run_comm(_gather_shards([_bf(w_in[0].T), small], [True, False]), "gather_w_in")
    meta = g_small[:, :, 0:256].transpose(1, 0, 2).reshape(NM, D)
    wgu = g_small[:, :, 256:320].transpose(1, 0, 2).reshape(NM, 256)

    xs, tgt = x[0], loss_target[0]
    t = xs.shape[0]
    wfin = final_norm_w.reshape(1, D)
    metapad = jnp.concatenate([meta, jnp.zeros((TM - NM, D), F32)], axis=0)
    wgu_p = _bf(jnp.concatenate([wgu, jnp.zeros((128 - 16, 256), F32)], axis=0))
    tabs = _rope_tables(t)

    w1s, w2s = _bf(w_ff1[0]), _bf(w_ff2[0])
    proj, (g_out, w1a) = _proj_fwd(xs, metapad, norm_mix_w, wt3, tabs,
                                   _gather_shards([_bf(w_out[0]), w1s[:HK]], [True] * 2))
    (ogla, oraw, sst, bcum, dgate), (w1b, w2a) = _gla_fwd(proj, wgu_p, b_gate, gla_norm_w, t,
                                                          _gather_shards([w1s[HK:], w2s[:HK]], [True] * 2))
    (oswa, lse), (w2b,) = _swa_fwd(proj, sinks, t, _gather_shards([w2s[HK:]], [True]))
    wo, w1, w2 = g_out.reshape(D, D), (w1a, w1b), (w2a, w2b)
    h1, f, a, dh2, loss, gfin = _mlp_fwd(xs, metapad, tgt, ogla, oswa, wo, norm_ff_w, w1, w2, wfin)

    da, dh2b, dh1, do, dwo, gff = _mlp_bwd(h1, a, dh2, ogla, oswa, wo, norm_ff_w, w1, w2)
    dw1, dw2 = _ffn_wgrad(f, a, da, dh2b)
    big = [dwo, dw1, dw2]
    (dgla, dlr, dwgu, dbg, dgnw), theirs = _gla_bwd(proj, oraw, sst, bcum, dgate, do, wgu_p, gla_norm_w, t,
                                                    _swap_halves(big))
    sums_bf, own = _sum_cores(jnp.concatenate([core, shard]), big, theirs)
    swa_grid = (t + TM) // SB // _swa_steps(t + TM)
    (dsq, dsk, dsv, dsink), arrived = _swa_bwd(proj, sinks, lse, do, t, _scatter_shards(sums_bf, swa_grid // 3))
    halves = _sum_chips(own, arrived)
    (gx, gmeta, dwt, gmix), _ = _proj_bwd(xs, metapad, norm_mix_w, wt3, tabs, dgla, dsq, dsk, dsv, dlr, dh1)

    gwt_in, joined = _reduce_w_in(dwt, _join_halves(halves))
    gw_out, gw_1, gw_2 = [j.reshape((-1, j.shape[2])) for j in joined]

    tail = jnp.concatenate([dbg, dgnw, dsink, loss, jnp.zeros((1, D - 256 - 128 - 8 - 1), F32)], axis=1)
    pack = jnp.concatenate([gmeta, gmix, gff, gfin, tail, dwgu[:16].reshape(4, D)], axis=0)
    tot = _allreduce_small(pack)
    g_meta = lax.dynamic_slice_in_dim(tot[0:NM], shard[0] * 256, 256, axis=1)
    g_mix, g_ff, g_fin = tot[16:17], tot[17:18], tot[18]
    g_bg, g_gnw, g_sinks, loss_tot = tot[19:20, 0:256], tot[19:20, 256:384], tot[19:20, 384:392], tot[19, 392]
    g_wgu = lax.dynamic_slice_in_dim(tot[20:24].reshape(NM, 256), shard[0] * 64, 64, axis=1)

    bo = _adamw_big([w_out[0], w_ff1[0], w_ff2[0]], [gw_out, gw_1, gw_2], [m_w_out[0], m_w_ff1[0], m_w_ff2[0]],
                    [v_w_out[0], v_w_ff1[0], v_w_ff2[0]])

    fin2 = lambda a: a.reshape(1, D)
    sw = [meta_tokens, norm_mix_w, w_gate_up[0], b_gate, gla_norm_w, sinks, norm_ff_w, fin2(final_norm_w), w_in[0].T]
    sg = [g_meta, g_mix, g_wgu, g_bg, g_gnw, g_sinks, g_ff, fin2(g_fin), gwt_in]
    sm = [m_meta_tokens, m_norm_mix_w, m_w_gate_up[0], m_b_gate, m_gla_norm_w, m_sinks, m_norm_ff_w, fin2(m_final_norm_w),
          m_w_in[0].T]
    sv = [v_meta_tokens, v_norm_mix_w, v_w_gate_up[0], v_b_gate, v_gla_norm_w, v_sinks, v_norm_ff_w, fin2(v_final_norm_w),
          v_w_in[0].T]
    so = _adamw_small(sw, sg, sm, sv)

    def ordered(small_o, big_o):
        meta_, mix_, wgu_, bg_, gnw_, sinks_, ff_, fin_, wt_ = small_o
        w_out_, w_1_, w_2_ = big_o
        return (meta_, mix_, wt_.T[None], wgu_[None], bg_, gnw_, sinks_, w_out_[None], ff_, w_1_[None], w_2_[None],
                fin_.reshape(D))

    grads = ordered(sg, [gw_out, gw_1, gw_2])
    deltas = ordered(so[0:9], bo[0:3])
    new_m = ordered(so[9:18], bo[3:6])
    new_v = ordered(so[18:27], bo[6:9])
    return (loss_tot, gx[None], *grads, *deltas, *new_m, *new_v)
```

```python
import functools
from typing import Callable, NamedTuple

import jax
import jax.numpy as jnp
import numpy as np
from jax import lax
from jax.experimental import pallas as pl
from jax.experimental.pallas import tpu as pltpu

F32 = jnp.float32
BF16 = jnp.bfloat16

D = 1024
DFF = 4096
NM = 16
TM = 256
DK = 64
CH = 128
SB = 128
EPS = 1e-5
C_GQ, C_GK, C_GV, C_GR, C_SQ, C_SK, C_SV, C_LR, DINP = 0, 256, 512, 1024, 1536, 2048, 2176, 2304, 2432
DIN = 2320
R_LR = 1536
ROPE_THETA = 500000.0
ADAM_LR, ADAM_B1, ADAM_B2, ADAM_EPS, ADAM_WD, ADAM_STEP = 0.001, 0.9, 0.999, 1e-08, 0.01, 10
NEG = -1e30
MESH = pl.DeviceIdType.MESH
VMEM_SPEC = pl.BlockSpec(memory_space=pltpu.VMEM)
ANY_SPEC = pl.BlockSpec(memory_space=pl.ANY)
SMEM_SPEC = pl.BlockSpec(memory_space=pltpu.SMEM)


def _cp(vmem_mb, sem=("arbitrary",)):
    return pltpu.CompilerParams(dimension_semantics=sem, vmem_limit_bytes=vmem_mb << 20)


def _dot(a, b):
    return jnp.dot(a, b, preferred_element_type=F32)


def _dot_nt(a, b):
    return lax.dot_general(a, b, (((1,), (1,)), ((), ())), preferred_element_type=F32)


def _dot_tn(a, b):
    return lax.dot_general(a, b, (((0,), (0,)), ((), ())), preferred_element_type=F32)


def _bf(x):
    return x.astype(BF16)


def _dot3(m01, x):
    x1 = _bf(x)
    r1 = x - x1.astype(F32)
    x2 = _bf(r1)
    x3 = _bf(r1 - x2.astype(F32))
    return _dot(m01, x1) + _dot(m01, x2) + _dot(m01, x3)


def _rms(h):
    rs = lax.rsqrt(jnp.mean(h * h, axis=-1, keepdims=True) + EPS)
    return h * rs, rs


def _rms_bwd(dy, yhat, rs, w):
    dyh = dy * w
    return rs * (dyh - yhat * jnp.mean(dyh * yhat, axis=-1, keepdims=True))


class _Comm(NamedTuple):
    ins: tuple
    outs: tuple
    sems: tuple
    phases: int
    plan: Callable
    late: int = 0


def _run_phase(fns):
    for fn in fns:
        fn()


def _call(body, name, grid, in_specs, out_specs, out_shape, scratch, params, args, comm=None):
    if comm is None:
        outs = pl.pallas_call(body, name=name, grid=grid, in_specs=in_specs, out_specs=out_specs, out_shape=out_shape,
                              scratch_shapes=scratch, compiler_params=params)(*args)
        return outs, None
    n_in, n_out, n_scr = len(in_specs), len(out_specs), len(scratch)
    ci, co = len(comm.ins), len(comm.outs)
    last = grid[0] - 1
    marks = [0, max(1, last - (comm.late or max(2, (last + 1) // 6)))][:comm.phases]

    def wrapped(*refs):
        own_in, c_in = refs[:n_in], refs[n_in:n_in + ci]
        refs = refs[n_in + ci:]
        own_out, c_out = refs[:n_out], refs[n_out:n_out + co]
        refs = refs[n_out + co:]
        own_scr, c_sem = refs[:n_scr], refs[n_scr:]
        i = pl.program_id(0)

        for p, mark in enumerate(marks):
            @pl.when(i == mark)
            def _():
                plan = comm.plan(c_in, c_out, c_sem)
                if p > 0:
                    _run_phase(plan[p - 1][1])
                _run_phase(plan[p][0])

        body(*own_in, *own_out, *own_scr)

        @pl.when(i == last)
        def _():
            _run_phase(comm.plan(c_in, c_out, c_sem)[-1][1])

    outs = pl.pallas_call(
        wrapped, name=name, grid=grid, in_specs=list(in_specs) + [ANY_SPEC] * ci, out_specs=list(out_specs) + [ANY_SPEC] * co,
        out_shape=list(out_shape) + list(comm.outs), scratch_shapes=list(scratch) + list(comm.sems), compiler_params=params,
    )(*args, *comm.ins)
    return outs[:n_out], outs[n_out:]


def _run_comm(comm, name):
    ci, co = len(comm.ins), len(comm.outs)

    def body(*refs):
        for starts, waits in comm.plan(refs[:ci], refs[ci:ci + co], refs[ci + co:]):
            _run_phase(starts)
            _run_phase(waits)

    return pl.pallas_call(body, name=name, in_specs=[ANY_SPEC] * ci, out_specs=[ANY_SPEC] * co, out_shape=list(comm.outs),
                          scratch_shapes=list(comm.sems))(*comm.ins)


def _join_shards(w3_ref, w_ref):
    for s in range(4):
        w_ref[(DIN // 4) * s:(DIN // 4) * (s + 1), :] = w3_ref[s]


def _proj_fwd(x, metapad, wm, wt3, tabs, comm=None):
    t = x.shape[0]
    nblk = t // TM

    def body(x_ref, mp_ref, wm_ref, w3_ref, tab_ref, proj_ref, w_ref):
        i = pl.program_id(0)

        @pl.when(i == 0)
        def _():
            _join_shards(w3_ref, w_ref)

        h = jnp.where(i == nblk, mp_ref[...], x_ref[...])
        u, _ = _rms(h)
        ub = _bf(u * wm_ref[...])
        proj_ref[:, 0:C_SQ] = _dot_nt(ub, w_ref[0:R_LR, :])
        att = _dot_nt(ub, w_ref[R_LR + 16:DIN, :])
        tab = tab_ref[...]
        proj_ref[:, C_SQ:C_SK] = _rope(att[:, 0:512], tab, 1.0) * 0.125
        proj_ref[:, C_SK:C_SV] = _rope(att[:, 512:640], tab, 1.0)
        proj_ref[:, C_SV:C_LR] = att[:, 640:768]
        proj_ref[:, C_LR:DINP] = jnp.zeros((TM, DINP - C_LR), F32)
        proj_ref[:, C_LR:C_LR + 16] = _dot_nt(ub, w_ref[R_LR:R_LR + 16, :])

    (proj,), got = _call(
        body, "proj_fwd", (nblk + 1,),
        [pl.BlockSpec((TM, D), lambda i: (jnp.minimum(i, nblk - 1), 0)), VMEM_SPEC, VMEM_SPEC, VMEM_SPEC,
         pl.BlockSpec((TM, 128), lambda i: (i, 0))],
        [pl.BlockSpec((TM, DINP), lambda i: (i, 0))], [jax.ShapeDtypeStruct((t + TM, DINP), F32)],
        [pltpu.VMEM((DIN, D), BF16)], _cp(48), (x, metapad, wm, wt3, tabs), comm)
    return proj, got


def _chunk_masks():
    r = lax.broadcasted_iota(jnp.int32, (TM, TM), 0)
    c = lax.broadcasted_iota(jnp.int32, (TM, TM), 1)
    same = (r // CH) == (c // CH)
    lower = _bf(jnp.where(same & (c <= r), 1.0, 0.0))
    upper = _bf(jnp.where(same & (c >= r), 1.0, 0.0))
    return lower, upper


def _gla_gate(lr, wgu, bg, valid, lower):
    z = _dot(_bf(lr), wgu) + bg
    g = (jnp.minimum(z, 0.0) - jnp.log(1.0 + jnp.exp(-jnp.abs(z)))) * (1.0 / 16.0)
    g = jnp.where(valid, g, 0.0)
    return z, _dot3(lower, g)


def _gla_decays(q, k, b):
    nc = TM // CH
    b3 = b.reshape(nc, CH, 256)
    blast = b3[:, CH - 1:CH, :]
    eb = jnp.exp(b)
    enb = jnp.exp(-b)
    ebl = jnp.exp(blast - b3).reshape(TM, 256)
    return eb, enb, ebl, jnp.exp(blast)


def _tri(lower_incl):
    r = lax.broadcasted_iota(jnp.int32, (CH, CH), 0)
    c = lax.broadcasted_iota(jnp.int32, (CH, CH), 1)
    return ((c <= r) if lower_incl else (c >= r))[None]


def _gla_fwd(proj, wgu, bg, gnw, t, comm=None):
    nblk = t // TM
    nt = nblk + 1
    nc = TM // CH

    def blk(i):
        return (i + nblk) % nt

    def body(q_ref, k_ref, v_ref, r_ref, lr_ref, wgu_ref, bg_ref, gnw_ref, o_ref, oraw_ref, sst_ref, b_ref, dgate_ref,
             st_scr):
        i = pl.program_id(0)

        @pl.when(i == 0)
        def _():
            st_scr[...] = jnp.zeros_like(st_scr)

        rows = blk(i) * TM + lax.broadcasted_iota(jnp.int32, (TM, 1), 0)
        lower, _ = _chunk_masks()
        valid = rows < t + NM
        z, b = _gla_gate(lr_ref[...], wgu_ref[...], bg_ref[...], valid, lower)
        b_ref[...] = b
        dgate_ref[...] = jnp.where(valid, (1.0 / 16.0) / (1.0 + jnp.exp(z)), 0.0)
        q = q_ref[...]
        k = k_ref[...]
        eb, enb, ebl, eblast = _gla_decays(q, k, b)
        qt = q * 0.125 * eb
        kt = k * enb
        kh = k * ebl
        tril = _tri(True)
        heads = range(4)
        hs = [slice(h * DK, (h + 1) * DK) for h in heads]
        qh = [_bf(qt[:, hs[h]]).reshape(nc, CH, DK) for h in heads]
        kth = [_bf(kt[:, hs[h]]).reshape(nc, CH, DK) for h in heads]
        khh = [_bf(kh[:, hs[h]]).reshape(nc, CH, DK) for h in heads]
        vh = [_bf(v_ref[:, h * 128:(h + 1) * 128]).reshape(nc, CH, 128) for h in heads]
        a = [jnp.einsum('cid,cjd->cij', qh[h], kth[h], preferred_element_type=F32) for h in heads]
        kv = [jnp.einsum('cjv,cjd->cvd', vh[h], khh[h], preferred_element_type=F32) for h in heads]
        o = [jnp.einsum('cij,cjv->civ', _bf(jnp.where(tril, a[h], 0.0)), vh[h], preferred_element_type=F32) for h in heads]
        states = []
        for h in heads:
            st = st_scr[h]
            per_chunk = []
            for c in range(nc):
                sst_ref[c, h] = st
                per_chunk.append(_bf(st))
                st = st * eblast[c, :, hs[h]] + kv[h][c]
            st_scr[h] = st
            states.append(per_chunk)
        o_inter = [[_dot_nt(qh[h][c], states[h][c]) for c in range(nc)] for h in heads]
        oraw = jnp.concatenate([(o[h] + jnp.stack(o_inter[h])).reshape(TM, 128) for h in heads], axis=1)
        oraw_ref[...] = oraw
        gn = gnw_ref[...]
        res = []
        for h in range(4):
            on, _ = _rms(oraw[:, h * 128:(h + 1) * 128])
            r = r_ref[:, h * 128:(h + 1) * 128]
            res.append(on * gn * (r * jax.nn.sigmoid(r)))
        o_ref[...] = _bf(jnp.concatenate(res, axis=1))

    def spec(w, cb):
        return pl.BlockSpec((TM, w), lambda i: (blk(i), cb))

    return _call(
        body, "gla_fwd", (nt,),
        [spec(256, 0), spec(256, 1), spec(512, 1), spec(512, 2), spec(128, C_LR // 128), VMEM_SPEC, VMEM_SPEC, VMEM_SPEC],
        [spec(512, 0), spec(512, 0), pl.BlockSpec((nc, 4, 128, DK), lambda i: (blk(i), 0, 0, 0)), spec(256, 0), spec(256, 0)],
        [jax.ShapeDtypeStruct((t + TM, 512), BF16), jax.ShapeDtypeStruct((t + TM, 512), F32),
         jax.ShapeDtypeStruct((nt * nc, 4, 128, DK), F32), jax.ShapeDtypeStruct((t + TM, 256), F32),
         jax.ShapeDtypeStruct((t + TM, 256), F32)],
        [pltpu.VMEM((4, 128, DK), F32)], _cp(40), (proj, proj, proj, proj, proj, wgu, bg, gnw), comm)


def _gla_bwd(proj, oraw, sst, bcum, dgate, do, wgu, gnw, t, comm=None):
    nblk = t // TM
    nt = nblk + 1
    nc = TM // CH

    def blk(i):
        return (2 * nblk - i) % nt

    def body(q_ref, k_ref, v_ref, r_ref, lr_ref, oraw_ref, sst_ref, b_ref, dgate_ref, do_ref, wgu_ref, gnw_ref,
             dgla_ref, dlr_ref, dwgu_ref, dbg_ref, dgnw_ref, dst_scr):
        i = pl.program_id(0)

        @pl.when(i == 0)
        def _():
            dst_scr[...] = jnp.zeros_like(dst_scr)
            dwgu_ref[...] = jnp.zeros_like(dwgu_ref)
            dbg_ref[...] = jnp.zeros_like(dbg_ref)
            dgnw_ref[...] = jnp.zeros_like(dgnw_ref)

        _, upper = _chunk_masks()
        lr = lr_ref[...]
        b = b_ref[...]
        q = q_ref[...]
        k = k_ref[...]
        eb, enb, ebl, eblast = _gla_decays(q, k, b)
        qt = q * 0.125 * eb
        kt = k * enb
        kh = k * ebl
        gn = gnw_ref[...]
        tril = _tri(True)
        triu = _tri(False)
        heads = range(4)
        hs = [slice(h * DK, (h + 1) * DK) for h in heads]
        vs = [slice(h * 128, (h + 1) * 128) for h in heads]
        ein = functools.partial(jnp.einsum, preferred_element_type=F32)
        dr_l, doh = [], []
        dgn = jnp.zeros((1, 128), F32)
        for h in heads:
            on, rs = _rms(oraw_ref[:, vs[h]])
            r = r_ref[:, vs[h]]
            sig = jax.nn.sigmoid(r)
            sil = r * sig
            dy = do_ref[:, vs[h]]
            dr_l.append(dy * on * gn * (sig * (1.0 + r * (1.0 - sig))))
            dgn = dgn + jnp.sum(dy * sil * on, axis=0, keepdims=True)
            doh.append(_bf(_rms_bwd(dy * sil, on, rs, gn)).reshape(nc, CH, 128))
        dgnw_ref[...] += dgn
        qh = [_bf(qt[:, hs[h]]).reshape(nc, CH, DK) for h in heads]
        kth = [_bf(kt[:, hs[h]]).reshape(nc, CH, DK) for h in heads]
        khh = [_bf(kh[:, hs[h]]).reshape(nc, CH, DK) for h in heads]
        vh = [_bf(v_ref[:, vs[h]]).reshape(nc, CH, 128) for h in heads]
        at = [ein('cjd,cid->cji', kth[h], qh[h]) for h in heads]
        da = [ein('civ,cjv->cij', doh[h], vh[h]) for h in heads]
        dat = [ein('cjv,civ->cji', vh[h], doh[h]) for h in heads]
        gq = [ein('civ,cid->cvd', doh[h], qh[h]) for h in heads]
        stf = [sst_ref[:, h] for h in heads]
        dqs = [ein('civ,cvd->cid', doh[h], _bf(stf[h])) for h in heads]
        dv = [ein('cji,civ->cjv', _bf(jnp.where(triu, at[h], 0.0)), doh[h]) for h in heads]
        dqt = [ein('cij,cjd->cid', _bf(jnp.where(tril, da[h], 0.0)), kth[h]) + dqs[h] for h in heads]
        dkt = [ein('cji,cid->cjd', _bf(jnp.where(triu, dat[h], 0.0)), qh[h]) for h in heads]
        dse = []
        for h in heads:
            dst = dst_scr[h]
            dsend = [None] * nc
            for c in reversed(range(nc)):
                dsend[c] = dst
                dst = dst * eblast[c, :, hs[h]] + gq[h][c]
            dst_scr[h] = dst
            dse.append(jnp.stack(dsend))
        dseb = [_bf(d) for d in dse]
        dv = [dv[h] + ein('cjd,cvd->cjv', khh[h], dseb[h]) for h in heads]
        dkh = [ein('cjv,cvd->cjd', vh[h], dseb[h]) for h in heads]
        carried = jnp.concatenate([jnp.sum(dse[h] * stf[h], axis=1, keepdims=True) for h in heads], axis=2)
        wide = lambda parts: jnp.concatenate([p.reshape(TM, DK) for p in parts], axis=1)
        dqt_w, dkt_w, dkh_w = wide(dqt), wide(dkt), wide(dkh)
        dkh_kh = dkh_w * kh
        extra = jnp.sum(dkh_kh.reshape(nc, CH, 256), axis=1, keepdims=True) + eblast * carried
        db = dqt_w * qt - dkt_w * kt - dkh_kh
        dg = _dot3(upper, db) + jnp.broadcast_to(extra, (nc, CH, 256)).reshape(TM, 256)
        dz = dg * dgate_ref[...]
        dzb = _bf(dz)
        dlr_ref[...] = _bf(_dot_nt(dzb, wgu_ref[...]))
        dwgu_ref[...] += _dot_tn(_bf(lr), dzb)
        dbg_ref[...] += jnp.sum(dz, axis=0, keepdims=True)
        dq = dqt_w * eb * 0.125
        dk = dkt_w * enb + dkh_w * ebl
        dgla_ref[...] = _bf(jnp.concatenate([dq, dk] + [d.reshape(TM, 128) for d in dv] + dr_l, axis=1))

    def spec(w, cb):
        return pl.BlockSpec((TM, w), lambda i: (blk(i), cb))

    def acc(shape):
        return pl.BlockSpec(shape, lambda i: (0, 0))

    return _call(
        body, "gla_bwd", (nt,),
        [spec(256, 0), spec(256, 1), spec(512, 1), spec(512, 2), spec(128, C_LR // 128), spec(512, 0),
         pl.BlockSpec((nc, 4, 128, DK), lambda i: (blk(i), 0, 0, 0)), spec(256, 0), spec(256, 0), spec(512, 0),
         VMEM_SPEC, VMEM_SPEC],
        [spec(1536, 0), spec(128, 0), acc((128, 256)), acc((1, 256)), acc((1, 128))],
        [jax.ShapeDtypeStruct((t + TM, 1536), BF16), jax.ShapeDtypeStruct((t + TM, 128), BF16),
         jax.ShapeDtypeStruct((128, 256), F32), jax.ShapeDtypeStruct((1, 256), F32), jax.ShapeDtypeStruct((1, 128), F32)],
        [pltpu.VMEM((4, 128, DK), F32)], _cp(48), (proj, proj, proj, proj, proj, oraw, sst, bcum, dgate, do, wgu, gnw), comm)


def _rope_tables(t):
    r = t + TM
    row = np.arange(r)
    pos = np.where(row < t, row + NM, np.where(row < t + NM, row - t, 0)).astype(np.float32)
    inv_freq = (1.0 / (np.float32(ROPE_THETA) ** (np.arange(0, 16, 2, dtype=np.float32) / np.float32(16)))).astype(np.float32)
    ang = (pos[:, None] * inv_freq[None, :]).astype(np.float32)
    cos, sin = np.cos(ang).astype(np.float32), np.sin(ang).astype(np.float32)
    one, zero = np.ones((r, 48), np.float32), np.zeros((r, 48), np.float32)
    return jnp.asarray(np.concatenate([cos, cos, one, -sin, sin, zero], axis=1))


def _rope(x, tab, sign):
    w = x.shape[1]
    rep = w // 64
    c = jnp.concatenate([tab[:, 0:64]] * rep, axis=1)
    s = jnp.concatenate([tab[:, 64:128]] * rep, axis=1)
    lane = lax.rem(lax.broadcasted_iota(jnp.int32, x.shape, 1), 64)
    partner = jnp.where(lane < 8, pltpu.roll(x, w - 8, 1), jnp.where(lane < 16, pltpu.roll(x, 8, 1), 0.0))
    return x * c + sign * (partner * s)


HB_BWD = 4


def _stack(x, hg):
    w = x.shape[1] // hg
    return x if hg == 1 else jnp.concatenate([x[:, g * w:(g + 1) * w] for g in range(hg)], axis=0)


def _unstack(x, hg):
    return x if hg == 1 else jnp.concatenate([x[g * SB:(g + 1) * SB] for g in range(hg)], axis=1)


def _swa_steps(r_tot):
    blocks = r_tot // SB
    return next(n for n in (6, 3, 2) if blocks % n == 0 and blocks // n >= 2)


def _swa_specs(nsb, nbq):
    def rows(h, w, cb, f):
        return pl.BlockSpec((h, w), lambda i: (f(i), cb))
    pair = lambda i: i
    prev = lambda i: jnp.maximum(nbq * i - 1, 0)
    meta = lambda i: nsb
    return rows, pair, prev, meta


def _swa_fwd(proj, sinks, t, comm=None):
    nsb = t // SB
    r_tot = t + TM
    nbq = _swa_steps(r_tot)
    qb = nbq * SB
    rows, pair, prev, meta = _swa_specs(nsb, nbq)

    def body(sink_ref, q_ref, kc_ref, kp_ref, km_ref, vc_ref, vp_ref, vm_ref, o_ref, lse_ref):
        i = pl.program_id(0)
        key = lax.broadcasted_iota(jnp.int32, (SB, SB), 0)
        qry = lax.broadcasted_iota(jnp.int32, (SB, SB), 1)
        km, vm = km_ref[0:NM, :], vm_ref[0:NM, :]
        for j in range(nbq):
            b = nbq * i + j
            rs = slice(j * SB, (j + 1) * SB)
            before = slice((j - 1) * SB, j * SB)
            real = b < nsb
            masks = (key <= qry, (key > qry) & (b > 0) & real, real)
            k3 = (kc_ref[rs, :], kp_ref[...] if j == 0 else kc_ref[before, :], km)
            v3 = (vc_ref[rs, :], vp_ref[...] if j == 0 else vc_ref[before, :], vm)
            valid = b * SB + lax.broadcasted_iota(jnp.int32, (1, SB), 1) < t + NM
            heads = range(8)
            kb = [[_bf(k[:, kv * 64:(kv + 1) * 64]) for k in k3] for kv in range(2)]
            vt = [[_bf(v[:, kv * 64:(kv + 1) * 64].T) for v in v3] for kv in range(2)]
            raw = [[_dot_nt(k, _bf(q_ref[rs, h * 64:(h + 1) * 64])) for k in kb[h // 4]] for h in heads]
            probs, inv_l, lse_l = [], [], []
            for h in heads:
                s = [jnp.where(m, sx, NEG) for m, sx in zip(masks, raw[h])]
                sink = sink_ref[0, h]
                top = jnp.maximum(jnp.max(jnp.maximum(s[0], s[1]), axis=0, keepdims=True),
                                  jnp.maximum(jnp.max(s[2], axis=0, keepdims=True), sink))
                p = [jnp.exp(sx - top) for sx in s]
                l = (jnp.sum(p[0] + p[1], axis=0, keepdims=True) + jnp.sum(p[2], axis=0, keepdims=True)
                     + jnp.exp(sink - top))
                probs.append([_bf(px) for px in p])
                inv_l.append(1.0 / l)
                lse_l.append(top + jnp.log(l))
            o_t = [_dot(vt[h // 4][0], probs[h][0]) + _dot(vt[h // 4][1], probs[h][1]) + _dot(vt[h // 4][2], probs[h][2])
                   for h in heads]
            o_ref[rs, :] = _bf(jnp.concatenate([jnp.where(valid, o_t[h] * inv_l[h], 0.0).T for h in heads], axis=1))
            lse_ref[:, rs] = jnp.concatenate(lse_l, axis=0)

    ck, cv = C_SK // 128, C_SV // 128
    return _call(
        body, "swa_fwd", (r_tot // qb,),
        [SMEM_SPEC, rows(qb, 512, C_SQ // 512, pair),
         rows(qb, 128, ck, pair), rows(SB, 128, ck, prev), rows(SB, 128, ck, meta),
         rows(qb, 128, cv, pair), rows(SB, 128, cv, prev), rows(SB, 128, cv, meta)],
        [rows(qb, 512, 0, pair), pl.BlockSpec((8, qb), lambda i: (0, i))],
        [jax.ShapeDtypeStruct((r_tot, 512), BF16), jax.ShapeDtypeStruct((8, r_tot), F32)],
        [], _cp(32), (sinks, proj, proj, proj, proj, proj, proj, proj), comm)


def _swa_bwd(proj, sinks, lse_t, do, t, comm=None):
    nsb = t // SB
    r_tot = t + TM
    nbq = _swa_steps(r_tot)
    qb = nbq * SB
    rows, pair, prev, meta = _swa_specs(nsb, nbq)
    hb = HB_BWD
    lanes = hb * SB

    def body(sink_ref, q_ref, kc_ref, kp_ref, km_ref, vc_ref, vp_ref, vm_ref, lse_ref, do_ref,
             dq_ref, dk_ref, dv_ref, dsink_ref):
        i = pl.program_id(0)

        @pl.when(i == 0)
        def _():
            dk_ref[...] = jnp.zeros_like(dk_ref)
            dv_ref[...] = jnp.zeros_like(dv_ref)
            dsink_ref[...] = jnp.zeros_like(dsink_ref)

        key = lax.broadcasted_iota(jnp.int32, (SB, lanes), 0)
        qry = lax.rem(lax.broadcasted_iota(jnp.int32, (SB, lanes), 1), SB)
        km, vm = km_ref[0:NM, :], vm_ref[0:NM, :]
        dsink_l = []
        for j in range(nbq):
            b = nbq * i + j
            rs = slice(j * SB, (j + 1) * SB)
            before = slice((j - 1) * SB, j * SB)
            real = b < nsb
            masks = (key <= qry, (key > qry) & (b > 0) & real, real)
            k3 = (kc_ref[rs, :], kp_ref[...] if j == 0 else kc_ref[before, :], km)
            v3 = (vc_ref[rs, :], vp_ref[...] if j == 0 else vc_ref[before, :], vm)
            groups = list(range(0, 8, hb))
            kvs = [h0 // 4 for h0 in groups]
            qg = [_bf(_stack(q_ref[rs, h0 * 64:(h0 + hb) * 64], hb)) for h0 in groups]
            dog = [_bf(_stack(do_ref[rs, h0 * 64:(h0 + hb) * 64], hb)) for h0 in groups]
            kb = [[_bf(k[:, kv * 64:(kv + 1) * 64]) for k in k3] for kv in kvs]
            vb = [[_bf(v[:, kv * 64:(kv + 1) * 64]) for v in v3] for kv in kvs]
            s = [[_dot_nt(k, qg[g]) for k in kb[g]] for g in range(len(groups))]
            dp = [[_dot_nt(v, dog[g]) for v in vb[g]] for g in range(len(groups))]
            p, ds, ds_blk = [], [], []
            for g, h0 in enumerate(groups):
                lse_row = jnp.concatenate([lse_ref[h:h + 1, rs] for h in range(h0, h0 + hb)], axis=1)
                sink_row = jnp.concatenate([jnp.full((1, SB), sink_ref[0, h], F32) for h in range(h0, h0 + hb)], axis=1)
                pg = [jnp.exp(jnp.where(m, sx, NEG) - lse_row) for m, sx in zip(masks, s[g])]
                delta = (jnp.sum(pg[0] * dp[g][0] + pg[1] * dp[g][1], axis=0, keepdims=True)
                         + jnp.sum(pg[2] * dp[g][2], axis=0, keepdims=True))
                ds.append([_bf(pp * (dd - delta)) for pp, dd in zip(pg, dp[g])])
                p.append([_bf(pp) for pp in pg])
                ds_row = -jnp.exp(sink_row - lse_row) * delta
                ds_blk += [jnp.sum(ds_row[:, q0 * SB:(q0 + 1) * SB], axis=1, keepdims=True) for q0 in range(hb)]
            dsink_l.append(jnp.concatenate(ds_blk, axis=1))
            dq_t = [_dot_tn(kb[g][0], ds[g][0]) + _dot_tn(kb[g][1], ds[g][1]) + _dot_tn(kb[g][2], ds[g][2])
                    for g in range(len(groups))]
            dq_ref[rs, :] = jnp.concatenate([_unstack(d.T, hb) for d in dq_t], axis=1)
            windows = (pl.ds(pl.multiple_of(b * SB, SB), SB), pl.ds(pl.multiple_of(jnp.maximum(b - 1, 0) * SB, SB), SB),
                       pl.ds(t, NM))
            for x in range(3):
                dk_kv, dv_kv = [], []
                for kv in range(2):
                    mine = [g for g in range(len(groups)) if kvs[g] == kv]
                    dk_kv.append(sum(_dot(ds[g][x], qg[g]) for g in mine))
                    dv_kv.append(sum(_dot(p[g][x], dog[g]) for g in mine))
                dk_ref[windows[x], :] += jnp.concatenate(dk_kv, axis=1)
                dv_ref[windows[x], :] += jnp.concatenate(dv_kv, axis=1)
        dsink_ref[...] += sum(dsink_l)

    ck, cv = C_SK // 128, C_SV // 128
    whole = lambda w: pl.BlockSpec((r_tot, w), lambda i: (0, 0))
    return _call(
        body, "swa_bwd", (r_tot // qb,),
        [SMEM_SPEC, rows(qb, 512, C_SQ // 512, pair),
         rows(qb, 128, ck, pair), rows(SB, 128, ck, prev), rows(SB, 128, ck, meta),
         rows(qb, 128, cv, pair), rows(SB, 128, cv, prev), rows(SB, 128, cv, meta),
         pl.BlockSpec((8, qb), lambda i: (0, i)), rows(qb, 512, 1, pair)],
        [rows(qb, 512, 0, pair), whole(128), whole(128), pl.BlockSpec((1, 8), lambda i: (0, 0))],
        [jax.ShapeDtypeStruct((r_tot, 512), F32), jax.ShapeDtypeStruct((r_tot, 128), F32),
         jax.ShapeDtypeStruct((r_tot, 128), F32), jax.ShapeDtypeStruct((1, 8), F32)],
        [], _cp(48), (sinks, proj, proj, proj, proj, proj, proj, proj, lse_t, do), comm)


HK = D // 2


def _mlp_fwd(x, tgt, ogla, oswa, wo, wff, w1, w2, wfin):
    t = x.shape[0]
    nblk = t // TM

    def body(x_ref, tgt_ref, og_ref, os_ref, wo_ref, wff_ref, w1a_ref, w1b_ref, w2a_ref, w2b_ref, wfin_ref,
             h1_ref, f_ref, a_ref, dh2_ref, loss_ref, gfin_ref):
        i = pl.program_id(0)

        @pl.when(i == 0)
        def _():
            loss_ref[...] = jnp.zeros_like(loss_ref)
            gfin_ref[...] = jnp.zeros_like(gfin_ref)

        @pl.when(i < nblk)
        def _():
            h1 = x_ref[...] + _dot(og_ref[...], wo_ref[0:512, :]) + _dot(os_ref[...], wo_ref[512:1024, :])
            h1_ref[...] = h1
            fh, _ = _rms(h1)
            f = _bf(fh * wff_ref[...])
            f_ref[...] = f
            acc = jnp.zeros((TM, D), F32)
            for n in range(4):
                a = _dot(f[:, 0:HK], w1a_ref[n]) + _dot(f[:, HK:D], w1b_ref[n])
                a_ref[:, n * D:(n + 1) * D] = _bf(a)
                zr = jnp.maximum(a, 0.0)
                z = _bf(zr * zr)
                acc = acc + _dot(z[:, 0:HK], w2a_ref[n]) + _dot(z[:, HK:D], w2b_ref[n])
            h2 = h1 + acc
            yh, rs2 = _rms(h2)
            wf = wfin_ref[...]
            e = yh * wf - tgt_ref[...]
            loss_ref[...] += jnp.sum(jnp.sum(e * e, axis=0, keepdims=True), axis=1, keepdims=True) * (0.5 / D)
            dy = e * (1.0 / D)
            gfin_ref[...] += jnp.sum(dy * yh, axis=0, keepdims=True)
            dh2_ref[...] = _rms_bwd(dy, yh, rs2, wf)

        @pl.when(i == nblk)
        def _():
            h1_ref[...] = jnp.zeros_like(h1_ref)
            f_ref[...] = jnp.zeros_like(f_ref)
            a_ref[...] = jnp.zeros_like(a_ref)
            dh2_ref[...] = jnp.zeros_like(dh2_ref)

    xs = pl.BlockSpec((TM, D), lambda i: (jnp.minimum(i, nblk - 1), 0))
    rs = lambda w: pl.BlockSpec((TM, w), lambda i: (i, 0))
    r_tot = t + TM
    return pl.pallas_call(
        body, name="mlp_fwd", grid=(nblk + 1,),
        in_specs=[xs, xs, rs(512), rs(512)] + [VMEM_SPEC] * 7,
        out_specs=[rs(D), rs(D), rs(DFF), rs(D), pl.BlockSpec((1, 1), lambda i: (0, 0)), pl.BlockSpec((1, D), lambda i: (0, 0))],
        out_shape=[jax.ShapeDtypeStruct((r_tot, D), F32), jax.ShapeDtypeStruct((r_tot, D), BF16),
                   jax.ShapeDtypeStruct((r_tot, DFF), BF16), jax.ShapeDtypeStruct((r_tot, D), F32),
                   jax.ShapeDtypeStruct((1, 1), F32), jax.ShapeDtypeStruct((1, D), F32)],
        compiler_params=_cp(56),
    )(x, tgt, ogla, oswa, wo, wff, *w1, *w2, wfin)


def _mlp_bwd(h1, a, dh2, ogla, oswa, wo, wff, w1, w2):
    r_tot = h1.shape[0]
    nt = r_tot // TM

    def body(h1_ref, a_ref, dh2_ref, og_ref, os_ref, wo_ref, wff_ref, w1a_ref, w1b_ref, w2a_ref, w2b_ref,
             da_ref, dh2b_ref, dh1_ref, do_ref, dwo_ref, gff_ref, dwo_acc):
        i = pl.program_id(0)

        @pl.when(i == 0)
        def _():
            dwo_acc[...] = jnp.zeros_like(dwo_acc)
            gff_ref[...] = jnp.zeros_like(gff_ref)

        @pl.when(i < nt - 1)
        def _():
            dh2 = dh2_ref[...]
            dh2b = _bf(dh2)
            dh2b_ref[...] = dh2b
            dfa = jnp.zeros((TM, HK), F32)
            dfb = jnp.zeros((TM, HK), F32)
            for n in range(4):
                dz = jnp.concatenate([_dot_nt(dh2b, w2a_ref[n]), _dot_nt(dh2b, w2b_ref[n])], axis=1)
                da = _bf(dz * (2.0 * jnp.maximum(a_ref[:, n * D:(n + 1) * D].astype(F32), 0.0)))
                da_ref[:, n * D:(n + 1) * D] = da
                dfa = dfa + _dot_nt(da, w1a_ref[n])
                dfb = dfb + _dot_nt(da, w1b_ref[n])
            df = jnp.concatenate([dfa, dfb], axis=1)
            fh, rs1 = _rms(h1_ref[...])
            gff_ref[...] += jnp.sum(df * fh, axis=0, keepdims=True)
            dh1 = dh2 + _rms_bwd(df, fh, rs1, wff_ref[...])
            dh1_ref[...] = dh1
            dh1b = _bf(dh1)
            do_ref[...] = _dot_nt(dh1b, wo_ref[...])
            dwo_acc[0:512, :] += _dot_tn(og_ref[...], dh1b)
            dwo_acc[512:1024, :] += _dot_tn(os_ref[...], dh1b)

        @pl.when(i == nt - 1)
        def _():
            da_ref[...] = jnp.zeros_like(da_ref)
            dh2b_ref[...] = jnp.zeros_like(dh2b_ref)
            dh1_ref[...] = jnp.zeros_like(dh1_ref)
            do_ref[...] = jnp.zeros_like(do_ref)
            for s in range(4):
                for hh in range(2):
                    dwo_ref[hh, s] = dwo_acc[(2 * s + hh) * 128:(2 * s + hh + 1) * 128, :]

    rs = lambda w: pl.BlockSpec((TM, w), lambda i: (i, 0))
    return pl.pallas_call(
        body, name="mlp_bwd", grid=(nt,),
        in_specs=[rs(D), rs(DFF), rs(D), rs(512), rs(512)] + [VMEM_SPEC] * 6,
        out_specs=[rs(DFF), rs(D), rs(D), rs(D), VMEM_SPEC, pl.BlockSpec((1, D), lambda i: (0, 0))],
        out_shape=[jax.ShapeDtypeStruct((r_tot, DFF), BF16), jax.ShapeDtypeStruct((r_tot, D), BF16),
                   jax.ShapeDtypeStruct((r_tot, D), F32), jax.ShapeDtypeStruct((r_tot, D), F32),
                   jax.ShapeDtypeStruct((2, 4, 128, D), F32), jax.ShapeDtypeStruct((1, D), F32)],
        scratch_shapes=[pltpu.VMEM((D, D), F32)],
        compiler_params=_cp(56),
    )(h1, a, dh2, ogla, oswa, wo, wff, *w1, *w2)


def _ffn_wgrad(f, a, da, dh2b):
    rows = f.shape[0] - TM
    kt = 1024 if rows % 1024 == 0 else TM
    nk = rows // kt

    def body(f_ref, a_ref, da_ref, dh2_ref, dw1_ref, dw2_ref, acc1, acc2):
        k = pl.program_id(1)

        @pl.when(k == 0)
        def _():
            acc1[...] = jnp.zeros_like(acc1)
            acc2[...] = jnp.zeros_like(acc2)

        zr = jnp.maximum(a_ref[...], 0.0)
        acc1[...] += _dot_tn(f_ref[...], da_ref[...])
        acc2[...] += _dot_tn(zr * zr, dh2_ref[...])

        @pl.when(k == nk - 1)
        def _():
            for hh in range(2):
                dw1_ref[hh, 0] = acc1[hh * 512:(hh + 1) * 512, :]
                dw2_ref[hh, 0] = acc2[hh * 512:(hh + 1) * 512, :]

    out = pl.BlockSpec((2, 1, 512, D), lambda n, k: (0, n, 0, 0))
    return pl.pallas_call(
        body, name="ffn_wgrad", grid=(4, nk),
        in_specs=[pl.BlockSpec((kt, D), lambda n, k: (k, 0)), pl.BlockSpec((kt, D), lambda n, k: (k, n)),
                  pl.BlockSpec((kt, D), lambda n, k: (k, n)), pl.BlockSpec((kt, D), lambda n, k: (k, 0))],
        out_specs=[out, out],
        out_shape=[jax.ShapeDtypeStruct((2, 4, 512, D), F32)] * 2,
        scratch_shapes=[pltpu.VMEM((D, D), F32), pltpu.VMEM((D, D), F32)],
        compiler_params=_cp(48, ("arbitrary", "arbitrary")),
    )(f, a, da, dh2b)


def _proj_bwd(x, metapad, wm, wt3, tabs, dgla, dswa_q, dsk, dsv, dlr, dh1, comm=None):
    t = x.shape[0]
    nblk = t // TM

    def body(x_ref, mp_ref, wm_ref, w3_ref, tab_ref, dg_ref, dq_ref, dk_ref, dv_ref, dlr_ref, dh1_ref,
             gx_ref, gmeta_ref, dw_ref, gmix_ref, w_ref, acc):
        i = pl.program_id(0)

        @pl.when(i == 0)
        def _():
            _join_shards(w3_ref, w_ref)
            acc[...] = jnp.zeros_like(acc)
            gmix_ref[...] = jnp.zeros_like(gmix_ref)

        h = jnp.where(i == nblk, mp_ref[...], x_ref[...])
        uh, rs = _rms(h)
        wm_v = wm_ref[...]
        u = _bf(uh * wm_v)
        tab = tab_ref[...]
        dq = _bf(_rope(dq_ref[...] * 0.125, tab, -1.0))
        dk = _bf(_rope(dk_ref[...], tab, -1.0))
        parts = ((dg_ref[...], 0, R_LR), (dlr_ref[:, 0:16], R_LR, 16), (dq, R_LR + 16, 512),
                 (dk, R_LR + 528, 128), (_bf(dv_ref[...]), R_LR + 656, 128))
        du = jnp.zeros((TM, D), F32)
        for val, r0, w in parts:
            du = du + _dot(val, w_ref[r0:r0 + w, :])
            acc[r0:r0 + w, :] += _dot_tn(val, u)
        gmix_ref[...] += jnp.sum(du * uh, axis=0, keepdims=True)
        dh0 = dh1_ref[...] + _rms_bwd(du, uh, rs, wm_v)

        @pl.when(i < nblk)
        def _():
            gx_ref[...] = dh0

        @pl.when(i == nblk)
        def _():
            gmeta_ref[...] = dh0[:NM]
            for s in range(4):
                dw_ref[s] = acc[(DIN // 4) * s:(DIN // 4) * (s + 1), :]

    xs = pl.BlockSpec((TM, D), lambda i: (jnp.minimum(i, nblk - 1), 0))
    rs_ = lambda w: pl.BlockSpec((TM, w), lambda i: (i, 0))
    return _call(
        body, "proj_bwd", (nblk + 1,),
        [xs, VMEM_SPEC, VMEM_SPEC, VMEM_SPEC, rs_(128), rs_(1536), rs_(512), rs_(128), rs_(128), rs_(128), rs_(D)],
        [xs, pl.BlockSpec((NM, D), lambda i: (0, 0)), VMEM_SPEC, pl.BlockSpec((1, D), lambda i: (0, 0))],
        [jax.ShapeDtypeStruct((t, D), F32), jax.ShapeDtypeStruct((NM, D), F32),
         jax.ShapeDtypeStruct((4, DIN // 4, D), F32), jax.ShapeDtypeStruct((1, D), F32)],
        [pltpu.VMEM((DIN, D), BF16), pltpu.VMEM((DIN, D), F32)], _cp(56),
        (x, metapad, wm, wt3, tabs, dgla, dswa_q, dsk, dsv, dlr, dh1), comm)


def _place():
    return lax.axis_index("x"), lax.axis_index("y"), lax.axis_index("c")


def _other_chips(x, y):
    return [(1 - x, y), (x, 1 - y), (1 - x, 1 - y)]


def _dma_sems(*counts):
    return tuple(pltpu.SemaphoreType.DMA((k,)) for k in counts)


def _gather_shards(shards, split):
    n = len(shards)
    two = [a for a in range(n) if split[a]]

    def plan(ins, outs, sems):
        isend, irecv, dsend, drecv, loc = sems
        x, y, c = _place()
        chips = _other_chips(x, y)

        def part(ref, a, half):
            if not split[a]:
                return ref
            w = shards[a].shape[1] // 2
            return ref.at[:, pl.ds(pl.multiple_of(half * w, 128), w)]

        def over_ici(a, k, shard_of):
            tx, ty = chips[k]
            sx, sy = shard_of
            return pltpu.make_async_remote_copy(
                src_ref=part(ins[a], a, c), dst_ref=part(outs[a].at[2 * sx + sy], a, c), send_sem=isend.at[3 * a + k],
                recv_sem=irecv.at[3 * a + k], device_id=(tx, ty, c), device_id_type=MESH)

        def over_d2d(a, k, half):
            tx, ty = chips[k]
            ref = part(outs[a].at[2 * tx + ty], a, half)
            return pltpu.make_async_remote_copy(
                src_ref=ref, dst_ref=ref, send_sem=dsend.at[3 * a + k], recv_sem=drecv.at[3 * a + k],
                device_id=(x, y, 1 - c), device_id_type=MESH)

        def local(a):
            return pltpu.make_async_copy(ins[a], outs[a].at[2 * x + y], loc.at[a])

        pairs = [(a, k) for a in range(n) for k in range(3)]
        first = ([lambda a=a: local(a).start() for a in range(n)]
                 + [lambda a=a, k=k: over_ici(a, k, (x, y)).start() for a, k in pairs],
                 [lambda a=a, k=k: over_ici(a, k, chips[k]).wait_recv() for a, k in pairs]
                 + [lambda a=a, k=k: over_ici(a, k, (x, y)).wait_send() for a, k in pairs]
                 + [lambda a=a: local(a).wait() for a in range(n)])
        pairs2 = [(a, k) for a in two for k in range(3)]
        second = ([lambda a=a, k=k: over_d2d(a, k, c).start() for a, k in pairs2],
                  [lambda a=a, k=k: over_d2d(a, k, 1 - c).wait_recv() for a, k in pairs2]
                  + [lambda a=a, k=k: over_d2d(a, k, c).wait_send() for a, k in pairs2])
        return [first, second] if two else [first]

    return _Comm(tuple(shards), tuple(jax.ShapeDtypeStruct((4,) + s.shape, s.dtype) for s in shards),
                 _dma_sems(3 * n, 3 * n, 3 * n, 3 * n, n), 2 if two else 1, plan)


def _swap_halves(grads):
    n = len(grads)

    def plan(ins, outs, sems):
        send, recv = sems
        x, y, c = _place()

        def swap(a):
            return pltpu.make_async_remote_copy(
                src_ref=ins[a].at[1 - c], dst_ref=outs[a], send_sem=send.at[a], recv_sem=recv.at[a],
                device_id=(x, y, 1 - c), device_id_type=MESH)

        return [([lambda a=a: swap(a).start() for a in range(n)], [lambda a=a: swap(a).wait() for a in range(n)])]

    return _Comm(tuple(grads), tuple(jax.ShapeDtypeStruct(g.shape[1:], g.dtype) for g in grads), _dma_sems(n, n), 1, plan)


SCATTER_ADD_ROWS = 128


def _scatter_shards(parts, late):
    n = len(parts)

    def plan(ins, outs, sems):
        send, recv, loc = sems[:3]
        onward, got = sems[3:3 + n], sems[3 + n:]
        x, y, c = _place()
        x_first = c == 0
        near = (jnp.where(x_first, 1 - x, x), jnp.where(x_first, y, 1 - y))
        far = (jnp.where(x_first, x, 1 - x), jnp.where(x_first, 1 - y, y))
        diagonal = (1 - x, 1 - y)
        shard = lambda chip: 2 * chip[0] + chip[1]

        def hop(src, dst, k, chip):
            return pltpu.make_async_remote_copy(src_ref=src, dst_ref=dst, send_sem=send.at[k], recv_sem=recv.at[k],
                                                device_id=(chip[0], chip[1], c), device_id_type=MESH)

        theirs = lambda a: hop(ins[a].at[shard(near)], outs[a].at[0], 3 * a, near)
        passing = lambda a: hop(ins[a].at[shard(diagonal)], got[a], 3 * a + 1, near)
        summed = lambda a: hop(onward[a], outs[a].at[1], 3 * a + 2, far)
        mine = lambda a: pltpu.make_async_copy(ins[a].at[shard(far)], onward[a], loc.at[a])

        def add(a):
            for r in range(0, parts[a].shape[1], SCATTER_ADD_ROWS):
                rows = slice(r, r + SCATTER_ADD_ROWS)
                onward[a][rows, :] = _bf(onward[a][rows, :].astype(F32) + got[a][rows, :].astype(F32))

        every = range(n)
        first = ([lambda a=a: mine(a).start() for a in every] + [lambda a=a: passing(a).start() for a in every]
                 + [lambda a=a: theirs(a).start() for a in every],
                 [lambda a=a: mine(a).wait() for a in every] + [lambda a=a: passing(a).wait_recv() for a in every]
                 + [lambda a=a: add(a) for a in every])
        second = ([lambda a=a: summed(a).start() for a in every],
                  [lambda a=a: passing(a).wait_send() for a in every] + [lambda a=a: theirs(a).wait() for a in every]
                  + [lambda a=a: summed(a).wait() for a in every])
        return [first, second]

    assert all(p.shape[1] % SCATTER_ADD_ROWS == 0 for p in parts)
    buffers = [pltpu.VMEM(p.shape[1:], p.dtype) for p in parts]
    return _Comm(tuple(parts), tuple(jax.ShapeDtypeStruct((2,) + p.shape[1:], p.dtype) for p in parts),
                 _dma_sems(3 * n, 3 * n, n) + tuple(buffers) * 2, 2, plan, late)


def _join_halves(halves):
    n = len(halves)

    def plan(ins, outs, sems):
        send, recv, loc = sems
        x, y, c = _place()

        def remote(a, half):
            return pltpu.make_async_remote_copy(
                src_ref=ins[a], dst_ref=outs[a].at[half], send_sem=send.at[a], recv_sem=recv.at[a],
                device_id=(x, y, 1 - c), device_id_type=MESH)

        def local(a):
            return pltpu.make_async_copy(ins[a], outs[a].at[c], loc.at[a])

        every = range(n)
        return [([lambda a=a: local(a).start() for a in every] + [lambda a=a: remote(a, c).start() for a in every],
                 [lambda a=a: remote(a, 1 - c).wait_recv() for a in every]
                 + [lambda a=a: remote(a, c).wait_send() for a in every] + [lambda a=a: local(a).wait() for a in every])]

    return _Comm(tuple(halves), tuple(jax.ShapeDtypeStruct((2,) + h.shape, h.dtype) for h in halves),
                 _dma_sems(n, n, n), 1, plan)


def _reduce_w_in(dwt, comm):
    rows, hw = DIN // 4, D // 2
    ci, co = len(comm.ins), len(comm.outs)

    def body(*refs):
        dw_ref, c_in, out_ref, c_out = refs[0], refs[1:1 + ci], refs[1 + ci], refs[2 + ci:2 + ci + co]
        mine, sib, tosend, rbuf, qbuf, full, send, recv, loc = refs[2 + ci + co:11 + ci + co]
        c_sem = refs[11 + ci + co:]
        x, y, c = _place()
        sibling = (x, y, 1 - c)
        (starts, waits), = comm.plan(c_in, c_out, c_sem)
        _run_phase(starts)

        def cols(ref, half):
            window = pl.ds(pl.multiple_of(half * hw, 128), hw)
            return ref.at[:, :, window] if len(ref.shape) == 3 else ref.at[:, window]

        load = pltpu.make_async_copy(cols(dw_ref, c), mine, loc.at[0])
        give = pltpu.make_async_remote_copy(src_ref=cols(dw_ref, 1 - c), dst_ref=sib, send_sem=send.at[3], recv_sem=recv.at[3],
                                            device_id=sibling, device_id_type=MESH)
        load.start()
        give.start()
        load.wait()
        give.wait()
        mine[...] = mine[...] + sib[...]
        cps = []
        for k, (tx, ty) in enumerate(_other_chips(x, y)):
            tosend[k] = _bf(mine[2 * tx + ty])
            cps.append(pltpu.make_async_remote_copy(
                src_ref=tosend.at[k], dst_ref=rbuf.at[k], send_sem=send.at[k], recv_sem=recv.at[k],
                device_id=(tx, ty, c), device_id_type=MESH))
            cps[-1].start()
        for cp in cps:
            cp.wait()
        qbuf[...] = mine[2 * x + y] + rbuf[0].astype(F32) + rbuf[1].astype(F32) + rbuf[2].astype(F32)
        keep = pltpu.make_async_copy(qbuf, cols(full, c), loc.at[1])
        pass_on = pltpu.make_async_remote_copy(src_ref=qbuf, dst_ref=cols(full, c), send_sem=send.at[4], recv_sem=recv.at[4],
                                               device_id=sibling, device_id_type=MESH)
        keep.start()
        pass_on.start()
        keep.wait()
        pass_on.wait_send()
        pltpu.make_async_remote_copy(src_ref=qbuf, dst_ref=cols(full, 1 - c), send_sem=send.at[4], recv_sem=recv.at[4],
                                     device_id=sibling, device_id_type=MESH).wait_recv()
        out_ref[...] = full[...]
        _run_phase(waits)

    outs = pl.pallas_call(
        body, name="reduce_w_in",
        in_specs=[ANY_SPEC] * (1 + ci), out_specs=[VMEM_SPEC] + [ANY_SPEC] * co,
        out_shape=[jax.ShapeDtypeStruct((rows, D), F32)] + list(comm.outs),
        scratch_shapes=[pltpu.VMEM((4, rows, hw), F32), pltpu.VMEM((4, rows, hw), F32), pltpu.VMEM((3, rows, hw), BF16),
                        pltpu.VMEM((3, rows, hw), BF16), pltpu.VMEM((rows, hw), F32), pltpu.VMEM((rows, D), F32),
                        *_dma_sems(5, 5, 2), *comm.sems],
        compiler_params=pltpu.CompilerParams(vmem_limit_bytes=48 << 20),
    )(dwt, *comm.ins)
    return outs[0], outs[1:]


def _allreduce_small(pack):
    p = pack.shape[0]

    def body(in_ref, out_ref, buf, send, recv):
        x, y, c = _place()
        me = 4 * x + 2 * y + c
        buf[me] = in_ref[...]

        def peer_of(k):
            return x ^ (k >> 2), y ^ ((k >> 1) & 1), c ^ (k & 1)

        sends = [pltpu.make_async_remote_copy(
            src_ref=in_ref, dst_ref=buf.at[me], send_sem=send.at[k - 1], recv_sem=recv.at[k - 1],
            device_id=peer_of(k), device_id_type=MESH) for k in range(1, 8)]
        for cp in sends:
            cp.start()
        for k in range(1, 8):
            px, py, pc = peer_of(k)
            pltpu.make_async_remote_copy(
                src_ref=in_ref, dst_ref=buf.at[4 * px + 2 * py + pc], send_sem=send.at[k - 1], recv_sem=recv.at[k - 1],
                device_id=(x, y, c), device_id_type=MESH).wait_recv()
        for cp in sends:
            cp.wait_send()
        acc = buf[0]
        for d in range(1, 8):
            acc = acc + buf[d]
        out_ref[...] = acc

    return pl.pallas_call(
        body, name="allreduce_small",
        in_specs=[VMEM_SPEC], out_specs=VMEM_SPEC, out_shape=jax.ShapeDtypeStruct(pack.shape, F32),
        scratch_shapes=[pltpu.VMEM((8, p, D), F32), *_dma_sems(7, 7)],
    )(pack)


GRID4 = 4


def _sum_cores(core_shard, mine, theirs):
    n = len(mine)

    def body(cs_ref, *refs):
        ms, ts, bfs, owns = refs[:n], refs[n:2 * n], refs[2 * n:3 * n], refs[3 * n:]
        keep = pl.program_id(0) == cs_ref[1]
        for a in range(n):
            acc = ms[a][0, 0] + ts[a][0]
            bfs[a][0] = _bf(acc)

            @pl.when(keep)
            def _():
                owns[a][...] = acc

    shapes = [m.shape[2:] for m in mine]
    in_specs = ([pl.BlockSpec((1, 1) + s, lambda i, cs: (cs[0], i, 0, 0)) for s in shapes]
                + [pl.BlockSpec((1,) + s, lambda i, cs: (i, 0, 0)) for s in shapes])
    out_specs = ([pl.BlockSpec((1,) + s, lambda i, cs: (i, 0, 0)) for s in shapes]
                 + [pl.BlockSpec(s, lambda i, cs: (0, 0)) for s in shapes])
    outs = pl.pallas_call(
        body, name="sum_cores",
        grid_spec=pltpu.PrefetchScalarGridSpec(num_scalar_prefetch=1, grid=(4,), in_specs=in_specs, out_specs=out_specs),
        out_shape=[jax.ShapeDtypeStruct((4,) + s, BF16) for s in shapes] + [jax.ShapeDtypeStruct(s, F32) for s in shapes],
        compiler_params=_cp(48),
    )(core_shard, *mine, *theirs)
    return outs[:n], outs[n:]


def _sum_chips(own, arrived):
    n = len(own)

    def body(*refs):
        os_, ars, outs = refs[:n], refs[n:2 * n], refs[2 * n:]
        for a in range(n):
            outs[a][...] = os_[a][...] + ars[a][0].astype(F32) + ars[a][1].astype(F32)

    blocks = [(o.shape[0] // GRID4, o.shape[1]) for o in own]
    return pl.pallas_call(
        body, name="sum_chips", grid=(GRID4,),
        in_specs=([pl.BlockSpec(b, lambda i: (i, 0)) for b in blocks]
                  + [pl.BlockSpec((2,) + b, lambda i: (0, i, 0)) for b in blocks]),
        out_specs=[pl.BlockSpec(b, lambda i: (i, 0)) for b in blocks],
        out_shape=[jax.ShapeDtypeStruct(o.shape, F32) for o in own],
        compiler_params=_cp(32),
    )(*own, *arrived)


def _adamw_math(w, g, m, v):
    m2 = ADAM_B1 * m + (1.0 - ADAM_B1) * g
    v2 = ADAM_B2 * v + (1.0 - ADAM_B2) * (g * g)
    m_hat = m2 / (1.0 - ADAM_B1 ** ADAM_STEP)
    v_hat = v2 / (1.0 - ADAM_B2 ** ADAM_STEP)
    return -ADAM_LR * (m_hat / (jnp.sqrt(v_hat) + ADAM_EPS) + ADAM_WD * w), m2, v2


def _adamw_big(ws, gs, ms, vs):
    n = len(ws)

    def body(*refs):
        for a in range(n):
            d, m2, v2 = _adamw_math(refs[a][...], refs[n + a][...], refs[2 * n + a][...], refs[3 * n + a][...])
            refs[4 * n + a][...] = d
            refs[5 * n + a][...] = m2
            refs[6 * n + a][...] = v2

    specs = [pl.BlockSpec((w.shape[0] // GRID4, w.shape[1]), lambda i: (i, 0)) for w in ws]
    return pl.pallas_call(
        body, name="adamw_big", grid=(GRID4,),
        in_specs=specs * 4, out_specs=specs * 3,
        out_shape=[jax.ShapeDtypeStruct(w.shape, F32) for w in ws] * 3,
        compiler_params=_cp(48),
    )(*ws, *gs, *ms, *vs)


def _adamw_small(ws, gs, ms, vs):
    n = len(ws)

    def body(*refs):
        for a in range(n):
            d, m2, v2 = _adamw_math(refs[a][...], refs[n + a][...], refs[2 * n + a][...], refs[3 * n + a][...])
            refs[4 * n + a][...] = d
            refs[5 * n + a][...] = m2
            refs[6 * n + a][...] = v2

    return pl.pallas_call(
        body, name="adamw_small",
        in_specs=[VMEM_SPEC] * (4 * n), out_specs=[VMEM_SPEC] * (3 * n),
        out_shape=[jax.ShapeDtypeStruct(w.shape, F32) for w in ws] * 3,
        compiler_params=pltpu.CompilerParams(vmem_limit_bytes=40 << 20),
    )(*ws, *gs, *ms, *vs)


def kernel(x, meta_tokens, norm_mix_w, w_in, w_gate_up, b_gate, gla_norm_w, sinks, w_out, norm_ff_w, w_ff1, w_ff2, final_norm_w, loss_target, m_meta_tokens, m_norm_mix_w, m_w_in, m_w_gate_up, m_b_gate, m_gla_norm_w, m_sinks, m_w_out, m_norm_ff_w, m_w_ff1, m_w_ff2, m_final_norm_w, v_meta_tokens, v_norm_mix_w, v_w_in, v_w_gate_up, v_b_gate, v_gla_norm_w, v_sinks, v_w_out, v_norm_ff_w, v_w_ff1, v_w_ff2, v_final_norm_w):
    xi, yi, ci = _place()
    shard = (2 * xi + yi).astype(jnp.int32).reshape(1)
    core = ci.astype(jnp.int32).reshape(1)

    small = jnp.concatenate([meta_tokens, w_gate_up[0], jnp.zeros((NM, 64), F32)], axis=1)
    wt3, g_small = _run_comm(_gather_shards([_bf(w_in[0].T), small], [True, False]), "gather_w_in")
    meta = g_small[:, :, 0:256].transpose(1, 0, 2).reshape(NM, D)
    wgu = g_small[:, :, 256:320].transpose(1, 0, 2).reshape(NM, 256)

    xs, tgt = x[0], loss_target[0]
    t = xs.shape[0]
    wfin = final_norm_w.reshape(1, D)
    metapad = jnp.concatenate([meta, jnp.zeros((TM - NM, D), F32)], axis=0)
    wgu_p = _bf(jnp.concatenate([wgu, jnp.zeros((128 - 16, 256), F32)], axis=0))
    tabs = _rope_tables(t)

    w1s, w2s = _bf(w_ff1[0]), _bf(w_ff2[0])
    proj, (g_out, w1a) = _proj_fwd(xs, metapad, norm_mix_w, wt3, tabs,
                                   _gather_shards([_bf(w_out[0]), w1s[:HK]], [True] * 2))
    (ogla, oraw, sst, bcum, dgate), (w1b, w2a) = _gla_fwd(proj, wgu_p, b_gate, gla_norm_w, t,
                                                          _gather_shards([w1s[HK:], w2s[:HK]], [True] * 2))
    (oswa, lse), (w2b,) = _swa_fwd(proj, sinks, t, _gather_shards([w2s[HK:]], [True]))
    wo, w1, w2 = g_out.reshape(D, D), (w1a, w1b), (w2a, w2b)
    h1, f, a, dh2, loss, gfin = _mlp_fwd(xs, tgt, ogla, oswa, wo, norm_ff_w, w1, w2, wfin)

    da, dh2b, dh1, do, dwo, gff = _mlp_bwd(h1, a, dh2, ogla, oswa, wo, norm_ff_w, w1, w2)
    dw1, dw2 = _ffn_wgrad(f, a, da, dh2b)
    big = [dwo, dw1, dw2]
    (dgla, dlr, dwgu, dbg, dgnw), theirs = _gla_bwd(proj, oraw, sst, bcum, dgate, do, wgu_p, gla_norm_w, t,
                                                    _swap_halves(big))
    sums_bf, own = _sum_cores(jnp.concatenate([core, shard]), big, theirs)
    swa_grid = (t + TM) // SB // _swa_steps(t + TM)
    (dsq, dsk, dsv, dsink), arrived = _swa_bwd(proj, sinks, lse, do, t, _scatter_shards(sums_bf, swa_grid // 3))
    halves = _sum_chips(own, arrived)
    (gx, gmeta, dwt, gmix), _ = _proj_bwd(xs, metapad, norm_mix_w, wt3, tabs, dgla, dsq, dsk, dsv, dlr, dh1)

    gwt_in, joined = _reduce_w_in(dwt, _join_halves(halves))
    gw_out, gw_1, gw_2 = [j.reshape((-1, j.shape[2])) for j in joined]

    tail = jnp.concatenate([dbg, dgnw, dsink, loss, jnp.zeros((1, D - 256 - 128 - 8 - 1), F32)], axis=1)
    pack = jnp.concatenate([gmeta, gmix, gff, gfin, tail, dwgu[:16].reshape(4, D)], axis=0)
    tot = _allreduce_small(pack)
    g_meta = lax.dynamic_slice_in_dim(tot[0:NM], shard[0] * 256, 256, axis=1)
    g_mix, g_ff, g_fin = tot[16:17], tot[17:18], tot[18]
    g_bg, g_gnw, g_sinks, loss_tot = tot[19:20, 0:256], tot[19:20, 256:384], tot[19:20, 384:392], tot[19, 392]
    g_wgu = lax.dynamic_slice_in_dim(tot[20:24].reshape(NM, 256), shard[0] * 64, 64, axis=1)

    bo = _adamw_big([w_out[0], w_ff1[0], w_ff2[0]], [gw_out, gw_1, gw_2], [m_w_out[0], m_w_ff1[0], m_w_ff2[0]],
                    [v_w_out[0], v_w_ff1[0], v_w_ff2[0]])

    fin2 = lambda a: a.reshape(1, D)
    sw = [meta_tokens, norm_mix_w, w_gate_up[0], b_gate, gla_norm_w, sinks, norm_ff_w, fin2(final_norm_w), w_in[0].T]
    sg = [g_meta, g_mix, g_wgu, g_bg, g_gnw, g_sinks, g_ff, fin2(g_fin), gwt_in]
    sm = [m_meta_tokens, m_norm_mix_w, m_w_gate_up[0], m_b_gate, m_gla_norm_w, m_sinks, m_norm_ff_w, fin2(m_final_norm_w),
          m_w_in[0].T]
    sv = [v_meta_tokens, v_norm_mix_w, v_w_gate_up[0], v_b_gate, v_gla_norm_w, v_sinks, v_norm_ff_w, fin2(v_final_norm_w),
          v_w_in[0].T]
    so = _adamw_small(sw, sg, sm, sv)

    def ordered(small_o, big_o):
        meta_, mix_, wgu_, bg_, gnw_, sinks_, ff_, fin_, wt_ = small_o
        w_out_, w_1_, w_2_ = big_o
        return (meta_, mix_, wt_.T[None], wgu_[None], bg_, gnw_, sinks_, w_out_[None], ff_, w_1_[None], w_2_[None],
                fin_.reshape(D))

    grads = ordered(sg, [gw_out, gw_1, gw_2])
    deltas = ordered(so[0:9], bo[0:3])
    new_m = ordered(so[9:18], bo[3:6])
    new_v = ordered(so[18:27], bo[6:9])
    return (loss_tot, gx[None], *grads, *deltas, *new_m, *new_v)
```

```python
import functools
from typing import Callable, NamedTuple

import jax
import jax.numpy as jnp
import numpy as np
from jax import lax
from jax.experimental import pallas as pl
from jax.experimental.pallas import tpu as pltpu

F32 = jnp.float32
BF16 = jnp.bfloat16

D = 1024
DFF = 4096
NM = 16
TM = 256
DK = 64
CH = 128
SB = 128
EPS = 1e-5
C_GQ, C_GK, C_GV, C_GR, C_SQ, C_SK, C_SV, C_LR, DINP = 0, 256, 512, 1024, 1536, 2048, 2176, 2304, 2432
DIN = 2320
R_LR = 1536
ROPE_THETA = 500000.0
ADAM_LR, ADAM_B1, ADAM_B2, ADAM_EPS, ADAM_WD, ADAM_STEP = 0.001, 0.9, 0.999, 1e-08, 0.01, 10
NEG = -1e30
MESH = pl.DeviceIdType.MESH
VMEM_SPEC = pl.BlockSpec(memory_space=pltpu.VMEM)
ANY_SPEC = pl.BlockSpec(memory_space=pl.ANY)
SMEM_SPEC = pl.BlockSpec(memory_space=pltpu.SMEM)


def _cp(vmem_mb, sem=("arbitrary",)):
    return pltpu.CompilerParams(dimension_semantics=sem, vmem_limit_bytes=vmem_mb << 20)


def _dot(a, b):
    return jnp.dot(a, b, preferred_element_type=F32)


def _dot_nt(a, b):
    return lax.dot_general(a, b, (((1,), (1,)), ((), ())), preferred_element_type=F32)


def _dot_tn(a, b):
    return lax.dot_general(a, b, (((0,), (0,)), ((), ())), preferred_element_type=F32)


def _bf(x):
    return x.astype(BF16)


def _dot3(m01, x):
    x1 = _bf(x)
    r1 = x - x1.astype(F32)
    x2 = _bf(r1)
    x3 = _bf(r1 - x2.astype(F32))
    return _dot(m01, x1) + _dot(m01, x2) + _dot(m01, x3)


def _rms(h):
    rs = lax.rsqrt(jnp.mean(h * h, axis=-1, keepdims=True) + EPS)
    return h * rs, rs


def _rms_bwd(dy, yhat, rs, w):
    dyh = dy * w
    return rs * (dyh - yhat * jnp.mean(dyh * yhat, axis=-1, keepdims=True))


class _Comm(NamedTuple):
    ins: tuple
    outs: tuple
    sems: tuple
    phases: int
    plan: Callable
    late: int = 0


def _run_phase(fns):
    for fn in fns:
        fn()


def _call(body, name, grid, in_specs, out_specs, out_shape, scratch, params, args, comm=None):
    if comm is None:
        outs = pl.pallas_call(body, name=name, grid=grid, in_specs=in_specs, out_specs=out_specs, out_shape=out_shape,
                              scratch_shapes=scratch, compiler_params=params)(*args)
        return outs, None
    n_in, n_out, n_scr = len(in_specs), len(out_specs), len(scratch)
    ci, co = len(comm.ins), len(comm.outs)
    last = grid[0] - 1
    marks = [0, max(1, last - (comm.late or max(2, (last + 1) // 6)))][:comm.phases]

    def wrapped(*refs):
        own_in, c_in = refs[:n_in], refs[n_in:n_in + ci]
        refs = refs[n_in + ci:]
        own_out, c_out = refs[:n_out], refs[n_out:n_out + co]
        refs = refs[n_out + co:]
        own_scr, c_sem = refs[:n_scr], refs[n_scr:]
        i = pl.program_id(0)

        for p, mark in enumerate(marks):
            @pl.when(i == mark)
            def _():
                plan = comm.plan(c_in, c_out, c_sem)
                if p > 0:
                    _run_phase(plan[p - 1][1])
                _run_phase(plan[p][0])

        body(*own_in, *own_out, *own_scr)

        @pl.when(i == last)
        def _():
            _run_phase(comm.plan(c_in, c_out, c_sem)[-1][1])

    outs = pl.pallas_call(
        wrapped, name=name, grid=grid, in_specs=list(in_specs) + [ANY_SPEC] * ci, out_specs=list(out_specs) + [ANY_SPEC] * co,
        out_shape=list(out_shape) + list(comm.outs), scratch_shapes=list(scratch) + list(comm.sems), compiler_params=params,
    )(*args, *comm.ins)
    return outs[:n_out], outs[n_out:]


def _run_comm(comm, name):
    ci, co = len(comm.ins), len(comm.outs)

    def body(*refs):
        for starts, waits in comm.plan(refs[:ci], refs[ci:ci + co], refs[ci + co:]):
            _run_phase(starts)
            _run_phase(waits)

    return pl.pallas_call(body, name=name, in_specs=[ANY_SPEC] * ci, out_specs=[ANY_SPEC] * co, out_shape=list(comm.outs),
                          scratch_shapes=list(comm.sems))(*comm.ins)


def _join_shards(w3_ref, w_ref):
    for s in range(4):
        w_ref[(DIN // 4) * s:(DIN // 4) * (s + 1), :] = w3_ref[s]


def _proj_fwd(x, metapad, wm, wt3, tabs, comm=None):
    t = x.shape[0]
    nblk = t // TM

    def body(x_ref, mp_ref, wm_ref, w3_ref, tab_ref, proj_ref, w_ref):
        i = pl.program_id(0)

        @pl.when(i == 0)
        def _():
            _join_shards(w3_ref, w_ref)

        h = jnp.where(i == nblk, mp_ref[...], x_ref[...])
        u, _ = _rms(h)
        ub = _bf(u * wm_ref[...])
        proj_ref[:, 0:C_SQ] = _dot_nt(ub, w_ref[0:R_LR, :])
        att = _dot_nt(ub, w_ref[R_LR + 16:DIN, :])
        tab = tab_ref[...]
        proj_ref[:, C_SQ:C_SK] = _rope(att[:, 0:512], tab, 1.0) * 0.125
        proj_ref[:, C_SK:C_SV] = _rope(att[:, 512:640], tab, 1.0)
        proj_ref[:, C_SV:C_LR] = att[:, 640:768]
        proj_ref[:, C_LR:DINP] = jnp.zeros((TM, DINP - C_LR), F32)
        proj_ref[:, C_LR:C_LR + 16] = _dot_nt(ub, w_ref[R_LR:R_LR + 16, :])

    (proj,), got = _call(
        body, "proj_fwd", (nblk + 1,),
        [pl.BlockSpec((TM, D), lambda i: (jnp.minimum(i, nblk - 1), 0)), VMEM_SPEC, VMEM_SPEC, VMEM_SPEC,
         pl.BlockSpec((TM, 128), lambda i: (i, 0))],
        [pl.BlockSpec((TM, DINP), lambda i: (i, 0))], [jax.ShapeDtypeStruct((t + TM, DINP), F32)],
        [pltpu.VMEM((DIN, D), BF16)], _cp(48), (x, metapad, wm, wt3, tabs), comm)
    return proj, got


def _chunk_masks():
    r = lax.broadcasted_iota(jnp.int32, (TM, TM), 0)
    c = lax.broadcasted_iota(jnp.int32, (TM, TM), 1)
    same = (r // CH) == (c // CH)
    lower = _bf(jnp.where(same & (c <= r), 1.0, 0.0))
    upper = _bf(jnp.where(same & (c >= r), 1.0, 0.0))
    return lower, upper


def _gla_gate(lr, wgu, bg, valid, lower):
    z = _dot(_bf(lr), wgu) + bg
    g = (jnp.minimum(z, 0.0) - jnp.log(1.0 + jnp.exp(-jnp.abs(z)))) * (1.0 / 16.0)
    g = jnp.where(valid, g, 0.0)
    return z, _dot3(lower, g)


def _gla_decays(q, k, b):
    nc = TM // CH
    b3 = b.reshape(nc, CH, 256)
    blast = b3[:, CH - 1:CH, :]
    eb = jnp.exp(b)
    enb = jnp.exp(-b)
    ebl = jnp.exp(blast - b3).reshape(TM, 256)
    return eb, enb, ebl, jnp.exp(blast)


def _tri(lower_incl):
    r = lax.broadcasted_iota(jnp.int32, (CH, CH), 0)
    c = lax.broadcasted_iota(jnp.int32, (CH, CH), 1)
    return ((c <= r) if lower_incl else (c >= r))[None]


def _gla_fwd(proj, wgu, bg, gnw, t, comm=None):
    nblk = t // TM
    nt = nblk + 1
    nc = TM // CH

    def blk(i):
        return (i + nblk) % nt

    def body(q_ref, k_ref, v_ref, r_ref, lr_ref, wgu_ref, bg_ref, gnw_ref, o_ref, oraw_ref, sst_ref, b_ref, dgate_ref,
             st_scr):
        i = pl.program_id(0)

        @pl.when(i == 0)
        def _():
            st_scr[...] = jnp.zeros_like(st_scr)

        rows = blk(i) * TM + lax.broadcasted_iota(jnp.int32, (TM, 1), 0)
        lower, _ = _chunk_masks()
        valid = rows < t + NM
        z, b = _gla_gate(lr_ref[...], wgu_ref[...], bg_ref[...], valid, lower)
        b_ref[...] = b
        dgate_ref[...] = jnp.where(valid, (1.0 / 16.0) / (1.0 + jnp.exp(z)), 0.0)
        q = q_ref[...]
        k = k_ref[...]
        eb, enb, ebl, eblast = _gla_decays(q, k, b)
        qt = q * 0.125 * eb
        kt = k * enb
        kh = k * ebl
        tril = _tri(True)
        heads = range(4)
        hs = [slice(h * DK, (h + 1) * DK) for h in heads]
        qh = [_bf(qt[:, hs[h]]).reshape(nc, CH, DK) for h in heads]
        kth = [_bf(kt[:, hs[h]]).reshape(nc, CH, DK) for h in heads]
        khh = [_bf(kh[:, hs[h]]).reshape(nc, CH, DK) for h in heads]
        vh = [_bf(v_ref[:, h * 128:(h + 1) * 128]).reshape(nc, CH, 128) for h in heads]
        a = [jnp.einsum('cid,cjd->cij', qh[h], kth[h], preferred_element_type=F32) for h in heads]
        kv = [jnp.einsum('cjv,cjd->cvd', vh[h], khh[h], preferred_element_type=F32) for h in heads]
        o = [jnp.einsum('cij,cjv->civ', _bf(jnp.where(tril, a[h], 0.0)), vh[h], preferred_element_type=F32) for h in heads]
        states = []
        for h in heads:
            st = st_scr[h]
            per_chunk = []
            for c in range(nc):
                sst_ref[c, h] = st
                per_chunk.append(_bf(st))
                st = st * eblast[c, :, hs[h]] + kv[h][c]
            st_scr[h] = st
            states.append(per_chunk)
        o_inter = [[_dot_nt(qh[h][c], states[h][c]) for c in range(nc)] for h in heads]
        oraw = jnp.concatenate([(o[h] + jnp.stack(o_inter[h])).reshape(TM, 128) for h in heads], axis=1)
        oraw_ref[...] = oraw
        gn = gnw_ref[...]
        res = []
        for h in range(4):
            on, _ = _rms(oraw[:, h * 128:(h + 1) * 128])
            r = r_ref[:, h * 128:(h + 1) * 128]
            res.append(on * gn * (r * jax.nn.sigmoid(r)))
        o_ref[...] = _bf(jnp.concatenate(res, axis=1))

    def spec(w, cb):
        return pl.BlockSpec((TM, w), lambda i: (blk(i), cb))

    return _call(
        body, "gla_fwd", (nt,),
        [spec(256, 0), spec(256, 1), spec(512, 1), spec(512, 2), spec(128, C_LR // 128), VMEM_SPEC, VMEM_SPEC, VMEM_SPEC],
        [spec(512, 0), spec(512, 0), pl.BlockSpec((nc, 4, 128, DK), lambda i: (blk(i), 0, 0, 0)), spec(256, 0), spec(256, 0)],
        [jax.ShapeDtypeStruct((t + TM, 512), BF16), jax.ShapeDtypeStruct((t + TM, 512), F32),
         jax.ShapeDtypeStruct((nt * nc, 4, 128, DK), F32), jax.ShapeDtypeStruct((t + TM, 256), F32),
         jax.ShapeDtypeStruct((t + TM, 256), F32)],
        [pltpu.VMEM((4, 128, DK), F32)], _cp(40), (proj, proj, proj, proj, proj, wgu, bg, gnw), comm)


def _gla_bwd(proj, oraw, sst, bcum, dgate, do, wgu, gnw, t, comm=None):
    nblk = t // TM
    nt = nblk + 1
    nc = TM // CH

    def blk(i):
        return (2 * nblk - i) % nt

    def body(q_ref, k_ref, v_ref, r_ref, lr_ref, oraw_ref, sst_ref, b_ref, dgate_ref, do_ref, wgu_ref, gnw_ref,
             dgla_ref, dlr_ref, dwgu_ref, dbg_ref, dgnw_ref, dst_scr):
        i = pl.program_id(0)

        @pl.when(i == 0)
        def _():
            dst_scr[...] = jnp.zeros_like(dst_scr)
            dwgu_ref[...] = jnp.zeros_like(dwgu_ref)
            dbg_ref[...] = jnp.zeros_like(dbg_ref)
            dgnw_ref[...] = jnp.zeros_like(dgnw_ref)

        _, upper = _chunk_masks()
        lr = lr_ref[...]
        b = b_ref[...]
        q = q_ref[...]
        k = k_ref[...]
        eb, enb, ebl, eblast = _gla_decays(q, k, b)
        qt = q * 0.125 * eb
        kt = k * enb
        kh = k * ebl
        gn = gnw_ref[...]
        tril = _tri(True)
        triu = _tri(False)
        heads = range(4)
        hs = [slice(h * DK, (h + 1) * DK) for h in heads]
        vs = [slice(h * 128, (h + 1) * 128) for h in heads]
        ein = functools.partial(jnp.einsum, preferred_element_type=F32)
        dr_l, doh = [], []
        dgn = jnp.zeros((1, 128), F32)
        for h in heads:
            on, rs = _rms(oraw_ref[:, vs[h]])
            r = r_ref[:, vs[h]]
            sig = jax.nn.sigmoid(r)
            sil = r * sig
            dy = do_ref[:, vs[h]]
            dr_l.append(dy * on * gn * (sig * (1.0 + r * (1.0 - sig))))
            dgn = dgn + jnp.sum(dy * sil * on, axis=0, keepdims=True)
            doh.append(_bf(_rms_bwd(dy * sil, on, rs, gn)).reshape(nc, CH, 128))
        dgnw_ref[...] += dgn
        qh = [_bf(qt[:, hs[h]]).reshape(nc, CH, DK) for h in heads]
        kth = [_bf(kt[:, hs[h]]).reshape(nc, CH, DK) for h in heads]
        khh = [_bf(kh[:, hs[h]]).reshape(nc, CH, DK) for h in heads]
        vh = [_bf(v_ref[:, vs[h]]).reshape(nc, CH, 128) for h in heads]
        at = [ein('cjd,cid->cji', kth[h], qh[h]) for h in heads]
        da = [ein('civ,cjv->cij', doh[h], vh[h]) for h in heads]
        dat = [ein('cjv,civ->cji', vh[h], doh[h]) for h in heads]
        gq = [ein('civ,cid->cvd', doh[h], qh[h]) for h in heads]
        stf = [sst_ref[:, h] for h in heads]
        dqs = [ein('civ,cvd->cid', doh[h], _bf(stf[h])) for h in heads]
        dv = [ein('cji,civ->cjv', _bf(jnp.where(triu, at[h], 0.0)), doh[h]) for h in heads]
        dqt = [ein('cij,cjd->cid', _bf(jnp.where(tril, da[h], 0.0)), kth[h]) + dqs[h] for h in heads]
        dkt = [ein('cji,cid->cjd', _bf(jnp.where(triu, dat[h], 0.0)), qh[h]) for h in heads]
        dse = []
        for h in heads:
            dst = dst_scr[h]
            dsend = [None] * nc
            for c in reversed(range(nc)):
                dsend[c] = dst
                dst = dst * eblast[c, :, hs[h]] + gq[h][c]
            dst_scr[h] = dst
            dse.append(jnp.stack(dsend))
        dseb = [_bf(d) for d in dse]
        dv = [dv[h] + ein('cjd,cvd->cjv', khh[h], dseb[h]) for h in heads]
        dkh = [ein('cjv,cvd->cjd', vh[h], dseb[h]) for h in heads]
        carried = jnp.concatenate([jnp.sum(dse[h] * stf[h], axis=1, keepdims=True) for h in heads], axis=2)
        wide = lambda parts: jnp.concatenate([p.reshape(TM, DK) for p in parts], axis=1)
        dqt_w, dkt_w, dkh_w = wide(dqt), wide(dkt), wide(dkh)
        dkh_kh = dkh_w * kh
        extra = jnp.sum(dkh_kh.reshape(nc, CH, 256), axis=1, keepdims=True) + eblast * carried
        db = dqt_w * qt - dkt_w * kt - dkh_kh
        dg = _dot3(upper, db) + jnp.broadcast_to(extra, (nc, CH, 256)).reshape(TM, 256)
        dz = dg * dgate_ref[...]
        dzb = _bf(dz)
        dlr_ref[...] = _bf(_dot_nt(dzb, wgu_ref[...]))
        dwgu_ref[...] += _dot_tn(_bf(lr), dzb)
        dbg_ref[...] += jnp.sum(dz, axis=0, keepdims=True)
        dq = dqt_w * eb * 0.125
        dk = dkt_w * enb + dkh_w * ebl
        dgla_ref[...] = _bf(jnp.concatenate([dq, dk] + [d.reshape(TM, 128) for d in dv] + dr_l, axis=1))

    def spec(w, cb):
        return pl.BlockSpec((TM, w), lambda i: (blk(i), cb))

    def acc(shape):
        return pl.BlockSpec(shape, lambda i: (0, 0))

    return _call(
        body, "gla_bwd", (nt,),
        [spec(256, 0), spec(256, 1), spec(512, 1), spec(512, 2), spec(128, C_LR // 128), spec(512, 0),
         pl.BlockSpec((nc, 4, 128, DK), lambda i: (blk(i), 0, 0, 0)), spec(256, 0), spec(256, 0), spec(512, 0),
         VMEM_SPEC, VMEM_SPEC],
        [spec(1536, 0), spec(128, 0), acc((128, 256)), acc((1, 256)), acc((1, 128))],
        [jax.ShapeDtypeStruct((t + TM, 1536), BF16), jax.ShapeDtypeStruct((t + TM, 128), BF16),
         jax.ShapeDtypeStruct((128, 256), F32), jax.ShapeDtypeStruct((1, 256), F32), jax.ShapeDtypeStruct((1, 128), F32)],
        [pltpu.VMEM((4, 128, DK), F32)], _cp(48), (proj, proj, proj, proj, proj, oraw, sst, bcum, dgate, do, wgu, gnw), comm)


def _rope_tables(t):
    r = t + TM
    row = np.arange(r)
    pos = np.where(row < t, row + NM, np.where(row < t + NM, row - t, 0)).astype(np.float32)
    inv_freq = (1.0 / (np.float32(ROPE_THETA) ** (np.arange(0, 16, 2, dtype=np.float32) / np.float32(16)))).astype(np.float32)
    ang = (pos[:, None] * inv_freq[None, :]).astype(np.float32)
    cos, sin = np.cos(ang).astype(np.float32), np.sin(ang).astype(np.float32)
    one, zero = np.ones((r, 48), np.float32), np.zeros((r, 48), np.float32)
    return jnp.asarray(np.concatenate([cos, cos, one, -sin, sin, zero], axis=1))


def _rope(x, tab, sign):
    w = x.shape[1]
    rep = w // 64
    c = jnp.concatenate([tab[:, 0:64]] * rep, axis=1)
    s = jnp.concatenate([tab[:, 64:128]] * rep, axis=1)
    lane = lax.rem(lax.broadcasted_iota(jnp.int32, x.shape, 1), 64)
    partner = jnp.where(lane < 8, pltpu.roll(x, w - 8, 1), jnp.where(lane < 16, pltpu.roll(x, 8, 1), 0.0))
    return x * c + sign * (partner * s)


HB_BWD = 4


def _stack(x, hg):
    w = x.shape[1] // hg
    return x if hg == 1 else jnp.concatenate([x[:, g * w:(g + 1) * w] for g in range(hg)], axis=0)


def _unstack(x, hg):
    return x if hg == 1 else jnp.concatenate([x[g * SB:(g + 1) * SB] for g in range(hg)], axis=1)


def _swa_steps(r_tot):
    blocks = r_tot // SB
    return next(n for n in (6, 3, 2) if blocks % n == 0 and blocks // n >= 2)


def _swa_specs(nsb, nbq):
    def rows(h, w, cb, f):
        return pl.BlockSpec((h, w), lambda i: (f(i), cb))
    pair = lambda i: i
    prev = lambda i: jnp.maximum(nbq * i - 1, 0)
    meta = lambda i: nsb
    return rows, pair, prev, meta


def _swa_fwd(proj, sinks, t, comm=None):
    nsb = t // SB
    r_tot = t + TM
    nbq = _swa_steps(r_tot)
    qb = nbq * SB
    rows, pair, prev, meta = _swa_specs(nsb, nbq)

    def body(sink_ref, q_ref, kc_ref, kp_ref, km_ref, vc_ref, vp_ref, vm_ref, o_ref, lse_ref):
        i = pl.program_id(0)
        key = lax.broadcasted_iota(jnp.int32, (SB, SB), 0)
        qry = lax.broadcasted_iota(jnp.int32, (SB, SB), 1)
        km, vm = km_ref[0:NM, :], vm_ref[0:NM, :]
        for j in range(nbq):
            b = nbq * i + j
            rs = slice(j * SB, (j + 1) * SB)
            before = slice((j - 1) * SB, j * SB)
            real = b < nsb
            masks = (key <= qry, (key > qry) & (b > 0) & real, real)
            k3 = (kc_ref[rs, :], kp_ref[...] if j == 0 else kc_ref[before, :], km)
            v3 = (vc_ref[rs, :], vp_ref[...] if j == 0 else vc_ref[before, :], vm)
            valid = b * SB + lax.broadcasted_iota(jnp.int32, (1, SB), 1) < t + NM
            heads = range(8)
            kb = [[_bf(k[:, kv * 64:(kv + 1) * 64]) for k in k3] for kv in range(2)]
            vt = [[_bf(v[:, kv * 64:(kv + 1) * 64].T) for v in v3] for kv in range(2)]
            raw = [[_dot_nt(k, _bf(q_ref[rs, h * 64:(h + 1) * 64])) for k in kb[h // 4]] for h in heads]
            probs, inv_l, lse_l = [], [], []
            for h in heads:
                s = [jnp.where(m, sx, NEG) for m, sx in zip(masks, raw[h])]
                sink = sink_ref[0, h]
                top = jnp.maximum(jnp.max(jnp.maximum(s[0], s[1]), axis=0, keepdims=True),
                                  jnp.maximum(jnp.max(s[2], axis=0, keepdims=True), sink))
                p = [jnp.exp(sx - top) for sx in s]
                l = (jnp.sum(p[0] + p[1], axis=0, keepdims=True) + jnp.sum(p[2], axis=0, keepdims=True)
                     + jnp.exp(sink - top))
                probs.append([_bf(px) for px in p])
                inv_l.append(1.0 / l)
                lse_l.append(top + jnp.log(l))
            o_t = [_dot(vt[h // 4][0], probs[h][0]) + _dot(vt[h // 4][1], probs[h][1]) + _dot(vt[h // 4][2], probs[h][2])
                   for h in heads]
            o_ref[rs, :] = _bf(jnp.concatenate([jnp.where(valid, o_t[h] * inv_l[h], 0.0).T for h in heads], axis=1))
            lse_ref[:, rs] = jnp.concatenate(lse_l, axis=0)

    ck, cv = C_SK // 128, C_SV // 128
    return _call(
        body, "swa_fwd", (r_tot // qb,),
        [SMEM_SPEC, rows(qb, 512, C_SQ // 512, pair),
         rows(qb, 128, ck, pair), rows(SB, 128, ck, prev), rows(SB, 128, ck, meta),
         rows(qb, 128, cv, pair), rows(SB, 128, cv, prev), rows(SB, 128, cv, meta)],
        [rows(qb, 512, 0, pair), pl.BlockSpec((8, qb), lambda i: (0, i))],
        [jax.ShapeDtypeStruct((r_tot, 512), BF16), jax.ShapeDtypeStruct((8, r_tot), F32)],
        [], _cp(32), (sinks, proj, proj, proj, proj, proj, proj, proj), comm)


def _swa_bwd(proj, sinks, lse_t, do, t, comm=None):
    nsb = t // SB
    r_tot = t + TM
    nbq = _swa_steps(r_tot)
    qb = nbq * SB
    rows, pair, prev, meta = _swa_specs(nsb, nbq)
    hb = HB_BWD
    lanes = hb * SB

    def body(sink_ref, q_ref, kc_ref, kp_ref, km_ref, vc_ref, vp_ref, vm_ref, lse_ref, do_ref,
             dq_ref, dk_ref, dv_ref, dsink_ref):
        i = pl.program_id(0)

        @pl.when(i == 0)
        def _():
            dk_ref[...] = jnp.zeros_like(dk_ref)
            dv_ref[...] = jnp.zeros_like(dv_ref)
            dsink_ref[...] = jnp.zeros_like(dsink_ref)

        key = lax.broadcasted_iota(jnp.int32, (SB, lanes), 0)
        qry = lax.rem(lax.broadcasted_iota(jnp.int32, (SB, lanes), 1), SB)
        km, vm = km_ref[0:NM, :], vm_ref[0:NM, :]
        dsink_l = []
        for j in range(nbq):
            b = nbq * i + j
            rs = slice(j * SB, (j + 1) * SB)
            before = slice((j - 1) * SB, j * SB)
            real = b < nsb
            masks = (key <= qry, (key > qry) & (b > 0) & real, real)
            k3 = (kc_ref[rs, :], kp_ref[...] if j == 0 else kc_ref[before, :], km)
            v3 = (vc_ref[rs, :], vp_ref[...] if j == 0 else vc_ref[before, :], vm)
            groups = list(range(0, 8, hb))
            kvs = [h0 // 4 for h0 in groups]
            qg = [_bf(_stack(q_ref[rs, h0 * 64:(h0 + hb) * 64], hb)) for h0 in groups]
            dog = [_bf(_stack(do_ref[rs, h0 * 64:(h0 + hb) * 64], hb)) for h0 in groups]
            kb = [[_bf(k[:, kv * 64:(kv + 1) * 64]) for k in k3] for kv in kvs]
            vb = [[_bf(v[:, kv * 64:(kv + 1) * 64]) for v in v3] for kv in kvs]
            s = [[_dot_nt(k, qg[g]) for k in kb[g]] for g in range(len(groups))]
            dp = [[_dot_nt(v, dog[g]) for v in vb[g]] for g in range(len(groups))]
            p, ds, ds_blk = [], [], []
            for g, h0 in enumerate(groups):
                lse_row = jnp.concatenate([lse_ref[h:h + 1, rs] for h in range(h0, h0 + hb)], axis=1)
                sink_row = jnp.concatenate([jnp.full((1, SB), sink_ref[0, h], F32) for h in range(h0, h0 + hb)], axis=1)
                pg = [jnp.exp(jnp.where(m, sx, NEG) - lse_row) for m, sx in zip(masks, s[g])]
                delta = (jnp.sum(pg[0] * dp[g][0] + pg[1] * dp[g][1], axis=0, keepdims=True)
                         + jnp.sum(pg[2] * dp[g][2], axis=0, keepdims=True))
                ds.append([_bf(pp * (dd - delta)) for pp, dd in zip(pg, dp[g])])
                p.append([_bf(pp) for pp in pg])
                ds_row = -jnp.exp(sink_row - lse_row) * delta
                ds_blk += [jnp.sum(ds_row[:, q0 * SB:(q0 + 1) * SB], axis=1, keepdims=True) for q0 in range(hb)]
            dsink_l.append(jnp.concatenate(ds_blk, axis=1))
            dq_t = [_dot_tn(kb[g][0], ds[g][0]) + _dot_tn(kb[g][1], ds[g][1]) + _dot_tn(kb[g][2], ds[g][2])
                    for g in range(len(groups))]
            dq_ref[rs, :] = jnp.concatenate([_unstack(d.T, hb) for d in dq_t], axis=1)
            windows = (pl.ds(pl.multiple_of(b * SB, SB), SB), pl.ds(pl.multiple_of(jnp.maximum(b - 1, 0) * SB, SB), SB),
                       pl.ds(t, NM))
            for x in range(3):
                dk_kv, dv_kv = [], []
                for kv in range(2):
                    mine = [g for g in range(len(groups)) if kvs[g] == kv]
                    dk_kv.append(sum(_dot(ds[g][x], qg[g]) for g in mine))
                    dv_kv.append(sum(_dot(p[g][x], dog[g]) for g in mine))
                dk_ref[windows[x], :] += jnp.concatenate(dk_kv, axis=1)
                dv_ref[windows[x], :] += jnp.concatenate(dv_kv, axis=1)
        dsink_ref[...] += sum(dsink_l)

    ck, cv = C_SK // 128, C_SV // 128
    whole = lambda w: pl.BlockSpec((r_tot, w), lambda i: (0, 0))
    return _call(
        body, "swa_bwd", (r_tot // qb,),
        [SMEM_SPEC, rows(qb, 512, C_SQ // 512, pair),
         rows(qb, 128, ck, pair), rows(SB, 128, ck, prev), rows(SB, 128, ck, meta),
         rows(qb, 128, cv, pair), rows(SB, 128, cv, prev), rows(SB, 128, cv, meta),
         pl.BlockSpec((8, qb), lambda i: (0, i)), rows(qb, 512, 1, pair)],
        [rows(qb, 512, 0, pair), whole(128), whole(128), pl.BlockSpec((1, 8), lambda i: (0, 0))],
        [jax.ShapeDtypeStruct((r_tot, 512), F32), jax.ShapeDtypeStruct((r_tot, 128), F32),
         jax.ShapeDtypeStruct((r_tot, 128), F32), jax.ShapeDtypeStruct((1, 8), F32)],
        [], _cp(48), (sinks, proj, proj, proj, proj, proj, proj, proj, lse_t, do), comm)


HK = D // 2
MLP_FWD_ROWS = 512


def _mlp_fwd(x, tgt, ogla, oswa, wo, wff, w1, w2, wfin):
    t = x.shape[0]
    tm = MLP_FWD_ROWS if t % MLP_FWD_ROWS == 0 else TM

    def body(x_ref, tgt_ref, og_ref, os_ref, wo_ref, wff_ref, w1a_ref, w1b_ref, w2a_ref, w2b_ref, wfin_ref,
             h1_ref, f_ref, a_ref, dh2_ref, loss_ref, gfin_ref):
        i = pl.program_id(0)

        @pl.when(i == 0)
        def _():
            loss_ref[...] = jnp.zeros_like(loss_ref)
            gfin_ref[...] = jnp.zeros_like(gfin_ref)

        h1 = x_ref[...] + _dot(og_ref[...], wo_ref[0:512, :]) + _dot(os_ref[...], wo_ref[512:1024, :])
        h1_ref[...] = h1
        fh, _ = _rms(h1)
        f = _bf(fh * wff_ref[...])
        f_ref[...] = f
        acc = jnp.zeros((tm, D), F32)
        for n in range(4):
            a = _dot(f[:, 0:HK], w1a_ref[n]) + _dot(f[:, HK:D], w1b_ref[n])
            a_ref[:, n * D:(n + 1) * D] = _bf(a)
            zr = jnp.maximum(a, 0.0)
            z = _bf(zr * zr)
            acc = acc + _dot(z[:, 0:HK], w2a_ref[n]) + _dot(z[:, HK:D], w2b_ref[n])
        h2 = h1 + acc
        yh, rs2 = _rms(h2)
        wf = wfin_ref[...]
        e = yh * wf - tgt_ref[...]
        loss_ref[...] += jnp.sum(jnp.sum(e * e, axis=0, keepdims=True), axis=1, keepdims=True) * (0.5 / D)
        dy = e * (1.0 / D)
        gfin_ref[...] += jnp.sum(dy * yh, axis=0, keepdims=True)
        dh2_ref[...] = _rms_bwd(dy, yh, rs2, wf)

    rs = lambda w: pl.BlockSpec((tm, w), lambda i: (i, 0))
    return pl.pallas_call(
        body, name="mlp_fwd", grid=(t // tm,),
        in_specs=[rs(D), rs(D), rs(512), rs(512)] + [VMEM_SPEC] * 7,
        out_specs=[rs(D), rs(D), rs(DFF), rs(D), pl.BlockSpec((1, 1), lambda i: (0, 0)), pl.BlockSpec((1, D), lambda i: (0, 0))],
        out_shape=[jax.ShapeDtypeStruct((t, D), F32), jax.ShapeDtypeStruct((t, D), BF16),
                   jax.ShapeDtypeStruct((t, DFF), BF16), jax.ShapeDtypeStruct((t, D), F32),
                   jax.ShapeDtypeStruct((1, 1), F32), jax.ShapeDtypeStruct((1, D), F32)],
        compiler_params=_cp(60),
    )(x, tgt, ogla, oswa, wo, wff, *w1, *w2, wfin)


def _mlp_bwd(h1, a, dh2, ogla, oswa, wo, wff, w1, w2):
    r_tot = ogla.shape[0]
    nt = r_tot // TM

    def body(h1_ref, a_ref, dh2_ref, og_ref, os_ref, wo_ref, wff_ref, w1a_ref, w1b_ref, w2a_ref, w2b_ref,
             da_ref, dh2b_ref, dh1_ref, do_ref, dwo_ref, gff_ref, dwo_acc):
        i = pl.program_id(0)

        @pl.when(i == 0)
        def _():
            dwo_acc[...] = jnp.zeros_like(dwo_acc)
            gff_ref[...] = jnp.zeros_like(gff_ref)

        @pl.when(i < nt - 1)
        def _():
            dh2 = dh2_ref[...]
            dh2b = _bf(dh2)
            dh2b_ref[...] = dh2b
            dfa = jnp.zeros((TM, HK), F32)
            dfb = jnp.zeros((TM, HK), F32)
            for n in range(4):
                dz = jnp.concatenate([_dot_nt(dh2b, w2a_ref[n]), _dot_nt(dh2b, w2b_ref[n])], axis=1)
                da = _bf(dz * (2.0 * jnp.maximum(a_ref[:, n * D:(n + 1) * D].astype(F32), 0.0)))
                da_ref[:, n * D:(n + 1) * D] = da
                dfa = dfa + _dot_nt(da, w1a_ref[n])
                dfb = dfb + _dot_nt(da, w1b_ref[n])
            df = jnp.concatenate([dfa, dfb], axis=1)
            fh, rs1 = _rms(h1_ref[...])
            gff_ref[...] += jnp.sum(df * fh, axis=0, keepdims=True)
            dh1 = dh2 + _rms_bwd(df, fh, rs1, wff_ref[...])
            dh1_ref[...] = dh1
            dh1b = _bf(dh1)
            do_ref[...] = _dot_nt(dh1b, wo_ref[...])
            dwo_acc[0:512, :] += _dot_tn(og_ref[...], dh1b)
            dwo_acc[512:1024, :] += _dot_tn(os_ref[...], dh1b)

        @pl.when(i == nt - 1)
        def _():
            da_ref[...] = jnp.zeros_like(da_ref)
            dh2b_ref[...] = jnp.zeros_like(dh2b_ref)
            dh1_ref[...] = jnp.zeros_like(dh1_ref)
            do_ref[...] = jnp.zeros_like(do_ref)
            for s in range(4):
                for hh in range(2):
                    dwo_ref[hh, s] = dwo_acc[(2 * s + hh) * 128:(2 * s + hh + 1) * 128, :]

    rs = lambda w: pl.BlockSpec((TM, w), lambda i: (i, 0))
    real = lambda w: pl.BlockSpec((TM, w), lambda i: (jnp.minimum(i, nt - 2), 0))
    return pl.pallas_call(
        body, name="mlp_bwd", grid=(nt,),
        in_specs=[real(D), real(DFF), real(D), rs(512), rs(512)] + [VMEM_SPEC] * 6,
        out_specs=[rs(DFF), rs(D), rs(D), rs(D), VMEM_SPEC, pl.BlockSpec((1, D), lambda i: (0, 0))],
        out_shape=[jax.ShapeDtypeStruct((r_tot, DFF), BF16), jax.ShapeDtypeStruct((r_tot, D), BF16),
                   jax.ShapeDtypeStruct((r_tot, D), F32), jax.ShapeDtypeStruct((r_tot, D), F32),
                   jax.ShapeDtypeStruct((2, 4, 128, D), F32), jax.ShapeDtypeStruct((1, D), F32)],
        scratch_shapes=[pltpu.VMEM((D, D), F32)],
        compiler_params=_cp(56),
    )(h1, a, dh2, ogla, oswa, wo, wff, *w1, *w2)


def _ffn_wgrad(f, a, da, dh2b):
    rows = f.shape[0]
    kt = 1024 if rows % 1024 == 0 else TM
    nk = rows // kt

    def body(f_ref, a_ref, da_ref, dh2_ref, dw1_ref, dw2_ref, acc1, acc2):
        k = pl.program_id(1)

        @pl.when(k == 0)
        def _():
            acc1[...] = jnp.zeros_like(acc1)
            acc2[...] = jnp.zeros_like(acc2)

        zr = jnp.maximum(a_ref[...], 0.0)
        acc1[...] += _dot_tn(f_ref[...], da_ref[...])
        acc2[...] += _dot_tn(zr * zr, dh2_ref[...])

        @pl.when(k == nk - 1)
        def _():
            for hh in range(2):
                dw1_ref[hh, 0] = acc1[hh * 512:(hh + 1) * 512, :]
                dw2_ref[hh, 0] = acc2[hh * 512:(hh + 1) * 512, :]

    out = pl.BlockSpec((2, 1, 512, D), lambda n, k: (0, n, 0, 0))
    return pl.pallas_call(
        body, name="ffn_wgrad", grid=(4, nk),
        in_specs=[pl.BlockSpec((kt, D), lambda n, k: (k, 0)), pl.BlockSpec((kt, D), lambda n, k: (k, n)),
                  pl.BlockSpec((kt, D), lambda n, k: (k, n)), pl.BlockSpec((kt, D), lambda n, k: (k, 0))],
        out_specs=[out, out],
        out_shape=[jax.ShapeDtypeStruct((2, 4, 512, D), F32)] * 2,
        scratch_shapes=[pltpu.VMEM((D, D), F32), pltpu.VMEM((D, D), F32)],
        compiler_params=_cp(48, ("arbitrary", "arbitrary")),
    )(f, a, da, dh2b)


def _proj_bwd(x, metapad, wm, wt3, tabs, dgla, dswa_q, dsk, dsv, dlr, dh1, comm=None):
    t = x.shape[0]
    nblk = t // TM

    def body(x_ref, mp_ref, wm_ref, w3_ref, tab_ref, dg_ref, dq_ref, dk_ref, dv_ref, dlr_ref, dh1_ref,
             gx_ref, gmeta_ref, dw_ref, gmix_ref, w_ref, acc):
        i = pl.program_id(0)

        @pl.when(i == 0)
        def _():
            _join_shards(w3_ref, w_ref)
            acc[...] = jnp.zeros_like(acc)
            gmix_ref[...] = jnp.zeros_like(gmix_ref)

        h = jnp.where(i == nblk, mp_ref[...], x_ref[...])
        uh, rs = _rms(h)
        wm_v = wm_ref[...]
        u = _bf(uh * wm_v)
        tab = tab_ref[...]
        dq = _bf(_rope(dq_ref[...] * 0.125, tab, -1.0))
        dk = _bf(_rope(dk_ref[...], tab, -1.0))
        parts = ((dg_ref[...], 0, R_LR), (dlr_ref[:, 0:16], R_LR, 16), (dq, R_LR + 16, 512),
                 (dk, R_LR + 528, 128), (_bf(dv_ref[...]), R_LR + 656, 128))
        du = jnp.zeros((TM, D), F32)
        for val, r0, w in parts:
            du = du + _dot(val, w_ref[r0:r0 + w, :])
            acc[r0:r0 + w, :] += _dot_tn(val, u)
        gmix_ref[...] += jnp.sum(du * uh, axis=0, keepdims=True)
        dh0 = dh1_ref[...] + _rms_bwd(du, uh, rs, wm_v)

        @pl.when(i < nblk)
        def _():
            gx_ref[...] = dh0

        @pl.when(i == nblk)
        def _():
            gmeta_ref[...] = dh0[:NM]
            for s in range(4):
                dw_ref[s] = acc[(DIN // 4) * s:(DIN // 4) * (s + 1), :]

    xs = pl.BlockSpec((TM, D), lambda i: (jnp.minimum(i, nblk - 1), 0))
    rs_ = lambda w: pl.BlockSpec((TM, w), lambda i: (i, 0))
    return _call(
        body, "proj_bwd", (nblk + 1,),
        [xs, VMEM_SPEC, VMEM_SPEC, VMEM_SPEC, rs_(128), rs_(1536), rs_(512), rs_(128), rs_(128), rs_(128), rs_(D)],
        [xs, pl.BlockSpec((NM, D), lambda i: (0, 0)), VMEM_SPEC, pl.BlockSpec((1, D), lambda i: (0, 0))],
        [jax.ShapeDtypeStruct((t, D), F32), jax.ShapeDtypeStruct((NM, D), F32),
         jax.ShapeDtypeStruct((4, DIN // 4, D), F32), jax.ShapeDtypeStruct((1, D), F32)],
        [pltpu.VMEM((DIN, D), BF16), pltpu.VMEM((DIN, D), F32)], _cp(56),
        (x, metapad, wm, wt3, tabs, dgla, dswa_q, dsk, dsv, dlr, dh1), comm)


def _place():
    return lax.axis_index("x"), lax.axis_index("y"), lax.axis_index("c")


def _other_chips(x, y):
    return [(1 - x, y), (x, 1 - y), (1 - x, 1 - y)]


def _dma_sems(*counts):
    return tuple(pltpu.SemaphoreType.DMA((k,)) for k in counts)


def _gather_shards(shards, split):
    n = len(shards)
    two = [a for a in range(n) if split[a]]

    def plan(ins, outs, sems):
        isend, irecv, dsend, drecv, loc = sems
        x, y, c = _place()
        chips = _other_chips(x, y)

        def part(ref, a, half):
            if not split[a]:
                return ref
            w = shards[a].shape[1] // 2
            return ref.at[:, pl.ds(pl.multiple_of(half * w, 128), w)]

        def over_ici(a, k, shard_of):
            tx, ty = chips[k]
            sx, sy = shard_of
            return pltpu.make_async_remote_copy(
                src_ref=part(ins[a], a, c), dst_ref=part(outs[a].at[2 * sx + sy], a, c), send_sem=isend.at[3 * a + k],
                recv_sem=irecv.at[3 * a + k], device_id=(tx, ty, c), device_id_type=MESH)

        def over_d2d(a, k, half):
            tx, ty = chips[k]
            ref = part(outs[a].at[2 * tx + ty], a, half)
            return pltpu.make_async_remote_copy(
                src_ref=ref, dst_ref=ref, send_sem=dsend.at[3 * a + k], recv_sem=drecv.at[3 * a + k],
                device_id=(x, y, 1 - c), device_id_type=MESH)

        def local(a):
            return pltpu.make_async_copy(ins[a], outs[a].at[2 * x + y], loc.at[a])

        pairs = [(a, k) for a in range(n) for k in range(3)]
        first = ([lambda a=a: local(a).start() for a in range(n)]
                 + [lambda a=a, k=k: over_ici(a, k, (x, y)).start() for a, k in pairs],
                 [lambda a=a, k=k: over_ici(a, k, chips[k]).wait_recv() for a, k in pairs]
                 + [lambda a=a, k=k: over_ici(a, k, (x, y)).wait_send() for a, k in pairs]
                 + [lambda a=a: local(a).wait() for a in range(n)])
        pairs2 = [(a, k) for a in two for k in range(3)]
        second = ([lambda a=a, k=k: over_d2d(a, k, c).start() for a, k in pairs2],
                  [lambda a=a, k=k: over_d2d(a, k, 1 - c).wait_recv() for a, k in pairs2]
                  + [lambda a=a, k=k: over_d2d(a, k, c).wait_send() for a, k in pairs2])
        return [first, second] if two else [first]

    return _Comm(tuple(shards), tuple(jax.ShapeDtypeStruct((4,) + s.shape, s.dtype) for s in shards),
                 _dma_sems(3 * n, 3 * n, 3 * n, 3 * n, n), 2 if two else 1, plan)


def _swap_halves(grads):
    n = len(grads)

    def plan(ins, outs, sems):
        send, recv = sems
        x, y, c = _place()

        def swap(a):
            return pltpu.make_async_remote_copy(
                src_ref=ins[a].at[1 - c], dst_ref=outs[a], send_sem=send.at[a], recv_sem=recv.at[a],
                device_id=(x, y, 1 - c), device_id_type=MESH)

        return [([lambda a=a: swap(a).start() for a in range(n)], [lambda a=a: swap(a).wait() for a in range(n)])]

    return _Comm(tuple(grads), tuple(jax.ShapeDtypeStruct(g.shape[1:], g.dtype) for g in grads), _dma_sems(n, n), 1, plan)


SCATTER_ADD_ROWS = 128


def _scatter_shards(parts, late):
    n = len(parts)

    def plan(ins, outs, sems):
        send, recv, loc = sems[:3]
        onward, got = sems[3:3 + n], sems[3 + n:]
        x, y, c = _place()
        x_first = c == 0
        near = (jnp.where(x_first, 1 - x, x), jnp.where(x_first, y, 1 - y))
        far = (jnp.where(x_first, x, 1 - x), jnp.where(x_first, 1 - y, y))
        diagonal = (1 - x, 1 - y)
        shard = lambda chip: 2 * chip[0] + chip[1]

        def hop(src, dst, k, chip):
            return pltpu.make_async_remote_copy(src_ref=src, dst_ref=dst, send_sem=send.at[k], recv_sem=recv.at[k],
                                                device_id=(chip[0], chip[1], c), device_id_type=MESH)

        theirs = lambda a: hop(ins[a].at[shard(near)], outs[a].at[0], 3 * a, near)
        passing = lambda a: hop(ins[a].at[shard(diagonal)], got[a], 3 * a + 1, near)
        summed = lambda a: hop(onward[a], outs[a].at[1], 3 * a + 2, far)
        mine = lambda a: pltpu.make_async_copy(ins[a].at[shard(far)], onward[a], loc.at[a])

        def add(a):
            for r in range(0, parts[a].shape[1], SCATTER_ADD_ROWS):
                rows = slice(r, r + SCATTER_ADD_ROWS)
                onward[a][rows, :] = _bf(onward[a][rows, :].astype(F32) + got[a][rows, :].astype(F32))

        every = range(n)
        first = ([lambda a=a: mine(a).start() for a in every] + [lambda a=a: passing(a).start() for a in every]
                 + [lambda a=a: theirs(a).start() for a in every],
                 [lambda a=a: mine(a).wait() for a in every] + [lambda a=a: passing(a).wait_recv() for a in every]
                 + [lambda a=a: add(a) for a in every])
        second = ([lambda a=a: summed(a).start() for a in every],
                  [lambda a=a: passing(a).wait_send() for a in every] + [lambda a=a: theirs(a).wait() for a in every]
                  + [lambda a=a: summed(a).wait() for a in every])
        return [first, second]

    assert all(p.shape[1] % SCATTER_ADD_ROWS == 0 for p in parts)
    buffers = [pltpu.VMEM(p.shape[1:], p.dtype) for p in parts]
    return _Comm(tuple(parts), tuple(jax.ShapeDtypeStruct((2,) + p.shape[1:], p.dtype) for p in parts),
                 _dma_sems(3 * n, 3 * n, n) + tuple(buffers) * 2, 2, plan, late)


def _join_halves(halves):
    n = len(halves)

    def plan(ins, outs, sems):
        send, recv, loc = sems
        x, y, c = _place()

        def remote(a, half):
            return pltpu.make_async_remote_copy(
                src_ref=ins[a], dst_ref=outs[a].at[half], send_sem=send.at[a], recv_sem=recv.at[a],
                device_id=(x, y, 1 - c), device_id_type=MESH)

        def local(a):
            return pltpu.make_async_copy(ins[a], outs[a].at[c], loc.at[a])

        every = range(n)
        return [([lambda a=a: local(a).start() for a in every] + [lambda a=a: remote(a, c).start() for a in every],
                 [lambda a=a: remote(a, 1 - c).wait_recv() for a in every]
                 + [lambda a=a: remote(a, c).wait_send() for a in every] + [lambda a=a: local(a).wait() for a in every])]

    return _Comm(tuple(halves), tuple(jax.ShapeDtypeStruct((2,) + h.shape, h.dtype) for h in halves),
                 _dma_sems(n, n, n), 1, plan)


def _reduce_w_in(dwt, comm):
    rows, hw = DIN // 4, D // 2
    ci, co = len(comm.ins), len(comm.outs)

    def body(*refs):
        dw_ref, c_in, out_ref, c_out = refs[0], refs[1:1 + ci], refs[1 + ci], refs[2 + ci:2 + ci + co]
        mine, sib, tosend, rbuf, qbuf, full, send, recv, loc = refs[2 + ci + co:11 + ci + co]
        c_sem = refs[11 + ci + co:]
        x, y, c = _place()
        sibling = (x, y, 1 - c)
        (starts, waits), = comm.plan(c_in, c_out, c_sem)
        _run_phase(starts)

        def cols(ref, half):
            window = pl.ds(pl.multiple_of(half * hw, 128), hw)
            return ref.at[:, :, window] if len(ref.shape) == 3 else ref.at[:, window]

        load = pltpu.make_async_copy(cols(dw_ref, c), mine, loc.at[0])
        give = pltpu.make_async_remote_copy(src_ref=cols(dw_ref, 1 - c), dst_ref=sib, send_sem=send.at[3], recv_sem=recv.at[3],
                                            device_id=sibling, device_id_type=MESH)
        load.start()
        give.start()
        load.wait()
        give.wait()
        mine[...] = mine[...] + sib[...]
        cps = []
        for k, (tx, ty) in enumerate(_other_chips(x, y)):
            tosend[k] = _bf(mine[2 * tx + ty])
            cps.append(pltpu.make_async_remote_copy(
                src_ref=tosend.at[k], dst_ref=rbuf.at[k], send_sem=send.at[k], recv_sem=recv.at[k],
                device_id=(tx, ty, c), device_id_type=MESH))
            cps[-1].start()
        for cp in cps:
            cp.wait()
        qbuf[...] = mine[2 * x + y] + rbuf[0].astype(F32) + rbuf[1].astype(F32) + rbuf[2].astype(F32)
        keep = pltpu.make_async_copy(qbuf, cols(full, c), loc.at[1])
        pass_on = pltpu.make_async_remote_copy(src_ref=qbuf, dst_ref=cols(full, c), send_sem=send.at[4], recv_sem=recv.at[4],
                                               device_id=sibling, device_id_type=MESH)
        keep.start()
        pass_on.start()
        keep.wait()
        pass_on.wait_send()
        pltpu.make_async_remote_copy(src_ref=qbuf, dst_ref=cols(full, 1 - c), send_sem=send.at[4], recv_sem=recv.at[4],
                                     device_id=sibling, device_id_type=MESH).wait_recv()
        out_ref[...] = full[...]
        _run_phase(waits)

    outs = pl.pallas_call(
        body, name="reduce_w_in",
        in_specs=[ANY_SPEC] * (1 + ci), out_specs=[VMEM_SPEC] + [ANY_SPEC] * co,
        out_shape=[jax.ShapeDtypeStruct((rows, D), F32)] + list(comm.outs),
        scratch_shapes=[pltpu.VMEM((4, rows, hw), F32), pltpu.VMEM((4, rows, hw), F32), pltpu.VMEM((3, rows, hw), BF16),
                        pltpu.VMEM((3, rows, hw), BF16), pltpu.VMEM((rows, hw), F32), pltpu.VMEM((rows, D), F32),
                        *_dma_sems(5, 5, 2), *comm.sems],
        compiler_params=pltpu.CompilerParams(vmem_limit_bytes=48 << 20),
    )(dwt, *comm.ins)
    return outs[0], outs[1:]


def _allreduce_small(pack):
    p = pack.shape[0]

    def body(in_ref, out_ref, buf, send, recv):
        x, y, c = _place()
        me = 4 * x + 2 * y + c
        buf[me] = in_ref[...]

        def peer_of(k):
            return x ^ (k >> 2), y ^ ((k >> 1) & 1), c ^ (k & 1)

        sends = [pltpu.make_async_remote_copy(
            src_ref=in_ref, dst_ref=buf.at[me], send_sem=send.at[k - 1], recv_sem=recv.at[k - 1],
            device_id=peer_of(k), device_id_type=MESH) for k in range(1, 8)]
        for cp in sends:
            cp.start()
        for k in range(1, 8):
            px, py, pc = peer_of(k)
            pltpu.make_async_remote_copy(
                src_ref=in_ref, dst_ref=buf.at[4 * px + 2 * py + pc], send_sem=send.at[k - 1], recv_sem=recv.at[k - 1],
                device_id=(x, y, c), device_id_type=MESH).wait_recv()
        for cp in sends:
            cp.wait_send()
        acc = buf[0]
        for d in range(1, 8):
            acc = acc + buf[d]
        out_ref[...] = acc

    return pl.pallas_call(
        body, name="allreduce_small",
        in_specs=[VMEM_SPEC], out_specs=VMEM_SPEC, out_shape=jax.ShapeDtypeStruct(pack.shape, F32),
        scratch_shapes=[pltpu.VMEM((8, p, D), F32), *_dma_sems(7, 7)],
    )(pack)


GRID4 = 4


def _sum_cores(core_shard, mine, theirs):
    n = len(mine)

    def body(cs_ref, *refs):
        ms, ts, bfs, owns = refs[:n], refs[n:2 * n], refs[2 * n:3 * n], refs[3 * n:]
        keep = pl.program_id(0) == cs_ref[1]
        for a in range(n):
            acc = ms[a][0, 0] + ts[a][0]
            bfs[a][0] = _bf(acc)

            @pl.when(keep)
            def _():
                owns[a][...] = acc

    shapes = [m.shape[2:] for m in mine]
    in_specs = ([pl.BlockSpec((1, 1) + s, lambda i, cs: (cs[0], i, 0, 0)) for s in shapes]
                + [pl.BlockSpec((1,) + s, lambda i, cs: (i, 0, 0)) for s in shapes])
    out_specs = ([pl.BlockSpec((1,) + s, lambda i, cs: (i, 0, 0)) for s in shapes]
                 + [pl.BlockSpec(s, lambda i, cs: (0, 0)) for s in shapes])
    outs = pl.pallas_call(
        body, name="sum_cores",
        grid_spec=pltpu.PrefetchScalarGridSpec(num_scalar_prefetch=1, grid=(4,), in_specs=in_specs, out_specs=out_specs),
        out_shape=[jax.ShapeDtypeStruct((4,) + s, BF16) for s in shapes] + [jax.ShapeDtypeStruct(s, F32) for s in shapes],
        compiler_params=_cp(48),
    )(core_shard, *mine, *theirs)
    return outs[:n], outs[n:]


def _sum_chips(own, arrived):
    n = len(own)

    def body(*refs):
        os_, ars, outs = refs[:n], refs[n:2 * n], refs[2 * n:]
        for a in range(n):
            outs[a][...] = os_[a][...] + ars[a][0].astype(F32) + ars[a][1].astype(F32)

    blocks = [(o.shape[0] // GRID4, o.shape[1]) for o in own]
    return pl.pallas_call(
        body, name="sum_chips", grid=(GRID4,),
        in_specs=([pl.BlockSpec(b, lambda i: (i, 0)) for b in blocks]
                  + [pl.BlockSpec((2,) + b, lambda i: (0, i, 0)) for b in blocks]),
        out_specs=[pl.BlockSpec(b, lambda i: (i, 0)) for b in blocks],
        out_shape=[jax.ShapeDtypeStruct(o.shape, F32) for o in own],
        compiler_params=_cp(32),
    )(*own, *arrived)


def _adamw_math(w, g, m, v):
    m2 = ADAM_B1 * m + (1.0 - ADAM_B1) * g
    v2 = ADAM_B2 * v + (1.0 - ADAM_B2) * (g * g)
    m_hat = m2 / (1.0 - ADAM_B1 ** ADAM_STEP)
    v_hat = v2 / (1.0 - ADAM_B2 ** ADAM_STEP)
    return -ADAM_LR * (m_hat / (jnp.sqrt(v_hat) + ADAM_EPS) + ADAM_WD * w), m2, v2


def _adamw_big(ws, gs, ms, vs):
    n = len(ws)

    def body(*refs):
        for a in range(n):
            d, m2, v2 = _adamw_math(refs[a][...], refs[n + a][...], refs[2 * n + a][...], refs[3 * n + a][...])
            refs[4 * n + a][...] = d
            refs[5 * n + a][...] = m2
            refs[6 * n + a][...] = v2

    specs = [pl.BlockSpec((w.shape[0] // GRID4, w.shape[1]), lambda i: (i, 0)) for w in ws]
    return pl.pallas_call(
        body, name="adamw_big", grid=(GRID4,),
        in_specs=specs * 4, out_specs=specs * 3,
        out_shape=[jax.ShapeDtypeStruct(w.shape, F32) for w in ws] * 3,
        compiler_params=_cp(48),
    )(*ws, *gs, *ms, *vs)


def _adamw_small(ws, gs, ms, vs):
    n = len(ws)

    def body(*refs):
        for a in range(n):
            d, m2, v2 = _adamw_math(refs[a][...], refs[n + a][...], refs[2 * n + a][...], refs[3 * n + a][...])
            refs[4 * n + a][...] = d
            refs[5 * n + a][...] = m2
            refs[6 * n + a][...] = v2

    return pl.pallas_call(
        body, name="adamw_small",
        in_specs=[VMEM_SPEC] * (4 * n), out_specs=[VMEM_SPEC] * (3 * n),
        out_shape=[jax.ShapeDtypeStruct(w.shape, F32) for w in ws] * 3,
        compiler_params=pltpu.CompilerParams(vmem_limit_bytes=40 << 20),
    )(*ws, *gs, *ms, *vs)


def kernel(x, meta_tokens, norm_mix_w, w_in, w_gate_up, b_gate, gla_norm_w, sinks, w_out, norm_ff_w, w_ff1, w_ff2, final_norm_w, loss_target, m_meta_tokens, m_norm_mix_w, m_w_in, m_w_gate_up, m_b_gate, m_gla_norm_w, m_sinks, m_w_out, m_norm_ff_w, m_w_ff1, m_w_ff2, m_final_norm_w, v_meta_tokens, v_norm_mix_w, v_w_in, v_w_gate_up, v_b_gate, v_gla_norm_w, v_sinks, v_w_out, v_norm_ff_w, v_w_ff1, v_w_ff2, v_final_norm_w):
    xi, yi, ci = _place()
    shard = (2 * xi + yi).astype(jnp.int32).reshape(1)
    core = ci.astype(jnp.int32).reshape(1)

    small = jnp.concatenate([meta_tokens, w_gate_up[0], jnp.zeros((NM, 64), F32)], axis=1)
    wt3, g_small = _run_comm(_gather_shards([_bf(w_in[0].T), small], [True, False]), "gather_w_in")
    meta = g_small[:, :, 0:256].transpose(1, 0, 2).reshape(NM, D)
    wgu = g_small[:, :, 256:320].transpose(1, 0, 2).reshape(NM, 256)

    xs, tgt = x[0], loss_target[0]
    t = xs.shape[0]
    wfin = final_norm_w.reshape(1, D)
    metapad = jnp.concatenate([meta, jnp.zeros((TM - NM, D), F32)], axis=0)
    wgu_p = _bf(jnp.concatenate([wgu, jnp.zeros((128 - 16, 256), F32)], axis=0))
    tabs = _rope_tables(t)

    w1s, w2s = _bf(w_ff1[0]), _bf(w_ff2[0])
    proj, (g_out, w1a) = _proj_fwd(xs, metapad, norm_mix_w, wt3, tabs,
                                   _gather_shards([_bf(w_out[0]), w1s[:HK]], [True] * 2))
    (ogla, oraw, sst, bcum, dgate), (w1b, w2a) = _gla_fwd(proj, wgu_p, b_gate, gla_norm_w, t,
                                                          _gather_shards([w1s[HK:], w2s[:HK]], [True] * 2))
    (oswa, lse), (w2b,) = _swa_fwd(proj, sinks, t, _gather_shards([w2s[HK:]], [True]))
    wo, w1, w2 = g_out.reshape(D, D), (w1a, w1b), (w2a, w2b)
    h1, f, a, dh2, loss, gfin = _mlp_fwd(xs, tgt, ogla, oswa, wo, norm_ff_w, w1, w2, wfin)

    da, dh2b, dh1, do, dwo, gff = _mlp_bwd(h1, a, dh2, ogla, oswa, wo, norm_ff_w, w1, w2)
    dw1, dw2 = _ffn_wgrad(f, a, da, dh2b)
    big = [dwo, dw1, dw2]
    (dgla, dlr, dwgu, dbg, dgnw), theirs = _gla_bwd(proj, oraw, sst, bcum, dgate, do, wgu_p, gla_norm_w, t,
                                                    _swap_halves(big))
    sums_bf, own = _sum_cores(jnp.concatenate([core, shard]), big, theirs)
    swa_grid = (t + TM) // SB // _swa_steps(t + TM)
    (dsq, dsk, dsv, dsink), arrived = _swa_bwd(proj, sinks, lse, do, t, _scatter_shards(sums_bf, swa_grid // 3))
    halves = _sum_chips(own, arrived)
    (gx, gmeta, dwt, gmix), _ = _proj_bwd(xs, metapad, norm_mix_w, wt3, tabs, dgla, dsq, dsk, dsv, dlr, dh1)

    gwt_in, joined = _reduce_w_in(dwt, _join_halves(halves))
    gw_out, gw_1, gw_2 = [j.reshape((-1, j.shape[2])) for j in joined]

    tail = jnp.concatenate([dbg, dgnw, dsink, loss, jnp.zeros((1, D - 256 - 128 - 8 - 1), F32)], axis=1)
    pack = jnp.concatenate([gmeta, gmix, gff, gfin, tail, dwgu[:16].reshape(4, D)], axis=0)
    tot = _allreduce_small(pack)
    g_meta = lax.dynamic_slice_in_dim(tot[0:NM], shard[0] * 256, 256, axis=1)
    g_mix, g_ff, g_fin = tot[16:17], tot[17:18], tot[18]
    g_bg, g_gnw, g_sinks, loss_tot = tot[19:20, 0:256], tot[19:20, 256:384], tot[19:20, 384:392], tot[19, 392]
    g_wgu = lax.dynamic_slice_in_dim(tot[20:24].reshape(NM, 256), shard[0] * 64, 64, axis=1)

    bo = _adamw_big([w_out[0], w_ff1[0], w_ff2[0]], [gw_out, gw_1, gw_2], [m_w_out[0], m_w_ff1[0], m_w_ff2[0]],
                    [v_w_out[0], v_w_ff1[0], v_w_ff2[0]])

    fin2 = lambda a: a.reshape(1, D)
    sw = [meta_tokens, norm_mix_w, w_gate_up[0], b_gate, gla_norm_w, sinks, norm_ff_w, fin2(final_norm_w), w_in[0].T]
    sg = [g_meta, g_mix, g_wgu, g_bg, g_gnw, g_sinks, g_ff, fin2(g_fin), gwt_in]
    sm = [m_meta_tokens, m_norm_mix_w, m_w_gate_up[0], m_b_gate, m_gla_norm_w, m_sinks, m_norm_ff_w, fin2(m_final_norm_w),
          m_w_in[0].T]
    sv = [v_meta_tokens, v_norm_mix_w, v_w_gate_up[0], v_b_gate, v_gla_norm_w, v_sinks, v_norm_ff_w, fin2(v_final_norm_w),
          v_w_in[0].T]
    so = _adamw_small(sw, sg, sm, sv)

    def ordered(small_o, big_o):
        meta_, mix_, wgu_, bg_, gnw_, sinks_, ff_, fin_, wt_ = small_o
        w_out_, w_1_, w_2_ = big_o
        return (meta_, mix_, wt_.T[None], wgu_[None], bg_, gnw_, sinks_, w_out_[None], ff_, w_1_[None], w_2_[None],
                fin_.reshape(D))

    grads = ordered(sg, [gw_out, gw_1, gw_2])
    deltas = ordered(so[0:9], bo[0:3])
    new_m = ordered(so[9:18], bo[3:6])
    new_v = ordered(so[18:27], bo[6:9])
    return (loss_tot, gx[None], *grads, *deltas, *new_m, *new_v)
```

```python
import functools
from typing import Callable, NamedTuple

import jax
import jax.numpy as jnp
import numpy as np
from jax import lax
from jax.experimental import pallas as pl
from jax.experimental.pallas import tpu as pltpu

F32 = jnp.float32
BF16 = jnp.bfloat16

D = 1024
DFF = 4096
NM = 16
TM = 256
DK = 64
CH = 128
SB = 128
EPS = 1e-5
C_GQ, C_GK, C_GV, C_GR, C_SQ, C_SK, C_SV, C_LR, DINP = 0, 256, 512, 1024, 1536, 2048, 2176, 2304, 2432
DIN = 2320
R_LR = 1536
ROPE_THETA = 500000.0
ADAM_LR, ADAM_B1, ADAM_B2, ADAM_EPS, ADAM_WD, ADAM_STEP = 0.001, 0.9, 0.999, 1e-08, 0.01, 10
NEG = -1e30
MESH = pl.DeviceIdType.MESH
VMEM_SPEC = pl.BlockSpec(memory_space=pltpu.VMEM)
ANY_SPEC = pl.BlockSpec(memory_space=pl.ANY)
SMEM_SPEC = pl.BlockSpec(memory_space=pltpu.SMEM)


def _cp(vmem_mb, sem=("arbitrary",)):
    return pltpu.CompilerParams(dimension_semantics=sem, vmem_limit_bytes=vmem_mb << 20)


def _dot(a, b):
    return jnp.dot(a, b, preferred_element_type=F32)


def _dot_nt(a, b):
    return lax.dot_general(a, b, (((1,), (1,)), ((), ())), preferred_element_type=F32)


def _dot_tn(a, b):
    return lax.dot_general(a, b, (((0,), (0,)), ((), ())), preferred_element_type=F32)


def _bf(x):
    return x.astype(BF16)


def _dot3(m01, x):
    x1 = _bf(x)
    r1 = x - x1.astype(F32)
    x2 = _bf(r1)
    x3 = _bf(r1 - x2.astype(F32))
    return _dot(m01, x1) + _dot(m01, x2) + _dot(m01, x3)


def _rms(h):
    rs = lax.rsqrt(jnp.mean(h * h, axis=-1, keepdims=True) + EPS)
    return h * rs, rs


def _rms_bwd(dy, yhat, rs, w):
    dyh = dy * w
    return rs * (dyh - yhat * jnp.mean(dyh * yhat, axis=-1, keepdims=True))


class _Comm(NamedTuple):
    ins: tuple
    outs: tuple
    sems: tuple
    phases: int
    plan: Callable
    late: int = 0


def _run_phase(fns):
    for fn in fns:
        fn()


def _call(body, name, grid, in_specs, out_specs, out_shape, scratch, params, args, comm=None):
    if comm is None:
        outs = pl.pallas_call(body, name=name, grid=grid, in_specs=in_specs, out_specs=out_specs, out_shape=out_shape,
                              scratch_shapes=scratch, compiler_params=params)(*args)
        return outs, None
    n_in, n_out, n_scr = len(in_specs), len(out_specs), len(scratch)
    ci, co = len(comm.ins), len(comm.outs)
    last = grid[0] - 1
    marks = [0, max(1, last - (comm.late or max(2, (last + 1) // 6)))][:comm.phases]

    def wrapped(*refs):
        own_in, c_in = refs[:n_in], refs[n_in:n_in + ci]
        refs = refs[n_in + ci:]
        own_out, c_out = refs[:n_out], refs[n_out:n_out + co]
        refs = refs[n_out + co:]
        own_scr, c_sem = refs[:n_scr], refs[n_scr:]
        i = pl.program_id(0)

        for p, mark in enumerate(marks):
            @pl.when(i == mark)
            def _():
                plan = comm.plan(c_in, c_out, c_sem)
                if p > 0:
                    _run_phase(plan[p - 1][1])
                _run_phase(plan[p][0])

        body(*own_in, *own_out, *own_scr)

        @pl.when(i == last)
        def _():
            _run_phase(comm.plan(c_in, c_out, c_sem)[-1][1])

    outs = pl.pallas_call(
        wrapped, name=name, grid=grid, in_specs=list(in_specs) + [ANY_SPEC] * ci, out_specs=list(out_specs) + [ANY_SPEC] * co,
        out_shape=list(out_shape) + list(comm.outs), scratch_shapes=list(scratch) + list(comm.sems), compiler_params=params,
    )(*args, *comm.ins)
    return outs[:n_out], outs[n_out:]


def _run_comm(comm, name):
    ci, co = len(comm.ins), len(comm.outs)

    def body(*refs):
        for starts, waits in comm.plan(refs[:ci], refs[ci:ci + co], refs[ci + co:]):
            _run_phase(starts)
            _run_phase(waits)

    return pl.pallas_call(body, name=name, in_specs=[ANY_SPEC] * ci, out_specs=[ANY_SPEC] * co, out_shape=list(comm.outs),
                          scratch_shapes=list(comm.sems))(*comm.ins)


def _join_shards(w3_ref, w_ref):
    for s in range(4):
        w_ref[(DIN // 4) * s:(DIN // 4) * (s + 1), :] = w3_ref[s]


def _proj_fwd(x, metapad, wm, wt3, tabs, comm=None):
    t = x.shape[0]
    nblk = t // TM

    def body(x_ref, mp_ref, wm_ref, w3_ref, tab_ref, proj_ref, w_ref):
        i = pl.program_id(0)

        @pl.when(i == 0)
        def _():
            _join_shards(w3_ref, w_ref)

        h = jnp.where(i == nblk, mp_ref[...], x_ref[...])
        u, _ = _rms(h)
        ub = _bf(u * wm_ref[...])
        proj_ref[:, 0:C_SQ] = _dot_nt(ub, w_ref[0:R_LR, :])
        att = _dot_nt(ub, w_ref[R_LR + 16:DIN, :])
        tab = tab_ref[...]
        proj_ref[:, C_SQ:C_SK] = _rope(att[:, 0:512], tab, 1.0) * 0.125
        proj_ref[:, C_SK:C_SV] = _rope(att[:, 512:640], tab, 1.0)
        proj_ref[:, C_SV:C_LR] = att[:, 640:768]
        proj_ref[:, C_LR:DINP] = jnp.zeros((TM, DINP - C_LR), F32)
        proj_ref[:, C_LR:C_LR + 16] = _dot_nt(ub, w_ref[R_LR:R_LR + 16, :])

    (proj,), got = _call(
        body, "proj_fwd", (nblk + 1,),
        [pl.BlockSpec((TM, D), lambda i: (jnp.minimum(i, nblk - 1), 0)), VMEM_SPEC, VMEM_SPEC, VMEM_SPEC,
         pl.BlockSpec((TM, 128), lambda i: (i, 0))],
        [pl.BlockSpec((TM, DINP), lambda i: (i, 0))], [jax.ShapeDtypeStruct((t + TM, DINP), F32)],
        [pltpu.VMEM((DIN, D), BF16)], _cp(48), (x, metapad, wm, wt3, tabs), comm)
    return proj, got


def _chunk_masks():
    r = lax.broadcasted_iota(jnp.int32, (TM, TM), 0)
    c = lax.broadcasted_iota(jnp.int32, (TM, TM), 1)
    same = (r // CH) == (c // CH)
    lower = _bf(jnp.where(same & (c <= r), 1.0, 0.0))
    upper = _bf(jnp.where(same & (c >= r), 1.0, 0.0))
    return lower, upper


def _gla_gate(lr, wgu, bg, valid, lower):
    z = _dot(_bf(lr), wgu) + bg
    g = (jnp.minimum(z, 0.0) - jnp.log(1.0 + jnp.exp(-jnp.abs(z)))) * (1.0 / 16.0)
    g = jnp.where(valid, g, 0.0)
    return z, _dot3(lower, g)


def _gla_decays(q, k, b):
    nc = TM // CH
    b3 = b.reshape(nc, CH, 256)
    blast = b3[:, CH - 1:CH, :]
    eb = jnp.exp(b)
    enb = jnp.exp(-b)
    ebl = jnp.exp(blast - b3).reshape(TM, 256)
    return eb, enb, ebl, jnp.exp(blast)


def _tri(lower_incl):
    r = lax.broadcasted_iota(jnp.int32, (CH, CH), 0)
    c = lax.broadcasted_iota(jnp.int32, (CH, CH), 1)
    return ((c <= r) if lower_incl else (c >= r))[None]


def _gla_fwd(proj, wgu, bg, gnw, t, comm=None):
    nblk = t // TM
    nt = nblk + 1
    nc = TM // CH

    def blk(i):
        return (i + nblk) % nt

    def body(q_ref, k_ref, v_ref, r_ref, lr_ref, wgu_ref, bg_ref, gnw_ref, o_ref, oraw_ref, sst_ref, b_ref, dgate_ref,
             st_scr):
        i = pl.program_id(0)

        @pl.when(i == 0)
        def _():
            st_scr[...] = jnp.zeros_like(st_scr)

        rows = blk(i) * TM + lax.broadcasted_iota(jnp.int32, (TM, 1), 0)
        lower, _ = _chunk_masks()
        valid = rows < t + NM
        z, b = _gla_gate(lr_ref[...], wgu_ref[...], bg_ref[...], valid, lower)
        b_ref[...] = b
        dgate_ref[...] = jnp.where(valid, (1.0 / 16.0) / (1.0 + jnp.exp(z)), 0.0)
        q = q_ref[...]
        k = k_ref[...]
        eb, enb, ebl, eblast = _gla_decays(q, k, b)
        qt = q * 0.125 * eb
        kt = k * enb
        kh = k * ebl
        tril = _tri(True)
        heads = range(4)
        hs = [slice(h * DK, (h + 1) * DK) for h in heads]
        qh = [_bf(qt[:, hs[h]]).reshape(nc, CH, DK) for h in heads]
        kth = [_bf(kt[:, hs[h]]).reshape(nc, CH, DK) for h in heads]
        khh = [_bf(kh[:, hs[h]]).reshape(nc, CH, DK) for h in heads]
        vh = [_bf(v_ref[:, h * 128:(h + 1) * 128]).reshape(nc, CH, 128) for h in heads]
        a = [jnp.einsum('cid,cjd->cij', qh[h], kth[h], preferred_element_type=F32) for h in heads]
        kv = [jnp.einsum('cjv,cjd->cvd', vh[h], khh[h], preferred_element_type=F32) for h in heads]
        o = [jnp.einsum('cij,cjv->civ', _bf(jnp.where(tril, a[h], 0.0)), vh[h], preferred_element_type=F32) for h in heads]
        states = []
        for h in heads:
            st = st_scr[h]
            per_chunk = []
            for c in range(nc):
                sst_ref[c, h] = st
                per_chunk.append(_bf(st))
                st = st * eblast[c, :, hs[h]] + kv[h][c]
            st_scr[h] = st
            states.append(per_chunk)
        o_inter = [[_dot_nt(qh[h][c], states[h][c]) for c in range(nc)] for h in heads]
        oraw = jnp.concatenate([(o[h] + jnp.stack(o_inter[h])).reshape(TM, 128) for h in heads], axis=1)
        oraw_ref[...] = oraw
        gn = gnw_ref[...]
        res = []
        for h in range(4):
            on, _ = _rms(oraw[:, h * 128:(h + 1) * 128])
            r = r_ref[:, h * 128:(h + 1) * 128]
            res.append(on * gn * (r * jax.nn.sigmoid(r)))
        o_ref[...] = _bf(jnp.concatenate(res, axis=1))

    def spec(w, cb):
        return pl.BlockSpec((TM, w), lambda i: (blk(i), cb))

    return _call(
        body, "gla_fwd", (nt,),
        [spec(256, 0), spec(256, 1), spec(512, 1), spec(512, 2), spec(128, C_LR // 128), VMEM_SPEC, VMEM_SPEC, VMEM_SPEC],
        [spec(512, 0), spec(512, 0), pl.BlockSpec((nc, 4, 128, DK), lambda i: (blk(i), 0, 0, 0)), spec(256, 0), spec(256, 0)],
        [jax.ShapeDtypeStruct((t + TM, 512), BF16), jax.ShapeDtypeStruct((t + TM, 512), F32),
         jax.ShapeDtypeStruct((nt * nc, 4, 128, DK), F32), jax.ShapeDtypeStruct((t + TM, 256), F32),
         jax.ShapeDtypeStruct((t + TM, 256), F32)],
        [pltpu.VMEM((4, 128, DK), F32)], _cp(40), (proj, proj, proj, proj, proj, wgu, bg, gnw), comm)


def _gla_bwd(proj, oraw, sst, bcum, dgate, do, wgu, gnw, t, comm=None):
    nblk = t // TM
    nt = nblk + 1
    nc = TM // CH

    def blk(i):
        return (2 * nblk - i) % nt

    def body(q_ref, k_ref, v_ref, r_ref, lr_ref, oraw_ref, sst_ref, b_ref, dgate_ref, do_ref, wgu_ref, gnw_ref,
             dgla_ref, dlr_ref, dwgu_ref, dbg_ref, dgnw_ref, dst_scr):
        i = pl.program_id(0)

        @pl.when(i == 0)
        def _():
            dst_scr[...] = jnp.zeros_like(dst_scr)
            dwgu_ref[...] = jnp.zeros_like(dwgu_ref)
            dbg_ref[...] = jnp.zeros_like(dbg_ref)
            dgnw_ref[...] = jnp.zeros_like(dgnw_ref)

        _, upper = _chunk_masks()
        lr = lr_ref[...]
        b = b_ref[...]
        q = q_ref[...]
        k = k_ref[...]
        eb, enb, ebl, eblast = _gla_decays(q, k, b)
        qt = q * 0.125 * eb
        kt = k * enb
        kh = k * ebl
        gn = gnw_ref[...]
        tril = _tri(True)
        triu = _tri(False)
        heads = range(4)
        hs = [slice(h * DK, (h + 1) * DK) for h in heads]
        vs = [slice(h * 128, (h + 1) * 128) for h in heads]
        ein = functools.partial(jnp.einsum, preferred_element_type=F32)
        dr_l, doh = [], []
        dgn = jnp.zeros((1, 128), F32)
        for h in heads:
            on, rs = _rms(oraw_ref[:, vs[h]])
            r = r_ref[:, vs[h]]
            sig = jax.nn.sigmoid(r)
            sil = r * sig
            dy = do_ref[:, vs[h]]
            dr_l.append(dy * on * gn * (sig * (1.0 + r * (1.0 - sig))))
            dgn = dgn + jnp.sum(dy * sil * on, axis=0, keepdims=True)
            doh.append(_bf(_rms_bwd(dy * sil, on, rs, gn)).reshape(nc, CH, 128))
        dgnw_ref[...] += dgn
        qh = [_bf(qt[:, hs[h]]).reshape(nc, CH, DK) for h in heads]
        kth = [_bf(kt[:, hs[h]]).reshape(nc, CH, DK) for h in heads]
        khh = [_bf(kh[:, hs[h]]).reshape(nc, CH, DK) for h in heads]
        vh = [_bf(v_ref[:, vs[h]]).reshape(nc, CH, 128) for h in heads]
        at = [ein('cjd,cid->cji', kth[h], qh[h]) for h in heads]
        da = [ein('civ,cjv->cij', doh[h], vh[h]) for h in heads]
        dat = [ein('cjv,civ->cji', vh[h], doh[h]) for h in heads]
        gq = [ein('civ,cid->cvd', doh[h], qh[h]) for h in heads]
        stf = [sst_ref[:, h] for h in heads]
        dqs = [ein('civ,cvd->cid', doh[h], _bf(stf[h])) for h in heads]
        dv = [ein('cji,civ->cjv', _bf(jnp.where(triu, at[h], 0.0)), doh[h]) for h in heads]
        dqt = [ein('cij,cjd->cid', _bf(jnp.where(tril, da[h], 0.0)), kth[h]) + dqs[h] for h in heads]
        dkt = [ein('cji,cid->cjd', _bf(jnp.where(triu, dat[h], 0.0)), qh[h]) for h in heads]
        dse = []
        for h in heads:
            dst = dst_scr[h]
            dsend = [None] * nc
            for c in reversed(range(nc)):
                dsend[c] = dst
                dst = dst * eblast[c, :, hs[h]] + gq[h][c]
            dst_scr[h] = dst
            dse.append(jnp.stack(dsend))
        dseb = [_bf(d) for d in dse]
        dv = [dv[h] + ein('cjd,cvd->cjv', khh[h], dseb[h]) for h in heads]
        dkh = [ein('cjv,cvd->cjd', vh[h], dseb[h]) for h in heads]
        carried = jnp.concatenate([jnp.sum(dse[h] * stf[h], axis=1, keepdims=True) for h in heads], axis=2)
        wide = lambda parts: jnp.concatenate([p.reshape(TM, DK) for p in parts], axis=1)
        dqt_w, dkt_w, dkh_w = wide(dqt), wide(dkt), wide(dkh)
        dkh_kh = dkh_w * kh
        extra = jnp.sum(dkh_kh.reshape(nc, CH, 256), axis=1, keepdims=True) + eblast * carried
        db = dqt_w * qt - dkt_w * kt - dkh_kh
        dg = _dot3(upper, db) + jnp.broadcast_to(extra, (nc, CH, 256)).reshape(TM, 256)
        dz = dg * dgate_ref[...]
        dzb = _bf(dz)
        dlr_ref[...] = _bf(_dot_nt(dzb, wgu_ref[...]))
        dwgu_ref[...] += _dot_tn(_bf(lr), dzb)
        dbg_ref[...] += jnp.sum(dz, axis=0, keepdims=True)
        dq = dqt_w * eb * 0.125
        dk = dkt_w * enb + dkh_w * ebl
        dgla_ref[...] = _bf(jnp.concatenate([dq, dk] + [d.reshape(TM, 128) for d in dv] + dr_l, axis=1))

    def spec(w, cb):
        return pl.BlockSpec((TM, w), lambda i: (blk(i), cb))

    def acc(shape):
        return pl.BlockSpec(shape, lambda i: (0, 0))

    return _call(
        body, "gla_bwd", (nt,),
        [spec(256, 0), spec(256, 1), spec(512, 1), spec(512, 2), spec(128, C_LR // 128), spec(512, 0),
         pl.BlockSpec((nc, 4, 128, DK), lambda i: (blk(i), 0, 0, 0)), spec(256, 0), spec(256, 0), spec(512, 0),
         VMEM_SPEC, VMEM_SPEC],
        [spec(1536, 0), spec(128, 0), acc((128, 256)), acc((1, 256)), acc((1, 128))],
        [jax.ShapeDtypeStruct((t + TM, 1536), BF16), jax.ShapeDtypeStruct((t + TM, 128), BF16),
         jax.ShapeDtypeStruct((128, 256), F32), jax.ShapeDtypeStruct((1, 256), F32), jax.ShapeDtypeStruct((1, 128), F32)],
        [pltpu.VMEM((4, 128, DK), F32)], _cp(48), (proj, proj, proj, proj, proj, oraw, sst, bcum, dgate, do, wgu, gnw), comm)


def _rope_tables(t):
    r = t + TM
    row = np.arange(r)
    pos = np.where(row < t, row + NM, np.where(row < t + NM, row - t, 0)).astype(np.float32)
    inv_freq = (1.0 / (np.float32(ROPE_THETA) ** (np.arange(0, 16, 2, dtype=np.float32) / np.float32(16)))).astype(np.float32)
    ang = (pos[:, None] * inv_freq[None, :]).astype(np.float32)
    cos, sin = np.cos(ang).astype(np.float32), np.sin(ang).astype(np.float32)
    one, zero = np.ones((r, 48), np.float32), np.zeros((r, 48), np.float32)
    return jnp.asarray(np.concatenate([cos, cos, one, -sin, sin, zero], axis=1))


def _rope(x, tab, sign):
    w = x.shape[1]
    rep = w // 64
    c = jnp.concatenate([tab[:, 0:64]] * rep, axis=1)
    s = jnp.concatenate([tab[:, 64:128]] * rep, axis=1)
    lane = lax.rem(lax.broadcasted_iota(jnp.int32, x.shape, 1), 64)
    partner = jnp.where(lane < 8, pltpu.roll(x, w - 8, 1), jnp.where(lane < 16, pltpu.roll(x, 8, 1), 0.0))
    return x * c + sign * (partner * s)


HB_BWD = 4


def _stack(x, hg):
    w = x.shape[1] // hg
    return x if hg == 1 else jnp.concatenate([x[:, g * w:(g + 1) * w] for g in range(hg)], axis=0)


def _unstack(x, hg):
    return x if hg == 1 else jnp.concatenate([x[g * SB:(g + 1) * SB] for g in range(hg)], axis=1)


def _swa_steps(r_tot):
    blocks = r_tot // SB
    return next(n for n in (6, 3, 2) if blocks % n == 0 and blocks // n >= 2)


def _swa_specs(nsb, nbq):
    def rows(h, w, cb, f):
        return pl.BlockSpec((h, w), lambda i: (f(i), cb))
    pair = lambda i: i
    prev = lambda i: jnp.maximum(nbq * i - 1, 0)
    meta = lambda i: nsb
    return rows, pair, prev, meta


def _swa_fwd(proj, sinks, t, comm=None):
    nsb = t // SB
    r_tot = t + TM
    nbq = _swa_steps(r_tot)
    qb = nbq * SB
    rows, pair, prev, meta = _swa_specs(nsb, nbq)

    def body(sink_ref, q_ref, kc_ref, kp_ref, km_ref, vc_ref, vp_ref, vm_ref, o_ref, lse_ref):
        i = pl.program_id(0)
        key = lax.broadcasted_iota(jnp.int32, (SB, SB), 0)
        qry = lax.broadcasted_iota(jnp.int32, (SB, SB), 1)
        km, vm = km_ref[0:NM, :], vm_ref[0:NM, :]
        for j in range(nbq):
            b = nbq * i + j
            rs = slice(j * SB, (j + 1) * SB)
            before = slice((j - 1) * SB, j * SB)
            real = b < nsb
            masks = (key <= qry, (key > qry) & (b > 0) & real, real)
            k3 = (kc_ref[rs, :], kp_ref[...] if j == 0 else kc_ref[before, :], km)
            v3 = (vc_ref[rs, :], vp_ref[...] if j == 0 else vc_ref[before, :], vm)
            valid = b * SB + lax.broadcasted_iota(jnp.int32, (1, SB), 1) < t + NM
            heads = range(8)
            kb = [[_bf(k[:, kv * 64:(kv + 1) * 64]) for k in k3] for kv in range(2)]
            vt = [[_bf(v[:, kv * 64:(kv + 1) * 64].T) for v in v3] for kv in range(2)]
            raw = [[_dot_nt(k, _bf(q_ref[rs, h * 64:(h + 1) * 64])) for k in kb[h // 4]] for h in heads]
            probs, inv_l, lse_l = [], [], []
            for h in heads:
                s = [jnp.where(m, sx, NEG) for m, sx in zip(masks, raw[h])]
                sink = sink_ref[0, h]
                top = jnp.maximum(jnp.max(jnp.maximum(s[0], s[1]), axis=0, keepdims=True),
                                  jnp.maximum(jnp.max(s[2], axis=0, keepdims=True), sink))
                p = [jnp.exp(sx - top) for sx in s]
                l = (jnp.sum(p[0] + p[1], axis=0, keepdims=True) + jnp.sum(p[2], axis=0, keepdims=True)
                     + jnp.exp(sink - top))
                probs.append([_bf(px) for px in p])
                inv_l.append(1.0 / l)
                lse_l.append(top + jnp.log(l))
            o_t = [_dot(vt[h // 4][0], probs[h][0]) + _dot(vt[h // 4][1], probs[h][1]) + _dot(vt[h // 4][2], probs[h][2])
                   for h in heads]
            o_ref[rs, :] = _bf(jnp.concatenate([jnp.where(valid, o_t[h] * inv_l[h], 0.0).T for h in heads], axis=1))
            lse_ref[:, rs] = jnp.concatenate(lse_l, axis=0)

    ck, cv = C_SK // 128, C_SV // 128
    return _call(
        body, "swa_fwd", (r_tot // qb,),
        [SMEM_SPEC, rows(qb, 512, C_SQ // 512, pair),
         rows(qb, 128, ck, pair), rows(SB, 128, ck, prev), rows(SB, 128, ck, meta),
         rows(qb, 128, cv, pair), rows(SB, 128, cv, prev), rows(SB, 128, cv, meta)],
        [rows(qb, 512, 0, pair), pl.BlockSpec((8, qb), lambda i: (0, i))],
        [jax.ShapeDtypeStruct((r_tot, 512), BF16), jax.ShapeDtypeStruct((8, r_tot), F32)],
        [], _cp(32), (sinks, proj, proj, proj, proj, proj, proj, proj), comm)


def _swa_bwd(proj, sinks, lse_t, do, t, comm=None):
    nsb = t // SB
    r_tot = t + TM
    nbq = _swa_steps(r_tot)
    qb = nbq * SB
    rows, pair, prev, meta = _swa_specs(nsb, nbq)
    hb = HB_BWD
    lanes = hb * SB

    def body(sink_ref, q_ref, kc_ref, kp_ref, km_ref, vc_ref, vp_ref, vm_ref, lse_ref, do_ref,
             dq_ref, dk_ref, dv_ref, dsink_ref):
        i = pl.program_id(0)

        @pl.when(i == 0)
        def _():
            dk_ref[...] = jnp.zeros_like(dk_ref)
            dv_ref[...] = jnp.zeros_like(dv_ref)
            dsink_ref[...] = jnp.zeros_like(dsink_ref)

        key = lax.broadcasted_iota(jnp.int32, (SB, lanes), 0)
        qry = lax.rem(lax.broadcasted_iota(jnp.int32, (SB, lanes), 1), SB)
        km, vm = km_ref[0:NM, :], vm_ref[0:NM, :]
        dsink_l = []
        for j in range(nbq):
            b = nbq * i + j
            rs = slice(j * SB, (j + 1) * SB)
            before = slice((j - 1) * SB, j * SB)
            real = b < nsb
            masks = (key <= qry, (key > qry) & (b > 0) & real, real)
            k3 = (kc_ref[rs, :], kp_ref[...] if j == 0 else kc_ref[before, :], km)
            v3 = (vc_ref[rs, :], vp_ref[...] if j == 0 else vc_ref[before, :], vm)
            groups = list(range(0, 8, hb))
            kvs = [h0 // 4 for h0 in groups]
            qg = [_bf(_stack(q_ref[rs, h0 * 64:(h0 + hb) * 64], hb)) for h0 in groups]
            dog = [_bf(_stack(do_ref[rs, h0 * 64:(h0 + hb) * 64], hb)) for h0 in groups]
            kb = [[_bf(k[:, kv * 64:(kv + 1) * 64]) for k in k3] for kv in kvs]
            vb = [[_bf(v[:, kv * 64:(kv + 1) * 64]) for v in v3] for kv in kvs]
            s = [[_dot_nt(k, qg[g]) for k in kb[g]] for g in range(len(groups))]
            dp = [[_dot_nt(v, dog[g]) for v in vb[g]] for g in range(len(groups))]
            p, ds, ds_blk = [], [], []
            for g, h0 in enumerate(groups):
                lse_row = jnp.concatenate([lse_ref[h:h + 1, rs] for h in range(h0, h0 + hb)], axis=1)
                sink_row = jnp.concatenate([jnp.full((1, SB), sink_ref[0, h], F32) for h in range(h0, h0 + hb)], axis=1)
                pg = [jnp.exp(jnp.where(m, sx, NEG) - lse_row) for m, sx in zip(masks, s[g])]
                delta = (jnp.sum(pg[0] * dp[g][0] + pg[1] * dp[g][1], axis=0, keepdims=True)
                         + jnp.sum(pg[2] * dp[g][2], axis=0, keepdims=True))
                ds.append([_bf(pp * (dd - delta)) for pp, dd in zip(pg, dp[g])])
                p.append([_bf(pp) for pp in pg])
                ds_row = -jnp.exp(sink_row - lse_row) * delta
                ds_blk += [jnp.sum(ds_row[:, q0 * SB:(q0 + 1) * SB], axis=1, keepdims=True) for q0 in range(hb)]
            dsink_l.append(jnp.concatenate(ds_blk, axis=1))
            dq_t = [_dot_tn(kb[g][0], ds[g][0]) + _dot_tn(kb[g][1], ds[g][1]) + _dot_tn(kb[g][2], ds[g][2])
                    for g in range(len(groups))]
            dq_ref[rs, :] = jnp.concatenate([_unstack(d.T, hb) for d in dq_t], axis=1)
            windows = (pl.ds(pl.multiple_of(b * SB, SB), SB), pl.ds(pl.multiple_of(jnp.maximum(b - 1, 0) * SB, SB), SB),
                       pl.ds(t, NM))
            for x in range(3):
                dk_kv, dv_kv = [], []
                for kv in range(2):
                    mine = [g for g in range(len(groups)) if kvs[g] == kv]
                    dk_kv.append(sum(_dot(ds[g][x], qg[g]) for g in mine))
                    dv_kv.append(sum(_dot(p[g][x], dog[g]) for g in mine))
                dk_ref[windows[x], :] += jnp.concatenate(dk_kv, axis=1)
                dv_ref[windows[x], :] += jnp.concatenate(dv_kv, axis=1)
        dsink_ref[...] += sum(dsink_l)

    ck, cv = C_SK // 128, C_SV // 128
    whole = lambda w: pl.BlockSpec((r_tot, w), lambda i: (0, 0))
    return _call(
        body, "swa_bwd", (r_tot // qb,),
        [SMEM_SPEC, rows(qb, 512, C_SQ // 512, pair),
         rows(qb, 128, ck, pair), rows(SB, 128, ck, prev), rows(SB, 128, ck, meta),
         rows(qb, 128, cv, pair), rows(SB, 128, cv, prev), rows(SB, 128, cv, meta),
         pl.BlockSpec((8, qb), lambda i: (0, i)), rows(qb, 512, 1, pair)],
        [rows(qb, 512, 0, pair), whole(128), whole(128), pl.BlockSpec((1, 8), lambda i: (0, 0))],
        [jax.ShapeDtypeStruct((r_tot, 512), F32), jax.ShapeDtypeStruct((r_tot, 128), F32),
         jax.ShapeDtypeStruct((r_tot, 128), F32), jax.ShapeDtypeStruct((1, 8), F32)],
        [], _cp(48), (sinks, proj, proj, proj, proj, proj, proj, proj, lse_t, do), comm)


HK = D // 2
MLP_ROWS = 512


def _mlp_fwd(x, tgt, ogla, oswa, wo, wff, w1, w2, wfin):
    t = x.shape[0]
    tm = MLP_ROWS if t % MLP_ROWS == 0 else TM

    def body(x_ref, tgt_ref, og_ref, os_ref, wo_ref, wff_ref, w1a_ref, w1b_ref, w2a_ref, w2b_ref, wfin_ref,
             h1_ref, f_ref, a_ref, dh2_ref, loss_ref, gfin_ref):
        i = pl.program_id(0)

        @pl.when(i == 0)
        def _():
            loss_ref[...] = jnp.zeros_like(loss_ref)
            gfin_ref[...] = jnp.zeros_like(gfin_ref)

        h1 = x_ref[...] + _dot(og_ref[...], wo_ref[0:512, :]) + _dot(os_ref[...], wo_ref[512:1024, :])
        h1_ref[...] = h1
        fh, _ = _rms(h1)
        f = _bf(fh * wff_ref[...])
        f_ref[...] = f
        acc = jnp.zeros((tm, D), F32)
        for n in range(4):
            a = _dot(f[:, 0:HK], w1a_ref[n]) + _dot(f[:, HK:D], w1b_ref[n])
            a_ref[:, n * D:(n + 1) * D] = _bf(a)
            zr = jnp.maximum(a, 0.0)
            z = _bf(zr * zr)
            acc = acc + _dot(z[:, 0:HK], w2a_ref[n]) + _dot(z[:, HK:D], w2b_ref[n])
        h2 = h1 + acc
        yh, rs2 = _rms(h2)
        wf = wfin_ref[...]
        e = yh * wf - tgt_ref[...]
        loss_ref[...] += jnp.sum(jnp.sum(e * e, axis=0, keepdims=True), axis=1, keepdims=True) * (0.5 / D)
        dy = e * (1.0 / D)
        gfin_ref[...] += jnp.sum(dy * yh, axis=0, keepdims=True)
        dh2_ref[...] = _rms_bwd(dy, yh, rs2, wf)

    rs = lambda w: pl.BlockSpec((tm, w), lambda i: (i, 0))
    return pl.pallas_call(
        body, name="mlp_fwd", grid=(t // tm,),
        in_specs=[rs(D), rs(D), rs(512), rs(512)] + [VMEM_SPEC] * 7,
        out_specs=[rs(D), rs(D), rs(DFF), rs(D), pl.BlockSpec((1, 1), lambda i: (0, 0)), pl.BlockSpec((1, D), lambda i: (0, 0))],
        out_shape=[jax.ShapeDtypeStruct((t, D), F32), jax.ShapeDtypeStruct((t, D), BF16),
                   jax.ShapeDtypeStruct((t, DFF), BF16), jax.ShapeDtypeStruct((t, D), F32),
                   jax.ShapeDtypeStruct((1, 1), F32), jax.ShapeDtypeStruct((1, D), F32)],
        compiler_params=_cp(60),
    )(x, tgt, ogla, oswa, wo, wff, *w1, *w2, wfin)


def _mlp_bwd(h1, a, dh2, wo, wff, w1, w2):
    t = h1.shape[0]
    tm = MLP_ROWS if t % MLP_ROWS == 0 else TM
    nb = t // tm

    def body(h1_ref, a_ref, dh2_ref, wo_ref, wff_ref, w1a_ref, w1b_ref, w2a_ref, w2b_ref,
             da_ref, dh2b_ref, dh1_ref, do_ref, dh1b_ref, gff_ref):
        i = pl.program_id(0)

        @pl.when(i == 0)
        def _():
            gff_ref[...] = jnp.zeros_like(gff_ref)

        @pl.when(i < nb)
        def _():
            dh2 = dh2_ref[...]
            dh2b = _bf(dh2)
            dh2b_ref[...] = dh2b
            dfa = jnp.zeros((tm, HK), F32)
            dfb = jnp.zeros((tm, HK), F32)
            for n in range(4):
                dz = jnp.concatenate([_dot_nt(dh2b, w2a_ref[n]), _dot_nt(dh2b, w2b_ref[n])], axis=1)
                da = _bf(dz * (2.0 * jnp.maximum(a_ref[:, n * D:(n + 1) * D].astype(F32), 0.0)))
                da_ref[:, n * D:(n + 1) * D] = da
                dfa = dfa + _dot_nt(da, w1a_ref[n])
                dfb = dfb + _dot_nt(da, w1b_ref[n])
            df = jnp.concatenate([dfa, dfb], axis=1)
            fh, rs1 = _rms(h1_ref[...])
            gff_ref[...] += jnp.sum(df * fh, axis=0, keepdims=True)
            dh1 = dh2 + _rms_bwd(df, fh, rs1, wff_ref[...])
            dh1_ref[...] = dh1
            dh1b = _bf(dh1)
            dh1b_ref[...] = dh1b
            do_ref[...] = _dot_nt(dh1b, wo_ref[...])

        @pl.when(i == nb)
        def _():
            dh1_ref[...] = jnp.zeros_like(dh1_ref)
            do_ref[...] = jnp.zeros_like(do_ref)

    rs = lambda w: pl.BlockSpec((tm, w), lambda i: (i, 0))
    real = lambda w: pl.BlockSpec((tm, w), lambda i: (jnp.minimum(i, nb - 1), 0))
    return pl.pallas_call(
        body, name="mlp_bwd", grid=(nb + 1,),
        in_specs=[real(D), real(DFF), real(D)] + [VMEM_SPEC] * 6,
        out_specs=[real(DFF), real(D), rs(D), rs(D), real(D), pl.BlockSpec((1, D), lambda i: (0, 0))],
        out_shape=[jax.ShapeDtypeStruct((t, DFF), BF16), jax.ShapeDtypeStruct((t, D), BF16),
                   jax.ShapeDtypeStruct((t + tm, D), F32), jax.ShapeDtypeStruct((t + tm, D), F32),
                   jax.ShapeDtypeStruct((t, D), BF16), jax.ShapeDtypeStruct((1, D), F32)],
        compiler_params=_cp(60),
    )(h1, a, dh2, wo, wff, *w1, *w2)


def _wo_grad(ogla, oswa, dh1b):
    t = dh1b.shape[0]
    kt = 1024 if t % 1024 == 0 else TM
    nk = t // kt

    def body(og_ref, os_ref, d_ref, dwo_ref, acc):
        k = pl.program_id(0)

        @pl.when(k == 0)
        def _():
            acc[...] = jnp.zeros_like(acc)

        d = d_ref[...]
        acc[0:512, :] += _dot_tn(og_ref[...], d)
        acc[512:1024, :] += _dot_tn(os_ref[...], d)

        @pl.when(k == nk - 1)
        def _():
            for s in range(4):
                for hh in range(2):
                    dwo_ref[hh, s] = acc[(2 * s + hh) * 128:(2 * s + hh + 1) * 128, :]

    rs = lambda w: pl.BlockSpec((kt, w), lambda k: (k, 0))
    return pl.pallas_call(
        body, name="wo_grad", grid=(nk,),
        in_specs=[rs(512), rs(512), rs(D)], out_specs=VMEM_SPEC,
        out_shape=jax.ShapeDtypeStruct((2, 4, 128, D), F32),
        scratch_shapes=[pltpu.VMEM((D, D), F32)], compiler_params=_cp(32),
    )(ogla, oswa, dh1b)


def _ffn_wgrad(f, a, da, dh2b):
    rows = f.shape[0]
    kt = 1024 if rows % 1024 == 0 else TM
    nk = rows // kt

    def body(f_ref, a_ref, da_ref, dh2_ref, dw1_ref, dw2_ref, acc1, acc2):
        k = pl.program_id(1)

        @pl.when(k == 0)
        def _():
            acc1[...] = jnp.zeros_like(acc1)
            acc2[...] = jnp.zeros_like(acc2)

        zr = jnp.maximum(a_ref[...], 0.0)
        acc1[...] += _dot_tn(f_ref[...], da_ref[...])
        acc2[...] += _dot_tn(zr * zr, dh2_ref[...])

        @pl.when(k == nk - 1)
        def _():
            for hh in range(2):
                dw1_ref[hh, 0] = acc1[hh * 512:(hh + 1) * 512, :]
                dw2_ref[hh, 0] = acc2[hh * 512:(hh + 1) * 512, :]

    out = pl.BlockSpec((2, 1, 512, D), lambda n, k: (0, n, 0, 0))
    return pl.pallas_call(
        body, name="ffn_wgrad", grid=(4, nk),
        in_specs=[pl.BlockSpec((kt, D), lambda n, k: (k, 0)), pl.BlockSpec((kt, D), lambda n, k: (k, n)),
                  pl.BlockSpec((kt, D), lambda n, k: (k, n)), pl.BlockSpec((kt, D), lambda n, k: (k, 0))],
        out_specs=[out, out],
        out_shape=[jax.ShapeDtypeStruct((2, 4, 512, D), F32)] * 2,
        scratch_shapes=[pltpu.VMEM((D, D), F32), pltpu.VMEM((D, D), F32)],
        compiler_params=_cp(48, ("arbitrary", "arbitrary")),
    )(f, a, da, dh2b)


def _proj_bwd(x, metapad, wm, wt3, tabs, dgla, dswa_q, dsk, dsv, dlr, dh1, comm=None):
    t = x.shape[0]
    nblk = t // TM

    def body(x_ref, mp_ref, wm_ref, w3_ref, tab_ref, dg_ref, dq_ref, dk_ref, dv_ref, dlr_ref, dh1_ref,
             gx_ref, gmeta_ref, dw_ref, gmix_ref, w_ref, acc):
        i = pl.program_id(0)

        @pl.when(i == 0)
        def _():
            _join_shards(w3_ref, w_ref)
            acc[...] = jnp.zeros_like(acc)
            gmix_ref[...] = jnp.zeros_like(gmix_ref)

        h = jnp.where(i == nblk, mp_ref[...], x_ref[...])
        uh, rs = _rms(h)
        wm_v = wm_ref[...]
        u = _bf(uh * wm_v)
        tab = tab_ref[...]
        dq = _bf(_rope(dq_ref[...] * 0.125, tab, -1.0))
        dk = _bf(_rope(dk_ref[...], tab, -1.0))
        parts = ((dg_ref[...], 0, R_LR), (dlr_ref[:, 0:16], R_LR, 16), (dq, R_LR + 16, 512),
                 (dk, R_LR + 528, 128), (_bf(dv_ref[...]), R_LR + 656, 128))
        du = jnp.zeros((TM, D), F32)
        for val, r0, w in parts:
            du = du + _dot(val, w_ref[r0:r0 + w, :])
            acc[r0:r0 + w, :] += _dot_tn(val, u)
        gmix_ref[...] += jnp.sum(du * uh, axis=0, keepdims=True)
        dh0 = dh1_ref[...] + _rms_bwd(du, uh, rs, wm_v)

        @pl.when(i < nblk)
        def _():
            gx_ref[...] = dh0

        @pl.when(i == nblk)
        def _():
            gmeta_ref[...] = dh0[:NM]
            for s in range(4):
                dw_ref[s] = acc[(DIN // 4) * s:(DIN // 4) * (s + 1), :]

    xs = pl.BlockSpec((TM, D), lambda i: (jnp.minimum(i, nblk - 1), 0))
    rs_ = lambda w: pl.BlockSpec((TM, w), lambda i: (i, 0))
    return _call(
        body, "proj_bwd", (nblk + 1,),
        [xs, VMEM_SPEC, VMEM_SPEC, VMEM_SPEC, rs_(128), rs_(1536), rs_(512), rs_(128), rs_(128), rs_(128), rs_(D)],
        [xs, pl.BlockSpec((NM, D), lambda i: (0, 0)), VMEM_SPEC, pl.BlockSpec((1, D), lambda i: (0, 0))],
        [jax.ShapeDtypeStruct((t, D), F32), jax.ShapeDtypeStruct((NM, D), F32),
         jax.ShapeDtypeStruct((4, DIN // 4, D), F32), jax.ShapeDtypeStruct((1, D), F32)],
        [pltpu.VMEM((DIN, D), BF16), pltpu.VMEM((DIN, D), F32)], _cp(56),
        (x, metapad, wm, wt3, tabs, dgla, dswa_q, dsk, dsv, dlr, dh1), comm)


def _place():
    return lax.axis_index("x"), lax.axis_index("y"), lax.axis_index("c")


def _other_chips(x, y):
    return [(1 - x, y), (x, 1 - y), (1 - x, 1 - y)]


def _dma_sems(*counts):
    return tuple(pltpu.SemaphoreType.DMA((k,)) for k in counts)


def _gather_shards(shards, split):
    n = len(shards)
    two = [a for a in range(n) if split[a]]

    def plan(ins, outs, sems):
        isend, irecv, dsend, drecv, loc = sems
        x, y, c = _place()
        chips = _other_chips(x, y)

        def part(ref, a, half):
            if not split[a]:
                return ref
            w = shards[a].shape[1] // 2
            return ref.at[:, pl.ds(pl.multiple_of(half * w, 128), w)]

        def over_ici(a, k, shard_of):
            tx, ty = chips[k]
            sx, sy = shard_of
            return pltpu.make_async_remote_copy(
                src_ref=part(ins[a], a, c), dst_ref=part(outs[a].at[2 * sx + sy], a, c), send_sem=isend.at[3 * a + k],
                recv_sem=irecv.at[3 * a + k], device_id=(tx, ty, c), device_id_type=MESH)

        def over_d2d(a, k, half):
            tx, ty = chips[k]
            ref = part(outs[a].at[2 * tx + ty], a, half)
            return pltpu.make_async_remote_copy(
                src_ref=ref, dst_ref=ref, send_sem=dsend.at[3 * a + k], recv_sem=drecv.at[3 * a + k],
                device_id=(x, y, 1 - c), device_id_type=MESH)

        def local(a):
            return pltpu.make_async_copy(ins[a], outs[a].at[2 * x + y], loc.at[a])

        pairs = [(a, k) for a in range(n) for k in range(3)]
        first = ([lambda a=a: local(a).start() for a in range(n)]
                 + [lambda a=a, k=k: over_ici(a, k, (x, y)).start() for a, k in pairs],
                 [lambda a=a, k=k: over_ici(a, k, chips[k]).wait_recv() for a, k in pairs]
                 + [lambda a=a, k=k: over_ici(a, k, (x, y)).wait_send() for a, k in pairs]
                 + [lambda a=a: local(a).wait() for a in range(n)])
        pairs2 = [(a, k) for a in two for k in range(3)]
        second = ([lambda a=a, k=k: over_d2d(a, k, c).start() for a, k in pairs2],
                  [lambda a=a, k=k: over_d2d(a, k, 1 - c).wait_recv() for a, k in pairs2]
                  + [lambda a=a, k=k: over_d2d(a, k, c).wait_send() for a, k in pairs2])
        return [first, second] if two else [first]

    return _Comm(tuple(shards), tuple(jax.ShapeDtypeStruct((4,) + s.shape, s.dtype) for s in shards),
                 _dma_sems(3 * n, 3 * n, 3 * n, 3 * n, n), 2 if two else 1, plan)


def _swap_halves(grads):
    n = len(grads)

    def plan(ins, outs, sems):
        send, recv = sems
        x, y, c = _place()

        def swap(a):
            return pltpu.make_async_remote_copy(
                src_ref=ins[a].at[1 - c], dst_ref=outs[a], send_sem=send.at[a], recv_sem=recv.at[a],
                device_id=(x, y, 1 - c), device_id_type=MESH)

        return [([lambda a=a: swap(a).start() for a in range(n)], [lambda a=a: swap(a).wait() for a in range(n)])]

    return _Comm(tuple(grads), tuple(jax.ShapeDtypeStruct(g.shape[1:], g.dtype) for g in grads), _dma_sems(n, n), 1, plan)


SCATTER_ADD_ROWS = 128


def _scatter_shards(parts, late):
    n = len(parts)

    def plan(ins, outs, sems):
        send, recv, loc = sems[:3]
        onward, got = sems[3:3 + n], sems[3 + n:]
        x, y, c = _place()
        x_first = c == 0
        near = (jnp.where(x_first, 1 - x, x), jnp.where(x_first, y, 1 - y))
        far = (jnp.where(x_first, x, 1 - x), jnp.where(x_first, 1 - y, y))
        diagonal = (1 - x, 1 - y)
        shard = lambda chip: 2 * chip[0] + chip[1]

        def hop(src, dst, k, chip):
            return pltpu.make_async_remote_copy(src_ref=src, dst_ref=dst, send_sem=send.at[k], recv_sem=recv.at[k],
                                                device_id=(chip[0], chip[1], c), device_id_type=MESH)

        theirs = lambda a: hop(ins[a].at[shard(near)], outs[a].at[0], 3 * a, near)
        passing = lambda a: hop(ins[a].at[shard(diagonal)], got[a], 3 * a + 1, near)
        summed = lambda a: hop(onward[a], outs[a].at[1], 3 * a + 2, far)
        mine = lambda a: pltpu.make_async_copy(ins[a].at[shard(far)], onward[a], loc.at[a])

        def add(a):
            for r in range(0, parts[a].shape[1], SCATTER_ADD_ROWS):
                rows = slice(r, r + SCATTER_ADD_ROWS)
                onward[a][rows, :] = _bf(onward[a][rows, :].astype(F32) + got[a][rows, :].astype(F32))

        every = range(n)
        first = ([lambda a=a: mine(a).start() for a in every] + [lambda a=a: passing(a).start() for a in every]
                 + [lambda a=a: theirs(a).start() for a in every],
                 [lambda a=a: mine(a).wait() for a in every] + [lambda a=a: passing(a).wait_recv() for a in every]
                 + [lambda a=a: add(a) for a in every])
        second = ([lambda a=a: summed(a).start() for a in every],
                  [lambda a=a: passing(a).wait_send() for a in every] + [lambda a=a: theirs(a).wait() for a in every]
                  + [lambda a=a: summed(a).wait() for a in every])
        return [first, second]

    assert all(p.shape[1] % SCATTER_ADD_ROWS == 0 for p in parts)
    buffers = [pltpu.VMEM(p.shape[1:], p.dtype) for p in parts]
    return _Comm(tuple(parts), tuple(jax.ShapeDtypeStruct((2,) + p.shape[1:], p.dtype) for p in parts),
                 _dma_sems(3 * n, 3 * n, n) + tuple(buffers) * 2, 2, plan, late)


def _join_halves(halves):
    n = len(halves)

    def plan(ins, outs, sems):
        send, recv, loc = sems
        x, y, c = _place()

        def remote(a, half):
            return pltpu.make_async_remote_copy(
                src_ref=ins[a], dst_ref=outs[a].at[half], send_sem=send.at[a], recv_sem=recv.at[a],
                device_id=(x, y, 1 - c), device_id_type=MESH)

        def local(a):
            return pltpu.make_async_copy(ins[a], outs[a].at[c], loc.at[a])

        every = range(n)
        return [([lambda a=a: local(a).start() for a in every] + [lambda a=a: remote(a, c).start() for a in every],
                 [lambda a=a: remote(a, 1 - c).wait_recv() for a in every]
                 + [lambda a=a: remote(a, c).wait_send() for a in every] + [lambda a=a: local(a).wait() for a in every])]

    return _Comm(tuple(halves), tuple(jax.ShapeDtypeStruct((2,) + h.shape, h.dtype) for h in halves),
                 _dma_sems(n, n, n), 1, plan)


def _reduce_w_in(dwt, comm):
    rows, hw = DIN // 4, D // 2
    ci, co = len(comm.ins), len(comm.outs)

    def body(*refs):
        dw_ref, c_in, out_ref, c_out = refs[0], refs[1:1 + ci], refs[1 + ci], refs[2 + ci:2 + ci + co]
        mine, sib, tosend, rbuf, qbuf, full, send, recv, loc = refs[2 + ci + co:11 + ci + co]
        c_sem = refs[11 + ci + co:]
        x, y, c = _place()
        sibling = (x, y, 1 - c)
        (starts, waits), = comm.plan(c_in, c_out, c_sem)
        _run_phase(starts)

        def cols(ref, half):
            window = pl.ds(pl.multiple_of(half * hw, 128), hw)
            return ref.at[:, :, window] if len(ref.shape) == 3 else ref.at[:, window]

        load = pltpu.make_async_copy(cols(dw_ref, c), mine, loc.at[0])
        give = pltpu.make_async_remote_copy(src_ref=cols(dw_ref, 1 - c), dst_ref=sib, send_sem=send.at[3], recv_sem=recv.at[3],
                                            device_id=sibling, device_id_type=MESH)
        load.start()
        give.start()
        load.wait()
        give.wait()
        mine[...] = mine[...] + sib[...]
        cps = []
        for k, (tx, ty) in enumerate(_other_chips(x, y)):
            tosend[k] = _bf(mine[2 * tx + ty])
            cps.append(pltpu.make_async_remote_copy(
                src_ref=tosend.at[k], dst_ref=rbuf.at[k], send_sem=send.at[k], recv_sem=recv.at[k],
                device_id=(tx, ty, c), device_id_type=MESH))
            cps[-1].start()
        for cp in cps:
            cp.wait()
        qbuf[...] = mine[2 * x + y] + rbuf[0].astype(F32) + rbuf[1].astype(F32) + rbuf[2].astype(F32)
        keep = pltpu.make_async_copy(qbuf, cols(full, c), loc.at[1])
        pass_on = pltpu.make_async_remote_copy(src_ref=qbuf, dst_ref=cols(full, c), send_sem=send.at[4], recv_sem=recv.at[4],
                                               device_id=sibling, device_id_type=MESH)
        keep.start()
        pass_on.start()
        keep.wait()
        pass_on.wait_send()
        pltpu.make_async_remote_copy(src_ref=qbuf, dst_ref=cols(full, 1 - c), send_sem=send.at[4], recv_sem=recv.at[4],
                                     device_id=sibling, device_id_type=MESH).wait_recv()
        out_ref[...] = full[...]
        _run_phase(waits)

    outs = pl.pallas_call(
        body, name="reduce_w_in",
        in_specs=[ANY_SPEC] * (1 + ci), out_specs=[VMEM_SPEC] + [ANY_SPEC] * co,
        out_shape=[jax.ShapeDtypeStruct((rows, D), F32)] + list(comm.outs),
        scratch_shapes=[pltpu.VMEM((4, rows, hw), F32), pltpu.VMEM((4, rows, hw), F32), pltpu.VMEM((3, rows, hw), BF16),
                        pltpu.VMEM((3, rows, hw), BF16), pltpu.VMEM((rows, hw), F32), pltpu.VMEM((rows, D), F32),
                        *_dma_sems(5, 5, 2), *comm.sems],
        compiler_params=pltpu.CompilerParams(vmem_limit_bytes=48 << 20),
    )(dwt, *comm.ins)
    return outs[0], outs[1:]


def _allreduce_small(pack):
    p = pack.shape[0]

    def body(in_ref, out_ref, buf, send, recv):
        x, y, c = _place()
        me = 4 * x + 2 * y + c
        buf[me] = in_ref[...]

        def peer_of(k):
            return x ^ (k >> 2), y ^ ((k >> 1) & 1), c ^ (k & 1)

        sends = [pltpu.make_async_remote_copy(
            src_ref=in_ref, dst_ref=buf.at[me], send_sem=send.at[k - 1], recv_sem=recv.at[k - 1],
            device_id=peer_of(k), device_id_type=MESH) for k in range(1, 8)]
        for cp in sends:
            cp.start()
        for k in range(1, 8):
            px, py, pc = peer_of(k)
            pltpu.make_async_remote_copy(
                src_ref=in_ref, dst_ref=buf.at[4 * px + 2 * py + pc], send_sem=send.at[k - 1], recv_sem=recv.at[k - 1],
                device_id=(x, y, c), device_id_type=MESH).wait_recv()
        for cp in sends:
            cp.wait_send()
        acc = buf[0]
        for d in range(1, 8):
            acc = acc + buf[d]
        out_ref[...] = acc

    return pl.pallas_call(
        body, name="allreduce_small",
        in_specs=[VMEM_SPEC], out_specs=VMEM_SPEC, out_shape=jax.ShapeDtypeStruct(pack.shape, F32),
        scratch_shapes=[pltpu.VMEM((8, p, D), F32), *_dma_sems(7, 7)],
    )(pack)


GRID4 = 4


def _sum_cores(core_shard, mine, theirs):
    n = len(mine)

    def body(cs_ref, *refs):
        ms, ts, bfs, owns = refs[:n], refs[n:2 * n], refs[2 * n:3 * n], refs[3 * n:]
        keep = pl.program_id(0) == cs_ref[1]
        for a in range(n):
            acc = ms[a][0, 0] + ts[a][0]
            bfs[a][0] = _bf(acc)

            @pl.when(keep)
            def _():
                owns[a][...] = acc

    shapes = [m.shape[2:] for m in mine]
    in_specs = ([pl.BlockSpec((1, 1) + s, lambda i, cs: (cs[0], i, 0, 0)) for s in shapes]
                + [pl.BlockSpec((1,) + s, lambda i, cs: (i, 0, 0)) for s in shapes])
    out_specs = ([pl.BlockSpec((1,) + s, lambda i, cs: (i, 0, 0)) for s in shapes]
                 + [pl.BlockSpec(s, lambda i, cs: (0, 0)) for s in shapes])
    outs = pl.pallas_call(
        body, name="sum_cores",
        grid_spec=pltpu.PrefetchScalarGridSpec(num_scalar_prefetch=1, grid=(4,), in_specs=in_specs, out_specs=out_specs),
        out_shape=[jax.ShapeDtypeStruct((4,) + s, BF16) for s in shapes] + [jax.ShapeDtypeStruct(s, F32) for s in shapes],
        compiler_params=_cp(48),
    )(core_shard, *mine, *theirs)
    return outs[:n], outs[n:]


def _sum_chips(own, arrived):
    n = len(own)

    def body(*refs):
        os_, ars, outs = refs[:n], refs[n:2 * n], refs[2 * n:]
        for a in range(n):
            outs[a][...] = os_[a][...] + ars[a][0].astype(F32) + ars[a][1].astype(F32)

    blocks = [(o.shape[0] // GRID4, o.shape[1]) for o in own]
    return pl.pallas_call(
        body, name="sum_chips", grid=(GRID4,),
        in_specs=([pl.BlockSpec(b, lambda i: (i, 0)) for b in blocks]
                  + [pl.BlockSpec((2,) + b, lambda i: (0, i, 0)) for b in blocks]),
        out_specs=[pl.BlockSpec(b, lambda i: (i, 0)) for b in blocks],
        out_shape=[jax.ShapeDtypeStruct(o.shape, F32) for o in own],
        compiler_params=_cp(32),
    )(*own, *arrived)


def _adamw_math(w, g, m, v):
    m2 = ADAM_B1 * m + (1.0 - ADAM_B1) * g
    v2 = ADAM_B2 * v + (1.0 - ADAM_B2) * (g * g)
    m_hat = m2 / (1.0 - ADAM_B1 ** ADAM_STEP)
    v_hat = v2 / (1.0 - ADAM_B2 ** ADAM_STEP)
    return -ADAM_LR * (m_hat / (jnp.sqrt(v_hat) + ADAM_EPS) + ADAM_WD * w), m2, v2


def _adamw_big(ws, gs, ms, vs):
    n = len(ws)

    def body(*refs):
        for a in range(n):
            d, m2, v2 = _adamw_math(refs[a][...], refs[n + a][...], refs[2 * n + a][...], refs[3 * n + a][...])
            refs[4 * n + a][...] = d
            refs[5 * n + a][...] = m2
            refs[6 * n + a][...] = v2

    specs = [pl.BlockSpec((w.shape[0] // GRID4, w.shape[1]), lambda i: (i, 0)) for w in ws]
    return pl.pallas_call(
        body, name="adamw_big", grid=(GRID4,),
        in_specs=specs * 4, out_specs=specs * 3,
        out_shape=[jax.ShapeDtypeStruct(w.shape, F32) for w in ws] * 3,
        compiler_params=_cp(48),
    )(*ws, *gs, *ms, *vs)


def _adamw_small(ws, gs, ms, vs):
    n = len(ws)

    def body(*refs):
        for a in range(n):
            d, m2, v2 = _adamw_math(refs[a][...], refs[n + a][...], refs[2 * n + a][...], refs[3 * n + a][...])
            refs[4 * n + a][...] = d
            refs[5 * n + a][...] = m2
            refs[6 * n + a][...] = v2

    return pl.pallas_call(
        body, name="adamw_small",
        in_specs=[VMEM_SPEC] * (4 * n), out_specs=[VMEM_SPEC] * (3 * n),
        out_shape=[jax.ShapeDtypeStruct(w.shape, F32) for w in ws] * 3,
        compiler_params=pltpu.CompilerParams(vmem_limit_bytes=40 << 20),
    )(*ws, *gs, *ms, *vs)


def kernel(x, meta_tokens, norm_mix_w, w_in, w_gate_up, b_gate, gla_norm_w, sinks, w_out, norm_ff_w, w_ff1, w_ff2, final_norm_w, loss_target, m_meta_tokens, m_norm_mix_w, m_w_in, m_w_gate_up, m_b_gate, m_gla_norm_w, m_sinks, m_w_out, m_norm_ff_w, m_w_ff1, m_w_ff2, m_final_norm_w, v_meta_tokens, v_norm_mix_w, v_w_in, v_w_gate_up, v_b_gate, v_gla_norm_w, v_sinks, v_w_out, v_norm_ff_w, v_w_ff1, v_w_ff2, v_final_norm_w):
    xi, yi, ci = _place()
    shard = (2 * xi + yi).astype(jnp.int32).reshape(1)
    core = ci.astype(jnp.int32).reshape(1)

    small = jnp.concatenate([meta_tokens, w_gate_up[0], jnp.zeros((NM, 64), F32)], axis=1)
    wt3, g_small = _run_comm(_gather_shards([_bf(w_in[0].T), small], [True, False]), "gather_w_in")
    meta = g_small[:, :, 0:256].transpose(1, 0, 2).reshape(NM, D)
    wgu = g_small[:, :, 256:320].transpose(1, 0, 2).reshape(NM, 256)

    xs, tgt = x[0], loss_target[0]
    t = xs.shape[0]
    wfin = final_norm_w.reshape(1, D)
    metapad = jnp.concatenate([meta, jnp.zeros((TM - NM, D), F32)], axis=0)
    wgu_p = _bf(jnp.concatenate([wgu, jnp.zeros((128 - 16, 256), F32)], axis=0))
    tabs = _rope_tables(t)

    w1s, w2s = _bf(w_ff1[0]), _bf(w_ff2[0])
    proj, (g_out, w1a) = _proj_fwd(xs, metapad, norm_mix_w, wt3, tabs,
                                   _gather_shards([_bf(w_out[0]), w1s[:HK]], [True] * 2))
    (ogla, oraw, sst, bcum, dgate), (w1b, w2a) = _gla_fwd(proj, wgu_p, b_gate, gla_norm_w, t,
                                                          _gather_shards([w1s[HK:], w2s[:HK]], [True] * 2))
    (oswa, lse), (w2b,) = _swa_fwd(proj, sinks, t, _gather_shards([w2s[HK:]], [True]))
    wo, w1, w2 = g_out.reshape(D, D), (w1a, w1b), (w2a, w2b)
    h1, f, a, dh2, loss, gfin = _mlp_fwd(xs, tgt, ogla, oswa, wo, norm_ff_w, w1, w2, wfin)

    da, dh2b, dh1, do, dh1b, gff = _mlp_bwd(h1, a, dh2, wo, norm_ff_w, w1, w2)
    dwo = _wo_grad(ogla, oswa, dh1b)
    dw1, dw2 = _ffn_wgrad(f, a, da, dh2b)
    big = [dwo, dw1, dw2]
    (dgla, dlr, dwgu, dbg, dgnw), theirs = _gla_bwd(proj, oraw, sst, bcum, dgate, do, wgu_p, gla_norm_w, t,
                                                    _swap_halves(big))
    sums_bf, own = _sum_cores(jnp.concatenate([core, shard]), big, theirs)
    swa_grid = (t + TM) // SB // _swa_steps(t + TM)
    (dsq, dsk, dsv, dsink), arrived = _swa_bwd(proj, sinks, lse, do, t, _scatter_shards(sums_bf, swa_grid // 3))
    halves = _sum_chips(own, arrived)
    (gx, gmeta, dwt, gmix), _ = _proj_bwd(xs, metapad, norm_mix_w, wt3, tabs, dgla, dsq, dsk, dsv, dlr, dh1)

    gwt_in, joined = _reduce_w_in(dwt, _join_halves(halves))
    gw_out, gw_1, gw_2 = [j.reshape((-1, j.shape[2])) for j in joined]

    tail = jnp.concatenate([dbg, dgnw, dsink, loss, jnp.zeros((1, D - 256 - 128 - 8 - 1), F32)], axis=1)
    pack = jnp.concatenate([gmeta, gmix, gff, gfin, tail, dwgu[:16].reshape(4, D)], axis=0)
    tot = _allreduce_small(pack)
    g_meta = lax.dynamic_slice_in_dim(tot[0:NM], shard[0] * 256, 256, axis=1)
    g_mix, g_ff, g_fin = tot[16:17], tot[17:18], tot[18]
    g_bg, g_gnw, g_sinks, loss_tot = tot[19:20, 0:256], tot[19:20, 256:384], tot[19:20, 384:392], tot[19, 392]
    g_wgu = lax.dynamic_slice_in_dim(tot[20:24].reshape(NM, 256), shard[0] * 64, 64, axis=1)

    bo = _adamw_big([w_out[0], w_ff1[0], w_ff2[0]], [gw_out, gw_1, gw_2], [m_w_out[0], m_w_ff1[0], m_w_ff2[0]],
                    [v_w_out[0], v_w_ff1[0], v_w_ff2[0]])

    fin2 = lambda a: a.reshape(1, D)
    sw = [meta_tokens, norm_mix_w, w_gate_up[0], b_gate, gla_norm_w, sinks, norm_ff_w, fin2(final_norm_w), w_in[0].T]
    sg = [g_meta, g_mix, g_wgu, g_bg, g_gnw, g_sinks, g_ff, fin2(g_fin), gwt_in]
    sm = [m_meta_tokens, m_norm_mix_w, m_w_gate_up[0], m_b_gate, m_gla_norm_w, m_sinks, m_norm_ff_w, fin2(m_final_norm_w),
          m_w_in[0].T]
    sv = [v_meta_tokens, v_norm_mix_w, v_w_gate_up[0], v_b_gate, v_gla_norm_w, v_sinks, v_norm_ff_w, fin2(v_final_norm_w),
          v_w_in[0].T]
    so = _adamw_small(sw, sg, sm, sv)

    def ordered(small_o, big_o):
        meta_, mix_, wgu_, bg_, gnw_, sinks_, ff_, fin_, wt_ = small_o
        w_out_, w_1_, w_2_ = big_o
        return (meta_, mix_, wt_.T[None], wgu_[None], bg_, gnw_, sinks_, w_out_[None], ff_, w_1_[None], w_2_[None],
                fin_.reshape(D))

    grads = ordered(sg, [gw_out, gw_1, gw_2])
    deltas = ordered(so[0:9], bo[0:3])
    new_m = ordered(so[9:18], bo[3:6])
    new_v = ordered(so[18:27], bo[6:9])
    return (loss_tot, gx[None], *grads, *deltas, *new_m, *new_v)
```

```python
import functools
from typing import Callable, NamedTuple

import jax
import jax.numpy as jnp
import numpy as np
from jax import lax
from jax.experimental import pallas as pl
from jax.experimental.pallas import tpu as pltpu

F32 = jnp.float32
BF16 = jnp.bfloat16

D = 1024
DFF = 4096
NM = 16
TM = 256
DK = 64
CH = 128
SB = 128
EPS = 1e-5
C_GQ, C_GK, C_GV, C_GR, C_SQ, C_SK, C_SV, C_LR, DINP = 0, 256, 512, 1024, 1536, 2048, 2176, 2304, 2432
DIN = 2320
R_LR = 1536
ROPE_THETA = 500000.0
ADAM_LR, ADAM_B1, ADAM_B2, ADAM_EPS, ADAM_WD, ADAM_STEP = 0.001, 0.9, 0.999, 1e-08, 0.01, 10
NEG = -1e30
MESH = pl.DeviceIdType.MESH
VMEM_SPEC = pl.BlockSpec(memory_space=pltpu.VMEM)
ANY_SPEC = pl.BlockSpec(memory_space=pl.ANY)
SMEM_SPEC = pl.BlockSpec(memory_space=pltpu.SMEM)


def _cp(vmem_mb, sem=("arbitrary",)):
    return pltpu.CompilerParams(dimension_semantics=sem, vmem_limit_bytes=vmem_mb << 20)


def _dot(a, b):
    return jnp.dot(a, b, preferred_element_type=F32)


def _dot_nt(a, b):
    return lax.dot_general(a, b, (((1,), (1,)), ((), ())), preferred_element_type=F32)


def _dot_tn(a, b):
    return lax.dot_general(a, b, (((0,), (0,)), ((), ())), preferred_element_type=F32)


def _bf(x):
    return x.astype(BF16)


def _dot3(m01, x):
    x1 = _bf(x)
    r1 = x - x1.astype(F32)
    x2 = _bf(r1)
    x3 = _bf(r1 - x2.astype(F32))
    return _dot(m01, x1) + _dot(m01, x2) + _dot(m01, x3)


def _rms(h):
    rs = lax.rsqrt(jnp.mean(h * h, axis=-1, keepdims=True) + EPS)
    return h * rs, rs


def _rms_bwd(dy, yhat, rs, w):
    dyh = dy * w
    return rs * (dyh - yhat * jnp.mean(dyh * yhat, axis=-1, keepdims=True))


class _Comm(NamedTuple):
    ins: tuple
    outs: tuple
    sems: tuple
    phases: int
    plan: Callable
    late: int = 0


def _run_phase(fns):
    for fn in fns:
        fn()


def _call(body, name, grid, in_specs, out_specs, out_shape, scratch, params, args, comm=None):
    if comm is None:
        outs = pl.pallas_call(body, name=name, grid=grid, in_specs=in_specs, out_specs=out_specs, out_shape=out_shape,
                              scratch_shapes=scratch, compiler_params=params)(*args)
        return outs, None
    n_in, n_out, n_scr = len(in_specs), len(out_specs), len(scratch)
    ci, co = len(comm.ins), len(comm.outs)
    last = grid[0] - 1
    marks = [0, max(1, last - (comm.late or max(2, (last + 1) // 6)))][:comm.phases]

    def wrapped(*refs):
        own_in, c_in = refs[:n_in], refs[n_in:n_in + ci]
        refs = refs[n_in + ci:]
        own_out, c_out = refs[:n_out], refs[n_out:n_out + co]
        refs = refs[n_out + co:]
        own_scr, c_sem = refs[:n_scr], refs[n_scr:]
        i = pl.program_id(0)

        for p, mark in enumerate(marks):
            @pl.when(i == mark)
            def _():
                plan = comm.plan(c_in, c_out, c_sem)
                if p > 0:
                    _run_phase(plan[p - 1][1])
                _run_phase(plan[p][0])

        body(*own_in, *own_out, *own_scr)

        @pl.when(i == last)
        def _():
            _run_phase(comm.plan(c_in, c_out, c_sem)[-1][1])

    outs = pl.pallas_call(
        wrapped, name=name, grid=grid, in_specs=list(in_specs) + [ANY_SPEC] * ci, out_specs=list(out_specs) + [ANY_SPEC] * co,
        out_shape=list(out_shape) + list(comm.outs), scratch_shapes=list(scratch) + list(comm.sems), compiler_params=params,
    )(*args, *comm.ins)
    return outs[:n_out], outs[n_out:]


def _run_comm(comm, name):
    ci, co = len(comm.ins), len(comm.outs)

    def body(*refs):
        for starts, waits in comm.plan(refs[:ci], refs[ci:ci + co], refs[ci + co:]):
            _run_phase(starts)
            _run_phase(waits)

    return pl.pallas_call(body, name=name, in_specs=[ANY_SPEC] * ci, out_specs=[ANY_SPEC] * co, out_shape=list(comm.outs),
                          scratch_shapes=list(comm.sems))(*comm.ins)


def _join_shards(w3_ref, w_ref):
    for s in range(4):
        w_ref[(DIN // 4) * s:(DIN // 4) * (s + 1), :] = w3_ref[s]


def _proj_fwd(x, metapad, wm, wt3, tabs, comm=None):
    t = x.shape[0]
    nblk = t // TM

    def body(x_ref, mp_ref, wm_ref, w3_ref, tab_ref, proj_ref, w_ref):
        i = pl.program_id(0)

        @pl.when(i == 0)
        def _():
            _join_shards(w3_ref, w_ref)

        h = jnp.where(i == nblk, mp_ref[...], x_ref[...])
        u, _ = _rms(h)
        ub = _bf(u * wm_ref[...])
        proj_ref[:, 0:C_SQ] = _dot_nt(ub, w_ref[0:R_LR, :])
        att = _dot_nt(ub, w_ref[R_LR + 16:DIN, :])
        tab = tab_ref[...]
        proj_ref[:, C_SQ:C_SK] = _rope(att[:, 0:512], tab, 1.0) * 0.125
        proj_ref[:, C_SK:C_SV] = _rope(att[:, 512:640], tab, 1.0)
        proj_ref[:, C_SV:C_LR] = att[:, 640:768]
        proj_ref[:, C_LR:DINP] = jnp.zeros((TM, DINP - C_LR), F32)
        proj_ref[:, C_LR:C_LR + 16] = _dot_nt(ub, w_ref[R_LR:R_LR + 16, :])

    (proj,), got = _call(
        body, "proj_fwd", (nblk + 1,),
        [pl.BlockSpec((TM, D), lambda i: (jnp.minimum(i, nblk - 1), 0)), VMEM_SPEC, VMEM_SPEC, VMEM_SPEC,
         pl.BlockSpec((TM, 128), lambda i: (i, 0))],
        [pl.BlockSpec((TM, DINP), lambda i: (i, 0))], [jax.ShapeDtypeStruct((t + TM, DINP), F32)],
        [pltpu.VMEM((DIN, D), BF16)], _cp(48), (x, metapad, wm, wt3, tabs), comm)
    return proj, got


def _chunk_masks():
    r = lax.broadcasted_iota(jnp.int32, (TM, TM), 0)
    c = lax.broadcasted_iota(jnp.int32, (TM, TM), 1)
    same = (r // CH) == (c // CH)
    lower = _bf(jnp.where(same & (c <= r), 1.0, 0.0))
    upper = _bf(jnp.where(same & (c >= r), 1.0, 0.0))
    return lower, upper


def _gla_gate(lr, wgu, bg, valid, lower):
    z = _dot(_bf(lr), wgu) + bg
    g = (jnp.minimum(z, 0.0) - jnp.log(1.0 + jnp.exp(-jnp.abs(z)))) * (1.0 / 16.0)
    g = jnp.where(valid, g, 0.0)
    return z, _dot3(lower, g)


def _gla_decays(q, k, b):
    nc = TM // CH
    b3 = b.reshape(nc, CH, 256)
    blast = b3[:, CH - 1:CH, :]
    eb = jnp.exp(b)
    enb = jnp.exp(-b)
    ebl = jnp.exp(blast - b3).reshape(TM, 256)
    return eb, enb, ebl, jnp.exp(blast)


def _tri(lower_incl):
    r = lax.broadcasted_iota(jnp.int32, (CH, CH), 0)
    c = lax.broadcasted_iota(jnp.int32, (CH, CH), 1)
    return ((c <= r) if lower_incl else (c >= r))[None]


def _gla_fwd(proj, wgu, bg, gnw, t, comm=None):
    nblk = t // TM
    nt = nblk + 1
    nc = TM // CH

    def blk(i):
        return (i + nblk) % nt

    def body(q_ref, k_ref, v_ref, r_ref, lr_ref, wgu_ref, bg_ref, gnw_ref, o_ref, oraw_ref, sst_ref, b_ref, dgate_ref,
             st_scr):
        i = pl.program_id(0)

        @pl.when(i == 0)
        def _():
            st_scr[...] = jnp.zeros_like(st_scr)

        rows = blk(i) * TM + lax.broadcasted_iota(jnp.int32, (TM, 1), 0)
        lower, _ = _chunk_masks()
        valid = rows < t + NM
        z, b = _gla_gate(lr_ref[...], wgu_ref[...], bg_ref[...], valid, lower)
        b_ref[...] = b
        dgate_ref[...] = jnp.where(valid, (1.0 / 16.0) / (1.0 + jnp.exp(z)), 0.0)
        q = q_ref[...]
        k = k_ref[...]
        eb, enb, ebl, eblast = _gla_decays(q, k, b)
        qt = q * 0.125 * eb
        kt = k * enb
        kh = k * ebl
        tril = _tri(True)
        heads = range(4)
        hs = [slice(h * DK, (h + 1) * DK) for h in heads]
        qh = [_bf(qt[:, hs[h]]).reshape(nc, CH, DK) for h in heads]
        kth = [_bf(kt[:, hs[h]]).reshape(nc, CH, DK) for h in heads]
        khh = [_bf(kh[:, hs[h]]).reshape(nc, CH, DK) for h in heads]
        vh = [_bf(v_ref[:, h * 128:(h + 1) * 128]).reshape(nc, CH, 128) for h in heads]
        a = [jnp.einsum('cid,cjd->cij', qh[h], kth[h], preferred_element_type=F32) for h in heads]
        kv = [jnp.einsum('cjv,cjd->cvd', vh[h], khh[h], preferred_element_type=F32) for h in heads]
        o = [jnp.einsum('cij,cjv->civ', _bf(jnp.where(tril, a[h], 0.0)), vh[h], preferred_element_type=F32) for h in heads]
        states = []
        for h in heads:
            st = st_scr[h]
            per_chunk = []
            for c in range(nc):
                sst_ref[c, h] = st
                per_chunk.append(_bf(st))
                st = st * eblast[c, :, hs[h]] + kv[h][c]
            st_scr[h] = st
            states.append(per_chunk)
        o_inter = [[_dot_nt(qh[h][c], states[h][c]) for c in range(nc)] for h in heads]
        oraw = jnp.concatenate([(o[h] + jnp.stack(o_inter[h])).reshape(TM, 128) for h in heads], axis=1)
        oraw_ref[...] = oraw
        gn = gnw_ref[...]
        res = []
        for h in range(4):
            on, _ = _rms(oraw[:, h * 128:(h + 1) * 128])
            r = r_ref[:, h * 128:(h + 1) * 128]
            res.append(on * gn * (r * jax.nn.sigmoid(r)))
        o_ref[...] = _bf(jnp.concatenate(res, axis=1))

    def spec(w, cb):
        return pl.BlockSpec((TM, w), lambda i: (blk(i), cb))

    return _call(
        body, "gla_fwd", (nt,),
        [spec(256, 0), spec(256, 1), spec(512, 1), spec(512, 2), spec(128, C_LR // 128), VMEM_SPEC, VMEM_SPEC, VMEM_SPEC],
        [spec(512, 0), spec(512, 0), pl.BlockSpec((nc, 4, 128, DK), lambda i: (blk(i), 0, 0, 0)), spec(256, 0), spec(256, 0)],
        [jax.ShapeDtypeStruct((t + TM, 512), BF16), jax.ShapeDtypeStruct((t + TM, 512), F32),
         jax.ShapeDtypeStruct((nt * nc, 4, 128, DK), F32), jax.ShapeDtypeStruct((t + TM, 256), F32),
         jax.ShapeDtypeStruct((t + TM, 256), F32)],
        [pltpu.VMEM((4, 128, DK), F32)], _cp(40), (proj, proj, proj, proj, proj, wgu, bg, gnw), comm)


def _gla_bwd(proj, oraw, sst, bcum, dgate, do, wgu, gnw, t, comm=None):
    nblk = t // TM
    nt = nblk + 1
    nc = TM // CH

    def blk(i):
        return (2 * nblk - i) % nt

    def body(q_ref, k_ref, v_ref, r_ref, lr_ref, oraw_ref, sst_ref, b_ref, dgate_ref, do_ref, wgu_ref, gnw_ref,
             dgla_ref, dlr_ref, dwgu_ref, dbg_ref, dgnw_ref, dst_scr):
        i = pl.program_id(0)

        @pl.when(i == 0)
        def _():
            dst_scr[...] = jnp.zeros_like(dst_scr)
            dwgu_ref[...] = jnp.zeros_like(dwgu_ref)
            dbg_ref[...] = jnp.zeros_like(dbg_ref)
            dgnw_ref[...] = jnp.zeros_like(dgnw_ref)

        _, upper = _chunk_masks()
        lr = lr_ref[...]
        b = b_ref[...]
        q = q_ref[...]
        k = k_ref[...]
        eb, enb, ebl, eblast = _gla_decays(q, k, b)
        qt = q * 0.125 * eb
        kt = k * enb
        kh = k * ebl
        gn = gnw_ref[...]
        tril = _tri(True)
        triu = _tri(False)
        heads = range(4)
        hs = [slice(h * DK, (h + 1) * DK) for h in heads]
        vs = [slice(h * 128, (h + 1) * 128) for h in heads]
        ein = functools.partial(jnp.einsum, preferred_element_type=F32)
        dr_l, doh = [], []
        dgn = jnp.zeros((1, 128), F32)
        for h in heads:
            on, rs = _rms(oraw_ref[:, vs[h]])
            r = r_ref[:, vs[h]]
            sig = jax.nn.sigmoid(r)
            sil = r * sig
            dy = do_ref[:, vs[h]]
            dr_l.append(dy * on * gn * (sig * (1.0 + r * (1.0 - sig))))
            dgn = dgn + jnp.sum(dy * sil * on, axis=0, keepdims=True)
            doh.append(_bf(_rms_bwd(dy * sil, on, rs, gn)).reshape(nc, CH, 128))
        dgnw_ref[...] += dgn
        qh = [_bf(qt[:, hs[h]]).reshape(nc, CH, DK) for h in heads]
        kth = [_bf(kt[:, hs[h]]).reshape(nc, CH, DK) for h in heads]
        khh = [_bf(kh[:, hs[h]]).reshape(nc, CH, DK) for h in heads]
        vh = [_bf(v_ref[:, vs[h]]).reshape(nc, CH, 128) for h in heads]
        at = [ein('cjd,cid->cji', kth[h], qh[h]) for h in heads]
        da = [ein('civ,cjv->cij', doh[h], vh[h]) for h in heads]
        dat = [ein('cjv,civ->cji', vh[h], doh[h]) for h in heads]
        gq = [ein('civ,cid->cvd', doh[h], qh[h]) for h in heads]
        stf = [sst_ref[:, h] for h in heads]
        dqs = [ein('civ,cvd->cid', doh[h], _bf(stf[h])) for h in heads]
        dv = [ein('cji,civ->cjv', _bf(jnp.where(triu, at[h], 0.0)), doh[h]) for h in heads]
        dqt = [ein('cij,cjd->cid', _bf(jnp.where(tril, da[h], 0.0)), kth[h]) + dqs[h] for h in heads]
        dkt = [ein('cji,cid->cjd', _bf(jnp.where(triu, dat[h], 0.0)), qh[h]) for h in heads]
        dse = []
        for h in heads:
            dst = dst_scr[h]
            dsend = [None] * nc
            for c in reversed(range(nc)):
                dsend[c] = dst
                dst = dst * eblast[c, :, hs[h]] + gq[h][c]
            dst_scr[h] = dst
            dse.append(jnp.stack(dsend))
        dseb = [_bf(d) for d in dse]
        dv = [dv[h] + ein('cjd,cvd->cjv', khh[h], dseb[h]) for h in heads]
        dkh = [ein('cjv,cvd->cjd', vh[h], dseb[h]) for h in heads]
        carried = jnp.concatenate([jnp.sum(dse[h] * stf[h], axis=1, keepdims=True) for h in heads], axis=2)
        wide = lambda parts: jnp.concatenate([p.reshape(TM, DK) for p in parts], axis=1)
        dqt_w, dkt_w, dkh_w = wide(dqt), wide(dkt), wide(dkh)
        dkh_kh = dkh_w * kh
        extra = jnp.sum(dkh_kh.reshape(nc, CH, 256), axis=1, keepdims=True) + eblast * carried
        db = dqt_w * qt - dkt_w * kt - dkh_kh
        dg = _dot3(upper, db) + jnp.broadcast_to(extra, (nc, CH, 256)).reshape(TM, 256)
        dz = dg * dgate_ref[...]
        dzb = _bf(dz)
        dlr_ref[...] = _bf(_dot_nt(dzb, wgu_ref[...]))
        dwgu_ref[...] += _dot_tn(_bf(lr), dzb)
        dbg_ref[...] += jnp.sum(dz, axis=0, keepdims=True)
        dq = dqt_w * eb * 0.125
        dk = dkt_w * enb + dkh_w * ebl
        dgla_ref[...] = _bf(jnp.concatenate([dq, dk] + [d.reshape(TM, 128) for d in dv] + dr_l, axis=1))

    def spec(w, cb):
        return pl.BlockSpec((TM, w), lambda i: (blk(i), cb))

    def acc(shape):
        return pl.BlockSpec(shape, lambda i: (0, 0))

    return _call(
        body, "gla_bwd", (nt,),
        [spec(256, 0), spec(256, 1), spec(512, 1), spec(512, 2), spec(128, C_LR // 128), spec(512, 0),
         pl.BlockSpec((nc, 4, 128, DK), lambda i: (blk(i), 0, 0, 0)), spec(256, 0), spec(256, 0), spec(512, 0),
         VMEM_SPEC, VMEM_SPEC],
        [spec(1536, 0), spec(128, 0), acc((128, 256)), acc((1, 256)), acc((1, 128))],
        [jax.ShapeDtypeStruct((t + TM, 1536), BF16), jax.ShapeDtypeStruct((t + TM, 128), BF16),
         jax.ShapeDtypeStruct((128, 256), F32), jax.ShapeDtypeStruct((1, 256), F32), jax.ShapeDtypeStruct((1, 128), F32)],
        [pltpu.VMEM((4, 128, DK), F32)], _cp(48), (proj, proj, proj, proj, proj, oraw, sst, bcum, dgate, do, wgu, gnw), comm)


def _rope_tables(t):
    r = t + TM
    row = np.arange(r)
    pos = np.where(row < t, row + NM, np.where(row < t + NM, row - t, 0)).astype(np.float32)
    inv_freq = (1.0 / (np.float32(ROPE_THETA) ** (np.arange(0, 16, 2, dtype=np.float32) / np.float32(16)))).astype(np.float32)
    ang = (pos[:, None] * inv_freq[None, :]).astype(np.float32)
    cos, sin = np.cos(ang).astype(np.float32), np.sin(ang).astype(np.float32)
    one, zero = np.ones((r, 48), np.float32), np.zeros((r, 48), np.float32)
    return jnp.asarray(np.concatenate([cos, cos, one, -sin, sin, zero], axis=1))


def _rope(x, tab, sign):
    w = x.shape[1]
    rep = w // 64
    c = jnp.concatenate([tab[:, 0:64]] * rep, axis=1)
    s = jnp.concatenate([tab[:, 64:128]] * rep, axis=1)
    lane = lax.rem(lax.broadcasted_iota(jnp.int32, x.shape, 1), 64)
    partner = jnp.where(lane < 8, pltpu.roll(x, w - 8, 1), jnp.where(lane < 16, pltpu.roll(x, 8, 1), 0.0))
    return x * c + sign * (partner * s)


HB_BWD = 4


def _stack(x, hg):
    w = x.shape[1] // hg
    return x if hg == 1 else jnp.concatenate([x[:, g * w:(g + 1) * w] for g in range(hg)], axis=0)


def _unstack(x, hg):
    return x if hg == 1 else jnp.concatenate([x[g * SB:(g + 1) * SB] for g in range(hg)], axis=1)


def _swa_steps(r_tot):
    blocks = r_tot // SB
    return next(n for n in (6, 3, 2) if blocks % n == 0 and blocks // n >= 2)


def _swa_specs(nsb, nbq):
    def rows(h, w, cb, f):
        return pl.BlockSpec((h, w), lambda i: (f(i), cb))
    pair = lambda i: i
    prev = lambda i: jnp.maximum(nbq * i - 1, 0)
    meta = lambda i: nsb
    return rows, pair, prev, meta


def _swa_fwd(proj, sinks, t, comm=None):
    nsb = t // SB
    r_tot = t + TM
    nbq = _swa_steps(r_tot)
    qb = nbq * SB
    rows, pair, prev, meta = _swa_specs(nsb, nbq)

    def body(sink_ref, q_ref, kc_ref, kp_ref, km_ref, vc_ref, vp_ref, vm_ref, o_ref, lse_ref):
        i = pl.program_id(0)
        key = lax.broadcasted_iota(jnp.int32, (SB, SB), 0)
        qry = lax.broadcasted_iota(jnp.int32, (SB, SB), 1)
        km, vm = km_ref[0:NM, :], vm_ref[0:NM, :]
        for j in range(nbq):
            b = nbq * i + j
            rs = slice(j * SB, (j + 1) * SB)
            before = slice((j - 1) * SB, j * SB)
            real = b < nsb
            masks = (key <= qry, (key > qry) & (b > 0) & real, real)
            k3 = (kc_ref[rs, :], kp_ref[...] if j == 0 else kc_ref[before, :], km)
            v3 = (vc_ref[rs, :], vp_ref[...] if j == 0 else vc_ref[before, :], vm)
            valid = b * SB + lax.broadcasted_iota(jnp.int32, (1, SB), 1) < t + NM
            heads = range(8)
            kb = [[_bf(k[:, kv * 64:(kv + 1) * 64]) for k in k3] for kv in range(2)]
            vt = [[_bf(v[:, kv * 64:(kv + 1) * 64].T) for v in v3] for kv in range(2)]
            raw = [[_dot_nt(k, _bf(q_ref[rs, h * 64:(h + 1) * 64])) for k in kb[h // 4]] for h in heads]
            probs, inv_l, lse_l = [], [], []
            for h in heads:
                s = [jnp.where(m, sx, NEG) for m, sx in zip(masks, raw[h])]
                sink = sink_ref[0, h]
                top = jnp.maximum(jnp.max(jnp.maximum(s[0], s[1]), axis=0, keepdims=True),
                                  jnp.maximum(jnp.max(s[2], axis=0, keepdims=True), sink))
                p = [jnp.exp(sx - top) for sx in s]
                l = (jnp.sum(p[0] + p[1], axis=0, keepdims=True) + jnp.sum(p[2], axis=0, keepdims=True)
                     + jnp.exp(sink - top))
                probs.append([_bf(px) for px in p])
                inv_l.append(1.0 / l)
                lse_l.append(top + jnp.log(l))
            o_t = [_dot(vt[h // 4][0], probs[h][0]) + _dot(vt[h // 4][1], probs[h][1]) + _dot(vt[h // 4][2], probs[h][2])
                   for h in heads]
            o_ref[rs, :] = _bf(jnp.concatenate([jnp.where(valid, o_t[h] * inv_l[h], 0.0).T for h in heads], axis=1))
            lse_ref[:, rs] = jnp.concatenate(lse_l, axis=0)

    ck, cv = C_SK // 128, C_SV // 128
    return _call(
        body, "swa_fwd", (r_tot // qb,),
        [SMEM_SPEC, rows(qb, 512, C_SQ // 512, pair),
         rows(qb, 128, ck, pair), rows(SB, 128, ck, prev), rows(SB, 128, ck, meta),
         rows(qb, 128, cv, pair), rows(SB, 128, cv, prev), rows(SB, 128, cv, meta)],
        [rows(qb, 512, 0, pair), pl.BlockSpec((8, qb), lambda i: (0, i))],
        [jax.ShapeDtypeStruct((r_tot, 512), BF16), jax.ShapeDtypeStruct((8, r_tot), F32)],
        [], _cp(32), (sinks, proj, proj, proj, proj, proj, proj, proj), comm)


def _swa_bwd(proj, sinks, lse_t, do, t, comm=None):
    nsb = t // SB
    r_tot = t + TM
    nbq = _swa_steps(r_tot)
    qb = nbq * SB
    rows, pair, prev, meta = _swa_specs(nsb, nbq)
    hb = HB_BWD
    lanes = hb * SB

    def body(sink_ref, q_ref, kc_ref, kp_ref, km_ref, vc_ref, vp_ref, vm_ref, lse_ref, do_ref,
             dq_ref, dk_ref, dv_ref, dsink_ref):
        i = pl.program_id(0)

        @pl.when(i == 0)
        def _():
            dk_ref[...] = jnp.zeros_like(dk_ref)
            dv_ref[...] = jnp.zeros_like(dv_ref)
            dsink_ref[...] = jnp.zeros_like(dsink_ref)

        key = lax.broadcasted_iota(jnp.int32, (SB, lanes), 0)
        qry = lax.rem(lax.broadcasted_iota(jnp.int32, (SB, lanes), 1), SB)
        km, vm = km_ref[0:NM, :], vm_ref[0:NM, :]
        dsink_l = []
        for j in range(nbq):
            b = nbq * i + j
            rs = slice(j * SB, (j + 1) * SB)
            before = slice((j - 1) * SB, j * SB)
            real = b < nsb
            masks = (key <= qry, (key > qry) & (b > 0) & real, real)
            k3 = (kc_ref[rs, :], kp_ref[...] if j == 0 else kc_ref[before, :], km)
            v3 = (vc_ref[rs, :], vp_ref[...] if j == 0 else vc_ref[before, :], vm)
            groups = list(range(0, 8, hb))
            kvs = [h0 // 4 for h0 in groups]
            qg = [_bf(_stack(q_ref[rs, h0 * 64:(h0 + hb) * 64], hb)) for h0 in groups]
            dog = [_bf(_stack(do_ref[rs, h0 * 64:(h0 + hb) * 64], hb)) for h0 in groups]
            kb = [[_bf(k[:, kv * 64:(kv + 1) * 64]) for k in k3] for kv in kvs]
            vb = [[_bf(v[:, kv * 64:(kv + 1) * 64]) for v in v3] for kv in kvs]
            s = [[_dot_nt(k, qg[g]) for k in kb[g]] for g in range(len(groups))]
            dp = [[_dot_nt(v, dog[g]) for v in vb[g]] for g in range(len(groups))]
            p, ds, ds_blk = [], [], []
            for g, h0 in enumerate(groups):
                lse_row = jnp.concatenate([lse_ref[h:h + 1, rs] for h in range(h0, h0 + hb)], axis=1)
                sink_row = jnp.concatenate([jnp.full((1, SB), sink_ref[0, h], F32) for h in range(h0, h0 + hb)], axis=1)
                pg = [jnp.exp(jnp.where(m, sx, NEG) - lse_row) for m, sx in zip(masks, s[g])]
                delta = (jnp.sum(pg[0] * dp[g][0] + pg[1] * dp[g][1], axis=0, keepdims=True)
                         + jnp.sum(pg[2] * dp[g][2], axis=0, keepdims=True))
                ds.append([_bf(pp * (dd - delta)) for pp, dd in zip(pg, dp[g])])
                p.append([_bf(pp) for pp in pg])
                ds_row = -jnp.exp(sink_row - lse_row) * delta
                ds_blk += [jnp.sum(ds_row[:, q0 * SB:(q0 + 1) * SB], axis=1, keepdims=True) for q0 in range(hb)]
            dsink_l.append(jnp.concatenate(ds_blk, axis=1))
            dq_t = [_dot_tn(kb[g][0], ds[g][0]) + _dot_tn(kb[g][1], ds[g][1]) + _dot_tn(kb[g][2], ds[g][2])
                    for g in range(len(groups))]
            dq_ref[rs, :] = jnp.concatenate([_unstack(d.T, hb) for d in dq_t], axis=1)
            windows = (pl.ds(pl.multiple_of(b * SB, SB), SB), pl.ds(pl.multiple_of(jnp.maximum(b - 1, 0) * SB, SB), SB),
                       pl.ds(t, NM))
            for x in range(3):
                dk_kv, dv_kv = [], []
                for kv in range(2):
                    mine = [g for g in range(len(groups)) if kvs[g] == kv]
                    dk_kv.append(sum(_dot(ds[g][x], qg[g]) for g in mine))
                    dv_kv.append(sum(_dot(p[g][x], dog[g]) for g in mine))
                dk_ref[windows[x], :] += jnp.concatenate(dk_kv, axis=1)
                dv_ref[windows[x], :] += jnp.concatenate(dv_kv, axis=1)
        dsink_ref[...] += sum(dsink_l)

    ck, cv = C_SK // 128, C_SV // 128
    whole = lambda w: pl.BlockSpec((r_tot, w), lambda i: (0, 0))
    return _call(
        body, "swa_bwd", (r_tot // qb,),
        [SMEM_SPEC, rows(qb, 512, C_SQ // 512, pair),
         rows(qb, 128, ck, pair), rows(SB, 128, ck, prev), rows(SB, 128, ck, meta),
         rows(qb, 128, cv, pair), rows(SB, 128, cv, prev), rows(SB, 128, cv, meta),
         pl.BlockSpec((8, qb), lambda i: (0, i)), rows(qb, 512, 1, pair)],
        [rows(qb, 512, 0, pair), whole(128), whole(128), pl.BlockSpec((1, 8), lambda i: (0, 0))],
        [jax.ShapeDtypeStruct((r_tot, 512), F32), jax.ShapeDtypeStruct((r_tot, 128), F32),
         jax.ShapeDtypeStruct((r_tot, 128), F32), jax.ShapeDtypeStruct((1, 8), F32)],
        [], _cp(48), (sinks, proj, proj, proj, proj, proj, proj, proj, lse_t, do), comm)


HK = D // 2
MLP_FWD_ROWS = 512


def _mlp_fwd(x, tgt, ogla, oswa, wo, wff, w1, w2, wfin):
    t = x.shape[0]
    tm = MLP_FWD_ROWS if t % MLP_FWD_ROWS == 0 else TM

    def body(x_ref, tgt_ref, og_ref, os_ref, wo_ref, wff_ref, w1a_ref, w1b_ref, w2a_ref, w2b_ref, wfin_ref,
             h1_ref, f_ref, a_ref, dh2_ref, loss_ref, gfin_ref):
        i = pl.program_id(0)

        @pl.when(i == 0)
        def _():
            loss_ref[...] = jnp.zeros_like(loss_ref)
            gfin_ref[...] = jnp.zeros_like(gfin_ref)

        h1 = x_ref[...] + _dot(og_ref[...], wo_ref[0:512, :]) + _dot(os_ref[...], wo_ref[512:1024, :])
        h1_ref[...] = h1
        fh, _ = _rms(h1)
        f = _bf(fh * wff_ref[...])
        f_ref[...] = f
        acc = jnp.zeros((tm, D), F32)
        for n in range(4):
            a = _dot(f[:, 0:HK], w1a_ref[n]) + _dot(f[:, HK:D], w1b_ref[n])
            a_ref[:, n * D:(n + 1) * D] = _bf(a)
            zr = jnp.maximum(a, 0.0)
            z = _bf(zr * zr)
            acc = acc + _dot(z[:, 0:HK], w2a_ref[n]) + _dot(z[:, HK:D], w2b_ref[n])
        h2 = h1 + acc
        yh, rs2 = _rms(h2)
        wf = wfin_ref[...]
        e = yh * wf - tgt_ref[...]
        loss_ref[...] += jnp.sum(jnp.sum(e * e, axis=0, keepdims=True), axis=1, keepdims=True) * (0.5 / D)
        dy = e * (1.0 / D)
        gfin_ref[...] += jnp.sum(dy * yh, axis=0, keepdims=True)
        dh2_ref[...] = _rms_bwd(dy, yh, rs2, wf)

    rs = lambda w: pl.BlockSpec((tm, w), lambda i: (i, 0))
    return pl.pallas_call(
        body, name="mlp_fwd", grid=(t // tm,),
        in_specs=[rs(D), rs(D), rs(512), rs(512)] + [VMEM_SPEC] * 7,
        out_specs=[rs(D), rs(D), rs(DFF), rs(D), pl.BlockSpec((1, 1), lambda i: (0, 0)), pl.BlockSpec((1, D), lambda i: (0, 0))],
        out_shape=[jax.ShapeDtypeStruct((t, D), F32), jax.ShapeDtypeStruct((t, D), BF16),
                   jax.ShapeDtypeStruct((t, DFF), BF16), jax.ShapeDtypeStruct((t, D), F32),
                   jax.ShapeDtypeStruct((1, 1), F32), jax.ShapeDtypeStruct((1, D), F32)],
        compiler_params=_cp(60),
    )(x, tgt, ogla, oswa, wo, wff, *w1, *w2, wfin)


def _mlp_bwd(h1, a, dh2, ogla, oswa, wo, wff, w1, w2):
    t = h1.shape[0]
    tm = MLP_FWD_ROWS if t % MLP_FWD_ROWS == 0 else TM
    nb = t // tm

    def body(h1_ref, a_ref, dh2_ref, og_ref, os_ref, wo_ref, wff_ref, w1a_ref, w1b_ref, w2a_ref, w2b_ref,
             da_ref, dh2b_ref, dh1_ref, do_ref, dwo_acc, gff_ref):
        i = pl.program_id(0)

        @pl.when(i == 0)
        def _():
            dwo_acc[...] = jnp.zeros_like(dwo_acc)
            gff_ref[...] = jnp.zeros_like(gff_ref)

        @pl.when(i < nb)
        def _():
            dh2 = dh2_ref[...]
            dh2b = _bf(dh2)
            dh2b_ref[...] = dh2b
            dfa = jnp.zeros((tm, HK), F32)
            dfb = jnp.zeros((tm, HK), F32)
            for n in range(4):
                dz = jnp.concatenate([_dot_nt(dh2b, w2a_ref[n]), _dot_nt(dh2b, w2b_ref[n])], axis=1)
                da = _bf(dz * (2.0 * jnp.maximum(a_ref[:, n * D:(n + 1) * D].astype(F32), 0.0)))
                da_ref[:, n * D:(n + 1) * D] = da
                dfa = dfa + _dot_nt(da, w1a_ref[n])
                dfb = dfb + _dot_nt(da, w1b_ref[n])
            df = jnp.concatenate([dfa, dfb], axis=1)
            fh, rs1 = _rms(h1_ref[...])
            gff_ref[...] += jnp.sum(df * fh, axis=0, keepdims=True)
            dh1 = dh2 + _rms_bwd(df, fh, rs1, wff_ref[...])
            dh1_ref[...] = dh1
            dh1b = _bf(dh1)
            do_ref[...] = _dot_nt(dh1b, wo_ref[...])
            dwo_acc[0:512, :] += _dot_tn(og_ref[...], dh1b)
            dwo_acc[512:1024, :] += _dot_tn(os_ref[...], dh1b)

        @pl.when(i == nb)
        def _():
            dh1_ref[...] = jnp.zeros_like(dh1_ref)
            do_ref[...] = jnp.zeros_like(do_ref)

    rs = lambda w: pl.BlockSpec((tm, w), lambda i: (i, 0))
    real = lambda w: pl.BlockSpec((tm, w), lambda i: (jnp.minimum(i, nb - 1), 0))
    return pl.pallas_call(
        body, name="mlp_bwd", grid=(nb + 1,),
        in_specs=[real(D), real(DFF), real(D), real(512), real(512)] + [VMEM_SPEC] * 6,
        out_specs=[real(DFF), real(D), rs(D), rs(D), VMEM_SPEC, pl.BlockSpec((1, D), lambda i: (0, 0))],
        out_shape=[jax.ShapeDtypeStruct((t, DFF), BF16), jax.ShapeDtypeStruct((t, D), BF16),
                   jax.ShapeDtypeStruct((t + tm, D), F32), jax.ShapeDtypeStruct((t + tm, D), F32),
                   jax.ShapeDtypeStruct((D, D), F32), jax.ShapeDtypeStruct((1, D), F32)],
        compiler_params=_cp(62),
    )(h1, a, dh2, ogla, oswa, wo, wff, *w1, *w2)


def _ffn_wgrad(f, a, da, dh2b):
    rows = f.shape[0]
    kt = 1024 if rows % 1024 == 0 else TM
    nk = rows // kt

    def body(f_ref, a_ref, da_ref, dh2_ref, dw1_ref, dw2_ref, acc1, acc2):
        k = pl.program_id(1)

        @pl.when(k == 0)
        def _():
            acc1[...] = jnp.zeros_like(acc1)
            acc2[...] = jnp.zeros_like(acc2)

        zr = jnp.maximum(a_ref[...], 0.0)
        acc1[...] += _dot_tn(f_ref[...], da_ref[...])
        acc2[...] += _dot_tn(zr * zr, dh2_ref[...])

        @pl.when(k == nk - 1)
        def _():
            for hh in range(2):
                dw1_ref[hh, 0] = acc1[hh * 512:(hh + 1) * 512, :]
                dw2_ref[hh, 0] = acc2[hh * 512:(hh + 1) * 512, :]

    out = pl.BlockSpec((2, 1, 512, D), lambda n, k: (0, n, 0, 0))
    return pl.pallas_call(
        body, name="ffn_wgrad", grid=(4, nk),
        in_specs=[pl.BlockSpec((kt, D), lambda n, k: (k, 0)), pl.BlockSpec((kt, D), lambda n, k: (k, n)),
                  pl.BlockSpec((kt, D), lambda n, k: (k, n)), pl.BlockSpec((kt, D), lambda n, k: (k, 0))],
        out_specs=[out, out],
        out_shape=[jax.ShapeDtypeStruct((2, 4, 512, D), F32)] * 2,
        scratch_shapes=[pltpu.VMEM((D, D), F32), pltpu.VMEM((D, D), F32)],
        compiler_params=_cp(48, ("arbitrary", "arbitrary")),
    )(f, a, da, dh2b)


def _proj_bwd(x, metapad, wm, wt3, tabs, dgla, dswa_q, dsk, dsv, dlr, dh1, comm=None):
    t = x.shape[0]
    nblk = t // TM

    def body(x_ref, mp_ref, wm_ref, w3_ref, tab_ref, dg_ref, dq_ref, dk_ref, dv_ref, dlr_ref, dh1_ref,
             gx_ref, gmeta_ref, dw_ref, gmix_ref, w_ref, acc):
        i = pl.program_id(0)

        @pl.when(i == 0)
        def _():
            _join_shards(w3_ref, w_ref)
            acc[...] = jnp.zeros_like(acc)
            gmix_ref[...] = jnp.zeros_like(gmix_ref)

        h = jnp.where(i == nblk, mp_ref[...], x_ref[...])
        uh, rs = _rms(h)
        wm_v = wm_ref[...]
        u = _bf(uh * wm_v)
        tab = tab_ref[...]
        dq = _bf(_rope(dq_ref[...] * 0.125, tab, -1.0))
        dk = _bf(_rope(dk_ref[...], tab, -1.0))
        parts = ((dg_ref[...], 0, R_LR), (dlr_ref[:, 0:16], R_LR, 16), (dq, R_LR + 16, 512),
                 (dk, R_LR + 528, 128), (_bf(dv_ref[...]), R_LR + 656, 128))
        du = jnp.zeros((TM, D), F32)
        for val, r0, w in parts:
            du = du + _dot(val, w_ref[r0:r0 + w, :])
            acc[r0:r0 + w, :] += _dot_tn(val, u)
        gmix_ref[...] += jnp.sum(du * uh, axis=0, keepdims=True)
        dh0 = dh1_ref[...] + _rms_bwd(du, uh, rs, wm_v)

        @pl.when(i < nblk)
        def _():
            gx_ref[...] = dh0

        @pl.when(i == nblk)
        def _():
            gmeta_ref[...] = dh0[:NM]
            for s in range(4):
                dw_ref[s] = acc[(DIN // 4) * s:(DIN // 4) * (s + 1), :]

    xs = pl.BlockSpec((TM, D), lambda i: (jnp.minimum(i, nblk - 1), 0))
    rs_ = lambda w: pl.BlockSpec((TM, w), lambda i: (i, 0))
    return _call(
        body, "proj_bwd", (nblk + 1,),
        [xs, VMEM_SPEC, VMEM_SPEC, VMEM_SPEC, rs_(128), rs_(1536), rs_(512), rs_(128), rs_(128), rs_(128), rs_(D)],
        [xs, pl.BlockSpec((NM, D), lambda i: (0, 0)), VMEM_SPEC, pl.BlockSpec((1, D), lambda i: (0, 0))],
        [jax.ShapeDtypeStruct((t, D), F32), jax.ShapeDtypeStruct((NM, D), F32),
         jax.ShapeDtypeStruct((4, DIN // 4, D), F32), jax.ShapeDtypeStruct((1, D), F32)],
        [pltpu.VMEM((DIN, D), BF16), pltpu.VMEM((DIN, D), F32)], _cp(56),
        (x, metapad, wm, wt3, tabs, dgla, dswa_q, dsk, dsv, dlr, dh1), comm)


def _place():
    return lax.axis_index("x"), lax.axis_index("y"), lax.axis_index("c")


def _other_chips(x, y):
    return [(1 - x, y), (x, 1 - y), (1 - x, 1 - y)]


def _dma_sems(*counts):
    return tuple(pltpu.SemaphoreType.DMA((k,)) for k in counts)


def _gather_shards(shards, split):
    n = len(shards)
    two = [a for a in range(n) if split[a]]

    def plan(ins, outs, sems):
        isend, irecv, dsend, drecv, loc = sems
        x, y, c = _place()
        chips = _other_chips(x, y)

        def part(ref, a, half):
            if not split[a]:
                return ref
            w = shards[a].shape[1] // 2
            return ref.at[:, pl.ds(pl.multiple_of(half * w, 128), w)]

        def over_ici(a, k, shard_of):
            tx, ty = chips[k]
            sx, sy = shard_of
            return pltpu.make_async_remote_copy(
                src_ref=part(ins[a], a, c), dst_ref=part(outs[a].at[2 * sx + sy], a, c), send_sem=isend.at[3 * a + k],
                recv_sem=irecv.at[3 * a + k], device_id=(tx, ty, c), device_id_type=MESH)

        def over_d2d(a, k, half):
            tx, ty = chips[k]
            ref = part(outs[a].at[2 * tx + ty], a, half)
            return pltpu.make_async_remote_copy(
                src_ref=ref, dst_ref=ref, send_sem=dsend.at[3 * a + k], recv_sem=drecv.at[3 * a + k],
                device_id=(x, y, 1 - c), device_id_type=MESH)

        def local(a):
            return pltpu.make_async_copy(ins[a], outs[a].at[2 * x + y], loc.at[a])

        pairs = [(a, k) for a in range(n) for k in range(3)]
        first = ([lambda a=a: local(a).start() for a in range(n)]
                 + [lambda a=a, k=k: over_ici(a, k, (x, y)).start() for a, k in pairs],
                 [lambda a=a, k=k: over_ici(a, k, chips[k]).wait_recv() for a, k in pairs]
                 + [lambda a=a, k=k: over_ici(a, k, (x, y)).wait_send() for a, k in pairs]
                 + [lambda a=a: local(a).wait() for a in range(n)])
        pairs2 = [(a, k) for a in two for k in range(3)]
        second = ([lambda a=a, k=k: over_d2d(a, k, c).start() for a, k in pairs2],
                  [lambda a=a, k=k: over_d2d(a, k, 1 - c).wait_recv() for a, k in pairs2]
                  + [lambda a=a, k=k: over_d2d(a, k, c).wait_send() for a, k in pairs2])
        return [first, second] if two else [first]

    return _Comm(tuple(shards), tuple(jax.ShapeDtypeStruct((4,) + s.shape, s.dtype) for s in shards),
                 _dma_sems(3 * n, 3 * n, 3 * n, 3 * n, n), 2 if two else 1, plan)


def _swap_halves(grads):
    n = len(grads)

    def plan(ins, outs, sems):
        send, recv = sems
        x, y, c = _place()

        def swap(a):
            return pltpu.make_async_remote_copy(
                src_ref=ins[a].at[1 - c], dst_ref=outs[a], send_sem=send.at[a], recv_sem=recv.at[a],
                device_id=(x, y, 1 - c), device_id_type=MESH)

        return [([lambda a=a: swap(a).start() for a in range(n)], [lambda a=a: swap(a).wait() for a in range(n)])]

    return _Comm(tuple(grads), tuple(jax.ShapeDtypeStruct(g.shape[1:], g.dtype) for g in grads), _dma_sems(n, n), 1, plan)


SCATTER_ADD_ROWS = 128


def _scatter_shards(parts, late):
    n = len(parts)

    def plan(ins, outs, sems):
        send, recv, loc = sems[:3]
        onward, got = sems[3:3 + n], sems[3 + n:]
        x, y, c = _place()
        x_first = c == 0
        near = (jnp.where(x_first, 1 - x, x), jnp.where(x_first, y, 1 - y))
        far = (jnp.where(x_first, x, 1 - x), jnp.where(x_first, 1 - y, y))
        diagonal = (1 - x, 1 - y)
        shard = lambda chip: 2 * chip[0] + chip[1]

        def hop(src, dst, k, chip):
            return pltpu.make_async_remote_copy(src_ref=src, dst_ref=dst, send_sem=send.at[k], recv_sem=recv.at[k],
                                                device_id=(chip[0], chip[1], c), device_id_type=MESH)

        theirs = lambda a: hop(ins[a].at[shard(near)], outs[a].at[0], 3 * a, near)
        passing = lambda a: hop(ins[a].at[shard(diagonal)], got[a], 3 * a + 1, near)
        summed = lambda a: hop(onward[a], outs[a].at[1], 3 * a + 2, far)
        mine = lambda a: pltpu.make_async_copy(ins[a].at[shard(far)], onward[a], loc.at[a])

        def add(a):
            for r in range(0, parts[a].shape[1], SCATTER_ADD_ROWS):
                rows = slice(r, r + SCATTER_ADD_ROWS)
                onward[a][rows, :] = _bf(onward[a][rows, :].astype(F32) + got[a][rows, :].astype(F32))

        every = range(n)
        first = ([lambda a=a: mine(a).start() for a in every] + [lambda a=a: passing(a).start() for a in every]
                 + [lambda a=a: theirs(a).start() for a in every],
                 [lambda a=a: mine(a).wait() for a in every] + [lambda a=a: passing(a).wait_recv() for a in every]
                 + [lambda a=a: add(a) for a in every])
        second = ([lambda a=a: summed(a).start() for a in every],
                  [lambda a=a: passing(a).wait_send() for a in every] + [lambda a=a: theirs(a).wait() for a in every]
                  + [lambda a=a: summed(a).wait() for a in every])
        return [first, second]

    assert all(p.shape[1] % SCATTER_ADD_ROWS == 0 for p in parts)
    buffers = [pltpu.VMEM(p.shape[1:], p.dtype) for p in parts]
    return _Comm(tuple(parts), tuple(jax.ShapeDtypeStruct((2,) + p.shape[1:], p.dtype) for p in parts),
                 _dma_sems(3 * n, 3 * n, n) + tuple(buffers) * 2, 2, plan, late)


def _join_halves(halves):
    n = len(halves)

    def plan(ins, outs, sems):
        send, recv, loc = sems
        x, y, c = _place()

        def remote(a, half):
            return pltpu.make_async_remote_copy(
                src_ref=ins[a], dst_ref=outs[a].at[half], send_sem=send.at[a], recv_sem=recv.at[a],
                device_id=(x, y, 1 - c), device_id_type=MESH)

        def local(a):
            return pltpu.make_async_copy(ins[a], outs[a].at[c], loc.at[a])

        every = range(n)
        return [([lambda a=a: local(a).start() for a in every] + [lambda a=a: remote(a, c).start() for a in every],
                 [lambda a=a: remote(a, 1 - c).wait_recv() for a in every]
                 + [lambda a=a: remote(a, c).wait_send() for a in every] + [lambda a=a: local(a).wait() for a in every])]

    return _Comm(tuple(halves), tuple(jax.ShapeDtypeStruct((2,) + h.shape, h.dtype) for h in halves),
                 _dma_sems(n, n, n), 1, plan)


def _reduce_w_in(dwt, comm):
    rows, hw = DIN // 4, D // 2
    ci, co = len(comm.ins), len(comm.outs)

    def body(*refs):
        dw_ref, c_in, out_ref, c_out = refs[0], refs[1:1 + ci], refs[1 + ci], refs[2 + ci:2 + ci + co]
        mine, sib, tosend, rbuf, qbuf, full, send, recv, loc = refs[2 + ci + co:11 + ci + co]
        c_sem = refs[11 + ci + co:]
        x, y, c = _place()
        sibling = (x, y, 1 - c)
        (starts, waits), = comm.plan(c_in, c_out, c_sem)
        _run_phase(starts)

        def cols(ref, half):
            window = pl.ds(pl.multiple_of(half * hw, 128), hw)
            return ref.at[:, :, window] if len(ref.shape) == 3 else ref.at[:, window]

        load = pltpu.make_async_copy(cols(dw_ref, c), mine, loc.at[0])
        give = pltpu.make_async_remote_copy(src_ref=cols(dw_ref, 1 - c), dst_ref=sib, send_sem=send.at[3], recv_sem=recv.at[3],
                                            device_id=sibling, device_id_type=MESH)
        load.start()
        give.start()
        load.wait()
        give.wait()
        mine[...] = mine[...] + sib[...]
        cps = []
        for k, (tx, ty) in enumerate(_other_chips(x, y)):
            tosend[k] = _bf(mine[2 * tx + ty])
            cps.append(pltpu.make_async_remote_copy(
                src_ref=tosend.at[k], dst_ref=rbuf.at[k], send_sem=send.at[k], recv_sem=recv.at[k],
                device_id=(tx, ty, c), device_id_type=MESH))
            cps[-1].start()
        for cp in cps:
            cp.wait()
        qbuf[...] = mine[2 * x + y] + rbuf[0].astype(F32) + rbuf[1].astype(F32) + rbuf[2].astype(F32)
        keep = pltpu.make_async_copy(qbuf, cols(full, c), loc.at[1])
        pass_on = pltpu.make_async_remote_copy(src_ref=qbuf, dst_ref=cols(full, c), send_sem=send.at[4], recv_sem=recv.at[4],
                                               device_id=sibling, device_id_type=MESH)
        keep.start()
        pass_on.start()
        keep.wait()
        pass_on.wait_send()
        pltpu.make_async_remote_copy(src_ref=qbuf, dst_ref=cols(full, 1 - c), send_sem=send.at[4], recv_sem=recv.at[4],
                                     device_id=sibling, device_id_type=MESH).wait_recv()
        out_ref[...] = full[...]
        _run_phase(waits)

    outs = pl.pallas_call(
        body, name="reduce_w_in",
        in_specs=[ANY_SPEC] * (1 + ci), out_specs=[VMEM_SPEC] + [ANY_SPEC] * co,
        out_shape=[jax.ShapeDtypeStruct((rows, D), F32)] + list(comm.outs),
        scratch_shapes=[pltpu.VMEM((4, rows, hw), F32), pltpu.VMEM((4, rows, hw), F32), pltpu.VMEM((3, rows, hw), BF16),
                        pltpu.VMEM((3, rows, hw), BF16), pltpu.VMEM((rows, hw), F32), pltpu.VMEM((rows, D), F32),
                        *_dma_sems(5, 5, 2), *comm.sems],
        compiler_params=pltpu.CompilerParams(vmem_limit_bytes=48 << 20),
    )(dwt, *comm.ins)
    return outs[0], outs[1:]


def _allreduce_small(pack):
    p = pack.shape[0]

    def body(in_ref, out_ref, buf, send, recv):
        x, y, c = _place()
        me = 4 * x + 2 * y + c
        buf[me] = in_ref[...]

        def peer_of(k):
            return x ^ (k >> 2), y ^ ((k >> 1) & 1), c ^ (k & 1)

        sends = [pltpu.make_async_remote_copy(
            src_ref=in_ref, dst_ref=buf.at[me], send_sem=send.at[k - 1], recv_sem=recv.at[k - 1],
            device_id=peer_of(k), device_id_type=MESH) for k in range(1, 8)]
        for cp in sends:
            cp.start()
        for k in range(1, 8):
            px, py, pc = peer_of(k)
            pltpu.make_async_remote_copy(
                src_ref=in_ref, dst_ref=buf.at[4 * px + 2 * py + pc], send_sem=send.at[k - 1], recv_sem=recv.at[k - 1],
                device_id=(x, y, c), device_id_type=MESH).wait_recv()
        for cp in sends:
            cp.wait_send()
        acc = buf[0]
        for d in range(1, 8):
            acc = acc + buf[d]
        out_ref[...] = acc

    return pl.pallas_call(
        body, name="allreduce_small",
        in_specs=[VMEM_SPEC], out_specs=VMEM_SPEC, out_shape=jax.ShapeDtypeStruct(pack.shape, F32),
        scratch_shapes=[pltpu.VMEM((8, p, D), F32), *_dma_sems(7, 7)],
    )(pack)


GRID4 = 4


def _sum_cores(core_shard, mine, theirs):
    n = len(mine)

    def body(cs_ref, *refs):
        ms, ts, bfs, owns = refs[:n], refs[n:2 * n], refs[2 * n:3 * n], refs[3 * n:]
        keep = pl.program_id(0) == cs_ref[1]
        for a in range(n):
            acc = ms[a][0, 0] + ts[a][0]
            bfs[a][0] = _bf(acc)

            @pl.when(keep)
            def _():
                owns[a][...] = acc

    shapes = [m.shape[2:] for m in mine]
    in_specs = ([pl.BlockSpec((1, 1) + s, lambda i, cs: (cs[0], i, 0, 0)) for s in shapes]
                + [pl.BlockSpec((1,) + s, lambda i, cs: (i, 0, 0)) for s in shapes])
    out_specs = ([pl.BlockSpec((1,) + s, lambda i, cs: (i, 0, 0)) for s in shapes]
                 + [pl.BlockSpec(s, lambda i, cs: (0, 0)) for s in shapes])
    outs = pl.pallas_call(
        body, name="sum_cores",
        grid_spec=pltpu.PrefetchScalarGridSpec(num_scalar_prefetch=1, grid=(4,), in_specs=in_specs, out_specs=out_specs),
        out_shape=[jax.ShapeDtypeStruct((4,) + s, BF16) for s in shapes] + [jax.ShapeDtypeStruct(s, F32) for s in shapes],
        compiler_params=_cp(48),
    )(core_shard, *mine, *theirs)
    return outs[:n], outs[n:]


def _sum_chips(own, arrived):
    n = len(own)

    def body(*refs):
        os_, ars, outs = refs[:n], refs[n:2 * n], refs[2 * n:]
        for a in range(n):
            outs[a][...] = os_[a][...] + ars[a][0].astype(F32) + ars[a][1].astype(F32)

    blocks = [(o.shape[0] // GRID4, o.shape[1]) for o in own]
    return pl.pallas_call(
        body, name="sum_chips", grid=(GRID4,),
        in_specs=([pl.BlockSpec(b, lambda i: (i, 0)) for b in blocks]
                  + [pl.BlockSpec((2,) + b, lambda i: (0, i, 0)) for b in blocks]),
        out_specs=[pl.BlockSpec(b, lambda i: (i, 0)) for b in blocks],
        out_shape=[jax.ShapeDtypeStruct(o.shape, F32) for o in own],
        compiler_params=_cp(32),
    )(*own, *arrived)


def _adamw_math(w, g, m, v):
    m2 = ADAM_B1 * m + (1.0 - ADAM_B1) * g
    v2 = ADAM_B2 * v + (1.0 - ADAM_B2) * (g * g)
    m_hat = m2 / (1.0 - ADAM_B1 ** ADAM_STEP)
    v_hat = v2 / (1.0 - ADAM_B2 ** ADAM_STEP)
    return -ADAM_LR * (m_hat / (jnp.sqrt(v_hat) + ADAM_EPS) + ADAM_WD * w), m2, v2


def _adamw_big(ws, gs, ms, vs):
    n = len(ws)

    def body(*refs):
        for a in range(n):
            d, m2, v2 = _adamw_math(refs[a][...], refs[n + a][...], refs[2 * n + a][...], refs[3 * n + a][...])
            refs[4 * n + a][...] = d
            refs[5 * n + a][...] = m2
            refs[6 * n + a][...] = v2

    specs = [pl.BlockSpec((w.shape[0] // GRID4, w.shape[1]), lambda i: (i, 0)) for w in ws]
    return pl.pallas_call(
        body, name="adamw_big", grid=(GRID4,),
        in_specs=specs * 4, out_specs=specs * 3,
        out_shape=[jax.ShapeDtypeStruct(w.shape, F32) for w in ws] * 3,
        compiler_params=_cp(48),
    )(*ws, *gs, *ms, *vs)


def _adamw_small(ws, gs, ms, vs):
    n = len(ws)

    def body(*refs):
        for a in range(n):
            d, m2, v2 = _adamw_math(refs[a][...], refs[n + a][...], refs[2 * n + a][...], refs[3 * n + a][...])
            refs[4 * n + a][...] = d
            refs[5 * n + a][...] = m2
            refs[6 * n + a][...] = v2

    return pl.pallas_call(
        body, name="adamw_small",
        in_specs=[VMEM_SPEC] * (4 * n), out_specs=[VMEM_SPEC] * (3 * n),
        out_shape=[jax.ShapeDtypeStruct(w.shape, F32) for w in ws] * 3,
        compiler_params=pltpu.CompilerParams(vmem_limit_bytes=40 << 20),
    )(*ws, *gs, *ms, *vs)


def kernel(x, meta_tokens, norm_mix_w, w_in, w_gate_up, b_gate, gla_norm_w, sinks, w_out, norm_ff_w, w_ff1, w_ff2, final_norm_w, loss_target, m_meta_tokens, m_norm_mix_w, m_w_in, m_w_gate_up, m_b_gate, m_gla_norm_w, m_sinks, m_w_out, m_norm_ff_w, m_w_ff1, m_w_ff2, m_final_norm_w, v_meta_tokens, v_norm_mix_w, v_w_in, v_w_gate_up, v_b_gate, v_gla_norm_w, v_sinks, v_w_out, v_norm_ff_w, v_w_ff1, v_w_ff2, v_final_norm_w):
    xi, yi, ci = _place()
    shard = (2 * xi + yi).astype(jnp.int32).reshape(1)
    core = ci.astype(jnp.int32).reshape(1)

    small = jnp.concatenate([meta_tokens, w_gate_up[0], jnp.zeros((NM, 64), F32)], axis=1)
    wt3, g_small = _run_comm(_gather_shards([_bf(w_in[0].T), small], [True, False]), "gather_w_in")
    meta = g_small[:, :, 0:256].transpose(1, 0, 2).reshape(NM, D)
    wgu = g_small[:, :, 256:320].transpose(1, 0, 2).reshape(NM, 256)

    xs, tgt = x[0], loss_target[0]
    t = xs.shape[0]
    wfin = final_norm_w.reshape(1, D)
    metapad = jnp.concatenate([meta, jnp.zeros((TM - NM, D), F32)], axis=0)
    wgu_p = _bf(jnp.concatenate([wgu, jnp.zeros((128 - 16, 256), F32)], axis=0))
    tabs = _rope_tables(t)

    w1s, w2s = _bf(w_ff1[0]), _bf(w_ff2[0])
    proj, (g_out, w1a) = _proj_fwd(xs, metapad, norm_mix_w, wt3, tabs,
                                   _gather_shards([_bf(w_out[0]), w1s[:HK]], [True] * 2))
    (ogla, oraw, sst, bcum, dgate), (w1b, w2a) = _gla_fwd(proj, wgu_p, b_gate, gla_norm_w, t,
                                                          _gather_shards([w1s[HK:], w2s[:HK]], [True] * 2))
    (oswa, lse), (w2b,) = _swa_fwd(proj, sinks, t, _gather_shards([w2s[HK:]], [True]))
    wo, w1, w2 = g_out.reshape(D, D), (w1a, w1b), (w2a, w2b)
    h1, f, a, dh2, loss, gfin = _mlp_fwd(xs, tgt, ogla, oswa, wo, norm_ff_w, w1, w2, wfin)

    da, dh2b, dh1, do, dwo, gff = _mlp_bwd(h1, a, dh2, ogla, oswa, wo, norm_ff_w, w1, w2)
    dwo = dwo.reshape(4, 2, 128, D).transpose(1, 0, 2, 3)
    dw1, dw2 = _ffn_wgrad(f, a, da, dh2b)
    big = [dwo, dw1, dw2]
    (dgla, dlr, dwgu, dbg, dgnw), theirs = _gla_bwd(proj, oraw, sst, bcum, dgate, do, wgu_p, gla_norm_w, t,
                                                    _swap_halves(big))
    sums_bf, own = _sum_cores(jnp.concatenate([core, shard]), big, theirs)
    swa_grid = (t + TM) // SB // _swa_steps(t + TM)
    (dsq, dsk, dsv, dsink), arrived = _swa_bwd(proj, sinks, lse, do, t, _scatter_shards(sums_bf, swa_grid // 3))
    halves = _sum_chips(own, arrived)
    (gx, gmeta, dwt, gmix), _ = _proj_bwd(xs, metapad, norm_mix_w, wt3, tabs, dgla, dsq, dsk, dsv, dlr, dh1)

    gwt_in, joined = _reduce_w_in(dwt, _join_halves(halves))
    gw_out, gw_1, gw_2 = [j.reshape((-1, j.shape[2])) for j in joined]

    tail = jnp.concatenate([dbg, dgnw, dsink, loss, jnp.zeros((1, D - 256 - 128 - 8 - 1), F32)], axis=1)
    pack = jnp.concatenate([gmeta, gmix, gff, gfin, tail, dwgu[:16].reshape(4, D)], axis=0)
    tot = _allreduce_small(pack)
    g_meta = lax.dynamic_slice_in_dim(tot[0:NM], shard[0] * 256, 256, axis=1)
    g_mix, g_ff, g_fin = tot[16:17], tot[17:18], tot[18]
    g_bg, g_gnw, g_sinks, loss_tot = tot[19:20, 0:256], tot[19:20, 256:384], tot[19:20, 384:392], tot[19, 392]
    g_wgu = lax.dynamic_slice_in_dim(tot[20:24].reshape(NM, 256), shard[0] * 64, 64, axis=1)

    bo = _adamw_big([w_out[0], w_ff1[0], w_ff2[0]], [gw_out, gw_1, gw_2], [m_w_out[0], m_w_ff1[0], m_w_ff2[0]],
                    [v_w_out[0], v_w_ff1[0], v_w_ff2[0]])

    fin2 = lambda a: a.reshape(1, D)
    sw = [meta_tokens, norm_mix_w, w_gate_up[0], b_gate, gla_norm_w, sinks, norm_ff_w, fin2(final_norm_w), w_in[0].T]
    sg = [g_meta, g_mix, g_wgu, g_bg, g_gnw, g_sinks, g_ff, fin2(g_fin), gwt_in]
    sm = [m_meta_tokens, m_norm_mix_w, m_w_gate_up[0], m_b_gate, m_gla_norm_w, m_sinks, m_norm_ff_w, fin2(m_final_norm_w),
          m_w_in[0].T]
    sv = [v_meta_tokens, v_norm_mix_w, v_w_gate_up[0], v_b_gate, v_gla_norm_w, v_sinks, v_norm_ff_w, fin2(v_final_norm_w),
          v_w_in[0].T]
    so = _adamw_small(sw, sg, sm, sv)

    def ordered(small_o, big_o):
        meta_, mix_, wgu_, bg_, gnw_, sinks_, ff_, fin_, wt_ = small_o
        w_out_, w_1_, w_2_ = big_o
        return (meta_, mix_, wt_.T[None], wgu_[None], bg_, gnw_, sinks_, w_out_[None], ff_, w_1_[None], w_2_[None],
                fin_.reshape(D))

    grads = ordered(sg, [gw_out, gw_1, gw_2])
    deltas = ordered(so[0:9], bo[0:3])
    new_m = ordered(so[9:18], bo[3:6])
    new_v = ordered(so[18:27], bo[6:9])
    return (loss_tot, gx[None], *grads, *deltas, *new_m, *new_v)
```

```python
import functools
from typing import Callable, NamedTuple

import jax
import jax.numpy as jnp
import numpy as np
from jax import lax
from jax.experimental import pallas as pl
from jax.experimental.pallas import tpu as pltpu

F32 = jnp.float32
BF16 = jnp.bfloat16

D = 1024
DFF = 4096
NM = 16
TM = 256
DK = 64
CH = 128
SB = 128
EPS = 1e-5
C_GQ, C_GK, C_GV, C_GR, C_SQ, C_SK, C_SV, C_LR, DINP = 0, 256, 512, 1024, 1536, 2048, 2176, 2304, 2432
DIN = 2320
R_LR = 1536
ROPE_THETA = 500000.0
ADAM_LR, ADAM_B1, ADAM_B2, ADAM_EPS, ADAM_WD, ADAM_STEP = 0.001, 0.9, 0.999, 1e-08, 0.01, 10
NEG = -1e30
MESH = pl.DeviceIdType.MESH
VMEM_SPEC = pl.BlockSpec(memory_space=pltpu.VMEM)
ANY_SPEC = pl.BlockSpec(memory_space=pl.ANY)
SMEM_SPEC = pl.BlockSpec(memory_space=pltpu.SMEM)


def _cp(vmem_mb, sem=("arbitrary",)):
    return pltpu.CompilerParams(dimension_semantics=sem, vmem_limit_bytes=vmem_mb << 20)


def _dot(a, b):
    return jnp.dot(a, b, preferred_element_type=F32)


def _dot_nt(a, b):
    return lax.dot_general(a, b, (((1,), (1,)), ((), ())), preferred_element_type=F32)


def _dot_tn(a, b):
    return lax.dot_general(a, b, (((0,), (0,)), ((), ())), preferred_element_type=F32)


def _bf(x):
    return x.astype(BF16)


def _dot3(m01, x):
    x1 = _bf(x)
    r1 = x - x1.astype(F32)
    x2 = _bf(r1)
    x3 = _bf(r1 - x2.astype(F32))
    return _dot(m01, x1) + _dot(m01, x2) + _dot(m01, x3)


def _rms(h):
    rs = lax.rsqrt(jnp.mean(h * h, axis=-1, keepdims=True) + EPS)
    return h * rs, rs


def _rms_bwd(dy, yhat, rs, w):
    dyh = dy * w
    return rs * (dyh - yhat * jnp.mean(dyh * yhat, axis=-1, keepdims=True))


class _Comm(NamedTuple):
    ins: tuple
    outs: tuple
    sems: tuple
    phases: int
    plan: Callable
    late: int = 0


def _run_phase(fns):
    for fn in fns:
        fn()


def _call(body, name, grid, in_specs, out_specs, out_shape, scratch, params, args, comm=None):
    if comm is None:
        outs = pl.pallas_call(body, name=name, grid=grid, in_specs=in_specs, out_specs=out_specs, out_shape=out_shape,
                              scratch_shapes=scratch, compiler_params=params)(*args)
        return outs, None
    n_in, n_out, n_scr = len(in_specs), len(out_specs), len(scratch)
    ci, co = len(comm.ins), len(comm.outs)
    last = grid[0] - 1
    marks = [0, max(1, last - (comm.late or max(2, (last + 1) // 6)))][:comm.phases]

    def wrapped(*refs):
        own_in, c_in = refs[:n_in], refs[n_in:n_in + ci]
        refs = refs[n_in + ci:]
        own_out, c_out = refs[:n_out], refs[n_out:n_out + co]
        refs = refs[n_out + co:]
        own_scr, c_sem = refs[:n_scr], refs[n_scr:]
        i = pl.program_id(0)

        for p, mark in enumerate(marks):
            @pl.when(i == mark)
            def _():
                plan = comm.plan(c_in, c_out, c_sem)
                if p > 0:
                    _run_phase(plan[p - 1][1])
                _run_phase(plan[p][0])

        body(*own_in, *own_out, *own_scr)

        @pl.when(i == last)
        def _():
            _run_phase(comm.plan(c_in, c_out, c_sem)[-1][1])

    outs = pl.pallas_call(
        wrapped, name=name, grid=grid, in_specs=list(in_specs) + [ANY_SPEC] * ci, out_specs=list(out_specs) + [ANY_SPEC] * co,
        out_shape=list(out_shape) + list(comm.outs), scratch_shapes=list(scratch) + list(comm.sems), compiler_params=params,
    )(*args, *comm.ins)
    return outs[:n_out], outs[n_out:]


def _run_comm(comm, name):
    ci, co = len(comm.ins), len(comm.outs)

    def body(*refs):
        for starts, waits in comm.plan(refs[:ci], refs[ci:ci + co], refs[ci + co:]):
            _run_phase(starts)
            _run_phase(waits)

    return pl.pallas_call(body, name=name, in_specs=[ANY_SPEC] * ci, out_specs=[ANY_SPEC] * co, out_shape=list(comm.outs),
                          scratch_shapes=list(comm.sems))(*comm.ins)


def _join_shards(w3_ref, w_ref):
    for s in range(4):
        w_ref[(DIN // 4) * s:(DIN // 4) * (s + 1), :] = w3_ref[s]


def _proj_fwd(x, metapad, wm, wt3, tabs, comm=None):
    t = x.shape[0]
    nblk = t // TM

    def body(x_ref, mp_ref, wm_ref, w3_ref, tab_ref, proj_ref, w_ref):
        i = pl.program_id(0)

        @pl.when(i == 0)
        def _():
            _join_shards(w3_ref, w_ref)

        h = jnp.where(i == nblk, mp_ref[...], x_ref[...])
        u, _ = _rms(h)
        ub = _bf(u * wm_ref[...])
        proj_ref[:, 0:C_SQ] = _dot_nt(ub, w_ref[0:R_LR, :])
        att = _dot_nt(ub, w_ref[R_LR + 16:DIN, :])
        tab = tab_ref[...]
        proj_ref[:, C_SQ:C_SK] = _rope(att[:, 0:512], tab, 1.0) * 0.125
        proj_ref[:, C_SK:C_SV] = _rope(att[:, 512:640], tab, 1.0)
        proj_ref[:, C_SV:C_LR] = att[:, 640:768]
        proj_ref[:, C_LR:DINP] = jnp.zeros((TM, DINP - C_LR), F32)
        proj_ref[:, C_LR:C_LR + 16] = _dot_nt(ub, w_ref[R_LR:R_LR + 16, :])

    (proj,), got = _call(
        body, "proj_fwd", (nblk + 1,),
        [pl.BlockSpec((TM, D), lambda i: (jnp.minimum(i, nblk - 1), 0)), VMEM_SPEC, VMEM_SPEC, VMEM_SPEC,
         pl.BlockSpec((TM, 128), lambda i: (i, 0))],
        [pl.BlockSpec((TM, DINP), lambda i: (i, 0))], [jax.ShapeDtypeStruct((t + TM, DINP), F32)],
        [pltpu.VMEM((DIN, D), BF16)], _cp(48), (x, metapad, wm, wt3, tabs), comm)
    return proj, got


def _chunk_masks():
    r = lax.broadcasted_iota(jnp.int32, (TM, TM), 0)
    c = lax.broadcasted_iota(jnp.int32, (TM, TM), 1)
    same = (r // CH) == (c // CH)
    lower = _bf(jnp.where(same & (c <= r), 1.0, 0.0))
    upper = _bf(jnp.where(same & (c >= r), 1.0, 0.0))
    return lower, upper


def _gla_gate(lr, wgu, bg, valid, lower):
    z = _dot(_bf(lr), wgu) + bg
    g = (jnp.minimum(z, 0.0) - jnp.log(1.0 + jnp.exp(-jnp.abs(z)))) * (1.0 / 16.0)
    g = jnp.where(valid, g, 0.0)
    return z, _dot3(lower, g)


def _gla_decays(q, k, b):
    nc = TM // CH
    b3 = b.reshape(nc, CH, 256)
    blast = b3[:, CH - 1:CH, :]
    eb = jnp.exp(b)
    enb = jnp.exp(-b)
    ebl = jnp.exp(blast - b3).reshape(TM, 256)
    return eb, enb, ebl, jnp.exp(blast)


def _tri(lower_incl):
    r = lax.broadcasted_iota(jnp.int32, (CH, CH), 0)
    c = lax.broadcasted_iota(jnp.int32, (CH, CH), 1)
    return ((c <= r) if lower_incl else (c >= r))[None]


def _gla_fwd(proj, wgu, bg, gnw, t, comm=None):
    nblk = t // TM
    nt = nblk + 1
    nc = TM // CH

    def blk(i):
        return (i + nblk) % nt

    def body(q_ref, k_ref, v_ref, r_ref, lr_ref, wgu_ref, bg_ref, gnw_ref, o_ref, oraw_ref, sst_ref, b_ref, dgate_ref,
             st_scr):
        i = pl.program_id(0)

        @pl.when(i == 0)
        def _():
            st_scr[...] = jnp.zeros_like(st_scr)

        rows = blk(i) * TM + lax.broadcasted_iota(jnp.int32, (TM, 1), 0)
        lower, _ = _chunk_masks()
        valid = rows < t + NM
        z, b = _gla_gate(lr_ref[...], wgu_ref[...], bg_ref[...], valid, lower)
        b_ref[...] = b
        dgate_ref[...] = jnp.where(valid, (1.0 / 16.0) / (1.0 + jnp.exp(z)), 0.0)
        q = q_ref[...]
        k = k_ref[...]
        eb, enb, ebl, eblast = _gla_decays(q, k, b)
        qt = q * 0.125 * eb
        kt = k * enb
        kh = k * ebl
        tril = _tri(True)
        heads = range(4)
        hs = [slice(h * DK, (h + 1) * DK) for h in heads]
        qh = [_bf(qt[:, hs[h]]).reshape(nc, CH, DK) for h in heads]
        kth = [_bf(kt[:, hs[h]]).reshape(nc, CH, DK) for h in heads]
        khh = [_bf(kh[:, hs[h]]).reshape(nc, CH, DK) for h in heads]
        vh = [_bf(v_ref[:, h * 128:(h + 1) * 128]).reshape(nc, CH, 128) for h in heads]
        a = [jnp.einsum('cid,cjd->cij', qh[h], kth[h], preferred_element_type=F32) for h in heads]
        kv = [jnp.einsum('cjv,cjd->cvd', vh[h], khh[h], preferred_element_type=F32) for h in heads]
        o = [jnp.einsum('cij,cjv->civ', _bf(jnp.where(tril, a[h], 0.0)), vh[h], preferred_element_type=F32) for h in heads]
        states = []
        for h in heads:
            st = st_scr[h]
            per_chunk = []
            for c in range(nc):
                sst_ref[c, h] = st
                per_chunk.append(_bf(st))
                st = st * eblast[c, :, hs[h]] + kv[h][c]
            st_scr[h] = st
            states.append(per_chunk)
        o_inter = [[_dot_nt(qh[h][c], states[h][c]) for c in range(nc)] for h in heads]
        oraw = jnp.concatenate([(o[h] + jnp.stack(o_inter[h])).reshape(TM, 128) for h in heads], axis=1)
        oraw_ref[...] = oraw
        gn = gnw_ref[...]
        res = []
        for h in range(4):
            on, _ = _rms(oraw[:, h * 128:(h + 1) * 128])
            r = r_ref[:, h * 128:(h + 1) * 128]
            res.append(on * gn * (r * jax.nn.sigmoid(r)))
        o_ref[...] = _bf(jnp.concatenate(res, axis=1))

    def spec(w, cb):
        return pl.BlockSpec((TM, w), lambda i: (blk(i), cb))

    return _call(
        body, "gla_fwd", (nt,),
        [spec(256, 0), spec(256, 1), spec(512, 1), spec(512, 2), spec(128, C_LR // 128), VMEM_SPEC, VMEM_SPEC, VMEM_SPEC],
        [spec(512, 0), spec(512, 0), pl.BlockSpec((nc, 4, 128, DK), lambda i: (blk(i), 0, 0, 0)), spec(256, 0), spec(256, 0)],
        [jax.ShapeDtypeStruct((t + TM, 512), BF16), jax.ShapeDtypeStruct((t + TM, 512), F32),
         jax.ShapeDtypeStruct((nt * nc, 4, 128, DK), F32), jax.ShapeDtypeStruct((t + TM, 256), F32),
         jax.ShapeDtypeStruct((t + TM, 256), F32)],
        [pltpu.VMEM((4, 128, DK), F32)], _cp(40), (proj, proj, proj, proj, proj, wgu, bg, gnw), comm)


def _gla_bwd(proj, oraw, sst, bcum, dgate, do, wgu, gnw, t, comm=None):
    nblk = t // TM
    nt = nblk + 1
    nc = TM // CH

    def blk(i):
        return (2 * nblk - i) % nt

    def body(q_ref, k_ref, v_ref, r_ref, lr_ref, oraw_ref, sst_ref, b_ref, dgate_ref, do_ref, wgu_ref, gnw_ref,
             dgla_ref, dlr_ref, dwgu_ref, dbg_ref, dgnw_ref, dst_scr):
        i = pl.program_id(0)

        @pl.when(i == 0)
        def _():
            dst_scr[...] = jnp.zeros_like(dst_scr)
            dwgu_ref[...] = jnp.zeros_like(dwgu_ref)
            dbg_ref[...] = jnp.zeros_like(dbg_ref)
            dgnw_ref[...] = jnp.zeros_like(dgnw_ref)

        _, upper = _chunk_masks()
        lr = lr_ref[...]
        b = b_ref[...]
        q = q_ref[...]
        k = k_ref[...]
        eb, enb, ebl, eblast = _gla_decays(q, k, b)
        qt = q * 0.125 * eb
        kt = k * enb
        kh = k * ebl
        gn = gnw_ref[...]
        tril = _tri(True)
        triu = _tri(False)
        heads = range(4)
        hs = [slice(h * DK, (h + 1) * DK) for h in heads]
        vs = [slice(h * 128, (h + 1) * 128) for h in heads]
        ein = functools.partial(jnp.einsum, preferred_element_type=F32)
        dr_l, doh = [], []
        dgn = jnp.zeros((1, 128), F32)
        for h in heads:
            on, rs = _rms(oraw_ref[:, vs[h]])
            r = r_ref[:, vs[h]]
            sig = jax.nn.sigmoid(r)
            sil = r * sig
            dy = do_ref[:, vs[h]]
            dr_l.append(dy * on * gn * (sig * (1.0 + r * (1.0 - sig))))
            dgn = dgn + jnp.sum(dy * sil * on, axis=0, keepdims=True)
            doh.append(_bf(_rms_bwd(dy * sil, on, rs, gn)).reshape(nc, CH, 128))
        dgnw_ref[...] += dgn
        qh = [_bf(qt[:, hs[h]]).reshape(nc, CH, DK) for h in heads]
        kth = [_bf(kt[:, hs[h]]).reshape(nc, CH, DK) for h in heads]
        khh = [_bf(kh[:, hs[h]]).reshape(nc, CH, DK) for h in heads]
        vh = [_bf(v_ref[:, vs[h]]).reshape(nc, CH, 128) for h in heads]
        at = [ein('cjd,cid->cji', kth[h], qh[h]) for h in heads]
        da = [ein('civ,cjv->cij', doh[h], vh[h]) for h in heads]
        dat = [ein('cjv,civ->cji', vh[h], doh[h]) for h in heads]
        gq = [ein('civ,cid->cvd', doh[h], qh[h]) for h in heads]
        stf = [sst_ref[:, h] for h in heads]
        dqs = [ein('civ,cvd->cid', doh[h], _bf(stf[h])) for h in heads]
        dv = [ein('cji,civ->cjv', _bf(jnp.where(triu, at[h], 0.0)), doh[h]) for h in heads]
        dqt = [ein('cij,cjd->cid', _bf(jnp.where(tril, da[h], 0.0)), kth[h]) + dqs[h] for h in heads]
        dkt = [ein('cji,cid->cjd', _bf(jnp.where(triu, dat[h], 0.0)), qh[h]) for h in heads]
        dse = []
        for h in heads:
            dst = dst_scr[h]
            dsend = [None] * nc
            for c in reversed(range(nc)):
                dsend[c] = dst
                dst = dst * eblast[c, :, hs[h]] + gq[h][c]
            dst_scr[h] = dst
            dse.append(jnp.stack(dsend))
        dseb = [_bf(d) for d in dse]
        dv = [dv[h] + ein('cjd,cvd->cjv', khh[h], dseb[h]) for h in heads]
        dkh = [ein('cjv,cvd->cjd', vh[h], dseb[h]) for h in heads]
        carried = jnp.concatenate([jnp.sum(dse[h] * stf[h], axis=1, keepdims=True) for h in heads], axis=2)
        wide = lambda parts: jnp.concatenate([p.reshape(TM, DK) for p in parts], axis=1)
        dqt_w, dkt_w, dkh_w = wide(dqt), wide(dkt), wide(dkh)
        dkh_kh = dkh_w * kh
        extra = jnp.sum(dkh_kh.reshape(nc, CH, 256), axis=1, keepdims=True) + eblast * carried
        db = dqt_w * qt - dkt_w * kt - dkh_kh
        dg = _dot3(upper, db) + jnp.broadcast_to(extra, (nc, CH, 256)).reshape(TM, 256)
        dz = dg * dgate_ref[...]
        dzb = _bf(dz)
        dlr_ref[...] = _bf(_dot_nt(dzb, wgu_ref[...]))
        dwgu_ref[...] += _dot_tn(_bf(lr), dzb)
        dbg_ref[...] += jnp.sum(dz, axis=0, keepdims=True)
        dq = dqt_w * eb * 0.125
        dk = dkt_w * enb + dkh_w * ebl
        dgla_ref[...] = _bf(jnp.concatenate([dq, dk] + [d.reshape(TM, 128) for d in dv] + dr_l, axis=1))

    def spec(w, cb):
        return pl.BlockSpec((TM, w), lambda i: (blk(i), cb))

    def acc(shape):
        return pl.BlockSpec(shape, lambda i: (0, 0))

    return _call(
        body, "gla_bwd", (nt,),
        [spec(256, 0), spec(256, 1), spec(512, 1), spec(512, 2), spec(128, C_LR // 128), spec(512, 0),
         pl.BlockSpec((nc, 4, 128, DK), lambda i: (blk(i), 0, 0, 0)), spec(256, 0), spec(256, 0), spec(512, 0),
         VMEM_SPEC, VMEM_SPEC],
        [spec(1536, 0), spec(128, 0), acc((128, 256)), acc((1, 256)), acc((1, 128))],
        [jax.ShapeDtypeStruct((t + TM, 1536), BF16), jax.ShapeDtypeStruct((t + TM, 128), BF16),
         jax.ShapeDtypeStruct((128, 256), F32), jax.ShapeDtypeStruct((1, 256), F32), jax.ShapeDtypeStruct((1, 128), F32)],
        [pltpu.VMEM((4, 128, DK), F32)], _cp(48), (proj, proj, proj, proj, proj, oraw, sst, bcum, dgate, do, wgu, gnw), comm)


def _rope_tables(t):
    r = t + TM
    row = np.arange(r)
    pos = np.where(row < t, row + NM, np.where(row < t + NM, row - t, 0)).astype(np.float32)
    inv_freq = (1.0 / (np.float32(ROPE_THETA) ** (np.arange(0, 16, 2, dtype=np.float32) / np.float32(16)))).astype(np.float32)
    ang = (pos[:, None] * inv_freq[None, :]).astype(np.float32)
    cos, sin = np.cos(ang).astype(np.float32), np.sin(ang).astype(np.float32)
    one, zero = np.ones((r, 48), np.float32), np.zeros((r, 48), np.float32)
    return jnp.asarray(np.concatenate([cos, cos, one, -sin, sin, zero], axis=1))


def _rope(x, tab, sign):
    w = x.shape[1]
    rep = w // 64
    c = jnp.concatenate([tab[:, 0:64]] * rep, axis=1)
    s = jnp.concatenate([tab[:, 64:128]] * rep, axis=1)
    lane = lax.rem(lax.broadcasted_iota(jnp.int32, x.shape, 1), 64)
    partner = jnp.where(lane < 8, pltpu.roll(x, w - 8, 1), jnp.where(lane < 16, pltpu.roll(x, 8, 1), 0.0))
    return x * c + sign * (partner * s)


HB_BWD = 4


def _stack(x, hg):
    w = x.shape[1] // hg
    return x if hg == 1 else jnp.concatenate([x[:, g * w:(g + 1) * w] for g in range(hg)], axis=0)


def _unstack(x, hg):
    return x if hg == 1 else jnp.concatenate([x[g * SB:(g + 1) * SB] for g in range(hg)], axis=1)


def _swa_steps(r_tot):
    blocks = r_tot // SB
    return next(n for n in (6, 3, 2) if blocks % n == 0 and blocks // n >= 2)


def _swa_specs(nsb, nbq):
    def rows(h, w, cb, f):
        return pl.BlockSpec((h, w), lambda i: (f(i), cb))
    pair = lambda i: i
    prev = lambda i: jnp.maximum(nbq * i - 1, 0)
    meta = lambda i: nsb
    return rows, pair, prev, meta


def _swa_fwd(proj, sinks, t, comm=None):
    nsb = t // SB
    r_tot = t + TM
    nbq = _swa_steps(r_tot)
    qb = nbq * SB
    rows, pair, prev, meta = _swa_specs(nsb, nbq)

    def body(sink_ref, q_ref, kc_ref, kp_ref, km_ref, vc_ref, vp_ref, vm_ref, o_ref, lse_ref):
        i = pl.program_id(0)
        key = lax.broadcasted_iota(jnp.int32, (SB, SB), 0)
        qry = lax.broadcasted_iota(jnp.int32, (SB, SB), 1)
        km, vm = km_ref[0:NM, :], vm_ref[0:NM, :]
        for j in range(nbq):
            b = nbq * i + j
            rs = slice(j * SB, (j + 1) * SB)
            before = slice((j - 1) * SB, j * SB)
            real = b < nsb
            masks = (key <= qry, (key > qry) & (b > 0) & real, real)
            k3 = (kc_ref[rs, :], kp_ref[...] if j == 0 else kc_ref[before, :], km)
            v3 = (vc_ref[rs, :], vp_ref[...] if j == 0 else vc_ref[before, :], vm)
            valid = b * SB + lax.broadcasted_iota(jnp.int32, (1, SB), 1) < t + NM
            heads = range(8)
            kb = [[_bf(k[:, kv * 64:(kv + 1) * 64]) for k in k3] for kv in range(2)]
            vt = [[_bf(v[:, kv * 64:(kv + 1) * 64].T) for v in v3] for kv in range(2)]
            raw = [[_dot_nt(k, _bf(q_ref[rs, h * 64:(h + 1) * 64])) for k in kb[h // 4]] for h in heads]
            probs, inv_l, lse_l = [], [], []
            for h in heads:
                s = [jnp.where(m, sx, NEG) for m, sx in zip(masks, raw[h])]
                sink = sink_ref[0, h]
                top = jnp.maximum(jnp.max(jnp.maximum(s[0], s[1]), axis=0, keepdims=True),
                                  jnp.maximum(jnp.max(s[2], axis=0, keepdims=True), sink))
                p = [jnp.exp(sx - top) for sx in s]
                l = (jnp.sum(p[0] + p[1], axis=0, keepdims=True) + jnp.sum(p[2], axis=0, keepdims=True)
                     + jnp.exp(sink - top))
                probs.append([_bf(px) for px in p])
                inv_l.append(1.0 / l)
                lse_l.append(top + jnp.log(l))
            o_t = [_dot(vt[h // 4][0], probs[h][0]) + _dot(vt[h // 4][1], probs[h][1]) + _dot(vt[h // 4][2], probs[h][2])
                   for h in heads]
            o_ref[rs, :] = _bf(jnp.concatenate([jnp.where(valid, o_t[h] * inv_l[h], 0.0).T for h in heads], axis=1))
            lse_ref[:, rs] = jnp.concatenate(lse_l, axis=0)

    ck, cv = C_SK // 128, C_SV // 128
    return _call(
        body, "swa_fwd", (r_tot // qb,),
        [SMEM_SPEC, rows(qb, 512, C_SQ // 512, pair),
         rows(qb, 128, ck, pair), rows(SB, 128, ck, prev), rows(SB, 128, ck, meta),
         rows(qb, 128, cv, pair), rows(SB, 128, cv, prev), rows(SB, 128, cv, meta)],
        [rows(qb, 512, 0, pair), pl.BlockSpec((8, qb), lambda i: (0, i))],
        [jax.ShapeDtypeStruct((r_tot, 512), BF16), jax.ShapeDtypeStruct((8, r_tot), F32)],
        [], _cp(32), (sinks, proj, proj, proj, proj, proj, proj, proj), comm)


def _swa_bwd(proj, sinks, lse_t, do, t, comm=None):
    nsb = t // SB
    r_tot = t + TM
    nbq = _swa_steps(r_tot)
    qb = nbq * SB
    rows, pair, prev, meta = _swa_specs(nsb, nbq)
    hb = HB_BWD
    lanes = hb * SB

    def body(sink_ref, q_ref, kc_ref, kp_ref, km_ref, vc_ref, vp_ref, vm_ref, lse_ref, do_ref,
             dq_ref, dk_ref, dv_ref, dsink_ref):
        i = pl.program_id(0)

        @pl.when(i == 0)
        def _():
            dk_ref[...] = jnp.zeros_like(dk_ref)
            dv_ref[...] = jnp.zeros_like(dv_ref)
            dsink_ref[...] = jnp.zeros_like(dsink_ref)

        key = lax.broadcasted_iota(jnp.int32, (SB, lanes), 0)
        qry = lax.rem(lax.broadcasted_iota(jnp.int32, (SB, lanes), 1), SB)
        km, vm = km_ref[0:NM, :], vm_ref[0:NM, :]
        dsink_l = []
        for j in range(nbq):
            b = nbq * i + j
            rs = slice(j * SB, (j + 1) * SB)
            before = slice((j - 1) * SB, j * SB)
            real = b < nsb
            masks = (key <= qry, (key > qry) & (b > 0) & real, real)
            k3 = (kc_ref[rs, :], kp_ref[...] if j == 0 else kc_ref[before, :], km)
            v3 = (vc_ref[rs, :], vp_ref[...] if j == 0 else vc_ref[before, :], vm)
            groups = list(range(0, 8, hb))
            kvs = [h0 // 4 for h0 in groups]
            qg = [_bf(_stack(q_ref[rs, h0 * 64:(h0 + hb) * 64], hb)) for h0 in groups]
            dog = [_bf(_stack(do_ref[rs, h0 * 64:(h0 + hb) * 64], hb)) for h0 in groups]
            kb = [[_bf(k[:, kv * 64:(kv + 1) * 64]) for k in k3] for kv in kvs]
            vb = [[_bf(v[:, kv * 64:(kv + 1) * 64]) for v in v3] for kv in kvs]
            s = [[_dot_nt(k, qg[g]) for k in kb[g]] for g in range(len(groups))]
            dp = [[_dot_nt(v, dog[g]) for v in vb[g]] for g in range(len(groups))]
            p, ds, ds_blk = [], [], []
            for g, h0 in enumerate(groups):
                lse_row = jnp.concatenate([lse_ref[h:h + 1, rs] for h in range(h0, h0 + hb)], axis=1)
                sink_row = jnp.concatenate([jnp.full((1, SB), sink_ref[0, h], F32) for h in range(h0, h0 + hb)], axis=1)
                pg = [jnp.exp(jnp.where(m, sx, NEG) - lse_row) for m, sx in zip(masks, s[g])]
                delta = (jnp.sum(pg[0] * dp[g][0] + pg[1] * dp[g][1], axis=0, keepdims=True)
                         + jnp.sum(pg[2] * dp[g][2], axis=0, keepdims=True))
                ds.append([_bf(pp * (dd - delta)) for pp, dd in zip(pg, dp[g])])
                p.append([_bf(pp) for pp in pg])
                ds_row = -jnp.exp(sink_row - lse_row) * delta
                ds_blk += [jnp.sum(ds_row[:, q0 * SB:(q0 + 1) * SB], axis=1, keepdims=True) for q0 in range(hb)]
            dsink_l.append(jnp.concatenate(ds_blk, axis=1))
            dq_t = [_dot_tn(kb[g][0], ds[g][0]) + _dot_tn(kb[g][1], ds[g][1]) + _dot_tn(kb[g][2], ds[g][2])
                    for g in range(len(groups))]
            dq_ref[rs, :] = jnp.concatenate([_unstack(d.T, hb) for d in dq_t], axis=1)
            windows = (pl.ds(pl.multiple_of(b * SB, SB), SB), pl.ds(pl.multiple_of(jnp.maximum(b - 1, 0) * SB, SB), SB),
                       pl.ds(t, NM))
            for x in range(3):
                dk_kv, dv_kv = [], []
                for kv in range(2):
                    mine = [g for g in range(len(groups)) if kvs[g] == kv]
                    dk_kv.append(sum(_dot(ds[g][x], qg[g]) for g in mine))
                    dv_kv.append(sum(_dot(p[g][x], dog[g]) for g in mine))
                dk_ref[windows[x], :] += jnp.concatenate(dk_kv, axis=1)
                dv_ref[windows[x], :] += jnp.concatenate(dv_kv, axis=1)
        dsink_ref[...] += sum(dsink_l)

    ck, cv = C_SK // 128, C_SV // 128
    whole = lambda w: pl.BlockSpec((r_tot, w), lambda i: (0, 0))
    return _call(
        body, "swa_bwd", (r_tot // qb,),
        [SMEM_SPEC, rows(qb, 512, C_SQ // 512, pair),
         rows(qb, 128, ck, pair), rows(SB, 128, ck, prev), rows(SB, 128, ck, meta),
         rows(qb, 128, cv, pair), rows(SB, 128, cv, prev), rows(SB, 128, cv, meta),
         pl.BlockSpec((8, qb), lambda i: (0, i)), rows(qb, 512, 1, pair)],
        [rows(qb, 512, 0, pair), whole(128), whole(128), pl.BlockSpec((1, 8), lambda i: (0, 0))],
        [jax.ShapeDtypeStruct((r_tot, 512), F32), jax.ShapeDtypeStruct((r_tot, 128), F32),
         jax.ShapeDtypeStruct((r_tot, 128), F32), jax.ShapeDtypeStruct((1, 8), F32)],
        [], _cp(48), (sinks, proj, proj, proj, proj, proj, proj, proj, lse_t, do), comm)


HK = D // 2
MLP_FWD_ROWS = 512


def _mlp_fwd(x, tgt, ogla, oswa, wo, wff, w1, w2, wfin):
    t = x.shape[0]
    tm = MLP_FWD_ROWS if t % MLP_FWD_ROWS == 0 else TM

    def body(x_ref, tgt_ref, og_ref, os_ref, wo_ref, wff_ref, w1a_ref, w1b_ref, w2a_ref, w2b_ref, wfin_ref,
             h1_ref, f_ref, a_ref, dh2_ref, loss_ref, gfin_ref):
        i = pl.program_id(0)

        @pl.when(i == 0)
        def _():
            loss_ref[...] = jnp.zeros_like(loss_ref)
            gfin_ref[...] = jnp.zeros_like(gfin_ref)

        h1 = x_ref[...] + _dot(og_ref[...], wo_ref[0:512, :]) + _dot(os_ref[...], wo_ref[512:1024, :])
        h1_ref[...] = h1
        fh, _ = _rms(h1)
        f = _bf(fh * wff_ref[...])
        f_ref[...] = f
        acc = jnp.zeros((tm, D), F32)
        for n in range(4):
            a = _dot(f[:, 0:HK], w1a_ref[n]) + _dot(f[:, HK:D], w1b_ref[n])
            a_ref[:, n * D:(n + 1) * D] = _bf(a)
            zr = jnp.maximum(a, 0.0)
            z = _bf(zr * zr)
            acc = acc + _dot(z[:, 0:HK], w2a_ref[n]) + _dot(z[:, HK:D], w2b_ref[n])
        h2 = h1 + acc
        yh, rs2 = _rms(h2)
        wf = wfin_ref[...]
        e = yh * wf - tgt_ref[...]
        loss_ref[...] += jnp.sum(jnp.sum(e * e, axis=0, keepdims=True), axis=1, keepdims=True) * (0.5 / D)
        dy = e * (1.0 / D)
        gfin_ref[...] += jnp.sum(dy * yh, axis=0, keepdims=True)
        dh2_ref[...] = _rms_bwd(dy, yh, rs2, wf)

    rs = lambda w: pl.BlockSpec((tm, w), lambda i: (i, 0))
    return pl.pallas_call(
        body, name="mlp_fwd", grid=(t // tm,),
        in_specs=[rs(D), rs(D), rs(512), rs(512)] + [VMEM_SPEC] * 7,
        out_specs=[rs(D), rs(D), rs(DFF), rs(D), pl.BlockSpec((1, 1), lambda i: (0, 0)), pl.BlockSpec((1, D), lambda i: (0, 0))],
        out_shape=[jax.ShapeDtypeStruct((t, D), F32), jax.ShapeDtypeStruct((t, D), BF16),
                   jax.ShapeDtypeStruct((t, DFF), BF16), jax.ShapeDtypeStruct((t, D), F32),
                   jax.ShapeDtypeStruct((1, 1), F32), jax.ShapeDtypeStruct((1, D), F32)],
        compiler_params=_cp(60),
    )(x, tgt, ogla, oswa, wo, wff, *w1, *w2, wfin)


def _mlp_bwd(h1, a, dh2, ogla, oswa, wo, wff, w1, w2):
    t = h1.shape[0]
    tm = MLP_FWD_ROWS if t % MLP_FWD_ROWS == 0 else TM
    nb = t // tm

    def body(h1_ref, a_ref, dh2_ref, og_ref, os_ref, wo_ref, wff_ref, w1a_ref, w1b_ref, w2a_ref, w2b_ref,
             da_ref, dh2b_ref, dh1_ref, do_ref, dwo_acc, gff_ref):
        i = pl.program_id(0)

        @pl.when(i == 0)
        def _():
            dwo_acc[...] = jnp.zeros_like(dwo_acc)
            gff_ref[...] = jnp.zeros_like(gff_ref)

        @pl.when(i < nb)
        def _():
            dh2 = dh2_ref[...]
            dh2b = _bf(dh2)
            dh2b_ref[...] = dh2b
            dfa = jnp.zeros((tm, HK), F32)
            dfb = jnp.zeros((tm, HK), F32)
            for n in range(4):
                dz = jnp.concatenate([_dot_nt(dh2b, w2a_ref[n]), _dot_nt(dh2b, w2b_ref[n])], axis=1)
                da = _bf(dz * (2.0 * jnp.maximum(a_ref[:, n * D:(n + 1) * D].astype(F32), 0.0)))
                da_ref[:, n * D:(n + 1) * D] = da
                dfa = dfa + _dot_nt(da, w1a_ref[n])
                dfb = dfb + _dot_nt(da, w1b_ref[n])
            df = jnp.concatenate([dfa, dfb], axis=1)
            fh, rs1 = _rms(h1_ref[...])
            gff_ref[...] += jnp.sum(df * fh, axis=0, keepdims=True)
            dh1 = dh2 + _rms_bwd(df, fh, rs1, wff_ref[...])
            dh1_ref[...] = dh1
            dh1b = _bf(dh1)
            do_ref[...] = _dot_nt(dh1b, wo_ref[...])
            dwo_acc[0:512, :] += _dot_tn(og_ref[...], dh1b)
            dwo_acc[512:1024, :] += _dot_tn(os_ref[...], dh1b)

        @pl.when(i == nb)
        def _():
            dh1_ref[...] = jnp.zeros_like(dh1_ref)
            do_ref[...] = jnp.zeros_like(do_ref)

    rs = lambda w: pl.BlockSpec((tm, w), lambda i: (i, 0))
    real = lambda w: pl.BlockSpec((tm, w), lambda i: (jnp.minimum(i, nb - 1), 0))
    return pl.pallas_call(
        body, name="mlp_bwd", grid=(nb + 1,),
        in_specs=[real(D), real(DFF), real(D), real(512), real(512)] + [VMEM_SPEC] * 6,
        out_specs=[real(DFF), real(D), rs(D), rs(D), VMEM_SPEC, pl.BlockSpec((1, D), lambda i: (0, 0))],
        out_shape=[jax.ShapeDtypeStruct((t, DFF), BF16), jax.ShapeDtypeStruct((t, D), BF16),
                   jax.ShapeDtypeStruct((t + tm, D), F32), jax.ShapeDtypeStruct((t + tm, D), F32),
                   jax.ShapeDtypeStruct((D, D), F32), jax.ShapeDtypeStruct((1, D), F32)],
        compiler_params=_cp(62),
    )(h1, a, dh2, ogla, oswa, wo, wff, *w1, *w2)


def _ffn_wgrad(f, a, da, dh2b):
    rows = f.shape[0]
    kt = 2048 if rows % 2048 == 0 else TM
    nk = rows // kt

    def body(f_ref, a_ref, da_ref, dh2_ref, dw1_ref, dw2_ref, acc1, acc2):
        k = pl.program_id(1)

        @pl.when(k == 0)
        def _():
            acc1[...] = jnp.zeros_like(acc1)
            acc2[...] = jnp.zeros_like(acc2)

        zr = jnp.maximum(a_ref[...], 0.0)
        acc1[...] += _dot_tn(f_ref[...], da_ref[...])
        acc2[...] += _dot_tn(zr * zr, dh2_ref[...])

        @pl.when(k == nk - 1)
        def _():
            for hh in range(2):
                dw1_ref[hh, 0] = acc1[hh * 512:(hh + 1) * 512, :]
                dw2_ref[hh, 0] = acc2[hh * 512:(hh + 1) * 512, :]

    out = pl.BlockSpec((2, 1, 512, D), lambda n, k: (0, n, 0, 0))
    return pl.pallas_call(
        body, name="ffn_wgrad", grid=(4, nk),
        in_specs=[pl.BlockSpec((kt, D), lambda n, k: (k, 0)), pl.BlockSpec((kt, D), lambda n, k: (k, n)),
                  pl.BlockSpec((kt, D), lambda n, k: (k, n)), pl.BlockSpec((kt, D), lambda n, k: (k, 0))],
        out_specs=[out, out],
        out_shape=[jax.ShapeDtypeStruct((2, 4, 512, D), F32)] * 2,
        scratch_shapes=[pltpu.VMEM((D, D), F32), pltpu.VMEM((D, D), F32)],
        compiler_params=_cp(62, ("arbitrary", "arbitrary")),
    )(f, a, da, dh2b)


def _proj_bwd(x, metapad, wm, wt3, tabs, dgla, dswa_q, dsk, dsv, dlr, dh1, comm=None):
    t = x.shape[0]
    nblk = t // TM

    def body(x_ref, mp_ref, wm_ref, w3_ref, tab_ref, dg_ref, dq_ref, dk_ref, dv_ref, dlr_ref, dh1_ref,
             gx_ref, gmeta_ref, dw_ref, gmix_ref, w_ref, acc):
        i = pl.program_id(0)

        @pl.when(i == 0)
        def _():
            _join_shards(w3_ref, w_ref)
            acc[...] = jnp.zeros_like(acc)
            gmix_ref[...] = jnp.zeros_like(gmix_ref)

        h = jnp.where(i == nblk, mp_ref[...], x_ref[...])
        uh, rs = _rms(h)
        wm_v = wm_ref[...]
        u = _bf(uh * wm_v)
        tab = tab_ref[...]
        dq = _bf(_rope(dq_ref[...] * 0.125, tab, -1.0))
        dk = _bf(_rope(dk_ref[...], tab, -1.0))
        parts = ((dg_ref[...], 0, R_LR), (dlr_ref[:, 0:16], R_LR, 16), (dq, R_LR + 16, 512),
                 (dk, R_LR + 528, 128), (_bf(dv_ref[...]), R_LR + 656, 128))
        du = jnp.zeros((TM, D), F32)
        for val, r0, w in parts:
            du = du + _dot(val, w_ref[r0:r0 + w, :])
            acc[r0:r0 + w, :] += _dot_tn(val, u)
        gmix_ref[...] += jnp.sum(du * uh, axis=0, keepdims=True)
        dh0 = dh1_ref[...] + _rms_bwd(du, uh, rs, wm_v)

        @pl.when(i < nblk)
        def _():
            gx_ref[...] = dh0

        @pl.when(i == nblk)
        def _():
            gmeta_ref[...] = dh0[:NM]
            for s in range(4):
                dw_ref[s] = acc[(DIN // 4) * s:(DIN // 4) * (s + 1), :]

    xs = pl.BlockSpec((TM, D), lambda i: (jnp.minimum(i, nblk - 1), 0))
    rs_ = lambda w: pl.BlockSpec((TM, w), lambda i: (i, 0))
    return _call(
        body, "proj_bwd", (nblk + 1,),
        [xs, VMEM_SPEC, VMEM_SPEC, VMEM_SPEC, rs_(128), rs_(1536), rs_(512), rs_(128), rs_(128), rs_(128), rs_(D)],
        [xs, pl.BlockSpec((NM, D), lambda i: (0, 0)), VMEM_SPEC, pl.BlockSpec((1, D), lambda i: (0, 0))],
        [jax.ShapeDtypeStruct((t, D), F32), jax.ShapeDtypeStruct((NM, D), F32),
         jax.ShapeDtypeStruct((4, DIN // 4, D), F32), jax.ShapeDtypeStruct((1, D), F32)],
        [pltpu.VMEM((DIN, D), BF16), pltpu.VMEM((DIN, D), F32)], _cp(56),
        (x, metapad, wm, wt3, tabs, dgla, dswa_q, dsk, dsv, dlr, dh1), comm)


def _place():
    return lax.axis_index("x"), lax.axis_index("y"), lax.axis_index("c")


def _other_chips(x, y):
    return [(1 - x, y), (x, 1 - y), (1 - x, 1 - y)]


def _dma_sems(*counts):
    return tuple(pltpu.SemaphoreType.DMA((k,)) for k in counts)


def _gather_shards(shards, split):
    n = len(shards)
    two = [a for a in range(n) if split[a]]

    def plan(ins, outs, sems):
        isend, irecv, dsend, drecv, loc = sems
        x, y, c = _place()
        chips = _other_chips(x, y)

        def part(ref, a, half):
            if not split[a]:
                return ref
            w = shards[a].shape[1] // 2
            return ref.at[:, pl.ds(pl.multiple_of(half * w, 128), w)]

        def over_ici(a, k, shard_of):
            tx, ty = chips[k]
            sx, sy = shard_of
            return pltpu.make_async_remote_copy(
                src_ref=part(ins[a], a, c), dst_ref=part(outs[a].at[2 * sx + sy], a, c), send_sem=isend.at[3 * a + k],
                recv_sem=irecv.at[3 * a + k], device_id=(tx, ty, c), device_id_type=MESH)

        def over_d2d(a, k, half):
            tx, ty = chips[k]
            ref = part(outs[a].at[2 * tx + ty], a, half)
            return pltpu.make_async_remote_copy(
                src_ref=ref, dst_ref=ref, send_sem=dsend.at[3 * a + k], recv_sem=drecv.at[3 * a + k],
                device_id=(x, y, 1 - c), device_id_type=MESH)

        def local(a):
            return pltpu.make_async_copy(ins[a], outs[a].at[2 * x + y], loc.at[a])

        pairs = [(a, k) for a in range(n) for k in range(3)]
        first = ([lambda a=a: local(a).start() for a in range(n)]
                 + [lambda a=a, k=k: over_ici(a, k, (x, y)).start() for a, k in pairs],
                 [lambda a=a, k=k: over_ici(a, k, chips[k]).wait_recv() for a, k in pairs]
                 + [lambda a=a, k=k: over_ici(a, k, (x, y)).wait_send() for a, k in pairs]
                 + [lambda a=a: local(a).wait() for a in range(n)])
        pairs2 = [(a, k) for a in two for k in range(3)]
        second = ([lambda a=a, k=k: over_d2d(a, k, c).start() for a, k in pairs2],
                  [lambda a=a, k=k: over_d2d(a, k, 1 - c).wait_recv() for a, k in pairs2]
                  + [lambda a=a, k=k: over_d2d(a, k, c).wait_send() for a, k in pairs2])
        return [first, second] if two else [first]

    return _Comm(tuple(shards), tuple(jax.ShapeDtypeStruct((4,) + s.shape, s.dtype) for s in shards),
                 _dma_sems(3 * n, 3 * n, 3 * n, 3 * n, n), 2 if two else 1, plan)


def _swap_halves(grads):
    n = len(grads)

    def plan(ins, outs, sems):
        send, recv = sems
        x, y, c = _place()

        def swap(a):
            return pltpu.make_async_remote_copy(
                src_ref=ins[a].at[1 - c], dst_ref=outs[a], send_sem=send.at[a], recv_sem=recv.at[a],
                device_id=(x, y, 1 - c), device_id_type=MESH)

        return [([lambda a=a: swap(a).start() for a in range(n)], [lambda a=a: swap(a).wait() for a in range(n)])]

    return _Comm(tuple(grads), tuple(jax.ShapeDtypeStruct(g.shape[1:], g.dtype) for g in grads), _dma_sems(n, n), 1, plan)


SCATTER_ADD_ROWS = 128


def _scatter_shards(parts, late):
    n = len(parts)

    def plan(ins, outs, sems):
        send, recv, loc = sems[:3]
        onward, got = sems[3:3 + n], sems[3 + n:]
        x, y, c = _place()
        x_first = c == 0
        near = (jnp.where(x_first, 1 - x, x), jnp.where(x_first, y, 1 - y))
        far = (jnp.where(x_first, x, 1 - x), jnp.where(x_first, 1 - y, y))
        diagonal = (1 - x, 1 - y)
        shard = lambda chip: 2 * chip[0] + chip[1]

        def hop(src, dst, k, chip):
            return pltpu.make_async_remote_copy(src_ref=src, dst_ref=dst, send_sem=send.at[k], recv_sem=recv.at[k],
                                                device_id=(chip[0], chip[1], c), device_id_type=MESH)

        theirs = lambda a: hop(ins[a].at[shard(near)], outs[a].at[0], 3 * a, near)
        passing = lambda a: hop(ins[a].at[shard(diagonal)], got[a], 3 * a + 1, near)
        summed = lambda a: hop(onward[a], outs[a].at[1], 3 * a + 2, far)
        mine = lambda a: pltpu.make_async_copy(ins[a].at[shard(far)], onward[a], loc.at[a])

        def add(a):
            for r in range(0, parts[a].shape[1], SCATTER_ADD_ROWS):
                rows = slice(r, r + SCATTER_ADD_ROWS)
                onward[a][rows, :] = _bf(onward[a][rows, :].astype(F32) + got[a][rows, :].astype(F32))

        every = range(n)
        first = ([lambda a=a: mine(a).start() for a in every] + [lambda a=a: passing(a).start() for a in every]
                 + [lambda a=a: theirs(a).start() for a in every],
                 [lambda a=a: mine(a).wait() for a in every] + [lambda a=a: passing(a).wait_recv() for a in every]
                 + [lambda a=a: add(a) for a in every])
        second = ([lambda a=a: summed(a).start() for a in every],
                  [lambda a=a: passing(a).wait_send() for a in every] + [lambda a=a: theirs(a).wait() for a in every]
                  + [lambda a=a: summed(a).wait() for a in every])
        return [first, second]

    assert all(p.shape[1] % SCATTER_ADD_ROWS == 0 for p in parts)
    buffers = [pltpu.VMEM(p.shape[1:], p.dtype) for p in parts]
    return _Comm(tuple(parts), tuple(jax.ShapeDtypeStruct((2,) + p.shape[1:], p.dtype) for p in parts),
                 _dma_sems(3 * n, 3 * n, n) + tuple(buffers) * 2, 2, plan, late)


def _join_halves(halves):
    n = len(halves)

    def plan(ins, outs, sems):
        send, recv, loc = sems
        x, y, c = _place()

        def remote(a, half):
            return pltpu.make_async_remote_copy(
                src_ref=ins[a], dst_ref=outs[a].at[half], send_sem=send.at[a], recv_sem=recv.at[a],
                device_id=(x, y, 1 - c), device_id_type=MESH)

        def local(a):
            return pltpu.make_async_copy(ins[a], outs[a].at[c], loc.at[a])

        every = range(n)
        return [([lambda a=a: local(a).start() for a in every] + [lambda a=a: remote(a, c).start() for a in every],
                 [lambda a=a: remote(a, 1 - c).wait_recv() for a in every]
                 + [lambda a=a: remote(a, c).wait_send() for a in every] + [lambda a=a: local(a).wait() for a in every])]

    return _Comm(tuple(halves), tuple(jax.ShapeDtypeStruct((2,) + h.shape, h.dtype) for h in halves),
                 _dma_sems(n, n, n), 1, plan)


def _reduce_w_in(dwt, comm):
    rows, hw = DIN // 4, D // 2
    ci, co = len(comm.ins), len(comm.outs)

    def body(*refs):
        dw_ref, c_in, out_ref, c_out = refs[0], refs[1:1 + ci], refs[1 + ci], refs[2 + ci:2 + ci + co]
        mine, sib, tosend, rbuf, qbuf, full, send, recv, loc = refs[2 + ci + co:11 + ci + co]
        c_sem = refs[11 + ci + co:]
        x, y, c = _place()
        sibling = (x, y, 1 - c)
        (starts, waits), = comm.plan(c_in, c_out, c_sem)
        _run_phase(starts)

        def cols(ref, half):
            window = pl.ds(pl.multiple_of(half * hw, 128), hw)
            return ref.at[:, :, window] if len(ref.shape) == 3 else ref.at[:, window]

        load = pltpu.make_async_copy(cols(dw_ref, c), mine, loc.at[0])
        give = pltpu.make_async_remote_copy(src_ref=cols(dw_ref, 1 - c), dst_ref=sib, send_sem=send.at[3], recv_sem=recv.at[3],
                                            device_id=sibling, device_id_type=MESH)
        load.start()
        give.start()
        load.wait()
        give.wait()
        mine[...] = mine[...] + sib[...]
        cps = []
        for k, (tx, ty) in enumerate(_other_chips(x, y)):
            tosend[k] = _bf(mine[2 * tx + ty])
            cps.append(pltpu.make_async_remote_copy(
                src_ref=tosend.at[k], dst_ref=rbuf.at[k], send_sem=send.at[k], recv_sem=recv.at[k],
                device_id=(tx, ty, c), device_id_type=MESH))
            cps[-1].start()
        for cp in cps:
            cp.wait()
        qbuf[...] = mine[2 * x + y] + rbuf[0].astype(F32) + rbuf[1].astype(F32) + rbuf[2].astype(F32)
        keep = pltpu.make_async_copy(qbuf, cols(full, c), loc.at[1])
        pass_on = pltpu.make_async_remote_copy(src_ref=qbuf, dst_ref=cols(full, c), send_sem=send.at[4], recv_sem=recv.at[4],
                                               device_id=sibling, device_id_type=MESH)
        keep.start()
        pass_on.start()
        keep.wait()
        pass_on.wait_send()
        pltpu.make_async_remote_copy(src_ref=qbuf, dst_ref=cols(full, 1 - c), send_sem=send.at[4], recv_sem=recv.at[4],
                                     device_id=sibling, device_id_type=MESH).wait_recv()
        out_ref[...] = full[...]
        _run_phase(waits)

    outs = pl.pallas_call(
        body, name="reduce_w_in",
        in_specs=[ANY_SPEC] * (1 + ci), out_specs=[VMEM_SPEC] + [ANY_SPEC] * co,
        out_shape=[jax.ShapeDtypeStruct((rows, D), F32)] + list(comm.outs),
        scratch_shapes=[pltpu.VMEM((4, rows, hw), F32), pltpu.VMEM((4, rows, hw), F32), pltpu.VMEM((3, rows, hw), BF16),
                        pltpu.VMEM((3, rows, hw), BF16), pltpu.VMEM((rows, hw), F32), pltpu.VMEM((rows, D), F32),
                        *_dma_sems(5, 5, 2), *comm.sems],
        compiler_params=pltpu.CompilerParams(vmem_limit_bytes=48 << 20),
    )(dwt, *comm.ins)
    return outs[0], outs[1:]


def _allreduce_small(pack):
    p = pack.shape[0]

    def body(in_ref, out_ref, buf, send, recv):
        x, y, c = _place()
        me = 4 * x + 2 * y + c
        buf[me] = in_ref[...]

        def peer_of(k):
            return x ^ (k >> 2), y ^ ((k >> 1) & 1), c ^ (k & 1)

        sends = [pltpu.make_async_remote_copy(
            src_ref=in_ref, dst_ref=buf.at[me], send_sem=send.at[k - 1], recv_sem=recv.at[k - 1],
            device_id=peer_of(k), device_id_type=MESH) for k in range(1, 8)]
        for cp in sends:
            cp.start()
        for k in range(1, 8):
            px, py, pc = peer_of(k)
            pltpu.make_async_remote_copy(
                src_ref=in_ref, dst_ref=buf.at[4 * px + 2 * py + pc], send_sem=send.at[k - 1], recv_sem=recv.at[k - 1],
                device_id=(x, y, c), device_id_type=MESH).wait_recv()
        for cp in sends:
            cp.wait_send()
        acc = buf[0]
        for d in range(1, 8):
            acc = acc + buf[d]
        out_ref[...] = acc

    return pl.pallas_call(
        body, name="allreduce_small",
        in_specs=[VMEM_SPEC], out_specs=VMEM_SPEC, out_shape=jax.ShapeDtypeStruct(pack.shape, F32),
        scratch_shapes=[pltpu.VMEM((8, p, D), F32), *_dma_sems(7, 7)],
    )(pack)


GRID4 = 4


def _sum_cores(core_shard, mine, theirs):
    n = len(mine)

    def body(cs_ref, *refs):
        ms, ts, bfs, owns = refs[:n], refs[n:2 * n], refs[2 * n:3 * n], refs[3 * n:]
        keep = pl.program_id(0) == cs_ref[1]
        for a in range(n):
            acc = ms[a][0, 0] + ts[a][0]
            bfs[a][0] = _bf(acc)

            @pl.when(keep)
            def _():
                owns[a][...] = acc

    shapes = [m.shape[2:] for m in mine]
    in_specs = ([pl.BlockSpec((1, 1) + s, lambda i, cs: (cs[0], i, 0, 0)) for s in shapes]
                + [pl.BlockSpec((1,) + s, lambda i, cs: (i, 0, 0)) for s in shapes])
    out_specs = ([pl.BlockSpec((1,) + s, lambda i, cs: (i, 0, 0)) for s in shapes]
                 + [pl.BlockSpec(s, lambda i, cs: (0, 0)) for s in shapes])
    outs = pl.pallas_call(
        body, name="sum_cores",
        grid_spec=pltpu.PrefetchScalarGridSpec(num_scalar_prefetch=1, grid=(4,), in_specs=in_specs, out_specs=out_specs),
        out_shape=[jax.ShapeDtypeStruct((4,) + s, BF16) for s in shapes] + [jax.ShapeDtypeStruct(s, F32) for s in shapes],
        compiler_params=_cp(48),
    )(core_shard, *mine, *theirs)
    return outs[:n], outs[n:]


def _sum_chips(own, arrived):
    n = len(own)

    def body(*refs):
        os_, ars, outs = refs[:n], refs[n:2 * n], refs[2 * n:]
        for a in range(n):
            outs[a][...] = os_[a][...] + ars[a][0].astype(F32) + ars[a][1].astype(F32)

    blocks = [(o.shape[0] // GRID4, o.shape[1]) for o in own]
    return pl.pallas_call(
        body, name="sum_chips", grid=(GRID4,),
        in_specs=([pl.BlockSpec(b, lambda i: (i, 0)) for b in blocks]
                  + [pl.BlockSpec((2,) + b, lambda i: (0, i, 0)) for b in blocks]),
        out_specs=[pl.BlockSpec(b, lambda i: (i, 0)) for b in blocks],
        out_shape=[jax.ShapeDtypeStruct(o.shape, F32) for o in own],
        compiler_params=_cp(32),
    )(*own, *arrived)


def _adamw_math(w, g, m, v):
    m2 = ADAM_B1 * m + (1.0 - ADAM_B1) * g
    v2 = ADAM_B2 * v + (1.0 - ADAM_B2) * (g * g)
    m_hat = m2 / (1.0 - ADAM_B1 ** ADAM_STEP)
    v_hat = v2 / (1.0 - ADAM_B2 ** ADAM_STEP)
    return -ADAM_LR * (m_hat / (jnp.sqrt(v_hat) + ADAM_EPS) + ADAM_WD * w), m2, v2


def _adamw_big(ws, gs, ms, vs):
    n = len(ws)

    def body(*refs):
        for a in range(n):
            d, m2, v2 = _adamw_math(refs[a][...], refs[n + a][...], refs[2 * n + a][...], refs[3 * n + a][...])
            refs[4 * n + a][...] = d
            refs[5 * n + a][...] = m2
            refs[6 * n + a][...] = v2

    specs = [pl.BlockSpec((w.shape[0] // GRID4, w.shape[1]), lambda i: (i, 0)) for w in ws]
    return pl.pallas_call(
        body, name="adamw_big", grid=(GRID4,),
        in_specs=specs * 4, out_specs=specs * 3,
        out_shape=[jax.ShapeDtypeStruct(w.shape, F32) for w in ws] * 3,
        compiler_params=_cp(48),
    )(*ws, *gs, *ms, *vs)


def _adamw_small(ws, gs, ms, vs):
    n = len(ws)

    def body(*refs):
        for a in range(n):
            d, m2, v2 = _adamw_math(refs[a][...], refs[n + a][...], refs[2 * n + a][...], refs[3 * n + a][...])
            refs[4 * n + a][...] = d
            refs[5 * n + a][...] = m2
            refs[6 * n + a][...] = v2

    return pl.pallas_call(
        body, name="adamw_small",
        in_specs=[VMEM_SPEC] * (4 * n), out_specs=[VMEM_SPEC] * (3 * n),
        out_shape=[jax.ShapeDtypeStruct(w.shape, F32) for w in ws] * 3,
        compiler_params=pltpu.CompilerParams(vmem_limit_bytes=40 << 20),
    )(*ws, *gs, *ms, *vs)


def kernel(x, meta_tokens, norm_mix_w, w_in, w_gate_up, b_gate, gla_norm_w, sinks, w_out, norm_ff_w, w_ff1, w_ff2, final_norm_w, loss_target, m_meta_tokens, m_norm_mix_w, m_w_in, m_w_gate_up, m_b_gate, m_gla_norm_w, m_sinks, m_w_out, m_norm_ff_w, m_w_ff1, m_w_ff2, m_final_norm_w, v_meta_tokens, v_norm_mix_w, v_w_in, v_w_gate_up, v_b_gate, v_gla_norm_w, v_sinks, v_w_out, v_norm_ff_w, v_w_ff1, v_w_ff2, v_final_norm_w):
    xi, yi, ci = _place()
    shard = (2 * xi + yi).astype(jnp.int32).reshape(1)
    core = ci.astype(jnp.int32).reshape(1)

    small = jnp.concatenate([meta_tokens, w_gate_up[0], jnp.zeros((NM, 64), F32)], axis=1)
    wt3, g_small = _run_comm(_gather_shards([_bf(w_in[0].T), small], [True, False]), "gather_w_in")
    meta = g_small[:, :, 0:256].transpose(1, 0, 2).reshape(NM, D)
    wgu = g_small[:, :, 256:320].transpose(1, 0, 2).reshape(NM, 256)

    xs, tgt = x[0], loss_target[0]
    t = xs.shape[0]
    wfin = final_norm_w.reshape(1, D)
    metapad = jnp.concatenate([meta, jnp.zeros((TM - NM, D), F32)], axis=0)
    wgu_p = _bf(jnp.concatenate([wgu, jnp.zeros((128 - 16, 256), F32)], axis=0))
    tabs = _rope_tables(t)

    w1s, w2s = _bf(w_ff1[0]), _bf(w_ff2[0])
    proj, (g_out, w1a) = _proj_fwd(xs, metapad, norm_mix_w, wt3, tabs,
                                   _gather_shards([_bf(w_out[0]), w1s[:HK]], [True] * 2))
    (ogla, oraw, sst, bcum, dgate), (w1b, w2a) = _gla_fwd(proj, wgu_p, b_gate, gla_norm_w, t,
                                                          _gather_shards([w1s[HK:], w2s[:HK]], [True] * 2))
    (oswa, lse), (w2b,) = _swa_fwd(proj, sinks, t, _gather_shards([w2s[HK:]], [True]))
    wo, w1, w2 = g_out.reshape(D, D), (w1a, w1b), (w2a, w2b)
    h1, f, a, dh2, loss, gfin = _mlp_fwd(xs, tgt, ogla, oswa, wo, norm_ff_w, w1, w2, wfin)

    da, dh2b, dh1, do, dwo, gff = _mlp_bwd(h1, a, dh2, ogla, oswa, wo, norm_ff_w, w1, w2)
    dwo = dwo.reshape(4, 2, 128, D).transpose(1, 0, 2, 3)
    dw1, dw2 = _ffn_wgrad(f, a, da, dh2b)
    big = [dwo, dw1, dw2]
    (dgla, dlr, dwgu, dbg, dgnw), theirs = _gla_bwd(proj, oraw, sst, bcum, dgate, do, wgu_p, gla_norm_w, t,
                                                    _swap_halves(big))
    sums_bf, own = _sum_cores(jnp.concatenate([core, shard]), big, theirs)
    swa_grid = (t + TM) // SB // _swa_steps(t + TM)
    (dsq, dsk, dsv, dsink), arrived = _swa_bwd(proj, sinks, lse, do, t, _scatter_shards(sums_bf, swa_grid // 3))
    halves = _sum_chips(own, arrived)
    (gx, gmeta, dwt, gmix), _ = _proj_bwd(xs, metapad, norm_mix_w, wt3, tabs, dgla, dsq, dsk, dsv, dlr, dh1)

    gwt_in, joined = _reduce_w_in(dwt, _join_halves(halves))
    gw_out, gw_1, gw_2 = [j.reshape((-1, j.shape[2])) for j in joined]

    tail = jnp.concatenate([dbg, dgnw, dsink, loss, jnp.zeros((1, D - 256 - 128 - 8 - 1), F32)], axis=1)
    pack = jnp.concatenate([gmeta, gmix, gff, gfin, tail, dwgu[:16].reshape(4, D)], axis=0)
    tot = _allreduce_small(pack)
    g_meta = lax.dynamic_slice_in_dim(tot[0:NM], shard[0] * 256, 256, axis=1)
    g_mix, g_ff, g_fin = tot[16:17], tot[17:18], tot[18]
    g_bg, g_gnw, g_sinks, loss_tot = tot[19:20, 0:256], tot[19:20, 256:384], tot[19:20, 384:392], tot[19, 392]
    g_wgu = lax.dynamic_slice_in_dim(tot[20:24].reshape(NM, 256), shard[0] * 64, 64, axis=1)

    bo = _adamw_big([w_out[0], w_ff1[0], w_ff2[0]], [gw_out, gw_1, gw_2], [m_w_out[0], m_w_ff1[0], m_w_ff2[0]],
                    [v_w_out[0], v_w_ff1[0], v_w_ff2[0]])

    fin2 = lambda a: a.reshape(1, D)
    sw = [meta_tokens, norm_mix_w, w_gate_up[0], b_gate, gla_norm_w, sinks, norm_ff_w, fin2(final_norm_w), w_in[0].T]
    sg = [g_meta, g_mix, g_wgu, g_bg, g_gnw, g_sinks, g_ff, fin2(g_fin), gwt_in]
    sm = [m_meta_tokens, m_norm_mix_w, m_w_gate_up[0], m_b_gate, m_gla_norm_w, m_sinks, m_norm_ff_w, fin2(m_final_norm_w),
          m_w_in[0].T]
    sv = [v_meta_tokens, v_norm_mix_w, v_w_gate_up[0], v_b_gate, v_gla_norm_w, v_sinks, v_norm_ff_w, fin2(v_final_norm_w),
          v_w_in[0].T]
    so = _adamw_small(sw, sg, sm, sv)

    def ordered(small_o, big_o):
        meta_, mix_, wgu_, bg_, gnw_, sinks_, ff_, fin_, wt_ = small_o
        w_out_, w_1_, w_2_ = big_o
        return (meta_, mix_, wt_.T[None], wgu_[None], bg_, gnw_, sinks_, w_out_[None], ff_, w_1_[None], w_2_[None],
                fin_.reshape(D))

    grads = ordered(sg, [gw_out, gw_1, gw_2])
    deltas = ordered(so[0:9], bo[0:3])
    new_m = ordered(so[9:18], bo[3:6])
    new_v = ordered(so[18:27], bo[6:9])
    return (loss_tot, gx[None], *grads, *deltas, *new_m, *new_v)
```
